```python
import math
import jax, jax.numpy as jnp
from jax import lax
import numpy as np

D_MODEL = 1024
BATCH = 4
SEQ = 4096
DEPTH = 2
DEC_BATCH = 32
DEC_SEQ = 1
PAST_LEN = 16384
PAGE_SIZE = 128

N_A_LAYERS = DEPTH // 2
N_B_LAYERS = DEPTH - N_A_LAYERS
N_DENSE = (DEPTH + 1) // 2
N_MOE = DEPTH // 2

RWKV_HEAD = 64
RWKV_HEADS = D_MODEL // RWKV_HEAD
DECAY_LORA = 64
AAA_LORA = 64
GATE_LORA = 128
DECAY_SCALE = math.exp(-0.5)
GN_EPS = RWKV_HEAD * 1e-5

NSA_HEADS = 16
NSA_HEAD_DIM = 64
NSA_KV_GROUPS = 2
NSA_GROUP_HEADS = NSA_HEADS // NSA_KV_GROUPS
N_BRANCH = 3
CMP_BLOCK = 32
CMP_STRIDE = 16
CMP_HIDDEN = 128
SLC_BLOCK = 64
SLC_TOPN = 16
WINDOW = 512
Q_BLOCK = 128
FORCED_SCORE = 1e4

ROPE_THETA = 500000.0
ROT_DIM = NSA_HEAD_DIM // 4

D_FF = 2816
N_EXPERTS = 8
TOP_K = 2
D_FF_EXPERT = 2816
MOE_BLOCK = 256
NORM_EPS = 1e-6

kernel_name = 'yoco_rwkv7_nsa_moe_step'


def rmsnorm(x, g):
    xf = x.astype(jnp.float32)
    y = xf * lax.rsqrt(jnp.mean(xf * xf, axis=-1, keepdims=True) + NORM_EPS)
    return (y * g.astype(jnp.float32)).astype(x.dtype)


def rope_partial(x, pos):
    half = ROT_DIM // 2
    inv = ROPE_THETA ** (-2.0 * jnp.arange(half, dtype=jnp.float32) / ROT_DIM)
    ang = pos.astype(jnp.float32)[:, None] * inv[None, :]
    shape = (1, pos.shape[0]) + (1,) * (x.ndim - 3) + (half,)
    cos = jnp.cos(ang).reshape(shape)
    sin = jnp.sin(ang).reshape(shape)
    xf = x.astype(jnp.float32)
    x1, x2 = xf[..., :half], xf[..., half:ROT_DIM]
    out = jnp.concatenate([x1 * cos - x2 * sin, x2 * cos + x1 * sin, xf[..., ROT_DIM:]], axis=-1)
    return out.astype(x.dtype)


def masked_softmax(s, mask):
    s = jnp.where(mask, s.astype(jnp.float32), -jnp.inf)
    m = jnp.max(s, axis=-1, keepdims=True)
    m = jnp.where(jnp.isfinite(m), m, 0.0)
    e = jnp.where(mask, jnp.exp(s - m), 0.0)
    return e / jnp.maximum(jnp.sum(e, axis=-1, keepdims=True), 1e-30)


def swiglu(h, w_gu, w_down):
    g, u = jnp.split(h @ w_gu, 2, axis=-1)
    return (jax.nn.silu(g) * u) @ w_down


def moe_swiglu(h, w_router, w_gu_e, w_down_e):
    shp = h.shape
    xt = h.reshape(-1, shp[-1])
    n_tok = xt.shape[0]
    logits = (xt @ w_router).astype(jnp.float32)
    top_val, top_idx = lax.top_k(logits, TOP_K)
    gate = jax.nn.softmax(top_val, axis=-1).astype(h.dtype)
    nk = n_tok * TOP_K
    flat_e = top_idx.reshape(nk)
    flat_tok = jnp.repeat(jnp.arange(n_tok, dtype=jnp.int32), TOP_K)
    flat_gate = gate.reshape(nk)
    order = jnp.argsort(flat_e)
    sorted_e = flat_e[order]
    counts = jnp.bincount(flat_e, length=N_EXPERTS)
    padded = (counts + MOE_BLOCK - 1) // MOE_BLOCK * MOE_BLOCK
    ends_pad = jnp.cumsum(padded)
    starts_pad = ends_pad - padded
    starts_raw = jnp.cumsum(counts) - counts
    dest = starts_pad[sorted_e] + jnp.arange(nk) - starts_raw[sorted_e]
    n_rows = (nk + MOE_BLOCK - 1) // MOE_BLOCK * MOE_BLOCK + N_EXPERTS * MOE_BLOCK
    n_blk = n_rows // MOE_BLOCK
    row_tok = jnp.full((n_rows,), n_tok, jnp.int32).at[dest].set(flat_tok[order])
    row_gate = jnp.zeros((n_rows,), h.dtype).at[dest].set(flat_gate[order])
    blk_e = jnp.minimum(jnp.searchsorted(ends_pad, jnp.arange(n_blk) * MOE_BLOCK, side='right'), N_EXPERTS - 1)
    x_pad = jnp.concatenate([xt, jnp.zeros((1, shp[-1]), xt.dtype)], axis=0)
    xb = x_pad[row_tok].reshape(n_blk, MOE_BLOCK, shp[-1])

    def expert_block(args):
        xblk, e = args
        return swiglu(xblk, w_gu_e[e], w_down_e[e])

    yb = lax.map(expert_block, (xb, blk_e))
    y_rows = yb.reshape(n_rows, shp[-1]) * row_gate[:, None]
    out = jnp.zeros((n_tok + 1, shp[-1]), h.dtype).at[row_tok].add(y_rows)[:n_tok]
    return out.reshape(shp)


def wkv7_scan(r, decay, k, v, a_vec, b_vec, s0):
    def step(S, inp):
        r_t, w_t, k_t, v_t, a_t, b_t = inp
        sa = jnp.einsum('nhvk,nhk->nhv', S, a_t)
        S = S * w_t[:, :, None, :] + sa[..., None] * b_t[:, :, None, :] + v_t[..., None] * k_t[:, :, None, :]
        return S, jnp.einsum('nhvk,nhk->nhv', S, r_t)

    xs = tuple(jnp.moveaxis(t, 1, 0) for t in (r, decay, k, v, a_vec, b_vec))
    s_fin, ys = lax.scan(step, s0, xs)
    return jnp.moveaxis(ys, 0, 1), s_fin


def rwkv7_time_mix(h, h_prev, s0, mu, w_rkv, w0, w1, w2, a0, a1, a2, g1, g2, k_k, k_a, r_k, gn_w, gn_b, w_o):
    n, T, D = h.shape
    f32 = jnp.float32
    dx = jnp.concatenate([h_prev[:, None, :].astype(h.dtype), h[:, :-1]], axis=1) - h
    xr, xw, xk, xv, xa, xg = (h + dx * mu[i] for i in range(6))
    r = xr @ w_rkv[0]
    k = xk @ w_rkv[1]
    v = xv @ w_rkv[2]
    decay = jnp.exp(-DECAY_SCALE * jax.nn.sigmoid((w0 + jnp.tanh(xw @ w1) @ w2).astype(f32)))
    a = jax.nn.sigmoid((a0 + (xa @ a1) @ a2).astype(f32))
    g = jax.nn.sigmoid(xg @ g1) @ g2
    heads = lambda t: t.astype(f32).reshape(n, T, RWKV_HEADS, RWKV_HEAD)
    kk = heads(k * k_k)
    kk = kk / jnp.maximum(jnp.sqrt(jnp.sum(kk * kk, axis=-1, keepdims=True)), 1e-12)
    a_h = heads(a)
    k_h = heads(k.astype(f32) * (1.0 + (a - 1.0) * k_a.astype(f32)))
    r_h, v_h = heads(r), heads(v)
    y, s_fin = wkv7_scan(r_h, heads(decay), k_h, v_h, -kk, kk * a_h, s0.astype(f32))
    mean = jnp.mean(y, axis=-1, keepdims=True)
    var = jnp.mean(jnp.square(y - mean), axis=-1, keepdims=True)
    y = ((y - mean) * lax.rsqrt(var + GN_EPS)).reshape(n, T, D) * gn_w.astype(f32) + gn_b.astype(f32)
    bonus = jnp.sum(r_h * k_h * r_k.astype(f32), axis=-1, keepdims=True) * v_h
    y = (y + bonus.reshape(n, T, D)).astype(h.dtype)
    return (y * g) @ w_o, s_fin, h[:, -1]


def shared_kv_rows(x, norm_kv, w_kv, pos):
    n, T, _ = x.shape
    kv = (rmsnorm(x, norm_kv) @ w_kv).reshape(n, T, N_BRANCH, 2, NSA_KV_GROUPS, NSA_HEAD_DIM)
    cmp_kv = kv[:, :, 0]
    slc_kv = jnp.stack([rope_partial(kv[:, :, 1, 0], pos), kv[:, :, 1, 1]], axis=2)
    win_kv = jnp.stack([rope_partial(kv[:, :, 2, 0], pos), kv[:, :, 2, 1]], axis=2)
    return cmp_kv, slc_kv, win_kv


def compress_rows(rows, pos_emb, w1, b1, w2):
    n, T, G, dh = rows.shape
    c = rows.reshape(n, T // CMP_STRIDE, CMP_STRIDE, G, dh)
    w1r = w1.reshape(CMP_BLOCK, dh, CMP_HIDDEN)
    first = jnp.einsum('ncjgd,jdh->ncgh', c, w1r[:CMP_STRIDE])
    second = jnp.einsum('ncjgd,jdh->ncgh', c, w1r[CMP_STRIDE:])
    second = jnp.concatenate([second[:, 1:], jnp.zeros_like(second[:, :1])], axis=1)
    hid = first + second + jnp.einsum('jd,jdh->h', pos_emb, w1r) + b1
    return jax.nn.gelu(hid) @ w2


def nsa_query(h, w_in):
    n, T, _ = h.shape
    proj = h @ w_in
    q = proj[..., :NSA_HEADS * NSA_HEAD_DIM].reshape(n, T, NSA_KV_GROUPS, NSA_GROUP_HEADS, NSA_HEAD_DIM)
    gate = proj[..., NSA_HEADS * NSA_HEAD_DIM:].reshape(n, T, NSA_KV_GROUPS, NSA_GROUP_HEADS, N_BRANCH)
    return q, gate


def nsa_branches(q, gate, t_pos, kc, vc, gather_sel, k_w, v_w, s_pos):
    scale = NSA_HEAD_DIM ** -0.5
    q_rot = rope_partial(q, t_pos)
    n_c = kc.shape[1]
    vis = (jnp.arange(n_c) * CMP_STRIDE + CMP_BLOCK - 1)[None, :] <= t_pos[:, None]
    p_cmp = masked_softmax(jnp.einsum('nqghd,ncgd->nghqc', q, kc) * scale, vis)
    o_cmp = jnp.einsum('nghqc,ncgd->nqghd', p_cmp.astype(vc.dtype), vc)
    ratio = SLC_BLOCK // CMP_STRIDE
    lead = CMP_BLOCK // CMP_STRIDE - 1
    n_s = n_c // ratio
    pg = jnp.pad(jnp.sum(p_cmp, axis=2), ((0, 0), (0, 0), (0, 0), (lead, 0)))
    p_slc = pg[..., 0:ratio * n_s:ratio]
    for o in range(1, ratio + lead):
        p_slc = p_slc + pg[..., o:o + ratio * n_s:ratio]
    jb = jnp.arange(n_s)[None, :]
    jt = (t_pos // SLC_BLOCK)[:, None]
    forced = (jb == 0) | (jb == jt) | (jb == jt - 1)
    score = jnp.where(jb > jt, -jnp.inf, jnp.where(forced, FORCED_SCORE, p_slc))
    _, idx = lax.top_k(score, min(SLC_TOPN, n_s))
    k_sel, v_sel = gather_sel(idx)
    tok = idx[..., None] * SLC_BLOCK + jnp.arange(SLC_BLOCK)
    m_sel = tok <= t_pos[None, None, :, None, None]
    s = jnp.einsum('nqghd,ngqksd->nghqks', q_rot, k_sel) * scale
    shp = s.shape
    p = masked_softmax(s.reshape(shp[:4] + (-1,)), m_sel.reshape(m_sel.shape[:3] + (-1,))[:, :, None])
    o_slc = jnp.einsum('nghqks,ngqksd->nqghd', p.reshape(shp).astype(v_sel.dtype), v_sel)
    m_w = (s_pos[None, :] <= t_pos[:, None]) & (s_pos[None, :] >= t_pos[:, None] - WINDOW) & (s_pos[None, :] >= 0)
    p_w = masked_softmax(jnp.einsum('nqghd,nkgd->nghqk', q_rot, k_w) * scale, m_w)
    o_win = jnp.einsum('nghqk,nkgd->nqghd', p_w.astype(v_w.dtype), v_w)
    g = jax.nn.sigmoid(gate.astype(jnp.float32)).astype(q.dtype)
    return g[..., 0:1] * o_cmp + g[..., 1:2] * o_slc + g[..., 2:3] * o_win


def prompt_kv_context(cmp_kv, slc_kv, win_kv, cmp_pos, cmp_w1, cmp_b1, cmp_w2):
    n, T = cmp_kv.shape[:2]
    kc = compress_rows(cmp_kv[:, :, 0], cmp_pos[0], cmp_w1[0], cmp_b1[0], cmp_w2[0])
    vc = compress_rows(cmp_kv[:, :, 1], cmp_pos[1], cmp_w1[1], cmp_b1[1], cmp_w2[1])
    n_s = T // SLC_BLOCK
    to_blocks = lambda a: a.reshape(n, n_s, SLC_BLOCK, NSA_KV_GROUPS, NSA_HEAD_DIM).transpose(0, 3, 1, 2, 4)
    win_pad = jnp.pad(win_kv, ((0, 0), (WINDOW, 0), (0, 0), (0, 0), (0, 0)))
    return kc, vc, to_blocks(slc_kv[:, :, 0]), to_blocks(slc_kv[:, :, 1]), win_pad


def prompt_nsa(h, ctx, w_in, w_o):
    kc, vc, sel_k, sel_v, win_pad = ctx
    n, T, _ = h.shape
    q, gate = nsa_query(h, w_in)
    n_qb = T // Q_BLOCK
    blocks = lambda a: a.reshape((n, n_qb, Q_BLOCK) + a.shape[2:]).swapaxes(0, 1)
    bi = jnp.arange(n)[:, None, None, None]
    gi = jnp.arange(NSA_KV_GROUPS)[None, :, None, None]
    gather_sel = lambda idx: (sel_k[bi, gi, idx], sel_v[bi, gi, idx])

    def one_block(args):
        qi, q_blk, g_blk = args
        t0 = qi * Q_BLOCK
        t_pos = t0 + jnp.arange(Q_BLOCK)
        band = lax.dynamic_slice_in_dim(win_pad, t0, WINDOW + Q_BLOCK, axis=1)
        s_pos = t0 - WINDOW + jnp.arange(WINDOW + Q_BLOCK)
        return nsa_branches(q_blk, g_blk, t_pos, kc, vc, gather_sel, band[:, :, 0], band[:, :, 1], s_pos)

    o = lax.map(one_block, (jnp.arange(n_qb), blocks(q), blocks(gate)))
    return o.swapaxes(0, 1).reshape(n, T, NSA_HEADS * NSA_HEAD_DIM) @ w_o


def sample_kv_context(cmp_new, slc_new, win_new, cache_cmp_kv, cache_slc_kv, cache_win_kv, page_table,
                      cmp_pos, cmp_w1, cmp_b1, cmp_w2):
    n, S = cmp_new.shape[:2]
    n_new_blk = -(-S // SLC_BLOCK)
    pad = ((0, 0), (0, n_new_blk * SLC_BLOCK - S), (0, 0), (0, 0), (0, 0))
    past = cache_cmp_kv[page_table].reshape((n, PAST_LEN) + cache_cmp_kv.shape[2:])
    rows = jnp.concatenate([past, jnp.pad(cmp_new.astype(past.dtype), pad)], axis=1)
    kc = compress_rows(rows[:, :, 0], cmp_pos[0], cmp_w1[0], cmp_b1[0], cmp_w2[0])
    vc = compress_rows(rows[:, :, 1], cmp_pos[1], cmp_w1[1], cmp_b1[1], cmp_w2[1])
    pool = cache_slc_kv.reshape((-1, PAGE_SIZE // SLC_BLOCK, SLC_BLOCK) + cache_slc_kv.shape[2:])
    new_blocks = jnp.pad(slc_new, pad).reshape((n, n_new_blk, SLC_BLOCK) + slc_new.shape[2:])
    win_all = jnp.concatenate([cache_win_kv, win_new.astype(cache_win_kv.dtype)], axis=1)
    return kc, vc, pool, new_blocks, win_all


def sample_nsa(h, ctx, page_table, w_in, w_o):
    kc, vc, pool, new_blocks, win_all = ctx
    n, S, _ = h.shape
    q, gate = nsa_query(h, w_in)
    t_pos = PAST_LEN + jnp.arange(S)
    sub = PAGE_SIZE // SLC_BLOCK
    n_past_blk = PAST_LEN // SLC_BLOCK
    n_new_blk = new_blocks.shape[1]
    bi = jnp.arange(n)[:, None, None, None]
    gi = jnp.arange(NSA_KV_GROUPS)[None, :, None, None]

    def gather_sel(idx):
        jp = jnp.minimum(idx, n_past_blk - 1)
        phys = page_table[bi, jp // sub]
        from_past = pool[phys, jp % sub, :, :, gi]
        jn = jnp.clip(idx - n_past_blk, 0, n_new_blk - 1)
        from_new = new_blocks[bi, jn, :, :, gi].astype(from_past.dtype)
        blk = jnp.where((idx >= n_past_blk)[..., None, None, None], from_new, from_past)
        return blk[..., 0, :], blk[..., 1, :]

    wb = win_all.shape[1] - S
    s_pos = PAST_LEN - wb + jnp.arange(win_all.shape[1])
    o = nsa_branches(q, gate, t_pos, kc, vc, gather_sel, win_all[:, :, 0], win_all[:, :, 1], s_pos)
    return o.reshape(n, S, NSA_HEADS * NSA_HEAD_DIM) @ w_o


def setup_inputs(seed: int = 0) -> dict:
    key = jax.random.key(seed)
    ks = iter(jax.random.split(key, 64))
    nrm = lambda shape, scale: jax.random.normal(next(ks), shape, jnp.float32) * scale
    D = D_MODEL
    G, DH = NSA_KV_GROUPS, NSA_HEAD_DIM
    n_pages = PAST_LEN // PAGE_SIZE
    n_pool = (DEC_BATCH * n_pages * 5) // 4
    wb = min(WINDOW, PAST_LEN)
    inp = {}
    inp['x_prompt'] = nrm((BATCH, SEQ, D), 1.0)
    inp['x_sample'] = nrm((DEC_BATCH, DEC_SEQ, D), 1.0)
    inp['state_wkv'] = nrm((N_A_LAYERS, DEC_BATCH, RWKV_HEADS, RWKV_HEAD, RWKV_HEAD), 0.2)
    inp['state_shift'] = nrm((N_A_LAYERS, DEC_BATCH, D), 1.0)
    inp['cache_cmp_kv'] = nrm((n_pool, PAGE_SIZE, 2, G, DH), 1.0)
    inp['cache_slc_kv'] = nrm((n_pool, PAGE_SIZE, 2, G, DH), 1.0)
    inp['cache_win_kv'] = nrm((DEC_BATCH, wb, 2, G, DH), 1.0)
    inp['page_table'] = jax.random.permutation(next(ks), n_pool)[:DEC_BATCH * n_pages].reshape(DEC_BATCH, n_pages).astype(jnp.int32)
    inp['norm_mix'] = 1.0 + nrm((DEPTH, D), 0.05)
    inp['norm_ffn'] = 1.0 + nrm((DEPTH, D), 0.05)
    inp['norm_kv'] = 1.0 + nrm((D,), 0.05)
    inp['norm_final'] = 1.0 + nrm((D,), 0.05)
    inp['rw_mu'] = jax.random.uniform(next(ks), (N_A_LAYERS, 6, D), jnp.float32)
    inp['rw_w_rkv'] = nrm((N_A_LAYERS, 3, D, D), D ** -0.5)
    inp['rw_w0'] = nrm((N_A_LAYERS, D), 1.0) - 1.0
    inp['rw_w1'] = nrm((N_A_LAYERS, D, DECAY_LORA), D ** -0.5)
    inp['rw_w2'] = nrm((N_A_LAYERS, DECAY_LORA, D), 0.5 * DECAY_LORA ** -0.5)
    inp['rw_a0'] = nrm((N_A_LAYERS, D), 0.1)
    inp['rw_a1'] = nrm((N_A_LAYERS, D, AAA_LORA), D ** -0.5)
    inp['rw_a2'] = nrm((N_A_LAYERS, AAA_LORA, D), 0.5 * AAA_LORA ** -0.5)
    inp['rw_g1'] = nrm((N_A_LAYERS, D, GATE_LORA), D ** -0.5)
    inp['rw_g2'] = nrm((N_A_LAYERS, GATE_LORA, D), GATE_LORA ** -0.5)
    inp['rw_k_k'] = 1.0 + nrm((N_A_LAYERS, D), 0.1)
    inp['rw_k_a'] = 1.0 + nrm((N_A_LAYERS, D), 0.1)
    inp['rw_r_k'] = nrm((N_A_LAYERS, RWKV_HEADS, RWKV_HEAD), 0.1)
    inp['rw_gn_w'] = 1.0 + nrm((N_A_LAYERS, D), 0.05)
    inp['rw_gn_b'] = nrm((N_A_LAYERS, D), 0.01)
    inp['rw_w_o'] = nrm((N_A_LAYERS, D, D), D ** -0.5)
    inp['nsa_w_kv'] = nrm((D, N_BRANCH * 2 * G * DH), D ** -0.5)
    inp['nsa_cmp_pos'] = nrm((2, CMP_BLOCK, DH), 0.1)
    inp['nsa_cmp_w1'] = nrm((2, CMP_BLOCK * DH, CMP_HIDDEN), (CMP_BLOCK * DH) ** -0.5)
    inp['nsa_cmp_b1'] = nrm((2, CMP_HIDDEN), 0.01)
    inp['nsa_cmp_w2'] = nrm((2, CMP_HIDDEN, DH), CMP_HIDDEN ** -0.5)
    inp['nsa_w_in'] = nrm((N_B_LAYERS, D, NSA_HEADS * DH + N_BRANCH * NSA_HEADS), D ** -0.5)
    inp['nsa_w_o'] = nrm((N_B_LAYERS, NSA_HEADS * DH, D), (NSA_HEADS * DH) ** -0.5)
    inp['ffn_w_gu'] = nrm((N_DENSE, D, 2 * D_FF), D ** -0.5)
    inp['ffn_w_down'] = nrm((N_DENSE, D_FF, D), D_FF ** -0.5)
    inp['moe_router'] = nrm((N_MOE, D, N_EXPERTS), D ** -0.5)
    inp['moe_w_gu'] = nrm((N_MOE, N_EXPERTS, D, 2 * D_FF_EXPERT), D ** -0.5)
    inp['moe_w_down'] = nrm((N_MOE, N_EXPERTS, D_FF_EXPERT, D), D_FF_EXPERT ** -0.5)
    return inp


def reference(x_prompt, x_sample, state_wkv, state_shift, cache_cmp_kv, cache_slc_kv, cache_win_kv, page_table,
              norm_mix, norm_ffn, norm_kv, norm_final,
              rw_mu, rw_w_rkv, rw_w0, rw_w1, rw_w2, rw_a0, rw_a1, rw_a2, rw_g1, rw_g2,
              rw_k_k, rw_k_a, rw_r_k, rw_gn_w, rw_gn_b, rw_w_o,
              nsa_w_kv, nsa_cmp_pos, nsa_cmp_w1, nsa_cmp_b1, nsa_cmp_w2, nsa_w_in, nsa_w_o,
              ffn_w_gu, ffn_w_down, moe_router, moe_w_gu, moe_w_down):
    cmp_params = (nsa_cmp_pos, nsa_cmp_w1, nsa_cmp_b1, nsa_cmp_w2)

    def trunk(x, pos, shift0, wkv0, make_context, attend):
        wkv_out, shift_out = [], []
        rows, ctx = None, None
        for li in range(DEPTH):
            h = rmsnorm(x, norm_mix[li])
            if li < N_A_LAYERS:
                y, s_fin, h_last = rwkv7_time_mix(
                    h, shift0[li], wkv0[li], rw_mu[li], rw_w_rkv[li], rw_w0[li], rw_w1[li], rw_w2[li],
                    rw_a0[li], rw_a1[li], rw_a2[li], rw_g1[li], rw_g2[li], rw_k_k[li], rw_k_a[li],
                    rw_r_k[li], rw_gn_w[li], rw_gn_b[li], rw_w_o[li])
                wkv_out.append(s_fin)
                shift_out.append(h_last)
            else:
                b = li - N_A_LAYERS
                y = attend(h, ctx, nsa_w_in[b], nsa_w_o[b])
            x = x + y
            h = rmsnorm(x, norm_ffn[li])
            if li % 2 == 0:
                x = x + swiglu(h, ffn_w_gu[li // 2], ffn_w_down[li // 2])
            else:
                x = x + moe_swiglu(h, moe_router[li // 2], moe_w_gu[li // 2], moe_w_down[li // 2])
            if li == N_A_LAYERS - 1:
                rows = shared_kv_rows(x, norm_kv, nsa_w_kv, pos)
                ctx = make_context(*rows)
        return rmsnorm(x, norm_final), jnp.stack(wkv_out), jnp.stack(shift_out), rows, ctx

    n_p = x_prompt.shape[0]
    pos_p = jnp.arange(x_prompt.shape[1], dtype=jnp.int32)
    shift0_p = jnp.zeros((N_A_LAYERS, n_p, D_MODEL), x_prompt.dtype)
    wkv0_p = jnp.zeros((N_A_LAYERS, n_p, RWKV_HEADS, RWKV_HEAD, RWKV_HEAD), jnp.float32)
    y_prompt, wkv_p, shift_p, rows_p, _ = trunk(
        x_prompt, pos_p, shift0_p, wkv0_p,
        lambda c, s, w: prompt_kv_context(c, s, w, *cmp_params),
        prompt_nsa)
    cmp_kv_p, slc_kv_p, win_rows_p = rows_p
    win_kv_p = win_rows_p[:, -min(WINDOW, x_prompt.shape[1]):]

    pos_s = PAST_LEN + jnp.arange(x_sample.shape[1], dtype=jnp.int32)
    y_sample, wkv_s, shift_s, rows_s, ctx_s = trunk(
        x_sample, pos_s, state_shift, state_wkv,
        lambda c, s, w: sample_kv_context(c, s, w, cache_cmp_kv, cache_slc_kv, cache_win_kv, page_table, *cmp_params),
        lambda h, ctx, wi, wo: sample_nsa(h, ctx, page_table, wi, wo))
    cmp_kv_s, slc_kv_s, _ = rows_s
    win_kv_s = ctx_s[4][:, -cache_win_kv.shape[1]:]

    return (y_prompt, y_sample, wkv_p, shift_p, cmp_kv_p, slc_kv_p, win_kv_p,
            wkv_s, shift_s, cmp_kv_s, slc_kv_s, win_kv_s)
```

```python
import functools
import math

import jax
import jax.numpy as jnp
from jax import lax
from jax.experimental import pallas as pl
from jax.experimental.pallas import tpu as pltpu

D_MODEL = 1024
DEPTH = 2
PAST_LEN = 16384
PAGE_SIZE = 128
N_A_LAYERS = DEPTH // 2
RWKV_HEAD = 64
RWKV_HEADS = D_MODEL // RWKV_HEAD
DECAY_SCALE = math.exp(-0.5)
GN_EPS = RWKV_HEAD * 1e-5
NSA_HEADS = 16
NSA_HEAD_DIM = 64
NSA_KV_GROUPS = 2
NSA_GROUP_HEADS = NSA_HEADS // NSA_KV_GROUPS
N_BRANCH = 3
CMP_BLOCK = 32
CMP_STRIDE = 16
CMP_HIDDEN = 128
SLC_BLOCK = 64
SLC_TOPN = 16
WINDOW = 512
Q_BLOCK = 128
FORCED_SCORE = 1e4
ROPE_THETA = 500000.0
ROT_DIM = NSA_HEAD_DIM // 4
N_EXPERTS = 8
TOP_K = 2
MOE_BLOCK = 256
NORM_EPS = 1e-6


def _rmsnorm_kernel(x_ref, g_ref, o_ref):
    x = x_ref[...]
    y = x * lax.rsqrt(jnp.mean(x * x, axis=-1, keepdims=True) + NORM_EPS)
    o_ref[...] = y * g_ref[...]


def rmsnorm(x, g):
    shp = x.shape
    x2 = x.reshape(-1, shp[-1])
    rows = x2.shape[0]
    tm = min(rows, 512)
    out = pl.pallas_call(
        _rmsnorm_kernel,
        grid=(rows // tm,),
        in_specs=[pl.BlockSpec((tm, shp[-1]), lambda i: (i, 0)),
                  pl.BlockSpec((1, shp[-1]), lambda i: (0, 0))],
        out_specs=pl.BlockSpec((tm, shp[-1]), lambda i: (i, 0)),
        out_shape=jax.ShapeDtypeStruct(x2.shape, x.dtype),
        name="rmsnorm",
    )(x2, g.reshape(1, -1))
    return out.reshape(shp)


def rope_partial(x, pos):
    half = ROT_DIM // 2
    inv = ROPE_THETA ** (-2.0 * jnp.arange(half, dtype=jnp.float32) / ROT_DIM)
    ang = pos.astype(jnp.float32)[:, None] * inv[None, :]
    shape = (1, pos.shape[0]) + (1,) * (x.ndim - 3) + (half,)
    cos = jnp.cos(ang).reshape(shape)
    sin = jnp.sin(ang).reshape(shape)
    xf = x.astype(jnp.float32)
    x1, x2 = xf[..., :half], xf[..., half:ROT_DIM]
    out = jnp.concatenate([x1 * cos - x2 * sin, x2 * cos + x1 * sin, xf[..., ROT_DIM:]], axis=-1)
    return out.astype(x.dtype)


def masked_softmax(s, mask):
    s = jnp.where(mask, s.astype(jnp.float32), -jnp.inf)
    m = jnp.max(s, axis=-1, keepdims=True)
    m = jnp.where(jnp.isfinite(m), m, 0.0)
    e = jnp.where(mask, jnp.exp(s - m), 0.0)
    return e / jnp.maximum(jnp.sum(e, axis=-1, keepdims=True), 1e-30)


def swiglu(h, w_gu, w_down):
    g, u = jnp.split(h @ w_gu, 2, axis=-1)
    return (jax.nn.silu(g) * u) @ w_down


def moe_swiglu(h, w_router, w_gu_e, w_down_e):
    shp = h.shape
    xt = h.reshape(-1, shp[-1])
    n_tok = xt.shape[0]
    logits = (xt @ w_router).astype(jnp.float32)
    top_val, top_idx = lax.top_k(logits, TOP_K)
    gate = jax.nn.softmax(top_val, axis=-1).astype(h.dtype)
    nk = n_tok * TOP_K
    flat_e = top_idx.reshape(nk)
    flat_tok = jnp.repeat(jnp.arange(n_tok, dtype=jnp.int32), TOP_K)
    flat_gate = gate.reshape(nk)
    order = jnp.argsort(flat_e)
    sorted_e = flat_e[order]
    counts = jnp.bincount(flat_e, length=N_EXPERTS)
    padded = (counts + MOE_BLOCK - 1) // MOE_BLOCK * MOE_BLOCK
    ends_pad = jnp.cumsum(padded)
    starts_pad = ends_pad - padded
    starts_raw = jnp.cumsum(counts) - counts
    dest = starts_pad[sorted_e] + jnp.arange(nk) - starts_raw[sorted_e]
    n_rows = (nk + MOE_BLOCK - 1) // MOE_BLOCK * MOE_BLOCK + N_EXPERTS * MOE_BLOCK
    n_blk = n_rows // MOE_BLOCK
    row_tok = jnp.full((n_rows,), n_tok, jnp.int32).at[dest].set(flat_tok[order])
    row_gate = jnp.zeros((n_rows,), h.dtype).at[dest].set(flat_gate[order])
    blk_e = jnp.minimum(jnp.searchsorted(ends_pad, jnp.arange(n_blk) * MOE_BLOCK, side='right'), N_EXPERTS - 1)
    x_pad = jnp.concatenate([xt, jnp.zeros((1, shp[-1]), xt.dtype)], axis=0)
    xb = x_pad[row_tok].reshape(n_blk, MOE_BLOCK, shp[-1])

    def expert_block(args):
        xblk, e = args
        return swiglu(xblk, w_gu_e[e], w_down_e[e])

    yb = lax.map(expert_block, (xb, blk_e))
    y_rows = yb.reshape(n_rows, shp[-1]) * row_gate[:, None]
    out = jnp.zeros((n_tok + 1, shp[-1]), h.dtype).at[row_tok].add(y_rows)[:n_tok]
    return out.reshape(shp)


def wkv7_scan(r, decay, k, v, a_vec, b_vec, s0):
    def step(S, inp):
        r_t, w_t, k_t, v_t, a_t, b_t = inp
        sa = jnp.einsum('nhvk,nhk->nhv', S, a_t)
        S = S * w_t[:, :, None, :] + sa[..., None] * b_t[:, :, None, :] + v_t[..., None] * k_t[:, :, None, :]
        return S, jnp.einsum('nhvk,nhk->nhv', S, r_t)

    xs = tuple(jnp.moveaxis(t, 1, 0) for t in (r, decay, k, v, a_vec, b_vec))
    s_fin, ys = lax.scan(step, s0, xs)
    return jnp.moveaxis(ys, 0, 1), s_fin


def rwkv7_time_mix(h, h_prev, s0, mu, w_rkv, w0, w1, w2, a0, a1, a2, g1, g2, k_k, k_a, r_k, gn_w, gn_b, w_o):
    n, T, D = h.shape
    f32 = jnp.float32
    dx = jnp.concatenate([h_prev[:, None, :].astype(h.dtype), h[:, :-1]], axis=1) - h
    xr, xw, xk, xv, xa, xg = (h + dx * mu[i] for i in range(6))
    r = xr @ w_rkv[0]
    k = xk @ w_rkv[1]
    v = xv @ w_rkv[2]
    decay = jnp.exp(-DECAY_SCALE * jax.nn.sigmoid((w0 + jnp.tanh(xw @ w1) @ w2).astype(f32)))
    a = jax.nn.sigmoid((a0 + (xa @ a1) @ a2).astype(f32))
    g = jax.nn.sigmoid(xg @ g1) @ g2
    heads = lambda t: t.astype(f32).reshape(n, T, RWKV_HEADS, RWKV_HEAD)
    kk = heads(k * k_k)
    kk = kk / jnp.maximum(jnp.sqrt(jnp.sum(kk * kk, axis=-1, keepdims=True)), 1e-12)
    a_h = heads(a)
    k_h = heads(k.astype(f32) * (1.0 + (a - 1.0) * k_a.astype(f32)))
    r_h, v_h = heads(r), heads(v)
    y, s_fin = wkv7_scan(r_h, heads(decay), k_h, v_h, -kk, kk * a_h, s0.astype(f32))
    mean = jnp.mean(y, axis=-1, keepdims=True)
    var = jnp.mean(jnp.square(y - mean), axis=-1, keepdims=True)
    y = ((y - mean) * lax.rsqrt(var + GN_EPS)).reshape(n, T, D) * gn_w.astype(f32) + gn_b.astype(f32)
    bonus = jnp.sum(r_h * k_h * r_k.astype(f32), axis=-1, keepdims=True) * v_h
    y = (y + bonus.reshape(n, T, D)).astype(h.dtype)
    return (y * g) @ w_o, s_fin, h[:, -1]


def shared_kv_rows(x, norm_kv, w_kv, pos):
    n, T, _ = x.shape
    kv = (rmsnorm(x, norm_kv) @ w_kv).reshape(n, T, N_BRANCH, 2, NSA_KV_GROUPS, NSA_HEAD_DIM)
    cmp_kv = kv[:, :, 0]
    slc_kv = jnp.stack([rope_partial(kv[:, :, 1, 0], pos), kv[:, :, 1, 1]], axis=2)
    win_kv = jnp.stack([rope_partial(kv[:, :, 2, 0], pos), kv[:, :, 2, 1]], axis=2)
    return cmp_kv, slc_kv, win_kv


def compress_rows(rows, pos_emb, w1, b1, w2):
    n, T, G, dh = rows.shape
    c = rows.reshape(n, T // CMP_STRIDE, CMP_STRIDE, G, dh)
    w1r = w1.reshape(CMP_BLOCK, dh, CMP_HIDDEN)
    first = jnp.einsum('ncjgd,jdh->ncgh', c, w1r[:CMP_STRIDE])
    second = jnp.einsum('ncjgd,jdh->ncgh', c, w1r[CMP_STRIDE:])
    second = jnp.concatenate([second[:, 1:], jnp.zeros_like(second[:, :1])], axis=1)
    hid = first + second + jnp.einsum('jd,jdh->h', pos_emb, w1r) + b1
    return jax.nn.gelu(hid) @ w2


def nsa_query(h, w_in):
    n, T, _ = h.shape
    proj = h @ w_in
    q = proj[..., :NSA_HEADS * NSA_HEAD_DIM].reshape(n, T, NSA_KV_GROUPS, NSA_GROUP_HEADS, NSA_HEAD_DIM)
    gate = proj[..., NSA_HEADS * NSA_HEAD_DIM:].reshape(n, T, NSA_KV_GROUPS, NSA_GROUP_HEADS, N_BRANCH)
    return q, gate


def nsa_branches(q, gate, t_pos, kc, vc, gather_sel, k_w, v_w, s_pos):
    scale = NSA_HEAD_DIM ** -0.5
    q_rot = rope_partial(q, t_pos)
    n_c = kc.shape[1]
    vis = (jnp.arange(n_c) * CMP_STRIDE + CMP_BLOCK - 1)[None, :] <= t_pos[:, None]
    p_cmp = masked_softmax(jnp.einsum('nqghd,ncgd->nghqc', q, kc) * scale, vis)
    o_cmp = jnp.einsum('nghqc,ncgd->nqghd', p_cmp.astype(vc.dtype), vc)
    ratio = SLC_BLOCK // CMP_STRIDE
    lead = CMP_BLOCK // CMP_STRIDE - 1
    n_s = n_c // ratio
    pg = jnp.pad(jnp.sum(p_cmp, axis=2), ((0, 0), (0, 0), (0, 0), (lead, 0)))
    p_slc = pg[..., 0:ratio * n_s:ratio]
    for o in range(1, ratio + lead):
        p_slc = p_slc + pg[..., o:o + ratio * n_s:ratio]
    jb = jnp.arange(n_s)[None, :]
    jt = (t_pos // SLC_BLOCK)[:, None]
    forced = (jb == 0) | (jb == jt) | (jb == jt - 1)
    score = jnp.where(jb > jt, -jnp.inf, jnp.where(forced, FORCED_SCORE, p_slc))
    _, idx = lax.top_k(score, min(SLC_TOPN, n_s))
    k_sel, v_sel = gather_sel(idx)
    tok = idx[..., None] * SLC_BLOCK + jnp.arange(SLC_BLOCK)
    m_sel = tok <= t_pos[None, None, :, None, None]
    s = jnp.einsum('nqghd,ngqksd->nghqks', q_rot, k_sel) * scale
    shp = s.shape
    p = masked_softmax(s.reshape(shp[:4] + (-1,)), m_sel.reshape(m_sel.shape[:3] + (-1,))[:, :, None])
    o_slc = jnp.einsum('nghqks,ngqksd->nqghd', p.reshape(shp).astype(v_sel.dtype), v_sel)
    m_w = (s_pos[None, :] <= t_pos[:, None]) & (s_pos[None, :] >= t_pos[:, None] - WINDOW) & (s_pos[None, :] >= 0)
    p_w = masked_softmax(jnp.einsum('nqghd,nkgd->nghqk', q_rot, k_w) * scale, m_w)
    o_win = jnp.einsum('nghqk,nkgd->nqghd', p_w.astype(v_w.dtype), v_w)
    g = jax.nn.sigmoid(gate.astype(jnp.float32)).astype(q.dtype)
    return g[..., 0:1] * o_cmp + g[..., 1:2] * o_slc + g[..., 2:3] * o_win


def prompt_kv_context(cmp_kv, slc_kv, win_kv, cmp_pos, cmp_w1, cmp_b1, cmp_w2):
    n, T = cmp_kv.shape[:2]
    kc = compress_rows(cmp_kv[:, :, 0], cmp_pos[0], cmp_w1[0], cmp_b1[0], cmp_w2[0])
    vc = compress_rows(cmp_kv[:, :, 1], cmp_pos[1], cmp_w1[1], cmp_b1[1], cmp_w2[1])
    n_s = T // SLC_BLOCK
    to_blocks = lambda a: a.reshape(n, n_s, SLC_BLOCK, NSA_KV_GROUPS, NSA_HEAD_DIM).transpose(0, 3, 1, 2, 4)
    win_pad = jnp.pad(win_kv, ((0, 0), (WINDOW, 0), (0, 0), (0, 0), (0, 0)))
    return kc, vc, to_blocks(slc_kv[:, :, 0]), to_blocks(slc_kv[:, :, 1]), win_pad


def prompt_nsa(h, ctx, w_in, w_o):
    kc, vc, sel_k, sel_v, win_pad = ctx
    n, T, _ = h.shape
    q, gate = nsa_query(h, w_in)
    n_qb = T // Q_BLOCK
    blocks = lambda a: a.reshape((n, n_qb, Q_BLOCK) + a.shape[2:]).swapaxes(0, 1)
    bi = jnp.arange(n)[:, None, None, None]
    gi = jnp.arange(NSA_KV_GROUPS)[None, :, None, None]
    gather_sel = lambda idx: (sel_k[bi, gi, idx], sel_v[bi, gi, idx])

    def one_block(args):
        qi, q_blk, g_blk = args
        t0 = qi * Q_BLOCK
        t_pos = t0 + jnp.arange(Q_BLOCK)
        band = lax.dynamic_slice_in_dim(win_pad, t0, WINDOW + Q_BLOCK, axis=1)
        s_pos = t0 - WINDOW + jnp.arange(WINDOW + Q_BLOCK)
        return nsa_branches(q_blk, g_blk, t_pos, kc, vc, gather_sel, band[:, :, 0], band[:, :, 1], s_pos)

    o = lax.map(one_block, (jnp.arange(n_qb), blocks(q), blocks(gate)))
    return o.swapaxes(0, 1).reshape(n, T, NSA_HEADS * NSA_HEAD_DIM) @ w_o


def sample_kv_context(cmp_new, slc_new, win_new, cache_cmp_kv, cache_slc_kv, cache_win_kv, page_table,
                      cmp_pos, cmp_w1, cmp_b1, cmp_w2):
    n, S = cmp_new.shape[:2]
    n_new_blk = -(-S // SLC_BLOCK)
    pad = ((0, 0), (0, n_new_blk * SLC_BLOCK - S), (0, 0), (0, 0), (0, 0))
    past = cache_cmp_kv[page_table].reshape((n, PAST_LEN) + cache_cmp_kv.shape[2:])
    rows = jnp.concatenate([past, jnp.pad(cmp_new.astype(past.dtype), pad)], axis=1)
    kc = compress_rows(rows[:, :, 0], cmp_pos[0], cmp_w1[0], cmp_b1[0], cmp_w2[0])
    vc = compress_rows(rows[:, :, 1], cmp_pos[1], cmp_w1[1], cmp_b1[1], cmp_w2[1])
    pool = cache_slc_kv.reshape((-1, PAGE_SIZE // SLC_BLOCK, SLC_BLOCK) + cache_slc_kv.shape[2:])
    new_blocks = jnp.pad(slc_new, pad).reshape((n, n_new_blk, SLC_BLOCK) + slc_new.shape[2:])
    win_all = jnp.concatenate([cache_win_kv, win_new.astype(cache_win_kv.dtype)], axis=1)
    return kc, vc, pool, new_blocks, win_all


def sample_nsa(h, ctx, page_table, w_in, w_o):
    kc, vc, pool, new_blocks, win_all = ctx
    n, S, _ = h.shape
    q, gate = nsa_query(h, w_in)
    t_pos = PAST_LEN + jnp.arange(S)
    sub = PAGE_SIZE // SLC_BLOCK
    n_past_blk = PAST_LEN // SLC_BLOCK
    n_new_blk = new_blocks.shape[1]
    bi = jnp.arange(n)[:, None, None, None]
    gi = jnp.arange(NSA_KV_GROUPS)[None, :, None, None]

    def gather_sel(idx):
        jp = jnp.minimum(idx, n_past_blk - 1)
        phys = page_table[bi, jp // sub]
        from_past = pool[phys, jp % sub, :, :, gi]
        jn = jnp.clip(idx - n_past_blk, 0, n_new_blk - 1)
        from_new = new_blocks[bi, jn, :, :, gi].astype(from_past.dtype)
        blk = jnp.where((idx >= n_past_blk)[..., None, None, None], from_new, from_past)
        return blk[..., 0, :], blk[..., 1, :]

    wb = win_all.shape[1] - S
    s_pos = PAST_LEN - wb + jnp.arange(win_all.shape[1])
    o = nsa_branches(q, gate, t_pos, kc, vc, gather_sel, win_all[:, :, 0], win_all[:, :, 1], s_pos)
    return o.reshape(n, S, NSA_HEADS * NSA_HEAD_DIM) @ w_o


def kernel(x_prompt, x_sample, state_wkv, state_shift, cache_cmp_kv, cache_slc_kv, cache_win_kv, page_table, norm_mix, norm_ffn, norm_kv, norm_final, rw_mu, rw_w_rkv, rw_w0, rw_w1, rw_w2, rw_a0, rw_a1, rw_a2, rw_g1, rw_g2, rw_k_k, rw_k_a, rw_r_k, rw_gn_w, rw_gn_b, rw_w_o, nsa_w_kv, nsa_cmp_pos, nsa_cmp_w1, nsa_cmp_b1, nsa_cmp_w2, nsa_w_in, nsa_w_o, ffn_w_gu, ffn_w_down, moe_router, moe_w_gu, moe_w_down):
    cmp_params = (nsa_cmp_pos, nsa_cmp_w1, nsa_cmp_b1, nsa_cmp_w2)

    def trunk(x, pos, shift0, wkv0, make_context, attend):
        wkv_out, shift_out = [], []
        rows, ctx = None, None
        for li in range(DEPTH):
            h = rmsnorm(x, norm_mix[li])
            if li < N_A_LAYERS:
                y, s_fin, h_last = rwkv7_time_mix(
                    h, shift0[li], wkv0[li], rw_mu[li], rw_w_rkv[li], rw_w0[li], rw_w1[li], rw_w2[li],
                    rw_a0[li], rw_a1[li], rw_a2[li], rw_g1[li], rw_g2[li], rw_k_k[li], rw_k_a[li],
                    rw_r_k[li], rw_gn_w[li], rw_gn_b[li], rw_w_o[li])
                wkv_out.append(s_fin)
                shift_out.append(h_last)
            else:
                b = li - N_A_LAYERS
                y = attend(h, ctx, nsa_w_in[b], nsa_w_o[b])
            x = x + y
            h = rmsnorm(x, norm_ffn[li])
            if li % 2 == 0:
                x = x + swiglu(h, ffn_w_gu[li // 2], ffn_w_down[li // 2])
            else:
                x = x + moe_swiglu(h, moe_router[li // 2], moe_w_gu[li // 2], moe_w_down[li // 2])
            if li == N_A_LAYERS - 1:
                rows = shared_kv_rows(x, norm_kv, nsa_w_kv, pos)
                ctx = make_context(*rows)
        return rmsnorm(x, norm_final), jnp.stack(wkv_out), jnp.stack(shift_out), rows, ctx

    n_p = x_prompt.shape[0]
    pos_p = jnp.arange(x_prompt.shape[1], dtype=jnp.int32)
    shift0_p = jnp.zeros((N_A_LAYERS, n_p, D_MODEL), x_prompt.dtype)
    wkv0_p = jnp.zeros((N_A_LAYERS, n_p, RWKV_HEADS, RWKV_HEAD, RWKV_HEAD), jnp.float32)
    y_prompt, wkv_p, shift_p, rows_p, _ = trunk(
        x_prompt, pos_p, shift0_p, wkv0_p,
        lambda c, s, w: prompt_kv_context(c, s, w, *cmp_params),
        prompt_nsa)
    cmp_kv_p, slc_kv_p, win_rows_p = rows_p
    win_kv_p = win_rows_p[:, -min(WINDOW, x_prompt.shape[1]):]

    pos_s = PAST_LEN + jnp.arange(x_sample.shape[1], dtype=jnp.int32)
    y_sample, wkv_s, shift_s, rows_s, ctx_s = trunk(
        x_sample, pos_s, state_shift, state_wkv,
        lambda c, s, w: sample_kv_context(c, s, w, cache_cmp_kv, cache_slc_kv, cache_win_kv, page_table, *cmp_params),
        lambda h, ctx, wi, wo: sample_nsa(h, ctx, page_table, wi, wo))
    cmp_kv_s, slc_kv_s, _ = rows_s
    win_kv_s = ctx_s[4][:, -cache_win_kv.shape[1]:]

    return (y_prompt, y_sample, wkv_p, shift_p, cmp_kv_p, slc_kv_p, win_kv_p,
            wkv_s, shift_s, cmp_kv_s, slc_kv_s, win_kv_s)
```

```python
import functools
import math

import jax
import jax.numpy as jnp
from jax import lax
from jax.experimental import pallas as pl
from jax.experimental.pallas import tpu as pltpu

D_MODEL = 1024
DEPTH = 2
PAST_LEN = 16384
PAGE_SIZE = 128
N_A_LAYERS = DEPTH // 2
RWKV_HEAD = 64
RWKV_HEADS = D_MODEL // RWKV_HEAD
DECAY_SCALE = math.exp(-0.5)
GN_EPS = RWKV_HEAD * 1e-5
NSA_HEADS = 16
NSA_HEAD_DIM = 64
NSA_KV_GROUPS = 2
NSA_GROUP_HEADS = NSA_HEADS // NSA_KV_GROUPS
N_BRANCH = 3
CMP_BLOCK = 32
CMP_STRIDE = 16
CMP_HIDDEN = 128
SLC_BLOCK = 64
SLC_TOPN = 16
WINDOW = 512
Q_BLOCK = 128
FORCED_SCORE = 1e4
ROPE_THETA = 500000.0
ROT_DIM = NSA_HEAD_DIM // 4
N_EXPERTS = 8
TOP_K = 2
MOE_BLOCK = 256
NORM_EPS = 1e-6


def _rmsnorm_kernel(x_ref, g_ref, o_ref):
    x = x_ref[...]
    y = x * lax.rsqrt(jnp.mean(x * x, axis=-1, keepdims=True) + NORM_EPS)
    o_ref[...] = y * g_ref[...]


def rmsnorm(x, g):
    shp = x.shape
    x2 = x.reshape(-1, shp[-1])
    rows = x2.shape[0]
    tm = min(rows, 512)
    out = pl.pallas_call(
        _rmsnorm_kernel,
        grid=(rows // tm,),
        in_specs=[pl.BlockSpec((tm, shp[-1]), lambda i: (i, 0)),
                  pl.BlockSpec((1, shp[-1]), lambda i: (0, 0))],
        out_specs=pl.BlockSpec((tm, shp[-1]), lambda i: (i, 0)),
        out_shape=jax.ShapeDtypeStruct(x2.shape, x.dtype),
        name="rmsnorm",
    )(x2, g.reshape(1, -1))
    return out.reshape(shp)


def rope_partial(x, pos):
    half = ROT_DIM // 2
    inv = ROPE_THETA ** (-2.0 * jnp.arange(half, dtype=jnp.float32) / ROT_DIM)
    ang = pos.astype(jnp.float32)[:, None] * inv[None, :]
    shape = (1, pos.shape[0]) + (1,) * (x.ndim - 3) + (half,)
    cos = jnp.cos(ang).reshape(shape)
    sin = jnp.sin(ang).reshape(shape)
    xf = x.astype(jnp.float32)
    x1, x2 = xf[..., :half], xf[..., half:ROT_DIM]
    out = jnp.concatenate([x1 * cos - x2 * sin, x2 * cos + x1 * sin, xf[..., ROT_DIM:]], axis=-1)
    return out.astype(x.dtype)


def masked_softmax(s, mask):
    s = jnp.where(mask, s.astype(jnp.float32), -jnp.inf)
    m = jnp.max(s, axis=-1, keepdims=True)
    m = jnp.where(jnp.isfinite(m), m, 0.0)
    e = jnp.where(mask, jnp.exp(s - m), 0.0)
    return e / jnp.maximum(jnp.sum(e, axis=-1, keepdims=True), 1e-30)


def swiglu(h, w_gu, w_down):
    g, u = jnp.split(h @ w_gu, 2, axis=-1)
    return (jax.nn.silu(g) * u) @ w_down


def moe_swiglu(h, w_router, w_gu_e, w_down_e):
    shp = h.shape
    xt = h.reshape(-1, shp[-1])
    n_tok = xt.shape[0]
    logits = (xt @ w_router).astype(jnp.float32)
    top_val, top_idx = lax.top_k(logits, TOP_K)
    gate = jax.nn.softmax(top_val, axis=-1).astype(h.dtype)
    nk = n_tok * TOP_K
    flat_e = top_idx.reshape(nk)
    flat_tok = jnp.repeat(jnp.arange(n_tok, dtype=jnp.int32), TOP_K)
    flat_gate = gate.reshape(nk)
    order = jnp.argsort(flat_e)
    sorted_e = flat_e[order]
    counts = jnp.bincount(flat_e, length=N_EXPERTS)
    padded = (counts + MOE_BLOCK - 1) // MOE_BLOCK * MOE_BLOCK
    ends_pad = jnp.cumsum(padded)
    starts_pad = ends_pad - padded
    starts_raw = jnp.cumsum(counts) - counts
    dest = starts_pad[sorted_e] + jnp.arange(nk) - starts_raw[sorted_e]
    n_rows = (nk + MOE_BLOCK - 1) // MOE_BLOCK * MOE_BLOCK + N_EXPERTS * MOE_BLOCK
    n_blk = n_rows // MOE_BLOCK
    row_tok = jnp.full((n_rows,), n_tok, jnp.int32).at[dest].set(flat_tok[order])
    row_gate = jnp.zeros((n_rows,), h.dtype).at[dest].set(flat_gate[order])
    blk_e = jnp.minimum(jnp.searchsorted(ends_pad, jnp.arange(n_blk) * MOE_BLOCK, side='right'), N_EXPERTS - 1)
    x_pad = jnp.concatenate([xt, jnp.zeros((1, shp[-1]), xt.dtype)], axis=0)
    xb = x_pad[row_tok].reshape(n_blk, MOE_BLOCK, shp[-1])

    def expert_block(args):
        xblk, e = args
        return swiglu(xblk, w_gu_e[e], w_down_e[e])

    yb = lax.map(expert_block, (xb, blk_e))
    y_rows = yb.reshape(n_rows, shp[-1]) * row_gate[:, None]
    out = jnp.zeros((n_tok + 1, shp[-1]), h.dtype).at[row_tok].add(y_rows)[:n_tok]
    return out.reshape(shp)


WKV_CHUNK = 64
WKV_HEADS_PER_STEP = 16

_NN = (((1,), (0,)), ((), ()))
_NT = (((1,), (1,)), ((), ()))
_TN = (((0,), (0,)), ((), ()))


def _mm(x, y, dims):
    return lax.dot_general(x.astype(jnp.bfloat16), y.astype(jnp.bfloat16), dims,
                           preferred_element_type=jnp.float32)


def _wkv7_chunk_kernel(r_ref, lw_ref, k_ref, v_ref, a_ref, b_ref, s0_ref, y_ref, sout_ref, state_ref):
    c = pl.program_id(2)
    L = r_ref.shape[1]
    hb = state_ref.shape[0]
    N = RWKV_HEAD

    @pl.when(c == 0)
    def _():
        state_ref[...] = s0_ref[0]

    row = lax.broadcasted_iota(jnp.int32, (L, L), 0)
    col = lax.broadcasted_iota(jnp.int32, (L, L), 1)
    strict = row > col
    incl = row >= col
    tri = incl.astype(jnp.bfloat16)
    eye_n = lax.broadcasted_iota(jnp.int32, (N, N), 0) == lax.broadcasted_iota(jnp.int32, (N, N), 1)

    lw = lw_ref[0]
    lw_hi = lw.astype(jnp.bfloat16)
    lw_lo = (lw - lw_hi.astype(jnp.float32)).astype(jnp.bfloat16)
    cum = (lax.dot_general(tri, lw_hi, _NN, preferred_element_type=jnp.float32)
           + lax.dot_general(tri, lw_lo, _NN, preferred_element_type=jnp.float32))
    cum_last = cum[L - 1:L, :]
    e_neg = jnp.exp(-cum)
    e_tail = jnp.exp(cum_last - cum)
    a_all = a_ref[0] * jnp.exp(cum - lw)
    r_all = r_ref[0] * jnp.exp(cum)
    b_in = b_ref[0]
    k_in = k_ref[0]
    bt_all = b_in * e_neg
    kt_all = k_in * e_neg
    bh_all = b_in * e_tail
    kh_all = k_in * e_tail
    wl_all = jnp.exp(cum_last)
    v_all = v_ref[0]

    heads = range(hb)
    per_head = lambda t: [t[:, j * N:(j + 1) * N] for j in heads]
    at, rt, bt, kt, bh, kh, v, wl = (per_head(t) for t in
                                     (a_all, r_all, bt_all, kt_all, bh_all, kh_all, v_all, wl_all))
    a_ab = [jnp.where(strict, _mm(at[j], bt[j], _NT), 0.0) for j in heads]
    a_ak = [jnp.where(strict, _mm(at[j], kt[j], _NT), 0.0) for j in heads]
    r_b = [jnp.where(incl, _mm(rt[j], bt[j], _NT), 0.0) for j in heads]
    r_k = [jnp.where(incl, _mm(rt[j], kt[j], _NT), 0.0) for j in heads]
    av = [_mm(a_ak[j], v[j], _NN) for j in heads]
    ht = [_mm(v[j], kh[j], _TN) for j in heads]
    yp = [_mm(r_k[j], v[j], _NN) for j in heads]
    pw = a_ab
    inv_a = a_ab
    n = 1
    while 2 * n < L:
        pw = [_mm(pw[j], pw[j], _NN) for j in heads]
        inv_a = [inv_a[j] + pw[j] + _mm(inv_a[j], pw[j], _NN) for j in heads]
        n *= 2
    ap = [at[j] + _mm(inv_a[j], at[j], _NN) for j in heads]
    vp = [av[j] + _mm(inv_a[j], av[j], _NN) for j in heads]
    g = [jnp.where(eye_n, wl[j], 0.0) + _mm(bh[j], ap[j], _TN) for j in heads]
    ht = [ht[j] + _mm(vp[j], bh[j], _TN) for j in heads]
    rp = [rt[j] + _mm(r_b[j], ap[j], _NN) for j in heads]
    yp = [yp[j] + _mm(r_b[j], vp[j], _NN) for j in heads]
    s_prev = [state_ref[j] for j in heads]
    y_ref[0] = jnp.concatenate([_mm(rp[j], s_prev[j], _NT) + yp[j] for j in heads], axis=-1)
    for j in heads:
        state_ref[j] = _mm(s_prev[j], g[j], _NT) + ht[j]

    @pl.when(c == pl.num_programs(2) - 1)
    def _():
        sout_ref[0] = state_ref[...]


def wkv7_chunked(r, lw, k, v, a_vec, b_vec, s0):
    n, T, D = r.shape
    L = WKV_CHUNK
    hb = WKV_HEADS_PER_STEP
    w = hb * RWKV_HEAD
    seq = pl.BlockSpec((1, L, w), lambda b, h, c: (b, c, h))
    st = pl.BlockSpec((1, hb, RWKV_HEAD, RWKV_HEAD), lambda b, h, c: (b, h, 0, 0))
    return pl.pallas_call(
        _wkv7_chunk_kernel,
        grid=(n, D // w, T // L),
        in_specs=[seq] * 6 + [st],
        out_specs=[seq, st],
        out_shape=[jax.ShapeDtypeStruct((n, T, D), jnp.float32),
                   jax.ShapeDtypeStruct(s0.shape, jnp.float32)],
        scratch_shapes=[pltpu.VMEM((hb, RWKV_HEAD, RWKV_HEAD), jnp.float32)],
        compiler_params=pltpu.CompilerParams(dimension_semantics=("parallel", "parallel", "arbitrary")),
        name="wkv7_chunked",
    )(r, lw, k, v, a_vec, b_vec, s0)


def wkv7_scan(r, lw, k, v, a_vec, b_vec, s0):
    def step(S, inp):
        r_t, lw_t, k_t, v_t, a_t, b_t = inp
        sa = jnp.sum(S * a_t[:, :, None, :], axis=-1)
        S = S * jnp.exp(lw_t)[:, :, None, :] + sa[..., None] * b_t[:, :, None, :] + v_t[..., None] * k_t[:, :, None, :]
        return S, jnp.sum(S * r_t[:, :, None, :], axis=-1)

    xs = tuple(jnp.moveaxis(t, 1, 0) for t in (r, lw, k, v, a_vec, b_vec))
    s_fin, ys = lax.scan(step, s0, xs)
    return jnp.moveaxis(ys, 0, 1), s_fin


def rwkv7_time_mix(h, h_prev, s0, mu, w_rkv, w0, w1, w2, a0, a1, a2, g1, g2, k_k, k_a, r_k, gn_w, gn_b, w_o):
    n, T, D = h.shape
    f32 = jnp.float32
    dx = jnp.concatenate([h_prev[:, None, :].astype(h.dtype), h[:, :-1]], axis=1) - h
    xr, xw, xk, xv, xa, xg = (h + dx * mu[i] for i in range(6))
    r = xr @ w_rkv[0]
    k = xk @ w_rkv[1]
    v = xv @ w_rkv[2]
    log_decay = -DECAY_SCALE * jax.nn.sigmoid((w0 + jnp.tanh(xw @ w1) @ w2).astype(f32))
    a = jax.nn.sigmoid((a0 + (xa @ a1) @ a2).astype(f32))
    g = jax.nn.sigmoid(xg @ g1) @ g2
    heads = lambda t: t.astype(f32).reshape(n, T, RWKV_HEADS, RWKV_HEAD)
    kk = heads(k * k_k)
    kk = kk / jnp.maximum(jnp.sqrt(jnp.sum(kk * kk, axis=-1, keepdims=True)), 1e-12)
    a_h = heads(a)
    k_h = heads(k.astype(f32) * (1.0 + (a - 1.0) * k_a.astype(f32)))
    r_h, v_h = heads(r), heads(v)
    if T % WKV_CHUNK == 0:
        flat = lambda t: t.reshape(n, T, D)
        y, s_fin = wkv7_chunked(flat(r_h), log_decay, flat(k_h), flat(v_h), flat(-kk), flat(kk * a_h),
                                s0.astype(f32))
        y = heads(y)
    else:
        y, s_fin = wkv7_scan(r_h, heads(log_decay), k_h, v_h, -kk, kk * a_h, s0.astype(f32))
    mean = jnp.mean(y, axis=-1, keepdims=True)
    var = jnp.mean(jnp.square(y - mean), axis=-1, keepdims=True)
    y = ((y - mean) * lax.rsqrt(var + GN_EPS)).reshape(n, T, D) * gn_w.astype(f32) + gn_b.astype(f32)
    bonus = jnp.sum(r_h * k_h * r_k.astype(f32), axis=-1, keepdims=True) * v_h
    y = (y + bonus.reshape(n, T, D)).astype(h.dtype)
    return (y * g) @ w_o, s_fin, h[:, -1]


def shared_kv_rows(x, norm_kv, w_kv, pos):
    n, T, _ = x.shape
    kv = (rmsnorm(x, norm_kv) @ w_kv).reshape(n, T, N_BRANCH, 2, NSA_KV_GROUPS, NSA_HEAD_DIM)
    cmp_kv = kv[:, :, 0]
    slc_kv = jnp.stack([rope_partial(kv[:, :, 1, 0], pos), kv[:, :, 1, 1]], axis=2)
    win_kv = jnp.stack([rope_partial(kv[:, :, 2, 0], pos), kv[:, :, 2, 1]], axis=2)
    return cmp_kv, slc_kv, win_kv


def compress_rows(rows, pos_emb, w1, b1, w2):
    n, T, G, dh = rows.shape
    c = rows.reshape(n, T // CMP_STRIDE, CMP_STRIDE, G, dh)
    w1r = w1.reshape(CMP_BLOCK, dh, CMP_HIDDEN)
    first = jnp.einsum('ncjgd,jdh->ncgh', c, w1r[:CMP_STRIDE])
    second = jnp.einsum('ncjgd,jdh->ncgh', c, w1r[CMP_STRIDE:])
    second = jnp.concatenate([second[:, 1:], jnp.zeros_like(second[:, :1])], axis=1)
    hid = first + second + jnp.einsum('jd,jdh->h', pos_emb, w1r) + b1
    return jax.nn.gelu(hid) @ w2


def nsa_query(h, w_in):
    n, T, _ = h.shape
    proj = h @ w_in
    q = proj[..., :NSA_HEADS * NSA_HEAD_DIM].reshape(n, T, NSA_KV_GROUPS, NSA_GROUP_HEADS, NSA_HEAD_DIM)
    gate = proj[..., NSA_HEADS * NSA_HEAD_DIM:].reshape(n, T, NSA_KV_GROUPS, NSA_GROUP_HEADS, N_BRANCH)
    return q, gate


def nsa_branches(q, gate, t_pos, kc, vc, gather_sel, k_w, v_w, s_pos):
    scale = NSA_HEAD_DIM ** -0.5
    q_rot = rope_partial(q, t_pos)
    n_c = kc.shape[1]
    vis = (jnp.arange(n_c) * CMP_STRIDE + CMP_BLOCK - 1)[None, :] <= t_pos[:, None]
    p_cmp = masked_softmax(jnp.einsum('nqghd,ncgd->nghqc', q, kc) * scale, vis)
    o_cmp = jnp.einsum('nghqc,ncgd->nqghd', p_cmp.astype(vc.dtype), vc)
    ratio = SLC_BLOCK // CMP_STRIDE
    lead = CMP_BLOCK // CMP_STRIDE - 1
    n_s = n_c // ratio
    pg = jnp.pad(jnp.sum(p_cmp, axis=2), ((0, 0), (0, 0), (0, 0), (lead, 0)))
    p_slc = pg[..., 0:ratio * n_s:ratio]
    for o in range(1, ratio + lead):
        p_slc = p_slc + pg[..., o:o + ratio * n_s:ratio]
    jb = jnp.arange(n_s)[None, :]
    jt = (t_pos // SLC_BLOCK)[:, None]
    forced = (jb == 0) | (jb == jt) | (jb == jt - 1)
    score = jnp.where(jb > jt, -jnp.inf, jnp.where(forced, FORCED_SCORE, p_slc))
    _, idx = lax.top_k(score, min(SLC_TOPN, n_s))
    k_sel, v_sel = gather_sel(idx)
    tok = idx[..., None] * SLC_BLOCK + jnp.arange(SLC_BLOCK)
    m_sel = tok <= t_pos[None, None, :, None, None]
    s = jnp.einsum('nqghd,ngqksd->nghqks', q_rot, k_sel) * scale
    shp = s.shape
    p = masked_softmax(s.reshape(shp[:4] + (-1,)), m_sel.reshape(m_sel.shape[:3] + (-1,))[:, :, None])
    o_slc = jnp.einsum('nghqks,ngqksd->nqghd', p.reshape(shp).astype(v_sel.dtype), v_sel)
    m_w = (s_pos[None, :] <= t_pos[:, None]) & (s_pos[None, :] >= t_pos[:, None] - WINDOW) & (s_pos[None, :] >= 0)
    p_w = masked_softmax(jnp.einsum('nqghd,nkgd->nghqk', q_rot, k_w) * scale, m_w)
    o_win = jnp.einsum('nghqk,nkgd->nqghd', p_w.astype(v_w.dtype), v_w)
    g = jax.nn.sigmoid(gate.astype(jnp.float32)).astype(q.dtype)
    return g[..., 0:1] * o_cmp + g[..., 1:2] * o_slc + g[..., 2:3] * o_win


def prompt_kv_context(cmp_kv, slc_kv, win_kv, cmp_pos, cmp_w1, cmp_b1, cmp_w2):
    kc = compress_rows(cmp_kv[:, :, 0], cmp_pos[0], cmp_w1[0], cmp_b1[0], cmp_w2[0])
    vc = compress_rows(cmp_kv[:, :, 1], cmp_pos[1], cmp_w1[1], cmp_b1[1], cmp_w2[1])
    return kc, vc, slc_kv, win_kv


NSA_SLABS = NSA_GROUP_HEADS
NSA_TILES = NSA_SLABS * NSA_KV_GROUPS
KEY_BLOCK = 128
MASKED = -1e30
SLC_PER_CMP = SLC_BLOCK // CMP_STRIDE
CMP_LEAD = CMP_BLOCK // CMP_STRIDE - 1
SLAB = NSA_KV_GROUPS * NSA_HEAD_DIM
SEL_LANES = 64


def _group_tiles(q_ref, qs_ref):
    tq = q_ref.shape[1]
    lane = lax.broadcasted_iota(jnp.int32, (tq, SLAB), 1)
    for i in range(NSA_SLABS):
        qs = q_ref[0, :, i * 128:(i + 1) * 128]
        for g in range(NSA_KV_GROUPS):
            in_group = (lane >= g * NSA_HEAD_DIM) & (lane < (g + 1) * NSA_HEAD_DIM)
            qs_ref[2 * i + g] = jnp.where(in_group, qs, 0.0).astype(jnp.bfloat16)


def _merge_groups(o_ref, tiles):
    tq = tiles[0].shape[0]
    lane = lax.broadcasted_iota(jnp.int32, (tq, SLAB), 1)
    for i in range(NSA_SLABS):
        o_ref[0, :, i * 128:(i + 1) * 128] = jnp.where(lane < NSA_HEAD_DIM, tiles[2 * i], tiles[2 * i + 1])


def _nsa_cmp_kernel(q_ref, kc_ref, vc_ref, o_ref, sel_ref, qs_ref):
    qi = pl.program_id(1)
    tq = q_ref.shape[1]
    nc = kc_ref.shape[1]
    ns = nc // SLC_PER_CMP
    t0 = qi * tq
    _group_tiles(q_ref, qs_ref)
    kc = kc_ref[0]
    vc = vc_ref[0]
    t_row = t0 + lax.broadcasted_iota(jnp.int32, (tq, nc), 0)
    c_pos = lax.broadcasted_iota(jnp.int32, (tq, nc), 1) * CMP_STRIDE + (CMP_BLOCK - 1)
    vis = c_pos <= t_row
    tiles = range(NSA_TILES)
    s = [_mm(qs_ref[r], kc, _NT) for r in tiles]
    p = []
    for r in tiles:
        sr = jnp.where(vis, s[r], MASKED)
        m = jnp.max(sr, axis=-1, keepdims=True)
        e = jnp.where(vis, jnp.exp(sr - m), 0.0)
        p.append(e / jnp.maximum(jnp.sum(e, axis=-1, keepdims=True), 1e-30))
    _merge_groups(o_ref, [_mm(p[r], vc, _NN) for r in tiles])

    jrow = lax.broadcasted_iota(jnp.int32, (ns, nc), 0)
    ccol = lax.broadcasted_iota(jnp.int32, (ns, nc), 1)
    pool = ((ccol >= SLC_PER_CMP * jrow - CMP_LEAD) & (ccol < SLC_PER_CMP * (jrow + 1))).astype(jnp.bfloat16)
    jb = lax.broadcasted_iota(jnp.int32, (ns, tq), 0)
    jt = (t0 + lax.broadcasted_iota(jnp.int32, (ns, tq), 1)) // SLC_BLOCK
    forced = (jb == 0) | (jb == jt) | (jb == jt - 1)
    sel_t = []
    for g in range(NSA_KV_GROUPS):
        pg = p[g]
        for i in range(1, NSA_SLABS):
            pg = pg + p[2 * i + g]
        pg_hi = pg.astype(jnp.bfloat16)
        pg_lo = (pg - pg_hi.astype(jnp.float32)).astype(jnp.bfloat16)
        p_slc = (lax.dot_general(pool, pg_hi, _NT, preferred_element_type=jnp.float32)
                 + lax.dot_general(pool, pg_lo, _NT, preferred_element_type=jnp.float32))
        score = jnp.where(jb > jt, -jnp.inf, jnp.where(forced, FORCED_SCORE, p_slc))
        rank = jnp.zeros((ns, tq), jnp.float32)
        for i in range(ns):
            row = score[i:i + 1, :]
            tie = jnp.where(jb > i, 1.0, 0.0)
            rank = rank + jnp.where(row > score, 1.0, 0.0) + jnp.where(row == score, tie, 0.0)
        sel_t.append(jnp.where(rank < min(SLC_TOPN, ns), 1.0, 0.0))
        if ns < SEL_LANES:
            sel_t.append(jnp.zeros((SEL_LANES - ns, tq), jnp.float32))
    sel_ref[0] = jnp.concatenate(sel_t, axis=0).T.astype(sel_ref.dtype)


def _nsa_dense_kernel(q_ref, k_ref, v_ref, sel_ref, o_ref, qs_ref, m_ref, l_ref, acc_ref, *, windowed):
    qi = pl.program_id(1)
    tq = q_ref.shape[1]
    t0 = qi * tq
    _group_tiles(q_ref, qs_ref)
    m_ref[...] = jnp.full(m_ref.shape, MASKED, jnp.float32)
    l_ref[...] = jnp.zeros(l_ref.shape, jnp.float32)
    acc_ref[...] = jnp.zeros(acc_ref.shape, jnp.float32)
    t_row = t0 + lax.broadcasted_iota(jnp.int32, (tq, KEY_BLOCK), 0)
    k_lane = lax.broadcasted_iota(jnp.int32, (tq, KEY_BLOCK), 1)
    tiles = range(NSA_TILES)
    first = jnp.maximum(qi - WINDOW // KEY_BLOCK, 0) if windowed else 0

    def key_block(kb, carry):
        start = pl.multiple_of(kb * KEY_BLOCK, KEY_BLOCK)
        kblk = k_ref[0, pl.ds(start, KEY_BLOCK), :]
        vblk = v_ref[0, pl.ds(start, KEY_BLOCK), :]
        k_pos = start + k_lane
        if windowed:
            allowed = [(k_pos <= t_row) & (k_pos >= t_row - WINDOW)] * NSA_KV_GROUPS
        else:
            sel = sel_ref[0]
            n_sel = NSA_KV_GROUPS * SEL_LANES
            erow = lax.broadcasted_iota(jnp.int32, (n_sel, KEY_BLOCK), 0)
            ecol = lax.broadcasted_iota(jnp.int32, (n_sel, KEY_BLOCK), 1)
            blk = kb * (KEY_BLOCK // SLC_BLOCK) + ecol // SLC_BLOCK
            allowed = []
            for g in range(NSA_KV_GROUPS):
                expand = (erow == blk + g * SEL_LANES).astype(jnp.bfloat16)
                picked = lax.dot_general(sel, expand, _NN, preferred_element_type=jnp.float32)
                allowed.append((picked > 0.5) & (k_pos <= t_row))
        s = [_mm(qs_ref[r], kblk, _NT) for r in tiles]
        p = []
        alpha = []
        for r in tiles:
            ok = allowed[r % NSA_KV_GROUPS]
            sr = jnp.where(ok, s[r], MASKED)
            m_prev = m_ref[r]
            m_new = jnp.maximum(m_prev, jnp.max(sr, axis=-1, keepdims=True))
            a = jnp.exp(m_prev - m_new)
            pr = jnp.where(ok, jnp.exp(sr - m_new), 0.0)
            l_ref[r] = a * l_ref[r] + jnp.sum(pr, axis=-1, keepdims=True)
            m_ref[r] = m_new
            alpha.append(a)
            p.append(pr)
        pv = [_mm(p[r], vblk, _NN) for r in tiles]
        for r in tiles:
            acc_ref[r] = alpha[r] * acc_ref[r] + pv[r]
        return carry

    lax.fori_loop(first, qi * (tq // KEY_BLOCK) + tq // KEY_BLOCK, key_block, 0)
    _merge_groups(o_ref, [acc_ref[r] / jnp.maximum(l_ref[r], 1e-30) for r in tiles])


def nsa_prompt_attention(q, q_rot, kc, vc, k_slc, v_slc, k_win, v_win):
    n, T, D = q.shape
    tq = Q_BLOCK
    nc = kc.shape[1]
    qspec = pl.BlockSpec((1, tq, D), lambda b, i: (b, i, 0))
    whole = lambda a: pl.BlockSpec((1,) + a.shape[1:], lambda b, i: (b, 0, 0))
    n_sel = NSA_KV_GROUPS * SEL_LANES
    sel_spec = pl.BlockSpec((1, tq, n_sel), lambda b, i: (b, i, 0))
    params = pltpu.CompilerParams(dimension_semantics=("parallel", "arbitrary"),
                                  vmem_limit_bytes=48 * 1024 * 1024)
    qs_scratch = pltpu.VMEM((NSA_TILES, tq, SLAB), jnp.bfloat16)
    o_cmp, sel = pl.pallas_call(
        _nsa_cmp_kernel,
        grid=(n, T // tq),
        in_specs=[qspec, whole(kc), whole(vc)],
        out_specs=[qspec, sel_spec],
        out_shape=[jax.ShapeDtypeStruct((n, T, D), jnp.float32),
                   jax.ShapeDtypeStruct((n, T, n_sel), jnp.bfloat16)],
        scratch_shapes=[qs_scratch],
        compiler_params=params,
        name="nsa_cmp_select",
    )(q, kc, vc)
    stat = pltpu.VMEM((NSA_TILES, tq, KEY_BLOCK), jnp.float32)

    def dense(windowed, k, v, name):
        return pl.pallas_call(
            functools.partial(_nsa_dense_kernel, windowed=windowed),
            grid=(n, T // tq),
            in_specs=[qspec, whole(k), whole(v), sel_spec],
            out_specs=qspec,
            out_shape=jax.ShapeDtypeStruct((n, T, D), jnp.float32),
            scratch_shapes=[qs_scratch, stat, stat, stat],
            compiler_params=params,
            name=name,
        )(q_rot, k, v, sel)

    return o_cmp, dense(False, k_slc, v_slc, "nsa_selected"), dense(True, k_win, v_win, "nsa_window")


def _to_slabs(x):
    lead = x.shape[:-1]
    x = x.reshape(lead + (NSA_KV_GROUPS, NSA_GROUP_HEADS, NSA_HEAD_DIM))
    return jnp.swapaxes(x, -3, -2).reshape(lead + (NSA_HEADS * NSA_HEAD_DIM,))


def _from_slabs(x):
    lead = x.shape[:-1]
    x = x.reshape(lead + (NSA_GROUP_HEADS, NSA_KV_GROUPS, NSA_HEAD_DIM))
    return jnp.swapaxes(x, -3, -2).reshape(lead + (NSA_HEADS * NSA_HEAD_DIM,))


def prompt_nsa(h, ctx, w_in, w_o):
    kc, vc, slc_kv, win_kv = ctx
    n, T, _ = h.shape
    q, gate = nsa_query(h, w_in)
    t_pos = jnp.arange(T)
    scale = NSA_HEAD_DIM ** -0.5
    flat = lambda a: a.reshape(n, a.shape[1], -1)
    bf = lambda a: flat(a).astype(jnp.bfloat16)
    o_cmp, o_slc, o_win = nsa_prompt_attention(
        _to_slabs(flat(q * scale)), _to_slabs(flat(rope_partial(q, t_pos) * scale)),
        bf(kc), bf(vc), bf(slc_kv[:, :, 0]), bf(slc_kv[:, :, 1]), bf(win_kv[:, :, 0]), bf(win_kv[:, :, 1]))
    heads = lambda o: _from_slabs(o).reshape(q.shape)
    g = jax.nn.sigmoid(gate.astype(jnp.float32)).astype(q.dtype)
    o = g[..., 0:1] * heads(o_cmp) + g[..., 1:2] * heads(o_slc) + g[..., 2:3] * heads(o_win)
    return o.reshape(n, T, NSA_HEADS * NSA_HEAD_DIM) @ w_o


def _prompt_kv_context_jax(cmp_kv, slc_kv, win_kv, cmp_pos, cmp_w1, cmp_b1, cmp_w2):
    n, T = cmp_kv.shape[:2]
    kc = compress_rows(cmp_kv[:, :, 0], cmp_pos[0], cmp_w1[0], cmp_b1[0], cmp_w2[0])
    vc = compress_rows(cmp_kv[:, :, 1], cmp_pos[1], cmp_w1[1], cmp_b1[1], cmp_w2[1])
    n_s = T // SLC_BLOCK
    to_blocks = lambda a: a.reshape(n, n_s, SLC_BLOCK, NSA_KV_GROUPS, NSA_HEAD_DIM).transpose(0, 3, 1, 2, 4)
    win_pad = jnp.pad(win_kv, ((0, 0), (WINDOW, 0), (0, 0), (0, 0), (0, 0)))
    return kc, vc, to_blocks(slc_kv[:, :, 0]), to_blocks(slc_kv[:, :, 1]), win_pad


def _prompt_nsa_jax(h, ctx, w_in, w_o):
    kc, vc, sel_k, sel_v, win_pad = ctx
    n, T, _ = h.shape
    q, gate = nsa_query(h, w_in)
    n_qb = T // Q_BLOCK
    blocks = lambda a: a.reshape((n, n_qb, Q_BLOCK) + a.shape[2:]).swapaxes(0, 1)
    bi = jnp.arange(n)[:, None, None, None]
    gi = jnp.arange(NSA_KV_GROUPS)[None, :, None, None]
    gather_sel = lambda idx: (sel_k[bi, gi, idx], sel_v[bi, gi, idx])

    def one_block(args):
        qi, q_blk, g_blk = args
        t0 = qi * Q_BLOCK
        t_pos = t0 + jnp.arange(Q_BLOCK)
        band = lax.dynamic_slice_in_dim(win_pad, t0, WINDOW + Q_BLOCK, axis=1)
        s_pos = t0 - WINDOW + jnp.arange(WINDOW + Q_BLOCK)
        return nsa_branches(q_blk, g_blk, t_pos, kc, vc, gather_sel, band[:, :, 0], band[:, :, 1], s_pos)

    o = lax.map(one_block, (jnp.arange(n_qb), blocks(q), blocks(gate)))
    return o.swapaxes(0, 1).reshape(n, T, NSA_HEADS * NSA_HEAD_DIM) @ w_o


def sample_kv_context(cmp_new, slc_new, win_new, cache_cmp_kv, cache_slc_kv, cache_win_kv, page_table,
                      cmp_pos, cmp_w1, cmp_b1, cmp_w2):
    n, S = cmp_new.shape[:2]
    n_new_blk = -(-S // SLC_BLOCK)
    pad = ((0, 0), (0, n_new_blk * SLC_BLOCK - S), (0, 0), (0, 0), (0, 0))
    past = cache_cmp_kv[page_table].reshape((n, PAST_LEN) + cache_cmp_kv.shape[2:])
    rows = jnp.concatenate([past, jnp.pad(cmp_new.astype(past.dtype), pad)], axis=1)
    kc = compress_rows(rows[:, :, 0], cmp_pos[0], cmp_w1[0], cmp_b1[0], cmp_w2[0])
    vc = compress_rows(rows[:, :, 1], cmp_pos[1], cmp_w1[1], cmp_b1[1], cmp_w2[1])
    pool = cache_slc_kv.reshape((-1, PAGE_SIZE // SLC_BLOCK, SLC_BLOCK) + cache_slc_kv.shape[2:])
    new_blocks = jnp.pad(slc_new, pad).reshape((n, n_new_blk, SLC_BLOCK) + slc_new.shape[2:])
    win_all = jnp.concatenate([cache_win_kv, win_new.astype(cache_win_kv.dtype)], axis=1)
    return kc, vc, pool, new_blocks, win_all


def sample_nsa(h, ctx, page_table, w_in, w_o):
    kc, vc, pool, new_blocks, win_all = ctx
    n, S, _ = h.shape
    q, gate = nsa_query(h, w_in)
    t_pos = PAST_LEN + jnp.arange(S)
    sub = PAGE_SIZE // SLC_BLOCK
    n_past_blk = PAST_LEN // SLC_BLOCK
    n_new_blk = new_blocks.shape[1]
    bi = jnp.arange(n)[:, None, None, None]
    gi = jnp.arange(NSA_KV_GROUPS)[None, :, None, None]

    def gather_sel(idx):
        jp = jnp.minimum(idx, n_past_blk - 1)
        phys = page_table[bi, jp // sub]
        from_past = pool[phys, jp % sub, :, :, gi]
        jn = jnp.clip(idx - n_past_blk, 0, n_new_blk - 1)
        from_new = new_blocks[bi, jn, :, :, gi].astype(from_past.dtype)
        blk = jnp.where((idx >= n_past_blk)[..., None, None, None], from_new, from_past)
        return blk[..., 0, :], blk[..., 1, :]

    wb = win_all.shape[1] - S
    s_pos = PAST_LEN - wb + jnp.arange(win_all.shape[1])
    o = nsa_branches(q, gate, t_pos, kc, vc, gather_sel, win_all[:, :, 0], win_all[:, :, 1], s_pos)
    return o.reshape(n, S, NSA_HEADS * NSA_HEAD_DIM) @ w_o


def kernel(x_prompt, x_sample, state_wkv, state_shift, cache_cmp_kv, cache_slc_kv, cache_win_kv, page_table, norm_mix, norm_ffn, norm_kv, norm_final, rw_mu, rw_w_rkv, rw_w0, rw_w1, rw_w2, rw_a0, rw_a1, rw_a2, rw_g1, rw_g2, rw_k_k, rw_k_a, rw_r_k, rw_gn_w, rw_gn_b, rw_w_o, nsa_w_kv, nsa_cmp_pos, nsa_cmp_w1, nsa_cmp_b1, nsa_cmp_w2, nsa_w_in, nsa_w_o, ffn_w_gu, ffn_w_down, moe_router, moe_w_gu, moe_w_down):
    cmp_params = (nsa_cmp_pos, nsa_cmp_w1, nsa_cmp_b1, nsa_cmp_w2)

    def trunk(x, pos, shift0, wkv0, make_context, attend):
        wkv_out, shift_out = [], []
        rows, ctx = None, None
        for li in range(DEPTH):
            h = rmsnorm(x, norm_mix[li])
            if li < N_A_LAYERS:
                y, s_fin, h_last = rwkv7_time_mix(
                    h, shift0[li], wkv0[li], rw_mu[li], rw_w_rkv[li], rw_w0[li], rw_w1[li], rw_w2[li],
                    rw_a0[li], rw_a1[li], rw_a2[li], rw_g1[li], rw_g2[li], rw_k_k[li], rw_k_a[li],
                    rw_r_k[li], rw_gn_w[li], rw_gn_b[li], rw_w_o[li])
                wkv_out.append(s_fin)
                shift_out.append(h_last)
            else:
                b = li - N_A_LAYERS
                y = attend(h, ctx, nsa_w_in[b], nsa_w_o[b])
            x = x + y
            h = rmsnorm(x, norm_ffn[li])
            if li % 2 == 0:
                x = x + swiglu(h, ffn_w_gu[li // 2], ffn_w_down[li // 2])
            else:
                x = x + moe_swiglu(h, moe_router[li // 2], moe_w_gu[li // 2], moe_w_down[li // 2])
            if li == N_A_LAYERS - 1:
                rows = shared_kv_rows(x, norm_kv, nsa_w_kv, pos)
                ctx = make_context(*rows)
        return rmsnorm(x, norm_final), jnp.stack(wkv_out), jnp.stack(shift_out), rows, ctx

    n_p = x_prompt.shape[0]
    pos_p = jnp.arange(x_prompt.shape[1], dtype=jnp.int32)
    shift0_p = jnp.zeros((N_A_LAYERS, n_p, D_MODEL), x_prompt.dtype)
    wkv0_p = jnp.zeros((N_A_LAYERS, n_p, RWKV_HEADS, RWKV_HEAD, RWKV_HEAD), jnp.float32)
    y_prompt, wkv_p, shift_p, rows_p, _ = trunk(
        x_prompt, pos_p, shift0_p, wkv0_p,
        lambda c, s, w: prompt_kv_context(c, s, w, *cmp_params),
        prompt_nsa)
    cmp_kv_p, slc_kv_p, win_rows_p = rows_p
    win_kv_p = win_rows_p[:, -min(WINDOW, x_prompt.shape[1]):]

    pos_s = PAST_LEN + jnp.arange(x_sample.shape[1], dtype=jnp.int32)
    y_sample, wkv_s, shift_s, rows_s, ctx_s = trunk(
        x_sample, pos_s, state_shift, state_wkv,
        lambda c, s, w: sample_kv_context(c, s, w, cache_cmp_kv, cache_slc_kv, cache_win_kv, page_table, *cmp_params),
        lambda h, ctx, wi, wo: sample_nsa(h, ctx, page_table, wi, wo))
    cmp_kv_s, slc_kv_s, _ = rows_s
    win_kv_s = ctx_s[4][:, -cache_win_kv.shape[1]:]

    return (y_prompt, y_sample, wkv_p, shift_p, cmp_kv_p, slc_kv_p, win_kv_p,
            wkv_s, shift_s, cmp_kv_s, slc_kv_s, win_kv_s)
```

```python
import functools
import math

import jax
import jax.numpy as jnp
from jax import lax
from jax.experimental import pallas as pl
from jax.experimental.pallas import tpu as pltpu

D_MODEL = 1024
DEPTH = 2
PAST_LEN = 16384
PAGE_SIZE = 128
N_A_LAYERS = DEPTH // 2
RWKV_HEAD = 64
RWKV_HEADS = D_MODEL // RWKV_HEAD
DECAY_SCALE = math.exp(-0.5)
GN_EPS = RWKV_HEAD * 1e-5
NSA_HEADS = 16
NSA_HEAD_DIM = 64
NSA_KV_GROUPS = 2
NSA_GROUP_HEADS = NSA_HEADS // NSA_KV_GROUPS
N_BRANCH = 3
CMP_BLOCK = 32
CMP_STRIDE = 16
CMP_HIDDEN = 128
SLC_BLOCK = 64
SLC_TOPN = 16
WINDOW = 512
Q_BLOCK = 128
FORCED_SCORE = 1e4
ROPE_THETA = 500000.0
ROT_DIM = NSA_HEAD_DIM // 4
N_EXPERTS = 8
TOP_K = 2
MOE_BLOCK = 256
NORM_EPS = 1e-6


def _rmsnorm_kernel(x_ref, g_ref, o_ref):
    x = x_ref[...]
    y = x * lax.rsqrt(jnp.mean(x * x, axis=-1, keepdims=True) + NORM_EPS)
    o_ref[...] = y * g_ref[...]


def rmsnorm(x, g):
    shp = x.shape
    x2 = x.reshape(-1, shp[-1])
    rows = x2.shape[0]
    tm = min(rows, 512)
    out = pl.pallas_call(
        _rmsnorm_kernel,
        grid=(rows // tm,),
        in_specs=[pl.BlockSpec((tm, shp[-1]), lambda i: (i, 0)),
                  pl.BlockSpec((1, shp[-1]), lambda i: (0, 0))],
        out_specs=pl.BlockSpec((tm, shp[-1]), lambda i: (i, 0)),
        out_shape=jax.ShapeDtypeStruct(x2.shape, x.dtype),
        name="rmsnorm",
    )(x2, g.reshape(1, -1))
    return out.reshape(shp)


def rope_partial(x, pos):
    half = ROT_DIM // 2
    inv = ROPE_THETA ** (-2.0 * jnp.arange(half, dtype=jnp.float32) / ROT_DIM)
    ang = pos.astype(jnp.float32)[:, None] * inv[None, :]
    shape = (1, pos.shape[0]) + (1,) * (x.ndim - 3) + (half,)
    cos = jnp.cos(ang).reshape(shape)
    sin = jnp.sin(ang).reshape(shape)
    xf = x.astype(jnp.float32)
    x1, x2 = xf[..., :half], xf[..., half:ROT_DIM]
    out = jnp.concatenate([x1 * cos - x2 * sin, x2 * cos + x1 * sin, xf[..., ROT_DIM:]], axis=-1)
    return out.astype(x.dtype)


def masked_softmax(s, mask):
    s = jnp.where(mask, s.astype(jnp.float32), -jnp.inf)
    m = jnp.max(s, axis=-1, keepdims=True)
    m = jnp.where(jnp.isfinite(m), m, 0.0)
    e = jnp.where(mask, jnp.exp(s - m), 0.0)
    return e / jnp.maximum(jnp.sum(e, axis=-1, keepdims=True), 1e-30)


FF_CHUNK = 1408
SWIGLU_ROWS = 512


def _swiglu_kernel(blk_e_ref, n_used_ref, x_ref, wg_ref, wu_ref, wd_ref, o_ref):
    i = pl.program_id(0)
    f = pl.program_id(1)

    @pl.when(i < n_used_ref[0])
    def _():
        x = x_ref[...]
        g = jnp.dot(x, wg_ref[0], preferred_element_type=jnp.float32)
        u = jnp.dot(x, wu_ref[0], preferred_element_type=jnp.float32)
        act = (g * jax.nn.sigmoid(g) * u).astype(jnp.bfloat16)
        y = jnp.dot(act, wd_ref[0], preferred_element_type=jnp.float32)

        @pl.when(f == 0)
        def _():
            o_ref[...] = y

        @pl.when(f > 0)
        def _():
            o_ref[...] += y

    @pl.when(i >= n_used_ref[0])
    def _():
        o_ref[...] = jnp.zeros(o_ref.shape, o_ref.dtype)


def grouped_swiglu(xb, blk_e, n_used, w_gu, w_down):
    rows, d = xb.shape
    b = min(SWIGLU_ROWS, rows)
    n_blk = rows // b
    ff = w_down.shape[1]
    tf = FF_CHUNK
    n_f = ff // tf
    chunk = lambda i, f, be, nu: jnp.where(i < nu[0], f, n_f - 1)
    return pl.pallas_call(
        _swiglu_kernel,
        grid_spec=pltpu.PrefetchScalarGridSpec(
            num_scalar_prefetch=2,
            grid=(n_blk, n_f),
            in_specs=[
                pl.BlockSpec((b, d), lambda i, f, be, nu: (i, 0)),
                pl.BlockSpec((1, d, tf), lambda i, f, be, nu: (be[i], 0, chunk(i, f, be, nu))),
                pl.BlockSpec((1, d, tf), lambda i, f, be, nu: (be[i], 0, n_f + chunk(i, f, be, nu))),
                pl.BlockSpec((1, tf, d), lambda i, f, be, nu: (be[i], chunk(i, f, be, nu), 0)),
            ],
            out_specs=pl.BlockSpec((b, d), lambda i, f, be, nu: (i, 0)),
        ),
        out_shape=jax.ShapeDtypeStruct((rows, d), jnp.float32),
        compiler_params=pltpu.CompilerParams(dimension_semantics=("arbitrary", "arbitrary"),
                                             vmem_limit_bytes=56 * 1024 * 1024),
        name="grouped_swiglu",
    )(blk_e.astype(jnp.int32), jnp.reshape(n_used, (1,)).astype(jnp.int32), xb, w_gu, w_gu, w_down)


def swiglu(h, w_gu, w_down, precise=False):
    if precise:
        g, u = jnp.split(dense(h, w_gu, True), 2, axis=-1)
        return dense(jax.nn.silu(g) * u, w_down, True)
    shp = h.shape
    xt = h.reshape(-1, shp[-1]).astype(jnp.bfloat16)
    n_blk = xt.shape[0] // min(SWIGLU_ROWS, xt.shape[0])
    y = grouped_swiglu(xt, jnp.zeros((n_blk,), jnp.int32), jnp.int32(n_blk),
                       w_gu.astype(jnp.bfloat16)[None], w_down.astype(jnp.bfloat16)[None])
    return y.reshape(shp)


def moe_swiglu(xt, logits, w_gu_e, w_down_e):
    n_tok, d = xt.shape
    b = SWIGLU_ROWS
    top_val, top_idx = lax.top_k(logits, TOP_K)
    gate = jax.nn.softmax(top_val, axis=-1).astype(xt.dtype)
    nk = n_tok * TOP_K
    flat_e = top_idx.reshape(nk)
    onehot = (flat_e[:, None] == jnp.arange(N_EXPERTS)[None, :]).astype(jnp.int32)
    before = jnp.cumsum(onehot, axis=0) - onehot
    counts = jnp.sum(onehot, axis=0)
    padded = (counts + b - 1) // b * b
    ends_pad = jnp.cumsum(padded)
    starts_pad = ends_pad - padded
    dest = jnp.sum(onehot * (starts_pad[None, :] + before), axis=1)
    n_rows = (nk + b - 1) // b * b + N_EXPERTS * b
    n_blk = n_rows // b
    flat_tok = jnp.repeat(jnp.arange(n_tok, dtype=jnp.int32), TOP_K)
    row_tok = jnp.full((n_rows,), n_tok, jnp.int32).at[dest].set(flat_tok)
    blk_e = jnp.minimum(jnp.searchsorted(ends_pad, jnp.arange(n_blk) * b, side='right'), N_EXPERTS - 1)
    x_pad = jnp.concatenate([xt.astype(jnp.bfloat16), jnp.zeros((1, d), jnp.bfloat16)], axis=0)
    yb = grouped_swiglu(x_pad[row_tok], blk_e, ends_pad[-1] // b,
                        w_gu_e.astype(jnp.bfloat16), w_down_e.astype(jnp.bfloat16))
    y_tok = yb[dest].reshape(n_tok, TOP_K, d)
    return jnp.sum(y_tok * gate[:, :, None], axis=1)


WKV_CHUNK = 64
WKV_HEADS_PER_STEP = 16

_NN = (((1,), (0,)), ((), ()))
_NT = (((1,), (1,)), ((), ()))
_TN = (((0,), (0,)), ((), ()))


def _mm(x, y, dims):
    return lax.dot_general(x.astype(jnp.bfloat16), y.astype(jnp.bfloat16), dims,
                           preferred_element_type=jnp.float32)


def _wkv7_chunk_kernel(r_ref, lw_ref, k_ref, v_ref, a_ref, b_ref, s0_ref, y_ref, sout_ref, state_ref):
    c = pl.program_id(2)
    L = r_ref.shape[1]
    hb = state_ref.shape[0]
    N = RWKV_HEAD

    @pl.when(c == 0)
    def _():
        state_ref[...] = s0_ref[0]

    row = lax.broadcasted_iota(jnp.int32, (L, L), 0)
    col = lax.broadcasted_iota(jnp.int32, (L, L), 1)
    strict = row > col
    incl = row >= col
    tri = incl.astype(jnp.bfloat16)
    eye_n = lax.broadcasted_iota(jnp.int32, (N, N), 0) == lax.broadcasted_iota(jnp.int32, (N, N), 1)

    lw = lw_ref[0]
    lw_hi = lw.astype(jnp.bfloat16)
    lw_lo = (lw - lw_hi.astype(jnp.float32)).astype(jnp.bfloat16)
    cum = (lax.dot_general(tri, lw_hi, _NN, preferred_element_type=jnp.float32)
           + lax.dot_general(tri, lw_lo, _NN, preferred_element_type=jnp.float32))
    cum_last = cum[L - 1:L, :]
    e_neg = jnp.exp(-cum)
    e_tail = jnp.exp(cum_last - cum)
    a_all = a_ref[0] * jnp.exp(cum - lw)
    r_all = r_ref[0] * jnp.exp(cum)
    b_in = b_ref[0]
    k_in = k_ref[0]
    bt_all = b_in * e_neg
    kt_all = k_in * e_neg
    bh_all = b_in * e_tail
    kh_all = k_in * e_tail
    wl_all = jnp.exp(cum_last)
    v_all = v_ref[0]

    heads = range(hb)
    per_head = lambda t: [t[:, j * N:(j + 1) * N] for j in heads]
    at, rt, bt, kt, bh, kh, v, wl = (per_head(t) for t in
                                     (a_all, r_all, bt_all, kt_all, bh_all, kh_all, v_all, wl_all))
    a_ab = [jnp.where(strict, _mm(at[j], bt[j], _NT), 0.0) for j in heads]
    a_ak = [jnp.where(strict, _mm(at[j], kt[j], _NT), 0.0) for j in heads]
    r_b = [jnp.where(incl, _mm(rt[j], bt[j], _NT), 0.0) for j in heads]
    r_k = [jnp.where(incl, _mm(rt[j], kt[j], _NT), 0.0) for j in heads]
    av = [_mm(a_ak[j], v[j], _NN) for j in heads]
    ht = [_mm(v[j], kh[j], _TN) for j in heads]
    yp = [_mm(r_k[j], v[j], _NN) for j in heads]
    pw = a_ab
    inv_a = a_ab
    n = 1
    while 2 * n < L:
        pw = [_mm(pw[j], pw[j], _NN) for j in heads]
        inv_a = [inv_a[j] + pw[j] + _mm(inv_a[j], pw[j], _NN) for j in heads]
        n *= 2
    ap = [at[j] + _mm(inv_a[j], at[j], _NN) for j in heads]
    vp = [av[j] + _mm(inv_a[j], av[j], _NN) for j in heads]
    g = [jnp.where(eye_n, wl[j], 0.0) + _mm(bh[j], ap[j], _TN) for j in heads]
    ht = [ht[j] + _mm(vp[j], bh[j], _TN) for j in heads]
    rp = [rt[j] + _mm(r_b[j], ap[j], _NN) for j in heads]
    yp = [yp[j] + _mm(r_b[j], vp[j], _NN) for j in heads]
    s_prev = [state_ref[j] for j in heads]
    y_ref[0] = jnp.concatenate([_mm(rp[j], s_prev[j], _NT) + yp[j] for j in heads], axis=-1)
    for j in heads:
        state_ref[j] = _mm(s_prev[j], g[j], _NT) + ht[j]

    @pl.when(c == pl.num_programs(2) - 1)
    def _():
        sout_ref[0] = state_ref[...]


def wkv7_chunked(r, lw, k, v, a_vec, b_vec, s0):
    n, T, D = r.shape
    L = WKV_CHUNK
    hb = WKV_HEADS_PER_STEP
    w = hb * RWKV_HEAD
    seq = pl.BlockSpec((1, L, w), lambda b, h, c: (b, c, h))
    st = pl.BlockSpec((1, hb, RWKV_HEAD, RWKV_HEAD), lambda b, h, c: (b, h, 0, 0))
    return pl.pallas_call(
        _wkv7_chunk_kernel,
        grid=(n, D // w, T // L),
        in_specs=[seq] * 6 + [st],
        out_specs=[seq, st],
        out_shape=[jax.ShapeDtypeStruct((n, T, D), jnp.float32),
                   jax.ShapeDtypeStruct(s0.shape, jnp.float32)],
        scratch_shapes=[pltpu.VMEM((hb, RWKV_HEAD, RWKV_HEAD), jnp.float32)],
        compiler_params=pltpu.CompilerParams(dimension_semantics=("parallel", "parallel", "arbitrary")),
        name="wkv7_chunked",
    )(r, lw, k, v, a_vec, b_vec, s0)


def wkv7_scan(r, lw, k, v, a_vec, b_vec, s0):
    def step(S, inp):
        r_t, lw_t, k_t, v_t, a_t, b_t = inp
        sa = jnp.sum(S * a_t[:, :, None, :], axis=-1)
        S = S * jnp.exp(lw_t)[:, :, None, :] + sa[..., None] * b_t[:, :, None, :] + v_t[..., None] * k_t[:, :, None, :]
        return S, jnp.sum(S * r_t[:, :, None, :], axis=-1)

    xs = tuple(jnp.moveaxis(t, 1, 0) for t in (r, lw, k, v, a_vec, b_vec))
    s_fin, ys = lax.scan(step, s0, xs)
    return jnp.moveaxis(ys, 0, 1), s_fin


def rwkv7_time_mix(h, h_prev, s0, mu, w_rkv, w0, w1, w2, a0, a1, a2, g1, g2, k_k, k_a, r_k, gn_w, gn_b, w_o,
                   precise=False):
    n, T, D = h.shape
    f32 = jnp.float32
    mm = functools.partial(dense, precise=precise)
    dx = jnp.concatenate([h_prev[:, None, :].astype(h.dtype), h[:, :-1]], axis=1) - h
    xr, xw, xk, xv, xa, xg = (h + dx * mu[i] for i in range(6))
    r = mm(xr, w_rkv[0])
    k = mm(xk, w_rkv[1])
    v = mm(xv, w_rkv[2])
    log_decay = -DECAY_SCALE * jax.nn.sigmoid((w0 + mm(jnp.tanh(mm(xw, w1)), w2)).astype(f32))
    a = jax.nn.sigmoid((a0 + mm(mm(xa, a1), a2)).astype(f32))
    g = mm(jax.nn.sigmoid(mm(xg, g1)), g2)
    heads = lambda t: t.astype(f32).reshape(n, T, RWKV_HEADS, RWKV_HEAD)
    kk = heads(k * k_k)
    kk = kk / jnp.maximum(jnp.sqrt(jnp.sum(kk * kk, axis=-1, keepdims=True)), 1e-12)
    a_h = heads(a)
    k_h = heads(k.astype(f32) * (1.0 + (a - 1.0) * k_a.astype(f32)))
    r_h, v_h = heads(r), heads(v)
    if T % WKV_CHUNK == 0:
        flat = lambda t: t.reshape(n, T, D)
        y, s_fin = wkv7_chunked(flat(r_h), log_decay, flat(k_h), flat(v_h), flat(-kk), flat(kk * a_h),
                                s0.astype(f32))
        y = heads(y)
    else:
        y, s_fin = wkv7_scan(r_h, heads(log_decay), k_h, v_h, -kk, kk * a_h, s0.astype(f32))
    mean = jnp.mean(y, axis=-1, keepdims=True)
    var = jnp.mean(jnp.square(y - mean), axis=-1, keepdims=True)
    y = ((y - mean) * lax.rsqrt(var + GN_EPS)).reshape(n, T, D) * gn_w.astype(f32) + gn_b.astype(f32)
    bonus = jnp.sum(r_h * k_h * r_k.astype(f32), axis=-1, keepdims=True) * v_h
    y = (y + bonus.reshape(n, T, D)).astype(h.dtype)
    return mm(y * g, w_o), s_fin, h[:, -1]


def shared_kv_rows(x, norm_kv, w_kv, pos, precise=False):
    n, T, _ = x.shape
    kv = dense(rmsnorm(x, norm_kv), w_kv, precise).reshape(n, T, N_BRANCH, 2, NSA_KV_GROUPS, NSA_HEAD_DIM)
    cmp_kv = kv[:, :, 0]
    slc_kv = jnp.stack([rope_partial(kv[:, :, 1, 0], pos), kv[:, :, 1, 1]], axis=2)
    win_kv = jnp.stack([rope_partial(kv[:, :, 2, 0], pos), kv[:, :, 2, 1]], axis=2)
    return cmp_kv, slc_kv, win_kv


CHUNK_LANES = CMP_STRIDE * 2 * NSA_KV_GROUPS * NSA_HEAD_DIM
HIDDEN_LANES = 2 * NSA_KV_GROUPS * 2 * CMP_HIDDEN
PAGES_PER_STEP = 32
CHUNKS_PER_PAGE = PAGE_SIZE // CMP_STRIDE


def _chunk_weights(cmp_w1):
    w = cmp_w1.reshape(2, 2, CMP_STRIDE, NSA_HEAD_DIM, CMP_HIDDEN)
    w = w.transpose(2, 0, 3, 1, 4)
    eye = jnp.eye(2, dtype=w.dtype)
    big = jnp.einsum('ab,cf,jaehk->jacebfhk', eye, jnp.eye(NSA_KV_GROUPS, dtype=w.dtype), w)
    return big.reshape(CHUNK_LANES, HIDDEN_LANES).astype(jnp.bfloat16)


def _paged_chunk_kernel(pt_ref, *refs):
    pages, w_ref, o_ref = refs[:PAGES_PER_STEP], refs[PAGES_PER_STEP], refs[PAGES_PER_STEP + 1]
    x = jnp.concatenate([p[0] for p in pages], axis=0).astype(jnp.bfloat16)
    o_ref[0] = jnp.dot(x, w_ref[...], preferred_element_type=jnp.float32)


def paged_chunk_hidden(cache, page_table, w_chunk):
    n, n_pages = page_table.shape
    pages = cache.reshape(cache.shape[0], CHUNKS_PER_PAGE, CHUNK_LANES)
    rows = PAGES_PER_STEP * CHUNKS_PER_PAGE

    def page_spec(k):
        return pl.BlockSpec((1, CHUNKS_PER_PAGE, CHUNK_LANES),
                            lambda b, s, pt: (pt[b, s * PAGES_PER_STEP + k], 0, 0))

    return pl.pallas_call(
        _paged_chunk_kernel,
        grid_spec=pltpu.PrefetchScalarGridSpec(
            num_scalar_prefetch=1,
            grid=(n, n_pages // PAGES_PER_STEP),
            in_specs=[page_spec(k) for k in range(PAGES_PER_STEP)]
                     + [pl.BlockSpec((CHUNK_LANES, HIDDEN_LANES), lambda b, s, pt: (0, 0))],
            out_specs=pl.BlockSpec((1, rows, HIDDEN_LANES), lambda b, s, pt: (b, s, 0)),
        ),
        out_shape=jax.ShapeDtypeStruct((n, n_pages * CHUNKS_PER_PAGE, HIDDEN_LANES), jnp.float32),
        compiler_params=pltpu.CompilerParams(dimension_semantics=("parallel", "arbitrary"),
                                             vmem_limit_bytes=48 * 1024 * 1024),
        name="paged_chunk_hidden",
    )(page_table.astype(jnp.int32), *([pages] * PAGES_PER_STEP), w_chunk)


def _matmul_kernel(x_ref, w_ref, o_ref):
    o_ref[...] = jnp.dot(x_ref[...].astype(jnp.bfloat16), w_ref[...],
                         preferred_element_type=jnp.float32).astype(o_ref.dtype)


def matmul(x, w, rows_per_step=512, out_dtype=jnp.float32):
    m, k = x.shape
    nn = w.shape[1]
    tm = min(rows_per_step, m)
    return pl.pallas_call(
        _matmul_kernel,
        grid=(m // tm,),
        in_specs=[pl.BlockSpec((tm, k), lambda i: (i, 0)), pl.BlockSpec((k, nn), lambda i: (0, 0))],
        out_specs=pl.BlockSpec((tm, nn), lambda i: (i, 0)),
        out_shape=jax.ShapeDtypeStruct((m, nn), out_dtype),
        compiler_params=pltpu.CompilerParams(dimension_semantics=("parallel",),
                                             vmem_limit_bytes=48 * 1024 * 1024),
        name="matmul",
    )(x, w.astype(jnp.bfloat16))


def _split_bf16(x):
    hi = x.astype(jnp.bfloat16)
    return hi, (x - hi.astype(jnp.float32)).astype(jnp.bfloat16)


def _matmul3_kernel(x_ref, w_ref, o_ref):
    x_hi, x_lo = _split_bf16(x_ref[...])
    w_hi, w_lo = _split_bf16(w_ref[...])
    dot = functools.partial(jnp.dot, preferred_element_type=jnp.float32)
    o_ref[...] = dot(x_hi, w_hi) + (dot(x_hi, w_lo) + dot(x_lo, w_hi))


def matmul_precise(x, w, cols_per_step=512):
    m, k = x.shape
    nn = w.shape[1]
    pad = (-nn) % 128
    if pad:
        w = jnp.pad(w, ((0, 0), (0, pad)))
    tn = math.gcd(cols_per_step, nn + pad)
    tm = min(m, 512)
    out = pl.pallas_call(
        _matmul3_kernel,
        grid=(m // tm, (nn + pad) // tn),
        in_specs=[pl.BlockSpec((tm, k), lambda i, j: (i, 0)), pl.BlockSpec((k, tn), lambda i, j: (0, j))],
        out_specs=pl.BlockSpec((tm, tn), lambda i, j: (i, j)),
        out_shape=jax.ShapeDtypeStruct((m, nn + pad), jnp.float32),
        compiler_params=pltpu.CompilerParams(dimension_semantics=("parallel", "parallel")),
        name="matmul_precise",
    )(x, w)
    return out[:, :nn] if pad else out


def dense(x, w, precise=False):
    if precise:
        return matmul_precise(x.reshape(-1, x.shape[-1]), w).reshape(x.shape[:-1] + (w.shape[1],))
    return x @ w


def compress_from_hidden(hidden, pos_emb, w1, b1, w2):
    n, C = hidden.shape[:2]
    hid = hidden.reshape(n, C, 2, NSA_KV_GROUPS, 2, CMP_HIDDEN)
    first, second = hid[:, :, :, :, 0], hid[:, :, :, :, 1]
    second = jnp.concatenate([second[:, 1:], jnp.zeros_like(second[:, :1])], axis=1)
    w1r = w1.reshape(2, CMP_BLOCK, NSA_HEAD_DIM, CMP_HIDDEN)
    bias = jnp.einsum('ajd,ajdh->ah', pos_emb, w1r) + b1
    act = jax.nn.gelu(first + second + bias[None, None, :, None, :])
    out = jnp.einsum('ncagh,ahd->ncagd', act, w2)
    return out[:, :, 0], out[:, :, 1]


def rows_chunk_hidden(kv_rows, w_chunk):
    n, T = kv_rows.shape[:2]
    chunks = kv_rows.reshape(n * T // CMP_STRIDE, CHUNK_LANES)
    return matmul(chunks, w_chunk, rows_per_step=256).reshape(n, T // CMP_STRIDE, HIDDEN_LANES)


def nsa_query(h, w_in, precise=False):
    n, T, _ = h.shape
    proj = dense(h, w_in, precise)
    q = proj[..., :NSA_HEADS * NSA_HEAD_DIM].reshape(n, T, NSA_KV_GROUPS, NSA_GROUP_HEADS, NSA_HEAD_DIM)
    gate = proj[..., NSA_HEADS * NSA_HEAD_DIM:].reshape(n, T, NSA_KV_GROUPS, NSA_GROUP_HEADS, N_BRANCH)
    return q, gate


def nsa_branches(q, gate, t_pos, kc, vc, gather_sel, k_w, v_w, s_pos):
    scale = NSA_HEAD_DIM ** -0.5
    q_rot = rope_partial(q, t_pos)
    n_c = kc.shape[1]
    vis = (jnp.arange(n_c) * CMP_STRIDE + CMP_BLOCK - 1)[None, :] <= t_pos[:, None]
    p_cmp = masked_softmax(jnp.einsum('nqghd,ncgd->nghqc', q, kc) * scale, vis)
    o_cmp = jnp.einsum('nghqc,ncgd->nqghd', p_cmp.astype(vc.dtype), vc)
    ratio = SLC_BLOCK // CMP_STRIDE
    lead = CMP_BLOCK // CMP_STRIDE - 1
    n_s = n_c // ratio
    pg = jnp.pad(jnp.sum(p_cmp, axis=2), ((0, 0), (0, 0), (0, 0), (lead, 0)))
    p_slc = pg[..., 0:ratio * n_s:ratio]
    for o in range(1, ratio + lead):
        p_slc = p_slc + pg[..., o:o + ratio * n_s:ratio]
    jb = jnp.arange(n_s)[None, :]
    jt = (t_pos // SLC_BLOCK)[:, None]
    forced = (jb == 0) | (jb == jt) | (jb == jt - 1)
    score = jnp.where(jb > jt, -jnp.inf, jnp.where(forced, FORCED_SCORE, p_slc))
    _, idx = lax.top_k(score, min(SLC_TOPN, n_s))
    k_sel, v_sel = gather_sel(idx)
    tok = idx[..., None] * SLC_BLOCK + jnp.arange(SLC_BLOCK)
    m_sel = tok <= t_pos[None, None, :, None, None]
    s = jnp.einsum('nqghd,ngqksd->nghqks', q_rot, k_sel) * scale
    shp = s.shape
    p = masked_softmax(s.reshape(shp[:4] + (-1,)), m_sel.reshape(m_sel.shape[:3] + (-1,))[:, :, None])
    o_slc = jnp.einsum('nghqks,ngqksd->nqghd', p.reshape(shp).astype(v_sel.dtype), v_sel)
    m_w = (s_pos[None, :] <= t_pos[:, None]) & (s_pos[None, :] >= t_pos[:, None] - WINDOW) & (s_pos[None, :] >= 0)
    p_w = masked_softmax(jnp.einsum('nqghd,nkgd->nghqk', q_rot, k_w) * scale, m_w)
    o_win = jnp.einsum('nghqk,nkgd->nqghd', p_w.astype(v_w.dtype), v_w)
    g = jax.nn.sigmoid(gate.astype(jnp.float32)).astype(q.dtype)
    return g[..., 0:1] * o_cmp + g[..., 1:2] * o_slc + g[..., 2:3] * o_win


def prompt_kv_context(cmp_kv, slc_kv, win_kv, cmp_pos, cmp_w1, cmp_b1, cmp_w2):
    hidden = rows_chunk_hidden(cmp_kv, _chunk_weights(cmp_w1))
    kc, vc = compress_from_hidden(hidden, cmp_pos, cmp_w1, cmp_b1, cmp_w2)
    return kc, vc, slc_kv, win_kv


NSA_SLABS = NSA_GROUP_HEADS
NSA_TILES = NSA_SLABS * NSA_KV_GROUPS
KEY_BLOCK = 128
MASKED = -1e30
SLC_PER_CMP = SLC_BLOCK // CMP_STRIDE
CMP_LEAD = CMP_BLOCK // CMP_STRIDE - 1
SLAB = NSA_KV_GROUPS * NSA_HEAD_DIM
SEL_LANES = 64


def _group_tiles(q_ref, qs_ref):
    tq = q_ref.shape[1]
    lane = lax.broadcasted_iota(jnp.int32, (tq, SLAB), 1)
    for i in range(NSA_SLABS):
        qs = q_ref[0, :, i * 128:(i + 1) * 128]
        for g in range(NSA_KV_GROUPS):
            in_group = (lane >= g * NSA_HEAD_DIM) & (lane < (g + 1) * NSA_HEAD_DIM)
            qs_ref[2 * i + g] = jnp.where(in_group, qs, 0.0).astype(jnp.bfloat16)


def _merge_groups(o_ref, tiles):
    tq = tiles[0].shape[0]
    lane = lax.broadcasted_iota(jnp.int32, (tq, SLAB), 1)
    for i in range(NSA_SLABS):
        o_ref[0, :, i * 128:(i + 1) * 128] = jnp.where(lane < NSA_HEAD_DIM, tiles[2 * i], tiles[2 * i + 1])


def _nsa_cmp_kernel(q_ref, kc_ref, vc_ref, o_ref, sel_ref, qs_ref):
    qi = pl.program_id(1)
    tq = q_ref.shape[1]
    nc = kc_ref.shape[1]
    ns = nc // SLC_PER_CMP
    t0 = qi * tq
    _group_tiles(q_ref, qs_ref)
    kc = kc_ref[0]
    vc = vc_ref[0]
    t_row = t0 + lax.broadcasted_iota(jnp.int32, (tq, nc), 0)
    c_pos = lax.broadcasted_iota(jnp.int32, (tq, nc), 1) * CMP_STRIDE + (CMP_BLOCK - 1)
    vis = c_pos <= t_row
    tiles = range(NSA_TILES)
    s = [_mm(qs_ref[r], kc, _NT) for r in tiles]
    p = []
    for r in tiles:
        sr = jnp.where(vis, s[r], MASKED)
        m = jnp.max(sr, axis=-1, keepdims=True)
        e = jnp.where(vis, jnp.exp(sr - m), 0.0)
        p.append(e / jnp.maximum(jnp.sum(e, axis=-1, keepdims=True), 1e-30))
    _merge_groups(o_ref, [_mm(p[r], vc, _NN) for r in tiles])

    jrow = lax.broadcasted_iota(jnp.int32, (ns, nc), 0)
    ccol = lax.broadcasted_iota(jnp.int32, (ns, nc), 1)
    pool = ((ccol >= SLC_PER_CMP * jrow - CMP_LEAD) & (ccol < SLC_PER_CMP * (jrow + 1))).astype(jnp.bfloat16)
    jb = lax.broadcasted_iota(jnp.int32, (ns, tq), 0)
    jt = (t0 + lax.broadcasted_iota(jnp.int32, (ns, tq), 1)) // SLC_BLOCK
    forced = (jb == 0) | (jb == jt) | (jb == jt - 1)
    sel_t = []
    for g in range(NSA_KV_GROUPS):
        pg = p[g]
        for i in range(1, NSA_SLABS):
            pg = pg + p[2 * i + g]
        pg_hi = pg.astype(jnp.bfloat16)
        pg_lo = (pg - pg_hi.astype(jnp.float32)).astype(jnp.bfloat16)
        p_slc = (lax.dot_general(pool, pg_hi, _NT, preferred_element_type=jnp.float32)
                 + lax.dot_general(pool, pg_lo, _NT, preferred_element_type=jnp.float32))
        score = jnp.where(jb > jt, -jnp.inf, jnp.where(forced, FORCED_SCORE, p_slc))
        rank = jnp.zeros((ns, tq), jnp.float32)
        for i in range(ns):
            row = score[i:i + 1, :]
            tie = jnp.where(jb > i, 1.0, 0.0)
            rank = rank + jnp.where(row > score, 1.0, 0.0) + jnp.where(row == score, tie, 0.0)
        sel_t.append(jnp.where(rank < min(SLC_TOPN, ns), 1.0, 0.0))
        if ns < SEL_LANES:
            sel_t.append(jnp.zeros((SEL_LANES - ns, tq), jnp.float32))
    sel_ref[0] = jnp.concatenate(sel_t, axis=0).T.astype(sel_ref.dtype)


def _nsa_dense_kernel(q_ref, k_ref, v_ref, sel_ref, o_ref, qs_ref, m_ref, l_ref, acc_ref, *, windowed):
    qi = pl.program_id(1)
    tq = q_ref.shape[1]
    t0 = qi * tq
    _group_tiles(q_ref, qs_ref)
    m_ref[...] = jnp.full(m_ref.shape, MASKED, jnp.float32)
    l_ref[...] = jnp.zeros(l_ref.shape, jnp.float32)
    acc_ref[...] = jnp.zeros(acc_ref.shape, jnp.float32)
    t_row = t0 + lax.broadcasted_iota(jnp.int32, (tq, KEY_BLOCK), 0)
    k_lane = lax.broadcasted_iota(jnp.int32, (tq, KEY_BLOCK), 1)
    tiles = range(NSA_TILES)
    first = jnp.maximum(qi - WINDOW // KEY_BLOCK, 0) if windowed else 0

    def key_block(kb, carry):
        start = pl.multiple_of(kb * KEY_BLOCK, KEY_BLOCK)
        kblk = k_ref[0, pl.ds(start, KEY_BLOCK), :]
        vblk = v_ref[0, pl.ds(start, KEY_BLOCK), :]
        k_pos = start + k_lane
        if windowed:
            allowed = [(k_pos <= t_row) & (k_pos >= t_row - WINDOW)] * NSA_KV_GROUPS
        else:
            sel = sel_ref[0]
            n_sel = NSA_KV_GROUPS * SEL_LANES
            erow = lax.broadcasted_iota(jnp.int32, (n_sel, KEY_BLOCK), 0)
            ecol = lax.broadcasted_iota(jnp.int32, (n_sel, KEY_BLOCK), 1)
            blk = kb * (KEY_BLOCK // SLC_BLOCK) + ecol // SLC_BLOCK
            allowed = []
            for g in range(NSA_KV_GROUPS):
                expand = (erow == blk + g * SEL_LANES).astype(jnp.bfloat16)
                picked = lax.dot_general(sel, expand, _NN, preferred_element_type=jnp.float32)
                allowed.append((picked > 0.5) & (k_pos <= t_row))
        s = [_mm(qs_ref[r], kblk, _NT) for r in tiles]
        p = []
        alpha = []
        for r in tiles:
            ok = allowed[r % NSA_KV_GROUPS]
            sr = jnp.where(ok, s[r], MASKED)
            m_prev = m_ref[r]
            m_new = jnp.maximum(m_prev, jnp.max(sr, axis=-1, keepdims=True))
            a = jnp.exp(m_prev - m_new)
            pr = jnp.where(ok, jnp.exp(sr - m_new), 0.0)
            l_ref[r] = a * l_ref[r] + jnp.sum(pr, axis=-1, keepdims=True)
            m_ref[r] = m_new
            alpha.append(a)
            p.append(pr)
        pv = [_mm(p[r], vblk, _NN) for r in tiles]
        for r in tiles:
            acc_ref[r] = alpha[r] * acc_ref[r] + pv[r]
        return carry

    lax.fori_loop(first, qi * (tq // KEY_BLOCK) + tq // KEY_BLOCK, key_block, 0)
    _merge_groups(o_ref, [acc_ref[r] / jnp.maximum(l_ref[r], 1e-30) for r in tiles])


def nsa_prompt_attention(q, q_rot, kc, vc, k_slc, v_slc, k_win, v_win):
    n, T, D = q.shape
    tq = Q_BLOCK
    nc = kc.shape[1]
    qspec = pl.BlockSpec((1, tq, D), lambda b, i: (b, i, 0))
    whole = lambda a: pl.BlockSpec((1,) + a.shape[1:], lambda b, i: (b, 0, 0))
    n_sel = NSA_KV_GROUPS * SEL_LANES
    sel_spec = pl.BlockSpec((1, tq, n_sel), lambda b, i: (b, i, 0))
    params = pltpu.CompilerParams(dimension_semantics=("parallel", "arbitrary"),
                                  vmem_limit_bytes=48 * 1024 * 1024)
    qs_scratch = pltpu.VMEM((NSA_TILES, tq, SLAB), jnp.bfloat16)
    o_cmp, sel = pl.pallas_call(
        _nsa_cmp_kernel,
        grid=(n, T // tq),
        in_specs=[qspec, whole(kc), whole(vc)],
        out_specs=[qspec, sel_spec],
        out_shape=[jax.ShapeDtypeStruct((n, T, D), jnp.float32),
                   jax.ShapeDtypeStruct((n, T, n_sel), jnp.bfloat16)],
        scratch_shapes=[qs_scratch],
        compiler_params=params,
        name="nsa_cmp_select",
    )(q, kc, vc)
    stat = pltpu.VMEM((NSA_TILES, tq, KEY_BLOCK), jnp.float32)

    def dense(windowed, k, v, name):
        return pl.pallas_call(
            functools.partial(_nsa_dense_kernel, windowed=windowed),
            grid=(n, T // tq),
            in_specs=[qspec, whole(k), whole(v), sel_spec],
            out_specs=qspec,
            out_shape=jax.ShapeDtypeStruct((n, T, D), jnp.float32),
            scratch_shapes=[qs_scratch, stat, stat, stat],
            compiler_params=params,
            name=name,
        )(q_rot, k, v, sel)

    return o_cmp, dense(False, k_slc, v_slc, "nsa_selected"), dense(True, k_win, v_win, "nsa_window")


def _to_slabs(x):
    lead = x.shape[:-1]
    x = x.reshape(lead + (NSA_KV_GROUPS, NSA_GROUP_HEADS, NSA_HEAD_DIM))
    return jnp.swapaxes(x, -3, -2).reshape(lead + (NSA_HEADS * NSA_HEAD_DIM,))


def _from_slabs(x):
    lead = x.shape[:-1]
    x = x.reshape(lead + (NSA_GROUP_HEADS, NSA_KV_GROUPS, NSA_HEAD_DIM))
    return jnp.swapaxes(x, -3, -2).reshape(lead + (NSA_HEADS * NSA_HEAD_DIM,))


def prompt_nsa(h, ctx, w_in, w_o):
    kc, vc, slc_kv, win_kv = ctx
    n, T, _ = h.shape
    q, gate = nsa_query(h, w_in)
    t_pos = jnp.arange(T)
    scale = NSA_HEAD_DIM ** -0.5
    flat = lambda a: a.reshape(n, a.shape[1], -1)
    bf = lambda a: flat(a).astype(jnp.bfloat16)
    o_cmp, o_slc, o_win = nsa_prompt_attention(
        _to_slabs(flat(q * scale)), _to_slabs(flat(rope_partial(q, t_pos) * scale)),
        bf(kc), bf(vc), bf(slc_kv[:, :, 0]), bf(slc_kv[:, :, 1]), bf(win_kv[:, :, 0]), bf(win_kv[:, :, 1]))
    heads = lambda o: _from_slabs(o).reshape(q.shape)
    g = jax.nn.sigmoid(gate.astype(jnp.float32)).astype(q.dtype)
    o = g[..., 0:1] * heads(o_cmp) + g[..., 1:2] * heads(o_slc) + g[..., 2:3] * heads(o_win)
    return o.reshape(n, T, NSA_HEADS * NSA_HEAD_DIM) @ w_o


def sample_kv_context(cmp_new, slc_new, win_new, cache_cmp_kv, cache_slc_kv, cache_win_kv, page_table,
                      cmp_pos, cmp_w1, cmp_b1, cmp_w2):
    n, S = cmp_new.shape[:2]
    n_new_blk = -(-S // SLC_BLOCK)
    pad = ((0, 0), (0, n_new_blk * SLC_BLOCK - S), (0, 0), (0, 0), (0, 0))
    w_chunk = _chunk_weights(cmp_w1)
    hidden = jnp.concatenate([paged_chunk_hidden(cache_cmp_kv, page_table, w_chunk),
                              rows_chunk_hidden(jnp.pad(cmp_new.astype(cache_cmp_kv.dtype), pad), w_chunk)], axis=1)
    kc, vc = compress_from_hidden(hidden, cmp_pos, cmp_w1, cmp_b1, cmp_w2)
    pool = cache_slc_kv.reshape((-1, PAGE_SIZE // SLC_BLOCK, SLC_BLOCK) + cache_slc_kv.shape[2:])
    new_blocks = jnp.pad(slc_new, pad).reshape((n, n_new_blk, SLC_BLOCK) + slc_new.shape[2:])
    win_all = jnp.concatenate([cache_win_kv, win_new.astype(cache_win_kv.dtype)], axis=1)
    return kc, vc, pool, new_blocks, win_all


def sample_nsa(h, ctx, page_table, w_in, w_o):
    kc, vc, pool, new_blocks, win_all = ctx
    n, S, _ = h.shape
    q, gate = nsa_query(h, w_in, precise=True)
    t_pos = PAST_LEN + jnp.arange(S)
    sub = PAGE_SIZE // SLC_BLOCK
    n_past_blk = PAST_LEN // SLC_BLOCK
    n_new_blk = new_blocks.shape[1]
    bi = jnp.arange(n)[:, None, None, None]
    gi = jnp.arange(NSA_KV_GROUPS)[None, :, None, None]

    def gather_sel(idx):
        jp = jnp.minimum(idx, n_past_blk - 1)
        phys = page_table[bi, jp // sub]
        from_past = pool[phys, jp % sub, :, :, gi]
        jn = jnp.clip(idx - n_past_blk, 0, n_new_blk - 1)
        from_new = new_blocks[bi, jn, :, :, gi].astype(from_past.dtype)
        blk = jnp.where((idx >= n_past_blk)[..., None, None, None], from_new, from_past)
        return blk[..., 0, :], blk[..., 1, :]

    wb = win_all.shape[1] - S
    s_pos = PAST_LEN - wb + jnp.arange(win_all.shape[1])
    with jax.default_matmul_precision("highest"):
        o = nsa_branches(q, gate, t_pos, kc, vc, gather_sel, win_all[:, :, 0], win_all[:, :, 1], s_pos)
    return dense(o.reshape(n, S, NSA_HEADS * NSA_HEAD_DIM), w_o, precise=True)


def kernel(x_prompt, x_sample, state_wkv, state_shift, cache_cmp_kv, cache_slc_kv, cache_win_kv, page_table, norm_mix, norm_ffn, norm_kv, norm_final, rw_mu, rw_w_rkv, rw_w0, rw_w1, rw_w2, rw_a0, rw_a1, rw_a2, rw_g1, rw_g2, rw_k_k, rw_k_a, rw_r_k, rw_gn_w, rw_gn_b, rw_w_o, nsa_w_kv, nsa_cmp_pos, nsa_cmp_w1, nsa_cmp_b1, nsa_cmp_w2, nsa_w_in, nsa_w_o, ffn_w_gu, ffn_w_down, moe_router, moe_w_gu, moe_w_down):
    cmp_params = (nsa_cmp_pos, nsa_cmp_w1, nsa_cmp_b1, nsa_cmp_w2)

    assert DEPTH == 2 and N_A_LAYERS == 1

    def trunk(x, pos, shift0, wkv0, make_context, attend, precise):
        h = rmsnorm(x, norm_mix[0])
        y, s_fin, h_last = rwkv7_time_mix(
            h, shift0[0], wkv0[0], rw_mu[0], rw_w_rkv[0], rw_w0[0], rw_w1[0], rw_w2[0],
            rw_a0[0], rw_a1[0], rw_a2[0], rw_g1[0], rw_g2[0], rw_k_k[0], rw_k_a[0],
            rw_r_k[0], rw_gn_w[0], rw_gn_b[0], rw_w_o[0], precise=precise)
        x = x + y
        x = x + swiglu(rmsnorm(x, norm_ffn[0]), ffn_w_gu[0], ffn_w_down[0], precise=precise)
        rows = shared_kv_rows(x, norm_kv, nsa_w_kv, pos, precise=precise)
        ctx = make_context(*rows)
        x = x + attend(rmsnorm(x, norm_mix[1]), ctx, nsa_w_in[0], nsa_w_o[0])
        return x, s_fin[None], h_last[None], rows, ctx

    n_p = x_prompt.shape[0]
    pos_p = jnp.arange(x_prompt.shape[1], dtype=jnp.int32)
    shift0_p = jnp.zeros((N_A_LAYERS, n_p, D_MODEL), x_prompt.dtype)
    wkv0_p = jnp.zeros((N_A_LAYERS, n_p, RWKV_HEADS, RWKV_HEAD, RWKV_HEAD), jnp.float32)
    x_p, wkv_p, shift_p, rows_p, _ = trunk(
        x_prompt, pos_p, shift0_p, wkv0_p,
        lambda c, s, w: prompt_kv_context(c, s, w, *cmp_params),
        prompt_nsa, False)
    cmp_kv_p, slc_kv_p, win_rows_p = rows_p
    win_kv_p = win_rows_p[:, -min(WINDOW, x_prompt.shape[1]):]

    pos_s = PAST_LEN + jnp.arange(x_sample.shape[1], dtype=jnp.int32)
    x_s, wkv_s, shift_s, rows_s, ctx_s = trunk(
        x_sample, pos_s, state_shift, state_wkv,
        lambda c, s, w: sample_kv_context(c, s, w, cache_cmp_kv, cache_slc_kv, cache_win_kv, page_table, *cmp_params),
        lambda h, ctx, wi, wo: sample_nsa(h, ctx, page_table, wi, wo), True)
    cmp_kv_s, slc_kv_s, _ = rows_s
    win_kv_s = ctx_s[4][:, -cache_win_kv.shape[1]:]

    tokens = lambda x: x.reshape(-1, D_MODEL)
    n_tok_p = x_p.shape[0] * x_p.shape[1]
    h_p, h_s = tokens(rmsnorm(x_p, norm_ffn[1])), tokens(rmsnorm(x_s, norm_ffn[1]))
    logits = jnp.concatenate([matmul_precise(h_p, moe_router[0]), matmul_precise(h_s, moe_router[0])], axis=0)
    moe = moe_swiglu(jnp.concatenate([h_p, h_s], axis=0), logits, moe_w_gu[0], moe_w_down[0])
    y_prompt = rmsnorm(x_p + moe[:n_tok_p].reshape(x_p.shape), norm_final)
    y_sample = rmsnorm(x_s + moe[n_tok_p:].reshape(x_s.shape), norm_final)

    return (y_prompt, y_sample, wkv_p, shift_p, cmp_kv_p, slc_kv_p, win_kv_p,
            wkv_s, shift_s, cmp_kv_s, slc_kv_s, win_kv_s)
```

```python
import functools
import math

import jax
import jax.numpy as jnp
from jax import lax
from jax.experimental import pallas as pl
from jax.experimental.pallas import tpu as pltpu

D_MODEL = 1024
DEPTH = 2
PAST_LEN = 16384
PAGE_SIZE = 128
N_A_LAYERS = DEPTH // 2
RWKV_HEAD = 64
RWKV_HEADS = D_MODEL // RWKV_HEAD
DECAY_SCALE = math.exp(-0.5)
GN_EPS = RWKV_HEAD * 1e-5
NSA_HEADS = 16
NSA_HEAD_DIM = 64
NSA_KV_GROUPS = 2
NSA_GROUP_HEADS = NSA_HEADS // NSA_KV_GROUPS
N_BRANCH = 3
CMP_BLOCK = 32
CMP_STRIDE = 16
CMP_HIDDEN = 128
SLC_BLOCK = 64
SLC_TOPN = 16
WINDOW = 512
Q_BLOCK = 128
FORCED_SCORE = 1e4
ROPE_THETA = 500000.0
ROT_DIM = NSA_HEAD_DIM // 4
N_EXPERTS = 8
TOP_K = 2
MOE_BLOCK = 256
NORM_EPS = 1e-6


def _rmsnorm_kernel(x_ref, g_ref, o_ref):
    x = x_ref[...]
    y = x * lax.rsqrt(jnp.mean(x * x, axis=-1, keepdims=True) + NORM_EPS)
    o_ref[...] = y * g_ref[...]


def rmsnorm(x, g):
    shp = x.shape
    x2 = x.reshape(-1, shp[-1])
    rows = x2.shape[0]
    tm = min(rows, 512)
    out = pl.pallas_call(
        _rmsnorm_kernel,
        grid=(rows // tm,),
        in_specs=[pl.BlockSpec((tm, shp[-1]), lambda i: (i, 0)),
                  pl.BlockSpec((1, shp[-1]), lambda i: (0, 0))],
        out_specs=pl.BlockSpec((tm, shp[-1]), lambda i: (i, 0)),
        out_shape=jax.ShapeDtypeStruct(x2.shape, x.dtype),
        name="rmsnorm",
    )(x2, g.reshape(1, -1))
    return out.reshape(shp)


def rope_partial(x, pos):
    half = ROT_DIM // 2
    inv = ROPE_THETA ** (-2.0 * jnp.arange(half, dtype=jnp.float32) / ROT_DIM)
    ang = pos.astype(jnp.float32)[:, None] * inv[None, :]
    shape = (1, pos.shape[0]) + (1,) * (x.ndim - 3) + (half,)
    cos = jnp.cos(ang).reshape(shape)
    sin = jnp.sin(ang).reshape(shape)
    xf = x.astype(jnp.float32)
    x1, x2 = xf[..., :half], xf[..., half:ROT_DIM]
    out = jnp.concatenate([x1 * cos - x2 * sin, x2 * cos + x1 * sin, xf[..., ROT_DIM:]], axis=-1)
    return out.astype(x.dtype)


def masked_softmax(s, mask):
    s = jnp.where(mask, s.astype(jnp.float32), -jnp.inf)
    m = jnp.max(s, axis=-1, keepdims=True)
    m = jnp.where(jnp.isfinite(m), m, 0.0)
    e = jnp.where(mask, jnp.exp(s - m), 0.0)
    return e / jnp.maximum(jnp.sum(e, axis=-1, keepdims=True), 1e-30)


FF_CHUNK = 1408
SWIGLU_ROWS = 512


def _swiglu_kernel(blk_e_ref, n_used_ref, x_ref, wg_ref, wu_ref, wd_ref, *rest):
    o_ref = rest[-1]
    i = pl.program_id(0)
    f = pl.program_id(1)

    @pl.when(i < n_used_ref[0])
    def _():
        x = x_ref[...]
        g = jnp.dot(x, wg_ref[0], preferred_element_type=jnp.float32)
        u = jnp.dot(x, wu_ref[0], preferred_element_type=jnp.float32)
        act = (g * jax.nn.sigmoid(g) * u).astype(jnp.bfloat16)
        y = jnp.dot(act, wd_ref[0], preferred_element_type=jnp.float32)

        @pl.when(f == 0)
        def _():
            o_ref[...] = y + rest[0][...] if len(rest) == 2 else y

        @pl.when(f > 0)
        def _():
            o_ref[...] += y

    @pl.when(i >= n_used_ref[0])
    def _():
        o_ref[...] = jnp.zeros(o_ref.shape, o_ref.dtype)


def grouped_swiglu(xb, blk_e, n_used, w_gu, w_down, res=None):
    rows, d = xb.shape
    extra = [] if res is None else [res]
    b = min(SWIGLU_ROWS, rows)
    n_blk = rows // b
    ff = w_down.shape[1]
    tf = FF_CHUNK
    n_f = ff // tf
    chunk = lambda i, f, be, nu: jnp.where(i < nu[0], f, n_f - 1)
    return pl.pallas_call(
        _swiglu_kernel,
        grid_spec=pltpu.PrefetchScalarGridSpec(
            num_scalar_prefetch=2,
            grid=(n_blk, n_f),
            in_specs=[
                pl.BlockSpec((b, d), lambda i, f, be, nu: (i, 0)),
                pl.BlockSpec((1, d, tf), lambda i, f, be, nu: (be[i], 0, chunk(i, f, be, nu))),
                pl.BlockSpec((1, d, tf), lambda i, f, be, nu: (be[i], 0, n_f + chunk(i, f, be, nu))),
                pl.BlockSpec((1, tf, d), lambda i, f, be, nu: (be[i], chunk(i, f, be, nu), 0)),
            ] + [pl.BlockSpec((b, d), lambda i, f, be, nu: (i, 0))] * len(extra),
            out_specs=pl.BlockSpec((b, d), lambda i, f, be, nu: (i, 0)),
        ),
        out_shape=jax.ShapeDtypeStruct((rows, d), jnp.float32),
        compiler_params=pltpu.CompilerParams(dimension_semantics=("arbitrary", "arbitrary"),
                                             vmem_limit_bytes=56 * 1024 * 1024),
        name="grouped_swiglu",
    )(blk_e.astype(jnp.int32), jnp.reshape(n_used, (1,)).astype(jnp.int32), xb, w_gu, w_gu, w_down, *extra)


def swiglu(h, w_gu, w_down, precise=False):
    g, u = jnp.split(dense(h, w_gu, precise), 2, axis=-1)
    return dense(jax.nn.silu(g) * u, w_down, precise)


def swiglu_residual(x, h, w_gu, w_down):
    n_blk = h.shape[0] // min(SWIGLU_ROWS, h.shape[0])
    return grouped_swiglu(h, jnp.zeros((n_blk,), jnp.int32), jnp.int32(n_blk),
                          w_gu.astype(jnp.bfloat16)[None], w_down.astype(jnp.bfloat16)[None], res=x)


def _combine_norm_kernel(x_ref, y0_ref, y1_ref, g_ref, gain_ref, o_ref):
    g = g_ref[...]
    x = x_ref[...] + (y0_ref[...] * g[:, 0:1] + y1_ref[...] * g[:, 1:2])
    o_ref[...] = _rms(x) * gain_ref[...]


def combine_norm(x, y0, y1, gate, gain):
    m, d = x.shape
    tm = min(2 * ROW_TILE, m)
    rows = lambda width: pl.BlockSpec((tm, width), lambda i: (i, 0))
    return pl.pallas_call(
        _combine_norm_kernel,
        grid=(m // tm,),
        in_specs=[rows(d), rows(d), rows(d), rows(gate.shape[1]), pl.BlockSpec((1, d), lambda i: (0, 0))],
        out_specs=rows(d),
        out_shape=jax.ShapeDtypeStruct((m, d), jnp.float32),
        compiler_params=pltpu.CompilerParams(dimension_semantics=("parallel",)),
        name="combine_norm",
    )(x, y0, y1, gate, gain.reshape(1, d))


def moe_swiglu(xt, logits, w_gu_e, w_down_e):
    n_tok, d = xt.shape
    b = SWIGLU_ROWS
    experts = jnp.arange(N_EXPERTS)[None, :]
    e0 = jnp.argmax(logits, axis=-1)
    v0 = jnp.max(logits, axis=-1)
    rest = jnp.where(experts == e0[:, None], -jnp.inf, logits)
    e1 = jnp.argmax(rest, axis=-1)
    v1 = jnp.max(rest, axis=-1)
    top_idx = jnp.stack([e0, e1], axis=-1).astype(jnp.int32)
    gate = jax.nn.softmax(jnp.stack([v0, v1], axis=-1), axis=-1)
    nk = n_tok * TOP_K
    flat_e = top_idx.reshape(nk)
    onehot = (flat_e[:, None] == jnp.arange(N_EXPERTS)[None, :]).astype(jnp.int32)
    before = jnp.cumsum(onehot, axis=0) - onehot
    counts = jnp.sum(onehot, axis=0)
    padded = (counts + b - 1) // b * b
    ends_pad = jnp.cumsum(padded)
    starts_pad = ends_pad - padded
    dest = jnp.sum(onehot * (starts_pad[None, :] + before), axis=1)
    n_rows = (nk + b - 1) // b * b + N_EXPERTS * b
    n_blk = n_rows // b
    flat_tok = jnp.repeat(jnp.arange(n_tok, dtype=jnp.int32), TOP_K)
    row_tok = jnp.full((n_rows,), n_tok, jnp.int32).at[dest].set(flat_tok)
    blk_e = jnp.minimum(jnp.searchsorted(ends_pad, jnp.arange(n_blk) * b, side='right'), N_EXPERTS - 1)
    x_pad = jnp.concatenate([xt.astype(jnp.bfloat16), jnp.zeros((1, d), jnp.bfloat16)], axis=0)
    yb = grouped_swiglu(x_pad[row_tok], blk_e, ends_pad[-1] // b,
                        w_gu_e.astype(jnp.bfloat16), w_down_e.astype(jnp.bfloat16))
    dest = dest.reshape(n_tok, TOP_K)
    return yb[dest[:, 0]], yb[dest[:, 1]], gate


WKV_CHUNK = 64
WKV_HEADS_PER_STEP = 16

_NN = (((1,), (0,)), ((), ()))
_NT = (((1,), (1,)), ((), ()))
_TN = (((0,), (0,)), ((), ()))


def _mm(x, y, dims):
    return lax.dot_general(x.astype(jnp.bfloat16), y.astype(jnp.bfloat16), dims,
                           preferred_element_type=jnp.float32)


def _wkv7_chunk_kernel(r_ref, lw_ref, k_ref, v_ref, kk_ref, a_ref, rk_ref, gnw_ref, gnb_ref, s0_ref,
                       y_ref, sout_ref, state_ref):
    c = pl.program_id(2)
    L = r_ref.shape[1]
    hb = state_ref.shape[0]
    N = RWKV_HEAD

    @pl.when(c == 0)
    def _():
        state_ref[...] = s0_ref[0]

    row = lax.broadcasted_iota(jnp.int32, (L, L), 0)
    col = lax.broadcasted_iota(jnp.int32, (L, L), 1)
    strict = row > col
    incl = row >= col
    tri = incl.astype(jnp.bfloat16)
    eye_n = lax.broadcasted_iota(jnp.int32, (N, N), 0) == lax.broadcasted_iota(jnp.int32, (N, N), 1)

    lw = lw_ref[0]
    lw_hi, lw_lo = _split_bf16(lw)
    cum = (lax.dot_general(tri, lw_hi, _NN, preferred_element_type=jnp.float32)
           + lax.dot_general(tri, lw_lo, _NN, preferred_element_type=jnp.float32))
    cum_last = cum[L - 1:L, :]
    e_neg = jnp.exp(-cum)
    e_tail = jnp.exp(cum_last - cum)
    e_prev = jnp.exp(cum - lw)
    r_in = r_ref[0]
    k_in = k_ref[0]
    r_all = r_in * jnp.exp(cum)
    kt_all = k_in * e_neg
    kh_all = k_in * e_tail
    wl_all = jnp.exp(cum_last)

    heads = range(hb)
    per_head = lambda t: [t[:, j * N:(j + 1) * N] for j in heads]
    rt, kt, kh, v, wl = (per_head(t) for t in (r_all, kt_all, kh_all, v_ref[0], wl_all))
    kk_raw, a_gate, e_prev, e_neg, e_tail = (per_head(t) for t in (kk_ref[0], a_ref[0], e_prev, e_neg, e_tail))
    at, bt, bh = [], [], []
    for j in heads:
        norm = jnp.sqrt(jnp.sum(kk_raw[j] * kk_raw[j], axis=-1, keepdims=True))
        kk = kk_raw[j] / jnp.maximum(norm, 1e-12)
        b = kk * a_gate[j]
        at.append(-kk * e_prev[j])
        bt.append(b * e_neg[j])
        bh.append(b * e_tail[j])
    a_ab = [jnp.where(strict, _mm(at[j], bt[j], _NT), 0.0) for j in heads]
    a_ak = [jnp.where(strict, _mm(at[j], kt[j], _NT), 0.0) for j in heads]
    r_b = [jnp.where(incl, _mm(rt[j], bt[j], _NT), 0.0) for j in heads]
    r_k = [jnp.where(incl, _mm(rt[j], kt[j], _NT), 0.0) for j in heads]
    av = [_mm(a_ak[j], v[j], _NN) for j in heads]
    ht = [_mm(v[j], kh[j], _TN) for j in heads]
    yp = [_mm(r_k[j], v[j], _NN) for j in heads]
    pw = a_ab
    inv_a = a_ab
    n = 1
    while 2 * n < L:
        pw = [_mm(pw[j], pw[j], _NN) for j in heads]
        inv_a = [inv_a[j] + pw[j] + _mm(inv_a[j], pw[j], _NN) for j in heads]
        n *= 2
    ap = [at[j] + _mm(inv_a[j], at[j], _NN) for j in heads]
    vp = [av[j] + _mm(inv_a[j], av[j], _NN) for j in heads]
    g = [jnp.where(eye_n, wl[j], 0.0) + _mm(bh[j], ap[j], _TN) for j in heads]
    ht = [ht[j] + _mm(vp[j], bh[j], _TN) for j in heads]
    rp = [rt[j] + _mm(r_b[j], ap[j], _NN) for j in heads]
    yp = [yp[j] + _mm(r_b[j], vp[j], _NN) for j in heads]
    s_prev = [state_ref[j] for j in heads]
    y = [_mm(rp[j], s_prev[j], _NT) + yp[j] for j in heads]
    for j in heads:
        state_ref[j] = _mm(s_prev[j], g[j], _NT) + ht[j]
    r_raw, k_raw = per_head(r_in), per_head(k_in)
    rk, gnw, gnb = per_head(rk_ref[...]), per_head(gnw_ref[...]), per_head(gnb_ref[...])
    out = []
    for j in heads:
        mean = jnp.mean(y[j], axis=-1, keepdims=True)
        cen = y[j] - mean
        var = jnp.mean(cen * cen, axis=-1, keepdims=True)
        bonus = jnp.sum(r_raw[j] * k_raw[j] * rk[j], axis=-1, keepdims=True) * v[j]
        out.append(cen * lax.rsqrt(var + GN_EPS) * gnw[j] + gnb[j] + bonus)
    y_ref[0] = jnp.concatenate(out, axis=-1)

    @pl.when(c == pl.num_programs(2) - 1)
    def _():
        sout_ref[0] = state_ref[...]


def wkv7_chunked(r, lw, k, v, kk, a_gate, r_k, gn_w, gn_b, s0):
    n, T, D = r.shape
    L = WKV_CHUNK
    hb = WKV_HEADS_PER_STEP
    w = hb * RWKV_HEAD
    seq = pl.BlockSpec((1, L, w), lambda b, h, c: (b, c, h))
    vec = pl.BlockSpec((1, w), lambda b, h, c: (0, h))
    st = pl.BlockSpec((1, hb, RWKV_HEAD, RWKV_HEAD), lambda b, h, c: (b, h, 0, 0))
    row = lambda t: t.reshape(1, D).astype(jnp.float32)
    return pl.pallas_call(
        _wkv7_chunk_kernel,
        grid=(n, D // w, T // L),
        in_specs=[seq] * 6 + [vec] * 3 + [st],
        out_specs=[seq, st],
        out_shape=[jax.ShapeDtypeStruct((n, T, D), jnp.float32),
                   jax.ShapeDtypeStruct(s0.shape, jnp.float32)],
        scratch_shapes=[pltpu.VMEM((hb, RWKV_HEAD, RWKV_HEAD), jnp.float32)],
        compiler_params=pltpu.CompilerParams(dimension_semantics=("parallel", "parallel", "arbitrary")),
        name="wkv7_chunked",
    )(r, lw, k, v, kk, a_gate, row(r_k), row(gn_w), row(gn_b), s0)


def wkv7_scan(r, lw, k, v, a_vec, b_vec, s0):
    def step(S, inp):
        r_t, lw_t, k_t, v_t, a_t, b_t = inp
        sa = jnp.sum(S * a_t[:, :, None, :], axis=-1)
        S = S * jnp.exp(lw_t)[:, :, None, :] + sa[..., None] * b_t[:, :, None, :] + v_t[..., None] * k_t[:, :, None, :]
        return S, jnp.sum(S * r_t[:, :, None, :], axis=-1)

    xs = tuple(jnp.moveaxis(t, 1, 0) for t in (r, lw, k, v, a_vec, b_vec))
    s_fin, ys = lax.scan(step, s0, xs)
    return jnp.moveaxis(ys, 0, 1), s_fin


ROW_TILE = 256


def _bdot(x, w):
    return jnp.dot(x.astype(jnp.bfloat16), w, preferred_element_type=jnp.float32)


def _rms(x):
    return x * lax.rsqrt(jnp.mean(x * x, axis=-1, keepdims=True) + NORM_EPS)


def _rwkv_pre_kernel(x_ref, shift_ref, gain_ref, mu_ref, vec_ref, wrkv_ref, w1_ref, a1_ref, g1_ref,
                     w2_ref, a2_ref, g2_ref,
                     r_ref, lw_ref, k_ref, v_ref, kk_ref, a_ref, g_ref, hlast_ref, prev_ref):
    @pl.when(pl.program_id(1) == 0)
    def _():
        prev_ref[...] = shift_ref[0]

    tm = x_ref.shape[1]
    h = _rms(x_ref[0]) * gain_ref[...]
    first_row = lax.broadcasted_iota(jnp.int32, h.shape, 0) == 0
    h_prev = jnp.where(first_row, prev_ref[...], pltpu.roll(h, 1, 0))
    prev_ref[...] = h[tm - 1:tm, :]
    hlast_ref[0] = h[tm - 1:tm, :]
    dx = h_prev - h
    mix = lambda i: (h + dx * mu_ref[i:i + 1, :]).astype(jnp.bfloat16)
    w0, a0, k_k, k_a = (vec_ref[i:i + 1, :] for i in range(4))
    r_ref[0] = _bdot(mix(0), wrkv_ref[0])
    k = _bdot(mix(2), wrkv_ref[1])
    v_ref[0] = _bdot(mix(3), wrkv_ref[2])
    lw_ref[0] = -DECAY_SCALE * jax.nn.sigmoid(w0 + _bdot(jnp.tanh(_bdot(mix(1), w1_ref[...])), w2_ref[...]))
    a = jax.nn.sigmoid(a0 + _bdot(_bdot(mix(4), a1_ref[...]), a2_ref[...]))
    g_ref[0] = _bdot(jax.nn.sigmoid(_bdot(mix(5), g1_ref[...])), g2_ref[...])
    a_ref[0] = a
    kk_ref[0] = k * k_k
    k_ref[0] = k * (1.0 + (a - 1.0) * k_a)


def _out_proj_kernel(*refs, n_terms, gated, routed):
    terms = refs[:n_terms]
    rest = refs[n_terms:]
    if gated:
        gate_ref, rest = rest[0], rest[1:]
    x_ref, w_ref, gain_ref = rest[:3]
    rest = rest[3:]
    if routed:
        wr_ref, rest = rest[0], rest[1:]
    xo_ref, ho_ref = rest[:2]
    y = terms[0][...]
    for t in terms[1:]:
        y = y + t[...]
    if gated:
        y = y * gate_ref[...]
    xo = x_ref[...] + _bdot(y, w_ref[...])
    xo_ref[...] = xo
    h = _rms(xo) * gain_ref[...]
    ho_ref[...] = h.astype(ho_ref.dtype)
    if routed:
        h_hi, h_lo = _split_bf16(h)
        w_hi, w_lo = _split_bf16(wr_ref[...])
        dot = functools.partial(jnp.dot, preferred_element_type=jnp.float32)
        rest[2][...] = dot(h_hi, w_hi) + (dot(h_hi, w_lo) + dot(h_lo, w_hi))


def out_proj(terms, gate, x, w, gain, w_router=None):
    m, d = x.shape
    kdim = w.shape[0]
    tm = min(ROW_TILE, m)
    rows = lambda width: pl.BlockSpec((tm, width), lambda i: (i, 0))
    full = lambda a: pl.BlockSpec(a.shape, lambda i: (0, 0))
    ins = list(terms) + ([gate] if gate is not None else [])
    consts = [w.astype(jnp.bfloat16), gain.reshape(1, d)]
    out_specs = [rows(d), rows(d)]
    out_shape = [jax.ShapeDtypeStruct((m, d), jnp.float32), jax.ShapeDtypeStruct((m, d), jnp.bfloat16)]
    if w_router is not None:
        consts.append(jnp.pad(w_router, ((0, 0), (0, 128 - w_router.shape[1]))))
        out_specs.append(rows(128))
        out_shape.append(jax.ShapeDtypeStruct((m, 128), jnp.float32))
    return pl.pallas_call(
        functools.partial(_out_proj_kernel, n_terms=len(terms), gated=gate is not None,
                          routed=w_router is not None),
        grid=(m // tm,),
        in_specs=[rows(kdim)] * len(ins) + [rows(d)] + [full(a) for a in consts],
        out_specs=out_specs,
        out_shape=out_shape,
        compiler_params=pltpu.CompilerParams(dimension_semantics=("parallel",),
                                             vmem_limit_bytes=48 * 1024 * 1024),
        name="out_proj",
    )(*ins, x, *consts)


def rwkv7_layer(x, shift0, s0, gain, mu, w_rkv, w0, w1, w2, a0, a1, a2, g1, g2, k_k, k_a, r_k, gn_w, gn_b, w_o,
                gain_next):
    n, T, D = x.shape
    tm = min(ROW_TILE, T)
    bf = lambda t: t.astype(jnp.bfloat16)
    seq = pl.BlockSpec((1, tm, D), lambda b, t: (b, t, 0))
    full = lambda a: pl.BlockSpec(a.shape, lambda b, t: (0,) * a.ndim)
    per_seq = pl.BlockSpec((1, 1, D), lambda b, t: (b, 0, 0))
    vecs = jnp.stack([w0, a0, k_k, k_a]).astype(jnp.float32)
    weights = [bf(w_rkv), bf(w1), bf(a1), bf(g1), bf(w2), bf(a2), bf(g2)]
    small = [gain.reshape(1, D), mu, vecs]
    outs = pl.pallas_call(
        _rwkv_pre_kernel,
        grid=(n, T // tm),
        in_specs=[seq, per_seq] + [full(a) for a in small + weights],
        out_specs=[seq] * 7 + [per_seq],
        out_shape=[jax.ShapeDtypeStruct((n, T, D), jnp.float32)] * 7
                  + [jax.ShapeDtypeStruct((n, 1, D), jnp.float32)],
        scratch_shapes=[pltpu.VMEM((1, D), jnp.float32)],
        compiler_params=pltpu.CompilerParams(dimension_semantics=("parallel", "arbitrary"),
                                             vmem_limit_bytes=56 * 1024 * 1024),
        name="rwkv_pre",
    )(x, shift0.reshape(n, 1, D), *small, *weights)
    r, lw, k, v, kk, a_gate, g, h_last = outs
    y, s_fin = wkv7_chunked(r, lw, k, v, kk, a_gate, r_k.reshape(-1), gn_w, gn_b, s0.astype(jnp.float32))
    flat = lambda t: t.reshape(n * T, D)
    x1, h1 = out_proj([flat(y)], flat(g), flat(x), w_o, gain_next)
    return x1, h1, s_fin, h_last.reshape(n, D)


def rwkv7_time_mix(h, h_prev, s0, mu, w_rkv, w0, w1, w2, a0, a1, a2, g1, g2, k_k, k_a, r_k, gn_w, gn_b, w_o,
                   precise=False):
    n, T, D = h.shape
    f32 = jnp.float32
    mm = functools.partial(dense, precise=precise)
    dx = jnp.concatenate([h_prev[:, None, :].astype(h.dtype), h[:, :-1]], axis=1) - h
    xr, xw, xk, xv, xa, xg = (h + dx * mu[i] for i in range(6))
    r = mm(xr, w_rkv[0])
    k = mm(xk, w_rkv[1])
    v = mm(xv, w_rkv[2])
    log_decay = -DECAY_SCALE * jax.nn.sigmoid((w0 + mm(jnp.tanh(mm(xw, w1)), w2)).astype(f32))
    a = jax.nn.sigmoid((a0 + mm(mm(xa, a1), a2)).astype(f32))
    g = mm(jax.nn.sigmoid(mm(xg, g1)), g2)
    heads = lambda t: t.astype(f32).reshape(n, T, RWKV_HEADS, RWKV_HEAD)
    kk = heads(k * k_k)
    kk = kk / jnp.maximum(jnp.sqrt(jnp.sum(kk * kk, axis=-1, keepdims=True)), 1e-12)
    a_h = heads(a)
    k_h = heads(k.astype(f32) * (1.0 + (a - 1.0) * k_a.astype(f32)))
    r_h, v_h = heads(r), heads(v)
    y, s_fin = wkv7_scan(r_h, heads(log_decay), k_h, v_h, -kk, kk * a_h, s0.astype(f32))
    mean = jnp.mean(y, axis=-1, keepdims=True)
    var = jnp.mean(jnp.square(y - mean), axis=-1, keepdims=True)
    y = ((y - mean) * lax.rsqrt(var + GN_EPS)).reshape(n, T, D) * gn_w.astype(f32) + gn_b.astype(f32)
    bonus = jnp.sum(r_h * k_h * r_k.astype(f32), axis=-1, keepdims=True) * v_h
    y = (y + bonus.reshape(n, T, D)).astype(h.dtype)
    return mm(y * g, w_o), s_fin, h[:, -1]


def shared_kv_rows(x, norm_kv, w_kv, pos, precise=False):
    n, T, _ = x.shape
    kv = dense(rmsnorm(x, norm_kv), w_kv, precise).reshape(n, T, N_BRANCH, 2, NSA_KV_GROUPS, NSA_HEAD_DIM)
    cmp_kv = kv[:, :, 0]
    slc_kv = jnp.stack([rope_partial(kv[:, :, 1, 0], pos), kv[:, :, 1, 1]], axis=2)
    win_kv = jnp.stack([rope_partial(kv[:, :, 2, 0], pos), kv[:, :, 2, 1]], axis=2)
    return cmp_kv, slc_kv, win_kv


CHUNK_LANES = CMP_STRIDE * 2 * NSA_KV_GROUPS * NSA_HEAD_DIM
HIDDEN_LANES = 2 * NSA_KV_GROUPS * 2 * CMP_HIDDEN
PAGES_PER_STEP = 32
CHUNKS_PER_PAGE = PAGE_SIZE // CMP_STRIDE


def _chunk_weights(cmp_w1):
    w = cmp_w1.reshape(2, 2, CMP_STRIDE, NSA_HEAD_DIM, CMP_HIDDEN)
    w = w.transpose(2, 0, 3, 1, 4)
    eye = jnp.eye(2, dtype=w.dtype)
    big = jnp.einsum('ab,cf,jaehk->jacebfhk', eye, jnp.eye(NSA_KV_GROUPS, dtype=w.dtype), w)
    return big.reshape(CHUNK_LANES, HIDDEN_LANES).astype(jnp.bfloat16)


def _paged_chunk_kernel(pt_ref, *refs):
    pages, w_ref, o_ref = refs[:PAGES_PER_STEP], refs[PAGES_PER_STEP], refs[PAGES_PER_STEP + 1]
    x = jnp.concatenate([p[0] for p in pages], axis=0).astype(jnp.bfloat16)
    o_ref[0] = jnp.dot(x, w_ref[...], preferred_element_type=jnp.float32)


def paged_chunk_hidden(cache, page_table, w_chunk):
    n, n_pages = page_table.shape
    pages = cache.reshape(cache.shape[0], CHUNKS_PER_PAGE, CHUNK_LANES)
    rows = PAGES_PER_STEP * CHUNKS_PER_PAGE

    def page_spec(k):
        return pl.BlockSpec((1, CHUNKS_PER_PAGE, CHUNK_LANES),
                            lambda b, s, pt: (pt[b, s * PAGES_PER_STEP + k], 0, 0))

    return pl.pallas_call(
        _paged_chunk_kernel,
        grid_spec=pltpu.PrefetchScalarGridSpec(
            num_scalar_prefetch=1,
            grid=(n, n_pages // PAGES_PER_STEP),
            in_specs=[page_spec(k) for k in range(PAGES_PER_STEP)]
                     + [pl.BlockSpec((CHUNK_LANES, HIDDEN_LANES), lambda b, s, pt: (0, 0))],
            out_specs=pl.BlockSpec((1, rows, HIDDEN_LANES), lambda b, s, pt: (b, s, 0)),
        ),
        out_shape=jax.ShapeDtypeStruct((n, n_pages * CHUNKS_PER_PAGE, HIDDEN_LANES), jnp.float32),
        compiler_params=pltpu.CompilerParams(dimension_semantics=("parallel", "arbitrary"),
                                             vmem_limit_bytes=48 * 1024 * 1024),
        name="paged_chunk_hidden",
    )(page_table.astype(jnp.int32), *([pages] * PAGES_PER_STEP), w_chunk)


def _matmul_kernel(x_ref, w_ref, o_ref):
    o_ref[...] = jnp.dot(x_ref[...].astype(jnp.bfloat16), w_ref[...],
                         preferred_element_type=jnp.float32).astype(o_ref.dtype)


def matmul(x, w, rows_per_step=512, out_dtype=jnp.float32):
    m, k = x.shape
    nn = w.shape[1]
    tm = min(rows_per_step, m)
    return pl.pallas_call(
        _matmul_kernel,
        grid=(m // tm,),
        in_specs=[pl.BlockSpec((tm, k), lambda i: (i, 0)), pl.BlockSpec((k, nn), lambda i: (0, 0))],
        out_specs=pl.BlockSpec((tm, nn), lambda i: (i, 0)),
        out_shape=jax.ShapeDtypeStruct((m, nn), out_dtype),
        compiler_params=pltpu.CompilerParams(dimension_semantics=("parallel",),
                                             vmem_limit_bytes=48 * 1024 * 1024),
        name="matmul",
    )(x, w.astype(jnp.bfloat16))


def _split_bf16(x):
    hi = x.astype(jnp.bfloat16)
    return hi, (x - hi.astype(jnp.float32)).astype(jnp.bfloat16)


def _matmul3_kernel(x_ref, w_ref, o_ref):
    x_hi, x_lo = _split_bf16(x_ref[...])
    w_hi, w_lo = _split_bf16(w_ref[...])
    dot = functools.partial(jnp.dot, preferred_element_type=jnp.float32)
    o_ref[...] = dot(x_hi, w_hi) + (dot(x_hi, w_lo) + dot(x_lo, w_hi))


def matmul_precise(x, w, cols_per_step=512):
    m, k = x.shape
    nn = w.shape[1]
    pad = (-nn) % 128
    if pad:
        w = jnp.pad(w, ((0, 0), (0, pad)))
    tn = math.gcd(cols_per_step, nn + pad)
    tm = min(m, 512)
    out = pl.pallas_call(
        _matmul3_kernel,
        grid=(m // tm, (nn + pad) // tn),
        in_specs=[pl.BlockSpec((tm, k), lambda i, j: (i, 0)), pl.BlockSpec((k, tn), lambda i, j: (0, j))],
        out_specs=pl.BlockSpec((tm, tn), lambda i, j: (i, j)),
        out_shape=jax.ShapeDtypeStruct((m, nn + pad), jnp.float32),
        compiler_params=pltpu.CompilerParams(dimension_semantics=("parallel", "parallel")),
        name="matmul_precise",
    )(x, w)
    return out[:, :nn] if pad else out


def dense(x, w, precise=False):
    if precise:
        return matmul_precise(x.reshape(-1, x.shape[-1]), w).reshape(x.shape[:-1] + (w.shape[1],))
    return x @ w


def compress_from_hidden(hidden, pos_emb, w1, b1, w2):
    n, C = hidden.shape[:2]
    hid = hidden.reshape(n, C, 2, NSA_KV_GROUPS, 2, CMP_HIDDEN)
    first, second = hid[:, :, :, :, 0], hid[:, :, :, :, 1]
    second = jnp.concatenate([second[:, 1:], jnp.zeros_like(second[:, :1])], axis=1)
    w1r = w1.reshape(2, CMP_BLOCK, NSA_HEAD_DIM, CMP_HIDDEN)
    bias = jnp.einsum('ajd,ajdh->ah', pos_emb, w1r) + b1
    act = jax.nn.gelu(first + second + bias[None, None, :, None, :])
    out = jnp.einsum('ncagh,ahd->ncagd', act, w2)
    return out[:, :, 0], out[:, :, 1]


def rows_chunk_hidden(kv_rows, w_chunk):
    n, T = kv_rows.shape[:2]
    chunks = kv_rows.reshape(n * T // CMP_STRIDE, CHUNK_LANES)
    return matmul(chunks, w_chunk, rows_per_step=256).reshape(n, T // CMP_STRIDE, HIDDEN_LANES)


def nsa_query(h, w_in, precise=False):
    n, T, _ = h.shape
    proj = dense(h, w_in, precise)
    q = proj[..., :NSA_HEADS * NSA_HEAD_DIM].reshape(n, T, NSA_KV_GROUPS, NSA_GROUP_HEADS, NSA_HEAD_DIM)
    gate = proj[..., NSA_HEADS * NSA_HEAD_DIM:].reshape(n, T, NSA_KV_GROUPS, NSA_GROUP_HEADS, N_BRANCH)
    return q, gate


def nsa_branches(q, gate, t_pos, kc, vc, gather_sel, k_w, v_w, s_pos):
    scale = NSA_HEAD_DIM ** -0.5
    q_rot = rope_partial(q, t_pos)
    n_c = kc.shape[1]
    vis = (jnp.arange(n_c) * CMP_STRIDE + CMP_BLOCK - 1)[None, :] <= t_pos[:, None]
    p_cmp = masked_softmax(jnp.einsum('nqghd,ncgd->nghqc', q, kc) * scale, vis)
    o_cmp = jnp.einsum('nghqc,ncgd->nqghd', p_cmp.astype(vc.dtype), vc)
    ratio = SLC_BLOCK // CMP_STRIDE
    lead = CMP_BLOCK // CMP_STRIDE - 1
    n_s = n_c // ratio
    pg = jnp.pad(jnp.sum(p_cmp, axis=2), ((0, 0), (0, 0), (0, 0), (lead, 0)))
    p_slc = pg[..., 0:ratio * n_s:ratio]
    for o in range(1, ratio + lead):
        p_slc = p_slc + pg[..., o:o + ratio * n_s:ratio]
    jb = jnp.arange(n_s)[None, :]
    jt = (t_pos // SLC_BLOCK)[:, None]
    forced = (jb == 0) | (jb == jt) | (jb == jt - 1)
    score = jnp.where(jb > jt, -jnp.inf, jnp.where(forced, FORCED_SCORE, p_slc))
    before = (score[..., :, None] > score[..., None, :]) | (
        (score[..., :, None] == score[..., None, :]) & (jb[0][:, None] < jb[0][None, :]))
    rank = jnp.sum(before, axis=-2)
    slots = jnp.arange(min(SLC_TOPN, n_s))
    idx = jnp.sum(jnp.where(rank[..., None, :] == slots[:, None], jb[0], 0), axis=-1)
    k_sel, v_sel = gather_sel(idx)
    tok = idx[..., None] * SLC_BLOCK + jnp.arange(SLC_BLOCK)
    m_sel = tok <= t_pos[None, None, :, None, None]
    s = jnp.einsum('nqghd,ngqksd->nghqks', q_rot, k_sel) * scale
    shp = s.shape
    p = masked_softmax(s.reshape(shp[:4] + (-1,)), m_sel.reshape(m_sel.shape[:3] + (-1,))[:, :, None])
    o_slc = jnp.einsum('nghqks,ngqksd->nqghd', p.reshape(shp).astype(v_sel.dtype), v_sel)
    m_w = (s_pos[None, :] <= t_pos[:, None]) & (s_pos[None, :] >= t_pos[:, None] - WINDOW) & (s_pos[None, :] >= 0)
    p_w = masked_softmax(jnp.einsum('nqghd,nkgd->nghqk', q_rot, k_w) * scale, m_w)
    o_win = jnp.einsum('nghqk,nkgd->nqghd', p_w.astype(v_w.dtype), v_w)
    g = jax.nn.sigmoid(gate.astype(jnp.float32)).astype(q.dtype)
    return g[..., 0:1] * o_cmp + g[..., 1:2] * o_slc + g[..., 2:3] * o_win


def prompt_kv_context(cmp_kv, slc_kv, win_kv, cmp_pos, cmp_w1, cmp_b1, cmp_w2):
    hidden = rows_chunk_hidden(cmp_kv, _chunk_weights(cmp_w1))
    kc, vc = compress_from_hidden(hidden, cmp_pos, cmp_w1, cmp_b1, cmp_w2)
    return kc, vc, slc_kv, win_kv


NSA_SLABS = NSA_GROUP_HEADS
NSA_TILES = NSA_SLABS * NSA_KV_GROUPS
KEY_BLOCK = 128
MASKED = -1e30
SLC_PER_CMP = SLC_BLOCK // CMP_STRIDE
CMP_LEAD = CMP_BLOCK // CMP_STRIDE - 1
SLAB = NSA_KV_GROUPS * NSA_HEAD_DIM
SEL_LANES = 64


def _group_tiles(q_ref, qs_ref):
    tq = q_ref.shape[1]
    lane = lax.broadcasted_iota(jnp.int32, (tq, SLAB), 1)
    for i in range(NSA_SLABS):
        qs = q_ref[0, :, i * 128:(i + 1) * 128]
        for g in range(NSA_KV_GROUPS):
            in_group = (lane >= g * NSA_HEAD_DIM) & (lane < (g + 1) * NSA_HEAD_DIM)
            qs_ref[2 * i + g] = jnp.where(in_group, qs, jnp.zeros_like(qs)).astype(jnp.bfloat16)


def _merge_groups(o_ref, tiles, gate_ref, branch):
    tq = tiles[0].shape[0]
    lane = lax.broadcasted_iota(jnp.int32, (tq, SLAB), 1)
    gates = gate_ref[0]
    col = lambda g, i: (g * NSA_GROUP_HEADS + i) * N_BRANCH + branch
    for i in range(NSA_SLABS):
        lo = tiles[2 * i] * gates[:, col(0, i):col(0, i) + 1]
        hi = tiles[2 * i + 1] * gates[:, col(1, i):col(1, i) + 1]
        o_ref[0, :, i * 128:(i + 1) * 128] = jnp.where(lane < NSA_HEAD_DIM, lo, hi)


def _nsa_cmp_kernel(q_ref, kc_ref, vc_ref, gate_ref, o_ref, sel_ref, qs_ref):
    qi = pl.program_id(1)
    tq = q_ref.shape[1]
    nc = kc_ref.shape[1]
    ns = nc // SLC_PER_CMP
    t0 = qi * tq
    _group_tiles(q_ref, qs_ref)
    kc = kc_ref[0]
    vc = vc_ref[0]
    t_row = t0 + lax.broadcasted_iota(jnp.int32, (tq, nc), 0)
    c_pos = lax.broadcasted_iota(jnp.int32, (tq, nc), 1) * CMP_STRIDE + (CMP_BLOCK - 1)
    vis = c_pos <= t_row
    tiles = range(NSA_TILES)
    s = [_mm(qs_ref[r], kc, _NT) for r in tiles]
    p = []
    for r in tiles:
        sr = jnp.where(vis, s[r], MASKED)
        m = jnp.max(sr, axis=-1, keepdims=True)
        e = jnp.where(vis, jnp.exp(sr - m), 0.0)
        p.append(e / jnp.maximum(jnp.sum(e, axis=-1, keepdims=True), 1e-30))
    _merge_groups(o_ref, [_mm(p[r], vc, _NN) for r in tiles], gate_ref, 0)

    jrow = lax.broadcasted_iota(jnp.int32, (ns, nc), 0)
    ccol = lax.broadcasted_iota(jnp.int32, (ns, nc), 1)
    pool = ((ccol >= SLC_PER_CMP * jrow - CMP_LEAD) & (ccol < SLC_PER_CMP * (jrow + 1))).astype(jnp.bfloat16)
    jb = lax.broadcasted_iota(jnp.int32, (ns, tq), 0)
    jt = (t0 + lax.broadcasted_iota(jnp.int32, (ns, tq), 1)) // SLC_BLOCK
    forced = (jb == 0) | (jb == jt) | (jb == jt - 1)
    sel_t = []
    for g in range(NSA_KV_GROUPS):
        pg = p[g]
        for i in range(1, NSA_SLABS):
            pg = pg + p[2 * i + g]
        pg_hi = pg.astype(jnp.bfloat16)
        pg_lo = (pg - pg_hi.astype(jnp.float32)).astype(jnp.bfloat16)
        p_slc = (lax.dot_general(pool, pg_hi, _NT, preferred_element_type=jnp.float32)
                 + lax.dot_general(pool, pg_lo, _NT, preferred_element_type=jnp.float32))
        score = jnp.where(jb > jt, -jnp.inf, jnp.where(forced, FORCED_SCORE, p_slc))
        rank = jnp.zeros((ns, tq), jnp.float32)
        for i in range(ns):
            row = score[i:i + 1, :]
            tie = jnp.where(jb > i, 1.0, 0.0)
            rank = rank + jnp.where(row > score, 1.0, 0.0) + jnp.where(row == score, tie, 0.0)
        sel_t.append(jnp.where(rank < min(SLC_TOPN, ns), 1.0, 0.0))
        if ns < SEL_LANES:
            sel_t.append(jnp.zeros((SEL_LANES - ns, tq), jnp.float32))
    sel_ref[0] = jnp.concatenate(sel_t, axis=0).T.astype(sel_ref.dtype)


def _nsa_dense_kernel(q_ref, k_ref, v_ref, sel_ref, gate_ref, o_ref, qs_ref, m_ref, l_ref, acc_ref, *, windowed):
    qi = pl.program_id(1)
    tq = q_ref.shape[1]
    t0 = qi * tq
    _group_tiles(q_ref, qs_ref)
    m_ref[...] = jnp.full(m_ref.shape, MASKED, jnp.float32)
    l_ref[...] = jnp.zeros(l_ref.shape, jnp.float32)
    acc_ref[...] = jnp.zeros(acc_ref.shape, jnp.float32)
    t_row = t0 + lax.broadcasted_iota(jnp.int32, (tq, KEY_BLOCK), 0)
    k_lane = lax.broadcasted_iota(jnp.int32, (tq, KEY_BLOCK), 1)
    tiles = range(NSA_TILES)
    first = jnp.maximum(qi - WINDOW // KEY_BLOCK, 0) if windowed else 0

    def key_block(kb, carry):
        start = pl.multiple_of(kb * KEY_BLOCK, KEY_BLOCK)
        kblk = k_ref[0, pl.ds(start, KEY_BLOCK), :]
        vblk = v_ref[0, pl.ds(start, KEY_BLOCK), :]
        k_pos = start + k_lane
        if windowed:
            allowed = [(k_pos <= t_row) & (k_pos >= t_row - WINDOW)] * NSA_KV_GROUPS
        else:
            sel = sel_ref[0]
            n_sel = NSA_KV_GROUPS * SEL_LANES
            erow = lax.broadcasted_iota(jnp.int32, (n_sel, KEY_BLOCK), 0)
            ecol = lax.broadcasted_iota(jnp.int32, (n_sel, KEY_BLOCK), 1)
            blk = kb * (KEY_BLOCK // SLC_BLOCK) + ecol // SLC_BLOCK
            allowed = []
            for g in range(NSA_KV_GROUPS):
                expand = (erow == blk + g * SEL_LANES).astype(jnp.bfloat16)
                picked = lax.dot_general(sel, expand, _NN, preferred_element_type=jnp.float32)
                allowed.append((picked > 0.5) & (k_pos <= t_row))
        s = [_mm(qs_ref[r], kblk, _NT) for r in tiles]
        p = []
        alpha = []
        for r in tiles:
            ok = allowed[r % NSA_KV_GROUPS]
            sr = jnp.where(ok, s[r], MASKED)
            m_prev = m_ref[r]
            m_new = jnp.maximum(m_prev, jnp.max(sr, axis=-1, keepdims=True))
            a = jnp.exp(m_prev - m_new)
            pr = jnp.where(ok, jnp.exp(sr - m_new), 0.0)
            l_ref[r] = a * l_ref[r] + jnp.sum(pr, axis=-1, keepdims=True)
            m_ref[r] = m_new
            alpha.append(a)
            p.append(pr)
        pv = [_mm(p[r], vblk, _NN) for r in tiles]
        for r in tiles:
            acc_ref[r] = alpha[r] * acc_ref[r] + pv[r]
        return carry

    lax.fori_loop(first, qi * (tq // KEY_BLOCK) + tq // KEY_BLOCK, key_block, 0)
    _merge_groups(o_ref, [acc_ref[r] / jnp.maximum(l_ref[r], 1e-30) for r in tiles], gate_ref,
                  2 if windowed else 1)


def nsa_prompt_attention(q, q_rot, kc, vc, kv_att, gates):
    n, T, D = q.shape
    tq = Q_BLOCK
    qspec = pl.BlockSpec((1, tq, D), lambda b, i: (b, i, 0))
    whole = lambda a: pl.BlockSpec((1,) + a.shape[1:], lambda b, i: (b, 0, 0))
    lanes = lambda j: pl.BlockSpec((1, T, SLAB), lambda b, i: (b, 0, j))
    n_sel = NSA_KV_GROUPS * SEL_LANES
    sel_spec = pl.BlockSpec((1, tq, n_sel), lambda b, i: (b, i, 0))
    gate_spec = pl.BlockSpec((1, tq, gates.shape[2]), lambda b, i: (b, i, 0))
    params = pltpu.CompilerParams(dimension_semantics=("parallel", "arbitrary"),
                                  vmem_limit_bytes=48 * 1024 * 1024)
    qs_scratch = pltpu.VMEM((NSA_TILES, tq, SLAB), jnp.bfloat16)
    o_cmp, sel = pl.pallas_call(
        _nsa_cmp_kernel,
        grid=(n, T // tq),
        in_specs=[qspec, whole(kc), whole(vc), gate_spec],
        out_specs=[qspec, sel_spec],
        out_shape=[jax.ShapeDtypeStruct((n, T, D), jnp.float32),
                   jax.ShapeDtypeStruct((n, T, n_sel), jnp.bfloat16)],
        scratch_shapes=[qs_scratch],
        compiler_params=params,
        name="nsa_cmp_select",
    )(q, kc, vc, gates)
    stat = pltpu.VMEM((NSA_TILES, tq, KEY_BLOCK), jnp.float32)

    def dense(windowed, first_lanes, name):
        return pl.pallas_call(
            functools.partial(_nsa_dense_kernel, windowed=windowed),
            grid=(n, T // tq),
            in_specs=[qspec, lanes(first_lanes), lanes(first_lanes + 1), sel_spec, gate_spec],
            out_specs=qspec,
            out_shape=jax.ShapeDtypeStruct((n, T, D), jnp.float32),
            scratch_shapes=[qs_scratch, stat, stat, stat],
            compiler_params=params,
            name=name,
        )(q_rot, kv_att, kv_att, sel, gates)

    return o_cmp, dense(False, 0, "nsa_selected"), dense(True, 2, "nsa_window")


def _rope_tables(pos):
    half = ROT_DIM // 2
    inv = ROPE_THETA ** (-2.0 * jnp.arange(half, dtype=jnp.float32) / ROT_DIM)
    ang = pos.astype(jnp.float32)[:, None] * inv[None, :]
    rest = NSA_HEAD_DIM - ROT_DIM
    cos = jnp.concatenate([jnp.cos(ang), jnp.cos(ang), jnp.ones((pos.shape[0], rest), jnp.float32)], axis=1)
    sin = jnp.concatenate([-jnp.sin(ang), jnp.sin(ang), jnp.zeros((pos.shape[0], rest), jnp.float32)], axis=1)
    return jnp.tile(cos, (1, NSA_KV_GROUPS)), jnp.tile(sin, (1, NSA_KV_GROUPS))


def _nsa_pre_kernel(x_ref, gkv_ref, gmix_ref, cos_ref, sin_ref, wkv_ref, wq_ref, wg_ref,
                    cmp_ref, slc_ref, win_ref, kvb_ref, q_ref, qr_ref, gate_ref):
    xh = _rms(x_ref[0])
    cos, sin = cos_ref[...], sin_ref[...]
    low = lax.broadcasted_iota(jnp.int32, cos.shape, 1) % NSA_HEAD_DIM < ROT_DIM // 2

    def rope(t):
        swapped = jnp.where(low, pltpu.roll(t, SLAB - ROT_DIM // 2, 1), pltpu.roll(t, ROT_DIM // 2, 1))
        return t * cos + swapped * sin

    kv = _bdot(xh * gkv_ref[...], wkv_ref[...])
    part = lambda j: kv[:, j * SLAB:(j + 1) * SLAB]
    k_slc, k_win = rope(part(2)), rope(part(4))
    cmp_ref[0] = kv[:, :2 * SLAB]
    slc_ref[0] = jnp.concatenate([k_slc, part(3)], axis=-1)
    win_ref[0] = jnp.concatenate([k_win, part(5)], axis=-1)
    kvb_ref[0] = jnp.concatenate([k_slc, part(3), k_win, part(5)], axis=-1).astype(jnp.bfloat16)
    h = xh * gmix_ref[...]
    q = _bdot(h, wq_ref[...])
    q_ref[0] = q.astype(jnp.bfloat16)
    qr_ref[0] = jnp.concatenate([rope(q[:, i * SLAB:(i + 1) * SLAB]) for i in range(NSA_SLABS)],
                                axis=-1).astype(jnp.bfloat16)
    gate_ref[0] = jax.nn.sigmoid(_bdot(h, wg_ref[...]))


def nsa_pre(x, pos, norm_kv, norm_mix, w_kv, w_in):
    n, T, D = x.shape
    tm = min(ROW_TILE, T)
    nq = NSA_HEADS * NSA_HEAD_DIM
    cos, sin = _rope_tables(pos)
    w_q = (_to_slabs(w_in[:, :nq]) * NSA_HEAD_DIM ** -0.5).astype(jnp.bfloat16)
    w_g = jnp.pad(w_in[:, nq:], ((0, 0), (0, SLAB - (w_in.shape[1] - nq)))).astype(jnp.bfloat16)
    seq = lambda width: pl.BlockSpec((1, tm, width), lambda b, t: (b, t, 0))
    full = lambda a: pl.BlockSpec(a.shape, lambda b, t: (0,) * a.ndim)
    table = pl.BlockSpec((tm, SLAB), lambda b, t: (t, 0))
    consts = [norm_kv.reshape(1, D), norm_mix.reshape(1, D)]
    weights = [w_kv.astype(jnp.bfloat16), w_q, w_g]
    widths = [2 * SLAB, 2 * SLAB, 2 * SLAB, 4 * SLAB, nq, nq, SLAB]
    dtypes = [jnp.float32] * 3 + [jnp.bfloat16] * 3 + [jnp.float32]
    return pl.pallas_call(
        _nsa_pre_kernel,
        grid=(n, T // tm),
        in_specs=[seq(D)] + [full(a) for a in consts] + [table, table] + [full(a) for a in weights],
        out_specs=[seq(w) for w in widths],
        out_shape=[jax.ShapeDtypeStruct((n, T, w), dt) for w, dt in zip(widths, dtypes)],
        compiler_params=pltpu.CompilerParams(dimension_semantics=("parallel", "parallel"),
                                             vmem_limit_bytes=48 * 1024 * 1024),
        name="nsa_pre",
    )(x, *consts, cos, sin, *weights)


def _to_slabs(x):
    lead = x.shape[:-1]
    x = x.reshape(lead + (NSA_KV_GROUPS, NSA_GROUP_HEADS, NSA_HEAD_DIM))
    return jnp.swapaxes(x, -3, -2).reshape(lead + (NSA_HEADS * NSA_HEAD_DIM,))


def _from_slabs(x):
    lead = x.shape[:-1]
    x = x.reshape(lead + (NSA_GROUP_HEADS, NSA_KV_GROUPS, NSA_HEAD_DIM))
    return jnp.swapaxes(x, -3, -2).reshape(lead + (NSA_HEADS * NSA_HEAD_DIM,))


def nsa_layer_prompt(x, norm_kv, norm_mix, w_kv, cmp_pos, cmp_w1, cmp_b1, cmp_w2, w_in, w_o, norm_next, w_router):
    n, T, D = x.shape
    cmp_kv, slc_kv, win_kv, kv_att, q, q_rot, gates = nsa_pre(x, jnp.arange(T), norm_kv, norm_mix, w_kv, w_in)
    hidden = rows_chunk_hidden(cmp_kv, _chunk_weights(cmp_w1))
    kc, vc = compress_from_hidden(hidden, cmp_pos, cmp_w1, cmp_b1, cmp_w2)
    lanes = lambda a: a.reshape(n, a.shape[1], SLAB).astype(jnp.bfloat16)
    branches = nsa_prompt_attention(q, q_rot, lanes(kc), lanes(vc), kv_att, gates)
    flat = lambda a: a.reshape(n * T, a.shape[-1])
    x_out, h_out, logits = out_proj([flat(o) for o in branches], None, flat(x), _to_slabs(w_o.T).T, norm_next,
                                    w_router)
    return x_out, h_out, logits, cmp_kv, slc_kv, win_kv


def sample_kv_context(cmp_new, slc_new, win_new, cache_cmp_kv, cache_slc_kv, cache_win_kv, page_table,
                      cmp_pos, cmp_w1, cmp_b1, cmp_w2):
    n, S = cmp_new.shape[:2]
    n_new_blk = -(-S // SLC_BLOCK)
    pad = ((0, 0), (0, n_new_blk * SLC_BLOCK - S), (0, 0), (0, 0), (0, 0))
    w_chunk = _chunk_weights(cmp_w1)
    hidden = jnp.concatenate([paged_chunk_hidden(cache_cmp_kv, page_table, w_chunk),
                              rows_chunk_hidden(jnp.pad(cmp_new.astype(cache_cmp_kv.dtype), pad), w_chunk)], axis=1)
    kc, vc = compress_from_hidden(hidden, cmp_pos, cmp_w1, cmp_b1, cmp_w2)
    pool = cache_slc_kv.transpose(0, 2, 3, 4, 1)
    new_blocks = jnp.pad(slc_new, pad).reshape((n, n_new_blk, SLC_BLOCK) + slc_new.shape[2:])
    win_all = jnp.concatenate([cache_win_kv, win_new.astype(cache_win_kv.dtype)], axis=1)
    return kc, vc, pool, new_blocks, win_all


def sample_nsa(h, ctx, page_table, w_in, w_o):
    kc, vc, pool, new_blocks, win_all = ctx
    n, S, _ = h.shape
    q, gate = nsa_query(h, w_in, precise=True)
    t_pos = PAST_LEN + jnp.arange(S)
    sub = PAGE_SIZE // SLC_BLOCK
    n_past_blk = PAST_LEN // SLC_BLOCK
    n_new_blk = new_blocks.shape[1]
    bi = jnp.arange(n)[:, None, None, None]
    gi = jnp.arange(NSA_KV_GROUPS)[None, :, None, None]

    def gather_sel(idx):
        jp = jnp.minimum(idx, n_past_blk - 1)
        phys = page_table[bi, jp // sub]
        planes = pool[phys, :, gi]
        halves = planes.reshape(planes.shape[:-1] + (sub, SLC_BLOCK))
        pick = (jp % sub)[..., None, None, None, None] == jnp.arange(sub)[:, None]
        from_past = jnp.swapaxes(jnp.sum(jnp.where(pick, halves, 0.0), axis=-2), -1, -2)
        from_past = jnp.swapaxes(from_past, -3, -2)
        jn = jnp.clip(idx - n_past_blk, 0, n_new_blk - 1)
        from_new = new_blocks[bi, jn, :, :, gi].astype(from_past.dtype)
        blk = jnp.where((idx >= n_past_blk)[..., None, None, None], from_new, from_past)
        return blk[..., 0, :], blk[..., 1, :]

    wb = win_all.shape[1] - S
    s_pos = PAST_LEN - wb + jnp.arange(win_all.shape[1])
    with jax.default_matmul_precision("highest"):
        o = nsa_branches(q, gate, t_pos, kc, vc, gather_sel, win_all[:, :, 0], win_all[:, :, 1], s_pos)
    return dense(o.reshape(n, S, NSA_HEADS * NSA_HEAD_DIM), w_o, precise=True)


def kernel(x_prompt, x_sample, state_wkv, state_shift, cache_cmp_kv, cache_slc_kv, cache_win_kv, page_table, norm_mix, norm_ffn, norm_kv, norm_final, rw_mu, rw_w_rkv, rw_w0, rw_w1, rw_w2, rw_a0, rw_a1, rw_a2, rw_g1, rw_g2, rw_k_k, rw_k_a, rw_r_k, rw_gn_w, rw_gn_b, rw_w_o, nsa_w_kv, nsa_cmp_pos, nsa_cmp_w1, nsa_cmp_b1, nsa_cmp_w2, nsa_w_in, nsa_w_o, ffn_w_gu, ffn_w_down, moe_router, moe_w_gu, moe_w_down):
    cmp_params = (nsa_cmp_pos, nsa_cmp_w1, nsa_cmp_b1, nsa_cmp_w2)
    assert DEPTH == 2 and N_A_LAYERS == 1
    D = D_MODEL

    n_p, T = x_prompt.shape[:2]
    x1, h1, wkv_fin, h_last = rwkv7_layer(
        x_prompt, jnp.zeros((n_p, D), x_prompt.dtype), jnp.zeros((n_p, RWKV_HEADS, RWKV_HEAD, RWKV_HEAD), jnp.float32),
        norm_mix[0], rw_mu[0], rw_w_rkv[0], rw_w0[0], rw_w1[0], rw_w2[0], rw_a0[0], rw_a1[0], rw_a2[0],
        rw_g1[0], rw_g2[0], rw_k_k[0], rw_k_a[0], rw_r_k[0], rw_gn_w[0], rw_gn_b[0], rw_w_o[0], norm_ffn[0])
    wkv_p, shift_p = wkv_fin[None], h_last[None]
    x2 = swiglu_residual(x1, h1, ffn_w_gu[0], ffn_w_down[0])
    x_p, h_p, logits_p, cmp_rows, slc_rows, win_rows = nsa_layer_prompt(
        x2.reshape(n_p, T, D), norm_kv, norm_mix[1], nsa_w_kv, *cmp_params, nsa_w_in[0], nsa_w_o[0], norm_ffn[1],
        moe_router[0])
    kv_shape = (n_p, T, 2, NSA_KV_GROUPS, NSA_HEAD_DIM)
    cmp_kv_p, slc_kv_p = cmp_rows.reshape(kv_shape), slc_rows.reshape(kv_shape)
    win_kv_p = win_rows.reshape(kv_shape)[:, -min(WINDOW, T):]

    pos_s = PAST_LEN + jnp.arange(x_sample.shape[1], dtype=jnp.int32)
    h = rmsnorm(x_sample, norm_mix[0])
    y, s_fin, h_last_s = rwkv7_time_mix(
        h, state_shift[0], state_wkv[0], rw_mu[0], rw_w_rkv[0], rw_w0[0], rw_w1[0], rw_w2[0],
        rw_a0[0], rw_a1[0], rw_a2[0], rw_g1[0], rw_g2[0], rw_k_k[0], rw_k_a[0],
        rw_r_k[0], rw_gn_w[0], rw_gn_b[0], rw_w_o[0], precise=True)
    wkv_s, shift_s = s_fin[None], h_last_s[None]
    x_s = x_sample + y
    x_s = x_s + swiglu(rmsnorm(x_s, norm_ffn[0]), ffn_w_gu[0], ffn_w_down[0], precise=True)
    cmp_kv_s, slc_kv_s, win_new = shared_kv_rows(x_s, norm_kv, nsa_w_kv, pos_s, precise=True)
    ctx_s = sample_kv_context(cmp_kv_s, slc_kv_s, win_new, cache_cmp_kv, cache_slc_kv, cache_win_kv, page_table,
                              *cmp_params)
    x_s = x_s + sample_nsa(rmsnorm(x_s, norm_mix[1]), ctx_s, page_table, nsa_w_in[0], nsa_w_o[0])
    win_kv_s = ctx_s[4][:, -cache_win_kv.shape[1]:]
    x_s = x_s.reshape(-1, D)
    h_s = rmsnorm(x_s, norm_ffn[1])
    logits_s = matmul_precise(h_s, moe_router[0])

    n_tok_p = n_p * T
    logits = jnp.concatenate([logits_p[:, :N_EXPERTS], logits_s], axis=0)
    y0, y1, gate = moe_swiglu(jnp.concatenate([h_p, h_s.astype(jnp.bfloat16)], axis=0), logits,
                              moe_w_gu[0], moe_w_down[0])
    gate = jnp.pad(gate, ((0, 0), (0, 128 - TOP_K)))
    y_prompt = combine_norm(x_p, y0[:n_tok_p], y1[:n_tok_p], gate[:n_tok_p], norm_final).reshape(x_prompt.shape)
    y_sample = combine_norm(x_s, y0[n_tok_p:], y1[n_tok_p:], gate[n_tok_p:], norm_final).reshape(x_sample.shape)


    return (y_prompt, y_sample, wkv_p, shift_p, cmp_kv_p, slc_kv_p, win_kv_p,
            wkv_s, shift_s, cmp_kv_s, slc_kv_s, win_kv_s)
```

```python
import functools
import math

import jax
import jax.numpy as jnp
from jax import lax
from jax.experimental import pallas as pl
from jax.experimental.pallas import tpu as pltpu

D_MODEL = 1024
DEPTH = 2
PAST_LEN = 16384
PAGE_SIZE = 128
N_A_LAYERS = DEPTH // 2
RWKV_HEAD = 64
RWKV_HEADS = D_MODEL // RWKV_HEAD
DECAY_SCALE = math.exp(-0.5)
GN_EPS = RWKV_HEAD * 1e-5
NSA_HEADS = 16
NSA_HEAD_DIM = 64
NSA_KV_GROUPS = 2
NSA_GROUP_HEADS = NSA_HEADS // NSA_KV_GROUPS
N_BRANCH = 3
CMP_BLOCK = 32
CMP_STRIDE = 16
CMP_HIDDEN = 128
SLC_BLOCK = 64
SLC_TOPN = 16
WINDOW = 512
Q_BLOCK = 128
FORCED_SCORE = 1e4
ROPE_THETA = 500000.0
ROT_DIM = NSA_HEAD_DIM // 4
N_EXPERTS = 8
TOP_K = 2
MOE_BLOCK = 256
NORM_EPS = 1e-6


def _rmsnorm_kernel(x_ref, g_ref, o_ref):
    x = x_ref[...]
    y = x * lax.rsqrt(jnp.mean(x * x, axis=-1, keepdims=True) + NORM_EPS)
    o_ref[...] = y * g_ref[...]


def rmsnorm(x, g):
    shp = x.shape
    x2 = x.reshape(-1, shp[-1])
    rows = x2.shape[0]
    tm = min(rows, 512)
    out = pl.pallas_call(
        _rmsnorm_kernel,
        grid=(rows // tm,),
        in_specs=[pl.BlockSpec((tm, shp[-1]), lambda i: (i, 0)),
                  pl.BlockSpec((1, shp[-1]), lambda i: (0, 0))],
        out_specs=pl.BlockSpec((tm, shp[-1]), lambda i: (i, 0)),
        out_shape=jax.ShapeDtypeStruct(x2.shape, x.dtype),
        name="rmsnorm",
    )(x2, g.reshape(1, -1))
    return out.reshape(shp)


def rope_partial(x, pos):
    half = ROT_DIM // 2
    inv = ROPE_THETA ** (-2.0 * jnp.arange(half, dtype=jnp.float32) / ROT_DIM)
    ang = pos.astype(jnp.float32)[:, None] * inv[None, :]
    shape = (1, pos.shape[0]) + (1,) * (x.ndim - 3) + (half,)
    cos = jnp.cos(ang).reshape(shape)
    sin = jnp.sin(ang).reshape(shape)
    xf = x.astype(jnp.float32)
    x1, x2 = xf[..., :half], xf[..., half:ROT_DIM]
    out = jnp.concatenate([x1 * cos - x2 * sin, x2 * cos + x1 * sin, xf[..., ROT_DIM:]], axis=-1)
    return out.astype(x.dtype)


def masked_softmax(s, mask):
    s = jnp.where(mask, s.astype(jnp.float32), -jnp.inf)
    m = jnp.max(s, axis=-1, keepdims=True)
    m = jnp.where(jnp.isfinite(m), m, 0.0)
    e = jnp.where(mask, jnp.exp(s - m), 0.0)
    return e / jnp.maximum(jnp.sum(e, axis=-1, keepdims=True), 1e-30)


FF_CHUNK = 1408
SWIGLU_ROWS = 512


def _swiglu_kernel(blk_e_ref, n_used_ref, x_ref, wg_ref, wu_ref, wd_ref, *rest):
    o_ref = rest[-1]
    i = pl.program_id(0)
    f = pl.program_id(1)

    @pl.when(i < n_used_ref[0])
    def _():
        x = x_ref[...]
        g = jnp.dot(x, wg_ref[0], preferred_element_type=jnp.float32)
        u = jnp.dot(x, wu_ref[0], preferred_element_type=jnp.float32)
        act = (g * jax.nn.sigmoid(g) * u).astype(jnp.bfloat16)
        y = jnp.dot(act, wd_ref[0], preferred_element_type=jnp.float32)

        @pl.when(f == 0)
        def _():
            o_ref[...] = y + rest[0][...] if len(rest) == 2 else y

        @pl.when(f > 0)
        def _():
            o_ref[...] += y

    @pl.when(i >= n_used_ref[0])
    def _():
        o_ref[...] = jnp.zeros(o_ref.shape, o_ref.dtype)


def grouped_swiglu(xb, blk_e, n_used, w_gu, w_down, res=None):
    rows, d = xb.shape
    extra = [] if res is None else [res]
    b = min(SWIGLU_ROWS, rows)
    n_blk = rows // b
    ff = w_down.shape[1]
    tf = FF_CHUNK
    n_f = ff // tf
    chunk = lambda i, f, be, nu: jnp.where(i < nu[0], f, n_f - 1)
    return pl.pallas_call(
        _swiglu_kernel,
        grid_spec=pltpu.PrefetchScalarGridSpec(
            num_scalar_prefetch=2,
            grid=(n_blk, n_f),
            in_specs=[
                pl.BlockSpec((b, d), lambda i, f, be, nu: (i, 0)),
                pl.BlockSpec((1, d, tf), lambda i, f, be, nu: (be[i], 0, chunk(i, f, be, nu))),
                pl.BlockSpec((1, d, tf), lambda i, f, be, nu: (be[i], 0, n_f + chunk(i, f, be, nu))),
                pl.BlockSpec((1, tf, d), lambda i, f, be, nu: (be[i], chunk(i, f, be, nu), 0)),
            ] + [pl.BlockSpec((b, d), lambda i, f, be, nu: (i, 0))] * len(extra),
            out_specs=pl.BlockSpec((b, d), lambda i, f, be, nu: (i, 0)),
        ),
        out_shape=jax.ShapeDtypeStruct((rows, d), jnp.float32),
        compiler_params=pltpu.CompilerParams(dimension_semantics=("arbitrary", "arbitrary"),
                                             vmem_limit_bytes=56 * 1024 * 1024),
        name="grouped_swiglu",
    )(blk_e.astype(jnp.int32), jnp.reshape(n_used, (1,)).astype(jnp.int32), xb, w_gu, w_gu, w_down, *extra)


def swiglu(h, w_gu, w_down, precise=False):
    g, u = jnp.split(dense(h, w_gu, precise), 2, axis=-1)
    return dense(jax.nn.silu(g) * u, w_down, precise)


def swiglu_residual(x, h, w_gu, w_down):
    n_blk = h.shape[0] // min(SWIGLU_ROWS, h.shape[0])
    return grouped_swiglu(h, jnp.zeros((n_blk,), jnp.int32), jnp.int32(n_blk),
                          w_gu.astype(jnp.bfloat16)[None], w_down.astype(jnp.bfloat16)[None], res=x)


def _combine_norm_kernel(x_ref, y0_ref, y1_ref, g_ref, gain_ref, o_ref):
    g = g_ref[...]
    x = x_ref[...] + (y0_ref[...] * g[:, 0:1] + y1_ref[...] * g[:, 1:2])
    o_ref[...] = _rms(x) * gain_ref[...]


def combine_norm(x, y0, y1, gate, gain):
    m, d = x.shape
    tm = min(2 * ROW_TILE, m)
    rows = lambda width: pl.BlockSpec((tm, width), lambda i: (i, 0))
    return pl.pallas_call(
        _combine_norm_kernel,
        grid=(m // tm,),
        in_specs=[rows(d), rows(d), rows(d), rows(gate.shape[1]), pl.BlockSpec((1, d), lambda i: (0, 0))],
        out_specs=rows(d),
        out_shape=jax.ShapeDtypeStruct((m, d), jnp.float32),
        compiler_params=pltpu.CompilerParams(dimension_semantics=("parallel",)),
        name="combine_norm",
    )(x, y0, y1, gate, gain.reshape(1, d))


def moe_swiglu(xt, logits, w_gu_e, w_down_e):
    n_tok, d = xt.shape
    b = SWIGLU_ROWS
    experts = jnp.arange(N_EXPERTS)[None, :]
    e0 = jnp.argmax(logits, axis=-1)
    v0 = jnp.max(logits, axis=-1)
    rest = jnp.where(experts == e0[:, None], -jnp.inf, logits)
    e1 = jnp.argmax(rest, axis=-1)
    v1 = jnp.max(rest, axis=-1)
    top_idx = jnp.stack([e0, e1], axis=-1).astype(jnp.int32)
    gate = jax.nn.softmax(jnp.stack([v0, v1], axis=-1), axis=-1)
    nk = n_tok * TOP_K
    flat_e = top_idx.reshape(nk)
    onehot = (flat_e[:, None] == jnp.arange(N_EXPERTS)[None, :]).astype(jnp.int32)
    before = jnp.cumsum(onehot, axis=0) - onehot
    counts = jnp.sum(onehot, axis=0)
    padded = (counts + b - 1) // b * b
    ends_pad = jnp.cumsum(padded)
    starts_pad = ends_pad - padded
    dest = jnp.sum(onehot * (starts_pad[None, :] + before), axis=1)
    n_rows = (nk + b - 1) // b * b + N_EXPERTS * b
    n_blk = n_rows // b
    flat_tok = jnp.repeat(jnp.arange(n_tok, dtype=jnp.int32), TOP_K)
    row_tok = jnp.full((n_rows,), n_tok, jnp.int32).at[dest].set(flat_tok)
    blk_e = jnp.minimum(jnp.searchsorted(ends_pad, jnp.arange(n_blk) * b, side='right'), N_EXPERTS - 1)
    x_pad = jnp.concatenate([xt.astype(jnp.bfloat16), jnp.zeros((1, d), jnp.bfloat16)], axis=0)
    yb = grouped_swiglu(x_pad[row_tok], blk_e, ends_pad[-1] // b,
                        w_gu_e.astype(jnp.bfloat16), w_down_e.astype(jnp.bfloat16))
    dest = dest.reshape(n_tok, TOP_K)
    return yb[dest[:, 0]], yb[dest[:, 1]], gate


WKV_CHUNK = 64
WKV_HEADS_PER_STEP = 16

_NN = (((1,), (0,)), ((), ()))
_NT = (((1,), (1,)), ((), ()))
_TN = (((0,), (0,)), ((), ()))


def _mm(x, y, dims):
    return lax.dot_general(x.astype(jnp.bfloat16), y.astype(jnp.bfloat16), dims,
                           preferred_element_type=jnp.float32)


def _wkv7_chunk_kernel(r_ref, lw_ref, k_ref, v_ref, kk_ref, a_ref, rk_ref, gnw_ref, gnb_ref, s0_ref,
                       y_ref, sout_ref, state_ref):
    c = pl.program_id(2)
    L = r_ref.shape[1]
    hb = state_ref.shape[0]
    N = RWKV_HEAD

    @pl.when(c == 0)
    def _():
        state_ref[...] = s0_ref[0]

    row = lax.broadcasted_iota(jnp.int32, (L, L), 0)
    col = lax.broadcasted_iota(jnp.int32, (L, L), 1)
    strict = row > col
    incl = row >= col
    tri = incl.astype(jnp.bfloat16)
    eye_n = lax.broadcasted_iota(jnp.int32, (N, N), 0) == lax.broadcasted_iota(jnp.int32, (N, N), 1)

    lw = lw_ref[0]
    lw_hi, lw_lo = _split_bf16(lw)
    cum = (lax.dot_general(tri, lw_hi, _NN, preferred_element_type=jnp.float32)
           + lax.dot_general(tri, lw_lo, _NN, preferred_element_type=jnp.float32))
    cum_last = cum[L - 1:L, :]
    e_neg = jnp.exp(-cum)
    e_tail = jnp.exp(cum_last - cum)
    e_prev = jnp.exp(cum - lw)
    r_in = r_ref[0]
    k_in = k_ref[0]
    r_all = r_in * jnp.exp(cum)
    kt_all = k_in * e_neg
    kh_all = k_in * e_tail
    wl_all = jnp.exp(cum_last)

    heads = range(hb)
    per_head = lambda t: [t[:, j * N:(j + 1) * N] for j in heads]
    rt, kt, kh, v, wl = (per_head(t) for t in (r_all, kt_all, kh_all, v_ref[0], wl_all))
    kk_raw, a_gate, e_prev, e_neg, e_tail = (per_head(t) for t in (kk_ref[0], a_ref[0], e_prev, e_neg, e_tail))
    at, bt, bh = [], [], []
    for j in heads:
        norm = jnp.sqrt(jnp.sum(kk_raw[j] * kk_raw[j], axis=-1, keepdims=True))
        kk = kk_raw[j] / jnp.maximum(norm, 1e-12)
        b = kk * a_gate[j]
        at.append(-kk * e_prev[j])
        bt.append(b * e_neg[j])
        bh.append(b * e_tail[j])
    a_ab = [jnp.where(strict, _mm(at[j], bt[j], _NT), 0.0) for j in heads]
    a_ak = [jnp.where(strict, _mm(at[j], kt[j], _NT), 0.0) for j in heads]
    r_b = [jnp.where(incl, _mm(rt[j], bt[j], _NT), 0.0) for j in heads]
    r_k = [jnp.where(incl, _mm(rt[j], kt[j], _NT), 0.0) for j in heads]
    av = [_mm(a_ak[j], v[j], _NN) for j in heads]
    ht = [_mm(v[j], kh[j], _TN) for j in heads]
    yp = [_mm(r_k[j], v[j], _NN) for j in heads]
    pw = a_ab
    inv_a = a_ab
    n = 1
    while 2 * n < L:
        pw = [_mm(pw[j], pw[j], _NN) for j in heads]
        inv_a = [inv_a[j] + pw[j] + _mm(inv_a[j], pw[j], _NN) for j in heads]
        n *= 2
    ap = [at[j] + _mm(inv_a[j], at[j], _NN) for j in heads]
    vp = [av[j] + _mm(inv_a[j], av[j], _NN) for j in heads]
    g = [jnp.where(eye_n, wl[j], 0.0) + _mm(bh[j], ap[j], _TN) for j in heads]
    ht = [ht[j] + _mm(vp[j], bh[j], _TN) for j in heads]
    rp = [rt[j] + _mm(r_b[j], ap[j], _NN) for j in heads]
    yp = [yp[j] + _mm(r_b[j], vp[j], _NN) for j in heads]
    s_prev = [state_ref[j] for j in heads]
    y = [_mm(rp[j], s_prev[j], _NT) + yp[j] for j in heads]
    for j in heads:
        state_ref[j] = _mm(s_prev[j], g[j], _NT) + ht[j]
    r_raw, k_raw = per_head(r_in), per_head(k_in)
    rk, gnw, gnb = per_head(rk_ref[...]), per_head(gnw_ref[...]), per_head(gnb_ref[...])
    out = []
    for j in heads:
        mean = jnp.mean(y[j], axis=-1, keepdims=True)
        cen = y[j] - mean
        var = jnp.mean(cen * cen, axis=-1, keepdims=True)
        bonus = jnp.sum(r_raw[j] * k_raw[j] * rk[j], axis=-1, keepdims=True) * v[j]
        out.append(cen * lax.rsqrt(var + GN_EPS) * gnw[j] + gnb[j] + bonus)
    y_ref[0] = jnp.concatenate(out, axis=-1)

    @pl.when(c == pl.num_programs(2) - 1)
    def _():
        sout_ref[0] = state_ref[...]


def wkv7_chunked(r, lw, k, v, kk, a_gate, r_k, gn_w, gn_b, s0):
    n, T, D = r.shape
    L = WKV_CHUNK
    hb = WKV_HEADS_PER_STEP
    w = hb * RWKV_HEAD
    seq = pl.BlockSpec((1, L, w), lambda b, h, c: (b, c, h))
    vec = pl.BlockSpec((1, w), lambda b, h, c: (0, h))
    st = pl.BlockSpec((1, hb, RWKV_HEAD, RWKV_HEAD), lambda b, h, c: (b, h, 0, 0))
    row = lambda t: t.reshape(1, D).astype(jnp.float32)
    return pl.pallas_call(
        _wkv7_chunk_kernel,
        grid=(n, D // w, T // L),
        in_specs=[seq] * 6 + [vec] * 3 + [st],
        out_specs=[seq, st],
        out_shape=[jax.ShapeDtypeStruct((n, T, D), jnp.float32),
                   jax.ShapeDtypeStruct(s0.shape, jnp.float32)],
        scratch_shapes=[pltpu.VMEM((hb, RWKV_HEAD, RWKV_HEAD), jnp.float32)],
        compiler_params=pltpu.CompilerParams(dimension_semantics=("parallel", "parallel", "arbitrary")),
        name="wkv7_chunked",
    )(r, lw, k, v, kk, a_gate, row(r_k), row(gn_w), row(gn_b), s0)


def wkv7_scan(r, lw, k, v, a_vec, b_vec, s0):
    def step(S, inp):
        r_t, lw_t, k_t, v_t, a_t, b_t = inp
        sa = jnp.sum(S * a_t[:, :, None, :], axis=-1)
        S = S * jnp.exp(lw_t)[:, :, None, :] + sa[..., None] * b_t[:, :, None, :] + v_t[..., None] * k_t[:, :, None, :]
        return S, jnp.sum(S * r_t[:, :, None, :], axis=-1)

    xs = tuple(jnp.moveaxis(t, 1, 0) for t in (r, lw, k, v, a_vec, b_vec))
    s_fin, ys = lax.scan(step, s0, xs)
    return jnp.moveaxis(ys, 0, 1), s_fin


ROW_TILE = 256


def _bdot(x, w):
    return jnp.dot(x.astype(jnp.bfloat16), w, preferred_element_type=jnp.float32)


def _rms(x):
    return x * lax.rsqrt(jnp.mean(x * x, axis=-1, keepdims=True) + NORM_EPS)


def _rwkv_pre_kernel(x_ref, shift_ref, gain_ref, mu_ref, vec_ref, wrkv_ref, w1_ref, a1_ref, g1_ref,
                     w2_ref, a2_ref, g2_ref,
                     r_ref, lw_ref, k_ref, v_ref, kk_ref, a_ref, g_ref, hlast_ref, prev_ref):
    @pl.when(pl.program_id(1) == 0)
    def _():
        prev_ref[...] = shift_ref[0]

    tm = x_ref.shape[1]
    h = _rms(x_ref[0]) * gain_ref[...]
    first_row = lax.broadcasted_iota(jnp.int32, h.shape, 0) == 0
    h_prev = jnp.where(first_row, prev_ref[...], pltpu.roll(h, 1, 0))
    prev_ref[...] = h[tm - 1:tm, :]
    hlast_ref[0] = h[tm - 1:tm, :]
    dx = h_prev - h
    mix = lambda i: (h + dx * mu_ref[i:i + 1, :]).astype(jnp.bfloat16)
    w0, a0, k_k, k_a = (vec_ref[i:i + 1, :] for i in range(4))
    r_ref[0] = _bdot(mix(0), wrkv_ref[0])
    k = _bdot(mix(2), wrkv_ref[1])
    v_ref[0] = _bdot(mix(3), wrkv_ref[2])
    lw_ref[0] = -DECAY_SCALE * jax.nn.sigmoid(w0 + _bdot(jnp.tanh(_bdot(mix(1), w1_ref[...])), w2_ref[...]))
    a = jax.nn.sigmoid(a0 + _bdot(_bdot(mix(4), a1_ref[...]), a2_ref[...]))
    g_ref[0] = _bdot(jax.nn.sigmoid(_bdot(mix(5), g1_ref[...])), g2_ref[...])
    a_ref[0] = a
    kk_ref[0] = k * k_k
    k_ref[0] = k * (1.0 + (a - 1.0) * k_a)


def _out_proj_kernel(*refs, n_terms, gated, routed):
    terms = refs[:n_terms]
    rest = refs[n_terms:]
    if gated:
        gate_ref, rest = rest[0], rest[1:]
    x_ref, w_ref, gain_ref = rest[:3]
    rest = rest[3:]
    if routed:
        wr_ref, rest = rest[0], rest[1:]
    xo_ref, ho_ref = rest[:2]
    y = terms[0][...]
    for t in terms[1:]:
        y = y + t[...]
    if gated:
        y = y * gate_ref[...]
    xo = x_ref[...] + _bdot(y, w_ref[...])
    xo_ref[...] = xo
    h = _rms(xo) * gain_ref[...]
    ho_ref[...] = h.astype(ho_ref.dtype)
    if routed:
        h_hi, h_lo = _split_bf16(h)
        w_hi, w_lo = _split_bf16(wr_ref[...])
        dot = functools.partial(jnp.dot, preferred_element_type=jnp.float32)
        rest[2][...] = dot(h_hi, w_hi) + (dot(h_hi, w_lo) + dot(h_lo, w_hi))


def out_proj(terms, gate, x, w, gain, w_router=None):
    m, d = x.shape
    kdim = w.shape[0]
    tm = min(ROW_TILE, m)
    rows = lambda width: pl.BlockSpec((tm, width), lambda i: (i, 0))
    full = lambda a: pl.BlockSpec(a.shape, lambda i: (0, 0))
    ins = list(terms) + ([gate] if gate is not None else [])
    consts = [w.astype(jnp.bfloat16), gain.reshape(1, d)]
    out_specs = [rows(d), rows(d)]
    out_shape = [jax.ShapeDtypeStruct((m, d), jnp.float32), jax.ShapeDtypeStruct((m, d), jnp.bfloat16)]
    if w_router is not None:
        consts.append(jnp.pad(w_router, ((0, 0), (0, 128 - w_router.shape[1]))))
        out_specs.append(rows(128))
        out_shape.append(jax.ShapeDtypeStruct((m, 128), jnp.float32))
    return pl.pallas_call(
        functools.partial(_out_proj_kernel, n_terms=len(terms), gated=gate is not None,
                          routed=w_router is not None),
        grid=(m // tm,),
        in_specs=[rows(kdim)] * len(ins) + [rows(d)] + [full(a) for a in consts],
        out_specs=out_specs,
        out_shape=out_shape,
        compiler_params=pltpu.CompilerParams(dimension_semantics=("parallel",),
                                             vmem_limit_bytes=48 * 1024 * 1024),
        name="out_proj",
    )(*ins, x, *consts)


def rwkv7_layer(x, shift0, s0, gain, mu, w_rkv, w0, w1, w2, a0, a1, a2, g1, g2, k_k, k_a, r_k, gn_w, gn_b, w_o,
                gain_next):
    n, T, D = x.shape
    tm = min(ROW_TILE, T)
    bf = lambda t: t.astype(jnp.bfloat16)
    seq = pl.BlockSpec((1, tm, D), lambda b, t: (b, t, 0))
    full = lambda a: pl.BlockSpec(a.shape, lambda b, t: (0,) * a.ndim)
    per_seq = pl.BlockSpec((1, 1, D), lambda b, t: (b, 0, 0))
    vecs = jnp.stack([w0, a0, k_k, k_a]).astype(jnp.float32)
    weights = [bf(w_rkv), bf(w1), bf(a1), bf(g1), bf(w2), bf(a2), bf(g2)]
    small = [gain.reshape(1, D), mu, vecs]
    outs = pl.pallas_call(
        _rwkv_pre_kernel,
        grid=(n, T // tm),
        in_specs=[seq, per_seq] + [full(a) for a in small + weights],
        out_specs=[seq] * 7 + [per_seq],
        out_shape=[jax.ShapeDtypeStruct((n, T, D), jnp.float32)] * 7
                  + [jax.ShapeDtypeStruct((n, 1, D), jnp.float32)],
        scratch_shapes=[pltpu.VMEM((1, D), jnp.float32)],
        compiler_params=pltpu.CompilerParams(dimension_semantics=("parallel", "arbitrary"),
                                             vmem_limit_bytes=56 * 1024 * 1024),
        name="rwkv_pre",
    )(x, shift0.reshape(n, 1, D), *small, *weights)
    r, lw, k, v, kk, a_gate, g, h_last = outs
    y, s_fin = wkv7_chunked(r, lw, k, v, kk, a_gate, r_k.reshape(-1), gn_w, gn_b, s0.astype(jnp.float32))
    flat = lambda t: t.reshape(n * T, D)
    x1, h1 = out_proj([flat(y)], flat(g), flat(x), w_o, gain_next)
    return x1, h1, s_fin, h_last.reshape(n, D)


def rwkv7_time_mix(h, h_prev, s0, mu, w_rkv, w0, w1, w2, a0, a1, a2, g1, g2, k_k, k_a, r_k, gn_w, gn_b, w_o,
                   precise=False):
    n, T, D = h.shape
    f32 = jnp.float32
    mm = functools.partial(dense, precise=precise)
    dx = jnp.concatenate([h_prev[:, None, :].astype(h.dtype), h[:, :-1]], axis=1) - h
    xr, xw, xk, xv, xa, xg = (h + dx * mu[i] for i in range(6))
    r = mm(xr, w_rkv[0])
    k = mm(xk, w_rkv[1])
    v = mm(xv, w_rkv[2])
    log_decay = -DECAY_SCALE * jax.nn.sigmoid((w0 + mm(jnp.tanh(mm(xw, w1)), w2)).astype(f32))
    a = jax.nn.sigmoid((a0 + mm(mm(xa, a1), a2)).astype(f32))
    g = mm(jax.nn.sigmoid(mm(xg, g1)), g2)
    heads = lambda t: t.astype(f32).reshape(n, T, RWKV_HEADS, RWKV_HEAD)
    kk = heads(k * k_k)
    kk = kk / jnp.maximum(jnp.sqrt(jnp.sum(kk * kk, axis=-1, keepdims=True)), 1e-12)
    a_h = heads(a)
    k_h = heads(k.astype(f32) * (1.0 + (a - 1.0) * k_a.astype(f32)))
    r_h, v_h = heads(r), heads(v)
    y, s_fin = wkv7_scan(r_h, heads(log_decay), k_h, v_h, -kk, kk * a_h, s0.astype(f32))
    mean = jnp.mean(y, axis=-1, keepdims=True)
    var = jnp.mean(jnp.square(y - mean), axis=-1, keepdims=True)
    y = ((y - mean) * lax.rsqrt(var + GN_EPS)).reshape(n, T, D) * gn_w.astype(f32) + gn_b.astype(f32)
    bonus = jnp.sum(r_h * k_h * r_k.astype(f32), axis=-1, keepdims=True) * v_h
    y = (y + bonus.reshape(n, T, D)).astype(h.dtype)
    return mm(y * g, w_o), s_fin, h[:, -1]


def shared_kv_rows(x, norm_kv, w_kv, pos, precise=False):
    n, T, _ = x.shape
    kv = dense(rmsnorm(x, norm_kv), w_kv, precise).reshape(n, T, N_BRANCH, 2, NSA_KV_GROUPS, NSA_HEAD_DIM)
    cmp_kv = kv[:, :, 0]
    slc_kv = jnp.stack([rope_partial(kv[:, :, 1, 0], pos), kv[:, :, 1, 1]], axis=2)
    win_kv = jnp.stack([rope_partial(kv[:, :, 2, 0], pos), kv[:, :, 2, 1]], axis=2)
    return cmp_kv, slc_kv, win_kv


CHUNK_LANES = CMP_STRIDE * 2 * NSA_KV_GROUPS * NSA_HEAD_DIM
HIDDEN_LANES = 2 * NSA_KV_GROUPS * 2 * CMP_HIDDEN
PAGES_PER_STEP = 32
CHUNKS_PER_PAGE = PAGE_SIZE // CMP_STRIDE


def _chunk_weights(cmp_w1):
    w = cmp_w1.reshape(2, 2, CMP_STRIDE, NSA_HEAD_DIM, CMP_HIDDEN)
    w = w.transpose(2, 0, 3, 1, 4)
    eye = jnp.eye(2, dtype=w.dtype)
    big = jnp.einsum('ab,cf,jaehk->jacebfhk', eye, jnp.eye(NSA_KV_GROUPS, dtype=w.dtype), w)
    return big.reshape(CHUNK_LANES, HIDDEN_LANES).astype(jnp.bfloat16)


def _token_weights(cmp_w1):
    w = cmp_w1.reshape(2, 2, CMP_STRIDE, NSA_HEAD_DIM, CMP_HIDDEN)
    return w.transpose(2, 0, 3, 1, 4).reshape(CMP_STRIDE, 2, NSA_HEAD_DIM, 2 * CMP_HIDDEN).astype(jnp.bfloat16)


def _paged_chunk_kernel(pt_ref, *refs):
    pages, w_ref, o_ref, xt_ref = (refs[:PAGES_PER_STEP], refs[PAGES_PER_STEP], refs[PAGES_PER_STEP + 1],
                                   refs[PAGES_PER_STEP + 2])
    planes = [(kv, g) for kv in range(2) for g in range(NSA_KV_GROUPS)]
    tok = lax.broadcasted_iota(jnp.int32, (PAGE_SIZE, PAGE_SIZE), 0)
    dst = lax.broadcasted_iota(jnp.int32, (PAGE_SIZE, PAGE_SIZE), 1)
    regroup = (tok == (dst % CHUNKS_PER_PAGE) * CMP_STRIDE + dst // CHUNKS_PER_PAGE).astype(jnp.bfloat16)
    for p in range(PAGES_PER_STEP):
        for i, (kv, g) in enumerate(planes):
            xt_ref[p, i] = _bdot(pages[p][0, kv, g], regroup).T
    width = 2 * CMP_HIDDEN
    for i, (kv, g) in enumerate(planes):
        acc = None
        for j in range(CMP_STRIDE):
            rows = [xt_ref[p, i, j * CHUNKS_PER_PAGE:(j + 1) * CHUNKS_PER_PAGE, :] for p in range(PAGES_PER_STEP)]
            y = _bdot(jnp.concatenate(rows, axis=0), w_ref[j, kv])
            acc = y if acc is None else acc + y
        o_ref[0, :, i * width:(i + 1) * width] = acc


def paged_chunk_hidden(cache, page_table, w_token):
    n, n_pages = page_table.shape
    planes = cache.transpose(0, 2, 3, 4, 1)
    rows = PAGES_PER_STEP * CHUNKS_PER_PAGE

    def page_spec(k):
        return pl.BlockSpec((1,) + planes.shape[1:], lambda b, s, pt: (pt[b, s * PAGES_PER_STEP + k], 0, 0, 0, 0))

    return pl.pallas_call(
        _paged_chunk_kernel,
        grid_spec=pltpu.PrefetchScalarGridSpec(
            num_scalar_prefetch=1,
            grid=(n, n_pages // PAGES_PER_STEP),
            in_specs=[page_spec(k) for k in range(PAGES_PER_STEP)]
                     + [pl.BlockSpec(w_token.shape, lambda b, s, pt: (0, 0, 0, 0))],
            out_specs=pl.BlockSpec((1, rows, HIDDEN_LANES), lambda b, s, pt: (b, s, 0)),
            scratch_shapes=[pltpu.VMEM((PAGES_PER_STEP, 2 * NSA_KV_GROUPS, PAGE_SIZE, NSA_HEAD_DIM), jnp.float32)],
        ),
        out_shape=jax.ShapeDtypeStruct((n, n_pages * CHUNKS_PER_PAGE, HIDDEN_LANES), jnp.float32),
        compiler_params=pltpu.CompilerParams(dimension_semantics=("parallel", "arbitrary"),
                                             vmem_limit_bytes=48 * 1024 * 1024),
        name="paged_chunk_hidden",
    )(page_table.astype(jnp.int32), *([planes] * PAGES_PER_STEP), w_token)


def _matmul_kernel(x_ref, w_ref, o_ref):
    o_ref[...] = jnp.dot(x_ref[...].astype(jnp.bfloat16), w_ref[...],
                         preferred_element_type=jnp.float32).astype(o_ref.dtype)


def matmul(x, w, rows_per_step=512, out_dtype=jnp.float32):
    m, k = x.shape
    nn = w.shape[1]
    tm = min(rows_per_step, m)
    return pl.pallas_call(
        _matmul_kernel,
        grid=(m // tm,),
        in_specs=[pl.BlockSpec((tm, k), lambda i: (i, 0)), pl.BlockSpec((k, nn), lambda i: (0, 0))],
        out_specs=pl.BlockSpec((tm, nn), lambda i: (i, 0)),
        out_shape=jax.ShapeDtypeStruct((m, nn), out_dtype),
        compiler_params=pltpu.CompilerParams(dimension_semantics=("parallel",),
                                             vmem_limit_bytes=48 * 1024 * 1024),
        name="matmul",
    )(x, w.astype(jnp.bfloat16))


def _split_bf16(x):
    hi = x.astype(jnp.bfloat16)
    return hi, (x - hi.astype(jnp.float32)).astype(jnp.bfloat16)


def _matmul3_kernel(x_ref, w_ref, o_ref):
    x_hi, x_lo = _split_bf16(x_ref[...])
    w_hi, w_lo = _split_bf16(w_ref[...])
    dot = functools.partial(jnp.dot, preferred_element_type=jnp.float32)
    o_ref[...] = dot(x_hi, w_hi) + (dot(x_hi, w_lo) + dot(x_lo, w_hi))


def matmul_precise(x, w, cols_per_step=512):
    m, k = x.shape
    nn = w.shape[1]
    pad = (-nn) % 128
    if pad:
        w = jnp.pad(w, ((0, 0), (0, pad)))
    tn = math.gcd(cols_per_step, nn + pad)
    tm = min(m, 512)
    out = pl.pallas_call(
        _matmul3_kernel,
        grid=(m // tm, (nn + pad) // tn),
        in_specs=[pl.BlockSpec((tm, k), lambda i, j: (i, 0)), pl.BlockSpec((k, tn), lambda i, j: (0, j))],
        out_specs=pl.BlockSpec((tm, tn), lambda i, j: (i, j)),
        out_shape=jax.ShapeDtypeStruct((m, nn + pad), jnp.float32),
        compiler_params=pltpu.CompilerParams(dimension_semantics=("parallel", "parallel")),
        name="matmul_precise",
    )(x, w)
    return out[:, :nn] if pad else out


def dense(x, w, precise=False):
    if precise:
        return matmul_precise(x.reshape(-1, x.shape[-1]), w).reshape(x.shape[:-1] + (w.shape[1],))
    return x @ w


def compress_from_hidden(hidden, pos_emb, w1, b1, w2):
    n, C = hidden.shape[:2]
    hid = hidden.reshape(n, C, 2, NSA_KV_GROUPS, 2, CMP_HIDDEN)
    first, second = hid[:, :, :, :, 0], hid[:, :, :, :, 1]
    second = jnp.concatenate([second[:, 1:], jnp.zeros_like(second[:, :1])], axis=1)
    w1r = w1.reshape(2, CMP_BLOCK, NSA_HEAD_DIM, CMP_HIDDEN)
    bias = jnp.einsum('ajd,ajdh->ah', pos_emb, w1r) + b1
    act = jax.nn.gelu(first + second + bias[None, None, :, None, :])
    out = jnp.einsum('ncagh,ahd->ncagd', act, w2)
    return out[:, :, 0], out[:, :, 1]


def rows_chunk_hidden(kv_rows, w_chunk):
    n, T = kv_rows.shape[:2]
    chunks = kv_rows.reshape(n * T // CMP_STRIDE, CHUNK_LANES)
    return matmul(chunks, w_chunk, rows_per_step=256).reshape(n, T // CMP_STRIDE, HIDDEN_LANES)


def nsa_query(h, w_in, precise=False):
    n, T, _ = h.shape
    proj = dense(h, w_in, precise)
    q = proj[..., :NSA_HEADS * NSA_HEAD_DIM].reshape(n, T, NSA_KV_GROUPS, NSA_GROUP_HEADS, NSA_HEAD_DIM)
    gate = proj[..., NSA_HEADS * NSA_HEAD_DIM:].reshape(n, T, NSA_KV_GROUPS, NSA_GROUP_HEADS, N_BRANCH)
    return q, gate


def nsa_branches(q, gate, t_pos, kc, vc, gather_sel, k_w, v_w, s_pos):
    scale = NSA_HEAD_DIM ** -0.5
    q_rot = rope_partial(q, t_pos)
    n_c = kc.shape[1]
    vis = (jnp.arange(n_c) * CMP_STRIDE + CMP_BLOCK - 1)[None, :] <= t_pos[:, None]
    p_cmp = masked_softmax(jnp.einsum('nqghd,ncgd->nghqc', q, kc) * scale, vis)
    o_cmp = jnp.einsum('nghqc,ncgd->nqghd', p_cmp.astype(vc.dtype), vc)
    ratio = SLC_BLOCK // CMP_STRIDE
    lead = CMP_BLOCK // CMP_STRIDE - 1
    n_s = n_c // ratio
    pg = jnp.pad(jnp.sum(p_cmp, axis=2), ((0, 0), (0, 0), (0, 0), (lead, 0)))
    p_slc = pg[..., 0:ratio * n_s:ratio]
    for o in range(1, ratio + lead):
        p_slc = p_slc + pg[..., o:o + ratio * n_s:ratio]
    jb = jnp.arange(n_s)[None, :]
    jt = (t_pos // SLC_BLOCK)[:, None]
    forced = (jb == 0) | (jb == jt) | (jb == jt - 1)
    score = jnp.where(jb > jt, -jnp.inf, jnp.where(forced, FORCED_SCORE, p_slc))
    before = (score[..., :, None] > score[..., None, :]) | (
        (score[..., :, None] == score[..., None, :]) & (jb[0][:, None] < jb[0][None, :]))
    rank = jnp.sum(before, axis=-2)
    slots = jnp.arange(min(SLC_TOPN, n_s))
    idx = jnp.sum(jnp.where(rank[..., None, :] == slots[:, None], jb[0], 0), axis=-1)
    k_sel, v_sel = gather_sel(idx)
    tok = idx[..., None] * SLC_BLOCK + jnp.arange(SLC_BLOCK)
    m_sel = tok <= t_pos[None, None, :, None, None]
    s = jnp.einsum('nqghd,ngqksd->nghqks', q_rot, k_sel) * scale
    shp = s.shape
    p = masked_softmax(s.reshape(shp[:4] + (-1,)), m_sel.reshape(m_sel.shape[:3] + (-1,))[:, :, None])
    o_slc = jnp.einsum('nghqks,ngqksd->nqghd', p.reshape(shp).astype(v_sel.dtype), v_sel)
    m_w = (s_pos[None, :] <= t_pos[:, None]) & (s_pos[None, :] >= t_pos[:, None] - WINDOW) & (s_pos[None, :] >= 0)
    p_w = masked_softmax(jnp.einsum('nqghd,nkgd->nghqk', q_rot, k_w) * scale, m_w)
    o_win = jnp.einsum('nghqk,nkgd->nqghd', p_w.astype(v_w.dtype), v_w)
    g = jax.nn.sigmoid(gate.astype(jnp.float32)).astype(q.dtype)
    return g[..., 0:1] * o_cmp + g[..., 1:2] * o_slc + g[..., 2:3] * o_win


def prompt_kv_context(cmp_kv, slc_kv, win_kv, cmp_pos, cmp_w1, cmp_b1, cmp_w2):
    hidden = rows_chunk_hidden(cmp_kv, _chunk_weights(cmp_w1))
    kc, vc = compress_from_hidden(hidden, cmp_pos, cmp_w1, cmp_b1, cmp_w2)
    return kc, vc, slc_kv, win_kv


NSA_SLABS = NSA_GROUP_HEADS
NSA_TILES = NSA_SLABS * NSA_KV_GROUPS
KEY_BLOCK = 128
MASKED = -1e30
SLC_PER_CMP = SLC_BLOCK // CMP_STRIDE
CMP_LEAD = CMP_BLOCK // CMP_STRIDE - 1
SLAB = NSA_KV_GROUPS * NSA_HEAD_DIM
SEL_LANES = 64


def _group_tiles(q_ref, qs_ref):
    tq = q_ref.shape[1]
    lane = lax.broadcasted_iota(jnp.int32, (tq, SLAB), 1)
    for i in range(NSA_SLABS):
        qs = q_ref[0, :, i * 128:(i + 1) * 128]
        for g in range(NSA_KV_GROUPS):
            in_group = (lane >= g * NSA_HEAD_DIM) & (lane < (g + 1) * NSA_HEAD_DIM)
            qs_ref[2 * i + g] = jnp.where(in_group, qs, jnp.zeros_like(qs)).astype(jnp.bfloat16)


def _merge_groups(o_ref, tiles, gate_ref, branch):
    tq = tiles[0].shape[0]
    lane = lax.broadcasted_iota(jnp.int32, (tq, SLAB), 1)
    gates = gate_ref[0]
    col = lambda g, i: (g * NSA_GROUP_HEADS + i) * N_BRANCH + branch
    for i in range(NSA_SLABS):
        lo = tiles[2 * i] * gates[:, col(0, i):col(0, i) + 1]
        hi = tiles[2 * i + 1] * gates[:, col(1, i):col(1, i) + 1]
        o_ref[0, :, i * 128:(i + 1) * 128] = jnp.where(lane < NSA_HEAD_DIM, lo, hi)


def _nsa_cmp_kernel(q_ref, kc_ref, vc_ref, gate_ref, o_ref, sel_ref, qs_ref):
    qi = pl.program_id(1)
    tq = q_ref.shape[1]
    nc = kc_ref.shape[1]
    ns = nc // SLC_PER_CMP
    t0 = qi * tq
    _group_tiles(q_ref, qs_ref)
    kc = kc_ref[0]
    vc = vc_ref[0]
    t_row = t0 + lax.broadcasted_iota(jnp.int32, (tq, nc), 0)
    c_pos = lax.broadcasted_iota(jnp.int32, (tq, nc), 1) * CMP_STRIDE + (CMP_BLOCK - 1)
    vis = c_pos <= t_row
    tiles = range(NSA_TILES)
    s = [_mm(qs_ref[r], kc, _NT) for r in tiles]
    p = []
    for r in tiles:
        sr = jnp.where(vis, s[r], MASKED)
        m = jnp.max(sr, axis=-1, keepdims=True)
        e = jnp.where(vis, jnp.exp(sr - m), 0.0)
        p.append(e / jnp.maximum(jnp.sum(e, axis=-1, keepdims=True), 1e-30))
    _merge_groups(o_ref, [_mm(p[r], vc, _NN) for r in tiles], gate_ref, 0)

    jrow = lax.broadcasted_iota(jnp.int32, (ns, nc), 0)
    ccol = lax.broadcasted_iota(jnp.int32, (ns, nc), 1)
    pool = ((ccol >= SLC_PER_CMP * jrow - CMP_LEAD) & (ccol < SLC_PER_CMP * (jrow + 1))).astype(jnp.bfloat16)
    jb = lax.broadcasted_iota(jnp.int32, (ns, tq), 0)
    jt = (t0 + lax.broadcasted_iota(jnp.int32, (ns, tq), 1)) // SLC_BLOCK
    forced = (jb == 0) | (jb == jt) | (jb == jt - 1)
    sel_t = []
    for g in range(NSA_KV_GROUPS):
        pg = p[g]
        for i in range(1, NSA_SLABS):
            pg = pg + p[2 * i + g]
        pg_hi = pg.astype(jnp.bfloat16)
        pg_lo = (pg - pg_hi.astype(jnp.float32)).astype(jnp.bfloat16)
        p_slc = (lax.dot_general(pool, pg_hi, _NT, preferred_element_type=jnp.float32)
                 + lax.dot_general(pool, pg_lo, _NT, preferred_element_type=jnp.float32))
        score = jnp.where(jb > jt, -jnp.inf, jnp.where(forced, FORCED_SCORE, p_slc))
        rank = jnp.zeros((ns, tq), jnp.float32)
        for i in range(ns):
            row = score[i:i + 1, :]
            tie = jnp.where(jb > i, 1.0, 0.0)
            rank = rank + jnp.where(row > score, 1.0, 0.0) + jnp.where(row == score, tie, 0.0)
        sel_t.append(jnp.where(rank < min(SLC_TOPN, ns), 1.0, 0.0))
        if ns < SEL_LANES:
            sel_t.append(jnp.zeros((SEL_LANES - ns, tq), jnp.float32))
    sel_ref[0] = jnp.concatenate(sel_t, axis=0).T.astype(sel_ref.dtype)


def _nsa_dense_kernel(q_ref, k_ref, vt_ref, sel_ref, gate_ref, o_ref, qs_ref, m_ref, acc_ref, *, windowed):
    qi = pl.program_id(1)
    tq = q_ref.shape[1]
    t0 = qi * tq
    _group_tiles(q_ref, qs_ref)
    m_ref[...] = jnp.full(m_ref.shape, MASKED, jnp.float32)
    acc_ref[...] = jnp.zeros(acc_ref.shape, jnp.float32)
    k_row = lax.broadcasted_iota(jnp.int32, (KEY_BLOCK, tq), 0)
    t_lane = t0 + lax.broadcasted_iota(jnp.int32, (KEY_BLOCK, tq), 1)
    tiles = range(NSA_TILES)
    first = jnp.maximum(qi - WINDOW // KEY_BLOCK, 0) if windowed else 0

    def key_block(kb, carry):
        start = pl.multiple_of(kb * KEY_BLOCK, KEY_BLOCK)
        kblk = k_ref[0, pl.ds(start, KEY_BLOCK), :]
        vt = vt_ref[0, :, pl.ds(start, KEY_BLOCK)]
        k_pos = start + k_row
        if windowed:
            allowed = [(k_pos <= t_lane) & (k_pos >= t_lane - WINDOW)] * NSA_KV_GROUPS
        else:
            sel = sel_ref[0]
            n_sel = NSA_KV_GROUPS * SEL_LANES
            e_key = lax.broadcasted_iota(jnp.int32, (KEY_BLOCK, n_sel), 0)
            e_lane = lax.broadcasted_iota(jnp.int32, (KEY_BLOCK, n_sel), 1)
            blk = kb * (KEY_BLOCK // SLC_BLOCK) + e_key // SLC_BLOCK
            allowed = []
            for g in range(NSA_KV_GROUPS):
                expand = (e_lane == blk + g * SEL_LANES).astype(jnp.bfloat16)
                picked = lax.dot_general(expand, sel, _NT, preferred_element_type=jnp.float32)
                allowed.append((picked > 0.5) & (k_pos <= t_lane))
        bias = [jnp.where(ok, 0.0, MASKED) for ok in allowed]
        d_row = lax.broadcasted_iota(jnp.int32, vt.shape, 0)
        vt_g = [jnp.where((d_row >= g * NSA_HEAD_DIM) & (d_row < (g + 1) * NSA_HEAD_DIM), vt, jnp.ones_like(vt))
                for g in range(NSA_KV_GROUPS)]
        s = [_mm(kblk, qs_ref[r], _NT) for r in tiles]
        p = []
        alpha = []
        for r in tiles:
            sr = s[r] + bias[r % NSA_KV_GROUPS]
            m_prev = m_ref[r]
            m_new = jnp.maximum(m_prev, jnp.max(sr, axis=0, keepdims=True))
            alpha.append(jnp.exp(m_prev - m_new))
            p.append(jnp.exp(sr - m_new).astype(jnp.bfloat16))
            m_ref[r] = m_new
        pv = [jnp.dot(vt_g[r % NSA_KV_GROUPS], p[r], preferred_element_type=jnp.float32) for r in tiles]
        for r in tiles:
            acc_ref[r] = alpha[r] * acc_ref[r] + pv[r]
        return carry

    lax.fori_loop(first, qi * (tq // KEY_BLOCK) + tq // KEY_BLOCK, key_block, 0)
    out = []
    for r in tiles:
        acc = acc_ref[r]
        sum_row = (1 - r % NSA_KV_GROUPS) * NSA_HEAD_DIM
        out.append((acc / jnp.maximum(acc[sum_row:sum_row + 1, :], 1e-30)).T)
    _merge_groups(o_ref, out, gate_ref, 2 if windowed else 1)


def nsa_prompt_attention(q, q_rot, kc, vc, kv_att, gates):
    n, T, D = q.shape
    tq = Q_BLOCK
    qspec = pl.BlockSpec((1, tq, D), lambda b, i: (b, i, 0))
    whole = lambda a: pl.BlockSpec((1,) + a.shape[1:], lambda b, i: (b, 0, 0))
    lanes = lambda j: pl.BlockSpec((1, T, SLAB), lambda b, i: (b, 0, j))
    n_sel = NSA_KV_GROUPS * SEL_LANES
    sel_spec = pl.BlockSpec((1, tq, n_sel), lambda b, i: (b, i, 0))
    gate_spec = pl.BlockSpec((1, tq, gates.shape[2]), lambda b, i: (b, i, 0))
    params = pltpu.CompilerParams(dimension_semantics=("parallel", "arbitrary"),
                                  vmem_limit_bytes=48 * 1024 * 1024)
    qs_scratch = pltpu.VMEM((NSA_TILES, tq, SLAB), jnp.bfloat16)
    o_cmp, sel = pl.pallas_call(
        _nsa_cmp_kernel,
        grid=(n, T // tq),
        in_specs=[qspec, whole(kc), whole(vc), gate_spec],
        out_specs=[qspec, sel_spec],
        out_shape=[jax.ShapeDtypeStruct((n, T, D), jnp.float32),
                   jax.ShapeDtypeStruct((n, T, n_sel), jnp.bfloat16)],
        scratch_shapes=[qs_scratch],
        compiler_params=params,
        name="nsa_cmp_select",
    )(q, kc, vc, gates)
    stat = pltpu.VMEM((NSA_TILES, 1, tq), jnp.float32)
    acc = pltpu.VMEM((NSA_TILES, SLAB, tq), jnp.float32)
    v_t = jnp.swapaxes(jnp.concatenate([kv_att[:, :, SLAB:2 * SLAB], kv_att[:, :, 3 * SLAB:]], axis=-1), 1, 2)
    rows = lambda j: pl.BlockSpec((1, SLAB, T), lambda b, i: (b, j, 0))

    def dense(windowed, branch, name):
        return pl.pallas_call(
            functools.partial(_nsa_dense_kernel, windowed=windowed),
            grid=(n, T // tq),
            in_specs=[qspec, lanes(2 * branch), rows(branch), sel_spec, gate_spec],
            out_specs=qspec,
            out_shape=jax.ShapeDtypeStruct((n, T, D), jnp.float32),
            scratch_shapes=[qs_scratch, stat, acc],
            compiler_params=params,
            name=name,
        )(q_rot, kv_att, v_t, sel, gates)

    return o_cmp, dense(False, 0, "nsa_selected"), dense(True, 1, "nsa_window")


def _rope_tables(pos):
    half = ROT_DIM // 2
    inv = ROPE_THETA ** (-2.0 * jnp.arange(half, dtype=jnp.float32) / ROT_DIM)
    ang = pos.astype(jnp.float32)[:, None] * inv[None, :]
    rest = NSA_HEAD_DIM - ROT_DIM
    cos = jnp.concatenate([jnp.cos(ang), jnp.cos(ang), jnp.ones((pos.shape[0], rest), jnp.float32)], axis=1)
    sin = jnp.concatenate([-jnp.sin(ang), jnp.sin(ang), jnp.zeros((pos.shape[0], rest), jnp.float32)], axis=1)
    return jnp.tile(cos, (1, NSA_KV_GROUPS)), jnp.tile(sin, (1, NSA_KV_GROUPS))


def _nsa_pre_kernel(x_ref, gkv_ref, gmix_ref, cos_ref, sin_ref, wkv_ref, wq_ref, wg_ref,
                    cmp_ref, slc_ref, win_ref, kvb_ref, q_ref, qr_ref, gate_ref):
    xh = _rms(x_ref[0])
    cos, sin = cos_ref[...], sin_ref[...]
    low = lax.broadcasted_iota(jnp.int32, cos.shape, 1) % NSA_HEAD_DIM < ROT_DIM // 2

    def rope(t):
        swapped = jnp.where(low, pltpu.roll(t, SLAB - ROT_DIM // 2, 1), pltpu.roll(t, ROT_DIM // 2, 1))
        return t * cos + swapped * sin

    kv = _bdot(xh * gkv_ref[...], wkv_ref[...])
    part = lambda j: kv[:, j * SLAB:(j + 1) * SLAB]
    k_slc, k_win = rope(part(2)), rope(part(4))
    cmp_ref[0] = kv[:, :2 * SLAB]
    slc_ref[0] = jnp.concatenate([k_slc, part(3)], axis=-1)
    win_ref[0] = jnp.concatenate([k_win, part(5)], axis=-1)
    kvb_ref[0] = jnp.concatenate([k_slc, part(3), k_win, part(5)], axis=-1).astype(jnp.bfloat16)
    h = xh * gmix_ref[...]
    q = _bdot(h, wq_ref[...])
    q_ref[0] = q.astype(jnp.bfloat16)
    qr_ref[0] = jnp.concatenate([rope(q[:, i * SLAB:(i + 1) * SLAB]) for i in range(NSA_SLABS)],
                                axis=-1).astype(jnp.bfloat16)
    gate_ref[0] = jax.nn.sigmoid(_bdot(h, wg_ref[...]))


def nsa_pre(x, pos, norm_kv, norm_mix, w_kv, w_in):
    n, T, D = x.shape
    tm = min(ROW_TILE, T)
    nq = NSA_HEADS * NSA_HEAD_DIM
    cos, sin = _rope_tables(pos)
    w_q = (_to_slabs(w_in[:, :nq]) * NSA_HEAD_DIM ** -0.5).astype(jnp.bfloat16)
    w_g = jnp.pad(w_in[:, nq:], ((0, 0), (0, SLAB - (w_in.shape[1] - nq)))).astype(jnp.bfloat16)
    seq = lambda width: pl.BlockSpec((1, tm, width), lambda b, t: (b, t, 0))
    full = lambda a: pl.BlockSpec(a.shape, lambda b, t: (0,) * a.ndim)
    table = pl.BlockSpec((tm, SLAB), lambda b, t: (t, 0))
    consts = [norm_kv.reshape(1, D), norm_mix.reshape(1, D)]
    weights = [w_kv.astype(jnp.bfloat16), w_q, w_g]
    widths = [2 * SLAB, 2 * SLAB, 2 * SLAB, 4 * SLAB, nq, nq, SLAB]
    dtypes = [jnp.float32] * 3 + [jnp.bfloat16] * 3 + [jnp.float32]
    return pl.pallas_call(
        _nsa_pre_kernel,
        grid=(n, T // tm),
        in_specs=[seq(D)] + [full(a) for a in consts] + [table, table] + [full(a) for a in weights],
        out_specs=[seq(w) for w in widths],
        out_shape=[jax.ShapeDtypeStruct((n, T, w), dt) for w, dt in zip(widths, dtypes)],
        compiler_params=pltpu.CompilerParams(dimension_semantics=("parallel", "parallel"),
                                             vmem_limit_bytes=48 * 1024 * 1024),
        name="nsa_pre",
    )(x, *consts, cos, sin, *weights)


def _to_slabs(x):
    lead = x.shape[:-1]
    x = x.reshape(lead + (NSA_KV_GROUPS, NSA_GROUP_HEADS, NSA_HEAD_DIM))
    return jnp.swapaxes(x, -3, -2).reshape(lead + (NSA_HEADS * NSA_HEAD_DIM,))


def _from_slabs(x):
    lead = x.shape[:-1]
    x = x.reshape(lead + (NSA_GROUP_HEADS, NSA_KV_GROUPS, NSA_HEAD_DIM))
    return jnp.swapaxes(x, -3, -2).reshape(lead + (NSA_HEADS * NSA_HEAD_DIM,))


def nsa_layer_prompt(x, norm_kv, norm_mix, w_kv, cmp_pos, cmp_w1, cmp_b1, cmp_w2, w_in, w_o, norm_next, w_router):
    n, T, D = x.shape
    cmp_kv, slc_kv, win_kv, kv_att, q, q_rot, gates = nsa_pre(x, jnp.arange(T), norm_kv, norm_mix, w_kv, w_in)
    hidden = rows_chunk_hidden(cmp_kv, _chunk_weights(cmp_w1))
    kc, vc = compress_from_hidden(hidden, cmp_pos, cmp_w1, cmp_b1, cmp_w2)
    lanes = lambda a: a.reshape(n, a.shape[1], SLAB).astype(jnp.bfloat16)
    branches = nsa_prompt_attention(q, q_rot, lanes(kc), lanes(vc), kv_att, gates)
    flat = lambda a: a.reshape(n * T, a.shape[-1])
    x_out, h_out, logits = out_proj([flat(o) for o in branches], None, flat(x), _to_slabs(w_o.T).T, norm_next,
                                    w_router)
    return x_out, h_out, logits, cmp_kv, slc_kv, win_kv


def sample_kv_context(cmp_new, slc_new, win_new, cache_cmp_kv, cache_slc_kv, cache_win_kv, page_table,
                      cmp_pos, cmp_w1, cmp_b1, cmp_w2):
    n, S = cmp_new.shape[:2]
    n_new_blk = -(-S // SLC_BLOCK)
    pad = ((0, 0), (0, n_new_blk * SLC_BLOCK - S), (0, 0), (0, 0), (0, 0))
    w_chunk = _chunk_weights(cmp_w1)
    hidden = jnp.concatenate([paged_chunk_hidden(cache_cmp_kv, page_table, _token_weights(cmp_w1)),
                              rows_chunk_hidden(jnp.pad(cmp_new.astype(cache_cmp_kv.dtype), pad), w_chunk)], axis=1)
    kc, vc = compress_from_hidden(hidden, cmp_pos, cmp_w1, cmp_b1, cmp_w2)
    pool = cache_slc_kv.transpose(0, 2, 3, 4, 1)
    new_blocks = jnp.pad(slc_new, pad).reshape((n, n_new_blk, SLC_BLOCK) + slc_new.shape[2:])
    win_all = jnp.concatenate([cache_win_kv, win_new.astype(cache_win_kv.dtype)], axis=1)
    return kc, vc, pool, new_blocks, win_all


def sample_nsa(h, ctx, page_table, w_in, w_o):
    kc, vc, pool, new_blocks, win_all = ctx
    n, S, _ = h.shape
    q, gate = nsa_query(h, w_in, precise=True)
    t_pos = PAST_LEN + jnp.arange(S)
    sub = PAGE_SIZE // SLC_BLOCK
    n_past_blk = PAST_LEN // SLC_BLOCK
    n_new_blk = new_blocks.shape[1]
    bi = jnp.arange(n)[:, None, None, None]
    gi = jnp.arange(NSA_KV_GROUPS)[None, :, None, None]

    def gather_sel(idx):
        jp = jnp.minimum(idx, n_past_blk - 1)
        phys = page_table[bi, jp // sub]
        planes = pool[phys, :, gi]
        halves = planes.reshape(planes.shape[:-1] + (sub, SLC_BLOCK))
        pick = (jp % sub)[..., None, None, None, None] == jnp.arange(sub)[:, None]
        from_past = jnp.swapaxes(jnp.sum(jnp.where(pick, halves, 0.0), axis=-2), -1, -2)
        from_past = jnp.swapaxes(from_past, -3, -2)
        jn = jnp.clip(idx - n_past_blk, 0, n_new_blk - 1)
        from_new = new_blocks[bi, jn, :, :, gi].astype(from_past.dtype)
        blk = jnp.where((idx >= n_past_blk)[..., None, None, None], from_new, from_past)
        return blk[..., 0, :], blk[..., 1, :]

    wb = win_all.shape[1] - S
    s_pos = PAST_LEN - wb + jnp.arange(win_all.shape[1])
    with jax.default_matmul_precision("highest"):
        o = nsa_branches(q, gate, t_pos, kc, vc, gather_sel, win_all[:, :, 0], win_all[:, :, 1], s_pos)
    return dense(o.reshape(n, S, NSA_HEADS * NSA_HEAD_DIM), w_o, precise=True)


def kernel(x_prompt, x_sample, state_wkv, state_shift, cache_cmp_kv, cache_slc_kv, cache_win_kv, page_table, norm_mix, norm_ffn, norm_kv, norm_final, rw_mu, rw_w_rkv, rw_w0, rw_w1, rw_w2, rw_a0, rw_a1, rw_a2, rw_g1, rw_g2, rw_k_k, rw_k_a, rw_r_k, rw_gn_w, rw_gn_b, rw_w_o, nsa_w_kv, nsa_cmp_pos, nsa_cmp_w1, nsa_cmp_b1, nsa_cmp_w2, nsa_w_in, nsa_w_o, ffn_w_gu, ffn_w_down, moe_router, moe_w_gu, moe_w_down):
    cmp_params = (nsa_cmp_pos, nsa_cmp_w1, nsa_cmp_b1, nsa_cmp_w2)
    assert DEPTH == 2 and N_A_LAYERS == 1
    D = D_MODEL

    n_p, T = x_prompt.shape[:2]
    x1, h1, wkv_fin, h_last = rwkv7_layer(
        x_prompt, jnp.zeros((n_p, D), x_prompt.dtype), jnp.zeros((n_p, RWKV_HEADS, RWKV_HEAD, RWKV_HEAD), jnp.float32),
        norm_mix[0], rw_mu[0], rw_w_rkv[0], rw_w0[0], rw_w1[0], rw_w2[0], rw_a0[0], rw_a1[0], rw_a2[0],
        rw_g1[0], rw_g2[0], rw_k_k[0], rw_k_a[0], rw_r_k[0], rw_gn_w[0], rw_gn_b[0], rw_w_o[0], norm_ffn[0])
    wkv_p, shift_p = wkv_fin[None], h_last[None]
    x2 = swiglu_residual(x1, h1, ffn_w_gu[0], ffn_w_down[0])
    x_p, h_p, logits_p, cmp_rows, slc_rows, win_rows = nsa_layer_prompt(
        x2.reshape(n_p, T, D), norm_kv, norm_mix[1], nsa_w_kv, *cmp_params, nsa_w_in[0], nsa_w_o[0], norm_ffn[1],
        moe_router[0])
    kv_shape = (n_p, T, 2, NSA_KV_GROUPS, NSA_HEAD_DIM)
    cmp_kv_p, slc_kv_p = cmp_rows.reshape(kv_shape), slc_rows.reshape(kv_shape)
    win_kv_p = win_rows.reshape(kv_shape)[:, -min(WINDOW, T):]

    pos_s = PAST_LEN + jnp.arange(x_sample.shape[1], dtype=jnp.int32)
    h = rmsnorm(x_sample, norm_mix[0])
    y, s_fin, h_last_s = rwkv7_time_mix(
        h, state_shift[0], state_wkv[0], rw_mu[0], rw_w_rkv[0], rw_w0[0], rw_w1[0], rw_w2[0],
        rw_a0[0], rw_a1[0], rw_a2[0], rw_g1[0], rw_g2[0], rw_k_k[0], rw_k_a[0],
        rw_r_k[0], rw_gn_w[0], rw_gn_b[0], rw_w_o[0], precise=True)
    wkv_s, shift_s = s_fin[None], h_last_s[None]
    x_s = x_sample + y
    x_s = x_s + swiglu(rmsnorm(x_s, norm_ffn[0]), ffn_w_gu[0], ffn_w_down[0], precise=True)
    cmp_kv_s, slc_kv_s, win_new = shared_kv_rows(x_s, norm_kv, nsa_w_kv, pos_s, precise=True)
    ctx_s = sample_kv_context(cmp_kv_s, slc_kv_s, win_new, cache_cmp_kv, cache_slc_kv, cache_win_kv, page_table,
                              *cmp_params)
    x_s = x_s + sample_nsa(rmsnorm(x_s, norm_mix[1]), ctx_s, page_table, nsa_w_in[0], nsa_w_o[0])
    win_kv_s = ctx_s[4][:, -cache_win_kv.shape[1]:]
    x_s = x_s.reshape(-1, D)
    h_s = rmsnorm(x_s, norm_ffn[1])
    logits_s = matmul_precise(h_s, moe_router[0])

    n_tok_p = n_p * T
    logits = jnp.concatenate([logits_p[:, :N_EXPERTS], logits_s], axis=0)
    y0, y1, gate = moe_swiglu(jnp.concatenate([h_p, h_s.astype(jnp.bfloat16)], axis=0), logits,
                              moe_w_gu[0], moe_w_down[0])
    gate = jnp.pad(gate, ((0, 0), (0, 128 - TOP_K)))
    y_prompt = combine_norm(x_p, y0[:n_tok_p], y1[:n_tok_p], gate[:n_tok_p], norm_final).reshape(x_prompt.shape)
    y_sample = combine_norm(x_s, y0[n_tok_p:], y1[n_tok_p:], gate[n_tok_p:], norm_final).reshape(x_sample.shape)


    return (y_prompt, y_sample, wkv_p, shift_p, cmp_kv_p, slc_kv_p, win_kv_p,
            wkv_s, shift_s, cmp_kv_s, slc_kv_s, win_kv_s)
```

```python
import functools
import math

import jax
import jax.numpy as jnp
from jax import lax
from jax.experimental import pallas as pl
from jax.experimental.pallas import tpu as pltpu

D_MODEL = 1024
DEPTH = 2
PAST_LEN = 16384
PAGE_SIZE = 128
N_A_LAYERS = DEPTH // 2
RWKV_HEAD = 64
RWKV_HEADS = D_MODEL // RWKV_HEAD
DECAY_SCALE = math.exp(-0.5)
GN_EPS = RWKV_HEAD * 1e-5
NSA_HEADS = 16
NSA_HEAD_DIM = 64
NSA_KV_GROUPS = 2
NSA_GROUP_HEADS = NSA_HEADS // NSA_KV_GROUPS
N_BRANCH = 3
CMP_BLOCK = 32
CMP_STRIDE = 16
CMP_HIDDEN = 128
SLC_BLOCK = 64
SLC_TOPN = 16
WINDOW = 512
Q_BLOCK = 128
FORCED_SCORE = 1e4
ROPE_THETA = 500000.0
ROT_DIM = NSA_HEAD_DIM // 4
N_EXPERTS = 8
TOP_K = 2
MOE_BLOCK = 256
NORM_EPS = 1e-6


def _rmsnorm_kernel(x_ref, g_ref, o_ref):
    x = x_ref[...]
    y = x * lax.rsqrt(jnp.mean(x * x, axis=-1, keepdims=True) + NORM_EPS)
    o_ref[...] = y * g_ref[...]


def rmsnorm(x, g):
    shp = x.shape
    x2 = x.reshape(-1, shp[-1])
    rows = x2.shape[0]
    tm = min(rows, 512)
    out = pl.pallas_call(
        _rmsnorm_kernel,
        grid=(rows // tm,),
        in_specs=[pl.BlockSpec((tm, shp[-1]), lambda i: (i, 0)),
                  pl.BlockSpec((1, shp[-1]), lambda i: (0, 0))],
        out_specs=pl.BlockSpec((tm, shp[-1]), lambda i: (i, 0)),
        out_shape=jax.ShapeDtypeStruct(x2.shape, x.dtype),
        name="rmsnorm",
    )(x2, g.reshape(1, -1))
    return out.reshape(shp)


def rope_partial(x, pos):
    half = ROT_DIM // 2
    inv = ROPE_THETA ** (-2.0 * jnp.arange(half, dtype=jnp.float32) / ROT_DIM)
    ang = pos.astype(jnp.float32)[:, None] * inv[None, :]
    shape = (1, pos.shape[0]) + (1,) * (x.ndim - 3) + (half,)
    cos = jnp.cos(ang).reshape(shape)
    sin = jnp.sin(ang).reshape(shape)
    xf = x.astype(jnp.float32)
    x1, x2 = xf[..., :half], xf[..., half:ROT_DIM]
    out = jnp.concatenate([x1 * cos - x2 * sin, x2 * cos + x1 * sin, xf[..., ROT_DIM:]], axis=-1)
    return out.astype(x.dtype)


def masked_softmax(s, mask):
    s = jnp.where(mask, s.astype(jnp.float32), -jnp.inf)
    m = jnp.max(s, axis=-1, keepdims=True)
    m = jnp.where(jnp.isfinite(m), m, 0.0)
    e = jnp.where(mask, jnp.exp(s - m), 0.0)
    return e / jnp.maximum(jnp.sum(e, axis=-1, keepdims=True), 1e-30)


FF_CHUNK = 1408
SWIGLU_ROWS = 512


def _swiglu_kernel(blk_e_ref, n_used_ref, x_ref, wg_ref, wu_ref, wd_ref, *rest):
    o_ref = rest[-1]
    i = pl.program_id(0)
    f = pl.program_id(1)

    @pl.when(i < n_used_ref[0])
    def _():
        x = x_ref[...]
        g = jnp.dot(x, wg_ref[0], preferred_element_type=jnp.float32)
        u = jnp.dot(x, wu_ref[0], preferred_element_type=jnp.float32)
        act = (g * jax.nn.sigmoid(g) * u).astype(jnp.bfloat16)
        y = jnp.dot(act, wd_ref[0], preferred_element_type=jnp.float32)

        @pl.when(f == 0)
        def _():
            o_ref[...] = y + rest[0][...] if len(rest) == 2 else y

        @pl.when(f > 0)
        def _():
            o_ref[...] += y

    @pl.when(i >= n_used_ref[0])
    def _():
        o_ref[...] = jnp.zeros(o_ref.shape, o_ref.dtype)


def grouped_swiglu(xb, blk_e, n_used, w_gu, w_down, res=None):
    rows, d = xb.shape
    extra = [] if res is None else [res]
    b = min(SWIGLU_ROWS, rows)
    n_blk = rows // b
    ff = w_down.shape[1]
    tf = FF_CHUNK
    n_f = ff // tf
    chunk = lambda i, f, be, nu: jnp.where(i < nu[0], f, n_f - 1)
    return pl.pallas_call(
        _swiglu_kernel,
        grid_spec=pltpu.PrefetchScalarGridSpec(
            num_scalar_prefetch=2,
            grid=(n_blk, n_f),
            in_specs=[
                pl.BlockSpec((b, d), lambda i, f, be, nu: (i, 0)),
                pl.BlockSpec((1, d, tf), lambda i, f, be, nu: (be[i], 0, chunk(i, f, be, nu))),
                pl.BlockSpec((1, d, tf), lambda i, f, be, nu: (be[i], 0, n_f + chunk(i, f, be, nu))),
                pl.BlockSpec((1, tf, d), lambda i, f, be, nu: (be[i], chunk(i, f, be, nu), 0)),
            ] + [pl.BlockSpec((b, d), lambda i, f, be, nu: (i, 0))] * len(extra),
            out_specs=pl.BlockSpec((b, d), lambda i, f, be, nu: (i, 0)),
        ),
        out_shape=jax.ShapeDtypeStruct((rows, d), jnp.float32),
        compiler_params=pltpu.CompilerParams(dimension_semantics=("arbitrary", "arbitrary"),
                                             vmem_limit_bytes=56 * 1024 * 1024),
        name="grouped_swiglu",
    )(blk_e.astype(jnp.int32), jnp.reshape(n_used, (1,)).astype(jnp.int32), xb, w_gu, w_gu, w_down, *extra)


def swiglu(h, w_gu, w_down, precise=False):
    g, u = jnp.split(dense(h, w_gu, precise), 2, axis=-1)
    return dense(jax.nn.silu(g) * u, w_down, precise)


def swiglu_residual(x, h, w_gu, w_down):
    n_blk = h.shape[0] // min(SWIGLU_ROWS, h.shape[0])
    return grouped_swiglu(h, jnp.zeros((n_blk,), jnp.int32), jnp.int32(n_blk),
                          w_gu.astype(jnp.bfloat16)[None], w_down.astype(jnp.bfloat16)[None], res=x)


def _combine_norm_kernel(x_ref, y0_ref, y1_ref, g_ref, gain_ref, o_ref):
    g = g_ref[...]
    x = x_ref[...] + (y0_ref[...] * g[:, 0:1] + y1_ref[...] * g[:, 1:2])
    o_ref[...] = _rms(x) * gain_ref[...]


def combine_norm(x, y0, y1, gate, gain):
    m, d = x.shape
    tm = min(2 * ROW_TILE, m)
    rows = lambda width: pl.BlockSpec((tm, width), lambda i: (i, 0))
    return pl.pallas_call(
        _combine_norm_kernel,
        grid=(m // tm,),
        in_specs=[rows(d), rows(d), rows(d), rows(gate.shape[1]), pl.BlockSpec((1, d), lambda i: (0, 0))],
        out_specs=rows(d),
        out_shape=jax.ShapeDtypeStruct((m, d), jnp.float32),
        compiler_params=pltpu.CompilerParams(dimension_semantics=("parallel",)),
        name="combine_norm",
    )(x, y0, y1, gate, gain.reshape(1, d))


def moe_swiglu(xt, logits, w_gu_e, w_down_e):
    n_tok, d = xt.shape
    b = SWIGLU_ROWS
    experts = jnp.arange(N_EXPERTS)[None, :]
    e0 = jnp.argmax(logits, axis=-1)
    v0 = jnp.max(logits, axis=-1)
    rest = jnp.where(experts == e0[:, None], -jnp.inf, logits)
    e1 = jnp.argmax(rest, axis=-1)
    v1 = jnp.max(rest, axis=-1)
    top_idx = jnp.stack([e0, e1], axis=-1).astype(jnp.int32)
    gate = jax.nn.softmax(jnp.stack([v0, v1], axis=-1), axis=-1)
    nk = n_tok * TOP_K
    flat_e = top_idx.reshape(nk)
    onehot = (flat_e[:, None] == jnp.arange(N_EXPERTS)[None, :]).astype(jnp.int32)
    before = jnp.cumsum(onehot, axis=0) - onehot
    counts = jnp.sum(onehot, axis=0)
    padded = (counts + b - 1) // b * b
    ends_pad = jnp.cumsum(padded)
    starts_pad = ends_pad - padded
    dest = jnp.sum(onehot * (starts_pad[None, :] + before), axis=1)
    n_rows = (nk + b - 1) // b * b + N_EXPERTS * b
    n_blk = n_rows // b
    flat_tok = jnp.repeat(jnp.arange(n_tok, dtype=jnp.int32), TOP_K)
    row_tok = jnp.full((n_rows,), n_tok, jnp.int32).at[dest].set(flat_tok)
    blk_e = jnp.minimum(jnp.searchsorted(ends_pad, jnp.arange(n_blk) * b, side='right'), N_EXPERTS - 1)
    x_pad = jnp.concatenate([xt.astype(jnp.bfloat16), jnp.zeros((1, d), jnp.bfloat16)], axis=0)
    yb = grouped_swiglu(x_pad[row_tok], blk_e, ends_pad[-1] // b,
                        w_gu_e.astype(jnp.bfloat16), w_down_e.astype(jnp.bfloat16))
    dest = dest.reshape(n_tok, TOP_K)
    return yb[dest[:, 0]], yb[dest[:, 1]], gate


WKV_CHUNK = 64
WKV_HEADS_PER_STEP = 16

_NN = (((1,), (0,)), ((), ()))
_NT = (((1,), (1,)), ((), ()))
_TN = (((0,), (0,)), ((), ()))


def _mm(x, y, dims):
    return lax.dot_general(x.astype(jnp.bfloat16), y.astype(jnp.bfloat16), dims,
                           preferred_element_type=jnp.float32)


def _wkv7_chunk_kernel(r_ref, lw_ref, k_ref, v_ref, kk_ref, a_ref, rk_ref, gnw_ref, gnb_ref, s0_ref,
                       y_ref, sout_ref, state_ref):
    c = pl.program_id(2)
    L = r_ref.shape[1]
    hb = state_ref.shape[0]
    N = RWKV_HEAD

    @pl.when(c == 0)
    def _():
        state_ref[...] = s0_ref[0]

    row = lax.broadcasted_iota(jnp.int32, (L, L), 0)
    col = lax.broadcasted_iota(jnp.int32, (L, L), 1)
    strict = row > col
    incl = row >= col
    tri = incl.astype(jnp.bfloat16)
    eye_n = lax.broadcasted_iota(jnp.int32, (N, N), 0) == lax.broadcasted_iota(jnp.int32, (N, N), 1)

    lw = lw_ref[0]
    lw_hi, lw_lo = _split_bf16(lw)
    cum = (lax.dot_general(tri, lw_hi, _NN, preferred_element_type=jnp.float32)
           + lax.dot_general(tri, lw_lo, _NN, preferred_element_type=jnp.float32))
    cum_last = cum[L - 1:L, :]
    e_neg = jnp.exp(-cum)
    e_tail = jnp.exp(cum_last - cum)
    e_prev = jnp.exp(cum - lw)
    r_in = r_ref[0]
    k_in = k_ref[0]
    r_all = r_in * jnp.exp(cum)
    kt_all = k_in * e_neg
    kh_all = k_in * e_tail
    wl_all = jnp.exp(cum_last)

    heads = range(hb)
    per_head = lambda t: [t[:, j * N:(j + 1) * N] for j in heads]
    rt, kt, kh, v, wl = (per_head(t) for t in (r_all, kt_all, kh_all, v_ref[0], wl_all))
    kk_raw, a_gate, e_prev, e_neg, e_tail = (per_head(t) for t in (kk_ref[0], a_ref[0], e_prev, e_neg, e_tail))
    at, bt, bh = [], [], []
    for j in heads:
        norm = jnp.sqrt(jnp.sum(kk_raw[j] * kk_raw[j], axis=-1, keepdims=True))
        kk = kk_raw[j] / jnp.maximum(norm, 1e-12)
        b = kk * a_gate[j]
        at.append(-kk * e_prev[j])
        bt.append(b * e_neg[j])
        bh.append(b * e_tail[j])
    a_ab = [jnp.where(strict, _mm(at[j], bt[j], _NT), 0.0) for j in heads]
    a_ak = [jnp.where(strict, _mm(at[j], kt[j], _NT), 0.0) for j in heads]
    r_b = [jnp.where(incl, _mm(rt[j], bt[j], _NT), 0.0) for j in heads]
    r_k = [jnp.where(incl, _mm(rt[j], kt[j], _NT), 0.0) for j in heads]
    av = [_mm(a_ak[j], v[j], _NN) for j in heads]
    ht = [_mm(v[j], kh[j], _TN) for j in heads]
    yp = [_mm(r_k[j], v[j], _NN) for j in heads]
    pw = a_ab
    inv_a = a_ab
    n = 1
    while 2 * n < L:
        pw = [_mm(pw[j], pw[j], _NN) for j in heads]
        inv_a = [inv_a[j] + pw[j] + _mm(inv_a[j], pw[j], _NN) for j in heads]
        n *= 2
    ap = [at[j] + _mm(inv_a[j], at[j], _NN) for j in heads]
    vp = [av[j] + _mm(inv_a[j], av[j], _NN) for j in heads]
    g = [jnp.where(eye_n, wl[j], 0.0) + _mm(bh[j], ap[j], _TN) for j in heads]
    ht = [ht[j] + _mm(vp[j], bh[j], _TN) for j in heads]
    rp = [rt[j] + _mm(r_b[j], ap[j], _NN) for j in heads]
    yp = [yp[j] + _mm(r_b[j], vp[j], _NN) for j in heads]
    s_prev = [state_ref[j] for j in heads]
    y = [_mm(rp[j], s_prev[j], _NT) + yp[j] for j in heads]
    for j in heads:
        state_ref[j] = _mm(s_prev[j], g[j], _NT) + ht[j]
    r_raw, k_raw = per_head(r_in), per_head(k_in)
    rk, gnw, gnb = per_head(rk_ref[...]), per_head(gnw_ref[...]), per_head(gnb_ref[...])
    out = []
    for j in heads:
        mean = jnp.mean(y[j], axis=-1, keepdims=True)
        cen = y[j] - mean
        var = jnp.mean(cen * cen, axis=-1, keepdims=True)
        bonus = jnp.sum(r_raw[j] * k_raw[j] * rk[j], axis=-1, keepdims=True) * v[j]
        out.append(cen * lax.rsqrt(var + GN_EPS) * gnw[j] + gnb[j] + bonus)
    y_ref[0] = jnp.concatenate(out, axis=-1)

    @pl.when(c == pl.num_programs(2) - 1)
    def _():
        sout_ref[0] = state_ref[...]


def wkv7_chunked(r, lw, k, v, kk, a_gate, r_k, gn_w, gn_b, s0):
    n, T, D = r.shape
    L = WKV_CHUNK
    hb = WKV_HEADS_PER_STEP
    w = hb * RWKV_HEAD
    seq = pl.BlockSpec((1, L, w), lambda b, h, c: (b, c, h))
    vec = pl.BlockSpec((1, w), lambda b, h, c: (0, h))
    st = pl.BlockSpec((1, hb, RWKV_HEAD, RWKV_HEAD), lambda b, h, c: (b, h, 0, 0))
    row = lambda t: t.reshape(1, D).astype(jnp.float32)
    return pl.pallas_call(
        _wkv7_chunk_kernel,
        grid=(n, D // w, T // L),
        in_specs=[seq] * 6 + [vec] * 3 + [st],
        out_specs=[seq, st],
        out_shape=[jax.ShapeDtypeStruct((n, T, D), jnp.float32),
                   jax.ShapeDtypeStruct(s0.shape, jnp.float32)],
        scratch_shapes=[pltpu.VMEM((hb, RWKV_HEAD, RWKV_HEAD), jnp.float32)],
        compiler_params=pltpu.CompilerParams(dimension_semantics=("parallel", "parallel", "arbitrary")),
        name="wkv7_chunked",
    )(r, lw, k, v, kk, a_gate, row(r_k), row(gn_w), row(gn_b), s0)


def wkv7_scan(r, lw, k, v, a_vec, b_vec, s0):
    def step(S, inp):
        r_t, lw_t, k_t, v_t, a_t, b_t = inp
        sa = jnp.sum(S * a_t[:, :, None, :], axis=-1)
        S = S * jnp.exp(lw_t)[:, :, None, :] + sa[..., None] * b_t[:, :, None, :] + v_t[..., None] * k_t[:, :, None, :]
        return S, jnp.sum(S * r_t[:, :, None, :], axis=-1)

    xs = tuple(jnp.moveaxis(t, 1, 0) for t in (r, lw, k, v, a_vec, b_vec))
    s_fin, ys = lax.scan(step, s0, xs)
    return jnp.moveaxis(ys, 0, 1), s_fin


ROW_TILE = 256


def _bdot(x, w):
    return jnp.dot(x.astype(jnp.bfloat16), w, preferred_element_type=jnp.float32)


def _rms(x):
    return x * lax.rsqrt(jnp.mean(x * x, axis=-1, keepdims=True) + NORM_EPS)


def _rwkv_pre_kernel(x_ref, shift_ref, gain_ref, mu_ref, vec_ref, wrkv_ref, w1_ref, a1_ref, g1_ref,
                     w2_ref, a2_ref, g2_ref,
                     r_ref, lw_ref, k_ref, v_ref, kk_ref, a_ref, g_ref, hlast_ref, prev_ref):
    @pl.when(pl.program_id(1) == 0)
    def _():
        prev_ref[...] = shift_ref[0]

    tm = x_ref.shape[1]
    h = _rms(x_ref[0]) * gain_ref[...]
    first_row = lax.broadcasted_iota(jnp.int32, h.shape, 0) == 0
    h_prev = jnp.where(first_row, prev_ref[...], pltpu.roll(h, 1, 0))
    prev_ref[...] = h[tm - 1:tm, :]
    hlast_ref[0] = h[tm - 1:tm, :]
    dx = h_prev - h
    mix = lambda i: (h + dx * mu_ref[i:i + 1, :]).astype(jnp.bfloat16)
    w0, a0, k_k, k_a = (vec_ref[i:i + 1, :] for i in range(4))
    r_ref[0] = _bdot(mix(0), wrkv_ref[0])
    k = _bdot(mix(2), wrkv_ref[1])
    v_ref[0] = _bdot(mix(3), wrkv_ref[2])
    lw_ref[0] = -DECAY_SCALE * jax.nn.sigmoid(w0 + _bdot(jnp.tanh(_bdot(mix(1), w1_ref[...])), w2_ref[...]))
    a = jax.nn.sigmoid(a0 + _bdot(_bdot(mix(4), a1_ref[...]), a2_ref[...]))
    g_ref[0] = _bdot(jax.nn.sigmoid(_bdot(mix(5), g1_ref[...])), g2_ref[...])
    a_ref[0] = a
    kk_ref[0] = k * k_k
    k_ref[0] = k * (1.0 + (a - 1.0) * k_a)


def _out_proj_kernel(*refs, n_terms, gated, routed):
    terms = refs[:n_terms]
    rest = refs[n_terms:]
    if gated:
        gate_ref, rest = rest[0], rest[1:]
    x_ref, w_ref, gain_ref = rest[:3]
    rest = rest[3:]
    if routed:
        wr_ref, rest = rest[0], rest[1:]
    xo_ref, ho_ref = rest[:2]
    y = terms[0][...]
    for t in terms[1:]:
        y = y + t[...]
    if gated:
        y = y * gate_ref[...]
    xo = x_ref[...] + _bdot(y, w_ref[...])
    xo_ref[...] = xo
    h = _rms(xo) * gain_ref[...]
    ho_ref[...] = h.astype(ho_ref.dtype)
    if routed:
        h_hi, h_lo = _split_bf16(h)
        w_hi, w_lo = _split_bf16(wr_ref[...])
        dot = functools.partial(jnp.dot, preferred_element_type=jnp.float32)
        rest[2][...] = dot(h_hi, w_hi) + (dot(h_hi, w_lo) + dot(h_lo, w_hi))


def out_proj(terms, gate, x, w, gain, w_router=None):
    m, d = x.shape
    kdim = w.shape[0]
    tm = min(ROW_TILE, m)
    rows = lambda width: pl.BlockSpec((tm, width), lambda i: (i, 0))
    full = lambda a: pl.BlockSpec(a.shape, lambda i: (0, 0))
    ins = list(terms) + ([gate] if gate is not None else [])
    consts = [w.astype(jnp.bfloat16), gain.reshape(1, d)]
    out_specs = [rows(d), rows(d)]
    out_shape = [jax.ShapeDtypeStruct((m, d), jnp.float32), jax.ShapeDtypeStruct((m, d), jnp.bfloat16)]
    if w_router is not None:
        consts.append(jnp.pad(w_router, ((0, 0), (0, 128 - w_router.shape[1]))))
        out_specs.append(rows(128))
        out_shape.append(jax.ShapeDtypeStruct((m, 128), jnp.float32))
    return pl.pallas_call(
        functools.partial(_out_proj_kernel, n_terms=len(terms), gated=gate is not None,
                          routed=w_router is not None),
        grid=(m // tm,),
        in_specs=[rows(kdim)] * len(ins) + [rows(d)] + [full(a) for a in consts],
        out_specs=out_specs,
        out_shape=out_shape,
        compiler_params=pltpu.CompilerParams(dimension_semantics=("parallel",),
                                             vmem_limit_bytes=48 * 1024 * 1024),
        name="out_proj",
    )(*ins, x, *consts)


def rwkv7_layer(x, shift0, s0, gain, mu, w_rkv, w0, w1, w2, a0, a1, a2, g1, g2, k_k, k_a, r_k, gn_w, gn_b, w_o,
                gain_next):
    n, T, D = x.shape
    tm = min(ROW_TILE, T)
    bf = lambda t: t.astype(jnp.bfloat16)
    seq = pl.BlockSpec((1, tm, D), lambda b, t: (b, t, 0))
    full = lambda a: pl.BlockSpec(a.shape, lambda b, t: (0,) * a.ndim)
    per_seq = pl.BlockSpec((1, 1, D), lambda b, t: (b, 0, 0))
    vecs = jnp.stack([w0, a0, k_k, k_a]).astype(jnp.float32)
    weights = [bf(w_rkv), bf(w1), bf(a1), bf(g1), bf(w2), bf(a2), bf(g2)]
    small = [gain.reshape(1, D), mu, vecs]
    outs = pl.pallas_call(
        _rwkv_pre_kernel,
        grid=(n, T // tm),
        in_specs=[seq, per_seq] + [full(a) for a in small + weights],
        out_specs=[seq] * 7 + [per_seq],
        out_shape=[jax.ShapeDtypeStruct((n, T, D), jnp.float32)] * 7
                  + [jax.ShapeDtypeStruct((n, 1, D), jnp.float32)],
        scratch_shapes=[pltpu.VMEM((1, D), jnp.float32)],
        compiler_params=pltpu.CompilerParams(dimension_semantics=("parallel", "arbitrary"),
                                             vmem_limit_bytes=56 * 1024 * 1024),
        name="rwkv_pre",
    )(x, shift0.reshape(n, 1, D), *small, *weights)
    r, lw, k, v, kk, a_gate, g, h_last = outs
    y, s_fin = wkv7_chunked(r, lw, k, v, kk, a_gate, r_k.reshape(-1), gn_w, gn_b, s0.astype(jnp.float32))
    flat = lambda t: t.reshape(n * T, D)
    x1, h1 = out_proj([flat(y)], flat(g), flat(x), w_o, gain_next)
    return x1, h1, s_fin, h_last.reshape(n, D)


def rwkv7_time_mix(h, h_prev, s0, mu, w_rkv, w0, w1, w2, a0, a1, a2, g1, g2, k_k, k_a, r_k, gn_w, gn_b, w_o,
                   precise=False):
    n, T, D = h.shape
    f32 = jnp.float32
    mm = functools.partial(dense, precise=precise)
    dx = jnp.concatenate([h_prev[:, None, :].astype(h.dtype), h[:, :-1]], axis=1) - h
    xr, xw, xk, xv, xa, xg = (h + dx * mu[i] for i in range(6))
    r = mm(xr, w_rkv[0])
    k = mm(xk, w_rkv[1])
    v = mm(xv, w_rkv[2])
    log_decay = -DECAY_SCALE * jax.nn.sigmoid((w0 + mm(jnp.tanh(mm(xw, w1)), w2)).astype(f32))
    a = jax.nn.sigmoid((a0 + mm(mm(xa, a1), a2)).astype(f32))
    g = mm(jax.nn.sigmoid(mm(xg, g1)), g2)
    heads = lambda t: t.astype(f32).reshape(n, T, RWKV_HEADS, RWKV_HEAD)
    kk = heads(k * k_k)
    kk = kk / jnp.maximum(jnp.sqrt(jnp.sum(kk * kk, axis=-1, keepdims=True)), 1e-12)
    a_h = heads(a)
    k_h = heads(k.astype(f32) * (1.0 + (a - 1.0) * k_a.astype(f32)))
    r_h, v_h = heads(r), heads(v)
    y, s_fin = wkv7_scan(r_h, heads(log_decay), k_h, v_h, -kk, kk * a_h, s0.astype(f32))
    mean = jnp.mean(y, axis=-1, keepdims=True)
    var = jnp.mean(jnp.square(y - mean), axis=-1, keepdims=True)
    y = ((y - mean) * lax.rsqrt(var + GN_EPS)).reshape(n, T, D) * gn_w.astype(f32) + gn_b.astype(f32)
    bonus = jnp.sum(r_h * k_h * r_k.astype(f32), axis=-1, keepdims=True) * v_h
    y = (y + bonus.reshape(n, T, D)).astype(h.dtype)
    return mm(y * g, w_o), s_fin, h[:, -1]


def shared_kv_rows(x, norm_kv, w_kv, pos, precise=False):
    n, T, _ = x.shape
    kv = dense(rmsnorm(x, norm_kv), w_kv, precise).reshape(n, T, N_BRANCH, 2, NSA_KV_GROUPS, NSA_HEAD_DIM)
    cmp_kv = kv[:, :, 0]
    slc_kv = jnp.stack([rope_partial(kv[:, :, 1, 0], pos), kv[:, :, 1, 1]], axis=2)
    win_kv = jnp.stack([rope_partial(kv[:, :, 2, 0], pos), kv[:, :, 2, 1]], axis=2)
    return cmp_kv, slc_kv, win_kv


CHUNK_LANES = CMP_STRIDE * 2 * NSA_KV_GROUPS * NSA_HEAD_DIM
HIDDEN_LANES = 2 * NSA_KV_GROUPS * 2 * CMP_HIDDEN
PAGES_PER_STEP = 32
CHUNKS_PER_PAGE = PAGE_SIZE // CMP_STRIDE


def _chunk_weights(cmp_w1):
    w = cmp_w1.reshape(2, 2, CMP_STRIDE, NSA_HEAD_DIM, CMP_HIDDEN)
    w = w.transpose(2, 0, 3, 1, 4)
    eye = jnp.eye(2, dtype=w.dtype)
    big = jnp.einsum('ab,cf,jaehk->jacebfhk', eye, jnp.eye(NSA_KV_GROUPS, dtype=w.dtype), w)
    return big.reshape(CHUNK_LANES, HIDDEN_LANES).astype(jnp.bfloat16)


def _token_weights(cmp_w1):
    w = cmp_w1.reshape(2, 2, CMP_STRIDE, NSA_HEAD_DIM, CMP_HIDDEN)
    return w.transpose(2, 0, 3, 1, 4).reshape(CMP_STRIDE, 2, NSA_HEAD_DIM, 2 * CMP_HIDDEN).astype(jnp.bfloat16)


def _paged_chunk_kernel(pt_ref, *refs):
    pages, w_ref, o_ref, xt_ref = (refs[:PAGES_PER_STEP], refs[PAGES_PER_STEP], refs[PAGES_PER_STEP + 1],
                                   refs[PAGES_PER_STEP + 2])
    planes = [(kv, g) for kv in range(2) for g in range(NSA_KV_GROUPS)]
    tok = lax.broadcasted_iota(jnp.int32, (PAGE_SIZE, PAGE_SIZE), 0)
    dst = lax.broadcasted_iota(jnp.int32, (PAGE_SIZE, PAGE_SIZE), 1)
    regroup = (tok == (dst % CHUNKS_PER_PAGE) * CMP_STRIDE + dst // CHUNKS_PER_PAGE).astype(jnp.bfloat16)
    for p in range(PAGES_PER_STEP):
        for i, (kv, g) in enumerate(planes):
            xt_ref[p, i] = _bdot(pages[p][0, kv, g], regroup).T
    width = 2 * CMP_HIDDEN
    for i, (kv, g) in enumerate(planes):
        acc = None
        for j in range(CMP_STRIDE):
            rows = [xt_ref[p, i, j * CHUNKS_PER_PAGE:(j + 1) * CHUNKS_PER_PAGE, :] for p in range(PAGES_PER_STEP)]
            y = _bdot(jnp.concatenate(rows, axis=0), w_ref[j, kv])
            acc = y if acc is None else acc + y
        o_ref[0, :, i * width:(i + 1) * width] = acc


def paged_chunk_hidden(cache, page_table, w_token):
    n, n_pages = page_table.shape
    planes = cache.transpose(0, 2, 3, 4, 1)
    rows = PAGES_PER_STEP * CHUNKS_PER_PAGE

    def page_spec(k):
        return pl.BlockSpec((1,) + planes.shape[1:], lambda b, s, pt: (pt[b, s * PAGES_PER_STEP + k], 0, 0, 0, 0))

    return pl.pallas_call(
        _paged_chunk_kernel,
        grid_spec=pltpu.PrefetchScalarGridSpec(
            num_scalar_prefetch=1,
            grid=(n, n_pages // PAGES_PER_STEP),
            in_specs=[page_spec(k) for k in range(PAGES_PER_STEP)]
                     + [pl.BlockSpec(w_token.shape, lambda b, s, pt: (0, 0, 0, 0))],
            out_specs=pl.BlockSpec((1, rows, HIDDEN_LANES), lambda b, s, pt: (b, s, 0)),
            scratch_shapes=[pltpu.VMEM((PAGES_PER_STEP, 2 * NSA_KV_GROUPS, PAGE_SIZE, NSA_HEAD_DIM), jnp.float32)],
        ),
        out_shape=jax.ShapeDtypeStruct((n, n_pages * CHUNKS_PER_PAGE, HIDDEN_LANES), jnp.float32),
        compiler_params=pltpu.CompilerParams(dimension_semantics=("parallel", "arbitrary"),
                                             vmem_limit_bytes=48 * 1024 * 1024),
        name="paged_chunk_hidden",
    )(page_table.astype(jnp.int32), *([planes] * PAGES_PER_STEP), w_token)


def _matmul_kernel(x_ref, w_ref, o_ref):
    o_ref[...] = jnp.dot(x_ref[...].astype(jnp.bfloat16), w_ref[...],
                         preferred_element_type=jnp.float32).astype(o_ref.dtype)


def matmul(x, w, rows_per_step=512, out_dtype=jnp.float32):
    m, k = x.shape
    nn = w.shape[1]
    tm = min(rows_per_step, m)
    return pl.pallas_call(
        _matmul_kernel,
        grid=(m // tm,),
        in_specs=[pl.BlockSpec((tm, k), lambda i: (i, 0)), pl.BlockSpec((k, nn), lambda i: (0, 0))],
        out_specs=pl.BlockSpec((tm, nn), lambda i: (i, 0)),
        out_shape=jax.ShapeDtypeStruct((m, nn), out_dtype),
        compiler_params=pltpu.CompilerParams(dimension_semantics=("parallel",),
                                             vmem_limit_bytes=48 * 1024 * 1024),
        name="matmul",
    )(x, w.astype(jnp.bfloat16))


def _split_bf16(x):
    hi = x.astype(jnp.bfloat16)
    return hi, (x - hi.astype(jnp.float32)).astype(jnp.bfloat16)


def _matmul3_kernel(x_ref, w_ref, o_ref):
    x_hi, x_lo = _split_bf16(x_ref[...])
    w_hi, w_lo = _split_bf16(w_ref[...])
    dot = functools.partial(jnp.dot, preferred_element_type=jnp.float32)
    o_ref[...] = dot(x_hi, w_hi) + (dot(x_hi, w_lo) + dot(x_lo, w_hi))


def matmul_precise(x, w, cols_per_step=512):
    m, k = x.shape
    nn = w.shape[1]
    pad = (-nn) % 128
    if pad:
        w = jnp.pad(w, ((0, 0), (0, pad)))
    tn = math.gcd(cols_per_step, nn + pad)
    tm = min(m, 512)
    out = pl.pallas_call(
        _matmul3_kernel,
        grid=(m // tm, (nn + pad) // tn),
        in_specs=[pl.BlockSpec((tm, k), lambda i, j: (i, 0)), pl.BlockSpec((k, tn), lambda i, j: (0, j))],
        out_specs=pl.BlockSpec((tm, tn), lambda i, j: (i, j)),
        out_shape=jax.ShapeDtypeStruct((m, nn + pad), jnp.float32),
        compiler_params=pltpu.CompilerParams(dimension_semantics=("parallel", "parallel")),
        name="matmul_precise",
    )(x, w)
    return out[:, :nn] if pad else out


def dense(x, w, precise=False):
    if precise:
        return matmul_precise(x.reshape(-1, x.shape[-1]), w).reshape(x.shape[:-1] + (w.shape[1],))
    return x @ w


def compress_from_hidden(hidden, pos_emb, w1, b1, w2):
    n, C = hidden.shape[:2]
    rows = -(-C // 8) * 8
    if rows != C:
        hidden = jnp.pad(hidden, ((0, 0), (0, rows - C), (0, 0)))
    w1r = w1.reshape(2, CMP_BLOCK, NSA_HEAD_DIM, CMP_HIDDEN)
    bias = jnp.einsum('ajd,ajdh->ah', pos_emb, w1r) + b1
    out = pl.pallas_call(
        functools.partial(_compress_finish_kernel, n_blocks=C),
        grid=(n,),
        in_specs=[pl.BlockSpec((1, rows, HIDDEN_LANES), lambda b: (b, 0, 0)),
                  pl.BlockSpec(bias.shape, lambda b: (0, 0)), pl.BlockSpec(w2.shape, lambda b: (0, 0, 0))],
        out_specs=pl.BlockSpec((1, rows, 2 * SLAB), lambda b: (b, 0, 0)),
        out_shape=jax.ShapeDtypeStruct((n, rows, 2 * SLAB), jnp.float32),
        compiler_params=pltpu.CompilerParams(dimension_semantics=("parallel",)),
        name="compress_finish",
    )(hidden, bias, w2.astype(jnp.bfloat16))[:, :C]
    shape = (n, C, NSA_KV_GROUPS, NSA_HEAD_DIM)
    return out[:, :, :SLAB].reshape(shape), out[:, :, SLAB:].reshape(shape)


def _compress_finish_kernel(h_ref, bias_ref, w2_ref, o_ref, *, n_blocks):
    h = h_ref[0]
    rows = h.shape[0]
    keep = lax.broadcasted_iota(jnp.int32, (rows, CMP_HIDDEN), 0) < n_blocks - 1
    outs = []
    for i in range(2 * NSA_KV_GROUPS):
        kv = i // NSA_KV_GROUPS
        first = h[:, 2 * i * CMP_HIDDEN:(2 * i + 1) * CMP_HIDDEN]
        second = h[:, (2 * i + 1) * CMP_HIDDEN:(2 * i + 2) * CMP_HIDDEN]
        nxt = jnp.where(keep, pltpu.roll(second, rows - 1, 0), 0.0)
        act = jax.nn.gelu(first + nxt + bias_ref[kv:kv + 1, :])
        outs.append(_bdot(act, w2_ref[kv]))
    o_ref[0] = jnp.concatenate(outs, axis=-1)


def rows_chunk_hidden(kv_rows, w_chunk):
    n, T = kv_rows.shape[:2]
    chunks = kv_rows.reshape(n * T // CMP_STRIDE, CHUNK_LANES)
    return matmul(chunks, w_chunk, rows_per_step=256).reshape(n, T // CMP_STRIDE, HIDDEN_LANES)


def nsa_query(h, w_in, precise=False):
    n, T, _ = h.shape
    proj = dense(h, w_in, precise)
    q = proj[..., :NSA_HEADS * NSA_HEAD_DIM].reshape(n, T, NSA_KV_GROUPS, NSA_GROUP_HEADS, NSA_HEAD_DIM)
    gate = proj[..., NSA_HEADS * NSA_HEAD_DIM:].reshape(n, T, NSA_KV_GROUPS, NSA_GROUP_HEADS, N_BRANCH)
    return q, gate


def compressed_branch_and_selection(q, t_pos, kc, vc):
    scale = NSA_HEAD_DIM ** -0.5
    n_c = kc.shape[1]
    vis = (jnp.arange(n_c) * CMP_STRIDE + CMP_BLOCK - 1)[None, :] <= t_pos[:, None]
    p_cmp = masked_softmax(jnp.einsum('nqghd,ncgd->nghqc', q, kc) * scale, vis)
    o_cmp = jnp.einsum('nghqc,ncgd->nqghd', p_cmp.astype(vc.dtype), vc)
    ratio = SLC_BLOCK // CMP_STRIDE
    lead = CMP_BLOCK // CMP_STRIDE - 1
    n_s = n_c // ratio
    pg = jnp.pad(jnp.sum(p_cmp, axis=2), ((0, 0), (0, 0), (0, 0), (lead, 0)))
    p_slc = pg[..., 0:ratio * n_s:ratio]
    for o in range(1, ratio + lead):
        p_slc = p_slc + pg[..., o:o + ratio * n_s:ratio]
    jb = jnp.arange(n_s)[None, :]
    jt = (t_pos // SLC_BLOCK)[:, None]
    forced = (jb == 0) | (jb == jt) | (jb == jt - 1)
    score = jnp.where(jb > jt, -jnp.inf, jnp.where(forced, FORCED_SCORE, p_slc))
    before = (score[..., :, None] > score[..., None, :]) | (
        (score[..., :, None] == score[..., None, :]) & (jb[0][:, None] < jb[0][None, :]))
    rank = jnp.sum(before, axis=-2)
    slots = jnp.arange(min(SLC_TOPN, n_s))
    idx = jnp.sum(jnp.where(rank[..., None, :] == slots[:, None], jb[0], 0), axis=-1)
    return o_cmp, idx


def _dot3(x, y, dims):
    x_hi, x_lo = _split_bf16(x)
    y_hi, y_lo = _split_bf16(y)
    dot = lambda a, b: lax.dot_general(a, b, dims, preferred_element_type=jnp.float32)
    return dot(x_hi, y_hi) + (dot(x_hi, y_lo) + dot(x_lo, y_hi))


def _decode_kernel(half_ref, new_ref, phys_ref, q_ref, rows_ref, win_ref, *refs):
    b = pl.program_id(0)
    n_sel = len(refs) - 1
    pages, o_ref = refs[:n_sel], refs[n_sel]
    k_per_group = n_sel // NSA_KV_GROUPS
    hg, dh = NSA_GROUP_HEADS, NSA_HEAD_DIM
    tok_half = lax.broadcasted_iota(jnp.int32, (hg, PAGE_SIZE), 1) // SLC_BLOCK
    rows = rows_ref[0]

    def attend(qg, scores, values, k_new, v_new, new_ok):
        s_new = jnp.where(new_ok, jnp.sum(qg * k_new, axis=-1, keepdims=True), MASKED)
        m = s_new
        for s in scores:
            m = jnp.maximum(m, jnp.max(s, axis=-1, keepdims=True))
        e_new = jnp.exp(s_new - m)
        total, out = e_new, e_new * v_new
        for s, v in zip(scores, values):
            e = jnp.exp(s - m)
            total = total + jnp.sum(e, axis=-1, keepdims=True)
            out = out + _dot3(e, v, _NT)
        return out / jnp.maximum(total, 1e-30)

    outs = []
    for g in range(NSA_KV_GROUPS):
        qg = q_ref[0, g * hg:(g + 1) * hg, :]
        lanes = slice(g * dh, (g + 1) * dh)
        scores, values = [], []
        any_new = jnp.int32(0)
        for k in range(k_per_group):
            j = g * k_per_group + k
            plane = pages[j][0]
            is_new = new_ref[b, j]
            ok = (tok_half == half_ref[b, j]) & (is_new == 0)
            scores.append(jnp.where(ok, _dot3(qg, plane[0, 0], _NN), MASKED))
            values.append(plane[1, 0])
            any_new = jnp.maximum(any_new, is_new)
        o_slc = attend(qg, scores, values, rows[0:1, lanes], rows[1:2, lanes], any_new > 0)
        o_win = attend(qg, [_dot3(qg, win_ref[0, 0, g], _NN)], [win_ref[0, 1, g]],
                       rows[2:3, lanes], rows[3:4, lanes], True)
        outs.append(jnp.concatenate([o_slc, o_win], axis=-1))
    o_ref[0] = jnp.concatenate(outs, axis=0)


def decode_attention(q_rot, idx, page_table, cache_slc_kv, cache_win_kv, slc_new, win_new):
    n = q_rot.shape[0]
    sub = PAGE_SIZE // SLC_BLOCK
    n_past_blk = PAST_LEN // SLC_BLOCK
    assert cache_win_kv.shape[1] <= WINDOW and cache_win_kv.shape[1] <= PAST_LEN
    k_sel = idx.shape[-1]
    flat = idx.reshape(n, NSA_KV_GROUPS * k_sel)
    past = jnp.minimum(flat, n_past_blk - 1)
    phys = jnp.take_along_axis(page_table, past // sub, axis=1).astype(jnp.int32)
    half = (past % sub).astype(jnp.int32)
    is_new = (flat >= n_past_blk).astype(jnp.int32)
    pool = cache_slc_kv.transpose(0, 2, 3, 4, 1)
    window = cache_win_kv.transpose(0, 2, 3, 4, 1)
    rows = jnp.concatenate([slc_new.reshape(n, 2, SLAB), win_new.reshape(n, 2, SLAB)], axis=1)

    def page_spec(j):
        g = j // k_sel
        return pl.BlockSpec((1, 2, 1, NSA_HEAD_DIM, PAGE_SIZE), lambda b, hf, nw, ph: (ph[b, j], 0, g, 0, 0))

    out = pl.pallas_call(
        _decode_kernel,
        grid_spec=pltpu.PrefetchScalarGridSpec(
            num_scalar_prefetch=3,
            grid=(n,),
            in_specs=[pl.BlockSpec((1,) + q_rot.shape[1:], lambda b, hf, nw, ph: (b, 0, 0)),
                      pl.BlockSpec((1,) + rows.shape[1:], lambda b, hf, nw, ph: (b, 0, 0)),
                      pl.BlockSpec((1,) + window.shape[1:], lambda b, hf, nw, ph: (b, 0, 0, 0, 0))]
                     + [page_spec(j) for j in range(NSA_KV_GROUPS * k_sel)],
            out_specs=pl.BlockSpec((1, NSA_HEADS, 2 * NSA_HEAD_DIM), lambda b, hf, nw, ph: (b, 0, 0)),
        ),
        out_shape=jax.ShapeDtypeStruct((n, NSA_HEADS, 2 * NSA_HEAD_DIM), jnp.float32),
        compiler_params=pltpu.CompilerParams(dimension_semantics=("parallel",)),
        name="decode_attention",
    )(half, is_new, phys, q_rot, rows, window, *([pool] * (NSA_KV_GROUPS * k_sel)))
    return out[:, :, :NSA_HEAD_DIM], out[:, :, NSA_HEAD_DIM:]


NSA_SLABS = NSA_GROUP_HEADS
NSA_TILES = NSA_SLABS * NSA_KV_GROUPS
KEY_BLOCK = 128
DENSE_BATCH = 16
MASKED = -1e30
SLC_PER_CMP = SLC_BLOCK // CMP_STRIDE
CMP_LEAD = CMP_BLOCK // CMP_STRIDE - 1
SLAB = NSA_KV_GROUPS * NSA_HEAD_DIM
SEL_LANES = 64


def _group_tiles(q_ref, qs_ref):
    tq = q_ref.shape[1]
    lane = lax.broadcasted_iota(jnp.int32, (tq, SLAB), 1)
    for i in range(NSA_SLABS):
        qs = q_ref[0, :, i * 128:(i + 1) * 128]
        for g in range(NSA_KV_GROUPS):
            in_group = (lane >= g * NSA_HEAD_DIM) & (lane < (g + 1) * NSA_HEAD_DIM)
            qs_ref[2 * i + g] = jnp.where(in_group, qs, jnp.zeros_like(qs)).astype(jnp.bfloat16)


def _merge_groups(o_ref, tiles, gate_ref, branch):
    tq = tiles[0].shape[0]
    lane = lax.broadcasted_iota(jnp.int32, (tq, SLAB), 1)
    gates = gate_ref[0]
    col = lambda g, i: (g * NSA_GROUP_HEADS + i) * N_BRANCH + branch
    for i in range(NSA_SLABS):
        lo = tiles[2 * i] * gates[:, col(0, i):col(0, i) + 1]
        hi = tiles[2 * i + 1] * gates[:, col(1, i):col(1, i) + 1]
        o_ref[0, :, i * 128:(i + 1) * 128] = jnp.where(lane < NSA_HEAD_DIM, lo, hi)


def _nsa_cmp_kernel(q_ref, kc_ref, vc_ref, gate_ref, o_ref, sel_ref, qs_ref):
    qi = pl.program_id(1)
    tq = q_ref.shape[1]
    nc = kc_ref.shape[1]
    ns = nc // SLC_PER_CMP
    t0 = qi * tq
    _group_tiles(q_ref, qs_ref)
    kc = kc_ref[0]
    vc = vc_ref[0]
    t_row = t0 + lax.broadcasted_iota(jnp.int32, (tq, nc), 0)
    c_pos = lax.broadcasted_iota(jnp.int32, (tq, nc), 1) * CMP_STRIDE + (CMP_BLOCK - 1)
    vis = c_pos <= t_row
    tiles = range(NSA_TILES)
    s = [_mm(qs_ref[r], kc, _NT) for r in tiles]
    p = []
    for r in tiles:
        sr = jnp.where(vis, s[r], MASKED)
        m = jnp.max(sr, axis=-1, keepdims=True)
        e = jnp.where(vis, jnp.exp(sr - m), 0.0)
        p.append(e / jnp.maximum(jnp.sum(e, axis=-1, keepdims=True), 1e-30))
    _merge_groups(o_ref, [_mm(p[r], vc, _NN) for r in tiles], gate_ref, 0)

    jrow = lax.broadcasted_iota(jnp.int32, (ns, nc), 0)
    ccol = lax.broadcasted_iota(jnp.int32, (ns, nc), 1)
    pool = ((ccol >= SLC_PER_CMP * jrow - CMP_LEAD) & (ccol < SLC_PER_CMP * (jrow + 1))).astype(jnp.bfloat16)
    jb = lax.broadcasted_iota(jnp.int32, (ns, tq), 0)
    jt = (t0 + lax.broadcasted_iota(jnp.int32, (ns, tq), 1)) // SLC_BLOCK
    forced = (jb == 0) | (jb == jt) | (jb == jt - 1)
    sel_t = []
    for g in range(NSA_KV_GROUPS):
        pg = p[g]
        for i in range(1, NSA_SLABS):
            pg = pg + p[2 * i + g]
        pg_hi = pg.astype(jnp.bfloat16)
        pg_lo = (pg - pg_hi.astype(jnp.float32)).astype(jnp.bfloat16)
        p_slc = (lax.dot_general(pool, pg_hi, _NT, preferred_element_type=jnp.float32)
                 + lax.dot_general(pool, pg_lo, _NT, preferred_element_type=jnp.float32))
        score = jnp.where(jb > jt, -jnp.inf, jnp.where(forced, FORCED_SCORE, p_slc))
        rank = jnp.zeros((ns, tq), jnp.float32)
        for i in range(ns):
            row = score[i:i + 1, :]
            tie = jnp.where(jb > i, 1.0, 0.0)
            rank = rank + jnp.where(row > score, 1.0, 0.0) + jnp.where(row == score, tie, 0.0)
        sel_t.append(jnp.where(rank < min(SLC_TOPN, ns), 1.0, 0.0))
        if ns < SEL_LANES:
            sel_t.append(jnp.zeros((SEL_LANES - ns, tq), jnp.float32))
    sel_ref[0] = jnp.concatenate(sel_t, axis=0).T.astype(sel_ref.dtype)


def _nsa_dense_kernel(q_ref, k_ref, vt_ref, sel_ref, gate_ref, o_ref, qs_ref, m_ref, acc_ref, *, windowed):
    qi = pl.program_id(1)
    tq = q_ref.shape[1]
    t0 = qi * tq
    _group_tiles(q_ref, qs_ref)
    m_ref[...] = jnp.full(m_ref.shape, MASKED, jnp.float32)
    acc_ref[...] = jnp.zeros(acc_ref.shape, jnp.float32)
    k_row = lax.broadcasted_iota(jnp.int32, (KEY_BLOCK, tq), 0)
    t_lane = t0 + lax.broadcasted_iota(jnp.int32, (KEY_BLOCK, tq), 1)
    tiles = range(NSA_TILES)
    first = jnp.maximum(qi - WINDOW // KEY_BLOCK, 0) if windowed else 0
    last = qi * (tq // KEY_BLOCK) + tq // KEY_BLOCK - 1

    def key_block(kb, carry):
        start = pl.multiple_of(kb * KEY_BLOCK, KEY_BLOCK)
        kblk = k_ref[0, pl.ds(start, KEY_BLOCK), :]
        vt = vt_ref[0, :, pl.ds(start, KEY_BLOCK)]
        k_pos = start + k_row
        if windowed:
            allowed = [(k_pos <= t_lane) & (k_pos >= t_lane - WINDOW)] * NSA_KV_GROUPS
        else:
            sel = sel_ref[0]
            n_sel = NSA_KV_GROUPS * SEL_LANES
            e_key = lax.broadcasted_iota(jnp.int32, (KEY_BLOCK, n_sel), 0)
            e_lane = lax.broadcasted_iota(jnp.int32, (KEY_BLOCK, n_sel), 1)
            blk = kb * (KEY_BLOCK // SLC_BLOCK) + e_key // SLC_BLOCK
            allowed = []
            for g in range(NSA_KV_GROUPS):
                expand = (e_lane == blk + g * SEL_LANES).astype(jnp.bfloat16)
                picked = lax.dot_general(expand, sel, _NT, preferred_element_type=jnp.float32)
                allowed.append((picked > 0.5) & (k_pos <= t_lane))
        bias = [jnp.where(ok, 0.0, MASKED) for ok in allowed]
        d_row = lax.broadcasted_iota(jnp.int32, vt.shape, 0)
        vt_g = [jnp.where((d_row >= g * NSA_HEAD_DIM) & (d_row < (g + 1) * NSA_HEAD_DIM), vt, jnp.ones_like(vt))
                for g in range(NSA_KV_GROUPS)]
        s = [_mm(kblk, qs_ref[r], _NT) for r in tiles]
        for lo in range(0, NSA_TILES, DENSE_BATCH):
            batch = range(lo, lo + DENSE_BATCH)
            p, alpha = {}, {}
            for r in batch:
                sr = s[r] + bias[r % NSA_KV_GROUPS]
                m_prev = m_ref[r]
                m_new = jnp.maximum(m_prev, jnp.max(sr, axis=0, keepdims=True))
                alpha[r] = jnp.exp(m_prev - m_new)
                p[r] = jnp.exp(sr - m_new).astype(jnp.bfloat16)
                m_ref[r] = m_new
            pv = {r: jnp.dot(vt_g[r % NSA_KV_GROUPS], p[r], preferred_element_type=jnp.float32) for r in batch}
            for r in batch:
                acc_ref[r] = alpha[r] * acc_ref[r] + pv[r]
        return carry

    lax.fori_loop(first, last + 1, key_block, 0)
    out = []
    for r in tiles:
        acc = acc_ref[r]
        sum_row = (1 - r % NSA_KV_GROUPS) * NSA_HEAD_DIM
        out.append((acc / jnp.maximum(acc[sum_row:sum_row + 1, :], 1e-30)).T)
    _merge_groups(o_ref, out, gate_ref, 2 if windowed else 1)


def nsa_prompt_attention(q, q_rot, kc, vc, kv_att, gates):
    n, T, D = q.shape
    tq = Q_BLOCK
    qspec = pl.BlockSpec((1, tq, D), lambda b, i: (b, i, 0))
    whole = lambda a: pl.BlockSpec((1,) + a.shape[1:], lambda b, i: (b, 0, 0))
    lanes = lambda j: pl.BlockSpec((1, T, SLAB), lambda b, i: (b, 0, j))
    n_sel = NSA_KV_GROUPS * SEL_LANES
    sel_spec = pl.BlockSpec((1, tq, n_sel), lambda b, i: (b, i, 0))
    gate_spec = pl.BlockSpec((1, tq, gates.shape[2]), lambda b, i: (b, i, 0))
    params = pltpu.CompilerParams(dimension_semantics=("parallel", "arbitrary"),
                                  vmem_limit_bytes=48 * 1024 * 1024)
    qs_scratch = pltpu.VMEM((NSA_TILES, tq, SLAB), jnp.bfloat16)
    o_cmp, sel = pl.pallas_call(
        _nsa_cmp_kernel,
        grid=(n, T // tq),
        in_specs=[qspec, whole(kc), whole(vc), gate_spec],
        out_specs=[qspec, sel_spec],
        out_shape=[jax.ShapeDtypeStruct((n, T, D), jnp.float32),
                   jax.ShapeDtypeStruct((n, T, n_sel), jnp.bfloat16)],
        scratch_shapes=[qs_scratch],
        compiler_params=params,
        name="nsa_cmp_select",
    )(q, kc, vc, gates)
    stat = pltpu.VMEM((NSA_TILES, 1, tq), jnp.float32)
    acc = pltpu.VMEM((NSA_TILES, SLAB, tq), jnp.float32)
    v_t = jnp.swapaxes(jnp.concatenate([kv_att[:, :, SLAB:2 * SLAB], kv_att[:, :, 3 * SLAB:]], axis=-1), 1, 2)
    rows = lambda j: pl.BlockSpec((1, SLAB, T), lambda b, i: (b, j, 0))

    def dense(windowed, branch, name):
        return pl.pallas_call(
            functools.partial(_nsa_dense_kernel, windowed=windowed),
            grid=(n, T // tq),
            in_specs=[qspec, lanes(2 * branch), rows(branch), sel_spec, gate_spec],
            out_specs=qspec,
            out_shape=jax.ShapeDtypeStruct((n, T, D), jnp.float32),
            scratch_shapes=[qs_scratch, stat, acc],
            compiler_params=params,
            name=name,
        )(q_rot, kv_att, v_t, sel, gates)

    return o_cmp, dense(False, 0, "nsa_selected"), dense(True, 1, "nsa_window")


def _rope_tables(pos):
    half = ROT_DIM // 2
    inv = ROPE_THETA ** (-2.0 * jnp.arange(half, dtype=jnp.float32) / ROT_DIM)
    ang = pos.astype(jnp.float32)[:, None] * inv[None, :]
    rest = NSA_HEAD_DIM - ROT_DIM
    cos = jnp.concatenate([jnp.cos(ang), jnp.cos(ang), jnp.ones((pos.shape[0], rest), jnp.float32)], axis=1)
    sin = jnp.concatenate([-jnp.sin(ang), jnp.sin(ang), jnp.zeros((pos.shape[0], rest), jnp.float32)], axis=1)
    return jnp.tile(cos, (1, NSA_KV_GROUPS)), jnp.tile(sin, (1, NSA_KV_GROUPS))


def _nsa_pre_kernel(x_ref, gkv_ref, gmix_ref, cos_ref, sin_ref, wkv_ref, wq_ref, wg_ref,
                    cmp_ref, slc_ref, win_ref, kvb_ref, q_ref, qr_ref, gate_ref):
    xh = _rms(x_ref[0])
    cos, sin = cos_ref[...], sin_ref[...]
    low = lax.broadcasted_iota(jnp.int32, cos.shape, 1) % NSA_HEAD_DIM < ROT_DIM // 2

    def rope(t):
        swapped = jnp.where(low, pltpu.roll(t, SLAB - ROT_DIM // 2, 1), pltpu.roll(t, ROT_DIM // 2, 1))
        return t * cos + swapped * sin

    kv = _bdot(xh * gkv_ref[...], wkv_ref[...])
    part = lambda j: kv[:, j * SLAB:(j + 1) * SLAB]
    k_slc, k_win = rope(part(2)), rope(part(4))
    cmp_ref[0] = kv[:, :2 * SLAB]
    slc_ref[0] = jnp.concatenate([k_slc, part(3)], axis=-1)
    win_ref[0] = jnp.concatenate([k_win, part(5)], axis=-1)
    kvb_ref[0] = jnp.concatenate([k_slc, part(3), k_win, part(5)], axis=-1).astype(jnp.bfloat16)
    h = xh * gmix_ref[...]
    q = _bdot(h, wq_ref[...])
    q_ref[0] = q.astype(jnp.bfloat16)
    qr_ref[0] = jnp.concatenate([rope(q[:, i * SLAB:(i + 1) * SLAB]) for i in range(NSA_SLABS)],
                                axis=-1).astype(jnp.bfloat16)
    gate_ref[0] = jax.nn.sigmoid(_bdot(h, wg_ref[...]))


def nsa_pre(x, pos, norm_kv, norm_mix, w_kv, w_in):
    n, T, D = x.shape
    tm = min(ROW_TILE, T)
    nq = NSA_HEADS * NSA_HEAD_DIM
    cos, sin = _rope_tables(pos)
    w_q = (_to_slabs(w_in[:, :nq]) * NSA_HEAD_DIM ** -0.5).astype(jnp.bfloat16)
    w_g = jnp.pad(w_in[:, nq:], ((0, 0), (0, SLAB - (w_in.shape[1] - nq)))).astype(jnp.bfloat16)
    seq = lambda width: pl.BlockSpec((1, tm, width), lambda b, t: (b, t, 0))
    full = lambda a: pl.BlockSpec(a.shape, lambda b, t: (0,) * a.ndim)
    table = pl.BlockSpec((tm, SLAB), lambda b, t: (t, 0))
    consts = [norm_kv.reshape(1, D), norm_mix.reshape(1, D)]
    weights = [w_kv.astype(jnp.bfloat16), w_q, w_g]
    widths = [2 * SLAB, 2 * SLAB, 2 * SLAB, 4 * SLAB, nq, nq, SLAB]
    dtypes = [jnp.float32] * 3 + [jnp.bfloat16] * 3 + [jnp.float32]
    return pl.pallas_call(
        _nsa_pre_kernel,
        grid=(n, T // tm),
        in_specs=[seq(D)] + [full(a) for a in consts] + [table, table] + [full(a) for a in weights],
        out_specs=[seq(w) for w in widths],
        out_shape=[jax.ShapeDtypeStruct((n, T, w), dt) for w, dt in zip(widths, dtypes)],
        compiler_params=pltpu.CompilerParams(dimension_semantics=("parallel", "parallel"),
                                             vmem_limit_bytes=48 * 1024 * 1024),
        name="nsa_pre",
    )(x, *consts, cos, sin, *weights)


def _to_slabs(x):
    lead = x.shape[:-1]
    x = x.reshape(lead + (NSA_KV_GROUPS, NSA_GROUP_HEADS, NSA_HEAD_DIM))
    return jnp.swapaxes(x, -3, -2).reshape(lead + (NSA_HEADS * NSA_HEAD_DIM,))


def _from_slabs(x):
    lead = x.shape[:-1]
    x = x.reshape(lead + (NSA_GROUP_HEADS, NSA_KV_GROUPS, NSA_HEAD_DIM))
    return jnp.swapaxes(x, -3, -2).reshape(lead + (NSA_HEADS * NSA_HEAD_DIM,))


def nsa_layer_prompt(x, norm_kv, norm_mix, w_kv, cmp_pos, cmp_w1, cmp_b1, cmp_w2, w_in, w_o, norm_next, w_router):
    n, T, D = x.shape
    cmp_kv, slc_kv, win_kv, kv_att, q, q_rot, gates = nsa_pre(x, jnp.arange(T), norm_kv, norm_mix, w_kv, w_in)
    hidden = rows_chunk_hidden(cmp_kv, _chunk_weights(cmp_w1))
    kc, vc = compress_from_hidden(hidden, cmp_pos, cmp_w1, cmp_b1, cmp_w2)
    lanes = lambda a: a.reshape(n, a.shape[1], SLAB).astype(jnp.bfloat16)
    branches = nsa_prompt_attention(q, q_rot, lanes(kc), lanes(vc), kv_att, gates)
    flat = lambda a: a.reshape(n * T, a.shape[-1])
    x_out, h_out, logits = out_proj([flat(o) for o in branches], None, flat(x), _to_slabs(w_o.T).T, norm_next,
                                    w_router)
    return x_out, h_out, logits, cmp_kv, slc_kv, win_kv


def sample_kv_context(cmp_new, slc_new, win_new, cache_cmp_kv, cache_slc_kv, cache_win_kv, page_table,
                      cmp_pos, cmp_w1, cmp_b1, cmp_w2):
    n, S = cmp_new.shape[:2]
    n_new_blk = -(-S // SLC_BLOCK)
    pad = ((0, 0), (0, n_new_blk * SLC_BLOCK - S), (0, 0), (0, 0), (0, 0))
    w_chunk = _chunk_weights(cmp_w1)
    hidden = jnp.concatenate([paged_chunk_hidden(cache_cmp_kv, page_table, _token_weights(cmp_w1)),
                              rows_chunk_hidden(jnp.pad(cmp_new.astype(cache_cmp_kv.dtype), pad), w_chunk)], axis=1)
    kc, vc = compress_from_hidden(hidden, cmp_pos, cmp_w1, cmp_b1, cmp_w2)
    win_all = jnp.concatenate([cache_win_kv, win_new.astype(cache_win_kv.dtype)], axis=1)
    return kc, vc, win_all


def sample_nsa(h, kc, vc, slc_new, win_new, cache_slc_kv, cache_win_kv, page_table, w_in, w_o):
    n, S, _ = h.shape
    assert S == 1
    q, gate = nsa_query(h, w_in, precise=True)
    t_pos = PAST_LEN + jnp.arange(S)
    with jax.default_matmul_precision("highest"):
        o_cmp, idx = compressed_branch_and_selection(q, t_pos, kc, vc)
    q_rot = (rope_partial(q, t_pos) * NSA_HEAD_DIM ** -0.5).reshape(n, NSA_HEADS, NSA_HEAD_DIM)
    o_slc, o_win = decode_attention(q_rot, idx[:, :, 0], page_table, cache_slc_kv, cache_win_kv, slc_new, win_new)
    g = jax.nn.sigmoid(gate.astype(jnp.float32)).astype(q.dtype)
    o = g[..., 0:1] * o_cmp + g[..., 1:2] * o_slc.reshape(q.shape) + g[..., 2:3] * o_win.reshape(q.shape)
    return dense(o.reshape(n, S, NSA_HEADS * NSA_HEAD_DIM), w_o, precise=True)


def kernel(x_prompt, x_sample, state_wkv, state_shift, cache_cmp_kv, cache_slc_kv, cache_win_kv, page_table, norm_mix, norm_ffn, norm_kv, norm_final, rw_mu, rw_w_rkv, rw_w0, rw_w1, rw_w2, rw_a0, rw_a1, rw_a2, rw_g1, rw_g2, rw_k_k, rw_k_a, rw_r_k, rw_gn_w, rw_gn_b, rw_w_o, nsa_w_kv, nsa_cmp_pos, nsa_cmp_w1, nsa_cmp_b1, nsa_cmp_w2, nsa_w_in, nsa_w_o, ffn_w_gu, ffn_w_down, moe_router, moe_w_gu, moe_w_down):
    cmp_params = (nsa_cmp_pos, nsa_cmp_w1, nsa_cmp_b1, nsa_cmp_w2)
    assert DEPTH == 2 and N_A_LAYERS == 1
    D = D_MODEL

    n_p, T = x_prompt.shape[:2]
    x1, h1, wkv_fin, h_last = rwkv7_layer(
        x_prompt, jnp.zeros((n_p, D), x_prompt.dtype), jnp.zeros((n_p, RWKV_HEADS, RWKV_HEAD, RWKV_HEAD), jnp.float32),
        norm_mix[0], rw_mu[0], rw_w_rkv[0], rw_w0[0], rw_w1[0], rw_w2[0], rw_a0[0], rw_a1[0], rw_a2[0],
        rw_g1[0], rw_g2[0], rw_k_k[0], rw_k_a[0], rw_r_k[0], rw_gn_w[0], rw_gn_b[0], rw_w_o[0], norm_ffn[0])
    wkv_p, shift_p = wkv_fin[None], h_last[None]
    x2 = swiglu_residual(x1, h1, ffn_w_gu[0], ffn_w_down[0])
    x_p, h_p, logits_p, cmp_rows, slc_rows, win_rows = nsa_layer_prompt(
        x2.reshape(n_p, T, D), norm_kv, norm_mix[1], nsa_w_kv, *cmp_params, nsa_w_in[0], nsa_w_o[0], norm_ffn[1],
        moe_router[0])
    kv_shape = (n_p, T, 2, NSA_KV_GROUPS, NSA_HEAD_DIM)
    cmp_kv_p, slc_kv_p = cmp_rows.reshape(kv_shape), slc_rows.reshape(kv_shape)
    win_kv_p = win_rows.reshape(kv_shape)[:, -min(WINDOW, T):]

    pos_s = PAST_LEN + jnp.arange(x_sample.shape[1], dtype=jnp.int32)
    h = rmsnorm(x_sample, norm_mix[0])
    y, s_fin, h_last_s = rwkv7_time_mix(
        h, state_shift[0], state_wkv[0], rw_mu[0], rw_w_rkv[0], rw_w0[0], rw_w1[0], rw_w2[0],
        rw_a0[0], rw_a1[0], rw_a2[0], rw_g1[0], rw_g2[0], rw_k_k[0], rw_k_a[0],
        rw_r_k[0], rw_gn_w[0], rw_gn_b[0], rw_w_o[0], precise=True)
    wkv_s, shift_s = s_fin[None], h_last_s[None]
    x_s = x_sample + y
    x_s = x_s + swiglu(rmsnorm(x_s, norm_ffn[0]), ffn_w_gu[0], ffn_w_down[0], precise=True)
    cmp_kv_s, slc_kv_s, win_new = shared_kv_rows(x_s, norm_kv, nsa_w_kv, pos_s, precise=True)
    kc_s, vc_s, win_all = sample_kv_context(cmp_kv_s, slc_kv_s, win_new, cache_cmp_kv, cache_slc_kv, cache_win_kv,
                                            page_table, *cmp_params)
    x_s = x_s + sample_nsa(rmsnorm(x_s, norm_mix[1]), kc_s, vc_s, slc_kv_s, win_new, cache_slc_kv, cache_win_kv,
                           page_table, nsa_w_in[0], nsa_w_o[0])
    win_kv_s = win_all[:, -cache_win_kv.shape[1]:]
    x_s = x_s.reshape(-1, D)
    h_s = rmsnorm(x_s, norm_ffn[1])
    logits_s = matmul_precise(h_s, moe_router[0])

    n_tok_p = n_p * T
    logits = jnp.concatenate([logits_p[:, :N_EXPERTS], logits_s], axis=0)
    y0, y1, gate = moe_swiglu(jnp.concatenate([h_p, h_s.astype(jnp.bfloat16)], axis=0), logits,
                              moe_w_gu[0], moe_w_down[0])
    gate = jnp.pad(gate, ((0, 0), (0, 128 - TOP_K)))
    y_prompt = combine_norm(x_p, y0[:n_tok_p], y1[:n_tok_p], gate[:n_tok_p], norm_final).reshape(x_prompt.shape)
    y_sample = combine_norm(x_s, y0[n_tok_p:], y1[n_tok_p:], gate[n_tok_p:], norm_final).reshape(x_sample.shape)


    return (y_prompt, y_sample, wkv_p, shift_p, cmp_kv_p, slc_kv_p, win_kv_p,
            wkv_s, shift_s, cmp_kv_s, slc_kv_s, win_kv_s)
```

```python
import functools
import math

import jax
import jax.numpy as jnp
from jax import lax
from jax.experimental import pallas as pl
from jax.experimental.pallas import tpu as pltpu

D_MODEL = 1024
DEPTH = 2
PAST_LEN = 16384
PAGE_SIZE = 128
N_A_LAYERS = DEPTH // 2
RWKV_HEAD = 64
RWKV_HEADS = D_MODEL // RWKV_HEAD
DECAY_SCALE = math.exp(-0.5)
GN_EPS = RWKV_HEAD * 1e-5
NSA_HEADS = 16
NSA_HEAD_DIM = 64
NSA_KV_GROUPS = 2
NSA_GROUP_HEADS = NSA_HEADS // NSA_KV_GROUPS
N_BRANCH = 3
CMP_BLOCK = 32
CMP_STRIDE = 16
CMP_HIDDEN = 128
SLC_BLOCK = 64
SLC_TOPN = 16
WINDOW = 512
Q_BLOCK = 128
FORCED_SCORE = 1e4
ROPE_THETA = 500000.0
ROT_DIM = NSA_HEAD_DIM // 4
N_EXPERTS = 8
TOP_K = 2
MOE_BLOCK = 256
NORM_EPS = 1e-6


def _rmsnorm_kernel(x_ref, g_ref, o_ref):
    x = x_ref[...]
    y = x * lax.rsqrt(jnp.mean(x * x, axis=-1, keepdims=True) + NORM_EPS)
    o_ref[...] = y * g_ref[...]


def rmsnorm(x, g):
    shp = x.shape
    x2 = x.reshape(-1, shp[-1])
    rows = x2.shape[0]
    tm = min(rows, 512)
    out = pl.pallas_call(
        _rmsnorm_kernel,
        grid=(rows // tm,),
        in_specs=[pl.BlockSpec((tm, shp[-1]), lambda i: (i, 0)),
                  pl.BlockSpec((1, shp[-1]), lambda i: (0, 0))],
        out_specs=pl.BlockSpec((tm, shp[-1]), lambda i: (i, 0)),
        out_shape=jax.ShapeDtypeStruct(x2.shape, x.dtype),
        name="rmsnorm",
    )(x2, g.reshape(1, -1))
    return out.reshape(shp)


def rope_partial(x, pos):
    half = ROT_DIM // 2
    inv = ROPE_THETA ** (-2.0 * jnp.arange(half, dtype=jnp.float32) / ROT_DIM)
    ang = pos.astype(jnp.float32)[:, None] * inv[None, :]
    shape = (1, pos.shape[0]) + (1,) * (x.ndim - 3) + (half,)
    cos = jnp.cos(ang).reshape(shape)
    sin = jnp.sin(ang).reshape(shape)
    xf = x.astype(jnp.float32)
    x1, x2 = xf[..., :half], xf[..., half:ROT_DIM]
    out = jnp.concatenate([x1 * cos - x2 * sin, x2 * cos + x1 * sin, xf[..., ROT_DIM:]], axis=-1)
    return out.astype(x.dtype)


def masked_softmax(s, mask):
    s = jnp.where(mask, s.astype(jnp.float32), -jnp.inf)
    m = jnp.max(s, axis=-1, keepdims=True)
    m = jnp.where(jnp.isfinite(m), m, 0.0)
    e = jnp.where(mask, jnp.exp(s - m), 0.0)
    return e / jnp.maximum(jnp.sum(e, axis=-1, keepdims=True), 1e-30)


FF_CHUNK = 1408
SWIGLU_ROWS = 512


def _swiglu_kernel(blk_e_ref, n_used_ref, x_ref, wg_ref, wu_ref, wd_ref, *rest):
    o_ref = rest[-1]
    i = pl.program_id(0)
    f = pl.program_id(1)

    @pl.when(i < n_used_ref[0])
    def _():
        x = x_ref[...]
        g = jnp.dot(x, wg_ref[0], preferred_element_type=jnp.float32)
        u = jnp.dot(x, wu_ref[0], preferred_element_type=jnp.float32)
        act = (g * jax.nn.sigmoid(g) * u).astype(jnp.bfloat16)
        y = jnp.dot(act, wd_ref[0], preferred_element_type=jnp.float32)

        @pl.when(f == 0)
        def _():
            o_ref[...] = y + rest[0][...] if len(rest) == 2 else y

        @pl.when(f > 0)
        def _():
            o_ref[...] += y

    @pl.when(i >= n_used_ref[0])
    def _():
        o_ref[...] = jnp.zeros(o_ref.shape, o_ref.dtype)


def grouped_swiglu(xb, blk_e, n_used, w_gu, w_down, res=None):
    rows, d = xb.shape
    extra = [] if res is None else [res]
    b = min(SWIGLU_ROWS, rows)
    n_blk = rows // b
    ff = w_down.shape[1]
    tf = FF_CHUNK
    n_f = ff // tf
    chunk = lambda i, f, be, nu: jnp.where(i < nu[0], f, n_f - 1)
    return pl.pallas_call(
        _swiglu_kernel,
        grid_spec=pltpu.PrefetchScalarGridSpec(
            num_scalar_prefetch=2,
            grid=(n_blk, n_f),
            in_specs=[
                pl.BlockSpec((b, d), lambda i, f, be, nu: (i, 0)),
                pl.BlockSpec((1, d, tf), lambda i, f, be, nu: (be[i], 0, chunk(i, f, be, nu))),
                pl.BlockSpec((1, d, tf), lambda i, f, be, nu: (be[i], 0, n_f + chunk(i, f, be, nu))),
                pl.BlockSpec((1, tf, d), lambda i, f, be, nu: (be[i], chunk(i, f, be, nu), 0)),
            ] + [pl.BlockSpec((b, d), lambda i, f, be, nu: (i, 0))] * len(extra),
            out_specs=pl.BlockSpec((b, d), lambda i, f, be, nu: (i, 0)),
        ),
        out_shape=jax.ShapeDtypeStruct((rows, d), jnp.float32),
        compiler_params=pltpu.CompilerParams(dimension_semantics=("arbitrary", "arbitrary"),
                                             vmem_limit_bytes=56 * 1024 * 1024),
        name="grouped_swiglu",
    )(blk_e.astype(jnp.int32), jnp.reshape(n_used, (1,)).astype(jnp.int32), xb, w_gu, w_gu, w_down, *extra)


def swiglu(h, w_gu, w_down, precise=False):
    g, u = jnp.split(dense(h, w_gu, precise), 2, axis=-1)
    return dense(jax.nn.silu(g) * u, w_down, precise)


def swiglu_residual(x, h, w_gu, w_down):
    n_blk = h.shape[0] // min(SWIGLU_ROWS, h.shape[0])
    return grouped_swiglu(h, jnp.zeros((n_blk,), jnp.int32), jnp.int32(n_blk),
                          w_gu.astype(jnp.bfloat16)[None], w_down.astype(jnp.bfloat16)[None], res=x)


def _combine_norm_kernel(x_ref, y0_ref, y1_ref, g_ref, gain_ref, o_ref):
    g = g_ref[...]
    x = x_ref[...] + (y0_ref[...] * g[:, 0:1] + y1_ref[...] * g[:, 1:2])
    o_ref[...] = _rms(x) * gain_ref[...]


def combine_norm(x, y0, y1, gate, gain):
    m, d = x.shape
    tm = min(2 * ROW_TILE, m)
    rows = lambda width: pl.BlockSpec((tm, width), lambda i: (i, 0))
    return pl.pallas_call(
        _combine_norm_kernel,
        grid=(m // tm,),
        in_specs=[rows(d), rows(d), rows(d), rows(gate.shape[1]), pl.BlockSpec((1, d), lambda i: (0, 0))],
        out_specs=rows(d),
        out_shape=jax.ShapeDtypeStruct((m, d), jnp.float32),
        compiler_params=pltpu.CompilerParams(dimension_semantics=("parallel",)),
        name="combine_norm",
    )(x, y0, y1, gate, gain.reshape(1, d))


def moe_swiglu(xt, logits, w_gu_e, w_down_e):
    n_tok, d = xt.shape
    b = SWIGLU_ROWS
    experts = jnp.arange(N_EXPERTS)[None, :]
    e0 = jnp.argmax(logits, axis=-1)
    v0 = jnp.max(logits, axis=-1)
    rest = jnp.where(experts == e0[:, None], -jnp.inf, logits)
    e1 = jnp.argmax(rest, axis=-1)
    v1 = jnp.max(rest, axis=-1)
    top_idx = jnp.stack([e0, e1], axis=-1).astype(jnp.int32)
    gate = jax.nn.softmax(jnp.stack([v0, v1], axis=-1), axis=-1)
    nk = n_tok * TOP_K
    flat_e = top_idx.reshape(nk)
    onehot = (flat_e[:, None] == jnp.arange(N_EXPERTS)[None, :]).astype(jnp.int32)
    before = jnp.cumsum(onehot, axis=0) - onehot
    counts = jnp.sum(onehot, axis=0)
    padded = (counts + b - 1) // b * b
    ends_pad = jnp.cumsum(padded)
    starts_pad = ends_pad - padded
    dest = jnp.sum(onehot * (starts_pad[None, :] + before), axis=1)
    n_rows = (nk + b - 1) // b * b + N_EXPERTS * b
    n_blk = n_rows // b
    flat_tok = jnp.repeat(jnp.arange(n_tok, dtype=jnp.int32), TOP_K)
    row_tok = jnp.full((n_rows,), n_tok, jnp.int32).at[dest].set(flat_tok)
    blk_e = jnp.minimum(jnp.searchsorted(ends_pad, jnp.arange(n_blk) * b, side='right'), N_EXPERTS - 1)
    x_pad = jnp.concatenate([xt.astype(jnp.bfloat16), jnp.zeros((1, d), jnp.bfloat16)], axis=0)
    yb = grouped_swiglu(x_pad[row_tok], blk_e, ends_pad[-1] // b,
                        w_gu_e.astype(jnp.bfloat16), w_down_e.astype(jnp.bfloat16))
    dest = dest.reshape(n_tok, TOP_K)
    return yb[dest[:, 0]], yb[dest[:, 1]], gate


WKV_CHUNK = 64
WKV_HEADS_PER_STEP = 16

_NN = (((1,), (0,)), ((), ()))
_NT = (((1,), (1,)), ((), ()))
_TN = (((0,), (0,)), ((), ()))


def _mm(x, y, dims):
    return lax.dot_general(x.astype(jnp.bfloat16), y.astype(jnp.bfloat16), dims,
                           preferred_element_type=jnp.float32)


def _wkv7_chunk_kernel(r_ref, lw_ref, k_ref, v_ref, kk_ref, a_ref, rk_ref, gnw_ref, gnb_ref, s0_ref,
                       y_ref, sout_ref, state_ref):
    c = pl.program_id(2)
    L = r_ref.shape[1]
    hb = state_ref.shape[0]
    N = RWKV_HEAD

    @pl.when(c == 0)
    def _():
        state_ref[...] = s0_ref[0]

    row = lax.broadcasted_iota(jnp.int32, (L, L), 0)
    col = lax.broadcasted_iota(jnp.int32, (L, L), 1)
    strict = row > col
    incl = row >= col
    tri = incl.astype(jnp.bfloat16)
    eye_n = lax.broadcasted_iota(jnp.int32, (N, N), 0) == lax.broadcasted_iota(jnp.int32, (N, N), 1)

    lw = lw_ref[0]
    lw_hi, lw_lo = _split_bf16(lw)
    cum = (lax.dot_general(tri, lw_hi, _NN, preferred_element_type=jnp.float32)
           + lax.dot_general(tri, lw_lo, _NN, preferred_element_type=jnp.float32))
    cum_last = cum[L - 1:L, :]
    e_neg = jnp.exp(-cum)
    e_tail = jnp.exp(cum_last - cum)
    e_prev = jnp.exp(cum - lw)
    r_in = r_ref[0]
    k_in = k_ref[0]
    r_all = r_in * jnp.exp(cum)
    kt_all = k_in * e_neg
    kh_all = k_in * e_tail
    wl_all = jnp.exp(cum_last)

    heads = range(hb)
    per_head = lambda t: [t[:, j * N:(j + 1) * N] for j in heads]
    rt, kt, kh, v, wl = (per_head(t) for t in (r_all, kt_all, kh_all, v_ref[0], wl_all))
    kk_raw, a_gate, e_prev, e_neg, e_tail = (per_head(t) for t in (kk_ref[0], a_ref[0], e_prev, e_neg, e_tail))
    at, bt, bh = [], [], []
    for j in heads:
        norm = jnp.sqrt(jnp.sum(kk_raw[j] * kk_raw[j], axis=-1, keepdims=True))
        kk = kk_raw[j] / jnp.maximum(norm, 1e-12)
        b = kk * a_gate[j]
        at.append(-kk * e_prev[j])
        bt.append(b * e_neg[j])
        bh.append(b * e_tail[j])
    a_ab = [jnp.where(strict, _mm(at[j], bt[j], _NT), 0.0) for j in heads]
    a_ak = [jnp.where(strict, _mm(at[j], kt[j], _NT), 0.0) for j in heads]
    r_b = [jnp.where(incl, _mm(rt[j], bt[j], _NT), 0.0) for j in heads]
    r_k = [jnp.where(incl, _mm(rt[j], kt[j], _NT), 0.0) for j in heads]
    av = [_mm(a_ak[j], v[j], _NN) for j in heads]
    ht = [_mm(v[j], kh[j], _TN) for j in heads]
    yp = [_mm(r_k[j], v[j], _NN) for j in heads]
    pw = a_ab
    inv_a = a_ab
    n = 1
    while 2 * n < L:
        pw = [_mm(pw[j], pw[j], _NN) for j in heads]
        inv_a = [inv_a[j] + pw[j] + _mm(inv_a[j], pw[j], _NN) for j in heads]
        n *= 2
    ap = [at[j] + _mm(inv_a[j], at[j], _NN) for j in heads]
    vp = [av[j] + _mm(inv_a[j], av[j], _NN) for j in heads]
    g = [jnp.where(eye_n, wl[j], 0.0) + _mm(bh[j], ap[j], _TN) for j in heads]
    ht = [ht[j] + _mm(vp[j], bh[j], _TN) for j in heads]
    rp = [rt[j] + _mm(r_b[j], ap[j], _NN) for j in heads]
    yp = [yp[j] + _mm(r_b[j], vp[j], _NN) for j in heads]
    s_prev = [state_ref[j] for j in heads]
    y = [_mm(rp[j], s_prev[j], _NT) + yp[j] for j in heads]
    for j in heads:
        state_ref[j] = _mm(s_prev[j], g[j], _NT) + ht[j]
    r_raw, k_raw = per_head(r_in), per_head(k_in)
    rk, gnw, gnb = per_head(rk_ref[...]), per_head(gnw_ref[...]), per_head(gnb_ref[...])
    out = []
    for j in heads:
        mean = jnp.mean(y[j], axis=-1, keepdims=True)
        cen = y[j] - mean
        var = jnp.mean(cen * cen, axis=-1, keepdims=True)
        bonus = jnp.sum(r_raw[j] * k_raw[j] * rk[j], axis=-1, keepdims=True) * v[j]
        out.append(cen * lax.rsqrt(var + GN_EPS) * gnw[j] + gnb[j] + bonus)
    y_ref[0] = jnp.concatenate(out, axis=-1)

    @pl.when(c == pl.num_programs(2) - 1)
    def _():
        sout_ref[0] = state_ref[...]


def _wkv7_pair_kernel(r_ref, lw_ref, k_ref, v_ref, kk_ref, a_ref, rk_ref, gnw_ref, gnb_ref, s0_ref,
                      y_ref, sout_ref, state_ref):
    c = pl.program_id(2)
    L = r_ref.shape[1]
    n_pairs = state_ref.shape[0]
    N = RWKV_HEAD
    W = 2 * N

    def block_diag(top, bottom):
        z = jnp.zeros((N, N), jnp.float32)
        return jnp.concatenate([jnp.concatenate([top, z], axis=1), jnp.concatenate([z, bottom], axis=1)], axis=0)

    @pl.when(c == 0)
    def _():
        for p in range(n_pairs):
            state_ref[p] = block_diag(s0_ref[0, 2 * p], s0_ref[0, 2 * p + 1])

    row2 = lax.broadcasted_iota(jnp.int32, (2 * L, W), 0)
    lane2 = lax.broadcasted_iota(jnp.int32, (2 * L, W), 1)
    own_lanes = (row2 // L) == (lane2 // N)
    rr = lax.broadcasted_iota(jnp.int32, (2 * L, 2 * L), 0)
    cc = lax.broadcasted_iota(jnp.int32, (2 * L, 2 * L), 1)
    same = (rr // L) == (cc // L)
    strict = same & (rr % L > cc % L)
    incl = same & (rr % L >= cc % L)
    wr = lax.broadcasted_iota(jnp.int32, (W, W), 0)
    wc = lax.broadcasted_iota(jnp.int32, (W, W), 1)
    eye_w = wr == wc
    ones_bd = ((wr // N) == (wc // N)).astype(jnp.bfloat16)
    tl = lax.broadcasted_iota(jnp.int32, (L, L), 0) >= lax.broadcasted_iota(jnp.int32, (L, L), 1)
    tri = tl.astype(jnp.bfloat16)

    def head_sum(x):
        hi, lo = _split_bf16(x)
        return (jnp.dot(hi, ones_bd, preferred_element_type=jnp.float32)
                + jnp.dot(lo, ones_bd, preferred_element_type=jnp.float32))

    def stack(x):
        return jnp.where(own_lanes, jnp.concatenate([x, x], axis=0), 0.0)

    unstack = lambda x: x[:L] + x[L:]

    lw = lw_ref[0]
    lw_hi, lw_lo = _split_bf16(lw)
    cum = (lax.dot_general(tri, lw_hi, _NN, preferred_element_type=jnp.float32)
           + lax.dot_general(tri, lw_lo, _NN, preferred_element_type=jnp.float32))
    cum_last = cum[L - 1:L, :]
    e_neg_all = jnp.exp(-cum)
    e_tail_all = jnp.exp(cum_last - cum)
    e_prev_all = jnp.exp(cum - lw)
    e_cum_all = jnp.exp(cum)
    wl_all = jnp.exp(cum_last)

    pairs = range(n_pairs)
    slab = lambda t, p: t[:, p * W:(p + 1) * W]
    r_in = [slab(r_ref[0], p) for p in pairs]
    k_in = [slab(k_ref[0], p) for p in pairs]
    v_in = [slab(v_ref[0], p) for p in pairs]
    ssq = [head_sum(slab(kk_ref[0], p) * slab(kk_ref[0], p)) for p in pairs]
    at_st, bt_st, bh_st, rt_st, kt_st, kh_st, v_st = [], [], [], [], [], [], []
    for p in pairs:
        kk = slab(kk_ref[0], p) / jnp.maximum(jnp.sqrt(ssq[p]), 1e-12)
        b = kk * slab(a_ref[0], p)
        at_st.append(stack(-kk * slab(e_prev_all, p)))
        bt_st.append(stack(b * slab(e_neg_all, p)))
        bh_st.append(stack(b * slab(e_tail_all, p)))
        rt_st.append(stack(r_in[p] * slab(e_cum_all, p)))
        kt_st.append(stack(k_in[p] * slab(e_neg_all, p)))
        kh_st.append(stack(k_in[p] * slab(e_tail_all, p)))
        v_st.append(stack(v_in[p]))
    a_ab = [jnp.where(strict, _mm(at_st[p], bt_st[p], _NT), 0.0) for p in pairs]
    a_ak = [jnp.where(strict, _mm(at_st[p], kt_st[p], _NT), 0.0) for p in pairs]
    r_b = [jnp.where(incl, _mm(rt_st[p], bt_st[p], _NT), 0.0) for p in pairs]
    r_k = [jnp.where(incl, _mm(rt_st[p], kt_st[p], _NT), 0.0) for p in pairs]
    av = [_mm(a_ak[p], v_st[p], _NN) for p in pairs]
    ht = [_mm(v_st[p], kh_st[p], _TN) for p in pairs]
    yp = [_mm(r_k[p], v_st[p], _NN) for p in pairs]
    pw = a_ab
    inv_a = a_ab
    n = 1
    while 2 * n < L:
        pw = [_mm(pw[p], pw[p], _NN) for p in pairs]
        inv_a = [inv_a[p] + pw[p] + _mm(inv_a[p], pw[p], _NN) for p in pairs]
        n *= 2
    ap = [at_st[p] + _mm(inv_a[p], at_st[p], _NN) for p in pairs]
    vp = [av[p] + _mm(inv_a[p], av[p], _NN) for p in pairs]
    g = [jnp.where(eye_w, slab(wl_all, p), 0.0) + _mm(bh_st[p], ap[p], _TN) for p in pairs]
    ht = [ht[p] + _mm(vp[p], bh_st[p], _TN) for p in pairs]
    rp = [unstack(rt_st[p] + _mm(r_b[p], ap[p], _NN)) for p in pairs]
    yp = [unstack(yp[p] + _mm(r_b[p], vp[p], _NN)) for p in pairs]
    s_prev = [state_ref[p] for p in pairs]
    y = [_mm(rp[p], s_prev[p], _NT) + yp[p] for p in pairs]
    for p in pairs:
        state_ref[p] = _mm(s_prev[p], g[p], _NT) + ht[p]
    mean = [head_sum(y[p]) * (1.0 / N) for p in pairs]
    cen = [y[p] - mean[p] for p in pairs]
    var = [head_sum(cen[p] * cen[p]) * (1.0 / N) for p in pairs]
    bonus = [head_sum(r_in[p] * k_in[p] * slab(rk_ref[...], p)) * v_in[p] for p in pairs]
    y_ref[0] = jnp.concatenate(
        [cen[p] * lax.rsqrt(var[p] + GN_EPS) * slab(gnw_ref[...], p) + slab(gnb_ref[...], p) + bonus[p]
         for p in pairs], axis=-1)

    @pl.when(c == pl.num_programs(2) - 1)
    def _():
        for p in pairs:
            s = state_ref[p]
            sout_ref[0, 2 * p] = s[:N, :N]
            sout_ref[0, 2 * p + 1] = s[N:, N:]


def wkv7_chunked(r, lw, k, v, kk, a_gate, r_k, gn_w, gn_b, s0):
    n, T, D = r.shape
    L = WKV_CHUNK
    hb = WKV_HEADS_PER_STEP
    w = hb * RWKV_HEAD
    seq = pl.BlockSpec((1, L, w), lambda b, h, c: (b, c, h))
    vec = pl.BlockSpec((1, w), lambda b, h, c: (0, h))
    st = pl.BlockSpec((1, hb, RWKV_HEAD, RWKV_HEAD), lambda b, h, c: (b, h, 0, 0))
    row = lambda t: t.reshape(1, D).astype(jnp.float32)
    return pl.pallas_call(
        _wkv7_pair_kernel,
        grid=(n, D // w, T // L),
        in_specs=[seq] * 6 + [vec] * 3 + [st],
        out_specs=[seq, st],
        out_shape=[jax.ShapeDtypeStruct((n, T, D), jnp.float32),
                   jax.ShapeDtypeStruct(s0.shape, jnp.float32)],
        scratch_shapes=[pltpu.VMEM((hb // 2, 2 * RWKV_HEAD, 2 * RWKV_HEAD), jnp.float32)],
        compiler_params=pltpu.CompilerParams(dimension_semantics=("parallel", "parallel", "arbitrary")),
        name="wkv7_chunked",
    )(r, lw, k, v, kk, a_gate, row(r_k), row(gn_w), row(gn_b), s0)


def wkv7_scan(r, lw, k, v, a_vec, b_vec, s0):
    def step(S, inp):
        r_t, lw_t, k_t, v_t, a_t, b_t = inp
        sa = jnp.sum(S * a_t[:, :, None, :], axis=-1)
        S = S * jnp.exp(lw_t)[:, :, None, :] + sa[..., None] * b_t[:, :, None, :] + v_t[..., None] * k_t[:, :, None, :]
        return S, jnp.sum(S * r_t[:, :, None, :], axis=-1)

    xs = tuple(jnp.moveaxis(t, 1, 0) for t in (r, lw, k, v, a_vec, b_vec))
    s_fin, ys = lax.scan(step, s0, xs)
    return jnp.moveaxis(ys, 0, 1), s_fin


ROW_TILE = 256


def _bdot(x, w):
    return jnp.dot(x.astype(jnp.bfloat16), w, preferred_element_type=jnp.float32)


def _rms(x):
    return x * lax.rsqrt(jnp.mean(x * x, axis=-1, keepdims=True) + NORM_EPS)


def _rwkv_pre_kernel(x_ref, shift_ref, gain_ref, mu_ref, vec_ref, wrkv_ref, w1_ref, a1_ref, g1_ref,
                     w2_ref, a2_ref, g2_ref,
                     r_ref, lw_ref, k_ref, v_ref, kk_ref, a_ref, g_ref, hlast_ref, prev_ref):
    @pl.when(pl.program_id(1) == 0)
    def _():
        prev_ref[...] = shift_ref[0]

    tm = x_ref.shape[1]
    h = _rms(x_ref[0]) * gain_ref[...]
    first_row = lax.broadcasted_iota(jnp.int32, h.shape, 0) == 0
    h_prev = jnp.where(first_row, prev_ref[...], pltpu.roll(h, 1, 0))
    prev_ref[...] = h[tm - 1:tm, :]
    hlast_ref[0] = h[tm - 1:tm, :]
    dx = h_prev - h
    mix = lambda i: (h + dx * mu_ref[i:i + 1, :]).astype(jnp.bfloat16)
    w0, a0, k_k, k_a = (vec_ref[i:i + 1, :] for i in range(4))
    r_ref[0] = _bdot(mix(0), wrkv_ref[0])
    k = _bdot(mix(2), wrkv_ref[1])
    v_ref[0] = _bdot(mix(3), wrkv_ref[2])
    lw_ref[0] = -DECAY_SCALE * jax.nn.sigmoid(w0 + _bdot(jnp.tanh(_bdot(mix(1), w1_ref[...])), w2_ref[...]))
    a = jax.nn.sigmoid(a0 + _bdot(_bdot(mix(4), a1_ref[...]), a2_ref[...]))
    g_ref[0] = _bdot(jax.nn.sigmoid(_bdot(mix(5), g1_ref[...])), g2_ref[...])
    a_ref[0] = a
    kk_ref[0] = k * k_k
    k_ref[0] = k * (1.0 + (a - 1.0) * k_a)


def _out_proj_kernel(*refs, n_terms, gated, routed):
    terms = refs[:n_terms]
    rest = refs[n_terms:]
    if gated:
        gate_ref, rest = rest[0], rest[1:]
    x_ref, w_ref, gain_ref = rest[:3]
    rest = rest[3:]
    if routed:
        wr_ref, rest = rest[0], rest[1:]
    xo_ref, ho_ref = rest[:2]
    y = terms[0][...]
    for t in terms[1:]:
        y = y + t[...]
    if gated:
        y = y * gate_ref[...]
    xo = x_ref[...] + _bdot(y, w_ref[...])
    xo_ref[...] = xo
    h = _rms(xo) * gain_ref[...]
    ho_ref[...] = h.astype(ho_ref.dtype)
    if routed:
        h_hi, h_lo = _split_bf16(h)
        w_hi, w_lo = _split_bf16(wr_ref[...])
        dot = functools.partial(jnp.dot, preferred_element_type=jnp.float32)
        rest[2][...] = dot(h_hi, w_hi) + (dot(h_hi, w_lo) + dot(h_lo, w_hi))


def out_proj(terms, gate, x, w, gain, w_router=None):
    m, d = x.shape
    kdim = w.shape[0]
    tm = min(ROW_TILE, m)
    rows = lambda width: pl.BlockSpec((tm, width), lambda i: (i, 0))
    full = lambda a: pl.BlockSpec(a.shape, lambda i: (0, 0))
    ins = list(terms) + ([gate] if gate is not None else [])
    consts = [w.astype(jnp.bfloat16), gain.reshape(1, d)]
    out_specs = [rows(d), rows(d)]
    out_shape = [jax.ShapeDtypeStruct((m, d), jnp.float32), jax.ShapeDtypeStruct((m, d), jnp.bfloat16)]
    if w_router is not None:
        consts.append(jnp.pad(w_router, ((0, 0), (0, 128 - w_router.shape[1]))))
        out_specs.append(rows(128))
        out_shape.append(jax.ShapeDtypeStruct((m, 128), jnp.float32))
    return pl.pallas_call(
        functools.partial(_out_proj_kernel, n_terms=len(terms), gated=gate is not None,
                          routed=w_router is not None),
        grid=(m // tm,),
        in_specs=[rows(kdim)] * len(ins) + [rows(d)] + [full(a) for a in consts],
        out_specs=out_specs,
        out_shape=out_shape,
        compiler_params=pltpu.CompilerParams(dimension_semantics=("parallel",),
                                             vmem_limit_bytes=48 * 1024 * 1024),
        name="out_proj",
    )(*ins, x, *consts)


def rwkv7_layer(x, shift0, s0, gain, mu, w_rkv, w0, w1, w2, a0, a1, a2, g1, g2, k_k, k_a, r_k, gn_w, gn_b, w_o,
                gain_next):
    n, T, D = x.shape
    tm = min(ROW_TILE, T)
    bf = lambda t: t.astype(jnp.bfloat16)
    seq = pl.BlockSpec((1, tm, D), lambda b, t: (b, t, 0))
    full = lambda a: pl.BlockSpec(a.shape, lambda b, t: (0,) * a.ndim)
    per_seq = pl.BlockSpec((1, 1, D), lambda b, t: (b, 0, 0))
    vecs = jnp.stack([w0, a0, k_k, k_a]).astype(jnp.float32)
    weights = [bf(w_rkv), bf(w1), bf(a1), bf(g1), bf(w2), bf(a2), bf(g2)]
    small = [gain.reshape(1, D), mu, vecs]
    outs = pl.pallas_call(
        _rwkv_pre_kernel,
        grid=(n, T // tm),
        in_specs=[seq, per_seq] + [full(a) for a in small + weights],
        out_specs=[seq] * 7 + [per_seq],
        out_shape=[jax.ShapeDtypeStruct((n, T, D), jnp.float32)] * 7
                  + [jax.ShapeDtypeStruct((n, 1, D), jnp.float32)],
        scratch_shapes=[pltpu.VMEM((1, D), jnp.float32)],
        compiler_params=pltpu.CompilerParams(dimension_semantics=("parallel", "arbitrary"),
                                             vmem_limit_bytes=56 * 1024 * 1024),
        name="rwkv_pre",
    )(x, shift0.reshape(n, 1, D), *small, *weights)
    r, lw, k, v, kk, a_gate, g, h_last = outs
    y, s_fin = wkv7_chunked(r, lw, k, v, kk, a_gate, r_k.reshape(-1), gn_w, gn_b, s0.astype(jnp.float32))
    flat = lambda t: t.reshape(n * T, D)
    x1, h1 = out_proj([flat(y)], flat(g), flat(x), w_o, gain_next)
    return x1, h1, s_fin, h_last.reshape(n, D)


def rwkv7_time_mix(h, h_prev, s0, mu, w_rkv, w0, w1, w2, a0, a1, a2, g1, g2, k_k, k_a, r_k, gn_w, gn_b, w_o,
                   precise=False):
    n, T, D = h.shape
    f32 = jnp.float32
    mm = functools.partial(dense, precise=precise)
    dx = jnp.concatenate([h_prev[:, None, :].astype(h.dtype), h[:, :-1]], axis=1) - h
    xr, xw, xk, xv, xa, xg = (h + dx * mu[i] for i in range(6))
    r = mm(xr, w_rkv[0])
    k = mm(xk, w_rkv[1])
    v = mm(xv, w_rkv[2])
    log_decay = -DECAY_SCALE * jax.nn.sigmoid((w0 + mm(jnp.tanh(mm(xw, w1)), w2)).astype(f32))
    a = jax.nn.sigmoid((a0 + mm(mm(xa, a1), a2)).astype(f32))
    g = mm(jax.nn.sigmoid(mm(xg, g1)), g2)
    heads = lambda t: t.astype(f32).reshape(n, T, RWKV_HEADS, RWKV_HEAD)
    kk = heads(k * k_k)
    kk = kk / jnp.maximum(jnp.sqrt(jnp.sum(kk * kk, axis=-1, keepdims=True)), 1e-12)
    a_h = heads(a)
    k_h = heads(k.astype(f32) * (1.0 + (a - 1.0) * k_a.astype(f32)))
    r_h, v_h = heads(r), heads(v)
    y, s_fin = wkv7_scan(r_h, heads(log_decay), k_h, v_h, -kk, kk * a_h, s0.astype(f32))
    mean = jnp.mean(y, axis=-1, keepdims=True)
    var = jnp.mean(jnp.square(y - mean), axis=-1, keepdims=True)
    y = ((y - mean) * lax.rsqrt(var + GN_EPS)).reshape(n, T, D) * gn_w.astype(f32) + gn_b.astype(f32)
    bonus = jnp.sum(r_h * k_h * r_k.astype(f32), axis=-1, keepdims=True) * v_h
    y = (y + bonus.reshape(n, T, D)).astype(h.dtype)
    return mm(y * g, w_o), s_fin, h[:, -1]


def shared_kv_rows(x, norm_kv, w_kv, pos, precise=False):
    n, T, _ = x.shape
    kv = dense(rmsnorm(x, norm_kv), w_kv, precise).reshape(n, T, N_BRANCH, 2, NSA_KV_GROUPS, NSA_HEAD_DIM)
    cmp_kv = kv[:, :, 0]
    slc_kv = jnp.stack([rope_partial(kv[:, :, 1, 0], pos), kv[:, :, 1, 1]], axis=2)
    win_kv = jnp.stack([rope_partial(kv[:, :, 2, 0], pos), kv[:, :, 2, 1]], axis=2)
    return cmp_kv, slc_kv, win_kv


CHUNK_LANES = CMP_STRIDE * 2 * NSA_KV_GROUPS * NSA_HEAD_DIM
HIDDEN_LANES = 2 * NSA_KV_GROUPS * 2 * CMP_HIDDEN
PAGES_PER_STEP = 32
CHUNKS_PER_PAGE = PAGE_SIZE // CMP_STRIDE


def _chunk_weights(cmp_w1):
    w = cmp_w1.reshape(2, 2, CMP_STRIDE, NSA_HEAD_DIM, CMP_HIDDEN)
    w = w.transpose(2, 0, 3, 1, 4)
    eye = jnp.eye(2, dtype=w.dtype)
    big = jnp.einsum('ab,cf,jaehk->jacebfhk', eye, jnp.eye(NSA_KV_GROUPS, dtype=w.dtype), w)
    return big.reshape(CHUNK_LANES, HIDDEN_LANES).astype(jnp.bfloat16)


def _token_weights(cmp_w1):
    w = cmp_w1.reshape(2, 2, CMP_STRIDE, NSA_HEAD_DIM, CMP_HIDDEN)
    return w.transpose(2, 0, 3, 1, 4).reshape(CMP_STRIDE, 2, NSA_HEAD_DIM, 2 * CMP_HIDDEN).astype(jnp.bfloat16)


def _paged_chunk_kernel(pt_ref, *refs):
    pages, w_ref, o_ref, xt_ref = (refs[:PAGES_PER_STEP], refs[PAGES_PER_STEP], refs[PAGES_PER_STEP + 1],
                                   refs[PAGES_PER_STEP + 2])
    planes = [(kv, g) for kv in range(2) for g in range(NSA_KV_GROUPS)]
    tok = lax.broadcasted_iota(jnp.int32, (PAGE_SIZE, PAGE_SIZE), 0)
    dst = lax.broadcasted_iota(jnp.int32, (PAGE_SIZE, PAGE_SIZE), 1)
    regroup = (tok == (dst % CHUNKS_PER_PAGE) * CMP_STRIDE + dst // CHUNKS_PER_PAGE).astype(jnp.bfloat16)
    for p in range(PAGES_PER_STEP):
        for i, (kv, g) in enumerate(planes):
            xt_ref[p, i] = _bdot(pages[p][0, kv, g], regroup).T
    width = 2 * CMP_HIDDEN
    for i, (kv, g) in enumerate(planes):
        acc = None
        for j in range(CMP_STRIDE):
            rows = [xt_ref[p, i, j * CHUNKS_PER_PAGE:(j + 1) * CHUNKS_PER_PAGE, :] for p in range(PAGES_PER_STEP)]
            y = _bdot(jnp.concatenate(rows, axis=0), w_ref[j, kv])
            acc = y if acc is None else acc + y
        o_ref[0, :, i * width:(i + 1) * width] = acc


def paged_chunk_hidden(cache, page_table, w_token):
    n, n_pages = page_table.shape
    planes = cache.transpose(0, 2, 3, 4, 1)
    rows = PAGES_PER_STEP * CHUNKS_PER_PAGE

    def page_spec(k):
        return pl.BlockSpec((1,) + planes.shape[1:], lambda b, s, pt: (pt[b, s * PAGES_PER_STEP + k], 0, 0, 0, 0))

    return pl.pallas_call(
        _paged_chunk_kernel,
        grid_spec=pltpu.PrefetchScalarGridSpec(
            num_scalar_prefetch=1,
            grid=(n, n_pages // PAGES_PER_STEP),
            in_specs=[page_spec(k) for k in range(PAGES_PER_STEP)]
                     + [pl.BlockSpec(w_token.shape, lambda b, s, pt: (0, 0, 0, 0))],
            out_specs=pl.BlockSpec((1, rows, HIDDEN_LANES), lambda b, s, pt: (b, s, 0)),
            scratch_shapes=[pltpu.VMEM((PAGES_PER_STEP, 2 * NSA_KV_GROUPS, PAGE_SIZE, NSA_HEAD_DIM), jnp.float32)],
        ),
        out_shape=jax.ShapeDtypeStruct((n, n_pages * CHUNKS_PER_PAGE, HIDDEN_LANES), jnp.float32),
        compiler_params=pltpu.CompilerParams(dimension_semantics=("parallel", "arbitrary"),
                                             vmem_limit_bytes=48 * 1024 * 1024),
        name="paged_chunk_hidden",
    )(page_table.astype(jnp.int32), *([planes] * PAGES_PER_STEP), w_token)


def _matmul_kernel(x_ref, w_ref, o_ref):
    o_ref[...] = jnp.dot(x_ref[...].astype(jnp.bfloat16), w_ref[...],
                         preferred_element_type=jnp.float32).astype(o_ref.dtype)


def matmul(x, w, rows_per_step=512, out_dtype=jnp.float32):
    m, k = x.shape
    nn = w.shape[1]
    tm = min(rows_per_step, m)
    return pl.pallas_call(
        _matmul_kernel,
        grid=(m // tm,),
        in_specs=[pl.BlockSpec((tm, k), lambda i: (i, 0)), pl.BlockSpec((k, nn), lambda i: (0, 0))],
        out_specs=pl.BlockSpec((tm, nn), lambda i: (i, 0)),
        out_shape=jax.ShapeDtypeStruct((m, nn), out_dtype),
        compiler_params=pltpu.CompilerParams(dimension_semantics=("parallel",),
                                             vmem_limit_bytes=48 * 1024 * 1024),
        name="matmul",
    )(x, w.astype(jnp.bfloat16))


def _split_bf16(x):
    hi = x.astype(jnp.bfloat16)
    return hi, (x - hi.astype(jnp.float32)).astype(jnp.bfloat16)


def _matmul3_kernel(x_ref, w_ref, o_ref):
    x_hi, x_lo = _split_bf16(x_ref[...])
    w_hi, w_lo = _split_bf16(w_ref[...])
    dot = functools.partial(jnp.dot, preferred_element_type=jnp.float32)
    o_ref[...] = dot(x_hi, w_hi) + (dot(x_hi, w_lo) + dot(x_lo, w_hi))


def matmul_precise(x, w, cols_per_step=512):
    m, k = x.shape
    nn = w.shape[1]
    pad = (-nn) % 128
    if pad:
        w = jnp.pad(w, ((0, 0), (0, pad)))
    tn = math.gcd(cols_per_step, nn + pad)
    tm = min(m, 512)
    out = pl.pallas_call(
        _matmul3_kernel,
        grid=(m // tm, (nn + pad) // tn),
        in_specs=[pl.BlockSpec((tm, k), lambda i, j: (i, 0)), pl.BlockSpec((k, tn), lambda i, j: (0, j))],
        out_specs=pl.BlockSpec((tm, tn), lambda i, j: (i, j)),
        out_shape=jax.ShapeDtypeStruct((m, nn + pad), jnp.float32),
        compiler_params=pltpu.CompilerParams(dimension_semantics=("parallel", "parallel")),
        name="matmul_precise",
    )(x, w)
    return out[:, :nn] if pad else out


def dense(x, w, precise=False):
    if precise:
        return matmul_precise(x.reshape(-1, x.shape[-1]), w).reshape(x.shape[:-1] + (w.shape[1],))
    return x @ w


def compress_from_hidden(hidden, pos_emb, w1, b1, w2):
    n, C = hidden.shape[:2]
    rows = -(-C // 8) * 8
    if rows != C:
        hidden = jnp.pad(hidden, ((0, 0), (0, rows - C), (0, 0)))
    w1r = w1.reshape(2, CMP_BLOCK, NSA_HEAD_DIM, CMP_HIDDEN)
    bias = jnp.einsum('ajd,ajdh->ah', pos_emb, w1r) + b1
    out = pl.pallas_call(
        functools.partial(_compress_finish_kernel, n_blocks=C),
        grid=(n,),
        in_specs=[pl.BlockSpec((1, rows, HIDDEN_LANES), lambda b: (b, 0, 0)),
                  pl.BlockSpec(bias.shape, lambda b: (0, 0)), pl.BlockSpec(w2.shape, lambda b: (0, 0, 0))],
        out_specs=pl.BlockSpec((1, rows, 2 * SLAB), lambda b: (b, 0, 0)),
        out_shape=jax.ShapeDtypeStruct((n, rows, 2 * SLAB), jnp.float32),
        compiler_params=pltpu.CompilerParams(dimension_semantics=("parallel",)),
        name="compress_finish",
    )(hidden, bias, w2.astype(jnp.bfloat16))[:, :C]
    shape = (n, C, NSA_KV_GROUPS, NSA_HEAD_DIM)
    return out[:, :, :SLAB].reshape(shape), out[:, :, SLAB:].reshape(shape)


def _compress_finish_kernel(h_ref, bias_ref, w2_ref, o_ref, *, n_blocks):
    h = h_ref[0]
    rows = h.shape[0]
    keep = lax.broadcasted_iota(jnp.int32, (rows, CMP_HIDDEN), 0) < n_blocks - 1
    outs = []
    for i in range(2 * NSA_KV_GROUPS):
        kv = i // NSA_KV_GROUPS
        first = h[:, 2 * i * CMP_HIDDEN:(2 * i + 1) * CMP_HIDDEN]
        second = h[:, (2 * i + 1) * CMP_HIDDEN:(2 * i + 2) * CMP_HIDDEN]
        nxt = jnp.where(keep, pltpu.roll(second, rows - 1, 0), 0.0)
        act = jax.nn.gelu(first + nxt + bias_ref[kv:kv + 1, :])
        outs.append(_bdot(act, w2_ref[kv]))
    o_ref[0] = jnp.concatenate(outs, axis=-1)


def rows_chunk_hidden(kv_rows, w_chunk):
    n, T = kv_rows.shape[:2]
    chunks = kv_rows.reshape(n * T // CMP_STRIDE, CHUNK_LANES)
    return matmul(chunks, w_chunk, rows_per_step=256).reshape(n, T // CMP_STRIDE, HIDDEN_LANES)


def nsa_query(h, w_in, precise=False):
    n, T, _ = h.shape
    proj = dense(h, w_in, precise)
    q = proj[..., :NSA_HEADS * NSA_HEAD_DIM].reshape(n, T, NSA_KV_GROUPS, NSA_GROUP_HEADS, NSA_HEAD_DIM)
    gate = proj[..., NSA_HEADS * NSA_HEAD_DIM:].reshape(n, T, NSA_KV_GROUPS, NSA_GROUP_HEADS, N_BRANCH)
    return q, gate


def compressed_branch_and_selection(q, t_pos, kc, vc):
    scale = NSA_HEAD_DIM ** -0.5
    n_c = kc.shape[1]
    vis = (jnp.arange(n_c) * CMP_STRIDE + CMP_BLOCK - 1)[None, :] <= t_pos[:, None]
    p_cmp = masked_softmax(jnp.einsum('nqghd,ncgd->nghqc', q, kc) * scale, vis)
    o_cmp = jnp.einsum('nghqc,ncgd->nqghd', p_cmp.astype(vc.dtype), vc)
    ratio = SLC_BLOCK // CMP_STRIDE
    lead = CMP_BLOCK // CMP_STRIDE - 1
    n_s = n_c // ratio
    pg = jnp.pad(jnp.sum(p_cmp, axis=2), ((0, 0), (0, 0), (0, 0), (lead, 0)))
    p_slc = pg[..., 0:ratio * n_s:ratio]
    for o in range(1, ratio + lead):
        p_slc = p_slc + pg[..., o:o + ratio * n_s:ratio]
    jb = jnp.arange(n_s)[None, :]
    jt = (t_pos // SLC_BLOCK)[:, None]
    forced = (jb == 0) | (jb == jt) | (jb == jt - 1)
    score = jnp.where(jb > jt, -jnp.inf, jnp.where(forced, FORCED_SCORE, p_slc))
    before = (score[..., :, None] > score[..., None, :]) | (
        (score[..., :, None] == score[..., None, :]) & (jb[0][:, None] < jb[0][None, :]))
    rank = jnp.sum(before, axis=-2)
    slots = jnp.arange(min(SLC_TOPN, n_s))
    idx = jnp.sum(jnp.where(rank[..., None, :] == slots[:, None], jb[0], 0), axis=-1)
    return o_cmp, idx


def _dot3(x, y, dims):
    x_hi, x_lo = _split_bf16(x)
    y_hi, y_lo = _split_bf16(y)
    dot = lambda a, b: lax.dot_general(a, b, dims, preferred_element_type=jnp.float32)
    return dot(x_hi, y_hi) + (dot(x_hi, y_lo) + dot(x_lo, y_hi))


def _decode_kernel(half_ref, new_ref, phys_ref, q_ref, rows_ref, win_ref, *refs):
    b = pl.program_id(0)
    n_sel = len(refs) - 1
    pages, o_ref = refs[:n_sel], refs[n_sel]
    k_per_group = n_sel // NSA_KV_GROUPS
    hg, dh = NSA_GROUP_HEADS, NSA_HEAD_DIM
    tok_half = lax.broadcasted_iota(jnp.int32, (hg, PAGE_SIZE), 1) // SLC_BLOCK
    rows = rows_ref[0]

    def attend(qg, scores, values, k_new, v_new, new_ok):
        s_new = jnp.where(new_ok, jnp.sum(qg * k_new, axis=-1, keepdims=True), MASKED)
        m = s_new
        for s in scores:
            m = jnp.maximum(m, jnp.max(s, axis=-1, keepdims=True))
        e_new = jnp.exp(s_new - m)
        total, out = e_new, e_new * v_new
        for s, v in zip(scores, values):
            e = jnp.exp(s - m)
            total = total + jnp.sum(e, axis=-1, keepdims=True)
            out = out + _dot3(e, v, _NT)
        return out / jnp.maximum(total, 1e-30)

    outs = []
    for g in range(NSA_KV_GROUPS):
        qg = q_ref[0, g * hg:(g + 1) * hg, :]
        lanes = slice(g * dh, (g + 1) * dh)
        scores, values = [], []
        any_new = jnp.int32(0)
        for k in range(k_per_group):
            j = g * k_per_group + k
            plane = pages[j][0]
            is_new = new_ref[b, j]
            ok = (tok_half == half_ref[b, j]) & (is_new == 0)
            scores.append(jnp.where(ok, _dot3(qg, plane[0, 0], _NN), MASKED))
            values.append(plane[1, 0])
            any_new = jnp.maximum(any_new, is_new)
        o_slc = attend(qg, scores, values, rows[0:1, lanes], rows[1:2, lanes], any_new > 0)
        o_win = attend(qg, [_dot3(qg, win_ref[0, 0, g], _NN)], [win_ref[0, 1, g]],
                       rows[2:3, lanes], rows[3:4, lanes], True)
        outs.append(jnp.concatenate([o_slc, o_win], axis=-1))
    o_ref[0] = jnp.concatenate(outs, axis=0)


def decode_attention(q_rot, idx, page_table, cache_slc_kv, cache_win_kv, slc_new, win_new):
    n = q_rot.shape[0]
    sub = PAGE_SIZE // SLC_BLOCK
    n_past_blk = PAST_LEN // SLC_BLOCK
    assert cache_win_kv.shape[1] <= WINDOW and cache_win_kv.shape[1] <= PAST_LEN
    k_sel = idx.shape[-1]
    flat = idx.reshape(n, NSA_KV_GROUPS * k_sel)
    past = jnp.minimum(flat, n_past_blk - 1)
    hit = (past // sub)[:, :, None] == jnp.arange(page_table.shape[1])[None, None, :]
    phys = jnp.sum(jnp.where(hit, page_table[:, None, :], 0), axis=-1).astype(jnp.int32)
    half = (past % sub).astype(jnp.int32)
    is_new = (flat >= n_past_blk).astype(jnp.int32)
    pool = cache_slc_kv.transpose(0, 2, 3, 4, 1)
    window = cache_win_kv.transpose(0, 2, 3, 4, 1)
    rows = jnp.concatenate([slc_new.reshape(n, 2, SLAB), win_new.reshape(n, 2, SLAB)], axis=1)

    def page_spec(j):
        g = j // k_sel
        return pl.BlockSpec((1, 2, 1, NSA_HEAD_DIM, PAGE_SIZE), lambda b, hf, nw, ph: (ph[b, j], 0, g, 0, 0))

    out = pl.pallas_call(
        _decode_kernel,
        grid_spec=pltpu.PrefetchScalarGridSpec(
            num_scalar_prefetch=3,
            grid=(n,),
            in_specs=[pl.BlockSpec((1,) + q_rot.shape[1:], lambda b, hf, nw, ph: (b, 0, 0)),
                      pl.BlockSpec((1,) + rows.shape[1:], lambda b, hf, nw, ph: (b, 0, 0)),
                      pl.BlockSpec((1,) + window.shape[1:], lambda b, hf, nw, ph: (b, 0, 0, 0, 0))]
                     + [page_spec(j) for j in range(NSA_KV_GROUPS * k_sel)],
            out_specs=pl.BlockSpec((1, NSA_HEADS, 2 * NSA_HEAD_DIM), lambda b, hf, nw, ph: (b, 0, 0)),
        ),
        out_shape=jax.ShapeDtypeStruct((n, NSA_HEADS, 2 * NSA_HEAD_DIM), jnp.float32),
        compiler_params=pltpu.CompilerParams(dimension_semantics=("parallel",)),
        name="decode_attention",
    )(half, is_new, phys, q_rot, rows, window, *([pool] * (NSA_KV_GROUPS * k_sel)))
    return out[:, :, :NSA_HEAD_DIM], out[:, :, NSA_HEAD_DIM:]


NSA_SLABS = NSA_GROUP_HEADS
NSA_TILES = NSA_SLABS * NSA_KV_GROUPS
KEY_BLOCK = 128
SELECTED_KEY_BLOCK = 256
DENSE_BATCH = 16
MASKED = -1e30
SLC_PER_CMP = SLC_BLOCK // CMP_STRIDE
CMP_LEAD = CMP_BLOCK // CMP_STRIDE - 1
SLAB = NSA_KV_GROUPS * NSA_HEAD_DIM
SEL_LANES = 64


def _group_tiles(q_ref, qs_ref):
    tq = q_ref.shape[1]
    lane = lax.broadcasted_iota(jnp.int32, (tq, SLAB), 1)
    for i in range(NSA_SLABS):
        qs = q_ref[0, :, i * 128:(i + 1) * 128]
        for g in range(NSA_KV_GROUPS):
            in_group = (lane >= g * NSA_HEAD_DIM) & (lane < (g + 1) * NSA_HEAD_DIM)
            qs_ref[2 * i + g] = jnp.where(in_group, qs, jnp.zeros_like(qs)).astype(jnp.bfloat16)


def _merge_groups(o_ref, tiles, gate_ref, branch):
    tq = tiles[0].shape[0]
    lane = lax.broadcasted_iota(jnp.int32, (tq, SLAB), 1)
    gates = gate_ref[0]
    col = lambda g, i: (g * NSA_GROUP_HEADS + i) * N_BRANCH + branch
    for i in range(NSA_SLABS):
        lo = tiles[2 * i] * gates[:, col(0, i):col(0, i) + 1]
        hi = tiles[2 * i + 1] * gates[:, col(1, i):col(1, i) + 1]
        o_ref[0, :, i * 128:(i + 1) * 128] = jnp.where(lane < NSA_HEAD_DIM, lo, hi)


def _nsa_cmp_kernel(q_ref, kc_ref, vc_ref, gate_ref, o_ref, sel_ref, qs_ref):
    qi = pl.program_id(1)
    tq = q_ref.shape[1]
    nc = kc_ref.shape[1]
    ns = nc // SLC_PER_CMP
    t0 = qi * tq
    _group_tiles(q_ref, qs_ref)
    kc = kc_ref[0]
    vc = vc_ref[0]
    t_row = t0 + lax.broadcasted_iota(jnp.int32, (tq, nc), 0)
    c_pos = lax.broadcasted_iota(jnp.int32, (tq, nc), 1) * CMP_STRIDE + (CMP_BLOCK - 1)
    vis = c_pos <= t_row
    tiles = range(NSA_TILES)
    s = [_mm(qs_ref[r], kc, _NT) for r in tiles]
    p = []
    for r in tiles:
        sr = jnp.where(vis, s[r], MASKED)
        m = jnp.max(sr, axis=-1, keepdims=True)
        e = jnp.where(vis, jnp.exp(sr - m), 0.0)
        p.append(e / jnp.maximum(jnp.sum(e, axis=-1, keepdims=True), 1e-30))
    _merge_groups(o_ref, [_mm(p[r], vc, _NN) for r in tiles], gate_ref, 0)

    jrow = lax.broadcasted_iota(jnp.int32, (ns, nc), 0)
    ccol = lax.broadcasted_iota(jnp.int32, (ns, nc), 1)
    pool = ((ccol >= SLC_PER_CMP * jrow - CMP_LEAD) & (ccol < SLC_PER_CMP * (jrow + 1))).astype(jnp.bfloat16)
    jb = lax.broadcasted_iota(jnp.int32, (ns, tq), 0)
    jt = (t0 + lax.broadcasted_iota(jnp.int32, (ns, tq), 1)) // SLC_BLOCK
    forced = (jb == 0) | (jb == jt) | (jb == jt - 1)
    sel_t = []
    for g in range(NSA_KV_GROUPS):
        pg = p[g]
        for i in range(1, NSA_SLABS):
            pg = pg + p[2 * i + g]
        pg_hi = pg.astype(jnp.bfloat16)
        pg_lo = (pg - pg_hi.astype(jnp.float32)).astype(jnp.bfloat16)
        p_slc = (lax.dot_general(pool, pg_hi, _NT, preferred_element_type=jnp.float32)
                 + lax.dot_general(pool, pg_lo, _NT, preferred_element_type=jnp.float32))
        score = jnp.where(jb > jt, -jnp.inf, jnp.where(forced, FORCED_SCORE, p_slc))
        rank = jnp.zeros((ns, tq), jnp.float32)
        for i in range(ns):
            row = score[i:i + 1, :]
            tie = jnp.where(jb > i, 1.0, 0.0)
            rank = rank + jnp.where(row > score, 1.0, 0.0) + jnp.where(row == score, tie, 0.0)
        sel_t.append(jnp.where(rank < min(SLC_TOPN, ns), 1.0, 0.0))
        if ns < SEL_LANES:
            sel_t.append(jnp.zeros((SEL_LANES - ns, tq), jnp.float32))
    sel_ref[0] = jnp.concatenate(sel_t, axis=0).T.astype(sel_ref.dtype)


def _nsa_dense_kernel(q_ref, k_ref, vt_ref, sel_ref, gate_ref, o_ref, qs_ref, m_ref, acc_ref, *, windowed, kblock):
    qi = pl.program_id(1)
    tq = q_ref.shape[1]
    t0 = qi * tq
    _group_tiles(q_ref, qs_ref)
    m_ref[...] = jnp.full(m_ref.shape, MASKED, jnp.float32)
    acc_ref[...] = jnp.zeros(acc_ref.shape, jnp.float32)
    k_row = lax.broadcasted_iota(jnp.int32, (kblock, tq), 0)
    t_lane = t0 + lax.broadcasted_iota(jnp.int32, (kblock, tq), 1)
    tiles = range(NSA_TILES)
    first = jnp.maximum(t0 - WINDOW, 0) // kblock if windowed else 0
    last = (t0 + tq - 1) // kblock

    def key_block(kb, carry):
        start = pl.multiple_of(kb * kblock, kblock)
        kblk = k_ref[0, pl.ds(start, kblock), :]
        vt = vt_ref[0, :, pl.ds(start, kblock)]
        k_pos = start + k_row
        if windowed:
            allowed = [(k_pos <= t_lane) & (k_pos >= t_lane - WINDOW)] * NSA_KV_GROUPS
        else:
            sel = sel_ref[0]
            n_sel = NSA_KV_GROUPS * SEL_LANES
            e_key = lax.broadcasted_iota(jnp.int32, (kblock, n_sel), 0)
            e_lane = lax.broadcasted_iota(jnp.int32, (kblock, n_sel), 1)
            blk = kb * (kblock // SLC_BLOCK) + e_key // SLC_BLOCK
            allowed = []
            for g in range(NSA_KV_GROUPS):
                expand = (e_lane == blk + g * SEL_LANES).astype(jnp.bfloat16)
                picked = lax.dot_general(expand, sel, _NT, preferred_element_type=jnp.float32)
                allowed.append((picked > 0.5) & (k_pos <= t_lane))
        bias = [jnp.where(ok, 0.0, MASKED) for ok in allowed]
        d_row = lax.broadcasted_iota(jnp.int32, vt.shape, 0)
        vt_g = [jnp.where((d_row >= g * NSA_HEAD_DIM) & (d_row < (g + 1) * NSA_HEAD_DIM), vt, jnp.ones_like(vt))
                for g in range(NSA_KV_GROUPS)]
        s = [_mm(kblk, qs_ref[r], _NT) for r in tiles]
        for lo in range(0, NSA_TILES, DENSE_BATCH):
            batch = range(lo, lo + DENSE_BATCH)
            p, alpha = {}, {}
            for r in batch:
                sr = s[r] + bias[r % NSA_KV_GROUPS]
                m_prev = m_ref[r]
                m_new = jnp.maximum(m_prev, jnp.max(sr, axis=0, keepdims=True))
                alpha[r] = jnp.exp(m_prev - m_new)
                p[r] = jnp.exp(sr - m_new).astype(jnp.bfloat16)
                m_ref[r] = m_new
            pv = {r: jnp.dot(vt_g[r % NSA_KV_GROUPS], p[r], preferred_element_type=jnp.float32) for r in batch}
            for r in batch:
                acc_ref[r] = alpha[r] * acc_ref[r] + pv[r]
        return carry

    lax.fori_loop(first, last + 1, key_block, 0)
    out = []
    for r in tiles:
        acc = acc_ref[r]
        sum_row = (1 - r % NSA_KV_GROUPS) * NSA_HEAD_DIM
        out.append((acc / jnp.maximum(acc[sum_row:sum_row + 1, :], 1e-30)).T)
    _merge_groups(o_ref, out, gate_ref, 2 if windowed else 1)


def nsa_prompt_attention(q, q_rot, kc, vc, kv_att, gates):
    n, T, D = q.shape
    tq = Q_BLOCK
    qspec = pl.BlockSpec((1, tq, D), lambda b, i: (b, i, 0))
    whole = lambda a: pl.BlockSpec((1,) + a.shape[1:], lambda b, i: (b, 0, 0))
    lanes = lambda j: pl.BlockSpec((1, T, SLAB), lambda b, i: (b, 0, j))
    n_sel = NSA_KV_GROUPS * SEL_LANES
    sel_spec = pl.BlockSpec((1, tq, n_sel), lambda b, i: (b, i, 0))
    gate_spec = pl.BlockSpec((1, tq, gates.shape[2]), lambda b, i: (b, i, 0))
    params = pltpu.CompilerParams(dimension_semantics=("parallel", "arbitrary"),
                                  vmem_limit_bytes=48 * 1024 * 1024)
    qs_scratch = pltpu.VMEM((NSA_TILES, tq, SLAB), jnp.bfloat16)
    o_cmp, sel = pl.pallas_call(
        _nsa_cmp_kernel,
        grid=(n, T // tq),
        in_specs=[qspec, whole(kc), whole(vc), gate_spec],
        out_specs=[qspec, sel_spec],
        out_shape=[jax.ShapeDtypeStruct((n, T, D), jnp.float32),
                   jax.ShapeDtypeStruct((n, T, n_sel), jnp.bfloat16)],
        scratch_shapes=[qs_scratch],
        compiler_params=params,
        name="nsa_cmp_select",
    )(q, kc, vc, gates)
    stat = pltpu.VMEM((NSA_TILES, 1, tq), jnp.float32)
    acc = pltpu.VMEM((NSA_TILES, SLAB, tq), jnp.float32)
    v_t = jnp.swapaxes(jnp.concatenate([kv_att[:, :, SLAB:2 * SLAB], kv_att[:, :, 3 * SLAB:]], axis=-1), 1, 2)
    rows = lambda j: pl.BlockSpec((1, SLAB, T), lambda b, i: (b, j, 0))

    def dense(windowed, branch, name):
        kblock = KEY_BLOCK if windowed else min(SELECTED_KEY_BLOCK, T)
        return pl.pallas_call(
            functools.partial(_nsa_dense_kernel, windowed=windowed, kblock=kblock),
            grid=(n, T // tq),
            in_specs=[qspec, lanes(2 * branch), rows(branch), sel_spec, gate_spec],
            out_specs=qspec,
            out_shape=jax.ShapeDtypeStruct((n, T, D), jnp.float32),
            scratch_shapes=[qs_scratch, stat, acc],
            compiler_params=params,
            name=name,
        )(q_rot, kv_att, v_t, sel, gates)

    return o_cmp, dense(False, 0, "nsa_selected"), dense(True, 1, "nsa_window")


def _rope_tables(pos):
    half = ROT_DIM // 2
    inv = ROPE_THETA ** (-2.0 * jnp.arange(half, dtype=jnp.float32) / ROT_DIM)
    ang = pos.astype(jnp.float32)[:, None] * inv[None, :]
    rest = NSA_HEAD_DIM - ROT_DIM
    cos = jnp.concatenate([jnp.cos(ang), jnp.cos(ang), jnp.ones((pos.shape[0], rest), jnp.float32)], axis=1)
    sin = jnp.concatenate([-jnp.sin(ang), jnp.sin(ang), jnp.zeros((pos.shape[0], rest), jnp.float32)], axis=1)
    return jnp.tile(cos, (1, NSA_KV_GROUPS)), jnp.tile(sin, (1, NSA_KV_GROUPS))


def _nsa_pre_kernel(x_ref, gkv_ref, gmix_ref, cos_ref, sin_ref, wkv_ref, wq_ref, wg_ref,
                    cmp_ref, slc_ref, win_ref, kvb_ref, q_ref, qr_ref, gate_ref):
    xh = _rms(x_ref[0])
    cos, sin = cos_ref[...], sin_ref[...]
    low = lax.broadcasted_iota(jnp.int32, cos.shape, 1) % NSA_HEAD_DIM < ROT_DIM // 2

    def rope(t):
        swapped = jnp.where(low, pltpu.roll(t, SLAB - ROT_DIM // 2, 1), pltpu.roll(t, ROT_DIM // 2, 1))
        return t * cos + swapped * sin

    kv = _bdot(xh * gkv_ref[...], wkv_ref[...])
    part = lambda j: kv[:, j * SLAB:(j + 1) * SLAB]
    k_slc, k_win = rope(part(2)), rope(part(4))
    cmp_ref[0] = kv[:, :2 * SLAB]
    slc_ref[0] = jnp.concatenate([k_slc, part(3)], axis=-1)
    win_ref[0] = jnp.concatenate([k_win, part(5)], axis=-1)
    kvb_ref[0] = jnp.concatenate([k_slc, part(3), k_win, part(5)], axis=-1).astype(jnp.bfloat16)
    h = xh * gmix_ref[...]
    q = _bdot(h, wq_ref[...])
    q_ref[0] = q.astype(jnp.bfloat16)
    qr_ref[0] = jnp.concatenate([rope(q[:, i * SLAB:(i + 1) * SLAB]) for i in range(NSA_SLABS)],
                                axis=-1).astype(jnp.bfloat16)
    gate_ref[0] = jax.nn.sigmoid(_bdot(h, wg_ref[...]))


def nsa_pre(x, pos, norm_kv, norm_mix, w_kv, w_in):
    n, T, D = x.shape
    tm = min(ROW_TILE, T)
    nq = NSA_HEADS * NSA_HEAD_DIM
    cos, sin = _rope_tables(pos)
    w_q = (_to_slabs(w_in[:, :nq]) * NSA_HEAD_DIM ** -0.5).astype(jnp.bfloat16)
    w_g = jnp.pad(w_in[:, nq:], ((0, 0), (0, SLAB - (w_in.shape[1] - nq)))).astype(jnp.bfloat16)
    seq = lambda width: pl.BlockSpec((1, tm, width), lambda b, t: (b, t, 0))
    full = lambda a: pl.BlockSpec(a.shape, lambda b, t: (0,) * a.ndim)
    table = pl.BlockSpec((tm, SLAB), lambda b, t: (t, 0))
    consts = [norm_kv.reshape(1, D), norm_mix.reshape(1, D)]
    weights = [w_kv.astype(jnp.bfloat16), w_q, w_g]
    widths = [2 * SLAB, 2 * SLAB, 2 * SLAB, 4 * SLAB, nq, nq, SLAB]
    dtypes = [jnp.float32] * 3 + [jnp.bfloat16] * 3 + [jnp.float32]
    return pl.pallas_call(
        _nsa_pre_kernel,
        grid=(n, T // tm),
        in_specs=[seq(D)] + [full(a) for a in consts] + [table, table] + [full(a) for a in weights],
        out_specs=[seq(w) for w in widths],
        out_shape=[jax.ShapeDtypeStruct((n, T, w), dt) for w, dt in zip(widths, dtypes)],
        compiler_params=pltpu.CompilerParams(dimension_semantics=("parallel", "parallel"),
                                             vmem_limit_bytes=48 * 1024 * 1024),
        name="nsa_pre",
    )(x, *consts, cos, sin, *weights)


def _to_slabs(x):
    lead = x.shape[:-1]
    x = x.reshape(lead + (NSA_KV_GROUPS, NSA_GROUP_HEADS, NSA_HEAD_DIM))
    return jnp.swapaxes(x, -3, -2).reshape(lead + (NSA_HEADS * NSA_HEAD_DIM,))


def _from_slabs(x):
    lead = x.shape[:-1]
    x = x.reshape(lead + (NSA_GROUP_HEADS, NSA_KV_GROUPS, NSA_HEAD_DIM))
    return jnp.swapaxes(x, -3, -2).reshape(lead + (NSA_HEADS * NSA_HEAD_DIM,))


def nsa_layer_prompt(x, norm_kv, norm_mix, w_kv, cmp_pos, cmp_w1, cmp_b1, cmp_w2, w_in, w_o, norm_next, w_router):
    n, T, D = x.shape
    cmp_kv, slc_kv, win_kv, kv_att, q, q_rot, gates = nsa_pre(x, jnp.arange(T), norm_kv, norm_mix, w_kv, w_in)
    hidden = rows_chunk_hidden(cmp_kv, _chunk_weights(cmp_w1))
    kc, vc = compress_from_hidden(hidden, cmp_pos, cmp_w1, cmp_b1, cmp_w2)
    lanes = lambda a: a.reshape(n, a.shape[1], SLAB).astype(jnp.bfloat16)
    branches = nsa_prompt_attention(q, q_rot, lanes(kc), lanes(vc), kv_att, gates)
    flat = lambda a: a.reshape(n * T, a.shape[-1])
    x_out, h_out, logits = out_proj([flat(o) for o in branches], None, flat(x), _to_slabs(w_o.T).T, norm_next,
                                    w_router)
    return x_out, h_out, logits, cmp_kv, slc_kv, win_kv


def sample_kv_context(cmp_new, slc_new, win_new, cache_cmp_kv, cache_slc_kv, cache_win_kv, page_table,
                      cmp_pos, cmp_w1, cmp_b1, cmp_w2):
    n, S = cmp_new.shape[:2]
    n_new_blk = -(-S // SLC_BLOCK)
    pad = ((0, 0), (0, n_new_blk * SLC_BLOCK - S), (0, 0), (0, 0), (0, 0))
    w_chunk = _chunk_weights(cmp_w1)
    hidden = jnp.concatenate([paged_chunk_hidden(cache_cmp_kv, page_table, _token_weights(cmp_w1)),
                              rows_chunk_hidden(jnp.pad(cmp_new.astype(cache_cmp_kv.dtype), pad), w_chunk)], axis=1)
    kc, vc = compress_from_hidden(hidden, cmp_pos, cmp_w1, cmp_b1, cmp_w2)
    win_all = jnp.concatenate([cache_win_kv, win_new.astype(cache_win_kv.dtype)], axis=1)
    return kc, vc, win_all


def sample_nsa(h, kc, vc, slc_new, win_new, cache_slc_kv, cache_win_kv, page_table, w_in, w_o):
    n, S, _ = h.shape
    assert S == 1
    q, gate = nsa_query(h, w_in, precise=True)
    t_pos = PAST_LEN + jnp.arange(S)
    with jax.default_matmul_precision("highest"):
        o_cmp, idx = compressed_branch_and_selection(q, t_pos, kc, vc)
    q_rot = (rope_partial(q, t_pos) * NSA_HEAD_DIM ** -0.5).reshape(n, NSA_HEADS, NSA_HEAD_DIM)
    o_slc, o_win = decode_attention(q_rot, idx[:, :, 0], page_table, cache_slc_kv, cache_win_kv, slc_new, win_new)
    g = jax.nn.sigmoid(gate.astype(jnp.float32)).astype(q.dtype)
    o = g[..., 0:1] * o_cmp + g[..., 1:2] * o_slc.reshape(q.shape) + g[..., 2:3] * o_win.reshape(q.shape)
    return dense(o.reshape(n, S, NSA_HEADS * NSA_HEAD_DIM), w_o, precise=True)


def kernel(x_prompt, x_sample, state_wkv, state_shift, cache_cmp_kv, cache_slc_kv, cache_win_kv, page_table, norm_mix, norm_ffn, norm_kv, norm_final, rw_mu, rw_w_rkv, rw_w0, rw_w1, rw_w2, rw_a0, rw_a1, rw_a2, rw_g1, rw_g2, rw_k_k, rw_k_a, rw_r_k, rw_gn_w, rw_gn_b, rw_w_o, nsa_w_kv, nsa_cmp_pos, nsa_cmp_w1, nsa_cmp_b1, nsa_cmp_w2, nsa_w_in, nsa_w_o, ffn_w_gu, ffn_w_down, moe_router, moe_w_gu, moe_w_down):
    cmp_params = (nsa_cmp_pos, nsa_cmp_w1, nsa_cmp_b1, nsa_cmp_w2)
    assert DEPTH == 2 and N_A_LAYERS == 1
    D = D_MODEL

    n_p, T = x_prompt.shape[:2]
    x1, h1, wkv_fin, h_last = rwkv7_layer(
        x_prompt, jnp.zeros((n_p, D), x_prompt.dtype), jnp.zeros((n_p, RWKV_HEADS, RWKV_HEAD, RWKV_HEAD), jnp.float32),
        norm_mix[0], rw_mu[0], rw_w_rkv[0], rw_w0[0], rw_w1[0], rw_w2[0], rw_a0[0], rw_a1[0], rw_a2[0],
        rw_g1[0], rw_g2[0], rw_k_k[0], rw_k_a[0], rw_r_k[0], rw_gn_w[0], rw_gn_b[0], rw_w_o[0], norm_ffn[0])
    wkv_p, shift_p = wkv_fin[None], h_last[None]
    x2 = swiglu_residual(x1, h1, ffn_w_gu[0], ffn_w_down[0])
    x_p, h_p, logits_p, cmp_rows, slc_rows, win_rows = nsa_layer_prompt(
        x2.reshape(n_p, T, D), norm_kv, norm_mix[1], nsa_w_kv, *cmp_params, nsa_w_in[0], nsa_w_o[0], norm_ffn[1],
        moe_router[0])
    kv_shape = (n_p, T, 2, NSA_KV_GROUPS, NSA_HEAD_DIM)
    cmp_kv_p, slc_kv_p = cmp_rows.reshape(kv_shape), slc_rows.reshape(kv_shape)
    win_kv_p = win_rows.reshape(kv_shape)[:, -min(WINDOW, T):]

    pos_s = PAST_LEN + jnp.arange(x_sample.shape[1], dtype=jnp.int32)
    h = rmsnorm(x_sample, norm_mix[0])
    y, s_fin, h_last_s = rwkv7_time_mix(
        h, state_shift[0], state_wkv[0], rw_mu[0], rw_w_rkv[0], rw_w0[0], rw_w1[0], rw_w2[0],
        rw_a0[0], rw_a1[0], rw_a2[0], rw_g1[0], rw_g2[0], rw_k_k[0], rw_k_a[0],
        rw_r_k[0], rw_gn_w[0], rw_gn_b[0], rw_w_o[0], precise=True)
    wkv_s, shift_s = s_fin[None], h_last_s[None]
    x_s = x_sample + y
    x_s = x_s + swiglu(rmsnorm(x_s, norm_ffn[0]), ffn_w_gu[0], ffn_w_down[0], precise=True)
    cmp_kv_s, slc_kv_s, win_new = shared_kv_rows(x_s, norm_kv, nsa_w_kv, pos_s, precise=True)
    kc_s, vc_s, win_all = sample_kv_context(cmp_kv_s, slc_kv_s, win_new, cache_cmp_kv, cache_slc_kv, cache_win_kv,
                                            page_table, *cmp_params)
    x_s = x_s + sample_nsa(rmsnorm(x_s, norm_mix[1]), kc_s, vc_s, slc_kv_s, win_new, cache_slc_kv, cache_win_kv,
                           page_table, nsa_w_in[0], nsa_w_o[0])
    win_kv_s = win_all[:, -cache_win_kv.shape[1]:]
    x_s = x_s.reshape(-1, D)
    h_s = rmsnorm(x_s, norm_ffn[1])
    logits_s = matmul_precise(h_s, moe_router[0])

    n_tok_p = n_p * T
    logits = jnp.concatenate([logits_p[:, :N_EXPERTS], logits_s], axis=0)
    y0, y1, gate = moe_swiglu(jnp.concatenate([h_p, h_s.astype(jnp.bfloat16)], axis=0), logits,
                              moe_w_gu[0], moe_w_down[0])
    gate = jnp.pad(gate, ((0, 0), (0, 128 - TOP_K)))
    y_prompt = combine_norm(x_p, y0[:n_tok_p], y1[:n_tok_p], gate[:n_tok_p], norm_final).reshape(x_prompt.shape)
    y_sample = combine_norm(x_s, y0[n_tok_p:], y1[n_tok_p:], gate[n_tok_p:], norm_final).reshape(x_sample.shape)


    return (y_prompt, y_sample, wkv_p, shift_p, cmp_kv_p, slc_kv_p, win_kv_p,
            wkv_s, shift_s, cmp_kv_s, slc_kv_s, win_kv_s)
```

```python
import functools
import math

import jax
import jax.numpy as jnp
from jax import lax
from jax.experimental import pallas as pl
from jax.experimental.pallas import tpu as pltpu

D_MODEL = 1024
DEPTH = 2
PAST_LEN = 16384
PAGE_SIZE = 128
N_A_LAYERS = DEPTH // 2
RWKV_HEAD = 64
RWKV_HEADS = D_MODEL // RWKV_HEAD
DECAY_SCALE = math.exp(-0.5)
GN_EPS = RWKV_HEAD * 1e-5
NSA_HEADS = 16
NSA_HEAD_DIM = 64
NSA_KV_GROUPS = 2
NSA_GROUP_HEADS = NSA_HEADS // NSA_KV_GROUPS
N_BRANCH = 3
CMP_BLOCK = 32
CMP_STRIDE = 16
CMP_HIDDEN = 128
SLC_BLOCK = 64
SLC_TOPN = 16
WINDOW = 512
Q_BLOCK = 128
FORCED_SCORE = 1e4
ROPE_THETA = 500000.0
ROT_DIM = NSA_HEAD_DIM // 4
N_EXPERTS = 8
TOP_K = 2
NORM_EPS = 1e-6


def _rmsnorm_kernel(x_ref, g_ref, o_ref):
    x = x_ref[...]
    y = x * lax.rsqrt(jnp.mean(x * x, axis=-1, keepdims=True) + NORM_EPS)
    o_ref[...] = y * g_ref[...]


def rmsnorm(x, g):
    shp = x.shape
    x2 = x.reshape(-1, shp[-1])
    rows = x2.shape[0]
    tm = min(rows, 512)
    out = pl.pallas_call(
        _rmsnorm_kernel,
        grid=(rows // tm,),
        in_specs=[pl.BlockSpec((tm, shp[-1]), lambda i: (i, 0)),
                  pl.BlockSpec((1, shp[-1]), lambda i: (0, 0))],
        out_specs=pl.BlockSpec((tm, shp[-1]), lambda i: (i, 0)),
        out_shape=jax.ShapeDtypeStruct(x2.shape, x.dtype),
        name="rmsnorm",
    )(x2, g.reshape(1, -1))
    return out.reshape(shp)


def rope_partial(x, pos):
    half = ROT_DIM // 2
    inv = ROPE_THETA ** (-2.0 * jnp.arange(half, dtype=jnp.float32) / ROT_DIM)
    ang = pos.astype(jnp.float32)[:, None] * inv[None, :]
    shape = (1, pos.shape[0]) + (1,) * (x.ndim - 3) + (half,)
    cos = jnp.cos(ang).reshape(shape)
    sin = jnp.sin(ang).reshape(shape)
    xf = x.astype(jnp.float32)
    x1, x2 = xf[..., :half], xf[..., half:ROT_DIM]
    out = jnp.concatenate([x1 * cos - x2 * sin, x2 * cos + x1 * sin, xf[..., ROT_DIM:]], axis=-1)
    return out.astype(x.dtype)


def masked_softmax(s, mask):
    s = jnp.where(mask, s.astype(jnp.float32), -jnp.inf)
    m = jnp.max(s, axis=-1, keepdims=True)
    m = jnp.where(jnp.isfinite(m), m, 0.0)
    e = jnp.where(mask, jnp.exp(s - m), 0.0)
    return e / jnp.maximum(jnp.sum(e, axis=-1, keepdims=True), 1e-30)


FF_CHUNK = 1408
SWIGLU_ROWS = 512


def _swiglu_kernel(blk_e_ref, n_used_ref, x_ref, wg_ref, wu_ref, wd_ref, *rest):
    o_ref = rest[-1]
    i = pl.program_id(0)
    f = pl.program_id(1)

    @pl.when(i < n_used_ref[0])
    def _():
        x = x_ref[...].astype(jnp.bfloat16)
        g = jnp.dot(x, wg_ref[0], preferred_element_type=jnp.float32)
        u = jnp.dot(x, wu_ref[0], preferred_element_type=jnp.float32)
        act = (g * jax.nn.sigmoid(g) * u).astype(jnp.bfloat16)
        y = jnp.dot(act, wd_ref[0], preferred_element_type=jnp.float32)

        @pl.when(f == 0)
        def _():
            o_ref[...] = y + rest[0][...] if len(rest) == 2 else y

        @pl.when(f > 0)
        def _():
            o_ref[...] += y

    @pl.when(i >= n_used_ref[0])
    def _():
        o_ref[...] = jnp.zeros(o_ref.shape, o_ref.dtype)


def grouped_swiglu(xb, blk_e, n_used, w_gu, w_down, res=None):
    rows, d = xb.shape
    extra = [] if res is None else [res]
    b = min(SWIGLU_ROWS, rows)
    n_blk = rows // b
    ff = w_down.shape[1]
    tf = FF_CHUNK
    n_f = ff // tf
    chunk = lambda i, f, be, nu: jnp.where(i < nu[0], f, n_f - 1)
    return pl.pallas_call(
        _swiglu_kernel,
        grid_spec=pltpu.PrefetchScalarGridSpec(
            num_scalar_prefetch=2,
            grid=(n_blk, n_f),
            in_specs=[
                pl.BlockSpec((b, d), lambda i, f, be, nu: (i, 0)),
                pl.BlockSpec((1, d, tf), lambda i, f, be, nu: (be[i], 0, chunk(i, f, be, nu))),
                pl.BlockSpec((1, d, tf), lambda i, f, be, nu: (be[i], 0, n_f + chunk(i, f, be, nu))),
                pl.BlockSpec((1, tf, d), lambda i, f, be, nu: (be[i], chunk(i, f, be, nu), 0)),
            ] + [pl.BlockSpec((b, d), lambda i, f, be, nu: (i, 0))] * len(extra),
            out_specs=pl.BlockSpec((b, d), lambda i, f, be, nu: (i, 0)),
        ),
        out_shape=jax.ShapeDtypeStruct((rows, d), jnp.float32),
        compiler_params=pltpu.CompilerParams(dimension_semantics=("arbitrary", "arbitrary"),
                                             vmem_limit_bytes=56 * 1024 * 1024),
        name="grouped_swiglu",
    )(blk_e.astype(jnp.int32), jnp.reshape(n_used, (1,)).astype(jnp.int32), xb, w_gu, w_gu, w_down, *extra)


def swiglu(h, w_gu, w_down, precise=False):
    g, u = jnp.split(dense(h, w_gu, precise), 2, axis=-1)
    return dense(jax.nn.silu(g) * u, w_down, precise)


def swiglu_residual(x, h, w_gu, w_down):
    n_blk = h.shape[0] // min(SWIGLU_ROWS, h.shape[0])
    return grouped_swiglu(h, jnp.zeros((n_blk,), jnp.int32), jnp.int32(n_blk),
                          w_gu.astype(jnp.bfloat16)[None], w_down.astype(jnp.bfloat16)[None], res=x)


def _combine_norm_kernel(x_ref, y0_ref, y1_ref, g_ref, gain_ref, o_ref):
    g = g_ref[...]
    x = x_ref[...] + (y0_ref[...] * g[:, 0:1] + y1_ref[...] * g[:, 1:2])
    o_ref[...] = _rms(x) * gain_ref[...]


def combine_norm(x, y0, y1, gate, gain):
    m, d = x.shape
    tm = min(2 * ROW_TILE, m)
    rows = lambda width: pl.BlockSpec((tm, width), lambda i: (i, 0))
    return pl.pallas_call(
        _combine_norm_kernel,
        grid=(m // tm,),
        in_specs=[rows(d), rows(d), rows(d), rows(gate.shape[1]), pl.BlockSpec((1, d), lambda i: (0, 0))],
        out_specs=rows(d),
        out_shape=jax.ShapeDtypeStruct((m, d), jnp.float32),
        compiler_params=pltpu.CompilerParams(dimension_semantics=("parallel",)),
        name="combine_norm",
    )(x, y0, y1, gate, gain.reshape(1, d))


def moe_swiglu(xt, logits, w_gu_e, w_down_e):
    n_tok, d = xt.shape
    b = SWIGLU_ROWS
    experts = jnp.arange(N_EXPERTS)[None, :]
    e0 = jnp.argmax(logits, axis=-1)
    v0 = jnp.max(logits, axis=-1)
    rest = jnp.where(experts == e0[:, None], -jnp.inf, logits)
    e1 = jnp.argmax(rest, axis=-1)
    v1 = jnp.max(rest, axis=-1)
    top_idx = jnp.stack([e0, e1], axis=-1).astype(jnp.int32)
    gate = jax.nn.softmax(jnp.stack([v0, v1], axis=-1), axis=-1)
    nk = n_tok * TOP_K
    flat_e = top_idx.reshape(nk)
    onehot = (flat_e[:, None] == jnp.arange(N_EXPERTS)[None, :]).astype(jnp.int32)
    before = jnp.cumsum(onehot, axis=0) - onehot
    counts = jnp.sum(onehot, axis=0)
    padded = (counts + b - 1) // b * b
    ends_pad = jnp.cumsum(padded)
    starts_pad = ends_pad - padded
    dest = jnp.sum(onehot * (starts_pad[None, :] + before), axis=1)
    n_rows = (nk + b - 1) // b * b + N_EXPERTS * b
    n_blk = n_rows // b
    flat_tok = jnp.repeat(jnp.arange(n_tok, dtype=jnp.int32), TOP_K)
    row_tok = jnp.full((n_rows,), n_tok, jnp.int32).at[dest].set(flat_tok)
    blk_e = jnp.minimum(jnp.searchsorted(ends_pad, jnp.arange(n_blk) * b, side='right'), N_EXPERTS - 1)
    x_pad = jnp.concatenate([xt.astype(jnp.float32), jnp.zeros((1, d), jnp.float32)], axis=0)
    yb = grouped_swiglu(x_pad[row_tok], blk_e, ends_pad[-1] // b,
                        w_gu_e.astype(jnp.bfloat16), w_down_e.astype(jnp.bfloat16))
    dest = dest.reshape(n_tok, TOP_K)
    return yb[dest[:, 0]], yb[dest[:, 1]], gate


WKV_CHUNK = 64
WKV_HEADS_PER_STEP = 16

_NN = (((1,), (0,)), ((), ()))
_NT = (((1,), (1,)), ((), ()))
_TN = (((0,), (0,)), ((), ()))


def _mm(x, y, dims):
    return lax.dot_general(x.astype(jnp.bfloat16), y.astype(jnp.bfloat16), dims,
                           preferred_element_type=jnp.float32)


def _wkv7_chunk_kernel(r_ref, lw_ref, k_ref, v_ref, kk_ref, a_ref, rk_ref, gnw_ref, gnb_ref, s0_ref,
                       y_ref, sout_ref, state_ref):
    c = pl.program_id(2)
    L = r_ref.shape[1]
    n_pairs = state_ref.shape[0]
    N = RWKV_HEAD
    W = 2 * N

    def block_diag(top, bottom):
        z = jnp.zeros((N, N), jnp.float32)
        return jnp.concatenate([jnp.concatenate([top, z], axis=1), jnp.concatenate([z, bottom], axis=1)], axis=0)

    @pl.when(c == 0)
    def _():
        for p in range(n_pairs):
            state_ref[p] = block_diag(s0_ref[0, 2 * p], s0_ref[0, 2 * p + 1])

    row2 = lax.broadcasted_iota(jnp.int32, (2 * L, W), 0)
    lane2 = lax.broadcasted_iota(jnp.int32, (2 * L, W), 1)
    own_lanes = (row2 // L) == (lane2 // N)
    rr = lax.broadcasted_iota(jnp.int32, (2 * L, 2 * L), 0)
    cc = lax.broadcasted_iota(jnp.int32, (2 * L, 2 * L), 1)
    same = (rr // L) == (cc // L)
    strict = same & (rr % L > cc % L)
    incl = same & (rr % L >= cc % L)
    wr = lax.broadcasted_iota(jnp.int32, (W, W), 0)
    wc = lax.broadcasted_iota(jnp.int32, (W, W), 1)
    eye_w = wr == wc
    ones_bd = ((wr // N) == (wc // N)).astype(jnp.bfloat16)
    tl = lax.broadcasted_iota(jnp.int32, (L, L), 0) >= lax.broadcasted_iota(jnp.int32, (L, L), 1)
    tri = tl.astype(jnp.bfloat16)

    def head_sum(x):
        hi, lo = _split_bf16(x)
        return (jnp.dot(hi, ones_bd, preferred_element_type=jnp.float32)
                + jnp.dot(lo, ones_bd, preferred_element_type=jnp.float32))

    def stack(x):
        return jnp.where(own_lanes, jnp.concatenate([x, x], axis=0), 0.0)

    unstack = lambda x: x[:L] + x[L:]

    lw = lw_ref[0]
    lw_hi, lw_lo = _split_bf16(lw)
    cum = (lax.dot_general(tri, lw_hi, _NN, preferred_element_type=jnp.float32)
           + lax.dot_general(tri, lw_lo, _NN, preferred_element_type=jnp.float32))
    cum_last = cum[L - 1:L, :]
    e_neg_all = jnp.exp(-cum)
    e_tail_all = jnp.exp(cum_last - cum)
    e_prev_all = jnp.exp(cum - lw)
    e_cum_all = jnp.exp(cum)
    wl_all = jnp.exp(cum_last)

    pairs = range(n_pairs)
    slab = lambda t, p: t[:, p * W:(p + 1) * W]
    r_in = [slab(r_ref[0], p) for p in pairs]
    k_in = [slab(k_ref[0], p) for p in pairs]
    v_in = [slab(v_ref[0], p) for p in pairs]
    ssq = [head_sum(slab(kk_ref[0], p) * slab(kk_ref[0], p)) for p in pairs]
    at_st, bt_st, bh_st, rt_st, kt_st, kh_st, v_st = [], [], [], [], [], [], []
    for p in pairs:
        kk = slab(kk_ref[0], p) / jnp.maximum(jnp.sqrt(ssq[p]), 1e-12)
        b = kk * slab(a_ref[0], p)
        at_st.append(stack(-kk * slab(e_prev_all, p)))
        bt_st.append(stack(b * slab(e_neg_all, p)))
        bh_st.append(stack(b * slab(e_tail_all, p)))
        rt_st.append(stack(r_in[p] * slab(e_cum_all, p)))
        kt_st.append(stack(k_in[p] * slab(e_neg_all, p)))
        kh_st.append(stack(k_in[p] * slab(e_tail_all, p)))
        v_st.append(stack(v_in[p]))
    a_ab = [jnp.where(strict, _mm(at_st[p], bt_st[p], _NT), 0.0) for p in pairs]
    a_ak = [jnp.where(strict, _mm(at_st[p], kt_st[p], _NT), 0.0) for p in pairs]
    r_b = [jnp.where(incl, _mm(rt_st[p], bt_st[p], _NT), 0.0) for p in pairs]
    r_k = [jnp.where(incl, _mm(rt_st[p], kt_st[p], _NT), 0.0) for p in pairs]
    av = [_mm(a_ak[p], v_st[p], _NN) for p in pairs]
    ht = [_mm(v_st[p], kh_st[p], _TN) for p in pairs]
    yp = [_mm(r_k[p], v_st[p], _NN) for p in pairs]
    pw = a_ab
    inv_a = a_ab
    n = 1
    while 2 * n < L:
        pw = [_mm(pw[p], pw[p], _NN) for p in pairs]
        inv_a = [inv_a[p] + pw[p] + _mm(inv_a[p], pw[p], _NN) for p in pairs]
        n *= 2
    ap = [at_st[p] + _mm(inv_a[p], at_st[p], _NN) for p in pairs]
    vp = [av[p] + _mm(inv_a[p], av[p], _NN) for p in pairs]
    g = [jnp.where(eye_w, slab(wl_all, p), 0.0) + _mm(bh_st[p], ap[p], _TN) for p in pairs]
    ht = [ht[p] + _mm(vp[p], bh_st[p], _TN) for p in pairs]
    rp = [unstack(rt_st[p] + _mm(r_b[p], ap[p], _NN)) for p in pairs]
    yp = [unstack(yp[p] + _mm(r_b[p], vp[p], _NN)) for p in pairs]
    s_prev = [state_ref[p] for p in pairs]
    y = [_mm(rp[p], s_prev[p], _NT) + yp[p] for p in pairs]
    for p in pairs:
        state_ref[p] = _mm(s_prev[p], g[p], _NT) + ht[p]
    mean = [head_sum(y[p]) * (1.0 / N) for p in pairs]
    cen = [y[p] - mean[p] for p in pairs]
    var = [head_sum(cen[p] * cen[p]) * (1.0 / N) for p in pairs]
    bonus = [head_sum(r_in[p] * k_in[p] * slab(rk_ref[...], p)) * v_in[p] for p in pairs]
    y_ref[0] = jnp.concatenate(
        [cen[p] * lax.rsqrt(var[p] + GN_EPS) * slab(gnw_ref[...], p) + slab(gnb_ref[...], p) + bonus[p]
         for p in pairs], axis=-1)

    @pl.when(c == pl.num_programs(2) - 1)
    def _():
        for p in pairs:
            s = state_ref[p]
            sout_ref[0, 2 * p] = s[:N, :N]
            sout_ref[0, 2 * p + 1] = s[N:, N:]


def wkv7_chunked(r, lw, k, v, kk, a_gate, r_k, gn_w, gn_b, s0):
    n, T, D = r.shape
    L = WKV_CHUNK
    hb = WKV_HEADS_PER_STEP
    w = hb * RWKV_HEAD
    seq = pl.BlockSpec((1, L, w), lambda b, h, c: (b, c, h))
    vec = pl.BlockSpec((1, w), lambda b, h, c: (0, h))
    st = pl.BlockSpec((1, hb, RWKV_HEAD, RWKV_HEAD), lambda b, h, c: (b, h, 0, 0))
    row = lambda t: t.reshape(1, D).astype(jnp.float32)
    return pl.pallas_call(
        _wkv7_chunk_kernel,
        grid=(n, D // w, T // L),
        in_specs=[seq] * 6 + [vec] * 3 + [st],
        out_specs=[seq, st],
        out_shape=[jax.ShapeDtypeStruct((n, T, D), jnp.float32),
                   jax.ShapeDtypeStruct(s0.shape, jnp.float32)],
        scratch_shapes=[pltpu.VMEM((hb // 2, 2 * RWKV_HEAD, 2 * RWKV_HEAD), jnp.float32)],
        compiler_params=pltpu.CompilerParams(dimension_semantics=("parallel", "parallel", "arbitrary")),
        name="wkv7_chunked",
    )(r, lw, k, v, kk, a_gate, row(r_k), row(gn_w), row(gn_b), s0)


def wkv7_scan(r, lw, k, v, a_vec, b_vec, s0):
    def step(S, inp):
        r_t, lw_t, k_t, v_t, a_t, b_t = inp
        sa = jnp.sum(S * a_t[:, :, None, :], axis=-1)
        S = S * jnp.exp(lw_t)[:, :, None, :] + sa[..., None] * b_t[:, :, None, :] + v_t[..., None] * k_t[:, :, None, :]
        return S, jnp.sum(S * r_t[:, :, None, :], axis=-1)

    xs = tuple(jnp.moveaxis(t, 1, 0) for t in (r, lw, k, v, a_vec, b_vec))
    s_fin, ys = lax.scan(step, s0, xs)
    return jnp.moveaxis(ys, 0, 1), s_fin


ROW_TILE = 256


def _bdot(x, w):
    return jnp.dot(x.astype(jnp.bfloat16), w, preferred_element_type=jnp.float32)


def _rms(x):
    return x * lax.rsqrt(jnp.mean(x * x, axis=-1, keepdims=True) + NORM_EPS)


def _rwkv_pre_kernel(x_ref, shift_ref, gain_ref, mu_ref, vec_ref, wrkv_ref, w1_ref, a1_ref, g1_ref,
                     w2_ref, a2_ref, g2_ref,
                     r_ref, lw_ref, k_ref, v_ref, kk_ref, a_ref, g_ref, hlast_ref, prev_ref):
    @pl.when(pl.program_id(1) == 0)
    def _():
        prev_ref[...] = shift_ref[0]

    tm = x_ref.shape[1]
    h = _rms(x_ref[0]) * gain_ref[...]
    first_row = lax.broadcasted_iota(jnp.int32, h.shape, 0) == 0
    h_prev = jnp.where(first_row, prev_ref[...], pltpu.roll(h, 1, 0))
    prev_ref[...] = h[tm - 1:tm, :]
    hlast_ref[0] = h[tm - 1:tm, :]
    dx = h_prev - h
    mix = lambda i: (h + dx * mu_ref[i:i + 1, :]).astype(jnp.bfloat16)
    w0, a0, k_k, k_a = (vec_ref[i:i + 1, :] for i in range(4))
    r_ref[0] = _bdot(mix(0), wrkv_ref[0])
    k = _bdot(mix(2), wrkv_ref[1])
    v_ref[0] = _bdot(mix(3), wrkv_ref[2])
    lw_ref[0] = -DECAY_SCALE * jax.nn.sigmoid(w0 + _bdot(jnp.tanh(_bdot(mix(1), w1_ref[...])), w2_ref[...]))
    a = jax.nn.sigmoid(a0 + _bdot(_bdot(mix(4), a1_ref[...]), a2_ref[...]))
    g_ref[0] = _bdot(jax.nn.sigmoid(_bdot(mix(5), g1_ref[...])), g2_ref[...])
    a_ref[0] = a
    kk_ref[0] = k * k_k
    k_ref[0] = k * (1.0 + (a - 1.0) * k_a)


def _out_proj_kernel(*refs, n_terms, gated, routed):
    terms = refs[:n_terms]
    rest = refs[n_terms:]
    if gated:
        gate_ref, rest = rest[0], rest[1:]
    x_ref, w_ref, gain_ref = rest[:3]
    rest = rest[3:]
    if routed:
        wr_ref, rest = rest[0], rest[1:]
    xo_ref, ho_ref = rest[:2]
    y = terms[0][...]
    for t in terms[1:]:
        y = y + t[...]
    if gated:
        y = y * gate_ref[...]
    xo = x_ref[...] + _bdot(y, w_ref[...])
    xo_ref[...] = xo
    h = _rms(xo) * gain_ref[...]
    ho_ref[...] = h.astype(ho_ref.dtype)
    if routed:
        h_hi, h_lo = _split_bf16(h)
        w_hi, w_lo = _split_bf16(wr_ref[...])
        dot = functools.partial(jnp.dot, preferred_element_type=jnp.float32)
        rest[2][...] = dot(h_hi, w_hi) + (dot(h_hi, w_lo) + dot(h_lo, w_hi))


def out_proj(terms, gate, x, w, gain, w_router=None):
    m, d = x.shape
    kdim = w.shape[0]
    tm = min(ROW_TILE, m)
    rows = lambda width: pl.BlockSpec((tm, width), lambda i: (i, 0))
    full = lambda a: pl.BlockSpec(a.shape, lambda i: (0, 0))
    ins = list(terms) + ([gate] if gate is not None else [])
    consts = [w.astype(jnp.bfloat16), gain.reshape(1, d)]
    out_specs = [rows(d), rows(d)]
    out_shape = [jax.ShapeDtypeStruct((m, d), jnp.float32), jax.ShapeDtypeStruct((m, d), jnp.bfloat16)]
    if w_router is not None:
        consts.append(jnp.pad(w_router, ((0, 0), (0, 128 - w_router.shape[1]))))
        out_specs.append(rows(128))
        out_shape.append(jax.ShapeDtypeStruct((m, 128), jnp.float32))
    return pl.pallas_call(
        functools.partial(_out_proj_kernel, n_terms=len(terms), gated=gate is not None,
                          routed=w_router is not None),
        grid=(m // tm,),
        in_specs=[rows(kdim)] * len(ins) + [rows(d)] + [full(a) for a in consts],
        out_specs=out_specs,
        out_shape=out_shape,
        compiler_params=pltpu.CompilerParams(dimension_semantics=("parallel",),
                                             vmem_limit_bytes=48 * 1024 * 1024),
        name="out_proj",
    )(*ins, x, *consts)


def rwkv7_layer(x, shift0, s0, gain, mu, w_rkv, w0, w1, w2, a0, a1, a2, g1, g2, k_k, k_a, r_k, gn_w, gn_b, w_o,
                gain_next):
    n, T, D = x.shape
    tm = min(ROW_TILE, T)
    bf = lambda t: t.astype(jnp.bfloat16)
    seq = pl.BlockSpec((1, tm, D), lambda b, t: (b, t, 0))
    full = lambda a: pl.BlockSpec(a.shape, lambda b, t: (0,) * a.ndim)
    per_seq = pl.BlockSpec((1, 1, D), lambda b, t: (b, 0, 0))
    vecs = jnp.stack([w0, a0, k_k, k_a]).astype(jnp.float32)
    weights = [bf(w_rkv), bf(w1), bf(a1), bf(g1), bf(w2), bf(a2), bf(g2)]
    small = [gain.reshape(1, D), mu, vecs]
    outs = pl.pallas_call(
        _rwkv_pre_kernel,
        grid=(n, T // tm),
        in_specs=[seq, per_seq] + [full(a) for a in small + weights],
        out_specs=[seq] * 7 + [per_seq],
        out_shape=[jax.ShapeDtypeStruct((n, T, D), jnp.float32)] * 7
                  + [jax.ShapeDtypeStruct((n, 1, D), jnp.float32)],
        scratch_shapes=[pltpu.VMEM((1, D), jnp.float32)],
        compiler_params=pltpu.CompilerParams(dimension_semantics=("parallel", "arbitrary"),
                                             vmem_limit_bytes=56 * 1024 * 1024),
        name="rwkv_pre",
    )(x, shift0.reshape(n, 1, D), *small, *weights)
    r, lw, k, v, kk, a_gate, g, h_last = outs
    y, s_fin = wkv7_chunked(r, lw, k, v, kk, a_gate, r_k.reshape(-1), gn_w, gn_b, s0.astype(jnp.float32))
    flat = lambda t: t.reshape(n * T, D)
    x1, h1 = out_proj([flat(y)], flat(g), flat(x), w_o, gain_next)
    return x1, h1, s_fin, h_last.reshape(n, D)


def rwkv7_time_mix(h, h_prev, s0, mu, w_rkv, w0, w1, w2, a0, a1, a2, g1, g2, k_k, k_a, r_k, gn_w, gn_b, w_o,
                   precise=False):
    n, T, D = h.shape
    f32 = jnp.float32
    mm = functools.partial(dense, precise=precise)
    dx = jnp.concatenate([h_prev[:, None, :].astype(h.dtype), h[:, :-1]], axis=1) - h
    xr, xw, xk, xv, xa, xg = (h + dx * mu[i] for i in range(6))
    r = mm(xr, w_rkv[0])
    k = mm(xk, w_rkv[1])
    v = mm(xv, w_rkv[2])
    log_decay = -DECAY_SCALE * jax.nn.sigmoid((w0 + mm(jnp.tanh(mm(xw, w1)), w2)).astype(f32))
    a = jax.nn.sigmoid((a0 + mm(mm(xa, a1), a2)).astype(f32))
    g = mm(jax.nn.sigmoid(mm(xg, g1)), g2)
    heads = lambda t: t.astype(f32).reshape(n, T, RWKV_HEADS, RWKV_HEAD)
    kk = heads(k * k_k)
    kk = kk / jnp.maximum(jnp.sqrt(jnp.sum(kk * kk, axis=-1, keepdims=True)), 1e-12)
    a_h = heads(a)
    k_h = heads(k.astype(f32) * (1.0 + (a - 1.0) * k_a.astype(f32)))
    r_h, v_h = heads(r), heads(v)
    y, s_fin = wkv7_scan(r_h, heads(log_decay), k_h, v_h, -kk, kk * a_h, s0.astype(f32))
    mean = jnp.mean(y, axis=-1, keepdims=True)
    var = jnp.mean(jnp.square(y - mean), axis=-1, keepdims=True)
    y = ((y - mean) * lax.rsqrt(var + GN_EPS)).reshape(n, T, D) * gn_w.astype(f32) + gn_b.astype(f32)
    bonus = jnp.sum(r_h * k_h * r_k.astype(f32), axis=-1, keepdims=True) * v_h
    y = (y + bonus.reshape(n, T, D)).astype(h.dtype)
    return mm(y * g, w_o), s_fin, h[:, -1]


def shared_kv_rows(x, norm_kv, w_kv, pos, precise=False):
    n, T, _ = x.shape
    kv = dense(rmsnorm(x, norm_kv), w_kv, precise).reshape(n, T, N_BRANCH, 2, NSA_KV_GROUPS, NSA_HEAD_DIM)
    cmp_kv = kv[:, :, 0]
    slc_kv = jnp.stack([rope_partial(kv[:, :, 1, 0], pos), kv[:, :, 1, 1]], axis=2)
    win_kv = jnp.stack([rope_partial(kv[:, :, 2, 0], pos), kv[:, :, 2, 1]], axis=2)
    return cmp_kv, slc_kv, win_kv


CHUNK_LANES = CMP_STRIDE * 2 * NSA_KV_GROUPS * NSA_HEAD_DIM
HIDDEN_LANES = 2 * NSA_KV_GROUPS * 2 * CMP_HIDDEN
PAGES_PER_STEP = 32
CHUNKS_PER_PAGE = PAGE_SIZE // CMP_STRIDE


def _chunk_weights(cmp_w1):
    w = cmp_w1.reshape(2, 2, CMP_STRIDE, NSA_HEAD_DIM, CMP_HIDDEN)
    w = w.transpose(2, 0, 3, 1, 4)
    eye = jnp.eye(2, dtype=w.dtype)
    big = jnp.einsum('ab,cf,jaehk->jacebfhk', eye, jnp.eye(NSA_KV_GROUPS, dtype=w.dtype), w)
    return big.reshape(CHUNK_LANES, HIDDEN_LANES).astype(jnp.bfloat16)


def _token_weights(cmp_w1):
    w = cmp_w1.reshape(2, 2, CMP_STRIDE, NSA_HEAD_DIM, CMP_HIDDEN)
    return w.transpose(2, 0, 3, 1, 4).reshape(CMP_STRIDE, 2, NSA_HEAD_DIM, 2 * CMP_HIDDEN).astype(jnp.bfloat16)


def _paged_chunk_kernel(pt_ref, *refs):
    pages, w_ref, o_ref, xt_ref = (refs[:PAGES_PER_STEP], refs[PAGES_PER_STEP], refs[PAGES_PER_STEP + 1],
                                   refs[PAGES_PER_STEP + 2])
    planes = [(kv, g) for kv in range(2) for g in range(NSA_KV_GROUPS)]
    tok = lax.broadcasted_iota(jnp.int32, (PAGE_SIZE, PAGE_SIZE), 0)
    dst = lax.broadcasted_iota(jnp.int32, (PAGE_SIZE, PAGE_SIZE), 1)
    regroup = (tok == (dst % CHUNKS_PER_PAGE) * CMP_STRIDE + dst // CHUNKS_PER_PAGE).astype(jnp.bfloat16)
    for p in range(PAGES_PER_STEP):
        for i, (kv, g) in enumerate(planes):
            xt_ref[p, i] = _bdot(pages[p][0, kv, g], regroup).T
    width = 2 * CMP_HIDDEN
    for i, (kv, g) in enumerate(planes):
        acc = None
        for j in range(CMP_STRIDE):
            rows = [xt_ref[p, i, j * CHUNKS_PER_PAGE:(j + 1) * CHUNKS_PER_PAGE, :] for p in range(PAGES_PER_STEP)]
            y = _bdot(jnp.concatenate(rows, axis=0), w_ref[j, kv])
            acc = y if acc is None else acc + y
        o_ref[0, :, i * width:(i + 1) * width] = acc


def paged_chunk_hidden(cache, page_table, w_token):
    n, n_pages = page_table.shape
    planes = cache.transpose(0, 2, 3, 4, 1)
    rows = PAGES_PER_STEP * CHUNKS_PER_PAGE

    def page_spec(k):
        return pl.BlockSpec((1,) + planes.shape[1:], lambda b, s, pt: (pt[b, s * PAGES_PER_STEP + k], 0, 0, 0, 0))

    return pl.pallas_call(
        _paged_chunk_kernel,
        grid_spec=pltpu.PrefetchScalarGridSpec(
            num_scalar_prefetch=1,
            grid=(n, n_pages // PAGES_PER_STEP),
            in_specs=[page_spec(k) for k in range(PAGES_PER_STEP)]
                     + [pl.BlockSpec(w_token.shape, lambda b, s, pt: (0, 0, 0, 0))],
            out_specs=pl.BlockSpec((1, rows, HIDDEN_LANES), lambda b, s, pt: (b, s, 0)),
            scratch_shapes=[pltpu.VMEM((PAGES_PER_STEP, 2 * NSA_KV_GROUPS, PAGE_SIZE, NSA_HEAD_DIM), jnp.float32)],
        ),
        out_shape=jax.ShapeDtypeStruct((n, n_pages * CHUNKS_PER_PAGE, HIDDEN_LANES), jnp.float32),
        compiler_params=pltpu.CompilerParams(dimension_semantics=("parallel", "arbitrary"),
                                             vmem_limit_bytes=48 * 1024 * 1024),
        name="paged_chunk_hidden",
    )(page_table.astype(jnp.int32), *([planes] * PAGES_PER_STEP), w_token)


def _matmul_kernel(x_ref, w_ref, o_ref):
    o_ref[...] = jnp.dot(x_ref[...].astype(jnp.bfloat16), w_ref[...],
                         preferred_element_type=jnp.float32).astype(o_ref.dtype)


def matmul(x, w, rows_per_step=512, out_dtype=jnp.float32):
    m, k = x.shape
    nn = w.shape[1]
    tm = min(rows_per_step, m)
    return pl.pallas_call(
        _matmul_kernel,
        grid=(m // tm,),
        in_specs=[pl.BlockSpec((tm, k), lambda i: (i, 0)), pl.BlockSpec((k, nn), lambda i: (0, 0))],
        out_specs=pl.BlockSpec((tm, nn), lambda i: (i, 0)),
        out_shape=jax.ShapeDtypeStruct((m, nn), out_dtype),
        compiler_params=pltpu.CompilerParams(dimension_semantics=("parallel",),
                                             vmem_limit_bytes=48 * 1024 * 1024),
        name="matmul",
    )(x, w.astype(jnp.bfloat16))


def _split_bf16(x):
    hi = x.astype(jnp.bfloat16)
    return hi, (x - hi.astype(jnp.float32)).astype(jnp.bfloat16)


def _matmul3_kernel(x_ref, w_ref, o_ref):
    x_hi, x_lo = _split_bf16(x_ref[...])
    w_hi, w_lo = _split_bf16(w_ref[...])
    dot = functools.partial(jnp.dot, preferred_element_type=jnp.float32)
    o_ref[...] = dot(x_hi, w_hi) + (dot(x_hi, w_lo) + dot(x_lo, w_hi))


def matmul_precise(x, w, cols_per_step=512):
    m, k = x.shape
    nn = w.shape[1]
    pad = (-nn) % 128
    if pad:
        w = jnp.pad(w, ((0, 0), (0, pad)))
    tn = math.gcd(cols_per_step, nn + pad)
    tm = min(m, 512)
    out = pl.pallas_call(
        _matmul3_kernel,
        grid=(m // tm, (nn + pad) // tn),
        in_specs=[pl.BlockSpec((tm, k), lambda i, j: (i, 0)), pl.BlockSpec((k, tn), lambda i, j: (0, j))],
        out_specs=pl.BlockSpec((tm, tn), lambda i, j: (i, j)),
        out_shape=jax.ShapeDtypeStruct((m, nn + pad), jnp.float32),
        compiler_params=pltpu.CompilerParams(dimension_semantics=("parallel", "parallel")),
        name="matmul_precise",
    )(x, w)
    return out[:, :nn] if pad else out


def dense(x, w, precise=False):
    if precise:
        return matmul_precise(x.reshape(-1, x.shape[-1]), w).reshape(x.shape[:-1] + (w.shape[1],))
    return x @ w


def compress_from_hidden(hidden, pos_emb, w1, b1, w2):
    n, C = hidden.shape[:2]
    rows = -(-C // 8) * 8
    if rows != C:
        hidden = jnp.pad(hidden, ((0, 0), (0, rows - C), (0, 0)))
    w1r = w1.reshape(2, CMP_BLOCK, NSA_HEAD_DIM, CMP_HIDDEN)
    bias = jnp.einsum('ajd,ajdh->ah', pos_emb, w1r) + b1
    out = pl.pallas_call(
        functools.partial(_compress_finish_kernel, n_blocks=C),
        grid=(n,),
        in_specs=[pl.BlockSpec((1, rows, HIDDEN_LANES), lambda b: (b, 0, 0)),
                  pl.BlockSpec(bias.shape, lambda b: (0, 0)), pl.BlockSpec(w2.shape, lambda b: (0, 0, 0))],
        out_specs=pl.BlockSpec((1, rows, 2 * SLAB), lambda b: (b, 0, 0)),
        out_shape=jax.ShapeDtypeStruct((n, rows, 2 * SLAB), jnp.float32),
        compiler_params=pltpu.CompilerParams(dimension_semantics=("parallel",)),
        name="compress_finish",
    )(hidden, bias, w2.astype(jnp.bfloat16))[:, :C]
    shape = (n, C, NSA_KV_GROUPS, NSA_HEAD_DIM)
    return out[:, :, :SLAB].reshape(shape), out[:, :, SLAB:].reshape(shape)


def _compress_finish_kernel(h_ref, bias_ref, w2_ref, o_ref, *, n_blocks):
    h = h_ref[0]
    rows = h.shape[0]
    keep = lax.broadcasted_iota(jnp.int32, (rows, CMP_HIDDEN), 0) < n_blocks - 1
    outs = []
    for i in range(2 * NSA_KV_GROUPS):
        kv = i // NSA_KV_GROUPS
        first = h[:, 2 * i * CMP_HIDDEN:(2 * i + 1) * CMP_HIDDEN]
        second = h[:, (2 * i + 1) * CMP_HIDDEN:(2 * i + 2) * CMP_HIDDEN]
        nxt = jnp.where(keep, pltpu.roll(second, rows - 1, 0), 0.0)
        act = jax.nn.gelu(first + nxt + bias_ref[kv:kv + 1, :])
        outs.append(_bdot(act, w2_ref[kv]))
    o_ref[0] = jnp.concatenate(outs, axis=-1)


def rows_chunk_hidden(kv_rows, w_chunk):
    n, T = kv_rows.shape[:2]
    chunks = kv_rows.reshape(n * T // CMP_STRIDE, CHUNK_LANES)
    return matmul(chunks, w_chunk, rows_per_step=256).reshape(n, T // CMP_STRIDE, HIDDEN_LANES)


def nsa_query(h, w_in, precise=False):
    n, T, _ = h.shape
    proj = dense(h, w_in, precise)
    q = proj[..., :NSA_HEADS * NSA_HEAD_DIM].reshape(n, T, NSA_KV_GROUPS, NSA_GROUP_HEADS, NSA_HEAD_DIM)
    gate = proj[..., NSA_HEADS * NSA_HEAD_DIM:].reshape(n, T, NSA_KV_GROUPS, NSA_GROUP_HEADS, N_BRANCH)
    return q, gate


def compressed_branch_and_selection(q, t_pos, kc, vc):
    scale = NSA_HEAD_DIM ** -0.5
    n_c = kc.shape[1]
    vis = (jnp.arange(n_c) * CMP_STRIDE + CMP_BLOCK - 1)[None, :] <= t_pos[:, None]
    p_cmp = masked_softmax(jnp.einsum('nqghd,ncgd->nghqc', q, kc) * scale, vis)
    o_cmp = jnp.einsum('nghqc,ncgd->nqghd', p_cmp.astype(vc.dtype), vc)
    ratio = SLC_BLOCK // CMP_STRIDE
    lead = CMP_BLOCK // CMP_STRIDE - 1
    n_s = n_c // ratio
    pg = jnp.pad(jnp.sum(p_cmp, axis=2), ((0, 0), (0, 0), (0, 0), (lead, 0)))
    p_slc = pg[..., 0:ratio * n_s:ratio]
    for o in range(1, ratio + lead):
        p_slc = p_slc + pg[..., o:o + ratio * n_s:ratio]
    jb = jnp.arange(n_s)[None, :]
    jt = (t_pos // SLC_BLOCK)[:, None]
    forced = (jb == 0) | (jb == jt) | (jb == jt - 1)
    score = jnp.where(jb > jt, -jnp.inf, jnp.where(forced, FORCED_SCORE, p_slc))
    before = (score[..., :, None] > score[..., None, :]) | (
        (score[..., :, None] == score[..., None, :]) & (jb[0][:, None] < jb[0][None, :]))
    rank = jnp.sum(before, axis=-2)
    slots = jnp.arange(min(SLC_TOPN, n_s))
    idx = jnp.sum(jnp.where(rank[..., None, :] == slots[:, None], jb[0], 0), axis=-1)
    return o_cmp, idx


def _dot3(x, y, dims):
    x_hi, x_lo = _split_bf16(x)
    y_hi, y_lo = _split_bf16(y)
    dot = lambda a, b: lax.dot_general(a, b, dims, preferred_element_type=jnp.float32)
    return dot(x_hi, y_hi) + (dot(x_hi, y_lo) + dot(x_lo, y_hi))


def _decode_kernel(half_ref, new_ref, phys_ref, q_ref, rows_ref, win_ref, *refs):
    b = pl.program_id(0)
    n_sel = len(refs) - 1
    pages, o_ref = refs[:n_sel], refs[n_sel]
    k_per_group = n_sel // NSA_KV_GROUPS
    hg, dh = NSA_GROUP_HEADS, NSA_HEAD_DIM
    tok_half = lax.broadcasted_iota(jnp.int32, (hg, PAGE_SIZE), 1) // SLC_BLOCK
    rows = rows_ref[0]

    def attend(qg, scores, values, k_new, v_new, new_ok):
        s_new = jnp.where(new_ok, jnp.sum(qg * k_new, axis=-1, keepdims=True), MASKED)
        m = s_new
        for s in scores:
            m = jnp.maximum(m, jnp.max(s, axis=-1, keepdims=True))
        e_new = jnp.exp(s_new - m)
        total, out = e_new, e_new * v_new
        for s, v in zip(scores, values):
            e = jnp.exp(s - m)
            total = total + jnp.sum(e, axis=-1, keepdims=True)
            out = out + _dot3(e, v, _NT)
        return out / jnp.maximum(total, 1e-30)

    outs = []
    for g in range(NSA_KV_GROUPS):
        qg = q_ref[0, g * hg:(g + 1) * hg, :]
        lanes = slice(g * dh, (g + 1) * dh)
        scores, values = [], []
        any_new = jnp.int32(0)
        for k in range(k_per_group):
            j = g * k_per_group + k
            plane = pages[j][0]
            is_new = new_ref[b, j]
            ok = (tok_half == half_ref[b, j]) & (is_new == 0)
            scores.append(jnp.where(ok, _dot3(qg, plane[0, 0], _NN), MASKED))
            values.append(plane[1, 0])
            any_new = jnp.maximum(any_new, is_new)
        o_slc = attend(qg, scores, values, rows[0:1, lanes], rows[1:2, lanes], any_new > 0)
        o_win = attend(qg, [_dot3(qg, win_ref[0, 0, g], _NN)], [win_ref[0, 1, g]],
                       rows[2:3, lanes], rows[3:4, lanes], True)
        outs.append(jnp.concatenate([o_slc, o_win], axis=-1))
    o_ref[0] = jnp.concatenate(outs, axis=0)


def decode_attention(q_rot, idx, page_table, cache_slc_kv, cache_win_kv, slc_new, win_new):
    n = q_rot.shape[0]
    sub = PAGE_SIZE // SLC_BLOCK
    n_past_blk = PAST_LEN // SLC_BLOCK
    assert cache_win_kv.shape[1] <= WINDOW and cache_win_kv.shape[1] <= PAST_LEN
    k_sel = idx.shape[-1]
    flat = idx.reshape(n, NSA_KV_GROUPS * k_sel)
    past = jnp.minimum(flat, n_past_blk - 1)
    hit = (past // sub)[:, :, None] == jnp.arange(page_table.shape[1])[None, None, :]
    phys = jnp.sum(jnp.where(hit, page_table[:, None, :], 0), axis=-1).astype(jnp.int32)
    half = (past % sub).astype(jnp.int32)
    is_new = (flat >= n_past_blk).astype(jnp.int32)
    pool = cache_slc_kv.transpose(0, 2, 3, 4, 1)
    window = cache_win_kv.transpose(0, 2, 3, 4, 1)
    rows = jnp.concatenate([slc_new.reshape(n, 2, SLAB), win_new.reshape(n, 2, SLAB)], axis=1)

    def page_spec(j):
        g = j // k_sel
        return pl.BlockSpec((1, 2, 1, NSA_HEAD_DIM, PAGE_SIZE), lambda b, hf, nw, ph: (ph[b, j], 0, g, 0, 0))

    out = pl.pallas_call(
        _decode_kernel,
        grid_spec=pltpu.PrefetchScalarGridSpec(
            num_scalar_prefetch=3,
            grid=(n,),
            in_specs=[pl.BlockSpec((1,) + q_rot.shape[1:], lambda b, hf, nw, ph: (b, 0, 0)),
                      pl.BlockSpec((1,) + rows.shape[1:], lambda b, hf, nw, ph: (b, 0, 0)),
                      pl.BlockSpec((1,) + window.shape[1:], lambda b, hf, nw, ph: (b, 0, 0, 0, 0))]
                     + [page_spec(j) for j in range(NSA_KV_GROUPS * k_sel)],
            out_specs=pl.BlockSpec((1, NSA_HEADS, 2 * NSA_HEAD_DIM), lambda b, hf, nw, ph: (b, 0, 0)),
        ),
        out_shape=jax.ShapeDtypeStruct((n, NSA_HEADS, 2 * NSA_HEAD_DIM), jnp.float32),
        compiler_params=pltpu.CompilerParams(dimension_semantics=("parallel",)),
        name="decode_attention",
    )(half, is_new, phys, q_rot, rows, window, *([pool] * (NSA_KV_GROUPS * k_sel)))
    return out[:, :, :NSA_HEAD_DIM], out[:, :, NSA_HEAD_DIM:]


NSA_SLABS = NSA_GROUP_HEADS
NSA_TILES = NSA_SLABS * NSA_KV_GROUPS
SELECTED_KEY_BLOCK = 256
DENSE_BATCH = 16
MASKED = -1e30
SLC_PER_CMP = SLC_BLOCK // CMP_STRIDE
CMP_LEAD = CMP_BLOCK // CMP_STRIDE - 1
SLAB = NSA_KV_GROUPS * NSA_HEAD_DIM
SEL_LANES = 64


def _group_tiles(q_ref, qs_ref):
    tq = q_ref.shape[1]
    lane = lax.broadcasted_iota(jnp.int32, (tq, SLAB), 1)
    for i in range(NSA_SLABS):
        qs = q_ref[0, :, i * 128:(i + 1) * 128]
        for g in range(NSA_KV_GROUPS):
            in_group = (lane >= g * NSA_HEAD_DIM) & (lane < (g + 1) * NSA_HEAD_DIM)
            qs_ref[2 * i + g] = jnp.where(in_group, qs, jnp.zeros_like(qs)).astype(jnp.bfloat16)


def _merge_groups(o_ref, tiles, gate_ref, branch):
    tq = tiles[0].shape[0]
    lane = lax.broadcasted_iota(jnp.int32, (tq, SLAB), 1)
    gates = gate_ref[0]
    col = lambda g, i: (g * NSA_GROUP_HEADS + i) * N_BRANCH + branch
    for i in range(NSA_SLABS):
        lo = tiles[2 * i] * gates[:, col(0, i):col(0, i) + 1]
        hi = tiles[2 * i + 1] * gates[:, col(1, i):col(1, i) + 1]
        o_ref[0, :, i * 128:(i + 1) * 128] = jnp.where(lane < NSA_HEAD_DIM, lo, hi)


def _nsa_cmp_kernel(q_ref, kc_ref, vc_ref, gate_ref, o_ref, sel_ref, qs_ref):
    qi = pl.program_id(1)
    tq = q_ref.shape[1]
    nc = kc_ref.shape[1]
    ns = nc // SLC_PER_CMP
    t0 = qi * tq
    _group_tiles(q_ref, qs_ref)
    kc = kc_ref[0]
    vc = vc_ref[0]
    t_row = t0 + lax.broadcasted_iota(jnp.int32, (tq, nc), 0)
    c_pos = lax.broadcasted_iota(jnp.int32, (tq, nc), 1) * CMP_STRIDE + (CMP_BLOCK - 1)
    vis = c_pos <= t_row
    tiles = range(NSA_TILES)
    s = [_mm(qs_ref[r], kc, _NT) for r in tiles]
    p = []
    for r in tiles:
        sr = jnp.where(vis, s[r], MASKED)
        m = jnp.max(sr, axis=-1, keepdims=True)
        e = jnp.where(vis, jnp.exp(sr - m), 0.0)
        p.append(e / jnp.maximum(jnp.sum(e, axis=-1, keepdims=True), 1e-30))
    _merge_groups(o_ref, [_mm(p[r], vc, _NN) for r in tiles], gate_ref, 0)

    jrow = lax.broadcasted_iota(jnp.int32, (ns, nc), 0)
    ccol = lax.broadcasted_iota(jnp.int32, (ns, nc), 1)
    pool = ((ccol >= SLC_PER_CMP * jrow - CMP_LEAD) & (ccol < SLC_PER_CMP * (jrow + 1))).astype(jnp.bfloat16)
    jb = lax.broadcasted_iota(jnp.int32, (ns, tq), 0)
    jt = (t0 + lax.broadcasted_iota(jnp.int32, (ns, tq), 1)) // SLC_BLOCK
    forced = (jb == 0) | (jb == jt) | (jb == jt - 1)
    sel_t = []
    for g in range(NSA_KV_GROUPS):
        pg = p[g]
        for i in range(1, NSA_SLABS):
            pg = pg + p[2 * i + g]
        pg_hi = pg.astype(jnp.bfloat16)
        pg_lo = (pg - pg_hi.astype(jnp.float32)).astype(jnp.bfloat16)
        p_slc = (lax.dot_general(pool, pg_hi, _NT, preferred_element_type=jnp.float32)
                 + lax.dot_general(pool, pg_lo, _NT, preferred_element_type=jnp.float32))
        score = jnp.where(jb > jt, -jnp.inf, jnp.where(forced, FORCED_SCORE, p_slc))
        rank = jnp.zeros((ns, tq), jnp.float32)
        for i in range(ns):
            row = score[i:i + 1, :]
            tie = jnp.where(jb > i, 1.0, 0.0)
            rank = rank + jnp.where(row > score, 1.0, 0.0) + jnp.where(row == score, tie, 0.0)
        sel_t.append(jnp.where(rank < min(SLC_TOPN, ns), 1.0, 0.0))
        if ns < SEL_LANES:
            sel_t.append(jnp.zeros((SEL_LANES - ns, tq), jnp.float32))
    sel_ref[0] = jnp.concatenate(sel_t, axis=0).T.astype(sel_ref.dtype)


def _nsa_dense_kernel(q_ref, k_ref, vt_ref, sel_ref, gate_ref, o_ref, qs_ref, m_ref, acc_ref, *, windowed, kblock):
    qi = pl.program_id(1)
    tq = q_ref.shape[1]
    t0 = qi * tq
    _group_tiles(q_ref, qs_ref)
    m_ref[...] = jnp.full(m_ref.shape, MASKED, jnp.float32)
    acc_ref[...] = jnp.zeros(acc_ref.shape, jnp.float32)
    k_row = lax.broadcasted_iota(jnp.int32, (kblock, tq), 0)
    t_lane = t0 + lax.broadcasted_iota(jnp.int32, (kblock, tq), 1)
    tiles = range(NSA_TILES)
    first = 0
    last = 0 if windowed else (t0 + tq - 1) // kblock

    def key_block(kb, carry):
        start = pl.multiple_of(jnp.maximum(t0 - WINDOW, 0) if windowed else kb * kblock, tq)
        kblk = k_ref[0, pl.ds(start, kblock), :]
        vt = vt_ref[0, :, pl.ds(start, kblock)]
        k_pos = start + k_row
        if windowed:
            allowed = [(k_pos <= t_lane) & (k_pos >= t_lane - WINDOW)] * NSA_KV_GROUPS
        else:
            sel = sel_ref[0]
            n_sel = NSA_KV_GROUPS * SEL_LANES
            e_key = lax.broadcasted_iota(jnp.int32, (kblock, n_sel), 0)
            e_lane = lax.broadcasted_iota(jnp.int32, (kblock, n_sel), 1)
            blk = kb * (kblock // SLC_BLOCK) + e_key // SLC_BLOCK
            allowed = []
            for g in range(NSA_KV_GROUPS):
                expand = (e_lane == blk + g * SEL_LANES).astype(jnp.bfloat16)
                picked = lax.dot_general(expand, sel, _NT, preferred_element_type=jnp.float32)
                allowed.append((picked > 0.5) & (k_pos <= t_lane))
        bias = [jnp.where(ok, 0.0, MASKED) for ok in allowed]
        d_row = lax.broadcasted_iota(jnp.int32, vt.shape, 0)
        vt_g = [jnp.where((d_row >= g * NSA_HEAD_DIM) & (d_row < (g + 1) * NSA_HEAD_DIM), vt, jnp.ones_like(vt))
                for g in range(NSA_KV_GROUPS)]
        s = [_mm(kblk, qs_ref[r], _NT) for r in tiles]
        for lo in range(0, NSA_TILES, DENSE_BATCH):
            batch = range(lo, lo + DENSE_BATCH)
            p, alpha = {}, {}
            for r in batch:
                sr = s[r] + bias[r % NSA_KV_GROUPS]
                m_prev = m_ref[r]
                m_new = jnp.maximum(m_prev, jnp.max(sr, axis=0, keepdims=True))
                alpha[r] = jnp.exp(m_prev - m_new)
                p[r] = jnp.exp(sr - m_new).astype(jnp.bfloat16)
                m_ref[r] = m_new
            pv = {r: jnp.dot(vt_g[r % NSA_KV_GROUPS], p[r], preferred_element_type=jnp.float32) for r in batch}
            for r in batch:
                acc_ref[r] = alpha[r] * acc_ref[r] + pv[r]
        return carry

    lax.fori_loop(first, last + 1, key_block, 0)
    out = []
    for r in tiles:
        acc = acc_ref[r]
        sum_row = (1 - r % NSA_KV_GROUPS) * NSA_HEAD_DIM
        out.append((acc / jnp.maximum(acc[sum_row:sum_row + 1, :], 1e-30)).T)
    _merge_groups(o_ref, out, gate_ref, 2 if windowed else 1)


def nsa_prompt_attention(q, q_rot, kc, vc, kv_att, gates):
    n, T, D = q.shape
    tq = Q_BLOCK
    qspec = pl.BlockSpec((1, tq, D), lambda b, i: (b, i, 0))
    whole = lambda a: pl.BlockSpec((1,) + a.shape[1:], lambda b, i: (b, 0, 0))
    lanes = lambda j: pl.BlockSpec((1, T, SLAB), lambda b, i: (b, 0, j))
    n_sel = NSA_KV_GROUPS * SEL_LANES
    sel_spec = pl.BlockSpec((1, tq, n_sel), lambda b, i: (b, i, 0))
    gate_spec = pl.BlockSpec((1, tq, gates.shape[2]), lambda b, i: (b, i, 0))
    params = pltpu.CompilerParams(dimension_semantics=("parallel", "arbitrary"),
                                  vmem_limit_bytes=48 * 1024 * 1024)
    qs_scratch = pltpu.VMEM((NSA_TILES, tq, SLAB), jnp.bfloat16)
    o_cmp, sel = pl.pallas_call(
        _nsa_cmp_kernel,
        grid=(n, T // tq),
        in_specs=[qspec, whole(kc), whole(vc), gate_spec],
        out_specs=[qspec, sel_spec],
        out_shape=[jax.ShapeDtypeStruct((n, T, D), jnp.float32),
                   jax.ShapeDtypeStruct((n, T, n_sel), jnp.bfloat16)],
        scratch_shapes=[qs_scratch],
        compiler_params=params,
        name="nsa_cmp_select",
    )(q, kc, vc, gates)
    stat = pltpu.VMEM((NSA_TILES, 1, tq), jnp.float32)
    acc = pltpu.VMEM((NSA_TILES, SLAB, tq), jnp.float32)
    v_t = jnp.swapaxes(jnp.concatenate([kv_att[:, :, SLAB:2 * SLAB], kv_att[:, :, 3 * SLAB:]], axis=-1), 1, 2)
    rows = lambda j: pl.BlockSpec((1, SLAB, T), lambda b, i: (b, j, 0))

    def dense(windowed, branch, name):
        kblock = min(WINDOW + tq, T) if windowed else min(SELECTED_KEY_BLOCK, T)
        return pl.pallas_call(
            functools.partial(_nsa_dense_kernel, windowed=windowed, kblock=kblock),
            grid=(n, T // tq),
            in_specs=[qspec, lanes(2 * branch), rows(branch), sel_spec, gate_spec],
            out_specs=qspec,
            out_shape=jax.ShapeDtypeStruct((n, T, D), jnp.float32),
            scratch_shapes=[qs_scratch, stat, acc],
            compiler_params=params,
            name=name,
        )(q_rot, kv_att, v_t, sel, gates)

    return o_cmp, dense(False, 0, "nsa_selected"), dense(True, 1, "nsa_window")


def _rope_tables(pos):
    half = ROT_DIM // 2
    inv = ROPE_THETA ** (-2.0 * jnp.arange(half, dtype=jnp.float32) / ROT_DIM)
    ang = pos.astype(jnp.float32)[:, None] * inv[None, :]
    rest = NSA_HEAD_DIM - ROT_DIM
    cos = jnp.concatenate([jnp.cos(ang), jnp.cos(ang), jnp.ones((pos.shape[0], rest), jnp.float32)], axis=1)
    sin = jnp.concatenate([-jnp.sin(ang), jnp.sin(ang), jnp.zeros((pos.shape[0], rest), jnp.float32)], axis=1)
    return jnp.tile(cos, (1, NSA_KV_GROUPS)), jnp.tile(sin, (1, NSA_KV_GROUPS))


def _nsa_pre_kernel(x_ref, gkv_ref, gmix_ref, cos_ref, sin_ref, wkv_ref, wq_ref, wg_ref,
                    cmp_ref, slc_ref, win_ref, kvb_ref, q_ref, qr_ref, gate_ref):
    xh = _rms(x_ref[0])
    cos, sin = cos_ref[...], sin_ref[...]
    low = lax.broadcasted_iota(jnp.int32, cos.shape, 1) % NSA_HEAD_DIM < ROT_DIM // 2

    def rope(t):
        swapped = jnp.where(low, pltpu.roll(t, SLAB - ROT_DIM // 2, 1), pltpu.roll(t, ROT_DIM // 2, 1))
        return t * cos + swapped * sin

    kv = _bdot(xh * gkv_ref[...], wkv_ref[...])
    part = lambda j: kv[:, j * SLAB:(j + 1) * SLAB]
    k_slc, k_win = rope(part(2)), rope(part(4))
    cmp_ref[0] = kv[:, :2 * SLAB]
    slc_ref[0] = jnp.concatenate([k_slc, part(3)], axis=-1)
    win_ref[0] = jnp.concatenate([k_win, part(5)], axis=-1)
    kvb_ref[0] = jnp.concatenate([k_slc, part(3), k_win, part(5)], axis=-1).astype(jnp.bfloat16)
    h = xh * gmix_ref[...]
    q = _bdot(h, wq_ref[...])
    q_ref[0] = q.astype(jnp.bfloat16)
    qr_ref[0] = jnp.concatenate([rope(q[:, i * SLAB:(i + 1) * SLAB]) for i in range(NSA_SLABS)],
                                axis=-1).astype(jnp.bfloat16)
    gate_ref[0] = jax.nn.sigmoid(_bdot(h, wg_ref[...]))


def nsa_pre(x, pos, norm_kv, norm_mix, w_kv, w_in):
    n, T, D = x.shape
    tm = min(ROW_TILE, T)
    nq = NSA_HEADS * NSA_HEAD_DIM
    cos, sin = _rope_tables(pos)
    w_q = (_to_slabs(w_in[:, :nq]) * NSA_HEAD_DIM ** -0.5).astype(jnp.bfloat16)
    w_g = jnp.pad(w_in[:, nq:], ((0, 0), (0, SLAB - (w_in.shape[1] - nq)))).astype(jnp.bfloat16)
    seq = lambda width: pl.BlockSpec((1, tm, width), lambda b, t: (b, t, 0))
    full = lambda a: pl.BlockSpec(a.shape, lambda b, t: (0,) * a.ndim)
    table = pl.BlockSpec((tm, SLAB), lambda b, t: (t, 0))
    consts = [norm_kv.reshape(1, D), norm_mix.reshape(1, D)]
    weights = [w_kv.astype(jnp.bfloat16), w_q, w_g]
    widths = [2 * SLAB, 2 * SLAB, 2 * SLAB, 4 * SLAB, nq, nq, SLAB]
    dtypes = [jnp.float32] * 3 + [jnp.bfloat16] * 3 + [jnp.float32]
    return pl.pallas_call(
        _nsa_pre_kernel,
        grid=(n, T // tm),
        in_specs=[seq(D)] + [full(a) for a in consts] + [table, table] + [full(a) for a in weights],
        out_specs=[seq(w) for w in widths],
        out_shape=[jax.ShapeDtypeStruct((n, T, w), dt) for w, dt in zip(widths, dtypes)],
        compiler_params=pltpu.CompilerParams(dimension_semantics=("parallel", "parallel"),
                                             vmem_limit_bytes=48 * 1024 * 1024),
        name="nsa_pre",
    )(x, *consts, cos, sin, *weights)


def _to_slabs(x):
    lead = x.shape[:-1]
    x = x.reshape(lead + (NSA_KV_GROUPS, NSA_GROUP_HEADS, NSA_HEAD_DIM))
    return jnp.swapaxes(x, -3, -2).reshape(lead + (NSA_HEADS * NSA_HEAD_DIM,))


def nsa_layer_prompt(x, norm_kv, norm_mix, w_kv, cmp_pos, cmp_w1, cmp_b1, cmp_w2, w_in, w_o, norm_next, w_router):
    n, T, D = x.shape
    cmp_kv, slc_kv, win_kv, kv_att, q, q_rot, gates = nsa_pre(x, jnp.arange(T), norm_kv, norm_mix, w_kv, w_in)
    hidden = rows_chunk_hidden(cmp_kv, _chunk_weights(cmp_w1))
    kc, vc = compress_from_hidden(hidden, cmp_pos, cmp_w1, cmp_b1, cmp_w2)
    lanes = lambda a: a.reshape(n, a.shape[1], SLAB).astype(jnp.bfloat16)
    branches = nsa_prompt_attention(q, q_rot, lanes(kc), lanes(vc), kv_att, gates)
    flat = lambda a: a.reshape(n * T, a.shape[-1])
    x_out, h_out, logits = out_proj([flat(o) for o in branches], None, flat(x), _to_slabs(w_o.T).T, norm_next,
                                    w_router)
    return x_out, h_out, logits, cmp_kv, slc_kv, win_kv


def sample_kv_context(cmp_new, slc_new, win_new, cache_cmp_kv, cache_slc_kv, cache_win_kv, page_table,
                      cmp_pos, cmp_w1, cmp_b1, cmp_w2):
    n, S = cmp_new.shape[:2]
    n_new_blk = -(-S // SLC_BLOCK)
    pad = ((0, 0), (0, n_new_blk * SLC_BLOCK - S), (0, 0), (0, 0), (0, 0))
    w_chunk = _chunk_weights(cmp_w1)
    hidden = jnp.concatenate([paged_chunk_hidden(cache_cmp_kv, page_table, _token_weights(cmp_w1)),
                              rows_chunk_hidden(jnp.pad(cmp_new.astype(cache_cmp_kv.dtype), pad), w_chunk)], axis=1)
    kc, vc = compress_from_hidden(hidden, cmp_pos, cmp_w1, cmp_b1, cmp_w2)
    win_all = jnp.concatenate([cache_win_kv, win_new.astype(cache_win_kv.dtype)], axis=1)
    return kc, vc, win_all


def sample_nsa(h, kc, vc, slc_new, win_new, cache_slc_kv, cache_win_kv, page_table, w_in, w_o):
    n, S, _ = h.shape
    assert S == 1
    q, gate = nsa_query(h, w_in, precise=True)
    t_pos = PAST_LEN + jnp.arange(S)
    with jax.default_matmul_precision("highest"):
        o_cmp, idx = compressed_branch_and_selection(q, t_pos, kc, vc)
    q_rot = (rope_partial(q, t_pos) * NSA_HEAD_DIM ** -0.5).reshape(n, NSA_HEADS, NSA_HEAD_DIM)
    o_slc, o_win = decode_attention(q_rot, idx[:, :, 0], page_table, cache_slc_kv, cache_win_kv, slc_new, win_new)
    g = jax.nn.sigmoid(gate.astype(jnp.float32)).astype(q.dtype)
    o = g[..., 0:1] * o_cmp + g[..., 1:2] * o_slc.reshape(q.shape) + g[..., 2:3] * o_win.reshape(q.shape)
    return dense(o.reshape(n, S, NSA_HEADS * NSA_HEAD_DIM), w_o, precise=True)


def kernel(x_prompt, x_sample, state_wkv, state_shift, cache_cmp_kv, cache_slc_kv, cache_win_kv, page_table, norm_mix, norm_ffn, norm_kv, norm_final, rw_mu, rw_w_rkv, rw_w0, rw_w1, rw_w2, rw_a0, rw_a1, rw_a2, rw_g1, rw_g2, rw_k_k, rw_k_a, rw_r_k, rw_gn_w, rw_gn_b, rw_w_o, nsa_w_kv, nsa_cmp_pos, nsa_cmp_w1, nsa_cmp_b1, nsa_cmp_w2, nsa_w_in, nsa_w_o, ffn_w_gu, ffn_w_down, moe_router, moe_w_gu, moe_w_down):
    cmp_params = (nsa_cmp_pos, nsa_cmp_w1, nsa_cmp_b1, nsa_cmp_w2)
    assert DEPTH == 2 and N_A_LAYERS == 1
    D = D_MODEL

    n_p, T = x_prompt.shape[:2]
    x1, h1, wkv_fin, h_last = rwkv7_layer(
        x_prompt, jnp.zeros((n_p, D), x_prompt.dtype), jnp.zeros((n_p, RWKV_HEADS, RWKV_HEAD, RWKV_HEAD), jnp.float32),
        norm_mix[0], rw_mu[0], rw_w_rkv[0], rw_w0[0], rw_w1[0], rw_w2[0], rw_a0[0], rw_a1[0], rw_a2[0],
        rw_g1[0], rw_g2[0], rw_k_k[0], rw_k_a[0], rw_r_k[0], rw_gn_w[0], rw_gn_b[0], rw_w_o[0], norm_ffn[0])
    wkv_p, shift_p = wkv_fin[None], h_last[None]
    x2 = swiglu_residual(x1, h1, ffn_w_gu[0], ffn_w_down[0])
    x_p, h_p, logits_p, cmp_rows, slc_rows, win_rows = nsa_layer_prompt(
        x2.reshape(n_p, T, D), norm_kv, norm_mix[1], nsa_w_kv, *cmp_params, nsa_w_in[0], nsa_w_o[0], norm_ffn[1],
        moe_router[0])
    kv_shape = (n_p, T, 2, NSA_KV_GROUPS, NSA_HEAD_DIM)
    cmp_kv_p, slc_kv_p = cmp_rows.reshape(kv_shape), slc_rows.reshape(kv_shape)
    win_kv_p = win_rows.reshape(kv_shape)[:, -min(WINDOW, T):]

    pos_s = PAST_LEN + jnp.arange(x_sample.shape[1], dtype=jnp.int32)
    h = rmsnorm(x_sample, norm_mix[0])
    y, s_fin, h_last_s = rwkv7_time_mix(
        h, state_shift[0], state_wkv[0], rw_mu[0], rw_w_rkv[0], rw_w0[0], rw_w1[0], rw_w2[0],
        rw_a0[0], rw_a1[0], rw_a2[0], rw_g1[0], rw_g2[0], rw_k_k[0], rw_k_a[0],
        rw_r_k[0], rw_gn_w[0], rw_gn_b[0], rw_w_o[0], precise=True)
    wkv_s, shift_s = s_fin[None], h_last_s[None]
    x_s = x_sample + y
    x_s = x_s + swiglu(rmsnorm(x_s, norm_ffn[0]), ffn_w_gu[0], ffn_w_down[0], precise=True)
    cmp_kv_s, slc_kv_s, win_new = shared_kv_rows(x_s, norm_kv, nsa_w_kv, pos_s, precise=True)
    kc_s, vc_s, win_all = sample_kv_context(cmp_kv_s, slc_kv_s, win_new, cache_cmp_kv, cache_slc_kv, cache_win_kv,
                                            page_table, *cmp_params)
    x_s = x_s + sample_nsa(rmsnorm(x_s, norm_mix[1]), kc_s, vc_s, slc_kv_s, win_new, cache_slc_kv, cache_win_kv,
                           page_table, nsa_w_in[0], nsa_w_o[0])
    win_kv_s = win_all[:, -cache_win_kv.shape[1]:]
    x_s = x_s.reshape(-1, D)
    h_s = rmsnorm(x_s, norm_ffn[1])
    logits_s = matmul_precise(h_s, moe_router[0])

    n_tok_p = n_p * T
    logits = jnp.concatenate([logits_p[:, :N_EXPERTS], logits_s], axis=0)
    y0, y1, gate = moe_swiglu(jnp.concatenate([h_p, h_s.astype(jnp.bfloat16)], axis=0), logits,
                              moe_w_gu[0], moe_w_down[0])
    gate = jnp.pad(gate, ((0, 0), (0, 128 - TOP_K)))
    y_prompt = combine_norm(x_p, y0[:n_tok_p], y1[:n_tok_p], gate[:n_tok_p], norm_final).reshape(x_prompt.shape)
    y_sample = combine_norm(x_s, y0[n_tok_p:], y1[n_tok_p:], gate[n_tok_p:], norm_final).reshape(x_sample.shape)


    return (y_prompt, y_sample, wkv_p, shift_p, cmp_kv_p, slc_kv_p, win_kv_p,
            wkv_s, shift_s, cmp_kv_s, slc_kv_s, win_kv_s)
```

```python
import functools
import math

import jax
import jax.numpy as jnp
from jax import lax
from jax.experimental import pallas as pl
from jax.experimental.pallas import tpu as pltpu

D_MODEL = 1024
DEPTH = 2
PAST_LEN = 16384
PAGE_SIZE = 128
N_A_LAYERS = DEPTH // 2
RWKV_HEAD = 64
RWKV_HEADS = D_MODEL // RWKV_HEAD
DECAY_SCALE = math.exp(-0.5)
GN_EPS = RWKV_HEAD * 1e-5
NSA_HEADS = 16
NSA_HEAD_DIM = 64
NSA_KV_GROUPS = 2
NSA_GROUP_HEADS = NSA_HEADS // NSA_KV_GROUPS
N_BRANCH = 3
CMP_BLOCK = 32
CMP_STRIDE = 16
CMP_HIDDEN = 128
SLC_BLOCK = 64
SLC_TOPN = 16
WINDOW = 512
Q_BLOCK = 128
FORCED_SCORE = 1e4
ROPE_THETA = 500000.0
ROT_DIM = NSA_HEAD_DIM // 4
N_EXPERTS = 8
TOP_K = 2
NORM_EPS = 1e-6


def _rmsnorm_kernel(x_ref, g_ref, o_ref):
    x = x_ref[...]
    y = x * lax.rsqrt(jnp.mean(x * x, axis=-1, keepdims=True) + NORM_EPS)
    o_ref[...] = y * g_ref[...]


def rmsnorm(x, g):
    shp = x.shape
    x2 = x.reshape(-1, shp[-1])
    rows = x2.shape[0]
    tm = min(rows, 512)
    out = pl.pallas_call(
        _rmsnorm_kernel,
        grid=(rows // tm,),
        in_specs=[pl.BlockSpec((tm, shp[-1]), lambda i: (i, 0)),
                  pl.BlockSpec((1, shp[-1]), lambda i: (0, 0))],
        out_specs=pl.BlockSpec((tm, shp[-1]), lambda i: (i, 0)),
        out_shape=jax.ShapeDtypeStruct(x2.shape, x.dtype),
        name="rmsnorm",
    )(x2, g.reshape(1, -1))
    return out.reshape(shp)


def rope_partial(x, pos):
    half = ROT_DIM // 2
    inv = ROPE_THETA ** (-2.0 * jnp.arange(half, dtype=jnp.float32) / ROT_DIM)
    ang = pos.astype(jnp.float32)[:, None] * inv[None, :]
    shape = (1, pos.shape[0]) + (1,) * (x.ndim - 3) + (half,)
    cos = jnp.cos(ang).reshape(shape)
    sin = jnp.sin(ang).reshape(shape)
    xf = x.astype(jnp.float32)
    x1, x2 = xf[..., :half], xf[..., half:ROT_DIM]
    out = jnp.concatenate([x1 * cos - x2 * sin, x2 * cos + x1 * sin, xf[..., ROT_DIM:]], axis=-1)
    return out.astype(x.dtype)


def masked_softmax(s, mask):
    s = jnp.where(mask, s.astype(jnp.float32), -jnp.inf)
    m = jnp.max(s, axis=-1, keepdims=True)
    m = jnp.where(jnp.isfinite(m), m, 0.0)
    e = jnp.where(mask, jnp.exp(s - m), 0.0)
    return e / jnp.maximum(jnp.sum(e, axis=-1, keepdims=True), 1e-30)


FF_CHUNK = 1408
SWIGLU_ROWS = 512


def _swiglu_kernel(blk_e_ref, n_used_ref, x_ref, wg_ref, wu_ref, wd_ref, *rest):
    o_ref = rest[-1]
    i = pl.program_id(0)
    f = pl.program_id(1)

    @pl.when(i < n_used_ref[0])
    def _():
        x = x_ref[...].astype(jnp.bfloat16)
        g = jnp.dot(x, wg_ref[0], preferred_element_type=jnp.float32)
        u = jnp.dot(x, wu_ref[0], preferred_element_type=jnp.float32)
        act = (g * jax.nn.sigmoid(g) * u).astype(jnp.bfloat16)
        y = jnp.dot(act, wd_ref[0], preferred_element_type=jnp.float32)

        @pl.when(f == 0)
        def _():
            o_ref[...] = y + rest[0][...] if len(rest) == 2 else y

        @pl.when(f > 0)
        def _():
            o_ref[...] += y

    @pl.when(i >= n_used_ref[0])
    def _():
        o_ref[...] = jnp.zeros(o_ref.shape, o_ref.dtype)


def grouped_swiglu(xb, blk_e, n_used, w_gu, w_down, res=None):
    rows, d = xb.shape
    extra = [] if res is None else [res]
    b = min(SWIGLU_ROWS, rows)
    n_blk = rows // b
    ff = w_down.shape[1]
    tf = FF_CHUNK
    n_f = ff // tf
    chunk = lambda i, f, be, nu: jnp.where(i < nu[0], f, n_f - 1)
    return pl.pallas_call(
        _swiglu_kernel,
        grid_spec=pltpu.PrefetchScalarGridSpec(
            num_scalar_prefetch=2,
            grid=(n_blk, n_f),
            in_specs=[
                pl.BlockSpec((b, d), lambda i, f, be, nu: (i, 0)),
                pl.BlockSpec((1, d, tf), lambda i, f, be, nu: (be[i], 0, chunk(i, f, be, nu))),
                pl.BlockSpec((1, d, tf), lambda i, f, be, nu: (be[i], 0, n_f + chunk(i, f, be, nu))),
                pl.BlockSpec((1, tf, d), lambda i, f, be, nu: (be[i], chunk(i, f, be, nu), 0)),
            ] + [pl.BlockSpec((b, d), lambda i, f, be, nu: (i, 0))] * len(extra),
            out_specs=pl.BlockSpec((b, d), lambda i, f, be, nu: (i, 0)),
        ),
        out_shape=jax.ShapeDtypeStruct((rows, d), jnp.float32),
        compiler_params=pltpu.CompilerParams(dimension_semantics=("arbitrary", "arbitrary"),
                                             vmem_limit_bytes=56 * 1024 * 1024),
        name="grouped_swiglu",
    )(blk_e.astype(jnp.int32), jnp.reshape(n_used, (1,)).astype(jnp.int32), xb, w_gu, w_gu, w_down, *extra)


def swiglu(h, w_gu, w_down, precise=False):
    g, u = jnp.split(dense(h, w_gu, precise), 2, axis=-1)
    return dense(jax.nn.silu(g) * u, w_down, precise)


def swiglu_residual(x, h, w_gu, w_down):
    n_blk = h.shape[0] // min(SWIGLU_ROWS, h.shape[0])
    return grouped_swiglu(h, jnp.zeros((n_blk,), jnp.int32), jnp.int32(n_blk),
                          w_gu.astype(jnp.bfloat16)[None], w_down.astype(jnp.bfloat16)[None], res=x)


def _combine_norm_kernel(x_ref, y0_ref, y1_ref, g_ref, gain_ref, o_ref):
    g = g_ref[...]
    x = x_ref[...] + (y0_ref[...] * g[:, 0:1] + y1_ref[...] * g[:, 1:2])
    o_ref[...] = _rms(x) * gain_ref[...]


def combine_norm(x, y0, y1, gate, gain):
    m, d = x.shape
    tm = min(2 * ROW_TILE, m)
    rows = lambda width: pl.BlockSpec((tm, width), lambda i: (i, 0))
    return pl.pallas_call(
        _combine_norm_kernel,
        grid=(m // tm,),
        in_specs=[rows(d), rows(d), rows(d), rows(gate.shape[1]), pl.BlockSpec((1, d), lambda i: (0, 0))],
        out_specs=rows(d),
        out_shape=jax.ShapeDtypeStruct((m, d), jnp.float32),
        compiler_params=pltpu.CompilerParams(dimension_semantics=("parallel",)),
        name="combine_norm",
    )(x, y0, y1, gate, gain.reshape(1, d))


def moe_swiglu(xt, logits, w_gu_e, w_down_e):
    n_tok, d = xt.shape
    b = SWIGLU_ROWS
    experts = jnp.arange(N_EXPERTS)[None, :]
    e0 = jnp.argmax(logits, axis=-1)
    v0 = jnp.max(logits, axis=-1)
    rest = jnp.where(experts == e0[:, None], -jnp.inf, logits)
    e1 = jnp.argmax(rest, axis=-1)
    v1 = jnp.max(rest, axis=-1)
    top_idx = jnp.stack([e0, e1], axis=-1).astype(jnp.int32)
    gate = jax.nn.softmax(jnp.stack([v0, v1], axis=-1), axis=-1)
    nk = n_tok * TOP_K
    flat_e = top_idx.reshape(nk)
    onehot = (flat_e[:, None] == jnp.arange(N_EXPERTS)[None, :]).astype(jnp.int32)
    before = jnp.cumsum(onehot, axis=0) - onehot
    counts = jnp.sum(onehot, axis=0)
    padded = (counts + b - 1) // b * b
    ends_pad = jnp.cumsum(padded)
    starts_pad = ends_pad - padded
    dest = jnp.sum(onehot * (starts_pad[None, :] + before), axis=1)
    n_rows = (nk + b - 1) // b * b + N_EXPERTS * b
    n_blk = n_rows // b
    flat_tok = jnp.repeat(jnp.arange(n_tok, dtype=jnp.int32), TOP_K)
    row_tok = jnp.full((n_rows,), n_tok, jnp.int32).at[dest].set(flat_tok)
    blk_e = jnp.minimum(jnp.searchsorted(ends_pad, jnp.arange(n_blk) * b, side='right'), N_EXPERTS - 1)
    x_pad = jnp.concatenate([xt.astype(jnp.float32), jnp.zeros((1, d), jnp.float32)], axis=0)
    yb = grouped_swiglu(x_pad[row_tok], blk_e, ends_pad[-1] // b,
                        w_gu_e.astype(jnp.bfloat16), w_down_e.astype(jnp.bfloat16))
    dest = dest.reshape(n_tok, TOP_K)
    return yb[dest[:, 0]], yb[dest[:, 1]], gate


WKV_CHUNK = 64
WKV_HEADS_PER_STEP = 16

_NN = (((1,), (0,)), ((), ()))
_NT = (((1,), (1,)), ((), ()))
_TN = (((0,), (0,)), ((), ()))


def _mm(x, y, dims):
    return lax.dot_general(x.astype(jnp.bfloat16), y.astype(jnp.bfloat16), dims,
                           preferred_element_type=jnp.float32)


def _wkv7_chunk_kernel(r_ref, lw_ref, k_ref, v_ref, kk_ref, a_ref, rk_ref, gnw_ref, gnb_ref, s0_ref,
                       y_ref, sout_ref, state_ref):
    c = pl.program_id(2)
    L = r_ref.shape[1]
    n_pairs = state_ref.shape[0]
    N = RWKV_HEAD
    W = 2 * N

    def block_diag(top, bottom):
        z = jnp.zeros((N, N), jnp.float32)
        return jnp.concatenate([jnp.concatenate([top, z], axis=1), jnp.concatenate([z, bottom], axis=1)], axis=0)

    @pl.when(c == 0)
    def _():
        for p in range(n_pairs):
            state_ref[p] = block_diag(s0_ref[0, 2 * p], s0_ref[0, 2 * p + 1])

    row2 = lax.broadcasted_iota(jnp.int32, (2 * L, W), 0)
    lane2 = lax.broadcasted_iota(jnp.int32, (2 * L, W), 1)
    own_lanes = (row2 // L) == (lane2 // N)
    rr = lax.broadcasted_iota(jnp.int32, (2 * L, 2 * L), 0)
    cc = lax.broadcasted_iota(jnp.int32, (2 * L, 2 * L), 1)
    same = (rr // L) == (cc // L)
    strict = same & (rr % L > cc % L)
    incl = same & (rr % L >= cc % L)
    wr = lax.broadcasted_iota(jnp.int32, (W, W), 0)
    wc = lax.broadcasted_iota(jnp.int32, (W, W), 1)
    eye_w = wr == wc
    ones_bd = ((wr // N) == (wc // N)).astype(jnp.bfloat16)
    tl = lax.broadcasted_iota(jnp.int32, (L, L), 0) >= lax.broadcasted_iota(jnp.int32, (L, L), 1)
    tri = tl.astype(jnp.bfloat16)

    def head_sum(x):
        hi, lo = _split_bf16(x)
        return (jnp.dot(hi, ones_bd, preferred_element_type=jnp.float32)
                + jnp.dot(lo, ones_bd, preferred_element_type=jnp.float32))

    def stack(x):
        return jnp.where(own_lanes, jnp.concatenate([x, x], axis=0), 0.0)

    unstack = lambda x: x[:L] + x[L:]

    lw = lw_ref[0]
    lw_hi, lw_lo = _split_bf16(lw)
    cum = (lax.dot_general(tri, lw_hi, _NN, preferred_element_type=jnp.float32)
           + lax.dot_general(tri, lw_lo, _NN, preferred_element_type=jnp.float32))
    cum_last = cum[L - 1:L, :]
    e_neg_all = jnp.exp(-cum)
    e_tail_all = jnp.exp(cum_last - cum)
    e_prev_all = jnp.exp(cum - lw)
    e_cum_all = jnp.exp(cum)
    wl_all = jnp.exp(cum_last)

    pairs = range(n_pairs)
    slab = lambda t, p: t[:, p * W:(p + 1) * W]
    r_in = [slab(r_ref[0], p) for p in pairs]
    k_in = [slab(k_ref[0], p) for p in pairs]
    v_in = [slab(v_ref[0], p) for p in pairs]
    ssq = [head_sum(slab(kk_ref[0], p) * slab(kk_ref[0], p)) for p in pairs]
    at_st, bt_st, bh_st, rt_st, kt_st, kh_st, v_st = [], [], [], [], [], [], []
    for p in pairs:
        kk = slab(kk_ref[0], p) / jnp.maximum(jnp.sqrt(ssq[p]), 1e-12)
        b = kk * slab(a_ref[0], p)
        at_st.append(stack(-kk * slab(e_prev_all, p)))
        bt_st.append(stack(b * slab(e_neg_all, p)))
        bh_st.append(stack(b * slab(e_tail_all, p)))
        rt_st.append(stack(r_in[p] * slab(e_cum_all, p)))
        kt_st.append(stack(k_in[p] * slab(e_neg_all, p)))
        kh_st.append(stack(k_in[p] * slab(e_tail_all, p)))
        v_st.append(stack(v_in[p]))
    a_ab = [jnp.where(strict, _mm(at_st[p], bt_st[p], _NT), 0.0) for p in pairs]
    a_ak = [jnp.where(strict, _mm(at_st[p], kt_st[p], _NT), 0.0) for p in pairs]
    r_b = [jnp.where(incl, _mm(rt_st[p], bt_st[p], _NT), 0.0) for p in pairs]
    r_k = [jnp.where(incl, _mm(rt_st[p], kt_st[p], _NT), 0.0) for p in pairs]
    av = [_mm(a_ak[p], v_st[p], _NN) for p in pairs]
    ht = [_mm(v_st[p], kh_st[p], _TN) for p in pairs]
    yp = [_mm(r_k[p], v_st[p], _NN) for p in pairs]
    pw = a_ab
    inv_a = a_ab
    n = 1
    while 2 * n < L:
        pw = [_mm(pw[p], pw[p], _NN) for p in pairs]
        inv_a = [inv_a[p] + pw[p] + _mm(inv_a[p], pw[p], _NN) for p in pairs]
        n *= 2
    ap = [at_st[p] + _mm(inv_a[p], at_st[p], _NN) for p in pairs]
    vp = [av[p] + _mm(inv_a[p], av[p], _NN) for p in pairs]
    g = [jnp.where(eye_w, slab(wl_all, p), 0.0) + _mm(bh_st[p], ap[p], _TN) for p in pairs]
    ht = [ht[p] + _mm(vp[p], bh_st[p], _TN) for p in pairs]
    rp = [unstack(rt_st[p] + _mm(r_b[p], ap[p], _NN)) for p in pairs]
    yp = [unstack(yp[p] + _mm(r_b[p], vp[p], _NN)) for p in pairs]
    s_prev = [state_ref[p] for p in pairs]
    y = [_mm(rp[p], s_prev[p], _NT) + yp[p] for p in pairs]
    for p in pairs:
        state_ref[p] = _mm(s_prev[p], g[p], _NT) + ht[p]
    mean = [head_sum(y[p]) * (1.0 / N) for p in pairs]
    cen = [y[p] - mean[p] for p in pairs]
    var = [head_sum(cen[p] * cen[p]) * (1.0 / N) for p in pairs]
    bonus = [head_sum(r_in[p] * k_in[p] * slab(rk_ref[...], p)) * v_in[p] for p in pairs]
    y_ref[0] = jnp.concatenate(
        [cen[p] * lax.rsqrt(var[p] + GN_EPS) * slab(gnw_ref[...], p) + slab(gnb_ref[...], p) + bonus[p]
         for p in pairs], axis=-1)

    @pl.when(c == pl.num_programs(2) - 1)
    def _():
        for p in pairs:
            s = state_ref[p]
            sout_ref[0, 2 * p] = s[:N, :N]
            sout_ref[0, 2 * p + 1] = s[N:, N:]


def wkv7_chunked(r, lw, k, v, kk, a_gate, r_k, gn_w, gn_b, s0):
    n, T, D = r.shape
    L = WKV_CHUNK
    hb = WKV_HEADS_PER_STEP
    w = hb * RWKV_HEAD
    seq = pl.BlockSpec((1, L, w), lambda b, h, c: (b, c, h))
    vec = pl.BlockSpec((1, w), lambda b, h, c: (0, h))
    st = pl.BlockSpec((1, hb, RWKV_HEAD, RWKV_HEAD), lambda b, h, c: (b, h, 0, 0))
    row = lambda t: t.reshape(1, D).astype(jnp.float32)
    return pl.pallas_call(
        _wkv7_chunk_kernel,
        grid=(n, D // w, T // L),
        in_specs=[seq] * 6 + [vec] * 3 + [st],
        out_specs=[seq, st],
        out_shape=[jax.ShapeDtypeStruct((n, T, D), jnp.float32),
                   jax.ShapeDtypeStruct(s0.shape, jnp.float32)],
        scratch_shapes=[pltpu.VMEM((hb // 2, 2 * RWKV_HEAD, 2 * RWKV_HEAD), jnp.float32)],
        compiler_params=pltpu.CompilerParams(dimension_semantics=("parallel", "parallel", "arbitrary")),
        name="wkv7_chunked",
    )(r, lw, k, v, kk, a_gate, row(r_k), row(gn_w), row(gn_b), s0)


def wkv7_scan(r, lw, k, v, a_vec, b_vec, s0):
    def step(S, inp):
        r_t, lw_t, k_t, v_t, a_t, b_t = inp
        sa = jnp.sum(S * a_t[:, :, None, :], axis=-1)
        S = S * jnp.exp(lw_t)[:, :, None, :] + sa[..., None] * b_t[:, :, None, :] + v_t[..., None] * k_t[:, :, None, :]
        return S, jnp.sum(S * r_t[:, :, None, :], axis=-1)

    xs = tuple(jnp.moveaxis(t, 1, 0) for t in (r, lw, k, v, a_vec, b_vec))
    s_fin, ys = lax.scan(step, s0, xs)
    return jnp.moveaxis(ys, 0, 1), s_fin


ROW_TILE = 256


def _bdot(x, w):
    return jnp.dot(x.astype(jnp.bfloat16), w, preferred_element_type=jnp.float32)


def _rms(x):
    return x * lax.rsqrt(jnp.mean(x * x, axis=-1, keepdims=True) + NORM_EPS)


def _rwkv_pre_kernel(x_ref, shift_ref, gain_ref, mu_ref, vec_ref, wrkv_ref, w1_ref, a1_ref, g1_ref,
                     w2_ref, a2_ref, g2_ref,
                     r_ref, lw_ref, k_ref, v_ref, kk_ref, a_ref, g_ref, hlast_ref, prev_ref):
    @pl.when(pl.program_id(1) == 0)
    def _():
        prev_ref[...] = shift_ref[0]

    tm = x_ref.shape[1]
    h = _rms(x_ref[0]) * gain_ref[...]
    first_row = lax.broadcasted_iota(jnp.int32, h.shape, 0) == 0
    h_prev = jnp.where(first_row, prev_ref[...], pltpu.roll(h, 1, 0))
    prev_ref[...] = h[tm - 1:tm, :]
    hlast_ref[0] = h[tm - 1:tm, :]
    dx = h_prev - h
    mix = lambda i: (h + dx * mu_ref[i:i + 1, :]).astype(jnp.bfloat16)
    w0, a0, k_k, k_a = (vec_ref[i:i + 1, :] for i in range(4))
    r_ref[0] = _bdot(mix(0), wrkv_ref[0])
    k = _bdot(mix(2), wrkv_ref[1])
    v_ref[0] = _bdot(mix(3), wrkv_ref[2])
    lw_ref[0] = -DECAY_SCALE * jax.nn.sigmoid(w0 + _bdot(jnp.tanh(_bdot(mix(1), w1_ref[...])), w2_ref[...]))
    a = jax.nn.sigmoid(a0 + _bdot(_bdot(mix(4), a1_ref[...]), a2_ref[...]))
    g_ref[0] = _bdot(jax.nn.sigmoid(_bdot(mix(5), g1_ref[...])), g2_ref[...])
    a_ref[0] = a
    kk_ref[0] = k * k_k
    k_ref[0] = k * (1.0 + (a - 1.0) * k_a)


def _out_proj_kernel(*refs, n_terms, gated, routed):
    terms = refs[:n_terms]
    rest = refs[n_terms:]
    if gated:
        gate_ref, rest = rest[0], rest[1:]
    x_ref, w_ref, gain_ref = rest[:3]
    rest = rest[3:]
    if routed:
        wr_ref, rest = rest[0], rest[1:]
    xo_ref, ho_ref = rest[:2]
    y = terms[0][...]
    for t in terms[1:]:
        y = y + t[...]
    if gated:
        y = y * gate_ref[...]
    xo = x_ref[...] + _bdot(y, w_ref[...])
    xo_ref[...] = xo
    h = _rms(xo) * gain_ref[...]
    ho_ref[...] = h.astype(ho_ref.dtype)
    if routed:
        h_hi, h_lo = _split_bf16(h)
        w_hi, w_lo = _split_bf16(wr_ref[...])
        dot = functools.partial(jnp.dot, preferred_element_type=jnp.float32)
        rest[2][...] = dot(h_hi, w_hi) + (dot(h_hi, w_lo) + dot(h_lo, w_hi))


def out_proj(terms, gate, x, w, gain, w_router=None):
    m, d = x.shape
    kdim = w.shape[0]
    tm = min(ROW_TILE, m)
    rows = lambda width: pl.BlockSpec((tm, width), lambda i: (i, 0))
    full = lambda a: pl.BlockSpec(a.shape, lambda i: (0, 0))
    ins = list(terms) + ([gate] if gate is not None else [])
    consts = [w.astype(jnp.bfloat16), gain.reshape(1, d)]
    out_specs = [rows(d), rows(d)]
    out_shape = [jax.ShapeDtypeStruct((m, d), jnp.float32), jax.ShapeDtypeStruct((m, d), jnp.bfloat16)]
    if w_router is not None:
        consts.append(jnp.pad(w_router, ((0, 0), (0, 128 - w_router.shape[1]))))
        out_specs.append(rows(128))
        out_shape.append(jax.ShapeDtypeStruct((m, 128), jnp.float32))
    return pl.pallas_call(
        functools.partial(_out_proj_kernel, n_terms=len(terms), gated=gate is not None,
                          routed=w_router is not None),
        grid=(m // tm,),
        in_specs=[rows(kdim)] * len(ins) + [rows(d)] + [full(a) for a in consts],
        out_specs=out_specs,
        out_shape=out_shape,
        compiler_params=pltpu.CompilerParams(dimension_semantics=("parallel",),
                                             vmem_limit_bytes=48 * 1024 * 1024),
        name="out_proj",
    )(*ins, x, *consts)


def rwkv7_layer(x, shift0, s0, gain, mu, w_rkv, w0, w1, w2, a0, a1, a2, g1, g2, k_k, k_a, r_k, gn_w, gn_b, w_o,
                gain_next):
    n, T, D = x.shape
    tm = min(ROW_TILE, T)
    bf = lambda t: t.astype(jnp.bfloat16)
    seq = pl.BlockSpec((1, tm, D), lambda b, t: (b, t, 0))
    full = lambda a: pl.BlockSpec(a.shape, lambda b, t: (0,) * a.ndim)
    per_seq = pl.BlockSpec((1, 1, D), lambda b, t: (b, 0, 0))
    vecs = jnp.stack([w0, a0, k_k, k_a]).astype(jnp.float32)
    weights = [bf(w_rkv), bf(w1), bf(a1), bf(g1), bf(w2), bf(a2), bf(g2)]
    small = [gain.reshape(1, D), mu, vecs]
    outs = pl.pallas_call(
        _rwkv_pre_kernel,
        grid=(n, T // tm),
        in_specs=[seq, per_seq] + [full(a) for a in small + weights],
        out_specs=[seq] * 7 + [per_seq],
        out_shape=[jax.ShapeDtypeStruct((n, T, D), jnp.float32)] * 7
                  + [jax.ShapeDtypeStruct((n, 1, D), jnp.float32)],
        scratch_shapes=[pltpu.VMEM((1, D), jnp.float32)],
        compiler_params=pltpu.CompilerParams(dimension_semantics=("parallel", "arbitrary"),
                                             vmem_limit_bytes=56 * 1024 * 1024),
        name="rwkv_pre",
    )(x, shift0.reshape(n, 1, D), *small, *weights)
    r, lw, k, v, kk, a_gate, g, h_last = outs
    y, s_fin = wkv7_chunked(r, lw, k, v, kk, a_gate, r_k.reshape(-1), gn_w, gn_b, s0.astype(jnp.float32))
    flat = lambda t: t.reshape(n * T, D)
    x1, h1 = out_proj([flat(y)], flat(g), flat(x), w_o, gain_next)
    return x1, h1, s_fin, h_last.reshape(n, D)


def rwkv7_time_mix(h, h_prev, s0, mu, w_rkv, w0, w1, w2, a0, a1, a2, g1, g2, k_k, k_a, r_k, gn_w, gn_b, w_o,
                   precise=False):
    n, T, D = h.shape
    f32 = jnp.float32
    mm = functools.partial(dense, precise=precise)
    dx = jnp.concatenate([h_prev[:, None, :].astype(h.dtype), h[:, :-1]], axis=1) - h
    xr, xw, xk, xv, xa, xg = (h + dx * mu[i] for i in range(6))
    r = mm(xr, w_rkv[0])
    k = mm(xk, w_rkv[1])
    v = mm(xv, w_rkv[2])
    log_decay = -DECAY_SCALE * jax.nn.sigmoid((w0 + mm(jnp.tanh(mm(xw, w1)), w2)).astype(f32))
    a = jax.nn.sigmoid((a0 + mm(mm(xa, a1), a2)).astype(f32))
    g = mm(jax.nn.sigmoid(mm(xg, g1)), g2)
    heads = lambda t: t.astype(f32).reshape(n, T, RWKV_HEADS, RWKV_HEAD)
    kk = heads(k * k_k)
    kk = kk / jnp.maximum(jnp.sqrt(jnp.sum(kk * kk, axis=-1, keepdims=True)), 1e-12)
    a_h = heads(a)
    k_h = heads(k.astype(f32) * (1.0 + (a - 1.0) * k_a.astype(f32)))
    r_h, v_h = heads(r), heads(v)
    y, s_fin = wkv7_scan(r_h, heads(log_decay), k_h, v_h, -kk, kk * a_h, s0.astype(f32))
    mean = jnp.mean(y, axis=-1, keepdims=True)
    var = jnp.mean(jnp.square(y - mean), axis=-1, keepdims=True)
    y = ((y - mean) * lax.rsqrt(var + GN_EPS)).reshape(n, T, D) * gn_w.astype(f32) + gn_b.astype(f32)
    bonus = jnp.sum(r_h * k_h * r_k.astype(f32), axis=-1, keepdims=True) * v_h
    y = (y + bonus.reshape(n, T, D)).astype(h.dtype)
    return mm(y * g, w_o), s_fin, h[:, -1]


def shared_kv_rows(x, norm_kv, w_kv, pos, precise=False):
    n, T, _ = x.shape
    kv = dense(rmsnorm(x, norm_kv), w_kv, precise).reshape(n, T, N_BRANCH, 2, NSA_KV_GROUPS, NSA_HEAD_DIM)
    cmp_kv = kv[:, :, 0]
    slc_kv = jnp.stack([rope_partial(kv[:, :, 1, 0], pos), kv[:, :, 1, 1]], axis=2)
    win_kv = jnp.stack([rope_partial(kv[:, :, 2, 0], pos), kv[:, :, 2, 1]], axis=2)
    return cmp_kv, slc_kv, win_kv


CHUNK_LANES = CMP_STRIDE * 2 * NSA_KV_GROUPS * NSA_HEAD_DIM
HIDDEN_LANES = 2 * NSA_KV_GROUPS * 2 * CMP_HIDDEN
PAGES_PER_STEP = 32
CHUNKS_PER_PAGE = PAGE_SIZE // CMP_STRIDE


def _chunk_weights(cmp_w1):
    w = cmp_w1.reshape(2, 2, CMP_STRIDE, NSA_HEAD_DIM, CMP_HIDDEN)
    w = w.transpose(2, 0, 3, 1, 4)
    eye = jnp.eye(2, dtype=w.dtype)
    big = jnp.einsum('ab,cf,jaehk->jacebfhk', eye, jnp.eye(NSA_KV_GROUPS, dtype=w.dtype), w)
    return big.reshape(CHUNK_LANES, HIDDEN_LANES).astype(jnp.bfloat16)


def _token_weights(cmp_w1):
    w = cmp_w1.reshape(2, 2, CMP_STRIDE, NSA_HEAD_DIM, CMP_HIDDEN)
    return w.transpose(2, 0, 3, 1, 4).reshape(CMP_STRIDE, 2, NSA_HEAD_DIM, 2 * CMP_HIDDEN).astype(jnp.bfloat16)


def _paged_chunk_kernel(pt_ref, *refs):
    pages, w_ref, o_ref, xt_ref = (refs[:PAGES_PER_STEP], refs[PAGES_PER_STEP], refs[PAGES_PER_STEP + 1],
                                   refs[PAGES_PER_STEP + 2])
    planes = [(kv, g) for kv in range(2) for g in range(NSA_KV_GROUPS)]
    tok = lax.broadcasted_iota(jnp.int32, (PAGE_SIZE, PAGE_SIZE), 0)
    dst = lax.broadcasted_iota(jnp.int32, (PAGE_SIZE, PAGE_SIZE), 1)
    regroup = (tok == (dst % CHUNKS_PER_PAGE) * CMP_STRIDE + dst // CHUNKS_PER_PAGE).astype(jnp.bfloat16)
    for p in range(PAGES_PER_STEP):
        for i, (kv, g) in enumerate(planes):
            xt_ref[p, i] = _bdot(pages[p][0, kv, g], regroup).T
    width = 2 * CMP_HIDDEN
    for i, (kv, g) in enumerate(planes):
        acc = None
        for j in range(CMP_STRIDE):
            rows = [xt_ref[p, i, j * CHUNKS_PER_PAGE:(j + 1) * CHUNKS_PER_PAGE, :] for p in range(PAGES_PER_STEP)]
            y = _bdot(jnp.concatenate(rows, axis=0), w_ref[j, kv])
            acc = y if acc is None else acc + y
        o_ref[0, :, i * width:(i + 1) * width] = acc


def paged_chunk_hidden(cache, page_table, w_token):
    n, n_pages = page_table.shape
    planes = cache.transpose(0, 2, 3, 4, 1)
    rows = PAGES_PER_STEP * CHUNKS_PER_PAGE

    def page_spec(k):
        return pl.BlockSpec((1,) + planes.shape[1:], lambda b, s, pt: (pt[b, s * PAGES_PER_STEP + k], 0, 0, 0, 0))

    return pl.pallas_call(
        _paged_chunk_kernel,
        grid_spec=pltpu.PrefetchScalarGridSpec(
            num_scalar_prefetch=1,
            grid=(n, n_pages // PAGES_PER_STEP),
            in_specs=[page_spec(k) for k in range(PAGES_PER_STEP)]
                     + [pl.BlockSpec(w_token.shape, lambda b, s, pt: (0, 0, 0, 0))],
            out_specs=pl.BlockSpec((1, rows, HIDDEN_LANES), lambda b, s, pt: (b, s, 0)),
            scratch_shapes=[pltpu.VMEM((PAGES_PER_STEP, 2 * NSA_KV_GROUPS, PAGE_SIZE, NSA_HEAD_DIM), jnp.float32)],
        ),
        out_shape=jax.ShapeDtypeStruct((n, n_pages * CHUNKS_PER_PAGE, HIDDEN_LANES), jnp.float32),
        compiler_params=pltpu.CompilerParams(dimension_semantics=("parallel", "arbitrary"),
                                             vmem_limit_bytes=48 * 1024 * 1024),
        name="paged_chunk_hidden",
    )(page_table.astype(jnp.int32), *([planes] * PAGES_PER_STEP), w_token)


def _matmul_kernel(x_ref, w_ref, o_ref):
    o_ref[...] = jnp.dot(x_ref[...].astype(jnp.bfloat16), w_ref[...],
                         preferred_element_type=jnp.float32).astype(o_ref.dtype)


def matmul(x, w, rows_per_step=512, out_dtype=jnp.float32):
    m, k = x.shape
    nn = w.shape[1]
    tm = min(rows_per_step, m)
    return pl.pallas_call(
        _matmul_kernel,
        grid=(m // tm,),
        in_specs=[pl.BlockSpec((tm, k), lambda i: (i, 0)), pl.BlockSpec((k, nn), lambda i: (0, 0))],
        out_specs=pl.BlockSpec((tm, nn), lambda i: (i, 0)),
        out_shape=jax.ShapeDtypeStruct((m, nn), out_dtype),
        compiler_params=pltpu.CompilerParams(dimension_semantics=("parallel",),
                                             vmem_limit_bytes=48 * 1024 * 1024),
        name="matmul",
    )(x, w.astype(jnp.bfloat16))


def _split_bf16(x):
    hi = x.astype(jnp.bfloat16)
    return hi, (x - hi.astype(jnp.float32)).astype(jnp.bfloat16)


def _matmul3_kernel(x_ref, w_ref, o_ref):
    x_hi, x_lo = _split_bf16(x_ref[...])
    w_hi, w_lo = _split_bf16(w_ref[...])
    dot = functools.partial(jnp.dot, preferred_element_type=jnp.float32)
    o_ref[...] = dot(x_hi, w_hi) + (dot(x_hi, w_lo) + dot(x_lo, w_hi))


def matmul_precise(x, w, cols_per_step=512):
    m, k = x.shape
    nn = w.shape[1]
    pad = (-nn) % 128
    if pad:
        w = jnp.pad(w, ((0, 0), (0, pad)))
    tn = math.gcd(cols_per_step, nn + pad)
    tm = min(m, 512)
    out = pl.pallas_call(
        _matmul3_kernel,
        grid=(m // tm, (nn + pad) // tn),
        in_specs=[pl.BlockSpec((tm, k), lambda i, j: (i, 0)), pl.BlockSpec((k, tn), lambda i, j: (0, j))],
        out_specs=pl.BlockSpec((tm, tn), lambda i, j: (i, j)),
        out_shape=jax.ShapeDtypeStruct((m, nn + pad), jnp.float32),
        compiler_params=pltpu.CompilerParams(dimension_semantics=("parallel", "parallel")),
        name="matmul_precise",
    )(x, w)
    return out[:, :nn] if pad else out


def dense(x, w, precise=False):
    if precise:
        return matmul_precise(x.reshape(-1, x.shape[-1]), w).reshape(x.shape[:-1] + (w.shape[1],))
    return x @ w


def compress_from_hidden(hidden, pos_emb, w1, b1, w2):
    n, C = hidden.shape[:2]
    rows = -(-C // 8) * 8
    if rows != C:
        hidden = jnp.pad(hidden, ((0, 0), (0, rows - C), (0, 0)))
    w1r = w1.reshape(2, CMP_BLOCK, NSA_HEAD_DIM, CMP_HIDDEN)
    bias = jnp.einsum('ajd,ajdh->ah', pos_emb, w1r) + b1
    out = pl.pallas_call(
        functools.partial(_compress_finish_kernel, n_blocks=C),
        grid=(n,),
        in_specs=[pl.BlockSpec((1, rows, HIDDEN_LANES), lambda b: (b, 0, 0)),
                  pl.BlockSpec(bias.shape, lambda b: (0, 0)), pl.BlockSpec(w2.shape, lambda b: (0, 0, 0))],
        out_specs=pl.BlockSpec((1, rows, 2 * SLAB), lambda b: (b, 0, 0)),
        out_shape=jax.ShapeDtypeStruct((n, rows, 2 * SLAB), jnp.float32),
        compiler_params=pltpu.CompilerParams(dimension_semantics=("parallel",)),
        name="compress_finish",
    )(hidden, bias, w2.astype(jnp.bfloat16))[:, :C]
    shape = (n, C, NSA_KV_GROUPS, NSA_HEAD_DIM)
    return out[:, :, :SLAB].reshape(shape), out[:, :, SLAB:].reshape(shape)


def _compress_finish_kernel(h_ref, bias_ref, w2_ref, o_ref, *, n_blocks):
    h = h_ref[0]
    rows = h.shape[0]
    keep = lax.broadcasted_iota(jnp.int32, (rows, CMP_HIDDEN), 0) < n_blocks - 1
    outs = []
    for i in range(2 * NSA_KV_GROUPS):
        kv = i // NSA_KV_GROUPS
        first = h[:, 2 * i * CMP_HIDDEN:(2 * i + 1) * CMP_HIDDEN]
        second = h[:, (2 * i + 1) * CMP_HIDDEN:(2 * i + 2) * CMP_HIDDEN]
        nxt = jnp.where(keep, pltpu.roll(second, rows - 1, 0), 0.0)
        act = jax.nn.gelu(first + nxt + bias_ref[kv:kv + 1, :])
        outs.append(_bdot(act, w2_ref[kv]))
    o_ref[0] = jnp.concatenate(outs, axis=-1)


def rows_chunk_hidden(kv_rows, w_chunk):
    n, T = kv_rows.shape[:2]
    chunks = kv_rows.reshape(n * T // CMP_STRIDE, CHUNK_LANES)
    return matmul(chunks, w_chunk, rows_per_step=256).reshape(n, T // CMP_STRIDE, HIDDEN_LANES)


def nsa_query(h, w_in, precise=False):
    n, T, _ = h.shape
    proj = dense(h, w_in, precise)
    q = proj[..., :NSA_HEADS * NSA_HEAD_DIM].reshape(n, T, NSA_KV_GROUPS, NSA_GROUP_HEADS, NSA_HEAD_DIM)
    gate = proj[..., NSA_HEADS * NSA_HEAD_DIM:].reshape(n, T, NSA_KV_GROUPS, NSA_GROUP_HEADS, N_BRANCH)
    return q, gate


def compressed_branch_and_selection(q, t_pos, kc, vc):
    scale = NSA_HEAD_DIM ** -0.5
    n_c = kc.shape[1]
    vis = (jnp.arange(n_c) * CMP_STRIDE + CMP_BLOCK - 1)[None, :] <= t_pos[:, None]
    p_cmp = masked_softmax(jnp.einsum('nqghd,ncgd->nghqc', q, kc) * scale, vis)
    o_cmp = jnp.einsum('nghqc,ncgd->nqghd', p_cmp.astype(vc.dtype), vc)
    ratio = SLC_BLOCK // CMP_STRIDE
    lead = CMP_BLOCK // CMP_STRIDE - 1
    n_s = n_c // ratio
    pg = jnp.pad(jnp.sum(p_cmp, axis=2), ((0, 0), (0, 0), (0, 0), (lead, 0)))
    p_slc = pg[..., 0:ratio * n_s:ratio]
    for o in range(1, ratio + lead):
        p_slc = p_slc + pg[..., o:o + ratio * n_s:ratio]
    jb = jnp.arange(n_s)[None, :]
    jt = (t_pos // SLC_BLOCK)[:, None]
    forced = (jb == 0) | (jb == jt) | (jb == jt - 1)
    score = jnp.where(jb > jt, -jnp.inf, jnp.where(forced, FORCED_SCORE, p_slc))
    before = (score[..., :, None] > score[..., None, :]) | (
        (score[..., :, None] == score[..., None, :]) & (jb[0][:, None] < jb[0][None, :]))
    rank = jnp.sum(before, axis=-2)
    slots = jnp.arange(min(SLC_TOPN, n_s))
    idx = jnp.sum(jnp.where(rank[..., None, :] == slots[:, None], jb[0], 0), axis=-1)
    return o_cmp, idx


def _dot3(x, y, dims):
    x_hi, x_lo = _split_bf16(x)
    y_hi, y_lo = _split_bf16(y)
    dot = lambda a, b: lax.dot_general(a, b, dims, preferred_element_type=jnp.float32)
    return dot(x_hi, y_hi) + (dot(x_hi, y_lo) + dot(x_lo, y_hi))


def _decode_kernel(half_ref, new_ref, phys_ref, q_ref, rows_ref, win_ref, *refs):
    b = pl.program_id(0)
    n_sel = len(refs) - 1
    pages, o_ref = refs[:n_sel], refs[n_sel]
    k_per_group = n_sel // NSA_KV_GROUPS
    hg, dh = NSA_GROUP_HEADS, NSA_HEAD_DIM
    tok_half = lax.broadcasted_iota(jnp.int32, (hg, PAGE_SIZE), 1) // SLC_BLOCK
    rows = rows_ref[0]

    def attend(qg, scores, values, k_new, v_new, new_ok):
        s_new = jnp.where(new_ok, jnp.sum(qg * k_new, axis=-1, keepdims=True), MASKED)
        m = s_new
        for s in scores:
            m = jnp.maximum(m, jnp.max(s, axis=-1, keepdims=True))
        e_new = jnp.exp(s_new - m)
        total, out = e_new, e_new * v_new
        for s, v in zip(scores, values):
            e = jnp.exp(s - m)
            total = total + jnp.sum(e, axis=-1, keepdims=True)
            out = out + _dot3(e, v, _NT)
        return out / jnp.maximum(total, 1e-30)

    outs = []
    for g in range(NSA_KV_GROUPS):
        qg = q_ref[0, g * hg:(g + 1) * hg, :]
        lanes = slice(g * dh, (g + 1) * dh)
        scores, values = [], []
        any_new = jnp.int32(0)
        for k in range(k_per_group):
            j = g * k_per_group + k
            plane = pages[j][0]
            is_new = new_ref[b, j]
            ok = (tok_half == half_ref[b, j]) & (is_new == 0)
            scores.append(jnp.where(ok, _dot3(qg, plane[0, 0], _NN), MASKED))
            values.append(plane[1, 0])
            any_new = jnp.maximum(any_new, is_new)
        o_slc = attend(qg, scores, values, rows[0:1, lanes], rows[1:2, lanes], any_new > 0)
        o_win = attend(qg, [_dot3(qg, win_ref[0, 0, g], _NN)], [win_ref[0, 1, g]],
                       rows[2:3, lanes], rows[3:4, lanes], True)
        outs.append(jnp.concatenate([o_slc, o_win], axis=-1))
    o_ref[0] = jnp.concatenate(outs, axis=0)


def decode_attention(q_rot, idx, page_table, cache_slc_kv, cache_win_kv, slc_new, win_new):
    n = q_rot.shape[0]
    sub = PAGE_SIZE // SLC_BLOCK
    n_past_blk = PAST_LEN // SLC_BLOCK
    assert cache_win_kv.shape[1] <= WINDOW and cache_win_kv.shape[1] <= PAST_LEN
    k_sel = idx.shape[-1]
    flat = idx.reshape(n, NSA_KV_GROUPS * k_sel)
    past = jnp.minimum(flat, n_past_blk - 1)
    hit = (past // sub)[:, :, None] == jnp.arange(page_table.shape[1])[None, None, :]
    phys = jnp.sum(jnp.where(hit, page_table[:, None, :], 0), axis=-1).astype(jnp.int32)
    half = (past % sub).astype(jnp.int32)
    is_new = (flat >= n_past_blk).astype(jnp.int32)
    pool = cache_slc_kv.transpose(0, 2, 3, 4, 1)
    window = cache_win_kv.transpose(0, 2, 3, 4, 1)
    rows = jnp.concatenate([slc_new.reshape(n, 2, SLAB), win_new.reshape(n, 2, SLAB)], axis=1)

    def page_spec(j):
        g = j // k_sel
        return pl.BlockSpec((1, 2, 1, NSA_HEAD_DIM, PAGE_SIZE), lambda b, hf, nw, ph: (ph[b, j], 0, g, 0, 0))

    out = pl.pallas_call(
        _decode_kernel,
        grid_spec=pltpu.PrefetchScalarGridSpec(
            num_scalar_prefetch=3,
            grid=(n,),
            in_specs=[pl.BlockSpec((1,) + q_rot.shape[1:], lambda b, hf, nw, ph: (b, 0, 0)),
                      pl.BlockSpec((1,) + rows.shape[1:], lambda b, hf, nw, ph: (b, 0, 0)),
                      pl.BlockSpec((1,) + window.shape[1:], lambda b, hf, nw, ph: (b, 0, 0, 0, 0))]
                     + [page_spec(j) for j in range(NSA_KV_GROUPS * k_sel)],
            out_specs=pl.BlockSpec((1, NSA_HEADS, 2 * NSA_HEAD_DIM), lambda b, hf, nw, ph: (b, 0, 0)),
        ),
        out_shape=jax.ShapeDtypeStruct((n, NSA_HEADS, 2 * NSA_HEAD_DIM), jnp.float32),
        compiler_params=pltpu.CompilerParams(dimension_semantics=("parallel",)),
        name="decode_attention",
    )(half, is_new, phys, q_rot, rows, window, *([pool] * (NSA_KV_GROUPS * k_sel)))
    return out[:, :, :NSA_HEAD_DIM], out[:, :, NSA_HEAD_DIM:]


NSA_SLABS = NSA_GROUP_HEADS
NSA_TILES = NSA_SLABS * NSA_KV_GROUPS
SELECTED_KEY_BLOCK = 512
DENSE_BATCH = 16
MASKED = -1e30
SLC_PER_CMP = SLC_BLOCK // CMP_STRIDE
CMP_LEAD = CMP_BLOCK // CMP_STRIDE - 1
SLAB = NSA_KV_GROUPS * NSA_HEAD_DIM
SEL_LANES = 64


def _group_tiles(q_ref, qs_ref):
    tq = q_ref.shape[1]
    lane = lax.broadcasted_iota(jnp.int32, (tq, SLAB), 1)
    for i in range(NSA_SLABS):
        qs = q_ref[0, :, i * 128:(i + 1) * 128]
        for g in range(NSA_KV_GROUPS):
            in_group = (lane >= g * NSA_HEAD_DIM) & (lane < (g + 1) * NSA_HEAD_DIM)
            qs_ref[2 * i + g] = jnp.where(in_group, qs, jnp.zeros_like(qs)).astype(jnp.bfloat16)


def _merge_groups(o_ref, tiles, gate_ref, branch):
    tq = tiles[0].shape[0]
    lane = lax.broadcasted_iota(jnp.int32, (tq, SLAB), 1)
    gates = gate_ref[0]
    col = lambda g, i: (g * NSA_GROUP_HEADS + i) * N_BRANCH + branch
    for i in range(NSA_SLABS):
        lo = tiles[2 * i] * gates[:, col(0, i):col(0, i) + 1]
        hi = tiles[2 * i + 1] * gates[:, col(1, i):col(1, i) + 1]
        o_ref[0, :, i * 128:(i + 1) * 128] = jnp.where(lane < NSA_HEAD_DIM, lo, hi)


def _nsa_cmp_kernel(q_ref, kc_ref, vc_ref, gate_ref, o_ref, sel_ref, qs_ref):
    qi = pl.program_id(1)
    tq = q_ref.shape[1]
    nc = kc_ref.shape[1]
    ns = nc // SLC_PER_CMP
    t0 = qi * tq
    _group_tiles(q_ref, qs_ref)
    kc = kc_ref[0]
    vc = vc_ref[0]
    t_row = t0 + lax.broadcasted_iota(jnp.int32, (tq, nc), 0)
    c_pos = lax.broadcasted_iota(jnp.int32, (tq, nc), 1) * CMP_STRIDE + (CMP_BLOCK - 1)
    vis = c_pos <= t_row
    tiles = range(NSA_TILES)
    s = [_mm(qs_ref[r], kc, _NT) for r in tiles]
    p = []
    for r in tiles:
        sr = jnp.where(vis, s[r], MASKED)
        m = jnp.max(sr, axis=-1, keepdims=True)
        e = jnp.where(vis, jnp.exp(sr - m), 0.0)
        p.append(e / jnp.maximum(jnp.sum(e, axis=-1, keepdims=True), 1e-30))
    _merge_groups(o_ref, [_mm(p[r], vc, _NN) for r in tiles], gate_ref, 0)

    jrow = lax.broadcasted_iota(jnp.int32, (ns, nc), 0)
    ccol = lax.broadcasted_iota(jnp.int32, (ns, nc), 1)
    pool = ((ccol >= SLC_PER_CMP * jrow - CMP_LEAD) & (ccol < SLC_PER_CMP * (jrow + 1))).astype(jnp.bfloat16)
    jb = lax.broadcasted_iota(jnp.int32, (ns, tq), 0)
    jt = (t0 + lax.broadcasted_iota(jnp.int32, (ns, tq), 1)) // SLC_BLOCK
    forced = (jb == 0) | (jb == jt) | (jb == jt - 1)
    sel_t = []
    for g in range(NSA_KV_GROUPS):
        pg = p[g]
        for i in range(1, NSA_SLABS):
            pg = pg + p[2 * i + g]
        pg_hi = pg.astype(jnp.bfloat16)
        pg_lo = (pg - pg_hi.astype(jnp.float32)).astype(jnp.bfloat16)
        p_slc = (lax.dot_general(pool, pg_hi, _NT, preferred_element_type=jnp.float32)
                 + lax.dot_general(pool, pg_lo, _NT, preferred_element_type=jnp.float32))
        score = jnp.where(jb > jt, -jnp.inf, jnp.where(forced, FORCED_SCORE, p_slc))
        rank = jnp.zeros((ns, tq), jnp.float32)
        for i in range(ns):
            row = score[i:i + 1, :]
            tie = jnp.where(jb > i, 1.0, 0.0)
            rank = rank + jnp.where(row > score, 1.0, 0.0) + jnp.where(row == score, tie, 0.0)
        sel_t.append(jnp.where(rank < min(SLC_TOPN, ns), 1.0, 0.0))
        if ns < SEL_LANES:
            sel_t.append(jnp.zeros((SEL_LANES - ns, tq), jnp.float32))
    sel_ref[0] = jnp.concatenate(sel_t, axis=0).T.astype(sel_ref.dtype)


def _nsa_dense_kernel(q_ref, k_ref, vt_ref, sel_ref, gate_ref, o_ref, qs_ref, m_ref, acc_ref, *, windowed, kblock):
    qi = pl.program_id(1)
    tq = q_ref.shape[1]
    t0 = qi * tq
    _group_tiles(q_ref, qs_ref)
    m_ref[...] = jnp.full(m_ref.shape, MASKED, jnp.float32)
    acc_ref[...] = jnp.zeros(acc_ref.shape, jnp.float32)
    k_row = lax.broadcasted_iota(jnp.int32, (kblock, tq), 0)
    t_lane = t0 + lax.broadcasted_iota(jnp.int32, (kblock, tq), 1)
    tiles = range(NSA_TILES)
    first = 0
    last = 0 if windowed else (t0 + tq - 1) // kblock

    def key_block(kb, carry):
        start = pl.multiple_of(jnp.maximum(t0 - WINDOW, 0) if windowed else kb * kblock, tq)
        kblk = k_ref[0, pl.ds(start, kblock), :]
        vt = vt_ref[0, :, pl.ds(start, kblock)]
        k_pos = start + k_row
        if windowed:
            allowed = [(k_pos <= t_lane) & (k_pos >= t_lane - WINDOW)] * NSA_KV_GROUPS
        else:
            sel = sel_ref[0]
            n_sel = NSA_KV_GROUPS * SEL_LANES
            e_key = lax.broadcasted_iota(jnp.int32, (kblock, n_sel), 0)
            e_lane = lax.broadcasted_iota(jnp.int32, (kblock, n_sel), 1)
            blk = kb * (kblock // SLC_BLOCK) + e_key // SLC_BLOCK
            allowed = []
            for g in range(NSA_KV_GROUPS):
                expand = (e_lane == blk + g * SEL_LANES).astype(jnp.bfloat16)
                picked = lax.dot_general(expand, sel, _NT, preferred_element_type=jnp.float32)
                allowed.append((picked > 0.5) & (k_pos <= t_lane))
        bias = [jnp.where(ok, 0.0, MASKED) for ok in allowed]
        d_row = lax.broadcasted_iota(jnp.int32, vt.shape, 0)
        vt_g = [jnp.where((d_row >= g * NSA_HEAD_DIM) & (d_row < (g + 1) * NSA_HEAD_DIM), vt, jnp.ones_like(vt))
                for g in range(NSA_KV_GROUPS)]
        s = [_mm(kblk, qs_ref[r], _NT) for r in tiles]
        for lo in range(0, NSA_TILES, DENSE_BATCH):
            batch = range(lo, lo + DENSE_BATCH)
            p, alpha = {}, {}
            for r in batch:
                sr = s[r] + bias[r % NSA_KV_GROUPS]
                m_prev = m_ref[r]
                m_new = jnp.maximum(m_prev, jnp.max(sr, axis=0, keepdims=True))
                alpha[r] = jnp.exp(m_prev - m_new)
                p[r] = jnp.exp(sr - m_new).astype(jnp.bfloat16)
                m_ref[r] = m_new
            pv = {r: jnp.dot(vt_g[r % NSA_KV_GROUPS], p[r], preferred_element_type=jnp.float32) for r in batch}
            for r in batch:
                acc_ref[r] = alpha[r] * acc_ref[r] + pv[r]
        return carry

    lax.fori_loop(first, last + 1, key_block, 0)
    out = []
    for r in tiles:
        acc = acc_ref[r]
        sum_row = (1 - r % NSA_KV_GROUPS) * NSA_HEAD_DIM
        out.append((acc / jnp.maximum(acc[sum_row:sum_row + 1, :], 1e-30)).T)
    _merge_groups(o_ref, out, gate_ref, 2 if windowed else 1)


def nsa_prompt_attention(q, q_rot, kc, vc, kv_att, gates):
    n, T, D = q.shape
    tq = Q_BLOCK
    qspec = pl.BlockSpec((1, tq, D), lambda b, i: (b, i, 0))
    whole = lambda a: pl.BlockSpec((1,) + a.shape[1:], lambda b, i: (b, 0, 0))
    lanes = lambda j: pl.BlockSpec((1, T, SLAB), lambda b, i: (b, 0, j))
    n_sel = NSA_KV_GROUPS * SEL_LANES
    sel_spec = pl.BlockSpec((1, tq, n_sel), lambda b, i: (b, i, 0))
    gate_spec = pl.BlockSpec((1, tq, gates.shape[2]), lambda b, i: (b, i, 0))
    params = pltpu.CompilerParams(dimension_semantics=("parallel", "arbitrary"),
                                  vmem_limit_bytes=48 * 1024 * 1024)
    qs_scratch = pltpu.VMEM((NSA_TILES, tq, SLAB), jnp.bfloat16)
    o_cmp, sel = pl.pallas_call(
        _nsa_cmp_kernel,
        grid=(n, T // tq),
        in_specs=[qspec, whole(kc), whole(vc), gate_spec],
        out_specs=[qspec, sel_spec],
        out_shape=[jax.ShapeDtypeStruct((n, T, D), jnp.float32),
                   jax.ShapeDtypeStruct((n, T, n_sel), jnp.bfloat16)],
        scratch_shapes=[qs_scratch],
        compiler_params=params,
        name="nsa_cmp_select",
    )(q, kc, vc, gates)
    stat = pltpu.VMEM((NSA_TILES, 1, tq), jnp.float32)
    acc = pltpu.VMEM((NSA_TILES, SLAB, tq), jnp.float32)
    v_t = jnp.swapaxes(jnp.concatenate([kv_att[:, :, SLAB:2 * SLAB], kv_att[:, :, 3 * SLAB:]], axis=-1), 1, 2)
    rows = lambda j: pl.BlockSpec((1, SLAB, T), lambda b, i: (b, j, 0))

    def dense(windowed, branch, name):
        kblock = min(WINDOW + tq, T) if windowed else min(SELECTED_KEY_BLOCK, T)
        return pl.pallas_call(
            functools.partial(_nsa_dense_kernel, windowed=windowed, kblock=kblock),
            grid=(n, T // tq),
            in_specs=[qspec, lanes(2 * branch), rows(branch), sel_spec, gate_spec],
            out_specs=qspec,
            out_shape=jax.ShapeDtypeStruct((n, T, D), jnp.float32),
            scratch_shapes=[qs_scratch, stat, acc],
            compiler_params=params,
            name=name,
        )(q_rot, kv_att, v_t, sel, gates)

    return o_cmp, dense(False, 0, "nsa_selected"), dense(True, 1, "nsa_window")


def _rope_tables(pos):
    half = ROT_DIM // 2
    inv = ROPE_THETA ** (-2.0 * jnp.arange(half, dtype=jnp.float32) / ROT_DIM)
    ang = pos.astype(jnp.float32)[:, None] * inv[None, :]
    rest = NSA_HEAD_DIM - ROT_DIM
    cos = jnp.concatenate([jnp.cos(ang), jnp.cos(ang), jnp.ones((pos.shape[0], rest), jnp.float32)], axis=1)
    sin = jnp.concatenate([-jnp.sin(ang), jnp.sin(ang), jnp.zeros((pos.shape[0], rest), jnp.float32)], axis=1)
    return jnp.tile(cos, (1, NSA_KV_GROUPS)), jnp.tile(sin, (1, NSA_KV_GROUPS))


def _nsa_pre_kernel(x_ref, gkv_ref, gmix_ref, cos_ref, sin_ref, wkv_ref, wq_ref, wg_ref,
                    cmp_ref, slc_ref, win_ref, kvb_ref, q_ref, qr_ref, gate_ref):
    xh = _rms(x_ref[0])
    cos, sin = cos_ref[...], sin_ref[...]
    low = lax.broadcasted_iota(jnp.int32, cos.shape, 1) % NSA_HEAD_DIM < ROT_DIM // 2

    def rope(t):
        swapped = jnp.where(low, pltpu.roll(t, SLAB - ROT_DIM // 2, 1), pltpu.roll(t, ROT_DIM // 2, 1))
        return t * cos + swapped * sin

    kv = _bdot(xh * gkv_ref[...], wkv_ref[...])
    part = lambda j: kv[:, j * SLAB:(j + 1) * SLAB]
    k_slc, k_win = rope(part(2)), rope(part(4))
    cmp_ref[0] = kv[:, :2 * SLAB]
    slc_ref[0] = jnp.concatenate([k_slc, part(3)], axis=-1)
    win_ref[0] = jnp.concatenate([k_win, part(5)], axis=-1)
    kvb_ref[0] = jnp.concatenate([k_slc, part(3), k_win, part(5)], axis=-1).astype(jnp.bfloat16)
    h = xh * gmix_ref[...]
    q = _bdot(h, wq_ref[...])
    q_ref[0] = q.astype(jnp.bfloat16)
    qr_ref[0] = jnp.concatenate([rope(q[:, i * SLAB:(i + 1) * SLAB]) for i in range(NSA_SLABS)],
                                axis=-1).astype(jnp.bfloat16)
    gate_ref[0] = jax.nn.sigmoid(_bdot(h, wg_ref[...]))


def nsa_pre(x, pos, norm_kv, norm_mix, w_kv, w_in):
    n, T, D = x.shape
    tm = min(ROW_TILE, T)
    nq = NSA_HEADS * NSA_HEAD_DIM
    cos, sin = _rope_tables(pos)
    w_q = (_to_slabs(w_in[:, :nq]) * NSA_HEAD_DIM ** -0.5).astype(jnp.bfloat16)
    w_g = jnp.pad(w_in[:, nq:], ((0, 0), (0, SLAB - (w_in.shape[1] - nq)))).astype(jnp.bfloat16)
    seq = lambda width: pl.BlockSpec((1, tm, width), lambda b, t: (b, t, 0))
    full = lambda a: pl.BlockSpec(a.shape, lambda b, t: (0,) * a.ndim)
    table = pl.BlockSpec((tm, SLAB), lambda b, t: (t, 0))
    consts = [norm_kv.reshape(1, D), norm_mix.reshape(1, D)]
    weights = [w_kv.astype(jnp.bfloat16), w_q, w_g]
    widths = [2 * SLAB, 2 * SLAB, 2 * SLAB, 4 * SLAB, nq, nq, SLAB]
    dtypes = [jnp.float32] * 3 + [jnp.bfloat16] * 3 + [jnp.float32]
    return pl.pallas_call(
        _nsa_pre_kernel,
        grid=(n, T // tm),
        in_specs=[seq(D)] + [full(a) for a in consts] + [table, table] + [full(a) for a in weights],
        out_specs=[seq(w) for w in widths],
        out_shape=[jax.ShapeDtypeStruct((n, T, w), dt) for w, dt in zip(widths, dtypes)],
        compiler_params=pltpu.CompilerParams(dimension_semantics=("parallel", "parallel"),
                                             vmem_limit_bytes=48 * 1024 * 1024),
        name="nsa_pre",
    )(x, *consts, cos, sin, *weights)


def _to_slabs(x):
    lead = x.shape[:-1]
    x = x.reshape(lead + (NSA_KV_GROUPS, NSA_GROUP_HEADS, NSA_HEAD_DIM))
    return jnp.swapaxes(x, -3, -2).reshape(lead + (NSA_HEADS * NSA_HEAD_DIM,))


def nsa_layer_prompt(x, norm_kv, norm_mix, w_kv, cmp_pos, cmp_w1, cmp_b1, cmp_w2, w_in, w_o, norm_next, w_router):
    n, T, D = x.shape
    cmp_kv, slc_kv, win_kv, kv_att, q, q_rot, gates = nsa_pre(x, jnp.arange(T), norm_kv, norm_mix, w_kv, w_in)
    hidden = rows_chunk_hidden(cmp_kv, _chunk_weights(cmp_w1))
    kc, vc = compress_from_hidden(hidden, cmp_pos, cmp_w1, cmp_b1, cmp_w2)
    lanes = lambda a: a.reshape(n, a.shape[1], SLAB).astype(jnp.bfloat16)
    branches = nsa_prompt_attention(q, q_rot, lanes(kc), lanes(vc), kv_att, gates)
    flat = lambda a: a.reshape(n * T, a.shape[-1])
    x_out, h_out, logits = out_proj([flat(o) for o in branches], None, flat(x), _to_slabs(w_o.T).T, norm_next,
                                    w_router)
    return x_out, h_out, logits, cmp_kv, slc_kv, win_kv


def sample_kv_context(cmp_new, slc_new, win_new, cache_cmp_kv, cache_slc_kv, cache_win_kv, page_table,
                      cmp_pos, cmp_w1, cmp_b1, cmp_w2):
    n, S = cmp_new.shape[:2]
    n_new_blk = -(-S // SLC_BLOCK)
    pad = ((0, 0), (0, n_new_blk * SLC_BLOCK - S), (0, 0), (0, 0), (0, 0))
    w_chunk = _chunk_weights(cmp_w1)
    hidden = jnp.concatenate([paged_chunk_hidden(cache_cmp_kv, page_table, _token_weights(cmp_w1)),
                              rows_chunk_hidden(jnp.pad(cmp_new.astype(cache_cmp_kv.dtype), pad), w_chunk)], axis=1)
    kc, vc = compress_from_hidden(hidden, cmp_pos, cmp_w1, cmp_b1, cmp_w2)
    win_all = jnp.concatenate([cache_win_kv, win_new.astype(cache_win_kv.dtype)], axis=1)
    return kc, vc, win_all


def sample_nsa(h, kc, vc, slc_new, win_new, cache_slc_kv, cache_win_kv, page_table, w_in, w_o):
    n, S, _ = h.shape
    assert S == 1
    q, gate = nsa_query(h, w_in, precise=True)
    t_pos = PAST_LEN + jnp.arange(S)
    with jax.default_matmul_precision("highest"):
        o_cmp, idx = compressed_branch_and_selection(q, t_pos, kc, vc)
    q_rot = (rope_partial(q, t_pos) * NSA_HEAD_DIM ** -0.5).reshape(n, NSA_HEADS, NSA_HEAD_DIM)
    o_slc, o_win = decode_attention(q_rot, idx[:, :, 0], page_table, cache_slc_kv, cache_win_kv, slc_new, win_new)
    g = jax.nn.sigmoid(gate.astype(jnp.float32)).astype(q.dtype)
    o = g[..., 0:1] * o_cmp + g[..., 1:2] * o_slc.reshape(q.shape) + g[..., 2:3] * o_win.reshape(q.shape)
    return dense(o.reshape(n, S, NSA_HEADS * NSA_HEAD_DIM), w_o, precise=True)


def kernel(x_prompt, x_sample, state_wkv, state_shift, cache_cmp_kv, cache_slc_kv, cache_win_kv, page_table, norm_mix, norm_ffn, norm_kv, norm_final, rw_mu, rw_w_rkv, rw_w0, rw_w1, rw_w2, rw_a0, rw_a1, rw_a2, rw_g1, rw_g2, rw_k_k, rw_k_a, rw_r_k, rw_gn_w, rw_gn_b, rw_w_o, nsa_w_kv, nsa_cmp_pos, nsa_cmp_w1, nsa_cmp_b1, nsa_cmp_w2, nsa_w_in, nsa_w_o, ffn_w_gu, ffn_w_down, moe_router, moe_w_gu, moe_w_down):
    cmp_params = (nsa_cmp_pos, nsa_cmp_w1, nsa_cmp_b1, nsa_cmp_w2)
    assert DEPTH == 2 and N_A_LAYERS == 1
    D = D_MODEL
    moe_gu, moe_down = moe_w_gu[0].astype(jnp.bfloat16), moe_w_down[0].astype(jnp.bfloat16)
    x_prompt, moe_gu, moe_down = lax.optimization_barrier((x_prompt, moe_gu, moe_down))

    n_p, T = x_prompt.shape[:2]
    x1, h1, wkv_fin, h_last = rwkv7_layer(
        x_prompt, jnp.zeros((n_p, D), x_prompt.dtype), jnp.zeros((n_p, RWKV_HEADS, RWKV_HEAD, RWKV_HEAD), jnp.float32),
        norm_mix[0], rw_mu[0], rw_w_rkv[0], rw_w0[0], rw_w1[0], rw_w2[0], rw_a0[0], rw_a1[0], rw_a2[0],
        rw_g1[0], rw_g2[0], rw_k_k[0], rw_k_a[0], rw_r_k[0], rw_gn_w[0], rw_gn_b[0], rw_w_o[0], norm_ffn[0])
    wkv_p, shift_p = wkv_fin[None], h_last[None]
    x2 = swiglu_residual(x1, h1, ffn_w_gu[0], ffn_w_down[0])
    x_p, h_p, logits_p, cmp_rows, slc_rows, win_rows = nsa_layer_prompt(
        x2.reshape(n_p, T, D), norm_kv, norm_mix[1], nsa_w_kv, *cmp_params, nsa_w_in[0], nsa_w_o[0], norm_ffn[1],
        moe_router[0])
    kv_shape = (n_p, T, 2, NSA_KV_GROUPS, NSA_HEAD_DIM)
    cmp_kv_p, slc_kv_p = cmp_rows.reshape(kv_shape), slc_rows.reshape(kv_shape)
    win_kv_p = win_rows.reshape(kv_shape)[:, -min(WINDOW, T):]

    pos_s = PAST_LEN + jnp.arange(x_sample.shape[1], dtype=jnp.int32)
    h = rmsnorm(x_sample, norm_mix[0])
    y, s_fin, h_last_s = rwkv7_time_mix(
        h, state_shift[0], state_wkv[0], rw_mu[0], rw_w_rkv[0], rw_w0[0], rw_w1[0], rw_w2[0],
        rw_a0[0], rw_a1[0], rw_a2[0], rw_g1[0], rw_g2[0], rw_k_k[0], rw_k_a[0],
        rw_r_k[0], rw_gn_w[0], rw_gn_b[0], rw_w_o[0], precise=True)
    wkv_s, shift_s = s_fin[None], h_last_s[None]
    x_s = x_sample + y
    x_s = x_s + swiglu(rmsnorm(x_s, norm_ffn[0]), ffn_w_gu[0], ffn_w_down[0], precise=True)
    cmp_kv_s, slc_kv_s, win_new = shared_kv_rows(x_s, norm_kv, nsa_w_kv, pos_s, precise=True)
    kc_s, vc_s, win_all = sample_kv_context(cmp_kv_s, slc_kv_s, win_new, cache_cmp_kv, cache_slc_kv, cache_win_kv,
                                            page_table, *cmp_params)
    x_s = x_s + sample_nsa(rmsnorm(x_s, norm_mix[1]), kc_s, vc_s, slc_kv_s, win_new, cache_slc_kv, cache_win_kv,
                           page_table, nsa_w_in[0], nsa_w_o[0])
    win_kv_s = win_all[:, -cache_win_kv.shape[1]:]
    x_s = x_s.reshape(-1, D)
    h_s = rmsnorm(x_s, norm_ffn[1])
    logits_s = matmul_precise(h_s, moe_router[0])

    n_tok_p = n_p * T
    logits = jnp.concatenate([logits_p[:, :N_EXPERTS], logits_s], axis=0)
    y0, y1, gate = moe_swiglu(jnp.concatenate([h_p, h_s.astype(jnp.bfloat16)], axis=0), logits,
                              moe_gu, moe_down)
    gate = jnp.pad(gate, ((0, 0), (0, 128 - TOP_K)))
    y_prompt = combine_norm(x_p, y0[:n_tok_p], y1[:n_tok_p], gate[:n_tok_p], norm_final).reshape(x_prompt.shape)
    y_sample = combine_norm(x_s, y0[n_tok_p:], y1[n_tok_p:], gate[n_tok_p:], norm_final).reshape(x_sample.shape)


    return (y_prompt, y_sample, wkv_p, shift_p, cmp_kv_p, slc_kv_p, win_kv_p,
            wkv_s, shift_s, cmp_kv_s, slc_kv_s, win_kv_s)
```

```python
import functools
import math

import jax
import jax.numpy as jnp
from jax import lax
from jax.experimental import pallas as pl
from jax.experimental.pallas import tpu as pltpu

D_MODEL = 1024
DEPTH = 2
PAST_LEN = 16384
PAGE_SIZE = 128
N_A_LAYERS = DEPTH // 2
RWKV_HEAD = 64
RWKV_HEADS = D_MODEL // RWKV_HEAD
DECAY_SCALE = math.exp(-0.5)
GN_EPS = RWKV_HEAD * 1e-5
NSA_HEADS = 16
NSA_HEAD_DIM = 64
NSA_KV_GROUPS = 2
NSA_GROUP_HEADS = NSA_HEADS // NSA_KV_GROUPS
N_BRANCH = 3
CMP_BLOCK = 32
CMP_STRIDE = 16
CMP_HIDDEN = 128
SLC_BLOCK = 64
SLC_TOPN = 16
WINDOW = 512
Q_BLOCK = 128
FORCED_SCORE = 1e4
ROPE_THETA = 500000.0
ROT_DIM = NSA_HEAD_DIM // 4
N_EXPERTS = 8
TOP_K = 2
NORM_EPS = 1e-6


def _rmsnorm_kernel(x_ref, g_ref, o_ref):
    x = x_ref[...]
    y = x * lax.rsqrt(jnp.mean(x * x, axis=-1, keepdims=True) + NORM_EPS)
    o_ref[...] = y * g_ref[...]


def rmsnorm(x, g):
    shp = x.shape
    x2 = x.reshape(-1, shp[-1])
    rows = x2.shape[0]
    tm = min(rows, 512)
    out = pl.pallas_call(
        _rmsnorm_kernel,
        grid=(rows // tm,),
        in_specs=[pl.BlockSpec((tm, shp[-1]), lambda i: (i, 0)),
                  pl.BlockSpec((1, shp[-1]), lambda i: (0, 0))],
        out_specs=pl.BlockSpec((tm, shp[-1]), lambda i: (i, 0)),
        out_shape=jax.ShapeDtypeStruct(x2.shape, x.dtype),
        name="rmsnorm",
    )(x2, g.reshape(1, -1))
    return out.reshape(shp)


def rope_partial(x, pos):
    half = ROT_DIM // 2
    inv = ROPE_THETA ** (-2.0 * jnp.arange(half, dtype=jnp.float32) / ROT_DIM)
    ang = pos.astype(jnp.float32)[:, None] * inv[None, :]
    shape = (1, pos.shape[0]) + (1,) * (x.ndim - 3) + (half,)
    cos = jnp.cos(ang).reshape(shape)
    sin = jnp.sin(ang).reshape(shape)
    xf = x.astype(jnp.float32)
    x1, x2 = xf[..., :half], xf[..., half:ROT_DIM]
    out = jnp.concatenate([x1 * cos - x2 * sin, x2 * cos + x1 * sin, xf[..., ROT_DIM:]], axis=-1)
    return out.astype(x.dtype)


def masked_softmax(s, mask):
    s = jnp.where(mask, s.astype(jnp.float32), -jnp.inf)
    m = jnp.max(s, axis=-1, keepdims=True)
    m = jnp.where(jnp.isfinite(m), m, 0.0)
    e = jnp.where(mask, jnp.exp(s - m), 0.0)
    return e / jnp.maximum(jnp.sum(e, axis=-1, keepdims=True), 1e-30)


FF_CHUNK = 1408
SWIGLU_ROWS = 512
MOE_PARTS = 4


def _swiglu_kernel(blk_e_ref, n_used_ref, x_ref, wg_ref, wu_ref, wd_ref, *rest, has_res):
    o_ref = rest[-1]
    i = pl.program_id(0)
    f = pl.program_id(1)

    @pl.when(i < n_used_ref[0])
    def _():
        x = x_ref[...].astype(jnp.bfloat16)
        g = jnp.dot(x, wg_ref[0], preferred_element_type=jnp.float32)
        u = jnp.dot(x, wu_ref[0], preferred_element_type=jnp.float32)
        act = (g * jax.nn.sigmoid(g) * u).astype(jnp.bfloat16)
        y = jnp.dot(act, wd_ref[0], preferred_element_type=jnp.float32)

        @pl.when(f == 0)
        def _():
            o_ref[...] = y + rest[0][...] if has_res else y

        @pl.when(f > 0)
        def _():
            o_ref[...] += y

    @pl.when(i >= n_used_ref[0])
    def _():
        o_ref[...] = jnp.zeros(o_ref.shape, o_ref.dtype)


def grouped_swiglu(xb, blk_e, n_used, w_gu, w_down, res=None, into=None):
    rows, d = xb.shape
    b = min(SWIGLU_ROWS, rows)
    n_blk = rows // b
    ff = w_down.shape[1]
    tf = FF_CHUNK
    n_f = ff // tf
    chunk = lambda i, f, be, nu: jnp.where(i < nu[0], f, n_f - 1)
    in_specs = [
        pl.BlockSpec((b, d), lambda i, f, be, nu: (i, 0)),
        pl.BlockSpec((1, d, tf), lambda i, f, be, nu: (be[i], 0, chunk(i, f, be, nu))),
        pl.BlockSpec((1, d, tf), lambda i, f, be, nu: (be[i], 0, n_f + chunk(i, f, be, nu))),
        pl.BlockSpec((1, tf, d), lambda i, f, be, nu: (be[i], chunk(i, f, be, nu), 0)),
    ]
    operands = [xb, w_gu, w_gu, w_down]
    if res is not None:
        in_specs.append(pl.BlockSpec((b, d), lambda i, f, be, nu: (i, 0)))
        operands.append(res)
    out_rows, first_blk, aliases = rows, 0, {}
    if into is not None:
        prev, first_blk, out_rows = into
        if prev is not None:
            in_specs.append(pl.BlockSpec(memory_space=pl.ANY))
            aliases = {2 + len(operands): 0}
            operands.append(prev)
    return pl.pallas_call(
        functools.partial(_swiglu_kernel, has_res=res is not None),
        grid_spec=pltpu.PrefetchScalarGridSpec(
            num_scalar_prefetch=2,
            grid=(n_blk, n_f),
            in_specs=in_specs,
            out_specs=pl.BlockSpec((b, d), lambda i, f, be, nu: (i + first_blk, 0)),
        ),
        out_shape=jax.ShapeDtypeStruct((out_rows, d), jnp.float32),
        input_output_aliases=aliases,
        compiler_params=pltpu.CompilerParams(dimension_semantics=("arbitrary", "arbitrary"),
                                             vmem_limit_bytes=56 * 1024 * 1024),
        name="grouped_swiglu",
    )(blk_e.astype(jnp.int32), jnp.reshape(n_used, (1,)).astype(jnp.int32), *operands)


def swiglu(h, w_gu, w_down, precise=False):
    g, u = jnp.split(dense(h, w_gu, precise), 2, axis=-1)
    return dense(jax.nn.silu(g) * u, w_down, precise)


def swiglu_residual(x, h, w_gu, w_down):
    n_blk = h.shape[0] // min(SWIGLU_ROWS, h.shape[0])
    return grouped_swiglu(h, jnp.zeros((n_blk,), jnp.int32), jnp.int32(n_blk),
                          w_gu.astype(jnp.bfloat16)[None], w_down.astype(jnp.bfloat16)[None], res=x)


def _combine_norm_kernel(x_ref, y0_ref, y1_ref, g_ref, gain_ref, o_ref):
    g = g_ref[...]
    x = x_ref[...] + (y0_ref[...] * g[:, 0:1] + y1_ref[...] * g[:, 1:2])
    o_ref[...] = _rms(x) * gain_ref[...]


def combine_norm(x, y0, y1, gate, gain):
    m, d = x.shape
    tm = min(2 * ROW_TILE, m)
    rows = lambda width: pl.BlockSpec((tm, width), lambda i: (i, 0))
    return pl.pallas_call(
        _combine_norm_kernel,
        grid=(m // tm,),
        in_specs=[rows(d), rows(d), rows(d), rows(gate.shape[1]), pl.BlockSpec((1, d), lambda i: (0, 0))],
        out_specs=rows(d),
        out_shape=jax.ShapeDtypeStruct((m, d), jnp.float32),
        compiler_params=pltpu.CompilerParams(dimension_semantics=("parallel",)),
        name="combine_norm",
    )(x, y0, y1, gate, gain.reshape(1, d))


def moe_swiglu(xt, logits, w_gu_e, w_down_e):
    n_tok, d = xt.shape
    b = SWIGLU_ROWS
    experts = jnp.arange(N_EXPERTS)[None, :]
    e0 = jnp.argmax(logits, axis=-1)
    v0 = jnp.max(logits, axis=-1)
    rest = jnp.where(experts == e0[:, None], -jnp.inf, logits)
    e1 = jnp.argmax(rest, axis=-1)
    v1 = jnp.max(rest, axis=-1)
    top_idx = jnp.stack([e0, e1], axis=-1).astype(jnp.int32)
    gate = jax.nn.softmax(jnp.stack([v0, v1], axis=-1), axis=-1)
    nk = n_tok * TOP_K
    flat_e = top_idx.reshape(nk)
    onehot = (flat_e[:, None] == jnp.arange(N_EXPERTS)[None, :]).astype(jnp.int32)
    before = jnp.cumsum(onehot, axis=0) - onehot
    counts = jnp.sum(onehot, axis=0)
    padded = (counts + b - 1) // b * b
    ends_pad = jnp.cumsum(padded)
    starts_pad = ends_pad - padded
    dest = jnp.sum(onehot * (starts_pad[None, :] + before), axis=1)
    part_rows = MOE_PARTS * b
    n_rows = -(-((nk + b - 1) // b * b + N_EXPERTS * b) // part_rows) * part_rows
    n_blk = n_rows // b
    flat_tok = jnp.repeat(jnp.arange(n_tok, dtype=jnp.int32), TOP_K)
    row_tok = jnp.full((n_rows,), n_tok, jnp.int32).at[dest].set(flat_tok)
    blk_e = jnp.minimum(jnp.searchsorted(ends_pad, jnp.arange(n_blk) * b, side='right'), N_EXPERTS - 1)
    x_pad = jnp.concatenate([xt.astype(jnp.float32), jnp.zeros((1, d), jnp.float32)], axis=0)
    w_gu_b, w_down_b = w_gu_e.astype(jnp.bfloat16), w_down_e.astype(jnp.bfloat16)
    n_used = ends_pad[-1] // b
    per_part = n_blk // MOE_PARTS
    yb = None
    for i in range(MOE_PARTS):
        rows = x_pad[row_tok[i * per_part * b:(i + 1) * per_part * b]]
        yb = grouped_swiglu(rows, blk_e[i * per_part:(i + 1) * per_part],
                            jnp.clip(n_used - i * per_part, 0, per_part), w_gu_b, w_down_b,
                            into=(yb, i * per_part, n_rows))
    dest = dest.reshape(n_tok, TOP_K)
    return yb[dest[:, 0]], yb[dest[:, 1]], gate


WKV_CHUNK = 64
WKV_HEADS_PER_STEP = 16

_NN = (((1,), (0,)), ((), ()))
_NT = (((1,), (1,)), ((), ()))
_TN = (((0,), (0,)), ((), ()))


def _mm(x, y, dims):
    return lax.dot_general(x.astype(jnp.bfloat16), y.astype(jnp.bfloat16), dims,
                           preferred_element_type=jnp.float32)


def _wkv7_chunk_kernel(r_ref, lw_ref, k_ref, v_ref, kk_ref, a_ref, rk_ref, gnw_ref, gnb_ref, s0_ref,
                       y_ref, sout_ref, state_ref):
    c = pl.program_id(2)
    L = r_ref.shape[1]
    n_pairs = state_ref.shape[0]
    N = RWKV_HEAD
    W = 2 * N

    def block_diag(top, bottom):
        z = jnp.zeros((N, N), jnp.float32)
        return jnp.concatenate([jnp.concatenate([top, z], axis=1), jnp.concatenate([z, bottom], axis=1)], axis=0)

    @pl.when(c == 0)
    def _():
        for p in range(n_pairs):
            state_ref[p] = block_diag(s0_ref[0, 2 * p], s0_ref[0, 2 * p + 1])

    row2 = lax.broadcasted_iota(jnp.int32, (2 * L, W), 0)
    lane2 = lax.broadcasted_iota(jnp.int32, (2 * L, W), 1)
    own_lanes = (row2 // L) == (lane2 // N)
    rr = lax.broadcasted_iota(jnp.int32, (2 * L, 2 * L), 0)
    cc = lax.broadcasted_iota(jnp.int32, (2 * L, 2 * L), 1)
    same = (rr // L) == (cc // L)
    strict = same & (rr % L > cc % L)
    incl = same & (rr % L >= cc % L)
    wr = lax.broadcasted_iota(jnp.int32, (W, W), 0)
    wc = lax.broadcasted_iota(jnp.int32, (W, W), 1)
    eye_w = wr == wc
    ones_bd = ((wr // N) == (wc // N)).astype(jnp.bfloat16)
    tl = lax.broadcasted_iota(jnp.int32, (L, L), 0) >= lax.broadcasted_iota(jnp.int32, (L, L), 1)
    tri = tl.astype(jnp.bfloat16)

    def head_sum(x):
        hi, lo = _split_bf16(x)
        return (jnp.dot(hi, ones_bd, preferred_element_type=jnp.float32)
                + jnp.dot(lo, ones_bd, preferred_element_type=jnp.float32))

    def stack(x):
        return jnp.where(own_lanes, jnp.concatenate([x, x], axis=0), 0.0)

    unstack = lambda x: x[:L] + x[L:]

    lw = lw_ref[0]
    lw_hi, lw_lo = _split_bf16(lw)
    cum = (lax.dot_general(tri, lw_hi, _NN, preferred_element_type=jnp.float32)
           + lax.dot_general(tri, lw_lo, _NN, preferred_element_type=jnp.float32))
    cum_last = cum[L - 1:L, :]
    e_neg_all = jnp.exp(-cum)
    e_tail_all = jnp.exp(cum_last - cum)
    e_prev_all = jnp.exp(cum - lw)
    e_cum_all = jnp.exp(cum)
    wl_all = jnp.exp(cum_last)

    pairs = range(n_pairs)
    slab = lambda t, p: t[:, p * W:(p + 1) * W]
    r_in = [slab(r_ref[0], p) for p in pairs]
    k_in = [slab(k_ref[0], p) for p in pairs]
    v_in = [slab(v_ref[0], p) for p in pairs]
    ssq = [head_sum(slab(kk_ref[0], p) * slab(kk_ref[0], p)) for p in pairs]
    at_st, bt_st, bh_st, rt_st, kt_st, kh_st, v_st = [], [], [], [], [], [], []
    for p in pairs:
        kk = slab(kk_ref[0], p) / jnp.maximum(jnp.sqrt(ssq[p]), 1e-12)
        b = kk * slab(a_ref[0], p)
        at_st.append(stack(-kk * slab(e_prev_all, p)))
        bt_st.append(stack(b * slab(e_neg_all, p)))
        bh_st.append(stack(b * slab(e_tail_all, p)))
        rt_st.append(stack(r_in[p] * slab(e_cum_all, p)))
        kt_st.append(stack(k_in[p] * slab(e_neg_all, p)))
        kh_st.append(stack(k_in[p] * slab(e_tail_all, p)))
        v_st.append(stack(v_in[p]))
    a_ab = [jnp.where(strict, _mm(at_st[p], bt_st[p], _NT), 0.0) for p in pairs]
    a_ak = [jnp.where(strict, _mm(at_st[p], kt_st[p], _NT), 0.0) for p in pairs]
    r_b = [jnp.where(incl, _mm(rt_st[p], bt_st[p], _NT), 0.0) for p in pairs]
    r_k = [jnp.where(incl, _mm(rt_st[p], kt_st[p], _NT), 0.0) for p in pairs]
    av = [_mm(a_ak[p], v_st[p], _NN) for p in pairs]
    ht = [_mm(v_st[p], kh_st[p], _TN) for p in pairs]
    yp = [_mm(r_k[p], v_st[p], _NN) for p in pairs]
    pw = a_ab
    inv_a = a_ab
    n = 1
    while 2 * n < L:
        pw = [_mm(pw[p], pw[p], _NN) for p in pairs]
        inv_a = [inv_a[p] + pw[p] + _mm(inv_a[p], pw[p], _NN) for p in pairs]
        n *= 2
    ap = [at_st[p] + _mm(inv_a[p], at_st[p], _NN) for p in pairs]
    vp = [av[p] + _mm(inv_a[p], av[p], _NN) for p in pairs]
    g = [jnp.where(eye_w, slab(wl_all, p), 0.0) + _mm(bh_st[p], ap[p], _TN) for p in pairs]
    ht = [ht[p] + _mm(vp[p], bh_st[p], _TN) for p in pairs]
    rp = [unstack(rt_st[p] + _mm(r_b[p], ap[p], _NN)) for p in pairs]
    yp = [unstack(yp[p] + _mm(r_b[p], vp[p], _NN)) for p in pairs]
    s_prev = [state_ref[p] for p in pairs]
    y = [_mm(rp[p], s_prev[p], _NT) + yp[p] for p in pairs]
    for p in pairs:
        state_ref[p] = _mm(s_prev[p], g[p], _NT) + ht[p]
    mean = [head_sum(y[p]) * (1.0 / N) for p in pairs]
    cen = [y[p] - mean[p] for p in pairs]
    var = [head_sum(cen[p] * cen[p]) * (1.0 / N) for p in pairs]
    bonus = [head_sum(r_in[p] * k_in[p] * slab(rk_ref[...], p)) * v_in[p] for p in pairs]
    y_ref[0] = jnp.concatenate(
        [cen[p] * lax.rsqrt(var[p] + GN_EPS) * slab(gnw_ref[...], p) + slab(gnb_ref[...], p) + bonus[p]
         for p in pairs], axis=-1)

    @pl.when(c == pl.num_programs(2) - 1)
    def _():
        for p in pairs:
            s = state_ref[p]
            sout_ref[0, 2 * p] = s[:N, :N]
            sout_ref[0, 2 * p + 1] = s[N:, N:]


def wkv7_chunked(r, lw, k, v, kk, a_gate, r_k, gn_w, gn_b, s0):
    n, T, D = r.shape
    L = WKV_CHUNK
    hb = WKV_HEADS_PER_STEP
    w = hb * RWKV_HEAD
    seq = pl.BlockSpec((1, L, w), lambda b, h, c: (b, c, h))
    vec = pl.BlockSpec((1, w), lambda b, h, c: (0, h))
    st = pl.BlockSpec((1, hb, RWKV_HEAD, RWKV_HEAD), lambda b, h, c: (b, h, 0, 0))
    row = lambda t: t.reshape(1, D).astype(jnp.float32)
    return pl.pallas_call(
        _wkv7_chunk_kernel,
        grid=(n, D // w, T // L),
        in_specs=[seq] * 6 + [vec] * 3 + [st],
        out_specs=[seq, st],
        out_shape=[jax.ShapeDtypeStruct((n, T, D), jnp.float32),
                   jax.ShapeDtypeStruct(s0.shape, jnp.float32)],
        scratch_shapes=[pltpu.VMEM((hb // 2, 2 * RWKV_HEAD, 2 * RWKV_HEAD), jnp.float32)],
        compiler_params=pltpu.CompilerParams(dimension_semantics=("parallel", "parallel", "arbitrary")),
        name="wkv7_chunked",
    )(r, lw, k, v, kk, a_gate, row(r_k), row(gn_w), row(gn_b), s0)


def wkv7_scan(r, lw, k, v, a_vec, b_vec, s0):
    def step(S, inp):
        r_t, lw_t, k_t, v_t, a_t, b_t = inp
        sa = jnp.sum(S * a_t[:, :, None, :], axis=-1)
        S = S * jnp.exp(lw_t)[:, :, None, :] + sa[..., None] * b_t[:, :, None, :] + v_t[..., None] * k_t[:, :, None, :]
        return S, jnp.sum(S * r_t[:, :, None, :], axis=-1)

    xs = tuple(jnp.moveaxis(t, 1, 0) for t in (r, lw, k, v, a_vec, b_vec))
    s_fin, ys = lax.scan(step, s0, xs)
    return jnp.moveaxis(ys, 0, 1), s_fin


ROW_TILE = 256


def _bdot(x, w):
    return jnp.dot(x.astype(jnp.bfloat16), w, preferred_element_type=jnp.float32)


def _rms(x):
    return x * lax.rsqrt(jnp.mean(x * x, axis=-1, keepdims=True) + NORM_EPS)


def _rwkv_pre_kernel(x_ref, shift_ref, gain_ref, mu_ref, vec_ref, wrkv_ref, w1_ref, a1_ref, g1_ref,
                     w2_ref, a2_ref, g2_ref,
                     r_ref, lw_ref, k_ref, v_ref, kk_ref, a_ref, g_ref, hlast_ref, prev_ref):
    @pl.when(pl.program_id(1) == 0)
    def _():
        prev_ref[...] = shift_ref[0]

    tm = x_ref.shape[1]
    h = _rms(x_ref[0]) * gain_ref[...]
    first_row = lax.broadcasted_iota(jnp.int32, h.shape, 0) == 0
    h_prev = jnp.where(first_row, prev_ref[...], pltpu.roll(h, 1, 0))
    prev_ref[...] = h[tm - 1:tm, :]
    hlast_ref[0] = h[tm - 1:tm, :]
    dx = h_prev - h
    mix = lambda i: (h + dx * mu_ref[i:i + 1, :]).astype(jnp.bfloat16)
    w0, a0, k_k, k_a = (vec_ref[i:i + 1, :] for i in range(4))
    r_ref[0] = _bdot(mix(0), wrkv_ref[0])
    k = _bdot(mix(2), wrkv_ref[1])
    v_ref[0] = _bdot(mix(3), wrkv_ref[2])
    lw_ref[0] = -DECAY_SCALE * jax.nn.sigmoid(w0 + _bdot(jnp.tanh(_bdot(mix(1), w1_ref[...])), w2_ref[...]))
    a = jax.nn.sigmoid(a0 + _bdot(_bdot(mix(4), a1_ref[...]), a2_ref[...]))
    g_ref[0] = _bdot(jax.nn.sigmoid(_bdot(mix(5), g1_ref[...])), g2_ref[...])
    a_ref[0] = a
    kk_ref[0] = k * k_k
    k_ref[0] = k * (1.0 + (a - 1.0) * k_a)


def _out_proj_kernel(*refs, n_terms, gated, routed):
    terms = refs[:n_terms]
    rest = refs[n_terms:]
    if gated:
        gate_ref, rest = rest[0], rest[1:]
    x_ref, w_ref, gain_ref = rest[:3]
    rest = rest[3:]
    if routed:
        wr_ref, rest = rest[0], rest[1:]
    xo_ref, ho_ref = rest[:2]
    y = terms[0][...]
    for t in terms[1:]:
        y = y + t[...]
    if gated:
        y = y * gate_ref[...]
    xo = x_ref[...] + _bdot(y, w_ref[...])
    xo_ref[...] = xo
    h = _rms(xo) * gain_ref[...]
    ho_ref[...] = h.astype(ho_ref.dtype)
    if routed:
        h_hi, h_lo = _split_bf16(h)
        w_hi, w_lo = _split_bf16(wr_ref[...])
        dot = functools.partial(jnp.dot, preferred_element_type=jnp.float32)
        rest[2][...] = dot(h_hi, w_hi) + (dot(h_hi, w_lo) + dot(h_lo, w_hi))


def out_proj(terms, gate, x, w, gain, w_router=None):
    m, d = x.shape
    kdim = w.shape[0]
    tm = min(ROW_TILE, m)
    rows = lambda width: pl.BlockSpec((tm, width), lambda i: (i, 0))
    full = lambda a: pl.BlockSpec(a.shape, lambda i: (0, 0))
    ins = list(terms) + ([gate] if gate is not None else [])
    consts = [w.astype(jnp.bfloat16), gain.reshape(1, d)]
    out_specs = [rows(d), rows(d)]
    out_shape = [jax.ShapeDtypeStruct((m, d), jnp.float32), jax.ShapeDtypeStruct((m, d), jnp.bfloat16)]
    if w_router is not None:
        consts.append(jnp.pad(w_router, ((0, 0), (0, 128 - w_router.shape[1]))))
        out_specs.append(rows(128))
        out_shape.append(jax.ShapeDtypeStruct((m, 128), jnp.float32))
    return pl.pallas_call(
        functools.partial(_out_proj_kernel, n_terms=len(terms), gated=gate is not None,
                          routed=w_router is not None),
        grid=(m // tm,),
        in_specs=[rows(kdim)] * len(ins) + [rows(d)] + [full(a) for a in consts],
        out_specs=out_specs,
        out_shape=out_shape,
        compiler_params=pltpu.CompilerParams(dimension_semantics=("parallel",),
                                             vmem_limit_bytes=48 * 1024 * 1024),
        name="out_proj",
    )(*ins, x, *consts)


def rwkv7_layer(x, shift0, s0, gain, mu, w_rkv, w0, w1, w2, a0, a1, a2, g1, g2, k_k, k_a, r_k, gn_w, gn_b, w_o,
                gain_next):
    n, T, D = x.shape
    tm = min(ROW_TILE, T)
    bf = lambda t: t.astype(jnp.bfloat16)
    seq = pl.BlockSpec((1, tm, D), lambda b, t: (b, t, 0))
    full = lambda a: pl.BlockSpec(a.shape, lambda b, t: (0,) * a.ndim)
    per_seq = pl.BlockSpec((1, 1, D), lambda b, t: (b, 0, 0))
    vecs = jnp.stack([w0, a0, k_k, k_a]).astype(jnp.float32)
    weights = [bf(w_rkv), bf(w1), bf(a1), bf(g1), bf(w2), bf(a2), bf(g2)]
    small = [gain.reshape(1, D), mu, vecs]
    outs = pl.pallas_call(
        _rwkv_pre_kernel,
        grid=(n, T // tm),
        in_specs=[seq, per_seq] + [full(a) for a in small + weights],
        out_specs=[seq] * 7 + [per_seq],
        out_shape=[jax.ShapeDtypeStruct((n, T, D), jnp.float32)] * 7
                  + [jax.ShapeDtypeStruct((n, 1, D), jnp.float32)],
        scratch_shapes=[pltpu.VMEM((1, D), jnp.float32)],
        compiler_params=pltpu.CompilerParams(dimension_semantics=("parallel", "arbitrary"),
                                             vmem_limit_bytes=56 * 1024 * 1024),
        name="rwkv_pre",
    )(x, shift0.reshape(n, 1, D), *small, *weights)
    r, lw, k, v, kk, a_gate, g, h_last = outs
    y, s_fin = wkv7_chunked(r, lw, k, v, kk, a_gate, r_k.reshape(-1), gn_w, gn_b, s0.astype(jnp.float32))
    flat = lambda t: t.reshape(n * T, D)
    x1, h1 = out_proj([flat(y)], flat(g), flat(x), w_o, gain_next)
    return x1, h1, s_fin, h_last.reshape(n, D)


def rwkv7_time_mix(h, h_prev, s0, mu, w_rkv, w0, w1, w2, a0, a1, a2, g1, g2, k_k, k_a, r_k, gn_w, gn_b, w_o,
                   precise=False):
    n, T, D = h.shape
    f32 = jnp.float32
    mm = functools.partial(dense, precise=precise)
    dx = jnp.concatenate([h_prev[:, None, :].astype(h.dtype), h[:, :-1]], axis=1) - h
    xr, xw, xk, xv, xa, xg = (h + dx * mu[i] for i in range(6))
    r = mm(xr, w_rkv[0])
    k = mm(xk, w_rkv[1])
    v = mm(xv, w_rkv[2])
    log_decay = -DECAY_SCALE * jax.nn.sigmoid((w0 + mm(jnp.tanh(mm(xw, w1)), w2)).astype(f32))
    a = jax.nn.sigmoid((a0 + mm(mm(xa, a1), a2)).astype(f32))
    g = mm(jax.nn.sigmoid(mm(xg, g1)), g2)
    heads = lambda t: t.astype(f32).reshape(n, T, RWKV_HEADS, RWKV_HEAD)
    kk = heads(k * k_k)
    kk = kk / jnp.maximum(jnp.sqrt(jnp.sum(kk * kk, axis=-1, keepdims=True)), 1e-12)
    a_h = heads(a)
    k_h = heads(k.astype(f32) * (1.0 + (a - 1.0) * k_a.astype(f32)))
    r_h, v_h = heads(r), heads(v)
    y, s_fin = wkv7_scan(r_h, heads(log_decay), k_h, v_h, -kk, kk * a_h, s0.astype(f32))
    mean = jnp.mean(y, axis=-1, keepdims=True)
    var = jnp.mean(jnp.square(y - mean), axis=-1, keepdims=True)
    y = ((y - mean) * lax.rsqrt(var + GN_EPS)).reshape(n, T, D) * gn_w.astype(f32) + gn_b.astype(f32)
    bonus = jnp.sum(r_h * k_h * r_k.astype(f32), axis=-1, keepdims=True) * v_h
    y = (y + bonus.reshape(n, T, D)).astype(h.dtype)
    return mm(y * g, w_o), s_fin, h[:, -1]


def shared_kv_rows(x, norm_kv, w_kv, pos, precise=False):
    n, T, _ = x.shape
    kv = dense(rmsnorm(x, norm_kv), w_kv, precise).reshape(n, T, N_BRANCH, 2, NSA_KV_GROUPS, NSA_HEAD_DIM)
    cmp_kv = kv[:, :, 0]
    slc_kv = jnp.stack([rope_partial(kv[:, :, 1, 0], pos), kv[:, :, 1, 1]], axis=2)
    win_kv = jnp.stack([rope_partial(kv[:, :, 2, 0], pos), kv[:, :, 2, 1]], axis=2)
    return cmp_kv, slc_kv, win_kv


CHUNK_LANES = CMP_STRIDE * 2 * NSA_KV_GROUPS * NSA_HEAD_DIM
HIDDEN_LANES = 2 * NSA_KV_GROUPS * 2 * CMP_HIDDEN
PAGES_PER_STEP = 32
CHUNKS_PER_PAGE = PAGE_SIZE // CMP_STRIDE


def _chunk_weights(cmp_w1):
    w = cmp_w1.reshape(2, 2, CMP_STRIDE, NSA_HEAD_DIM, CMP_HIDDEN)
    w = w.transpose(2, 0, 3, 1, 4)
    eye = jnp.eye(2, dtype=w.dtype)
    big = jnp.einsum('ab,cf,jaehk->jacebfhk', eye, jnp.eye(NSA_KV_GROUPS, dtype=w.dtype), w)
    return big.reshape(CHUNK_LANES, HIDDEN_LANES).astype(jnp.bfloat16)


def _token_weights(cmp_w1):
    w = cmp_w1.reshape(2, 2, CMP_STRIDE, NSA_HEAD_DIM, CMP_HIDDEN)
    return w.transpose(2, 0, 3, 1, 4).reshape(CMP_STRIDE, 2, NSA_HEAD_DIM, 2 * CMP_HIDDEN).astype(jnp.bfloat16)


def _paged_chunk_kernel(pt_ref, *refs):
    pages, w_ref, o_ref, xt_ref = (refs[:PAGES_PER_STEP], refs[PAGES_PER_STEP], refs[PAGES_PER_STEP + 1],
                                   refs[PAGES_PER_STEP + 2])
    planes = [(kv, g) for kv in range(2) for g in range(NSA_KV_GROUPS)]
    tok = lax.broadcasted_iota(jnp.int32, (PAGE_SIZE, PAGE_SIZE), 0)
    dst = lax.broadcasted_iota(jnp.int32, (PAGE_SIZE, PAGE_SIZE), 1)
    regroup = (tok == (dst % CHUNKS_PER_PAGE) * CMP_STRIDE + dst // CHUNKS_PER_PAGE).astype(jnp.bfloat16)
    for p in range(PAGES_PER_STEP):
        for i, (kv, g) in enumerate(planes):
            xt_ref[p, i] = _bdot(pages[p][0, kv, g], regroup).T
    width = 2 * CMP_HIDDEN
    for i, (kv, g) in enumerate(planes):
        acc = None
        for j in range(CMP_STRIDE):
            rows = [xt_ref[p, i, j * CHUNKS_PER_PAGE:(j + 1) * CHUNKS_PER_PAGE, :] for p in range(PAGES_PER_STEP)]
            y = _bdot(jnp.concatenate(rows, axis=0), w_ref[j, kv])
            acc = y if acc is None else acc + y
        o_ref[0, :, i * width:(i + 1) * width] = acc


def paged_chunk_hidden(cache, page_table, w_token):
    n, n_pages = page_table.shape
    planes = cache.transpose(0, 2, 3, 4, 1)
    rows = PAGES_PER_STEP * CHUNKS_PER_PAGE

    def page_spec(k):
        return pl.BlockSpec((1,) + planes.shape[1:], lambda b, s, pt: (pt[b, s * PAGES_PER_STEP + k], 0, 0, 0, 0))

    return pl.pallas_call(
        _paged_chunk_kernel,
        grid_spec=pltpu.PrefetchScalarGridSpec(
            num_scalar_prefetch=1,
            grid=(n, n_pages // PAGES_PER_STEP),
            in_specs=[page_spec(k) for k in range(PAGES_PER_STEP)]
                     + [pl.BlockSpec(w_token.shape, lambda b, s, pt: (0, 0, 0, 0))],
            out_specs=pl.BlockSpec((1, rows, HIDDEN_LANES), lambda b, s, pt: (b, s, 0)),
            scratch_shapes=[pltpu.VMEM((PAGES_PER_STEP, 2 * NSA_KV_GROUPS, PAGE_SIZE, NSA_HEAD_DIM), jnp.float32)],
        ),
        out_shape=jax.ShapeDtypeStruct((n, n_pages * CHUNKS_PER_PAGE, HIDDEN_LANES), jnp.float32),
        compiler_params=pltpu.CompilerParams(dimension_semantics=("parallel", "arbitrary"),
                                             vmem_limit_bytes=48 * 1024 * 1024),
        name="paged_chunk_hidden",
    )(page_table.astype(jnp.int32), *([planes] * PAGES_PER_STEP), w_token)


def _matmul_kernel(x_ref, w_ref, o_ref):
    o_ref[...] = jnp.dot(x_ref[...].astype(jnp.bfloat16), w_ref[...],
                         preferred_element_type=jnp.float32).astype(o_ref.dtype)


def matmul(x, w, rows_per_step=512, out_dtype=jnp.float32):
    m, k = x.shape
    nn = w.shape[1]
    tm = min(rows_per_step, m)
    return pl.pallas_call(
        _matmul_kernel,
        grid=(m // tm,),
        in_specs=[pl.BlockSpec((tm, k), lambda i: (i, 0)), pl.BlockSpec((k, nn), lambda i: (0, 0))],
        out_specs=pl.BlockSpec((tm, nn), lambda i: (i, 0)),
        out_shape=jax.ShapeDtypeStruct((m, nn), out_dtype),
        compiler_params=pltpu.CompilerParams(dimension_semantics=("parallel",),
                                             vmem_limit_bytes=48 * 1024 * 1024),
        name="matmul",
    )(x, w.astype(jnp.bfloat16))


def _split_bf16(x):
    hi = x.astype(jnp.bfloat16)
    return hi, (x - hi.astype(jnp.float32)).astype(jnp.bfloat16)


def _matmul3_kernel(x_ref, w_ref, o_ref):
    x_hi, x_lo = _split_bf16(x_ref[...])
    w_hi, w_lo = _split_bf16(w_ref[...])
    dot = functools.partial(jnp.dot, preferred_element_type=jnp.float32)
    o_ref[...] = dot(x_hi, w_hi) + (dot(x_hi, w_lo) + dot(x_lo, w_hi))


def matmul_precise(x, w, cols_per_step=512):
    m, k = x.shape
    nn = w.shape[1]
    pad = (-nn) % 128
    if pad:
        w = jnp.pad(w, ((0, 0), (0, pad)))
    tn = math.gcd(cols_per_step, nn + pad)
    tm = min(m, 512)
    out = pl.pallas_call(
        _matmul3_kernel,
        grid=(m // tm, (nn + pad) // tn),
        in_specs=[pl.BlockSpec((tm, k), lambda i, j: (i, 0)), pl.BlockSpec((k, tn), lambda i, j: (0, j))],
        out_specs=pl.BlockSpec((tm, tn), lambda i, j: (i, j)),
        out_shape=jax.ShapeDtypeStruct((m, nn + pad), jnp.float32),
        compiler_params=pltpu.CompilerParams(dimension_semantics=("parallel", "parallel")),
        name="matmul_precise",
    )(x, w)
    return out[:, :nn] if pad else out


def dense(x, w, precise=False):
    if precise:
        return matmul_precise(x.reshape(-1, x.shape[-1]), w).reshape(x.shape[:-1] + (w.shape[1],))
    return x @ w


def compress_from_hidden(hidden, pos_emb, w1, b1, w2):
    n, C = hidden.shape[:2]
    rows = -(-C // 8) * 8
    if rows != C:
        hidden = jnp.pad(hidden, ((0, 0), (0, rows - C), (0, 0)))
    w1r = w1.reshape(2, CMP_BLOCK, NSA_HEAD_DIM, CMP_HIDDEN)
    bias = jnp.einsum('ajd,ajdh->ah', pos_emb, w1r) + b1
    out = pl.pallas_call(
        functools.partial(_compress_finish_kernel, n_blocks=C),
        grid=(n,),
        in_specs=[pl.BlockSpec((1, rows, HIDDEN_LANES), lambda b: (b, 0, 0)),
                  pl.BlockSpec(bias.shape, lambda b: (0, 0)), pl.BlockSpec(w2.shape, lambda b: (0, 0, 0))],
        out_specs=pl.BlockSpec((1, rows, 2 * SLAB), lambda b: (b, 0, 0)),
        out_shape=jax.ShapeDtypeStruct((n, rows, 2 * SLAB), jnp.float32),
        compiler_params=pltpu.CompilerParams(dimension_semantics=("parallel",)),
        name="compress_finish",
    )(hidden, bias, w2.astype(jnp.bfloat16))[:, :C]
    shape = (n, C, NSA_KV_GROUPS, NSA_HEAD_DIM)
    return out[:, :, :SLAB].reshape(shape), out[:, :, SLAB:].reshape(shape)


def _compress_finish_kernel(h_ref, bias_ref, w2_ref, o_ref, *, n_blocks):
    h = h_ref[0]
    rows = h.shape[0]
    keep = lax.broadcasted_iota(jnp.int32, (rows, CMP_HIDDEN), 0) < n_blocks - 1
    outs = []
    for i in range(2 * NSA_KV_GROUPS):
        kv = i // NSA_KV_GROUPS
        first = h[:, 2 * i * CMP_HIDDEN:(2 * i + 1) * CMP_HIDDEN]
        second = h[:, (2 * i + 1) * CMP_HIDDEN:(2 * i + 2) * CMP_HIDDEN]
        nxt = jnp.where(keep, pltpu.roll(second, rows - 1, 0), 0.0)
        act = jax.nn.gelu(first + nxt + bias_ref[kv:kv + 1, :])
        outs.append(_bdot(act, w2_ref[kv]))
    o_ref[0] = jnp.concatenate(outs, axis=-1)


def rows_chunk_hidden(kv_rows, w_chunk):
    n, T = kv_rows.shape[:2]
    chunks = kv_rows.reshape(n * T // CMP_STRIDE, CHUNK_LANES)
    return matmul(chunks, w_chunk, rows_per_step=256).reshape(n, T // CMP_STRIDE, HIDDEN_LANES)


def nsa_query(h, w_in, precise=False):
    n, T, _ = h.shape
    proj = dense(h, w_in, precise)
    q = proj[..., :NSA_HEADS * NSA_HEAD_DIM].reshape(n, T, NSA_KV_GROUPS, NSA_GROUP_HEADS, NSA_HEAD_DIM)
    gate = proj[..., NSA_HEADS * NSA_HEAD_DIM:].reshape(n, T, NSA_KV_GROUPS, NSA_GROUP_HEADS, N_BRANCH)
    return q, gate


def compressed_branch_and_selection(q, t_pos, kc, vc):
    scale = NSA_HEAD_DIM ** -0.5
    n_c = kc.shape[1]
    vis = (jnp.arange(n_c) * CMP_STRIDE + CMP_BLOCK - 1)[None, :] <= t_pos[:, None]
    p_cmp = masked_softmax(jnp.einsum('nqghd,ncgd->nghqc', q, kc) * scale, vis)
    o_cmp = jnp.einsum('nghqc,ncgd->nqghd', p_cmp.astype(vc.dtype), vc)
    ratio = SLC_BLOCK // CMP_STRIDE
    lead = CMP_BLOCK // CMP_STRIDE - 1
    n_s = n_c // ratio
    pg = jnp.pad(jnp.sum(p_cmp, axis=2), ((0, 0), (0, 0), (0, 0), (lead, 0)))
    p_slc = pg[..., 0:ratio * n_s:ratio]
    for o in range(1, ratio + lead):
        p_slc = p_slc + pg[..., o:o + ratio * n_s:ratio]
    jb = jnp.arange(n_s)[None, :]
    jt = (t_pos // SLC_BLOCK)[:, None]
    forced = (jb == 0) | (jb == jt) | (jb == jt - 1)
    score = jnp.where(jb > jt, -jnp.inf, jnp.where(forced, FORCED_SCORE, p_slc))
    before = (score[..., :, None] > score[..., None, :]) | (
        (score[..., :, None] == score[..., None, :]) & (jb[0][:, None] < jb[0][None, :]))
    rank = jnp.sum(before, axis=-2)
    slots = jnp.arange(min(SLC_TOPN, n_s))
    idx = jnp.sum(jnp.where(rank[..., None, :] == slots[:, None], jb[0], 0), axis=-1)
    return o_cmp, idx


def _dot3(x, y, dims):
    x_hi, x_lo = _split_bf16(x)
    y_hi, y_lo = _split_bf16(y)
    dot = lambda a, b: lax.dot_general(a, b, dims, preferred_element_type=jnp.float32)
    return dot(x_hi, y_hi) + (dot(x_hi, y_lo) + dot(x_lo, y_hi))


def _decode_kernel(half_ref, new_ref, phys_ref, q_ref, rows_ref, win_ref, *refs):
    b = pl.program_id(0)
    n_sel = len(refs) - 1
    pages, o_ref = refs[:n_sel], refs[n_sel]
    k_per_group = n_sel // NSA_KV_GROUPS
    hg, dh = NSA_GROUP_HEADS, NSA_HEAD_DIM
    tok_half = lax.broadcasted_iota(jnp.int32, (hg, PAGE_SIZE), 1) // SLC_BLOCK
    rows = rows_ref[0]

    def attend(qg, scores, values, k_new, v_new, new_ok):
        s_new = jnp.where(new_ok, jnp.sum(qg * k_new, axis=-1, keepdims=True), MASKED)
        m = s_new
        for s in scores:
            m = jnp.maximum(m, jnp.max(s, axis=-1, keepdims=True))
        e_new = jnp.exp(s_new - m)
        total, out = e_new, e_new * v_new
        for s, v in zip(scores, values):
            e = jnp.exp(s - m)
            total = total + jnp.sum(e, axis=-1, keepdims=True)
            out = out + _dot3(e, v, _NT)
        return out / jnp.maximum(total, 1e-30)

    outs = []
    for g in range(NSA_KV_GROUPS):
        qg = q_ref[0, g * hg:(g + 1) * hg, :]
        lanes = slice(g * dh, (g + 1) * dh)
        scores, values = [], []
        any_new = jnp.int32(0)
        for k in range(k_per_group):
            j = g * k_per_group + k
            plane = pages[j][0]
            is_new = new_ref[b, j]
            ok = (tok_half == half_ref[b, j]) & (is_new == 0)
            scores.append(jnp.where(ok, _dot3(qg, plane[0, 0], _NN), MASKED))
            values.append(plane[1, 0])
            any_new = jnp.maximum(any_new, is_new)
        o_slc = attend(qg, scores, values, rows[0:1, lanes], rows[1:2, lanes], any_new > 0)
        o_win = attend(qg, [_dot3(qg, win_ref[0, 0, g], _NN)], [win_ref[0, 1, g]],
                       rows[2:3, lanes], rows[3:4, lanes], True)
        outs.append(jnp.concatenate([o_slc, o_win], axis=-1))
    o_ref[0] = jnp.concatenate(outs, axis=0)


def decode_attention(q_rot, idx, page_table, cache_slc_kv, cache_win_kv, slc_new, win_new):
    n = q_rot.shape[0]
    sub = PAGE_SIZE // SLC_BLOCK
    n_past_blk = PAST_LEN // SLC_BLOCK
    assert cache_win_kv.shape[1] <= WINDOW and cache_win_kv.shape[1] <= PAST_LEN
    k_sel = idx.shape[-1]
    flat = idx.reshape(n, NSA_KV_GROUPS * k_sel)
    past = jnp.minimum(flat, n_past_blk - 1)
    hit = (past // sub)[:, :, None] == jnp.arange(page_table.shape[1])[None, None, :]
    phys = jnp.sum(jnp.where(hit, page_table[:, None, :], 0), axis=-1).astype(jnp.int32)
    half = (past % sub).astype(jnp.int32)
    is_new = (flat >= n_past_blk).astype(jnp.int32)
    pool = cache_slc_kv.transpose(0, 2, 3, 4, 1)
    window = cache_win_kv.transpose(0, 2, 3, 4, 1)
    rows = jnp.concatenate([slc_new.reshape(n, 2, SLAB), win_new.reshape(n, 2, SLAB)], axis=1)

    def page_spec(j):
        g = j // k_sel
        return pl.BlockSpec((1, 2, 1, NSA_HEAD_DIM, PAGE_SIZE), lambda b, hf, nw, ph: (ph[b, j], 0, g, 0, 0))

    out = pl.pallas_call(
        _decode_kernel,
        grid_spec=pltpu.PrefetchScalarGridSpec(
            num_scalar_prefetch=3,
            grid=(n,),
            in_specs=[pl.BlockSpec((1,) + q_rot.shape[1:], lambda b, hf, nw, ph: (b, 0, 0)),
                      pl.BlockSpec((1,) + rows.shape[1:], lambda b, hf, nw, ph: (b, 0, 0)),
                      pl.BlockSpec((1,) + window.shape[1:], lambda b, hf, nw, ph: (b, 0, 0, 0, 0))]
                     + [page_spec(j) for j in range(NSA_KV_GROUPS * k_sel)],
            out_specs=pl.BlockSpec((1, NSA_HEADS, 2 * NSA_HEAD_DIM), lambda b, hf, nw, ph: (b, 0, 0)),
        ),
        out_shape=jax.ShapeDtypeStruct((n, NSA_HEADS, 2 * NSA_HEAD_DIM), jnp.float32),
        compiler_params=pltpu.CompilerParams(dimension_semantics=("parallel",)),
        name="decode_attention",
    )(half, is_new, phys, q_rot, rows, window, *([pool] * (NSA_KV_GROUPS * k_sel)))
    return out[:, :, :NSA_HEAD_DIM], out[:, :, NSA_HEAD_DIM:]


NSA_SLABS = NSA_GROUP_HEADS
NSA_TILES = NSA_SLABS * NSA_KV_GROUPS
SELECTED_KEY_BLOCK = 512
DENSE_BATCH = 16
MASKED = -1e30
SLC_PER_CMP = SLC_BLOCK // CMP_STRIDE
CMP_LEAD = CMP_BLOCK // CMP_STRIDE - 1
SLAB = NSA_KV_GROUPS * NSA_HEAD_DIM
SEL_LANES = 64


def _group_tiles(q_ref, qs_ref):
    tq = q_ref.shape[1]
    lane = lax.broadcasted_iota(jnp.int32, (tq, SLAB), 1)
    for i in range(NSA_SLABS):
        qs = q_ref[0, :, i * 128:(i + 1) * 128]
        for g in range(NSA_KV_GROUPS):
            in_group = (lane >= g * NSA_HEAD_DIM) & (lane < (g + 1) * NSA_HEAD_DIM)
            qs_ref[2 * i + g] = jnp.where(in_group, qs, jnp.zeros_like(qs)).astype(jnp.bfloat16)


def _merge_groups(o_ref, tiles, gate_ref, branch):
    tq = tiles[0].shape[0]
    lane = lax.broadcasted_iota(jnp.int32, (tq, SLAB), 1)
    gates = gate_ref[0]
    col = lambda g, i: (g * NSA_GROUP_HEADS + i) * N_BRANCH + branch
    for i in range(NSA_SLABS):
        lo = tiles[2 * i] * gates[:, col(0, i):col(0, i) + 1]
        hi = tiles[2 * i + 1] * gates[:, col(1, i):col(1, i) + 1]
        o_ref[0, :, i * 128:(i + 1) * 128] = jnp.where(lane < NSA_HEAD_DIM, lo, hi)


def _nsa_cmp_kernel(q_ref, kc_ref, vc_ref, gate_ref, o_ref, sel_ref, qs_ref):
    qi = pl.program_id(1)
    tq = q_ref.shape[1]
    nc = kc_ref.shape[1]
    ns = nc // SLC_PER_CMP
    t0 = qi * tq
    _group_tiles(q_ref, qs_ref)
    kc = kc_ref[0]
    vc = vc_ref[0]
    t_row = t0 + lax.broadcasted_iota(jnp.int32, (tq, nc), 0)
    c_pos = lax.broadcasted_iota(jnp.int32, (tq, nc), 1) * CMP_STRIDE + (CMP_BLOCK - 1)
    vis = c_pos <= t_row
    tiles = range(NSA_TILES)
    s = [_mm(qs_ref[r], kc, _NT) for r in tiles]
    p = []
    for r in tiles:
        sr = jnp.where(vis, s[r], MASKED)
        m = jnp.max(sr, axis=-1, keepdims=True)
        e = jnp.where(vis, jnp.exp(sr - m), 0.0)
        p.append(e / jnp.maximum(jnp.sum(e, axis=-1, keepdims=True), 1e-30))
    _merge_groups(o_ref, [_mm(p[r], vc, _NN) for r in tiles], gate_ref, 0)

    jrow = lax.broadcasted_iota(jnp.int32, (ns, nc), 0)
    ccol = lax.broadcasted_iota(jnp.int32, (ns, nc), 1)
    pool = ((ccol >= SLC_PER_CMP * jrow - CMP_LEAD) & (ccol < SLC_PER_CMP * (jrow + 1))).astype(jnp.bfloat16)
    jb = lax.broadcasted_iota(jnp.int32, (ns, tq), 0)
    jt = (t0 + lax.broadcasted_iota(jnp.int32, (ns, tq), 1)) // SLC_BLOCK
    forced = (jb == 0) | (jb == jt) | (jb == jt - 1)
    sel_t = []
    for g in range(NSA_KV_GROUPS):
        pg = p[g]
        for i in range(1, NSA_SLABS):
            pg = pg + p[2 * i + g]
        pg_hi = pg.astype(jnp.bfloat16)
        pg_lo = (pg - pg_hi.astype(jnp.float32)).astype(jnp.bfloat16)
        p_slc = (lax.dot_general(pool, pg_hi, _NT, preferred_element_type=jnp.float32)
                 + lax.dot_general(pool, pg_lo, _NT, preferred_element_type=jnp.float32))
        score = jnp.where(jb > jt, -jnp.inf, jnp.where(forced, FORCED_SCORE, p_slc))
        rank = jnp.zeros((ns, tq), jnp.float32)
        for i in range(ns):
            row = score[i:i + 1, :]
            tie = jnp.where(jb > i, 1.0, 0.0)
            rank = rank + jnp.where(row > score, 1.0, 0.0) + jnp.where(row == score, tie, 0.0)
        sel_t.append(jnp.where(rank < min(SLC_TOPN, ns), 1.0, 0.0))
        if ns < SEL_LANES:
            sel_t.append(jnp.zeros((SEL_LANES - ns, tq), jnp.float32))
    sel_ref[0] = jnp.concatenate(sel_t, axis=0).T.astype(sel_ref.dtype)


def _nsa_dense_kernel(q_ref, k_ref, vt_ref, sel_ref, gate_ref, o_ref, qs_ref, m_ref, acc_ref, *, windowed, kblock):
    qi = pl.program_id(1)
    tq = q_ref.shape[1]
    t0 = qi * tq
    _group_tiles(q_ref, qs_ref)
    m_ref[...] = jnp.full(m_ref.shape, MASKED, jnp.float32)
    acc_ref[...] = jnp.zeros(acc_ref.shape, jnp.float32)
    k_row = lax.broadcasted_iota(jnp.int32, (kblock, tq), 0)
    t_lane = t0 + lax.broadcasted_iota(jnp.int32, (kblock, tq), 1)
    tiles = range(NSA_TILES)
    first = 0
    last = 0 if windowed else (t0 + tq - 1) // kblock

    def key_block(kb, carry):
        start = pl.multiple_of(jnp.maximum(t0 - WINDOW, 0) if windowed else kb * kblock, tq)
        kblk = k_ref[0, pl.ds(start, kblock), :]
        vt = vt_ref[0, :, pl.ds(start, kblock)]
        k_pos = start + k_row
        if windowed:
            allowed = [(k_pos <= t_lane) & (k_pos >= t_lane - WINDOW)] * NSA_KV_GROUPS
        else:
            sel = sel_ref[0]
            n_sel = NSA_KV_GROUPS * SEL_LANES
            e_key = lax.broadcasted_iota(jnp.int32, (kblock, n_sel), 0)
            e_lane = lax.broadcasted_iota(jnp.int32, (kblock, n_sel), 1)
            blk = kb * (kblock // SLC_BLOCK) + e_key // SLC_BLOCK
            allowed = []
            for g in range(NSA_KV_GROUPS):
                expand = (e_lane == blk + g * SEL_LANES).astype(jnp.bfloat16)
                picked = lax.dot_general(expand, sel, _NT, preferred_element_type=jnp.float32)
                allowed.append((picked > 0.5) & (k_pos <= t_lane))
        bias = [jnp.where(ok, 0.0, MASKED) for ok in allowed]
        d_row = lax.broadcasted_iota(jnp.int32, vt.shape, 0)
        vt_g = [jnp.where((d_row >= g * NSA_HEAD_DIM) & (d_row < (g + 1) * NSA_HEAD_DIM), vt, jnp.ones_like(vt))
                for g in range(NSA_KV_GROUPS)]
        s = [_mm(kblk, qs_ref[r], _NT) for r in tiles]
        for lo in range(0, NSA_TILES, DENSE_BATCH):
            batch = range(lo, lo + DENSE_BATCH)
            p, alpha = {}, {}
            for r in batch:
                sr = s[r] + bias[r % NSA_KV_GROUPS]
                m_prev = m_ref[r]
                m_new = jnp.maximum(m_prev, jnp.max(sr, axis=0, keepdims=True))
                alpha[r] = jnp.exp(m_prev - m_new)
                p[r] = jnp.exp(sr - m_new).astype(jnp.bfloat16)
                m_ref[r] = m_new
            pv = {r: jnp.dot(vt_g[r % NSA_KV_GROUPS], p[r], preferred_element_type=jnp.float32) for r in batch}
            for r in batch:
                acc_ref[r] = alpha[r] * acc_ref[r] + pv[r]
        return carry

    lax.fori_loop(first, last + 1, key_block, 0)
    out = []
    for r in tiles:
        acc = acc_ref[r]
        sum_row = (1 - r % NSA_KV_GROUPS) * NSA_HEAD_DIM
        out.append((acc / jnp.maximum(acc[sum_row:sum_row + 1, :], 1e-30)).T)
    _merge_groups(o_ref, out, gate_ref, 2 if windowed else 1)


def nsa_prompt_attention(q, q_rot, kc, vc, kv_att, gates):
    n, T, D = q.shape
    tq = Q_BLOCK
    qspec = pl.BlockSpec((1, tq, D), lambda b, i: (b, i, 0))
    whole = lambda a: pl.BlockSpec((1,) + a.shape[1:], lambda b, i: (b, 0, 0))
    lanes = lambda j: pl.BlockSpec((1, T, SLAB), lambda b, i: (b, 0, j))
    n_sel = NSA_KV_GROUPS * SEL_LANES
    sel_spec = pl.BlockSpec((1, tq, n_sel), lambda b, i: (b, i, 0))
    gate_spec = pl.BlockSpec((1, tq, gates.shape[2]), lambda b, i: (b, i, 0))
    params = pltpu.CompilerParams(dimension_semantics=("parallel", "arbitrary"),
                                  vmem_limit_bytes=48 * 1024 * 1024)
    qs_scratch = pltpu.VMEM((NSA_TILES, tq, SLAB), jnp.bfloat16)
    o_cmp, sel = pl.pallas_call(
        _nsa_cmp_kernel,
        grid=(n, T // tq),
        in_specs=[qspec, whole(kc), whole(vc), gate_spec],
        out_specs=[qspec, sel_spec],
        out_shape=[jax.ShapeDtypeStruct((n, T, D), jnp.float32),
                   jax.ShapeDtypeStruct((n, T, n_sel), jnp.bfloat16)],
        scratch_shapes=[qs_scratch],
        compiler_params=params,
        name="nsa_cmp_select",
    )(q, kc, vc, gates)
    stat = pltpu.VMEM((NSA_TILES, 1, tq), jnp.float32)
    acc = pltpu.VMEM((NSA_TILES, SLAB, tq), jnp.float32)
    v_t = jnp.swapaxes(jnp.concatenate([kv_att[:, :, SLAB:2 * SLAB], kv_att[:, :, 3 * SLAB:]], axis=-1), 1, 2)
    rows = lambda j: pl.BlockSpec((1, SLAB, T), lambda b, i: (b, j, 0))

    def dense(windowed, branch, name):
        kblock = min(WINDOW + tq, T) if windowed else min(SELECTED_KEY_BLOCK, T)
        return pl.pallas_call(
            functools.partial(_nsa_dense_kernel, windowed=windowed, kblock=kblock),
            grid=(n, T // tq),
            in_specs=[qspec, lanes(2 * branch), rows(branch), sel_spec, gate_spec],
            out_specs=qspec,
            out_shape=jax.ShapeDtypeStruct((n, T, D), jnp.float32),
            scratch_shapes=[qs_scratch, stat, acc],
            compiler_params=params,
            name=name,
        )(q_rot, kv_att, v_t, sel, gates)

    return o_cmp, dense(False, 0, "nsa_selected"), dense(True, 1, "nsa_window")


def _rope_tables(pos):
    half = ROT_DIM // 2
    inv = ROPE_THETA ** (-2.0 * jnp.arange(half, dtype=jnp.float32) / ROT_DIM)
    ang = pos.astype(jnp.float32)[:, None] * inv[None, :]
    rest = NSA_HEAD_DIM - ROT_DIM
    cos = jnp.concatenate([jnp.cos(ang), jnp.cos(ang), jnp.ones((pos.shape[0], rest), jnp.float32)], axis=1)
    sin = jnp.concatenate([-jnp.sin(ang), jnp.sin(ang), jnp.zeros((pos.shape[0], rest), jnp.float32)], axis=1)
    return jnp.tile(cos, (1, NSA_KV_GROUPS)), jnp.tile(sin, (1, NSA_KV_GROUPS))


def _nsa_pre_kernel(x_ref, gkv_ref, gmix_ref, cos_ref, sin_ref, wkv_ref, wq_ref, wg_ref,
                    cmp_ref, slc_ref, win_ref, kvb_ref, q_ref, qr_ref, gate_ref):
    xh = _rms(x_ref[0])
    cos, sin = cos_ref[...], sin_ref[...]
    low = lax.broadcasted_iota(jnp.int32, cos.shape, 1) % NSA_HEAD_DIM < ROT_DIM // 2

    def rope(t):
        swapped = jnp.where(low, pltpu.roll(t, SLAB - ROT_DIM // 2, 1), pltpu.roll(t, ROT_DIM // 2, 1))
        return t * cos + swapped * sin

    kv = _bdot(xh * gkv_ref[...], wkv_ref[...])
    part = lambda j: kv[:, j * SLAB:(j + 1) * SLAB]
    k_slc, k_win = rope(part(2)), rope(part(4))
    cmp_ref[0] = kv[:, :2 * SLAB]
    slc_ref[0] = jnp.concatenate([k_slc, part(3)], axis=-1)
    win_ref[0] = jnp.concatenate([k_win, part(5)], axis=-1)
    kvb_ref[0] = jnp.concatenate([k_slc, part(3), k_win, part(5)], axis=-1).astype(jnp.bfloat16)
    h = xh * gmix_ref[...]
    q = _bdot(h, wq_ref[...])
    q_ref[0] = q.astype(jnp.bfloat16)
    qr_ref[0] = jnp.concatenate([rope(q[:, i * SLAB:(i + 1) * SLAB]) for i in range(NSA_SLABS)],
                                axis=-1).astype(jnp.bfloat16)
    gate_ref[0] = jax.nn.sigmoid(_bdot(h, wg_ref[...]))


def nsa_pre(x, pos, norm_kv, norm_mix, w_kv, w_in):
    n, T, D = x.shape
    tm = min(ROW_TILE, T)
    nq = NSA_HEADS * NSA_HEAD_DIM
    cos, sin = _rope_tables(pos)
    w_q = (_to_slabs(w_in[:, :nq]) * NSA_HEAD_DIM ** -0.5).astype(jnp.bfloat16)
    w_g = jnp.pad(w_in[:, nq:], ((0, 0), (0, SLAB - (w_in.shape[1] - nq)))).astype(jnp.bfloat16)
    seq = lambda width: pl.BlockSpec((1, tm, width), lambda b, t: (b, t, 0))
    full = lambda a: pl.BlockSpec(a.shape, lambda b, t: (0,) * a.ndim)
    table = pl.BlockSpec((tm, SLAB), lambda b, t: (t, 0))
    consts = [norm_kv.reshape(1, D), norm_mix.reshape(1, D)]
    weights = [w_kv.astype(jnp.bfloat16), w_q, w_g]
    widths = [2 * SLAB, 2 * SLAB, 2 * SLAB, 4 * SLAB, nq, nq, SLAB]
    dtypes = [jnp.float32] * 3 + [jnp.bfloat16] * 3 + [jnp.float32]
    return pl.pallas_call(
        _nsa_pre_kernel,
        grid=(n, T // tm),
        in_specs=[seq(D)] + [full(a) for a in consts] + [table, table] + [full(a) for a in weights],
        out_specs=[seq(w) for w in widths],
        out_shape=[jax.ShapeDtypeStruct((n, T, w), dt) for w, dt in zip(widths, dtypes)],
        compiler_params=pltpu.CompilerParams(dimension_semantics=("parallel", "parallel"),
                                             vmem_limit_bytes=48 * 1024 * 1024),
        name="nsa_pre",
    )(x, *consts, cos, sin, *weights)


def _to_slabs(x):
    lead = x.shape[:-1]
    x = x.reshape(lead + (NSA_KV_GROUPS, NSA_GROUP_HEADS, NSA_HEAD_DIM))
    return jnp.swapaxes(x, -3, -2).reshape(lead + (NSA_HEADS * NSA_HEAD_DIM,))


def nsa_layer_prompt(x, norm_kv, norm_mix, w_kv, cmp_pos, cmp_w1, cmp_b1, cmp_w2, w_in, w_o, norm_next, w_router):
    n, T, D = x.shape
    cmp_kv, slc_kv, win_kv, kv_att, q, q_rot, gates = nsa_pre(x, jnp.arange(T), norm_kv, norm_mix, w_kv, w_in)
    hidden = rows_chunk_hidden(cmp_kv, _chunk_weights(cmp_w1))
    kc, vc = compress_from_hidden(hidden, cmp_pos, cmp_w1, cmp_b1, cmp_w2)
    lanes = lambda a: a.reshape(n, a.shape[1], SLAB).astype(jnp.bfloat16)
    branches = nsa_prompt_attention(q, q_rot, lanes(kc), lanes(vc), kv_att, gates)
    flat = lambda a: a.reshape(n * T, a.shape[-1])
    x_out, h_out, logits = out_proj([flat(o) for o in branches], None, flat(x), _to_slabs(w_o.T).T, norm_next,
                                    w_router)
    return x_out, h_out, logits, cmp_kv, slc_kv, win_kv


def sample_kv_context(cmp_new, slc_new, win_new, cache_cmp_kv, cache_slc_kv, cache_win_kv, page_table,
                      cmp_pos, cmp_w1, cmp_b1, cmp_w2):
    n, S = cmp_new.shape[:2]
    n_new_blk = -(-S // SLC_BLOCK)
    pad = ((0, 0), (0, n_new_blk * SLC_BLOCK - S), (0, 0), (0, 0), (0, 0))
    w_chunk = _chunk_weights(cmp_w1)
    hidden = jnp.concatenate([paged_chunk_hidden(cache_cmp_kv, page_table, _token_weights(cmp_w1)),
                              rows_chunk_hidden(jnp.pad(cmp_new.astype(cache_cmp_kv.dtype), pad), w_chunk)], axis=1)
    kc, vc = compress_from_hidden(hidden, cmp_pos, cmp_w1, cmp_b1, cmp_w2)
    win_all = jnp.concatenate([cache_win_kv, win_new.astype(cache_win_kv.dtype)], axis=1)
    return kc, vc, win_all


def sample_nsa(h, kc, vc, slc_new, win_new, cache_slc_kv, cache_win_kv, page_table, w_in, w_o):
    n, S, _ = h.shape
    assert S == 1
    q, gate = nsa_query(h, w_in, precise=True)
    t_pos = PAST_LEN + jnp.arange(S)
    with jax.default_matmul_precision("highest"):
        o_cmp, idx = compressed_branch_and_selection(q, t_pos, kc, vc)
    q_rot = (rope_partial(q, t_pos) * NSA_HEAD_DIM ** -0.5).reshape(n, NSA_HEADS, NSA_HEAD_DIM)
    o_slc, o_win = decode_attention(q_rot, idx[:, :, 0], page_table, cache_slc_kv, cache_win_kv, slc_new, win_new)
    g = jax.nn.sigmoid(gate.astype(jnp.float32)).astype(q.dtype)
    o = g[..., 0:1] * o_cmp + g[..., 1:2] * o_slc.reshape(q.shape) + g[..., 2:3] * o_win.reshape(q.shape)
    return dense(o.reshape(n, S, NSA_HEADS * NSA_HEAD_DIM), w_o, precise=True)


def kernel(x_prompt, x_sample, state_wkv, state_shift, cache_cmp_kv, cache_slc_kv, cache_win_kv, page_table, norm_mix, norm_ffn, norm_kv, norm_final, rw_mu, rw_w_rkv, rw_w0, rw_w1, rw_w2, rw_a0, rw_a1, rw_a2, rw_g1, rw_g2, rw_k_k, rw_k_a, rw_r_k, rw_gn_w, rw_gn_b, rw_w_o, nsa_w_kv, nsa_cmp_pos, nsa_cmp_w1, nsa_cmp_b1, nsa_cmp_w2, nsa_w_in, nsa_w_o, ffn_w_gu, ffn_w_down, moe_router, moe_w_gu, moe_w_down):
    cmp_params = (nsa_cmp_pos, nsa_cmp_w1, nsa_cmp_b1, nsa_cmp_w2)
    assert DEPTH == 2 and N_A_LAYERS == 1
    D = D_MODEL
    moe_gu, moe_down = moe_w_gu[0].astype(jnp.bfloat16), moe_w_down[0].astype(jnp.bfloat16)
    x_prompt, moe_gu, moe_down = lax.optimization_barrier((x_prompt, moe_gu, moe_down))

    n_p, T = x_prompt.shape[:2]
    x1, h1, wkv_fin, h_last = rwkv7_layer(
        x_prompt, jnp.zeros((n_p, D), x_prompt.dtype), jnp.zeros((n_p, RWKV_HEADS, RWKV_HEAD, RWKV_HEAD), jnp.float32),
        norm_mix[0], rw_mu[0], rw_w_rkv[0], rw_w0[0], rw_w1[0], rw_w2[0], rw_a0[0], rw_a1[0], rw_a2[0],
        rw_g1[0], rw_g2[0], rw_k_k[0], rw_k_a[0], rw_r_k[0], rw_gn_w[0], rw_gn_b[0], rw_w_o[0], norm_ffn[0])
    wkv_p, shift_p = wkv_fin[None], h_last[None]
    x2 = swiglu_residual(x1, h1, ffn_w_gu[0], ffn_w_down[0])
    x_p, h_p, logits_p, cmp_rows, slc_rows, win_rows = nsa_layer_prompt(
        x2.reshape(n_p, T, D), norm_kv, norm_mix[1], nsa_w_kv, *cmp_params, nsa_w_in[0], nsa_w_o[0], norm_ffn[1],
        moe_router[0])
    kv_shape = (n_p, T, 2, NSA_KV_GROUPS, NSA_HEAD_DIM)
    cmp_kv_p, slc_kv_p = cmp_rows.reshape(kv_shape), slc_rows.reshape(kv_shape)
    win_kv_p = win_rows.reshape(kv_shape)[:, -min(WINDOW, T):]

    pos_s = PAST_LEN + jnp.arange(x_sample.shape[1], dtype=jnp.int32)
    h = rmsnorm(x_sample, norm_mix[0])
    y, s_fin, h_last_s = rwkv7_time_mix(
        h, state_shift[0], state_wkv[0], rw_mu[0], rw_w_rkv[0], rw_w0[0], rw_w1[0], rw_w2[0],
        rw_a0[0], rw_a1[0], rw_a2[0], rw_g1[0], rw_g2[0], rw_k_k[0], rw_k_a[0],
        rw_r_k[0], rw_gn_w[0], rw_gn_b[0], rw_w_o[0], precise=True)
    wkv_s, shift_s = s_fin[None], h_last_s[None]
    x_s = x_sample + y
    x_s = x_s + swiglu(rmsnorm(x_s, norm_ffn[0]), ffn_w_gu[0], ffn_w_down[0], precise=True)
    cmp_kv_s, slc_kv_s, win_new = shared_kv_rows(x_s, norm_kv, nsa_w_kv, pos_s, precise=True)
    kc_s, vc_s, win_all = sample_kv_context(cmp_kv_s, slc_kv_s, win_new, cache_cmp_kv, cache_slc_kv, cache_win_kv,
                                            page_table, *cmp_params)
    x_s = x_s + sample_nsa(rmsnorm(x_s, norm_mix[1]), kc_s, vc_s, slc_kv_s, win_new, cache_slc_kv, cache_win_kv,
                           page_table, nsa_w_in[0], nsa_w_o[0])
    win_kv_s = win_all[:, -cache_win_kv.shape[1]:]
    x_s = x_s.reshape(-1, D)
    h_s = rmsnorm(x_s, norm_ffn[1])
    logits_s = matmul_precise(h_s, moe_router[0])

    n_tok_p = n_p * T
    logits = jnp.concatenate([logits_p[:, :N_EXPERTS], logits_s], axis=0)
    y0, y1, gate = moe_swiglu(jnp.concatenate([h_p, h_s.astype(jnp.bfloat16)], axis=0), logits,
                              moe_gu, moe_down)
    gate = jnp.pad(gate, ((0, 0), (0, 128 - TOP_K)))
    y_prompt = combine_norm(x_p, y0[:n_tok_p], y1[:n_tok_p], gate[:n_tok_p], norm_final).reshape(x_prompt.shape)
    y_sample = combine_norm(x_s, y0[n_tok_p:], y1[n_tok_p:], gate[n_tok_p:], norm_final).reshape(x_sample.shape)


    return (y_prompt, y_sample, wkv_p, shift_p, cmp_kv_p, slc_kv_p, win_kv_p,
            wkv_s, shift_s, cmp_kv_s, slc_kv_s, win_kv_s)
```

```python
import functools
import math

import jax
import jax.numpy as jnp
from jax import lax
from jax.experimental import pallas as pl
from jax.experimental.pallas import tpu as pltpu

D_MODEL = 1024
DEPTH = 2
PAST_LEN = 16384
PAGE_SIZE = 128
N_A_LAYERS = DEPTH // 2
RWKV_HEAD = 64
RWKV_HEADS = D_MODEL // RWKV_HEAD
DECAY_SCALE = math.exp(-0.5)
GN_EPS = RWKV_HEAD * 1e-5
NSA_HEADS = 16
NSA_HEAD_DIM = 64
NSA_KV_GROUPS = 2
NSA_GROUP_HEADS = NSA_HEADS // NSA_KV_GROUPS
N_BRANCH = 3
CMP_BLOCK = 32
CMP_STRIDE = 16
CMP_HIDDEN = 128
SLC_BLOCK = 64
SLC_TOPN = 16
WINDOW = 512
Q_BLOCK = 128
FORCED_SCORE = 1e4
ROPE_THETA = 500000.0
ROT_DIM = NSA_HEAD_DIM // 4
N_EXPERTS = 8
TOP_K = 2
NORM_EPS = 1e-6


def _rmsnorm_kernel(x_ref, g_ref, o_ref):
    x = x_ref[...]
    y = x * lax.rsqrt(jnp.mean(x * x, axis=-1, keepdims=True) + NORM_EPS)
    o_ref[...] = y * g_ref[...]


def rmsnorm(x, g):
    shp = x.shape
    x2 = x.reshape(-1, shp[-1])
    rows = x2.shape[0]
    tm = min(rows, 512)
    out = pl.pallas_call(
        _rmsnorm_kernel,
        grid=(rows // tm,),
        in_specs=[pl.BlockSpec((tm, shp[-1]), lambda i: (i, 0)),
                  pl.BlockSpec((1, shp[-1]), lambda i: (0, 0))],
        out_specs=pl.BlockSpec((tm, shp[-1]), lambda i: (i, 0)),
        out_shape=jax.ShapeDtypeStruct(x2.shape, x.dtype),
        name="rmsnorm",
    )(x2, g.reshape(1, -1))
    return out.reshape(shp)


def rope_partial(x, pos):
    half = ROT_DIM // 2
    inv = ROPE_THETA ** (-2.0 * jnp.arange(half, dtype=jnp.float32) / ROT_DIM)
    ang = pos.astype(jnp.float32)[:, None] * inv[None, :]
    shape = (1, pos.shape[0]) + (1,) * (x.ndim - 3) + (half,)
    cos = jnp.cos(ang).reshape(shape)
    sin = jnp.sin(ang).reshape(shape)
    xf = x.astype(jnp.float32)
    x1, x2 = xf[..., :half], xf[..., half:ROT_DIM]
    out = jnp.concatenate([x1 * cos - x2 * sin, x2 * cos + x1 * sin, xf[..., ROT_DIM:]], axis=-1)
    return out.astype(x.dtype)


def masked_softmax(s, mask):
    s = jnp.where(mask, s.astype(jnp.float32), -jnp.inf)
    m = jnp.max(s, axis=-1, keepdims=True)
    m = jnp.where(jnp.isfinite(m), m, 0.0)
    e = jnp.where(mask, jnp.exp(s - m), 0.0)
    return e / jnp.maximum(jnp.sum(e, axis=-1, keepdims=True), 1e-30)


FF_CHUNK = 1408
SWIGLU_ROWS = 512
MOE_PARTS = 4


def _swiglu_kernel(blk_e_ref, n_used_ref, x_ref, wg_ref, wu_ref, wd_ref, *rest, has_res):
    o_ref = rest[-1]
    i = pl.program_id(0)
    f = pl.program_id(1)

    @pl.when(i < n_used_ref[0])
    def _():
        x = x_ref[...].astype(jnp.bfloat16)
        g = jnp.dot(x, wg_ref[0], preferred_element_type=jnp.float32)
        u = jnp.dot(x, wu_ref[0], preferred_element_type=jnp.float32)
        act = (g * jax.nn.sigmoid(g) * u).astype(jnp.bfloat16)
        y = jnp.dot(act, wd_ref[0], preferred_element_type=jnp.float32)

        @pl.when(f == 0)
        def _():
            o_ref[...] = y + rest[0][...] if has_res else y

        @pl.when(f > 0)
        def _():
            o_ref[...] += y

    @pl.when(i >= n_used_ref[0])
    def _():
        o_ref[...] = jnp.zeros(o_ref.shape, o_ref.dtype)


def grouped_swiglu(xb, blk_e, n_used, w_gu, w_down, res=None, into=None):
    rows, d = xb.shape
    b = min(SWIGLU_ROWS, rows)
    n_blk = rows // b
    ff = w_down.shape[1]
    tf = FF_CHUNK
    n_f = ff // tf
    chunk = lambda i, f, be, nu: jnp.where(i < nu[0], f, n_f - 1)
    in_specs = [
        pl.BlockSpec((b, d), lambda i, f, be, nu: (i, 0)),
        pl.BlockSpec((1, d, tf), lambda i, f, be, nu: (be[i], 0, chunk(i, f, be, nu))),
        pl.BlockSpec((1, d, tf), lambda i, f, be, nu: (be[i], 0, n_f + chunk(i, f, be, nu))),
        pl.BlockSpec((1, tf, d), lambda i, f, be, nu: (be[i], chunk(i, f, be, nu), 0)),
    ]
    operands = [xb, w_gu, w_gu, w_down]
    if res is not None:
        in_specs.append(pl.BlockSpec((b, d), lambda i, f, be, nu: (i, 0)))
        operands.append(res)
    out_rows, first_blk, aliases = rows, 0, {}
    if into is not None:
        prev, first_blk, out_rows = into
        if prev is not None:
            in_specs.append(pl.BlockSpec(memory_space=pl.ANY))
            aliases = {2 + len(operands): 0}
            operands.append(prev)
    return pl.pallas_call(
        functools.partial(_swiglu_kernel, has_res=res is not None),
        grid_spec=pltpu.PrefetchScalarGridSpec(
            num_scalar_prefetch=2,
            grid=(n_blk, n_f),
            in_specs=in_specs,
            out_specs=pl.BlockSpec((b, d), lambda i, f, be, nu: (i + first_blk, 0)),
        ),
        out_shape=jax.ShapeDtypeStruct((out_rows, d), jnp.float32),
        input_output_aliases=aliases,
        compiler_params=pltpu.CompilerParams(dimension_semantics=("arbitrary", "arbitrary"),
                                             vmem_limit_bytes=56 * 1024 * 1024),
        name="grouped_swiglu",
    )(blk_e.astype(jnp.int32), jnp.reshape(n_used, (1,)).astype(jnp.int32), *operands)


def swiglu(h, w_gu, w_down, precise=False):
    g, u = jnp.split(dense(h, w_gu, precise), 2, axis=-1)
    return dense(jax.nn.silu(g) * u, w_down, precise)


def swiglu_residual(x, h, w_gu, w_down):
    n_blk = h.shape[0] // min(SWIGLU_ROWS, h.shape[0])
    return grouped_swiglu(h, jnp.zeros((n_blk,), jnp.int32), jnp.int32(n_blk),
                          w_gu.astype(jnp.bfloat16)[None], w_down.astype(jnp.bfloat16)[None], res=x)


def _combine_norm_kernel(x_ref, y0_ref, y1_ref, g_ref, gain_ref, o_ref):
    g = g_ref[...]
    x = x_ref[...] + (y0_ref[...] * g[:, 0:1] + y1_ref[...] * g[:, 1:2])
    o_ref[...] = _rms(x) * gain_ref[...]


def combine_norm(x, y0, y1, gate, gain):
    m, d = x.shape
    tm = min(2 * ROW_TILE, m)
    rows = lambda width: pl.BlockSpec((tm, width), lambda i: (i, 0))
    return pl.pallas_call(
        _combine_norm_kernel,
        grid=(m // tm,),
        in_specs=[rows(d), rows(d), rows(d), rows(gate.shape[1]), pl.BlockSpec((1, d), lambda i: (0, 0))],
        out_specs=rows(d),
        out_shape=jax.ShapeDtypeStruct((m, d), jnp.float32),
        compiler_params=pltpu.CompilerParams(dimension_semantics=("parallel",)),
        name="combine_norm",
    )(x, y0, y1, gate, gain.reshape(1, d))


def moe_swiglu(xt, logits, w_gu_e, w_down_e):
    n_tok, d = xt.shape
    b = SWIGLU_ROWS
    experts = jnp.arange(N_EXPERTS)[None, :]
    e0 = jnp.argmax(logits, axis=-1)
    v0 = jnp.max(logits, axis=-1)
    rest = jnp.where(experts == e0[:, None], -jnp.inf, logits)
    e1 = jnp.argmax(rest, axis=-1)
    v1 = jnp.max(rest, axis=-1)
    top_idx = jnp.stack([e0, e1], axis=-1).astype(jnp.int32)
    gate = jax.nn.softmax(jnp.stack([v0, v1], axis=-1), axis=-1)
    nk = n_tok * TOP_K
    flat_e = top_idx.reshape(nk)
    onehot = (flat_e[:, None] == jnp.arange(N_EXPERTS)[None, :]).astype(jnp.int32)
    before = jnp.cumsum(onehot, axis=0) - onehot
    counts = jnp.sum(onehot, axis=0)
    padded = (counts + b - 1) // b * b
    ends_pad = jnp.cumsum(padded)
    starts_pad = ends_pad - padded
    dest = jnp.sum(onehot * (starts_pad[None, :] + before), axis=1)
    part_rows = MOE_PARTS * b
    n_rows = -(-((nk + b - 1) // b * b + N_EXPERTS * b) // part_rows) * part_rows
    n_blk = n_rows // b
    flat_tok = jnp.repeat(jnp.arange(n_tok, dtype=jnp.int32), TOP_K)
    row_tok = jnp.full((n_rows,), n_tok, jnp.int32).at[dest].set(flat_tok)
    blk_e = jnp.minimum(jnp.searchsorted(ends_pad, jnp.arange(n_blk) * b, side='right'), N_EXPERTS - 1)
    x_pad = jnp.concatenate([xt.astype(jnp.float32), jnp.zeros((1, d), jnp.float32)], axis=0)
    w_gu_b, w_down_b = w_gu_e.astype(jnp.bfloat16), w_down_e.astype(jnp.bfloat16)
    n_used = ends_pad[-1] // b
    per_part = n_blk // MOE_PARTS
    yb = None
    for i in range(MOE_PARTS):
        rows = x_pad[row_tok[i * per_part * b:(i + 1) * per_part * b]]
        yb = grouped_swiglu(rows, blk_e[i * per_part:(i + 1) * per_part],
                            jnp.clip(n_used - i * per_part, 0, per_part), w_gu_b, w_down_b,
                            into=(yb, i * per_part, n_rows))
    dest = dest.reshape(n_tok, TOP_K)
    return yb[dest[:, 0]], yb[dest[:, 1]], gate


WKV_CHUNK = 64
WKV_HEADS_PER_STEP = 16
WKV_SEQS_PER_STEP = 2

_NN = (((1,), (0,)), ((), ()))
_NT = (((1,), (1,)), ((), ()))
_TN = (((0,), (0,)), ((), ()))


def _mm(x, y, dims):
    return lax.dot_general(x.astype(jnp.bfloat16), y.astype(jnp.bfloat16), dims,
                           preferred_element_type=jnp.float32)


def _wkv7_chunk_kernel(r_ref, lw_ref, k_ref, v_ref, kk_ref, a_ref, rk_ref, gnw_ref, gnb_ref, s0_ref,
                       y_ref, sout_ref, state_ref):
    c = pl.program_id(2)
    n_seq, L = r_ref.shape[:2]
    n_pairs = state_ref.shape[0]
    per_seq = n_pairs // n_seq
    N = RWKV_HEAD
    W = 2 * N
    lanes = lambda ref: jnp.concatenate([ref[i] for i in range(n_seq)], axis=-1)
    tiled = lambda ref: jnp.concatenate([ref[...]] * n_seq, axis=-1)

    def block_diag(top, bottom):
        z = jnp.zeros((N, N), jnp.float32)
        return jnp.concatenate([jnp.concatenate([top, z], axis=1), jnp.concatenate([z, bottom], axis=1)], axis=0)

    @pl.when(c == 0)
    def _():
        for p in range(n_pairs):
            b, q = p // per_seq, p % per_seq
            state_ref[p] = block_diag(s0_ref[b, 2 * q], s0_ref[b, 2 * q + 1])

    row2 = lax.broadcasted_iota(jnp.int32, (2 * L, W), 0)
    lane2 = lax.broadcasted_iota(jnp.int32, (2 * L, W), 1)
    own_lanes = (row2 // L) == (lane2 // N)
    rr = lax.broadcasted_iota(jnp.int32, (2 * L, 2 * L), 0)
    cc = lax.broadcasted_iota(jnp.int32, (2 * L, 2 * L), 1)
    same = (rr // L) == (cc // L)
    strict = same & (rr % L > cc % L)
    incl = same & (rr % L >= cc % L)
    wr = lax.broadcasted_iota(jnp.int32, (W, W), 0)
    wc = lax.broadcasted_iota(jnp.int32, (W, W), 1)
    eye_w = wr == wc
    ones_bd = ((wr // N) == (wc // N)).astype(jnp.bfloat16)
    tl = lax.broadcasted_iota(jnp.int32, (L, L), 0) >= lax.broadcasted_iota(jnp.int32, (L, L), 1)
    tri = tl.astype(jnp.bfloat16)

    def head_sum(x):
        hi, lo = _split_bf16(x)
        return (jnp.dot(hi, ones_bd, preferred_element_type=jnp.float32)
                + jnp.dot(lo, ones_bd, preferred_element_type=jnp.float32))

    def stack(x):
        return jnp.where(own_lanes, jnp.concatenate([x, x], axis=0), 0.0)

    unstack = lambda x: x[:L] + x[L:]

    lw = lanes(lw_ref)
    lw_hi, lw_lo = _split_bf16(lw)
    cum = (lax.dot_general(tri, lw_hi, _NN, preferred_element_type=jnp.float32)
           + lax.dot_general(tri, lw_lo, _NN, preferred_element_type=jnp.float32))
    cum_last = cum[L - 1:L, :]
    e_neg_all = jnp.exp(-cum)
    e_tail_all = jnp.exp(cum_last - cum)
    e_prev_all = jnp.exp(cum - lw)
    e_cum_all = jnp.exp(cum)
    wl_all = jnp.exp(cum_last)

    pairs = range(n_pairs)
    slab = lambda t, p: t[:, p * W:(p + 1) * W]
    r_all, k_all, v_all, kk_all, a_all = (lanes(ref) for ref in (r_ref, k_ref, v_ref, kk_ref, a_ref))
    r_in = [slab(r_all, p) for p in pairs]
    k_in = [slab(k_all, p) for p in pairs]
    v_in = [slab(v_all, p) for p in pairs]
    ssq = [head_sum(slab(kk_all, p) * slab(kk_all, p)) for p in pairs]
    at_st, bt_st, bh_st, rt_st, kt_st, kh_st, v_st = [], [], [], [], [], [], []
    for p in pairs:
        kk = slab(kk_all, p) / jnp.maximum(jnp.sqrt(ssq[p]), 1e-12)
        b = kk * slab(a_all, p)
        at_st.append(stack(-kk * slab(e_prev_all, p)))
        bt_st.append(stack(b * slab(e_neg_all, p)))
        bh_st.append(stack(b * slab(e_tail_all, p)))
        rt_st.append(stack(r_in[p] * slab(e_cum_all, p)))
        kt_st.append(stack(k_in[p] * slab(e_neg_all, p)))
        kh_st.append(stack(k_in[p] * slab(e_tail_all, p)))
        v_st.append(stack(v_in[p]))
    a_ab = [jnp.where(strict, _mm(at_st[p], bt_st[p], _NT), 0.0) for p in pairs]
    a_ak = [jnp.where(strict, _mm(at_st[p], kt_st[p], _NT), 0.0) for p in pairs]
    r_b = [jnp.where(incl, _mm(rt_st[p], bt_st[p], _NT), 0.0) for p in pairs]
    r_k = [jnp.where(incl, _mm(rt_st[p], kt_st[p], _NT), 0.0) for p in pairs]
    av = [_mm(a_ak[p], v_st[p], _NN) for p in pairs]
    ht = [_mm(v_st[p], kh_st[p], _TN) for p in pairs]
    yp = [_mm(r_k[p], v_st[p], _NN) for p in pairs]
    pw = a_ab
    inv_a = a_ab
    n = 1
    while 2 * n < L:
        pw = [_mm(pw[p], pw[p], _NN) for p in pairs]
        inv_a = [inv_a[p] + pw[p] + _mm(inv_a[p], pw[p], _NN) for p in pairs]
        n *= 2
    ap = [at_st[p] + _mm(inv_a[p], at_st[p], _NN) for p in pairs]
    vp = [av[p] + _mm(inv_a[p], av[p], _NN) for p in pairs]
    g = [jnp.where(eye_w, slab(wl_all, p), 0.0) + _mm(bh_st[p], ap[p], _TN) for p in pairs]
    ht = [ht[p] + _mm(vp[p], bh_st[p], _TN) for p in pairs]
    rp = [unstack(rt_st[p] + _mm(r_b[p], ap[p], _NN)) for p in pairs]
    yp = [unstack(yp[p] + _mm(r_b[p], vp[p], _NN)) for p in pairs]
    s_prev = [state_ref[p] for p in pairs]
    y = [_mm(rp[p], s_prev[p], _NT) + yp[p] for p in pairs]
    for p in pairs:
        state_ref[p] = _mm(s_prev[p], g[p], _NT) + ht[p]
    mean = [head_sum(y[p]) * (1.0 / N) for p in pairs]
    cen = [y[p] - mean[p] for p in pairs]
    var = [head_sum(cen[p] * cen[p]) * (1.0 / N) for p in pairs]
    rk, gnw, gnb = tiled(rk_ref), tiled(gnw_ref), tiled(gnb_ref)
    bonus = [head_sum(r_in[p] * k_in[p] * slab(rk, p)) * v_in[p] for p in pairs]
    out = [cen[p] * lax.rsqrt(var[p] + GN_EPS) * slab(gnw, p) + slab(gnb, p) + bonus[p] for p in pairs]
    for b in range(n_seq):
        y_ref[b] = jnp.concatenate(out[b * per_seq:(b + 1) * per_seq], axis=-1)

    @pl.when(c == pl.num_programs(2) - 1)
    def _():
        for p in pairs:
            b, q = p // per_seq, p % per_seq
            s = state_ref[p]
            sout_ref[b, 2 * q] = s[:N, :N]
            sout_ref[b, 2 * q + 1] = s[N:, N:]


def wkv7_chunked(r, lw, k, v, kk, a_gate, r_k, gn_w, gn_b, s0):
    n, T, D = r.shape
    L = WKV_CHUNK
    hb = WKV_HEADS_PER_STEP
    w = hb * RWKV_HEAD
    nb = WKV_SEQS_PER_STEP if n % WKV_SEQS_PER_STEP == 0 else 1
    seq = pl.BlockSpec((nb, L, w), lambda b, h, c: (b, c, h))
    vec = pl.BlockSpec((1, w), lambda b, h, c: (0, h))
    st = pl.BlockSpec((nb, hb, RWKV_HEAD, RWKV_HEAD), lambda b, h, c: (b, h, 0, 0))
    row = lambda t: t.reshape(1, D).astype(jnp.float32)
    return pl.pallas_call(
        _wkv7_chunk_kernel,
        grid=(n // nb, D // w, T // L),
        in_specs=[seq] * 6 + [vec] * 3 + [st],
        out_specs=[seq, st],
        out_shape=[jax.ShapeDtypeStruct((n, T, D), jnp.float32),
                   jax.ShapeDtypeStruct(s0.shape, jnp.float32)],
        scratch_shapes=[pltpu.VMEM((nb * hb // 2, 2 * RWKV_HEAD, 2 * RWKV_HEAD), jnp.float32)],
        compiler_params=pltpu.CompilerParams(dimension_semantics=("parallel", "parallel", "arbitrary")),
        name="wkv7_chunked",
    )(r, lw, k, v, kk, a_gate, row(r_k), row(gn_w), row(gn_b), s0)


def wkv7_scan(r, lw, k, v, a_vec, b_vec, s0):
    def step(S, inp):
        r_t, lw_t, k_t, v_t, a_t, b_t = inp
        sa = jnp.sum(S * a_t[:, :, None, :], axis=-1)
        S = S * jnp.exp(lw_t)[:, :, None, :] + sa[..., None] * b_t[:, :, None, :] + v_t[..., None] * k_t[:, :, None, :]
        return S, jnp.sum(S * r_t[:, :, None, :], axis=-1)

    xs = tuple(jnp.moveaxis(t, 1, 0) for t in (r, lw, k, v, a_vec, b_vec))
    s_fin, ys = lax.scan(step, s0, xs)
    return jnp.moveaxis(ys, 0, 1), s_fin


ROW_TILE = 256


def _bdot(x, w):
    return jnp.dot(x.astype(jnp.bfloat16), w, preferred_element_type=jnp.float32)


def _rms(x):
    return x * lax.rsqrt(jnp.mean(x * x, axis=-1, keepdims=True) + NORM_EPS)


def _rwkv_pre_kernel(x_ref, shift_ref, gain_ref, mu_ref, vec_ref, wrkv_ref, w1_ref, a1_ref, g1_ref,
                     w2_ref, a2_ref, g2_ref,
                     r_ref, lw_ref, k_ref, v_ref, kk_ref, a_ref, g_ref, hlast_ref, prev_ref):
    @pl.when(pl.program_id(1) == 0)
    def _():
        prev_ref[...] = shift_ref[0]

    tm = x_ref.shape[1]
    h = _rms(x_ref[0]) * gain_ref[...]
    first_row = lax.broadcasted_iota(jnp.int32, h.shape, 0) == 0
    h_prev = jnp.where(first_row, prev_ref[...], pltpu.roll(h, 1, 0))
    prev_ref[...] = h[tm - 1:tm, :]
    hlast_ref[0] = h[tm - 1:tm, :]
    dx = h_prev - h
    mix = lambda i: (h + dx * mu_ref[i:i + 1, :]).astype(jnp.bfloat16)
    w0, a0, k_k, k_a = (vec_ref[i:i + 1, :] for i in range(4))
    r_ref[0] = _bdot(mix(0), wrkv_ref[0])
    k = _bdot(mix(2), wrkv_ref[1])
    v_ref[0] = _bdot(mix(3), wrkv_ref[2])
    lw_ref[0] = -DECAY_SCALE * jax.nn.sigmoid(w0 + _bdot(jnp.tanh(_bdot(mix(1), w1_ref[...])), w2_ref[...]))
    a = jax.nn.sigmoid(a0 + _bdot(_bdot(mix(4), a1_ref[...]), a2_ref[...]))
    g_ref[0] = _bdot(jax.nn.sigmoid(_bdot(mix(5), g1_ref[...])), g2_ref[...])
    a_ref[0] = a
    kk_ref[0] = k * k_k
    k_ref[0] = k * (1.0 + (a - 1.0) * k_a)


def _out_proj_kernel(*refs, n_terms, gated, routed):
    terms = refs[:n_terms]
    rest = refs[n_terms:]
    if gated:
        gate_ref, rest = rest[0], rest[1:]
    x_ref, w_ref, gain_ref = rest[:3]
    rest = rest[3:]
    if routed:
        wr_ref, rest = rest[0], rest[1:]
    xo_ref, ho_ref = rest[:2]
    y = terms[0][...]
    for t in terms[1:]:
        y = y + t[...]
    if gated:
        y = y * gate_ref[...]
    xo = x_ref[...] + _bdot(y, w_ref[...])
    xo_ref[...] = xo
    h = _rms(xo) * gain_ref[...]
    ho_ref[...] = h.astype(ho_ref.dtype)
    if routed:
        h_hi, h_lo = _split_bf16(h)
        w_hi, w_lo = _split_bf16(wr_ref[...])
        dot = functools.partial(jnp.dot, preferred_element_type=jnp.float32)
        rest[2][...] = dot(h_hi, w_hi) + (dot(h_hi, w_lo) + dot(h_lo, w_hi))


def out_proj(terms, gate, x, w, gain, w_router=None):
    m, d = x.shape
    kdim = w.shape[0]
    tm = min(ROW_TILE, m)
    rows = lambda width: pl.BlockSpec((tm, width), lambda i: (i, 0))
    full = lambda a: pl.BlockSpec(a.shape, lambda i: (0, 0))
    ins = list(terms) + ([gate] if gate is not None else [])
    consts = [w.astype(jnp.bfloat16), gain.reshape(1, d)]
    out_specs = [rows(d), rows(d)]
    out_shape = [jax.ShapeDtypeStruct((m, d), jnp.float32), jax.ShapeDtypeStruct((m, d), jnp.bfloat16)]
    if w_router is not None:
        consts.append(jnp.pad(w_router, ((0, 0), (0, 128 - w_router.shape[1]))))
        out_specs.append(rows(128))
        out_shape.append(jax.ShapeDtypeStruct((m, 128), jnp.float32))
    return pl.pallas_call(
        functools.partial(_out_proj_kernel, n_terms=len(terms), gated=gate is not None,
                          routed=w_router is not None),
        grid=(m // tm,),
        in_specs=[rows(kdim)] * len(ins) + [rows(d)] + [full(a) for a in consts],
        out_specs=out_specs,
        out_shape=out_shape,
        compiler_params=pltpu.CompilerParams(dimension_semantics=("parallel",),
                                             vmem_limit_bytes=48 * 1024 * 1024),
        name="out_proj",
    )(*ins, x, *consts)


def rwkv7_layer(x, shift0, s0, gain, mu, w_rkv, w0, w1, w2, a0, a1, a2, g1, g2, k_k, k_a, r_k, gn_w, gn_b, w_o,
                gain_next):
    n, T, D = x.shape
    tm = min(ROW_TILE, T)
    bf = lambda t: t.astype(jnp.bfloat16)
    seq = pl.BlockSpec((1, tm, D), lambda b, t: (b, t, 0))
    full = lambda a: pl.BlockSpec(a.shape, lambda b, t: (0,) * a.ndim)
    per_seq = pl.BlockSpec((1, 1, D), lambda b, t: (b, 0, 0))
    vecs = jnp.stack([w0, a0, k_k, k_a]).astype(jnp.float32)
    weights = [bf(w_rkv), bf(w1), bf(a1), bf(g1), bf(w2), bf(a2), bf(g2)]
    small = [gain.reshape(1, D), mu, vecs]
    outs = pl.pallas_call(
        _rwkv_pre_kernel,
        grid=(n, T // tm),
        in_specs=[seq, per_seq] + [full(a) for a in small + weights],
        out_specs=[seq] * 7 + [per_seq],
        out_shape=[jax.ShapeDtypeStruct((n, T, D), jnp.float32)] * 7
                  + [jax.ShapeDtypeStruct((n, 1, D), jnp.float32)],
        scratch_shapes=[pltpu.VMEM((1, D), jnp.float32)],
        compiler_params=pltpu.CompilerParams(dimension_semantics=("parallel", "arbitrary"),
                                             vmem_limit_bytes=56 * 1024 * 1024),
        name="rwkv_pre",
    )(x, shift0.reshape(n, 1, D), *small, *weights)
    r, lw, k, v, kk, a_gate, g, h_last = outs
    y, s_fin = wkv7_chunked(r, lw, k, v, kk, a_gate, r_k.reshape(-1), gn_w, gn_b, s0.astype(jnp.float32))
    flat = lambda t: t.reshape(n * T, D)
    x1, h1 = out_proj([flat(y)], flat(g), flat(x), w_o, gain_next)
    return x1, h1, s_fin, h_last.reshape(n, D)


def rwkv7_time_mix(h, h_prev, s0, mu, w_rkv, w0, w1, w2, a0, a1, a2, g1, g2, k_k, k_a, r_k, gn_w, gn_b, w_o,
                   precise=False):
    n, T, D = h.shape
    f32 = jnp.float32
    mm = functools.partial(dense, precise=precise)
    dx = jnp.concatenate([h_prev[:, None, :].astype(h.dtype), h[:, :-1]], axis=1) - h
    xr, xw, xk, xv, xa, xg = (h + dx * mu[i] for i in range(6))
    r = mm(xr, w_rkv[0])
    k = mm(xk, w_rkv[1])
    v = mm(xv, w_rkv[2])
    log_decay = -DECAY_SCALE * jax.nn.sigmoid((w0 + mm(jnp.tanh(mm(xw, w1)), w2)).astype(f32))
    a = jax.nn.sigmoid((a0 + mm(mm(xa, a1), a2)).astype(f32))
    g = mm(jax.nn.sigmoid(mm(xg, g1)), g2)
    heads = lambda t: t.astype(f32).reshape(n, T, RWKV_HEADS, RWKV_HEAD)
    kk = heads(k * k_k)
    kk = kk / jnp.maximum(jnp.sqrt(jnp.sum(kk * kk, axis=-1, keepdims=True)), 1e-12)
    a_h = heads(a)
    k_h = heads(k.astype(f32) * (1.0 + (a - 1.0) * k_a.astype(f32)))
    r_h, v_h = heads(r), heads(v)
    y, s_fin = wkv7_scan(r_h, heads(log_decay), k_h, v_h, -kk, kk * a_h, s0.astype(f32))
    mean = jnp.mean(y, axis=-1, keepdims=True)
    var = jnp.mean(jnp.square(y - mean), axis=-1, keepdims=True)
    y = ((y - mean) * lax.rsqrt(var + GN_EPS)).reshape(n, T, D) * gn_w.astype(f32) + gn_b.astype(f32)
    bonus = jnp.sum(r_h * k_h * r_k.astype(f32), axis=-1, keepdims=True) * v_h
    y = (y + bonus.reshape(n, T, D)).astype(h.dtype)
    return mm(y * g, w_o), s_fin, h[:, -1]


def shared_kv_rows(x, norm_kv, w_kv, pos, precise=False):
    n, T, _ = x.shape
    kv = dense(rmsnorm(x, norm_kv), w_kv, precise).reshape(n, T, N_BRANCH, 2, NSA_KV_GROUPS, NSA_HEAD_DIM)
    cmp_kv = kv[:, :, 0]
    slc_kv = jnp.stack([rope_partial(kv[:, :, 1, 0], pos), kv[:, :, 1, 1]], axis=2)
    win_kv = jnp.stack([rope_partial(kv[:, :, 2, 0], pos), kv[:, :, 2, 1]], axis=2)
    return cmp_kv, slc_kv, win_kv


CHUNK_LANES = CMP_STRIDE * 2 * NSA_KV_GROUPS * NSA_HEAD_DIM
HIDDEN_LANES = 2 * NSA_KV_GROUPS * 2 * CMP_HIDDEN
PAGES_PER_STEP = 32
CHUNKS_PER_PAGE = PAGE_SIZE // CMP_STRIDE


def _chunk_weights(cmp_w1):
    w = cmp_w1.reshape(2, 2, CMP_STRIDE, NSA_HEAD_DIM, CMP_HIDDEN)
    w = w.transpose(2, 0, 3, 1, 4)
    eye = jnp.eye(2, dtype=w.dtype)
    big = jnp.einsum('ab,cf,jaehk->jacebfhk', eye, jnp.eye(NSA_KV_GROUPS, dtype=w.dtype), w)
    return big.reshape(CHUNK_LANES, HIDDEN_LANES).astype(jnp.bfloat16)


def _token_weights(cmp_w1):
    w = cmp_w1.reshape(2, 2, CMP_STRIDE, NSA_HEAD_DIM, CMP_HIDDEN)
    return w.transpose(2, 0, 3, 1, 4).reshape(CMP_STRIDE, 2, NSA_HEAD_DIM, 2 * CMP_HIDDEN).astype(jnp.bfloat16)


def _paged_chunk_kernel(pt_ref, *refs):
    pages, w_ref, o_ref, xt_ref = (refs[:PAGES_PER_STEP], refs[PAGES_PER_STEP], refs[PAGES_PER_STEP + 1],
                                   refs[PAGES_PER_STEP + 2])
    planes = [(kv, g) for kv in range(2) for g in range(NSA_KV_GROUPS)]
    tok = lax.broadcasted_iota(jnp.int32, (PAGE_SIZE, PAGE_SIZE), 0)
    dst = lax.broadcasted_iota(jnp.int32, (PAGE_SIZE, PAGE_SIZE), 1)
    regroup = (tok == (dst % CHUNKS_PER_PAGE) * CMP_STRIDE + dst // CHUNKS_PER_PAGE).astype(jnp.bfloat16)
    for p in range(PAGES_PER_STEP):
        for i, (kv, g) in enumerate(planes):
            xt_ref[p, i] = _bdot(pages[p][0, kv, g], regroup).T
    width = 2 * CMP_HIDDEN
    for i, (kv, g) in enumerate(planes):
        acc = None
        for j in range(CMP_STRIDE):
            rows = [xt_ref[p, i, j * CHUNKS_PER_PAGE:(j + 1) * CHUNKS_PER_PAGE, :] for p in range(PAGES_PER_STEP)]
            y = _bdot(jnp.concatenate(rows, axis=0), w_ref[j, kv])
            acc = y if acc is None else acc + y
        o_ref[0, :, i * width:(i + 1) * width] = acc


def paged_chunk_hidden(cache, page_table, w_token):
    n, n_pages = page_table.shape
    planes = cache.transpose(0, 2, 3, 4, 1)
    rows = PAGES_PER_STEP * CHUNKS_PER_PAGE

    def page_spec(k):
        return pl.BlockSpec((1,) + planes.shape[1:], lambda b, s, pt: (pt[b, s * PAGES_PER_STEP + k], 0, 0, 0, 0))

    return pl.pallas_call(
        _paged_chunk_kernel,
        grid_spec=pltpu.PrefetchScalarGridSpec(
            num_scalar_prefetch=1,
            grid=(n, n_pages // PAGES_PER_STEP),
            in_specs=[page_spec(k) for k in range(PAGES_PER_STEP)]
                     + [pl.BlockSpec(w_token.shape, lambda b, s, pt: (0, 0, 0, 0))],
            out_specs=pl.BlockSpec((1, rows, HIDDEN_LANES), lambda b, s, pt: (b, s, 0)),
            scratch_shapes=[pltpu.VMEM((PAGES_PER_STEP, 2 * NSA_KV_GROUPS, PAGE_SIZE, NSA_HEAD_DIM), jnp.float32)],
        ),
        out_shape=jax.ShapeDtypeStruct((n, n_pages * CHUNKS_PER_PAGE, HIDDEN_LANES), jnp.float32),
        compiler_params=pltpu.CompilerParams(dimension_semantics=("parallel", "arbitrary"),
                                             vmem_limit_bytes=48 * 1024 * 1024),
        name="paged_chunk_hidden",
    )(page_table.astype(jnp.int32), *([planes] * PAGES_PER_STEP), w_token)


def _matmul_kernel(x_ref, w_ref, o_ref):
    o_ref[...] = jnp.dot(x_ref[...].astype(jnp.bfloat16), w_ref[...],
                         preferred_element_type=jnp.float32).astype(o_ref.dtype)


def matmul(x, w, rows_per_step=512, out_dtype=jnp.float32):
    m, k = x.shape
    nn = w.shape[1]
    tm = min(rows_per_step, m)
    return pl.pallas_call(
        _matmul_kernel,
        grid=(m // tm,),
        in_specs=[pl.BlockSpec((tm, k), lambda i: (i, 0)), pl.BlockSpec((k, nn), lambda i: (0, 0))],
        out_specs=pl.BlockSpec((tm, nn), lambda i: (i, 0)),
        out_shape=jax.ShapeDtypeStruct((m, nn), out_dtype),
        compiler_params=pltpu.CompilerParams(dimension_semantics=("parallel",),
                                             vmem_limit_bytes=48 * 1024 * 1024),
        name="matmul",
    )(x, w.astype(jnp.bfloat16))


def _split_bf16(x):
    hi = x.astype(jnp.bfloat16)
    return hi, (x - hi.astype(jnp.float32)).astype(jnp.bfloat16)


def _matmul3_kernel(x_ref, w_ref, o_ref):
    x_hi, x_lo = _split_bf16(x_ref[...])
    w_hi, w_lo = _split_bf16(w_ref[...])
    dot = functools.partial(jnp.dot, preferred_element_type=jnp.float32)
    o_ref[...] = dot(x_hi, w_hi) + (dot(x_hi, w_lo) + dot(x_lo, w_hi))


def matmul_precise(x, w, cols_per_step=512):
    m, k = x.shape
    nn = w.shape[1]
    pad = (-nn) % 128
    if pad:
        w = jnp.pad(w, ((0, 0), (0, pad)))
    tn = math.gcd(cols_per_step, nn + pad)
    tm = min(m, 512)
    out = pl.pallas_call(
        _matmul3_kernel,
        grid=(m // tm, (nn + pad) // tn),
        in_specs=[pl.BlockSpec((tm, k), lambda i, j: (i, 0)), pl.BlockSpec((k, tn), lambda i, j: (0, j))],
        out_specs=pl.BlockSpec((tm, tn), lambda i, j: (i, j)),
        out_shape=jax.ShapeDtypeStruct((m, nn + pad), jnp.float32),
        compiler_params=pltpu.CompilerParams(dimension_semantics=("parallel", "parallel")),
        name="matmul_precise",
    )(x, w)
    return out[:, :nn] if pad else out


def dense(x, w, precise=False):
    if precise:
        return matmul_precise(x.reshape(-1, x.shape[-1]), w).reshape(x.shape[:-1] + (w.shape[1],))
    return x @ w


def compress_from_hidden(hidden, pos_emb, w1, b1, w2):
    n, C = hidden.shape[:2]
    rows = -(-C // 8) * 8
    if rows != C:
        hidden = jnp.pad(hidden, ((0, 0), (0, rows - C), (0, 0)))
    w1r = w1.reshape(2, CMP_BLOCK, NSA_HEAD_DIM, CMP_HIDDEN)
    bias = jnp.einsum('ajd,ajdh->ah', pos_emb, w1r) + b1
    out = pl.pallas_call(
        functools.partial(_compress_finish_kernel, n_blocks=C),
        grid=(n,),
        in_specs=[pl.BlockSpec((1, rows, HIDDEN_LANES), lambda b: (b, 0, 0)),
                  pl.BlockSpec(bias.shape, lambda b: (0, 0)), pl.BlockSpec(w2.shape, lambda b: (0, 0, 0))],
        out_specs=pl.BlockSpec((1, rows, 2 * SLAB), lambda b: (b, 0, 0)),
        out_shape=jax.ShapeDtypeStruct((n, rows, 2 * SLAB), jnp.float32),
        compiler_params=pltpu.CompilerParams(dimension_semantics=("parallel",)),
        name="compress_finish",
    )(hidden, bias, w2.astype(jnp.bfloat16))[:, :C]
    shape = (n, C, NSA_KV_GROUPS, NSA_HEAD_DIM)
    return out[:, :, :SLAB].reshape(shape), out[:, :, SLAB:].reshape(shape)


def _compress_finish_kernel(h_ref, bias_ref, w2_ref, o_ref, *, n_blocks):
    h = h_ref[0]
    rows = h.shape[0]
    keep = lax.broadcasted_iota(jnp.int32, (rows, CMP_HIDDEN), 0) < n_blocks - 1
    outs = []
    for i in range(2 * NSA_KV_GROUPS):
        kv = i // NSA_KV_GROUPS
        first = h[:, 2 * i * CMP_HIDDEN:(2 * i + 1) * CMP_HIDDEN]
        second = h[:, (2 * i + 1) * CMP_HIDDEN:(2 * i + 2) * CMP_HIDDEN]
        nxt = jnp.where(keep, pltpu.roll(second, rows - 1, 0), 0.0)
        act = jax.nn.gelu(first + nxt + bias_ref[kv:kv + 1, :])
        outs.append(_bdot(act, w2_ref[kv]))
    o_ref[0] = jnp.concatenate(outs, axis=-1)


def rows_chunk_hidden(kv_rows, w_chunk):
    n, T = kv_rows.shape[:2]
    chunks = kv_rows.reshape(n * T // CMP_STRIDE, CHUNK_LANES)
    return matmul(chunks, w_chunk, rows_per_step=256).reshape(n, T // CMP_STRIDE, HIDDEN_LANES)


def nsa_query(h, w_in, precise=False):
    n, T, _ = h.shape
    proj = dense(h, w_in, precise)
    q = proj[..., :NSA_HEADS * NSA_HEAD_DIM].reshape(n, T, NSA_KV_GROUPS, NSA_GROUP_HEADS, NSA_HEAD_DIM)
    gate = proj[..., NSA_HEADS * NSA_HEAD_DIM:].reshape(n, T, NSA_KV_GROUPS, NSA_GROUP_HEADS, N_BRANCH)
    return q, gate


def compressed_branch_and_selection(q, t_pos, kc, vc):
    scale = NSA_HEAD_DIM ** -0.5
    n_c = kc.shape[1]
    vis = (jnp.arange(n_c) * CMP_STRIDE + CMP_BLOCK - 1)[None, :] <= t_pos[:, None]
    p_cmp = masked_softmax(jnp.einsum('nqghd,ncgd->nghqc', q, kc) * scale, vis)
    o_cmp = jnp.einsum('nghqc,ncgd->nqghd', p_cmp.astype(vc.dtype), vc)
    ratio = SLC_BLOCK // CMP_STRIDE
    lead = CMP_BLOCK // CMP_STRIDE - 1
    n_s = n_c // ratio
    pg = jnp.pad(jnp.sum(p_cmp, axis=2), ((0, 0), (0, 0), (0, 0), (lead, 0)))
    p_slc = pg[..., 0:ratio * n_s:ratio]
    for o in range(1, ratio + lead):
        p_slc = p_slc + pg[..., o:o + ratio * n_s:ratio]
    jb = jnp.arange(n_s)[None, :]
    jt = (t_pos // SLC_BLOCK)[:, None]
    forced = (jb == 0) | (jb == jt) | (jb == jt - 1)
    score = jnp.where(jb > jt, -jnp.inf, jnp.where(forced, FORCED_SCORE, p_slc))
    before = (score[..., :, None] > score[..., None, :]) | (
        (score[..., :, None] == score[..., None, :]) & (jb[0][:, None] < jb[0][None, :]))
    rank = jnp.sum(before, axis=-2)
    slots = jnp.arange(min(SLC_TOPN, n_s))
    idx = jnp.sum(jnp.where(rank[..., None, :] == slots[:, None], jb[0], 0), axis=-1)
    return o_cmp, idx


def _dot3(x, y, dims):
    x_hi, x_lo = _split_bf16(x)
    y_hi, y_lo = _split_bf16(y)
    dot = lambda a, b: lax.dot_general(a, b, dims, preferred_element_type=jnp.float32)
    return dot(x_hi, y_hi) + (dot(x_hi, y_lo) + dot(x_lo, y_hi))


def _decode_kernel(half_ref, new_ref, phys_ref, q_ref, rows_ref, win_ref, *refs):
    b = pl.program_id(0)
    n_sel = len(refs) - 1
    pages, o_ref = refs[:n_sel], refs[n_sel]
    k_per_group = n_sel // NSA_KV_GROUPS
    hg, dh = NSA_GROUP_HEADS, NSA_HEAD_DIM
    tok_half = lax.broadcasted_iota(jnp.int32, (hg, PAGE_SIZE), 1) // SLC_BLOCK
    rows = rows_ref[0]

    def attend(qg, scores, values, k_new, v_new, new_ok):
        s_new = jnp.where(new_ok, jnp.sum(qg * k_new, axis=-1, keepdims=True), MASKED)
        m = s_new
        for s in scores:
            m = jnp.maximum(m, jnp.max(s, axis=-1, keepdims=True))
        e_new = jnp.exp(s_new - m)
        total, out = e_new, e_new * v_new
        for s, v in zip(scores, values):
            e = jnp.exp(s - m)
            total = total + jnp.sum(e, axis=-1, keepdims=True)
            out = out + _dot3(e, v, _NT)
        return out / jnp.maximum(total, 1e-30)

    outs = []
    for g in range(NSA_KV_GROUPS):
        qg = q_ref[0, g * hg:(g + 1) * hg, :]
        lanes = slice(g * dh, (g + 1) * dh)
        scores, values = [], []
        any_new = jnp.int32(0)
        for k in range(k_per_group):
            j = g * k_per_group + k
            plane = pages[j][0]
            is_new = new_ref[b, j]
            ok = (tok_half == half_ref[b, j]) & (is_new == 0)
            scores.append(jnp.where(ok, _dot3(qg, plane[0, 0], _NN), MASKED))
            values.append(plane[1, 0])
            any_new = jnp.maximum(any_new, is_new)
        o_slc = attend(qg, scores, values, rows[0:1, lanes], rows[1:2, lanes], any_new > 0)
        o_win = attend(qg, [_dot3(qg, win_ref[0, 0, g], _NN)], [win_ref[0, 1, g]],
                       rows[2:3, lanes], rows[3:4, lanes], True)
        outs.append(jnp.concatenate([o_slc, o_win], axis=-1))
    o_ref[0] = jnp.concatenate(outs, axis=0)


def decode_attention(q_rot, idx, page_table, cache_slc_kv, cache_win_kv, slc_new, win_new):
    n = q_rot.shape[0]
    sub = PAGE_SIZE // SLC_BLOCK
    n_past_blk = PAST_LEN // SLC_BLOCK
    assert cache_win_kv.shape[1] <= WINDOW and cache_win_kv.shape[1] <= PAST_LEN
    k_sel = idx.shape[-1]
    flat = idx.reshape(n, NSA_KV_GROUPS * k_sel)
    past = jnp.minimum(flat, n_past_blk - 1)
    hit = (past // sub)[:, :, None] == jnp.arange(page_table.shape[1])[None, None, :]
    phys = jnp.sum(jnp.where(hit, page_table[:, None, :], 0), axis=-1).astype(jnp.int32)
    half = (past % sub).astype(jnp.int32)
    is_new = (flat >= n_past_blk).astype(jnp.int32)
    pool = cache_slc_kv.transpose(0, 2, 3, 4, 1)
    window = cache_win_kv.transpose(0, 2, 3, 4, 1)
    rows = jnp.concatenate([slc_new.reshape(n, 2, SLAB), win_new.reshape(n, 2, SLAB)], axis=1)

    def page_spec(j):
        g = j // k_sel
        return pl.BlockSpec((1, 2, 1, NSA_HEAD_DIM, PAGE_SIZE), lambda b, hf, nw, ph: (ph[b, j], 0, g, 0, 0))

    out = pl.pallas_call(
        _decode_kernel,
        grid_spec=pltpu.PrefetchScalarGridSpec(
            num_scalar_prefetch=3,
            grid=(n,),
            in_specs=[pl.BlockSpec((1,) + q_rot.shape[1:], lambda b, hf, nw, ph: (b, 0, 0)),
                      pl.BlockSpec((1,) + rows.shape[1:], lambda b, hf, nw, ph: (b, 0, 0)),
                      pl.BlockSpec((1,) + window.shape[1:], lambda b, hf, nw, ph: (b, 0, 0, 0, 0))]
                     + [page_spec(j) for j in range(NSA_KV_GROUPS * k_sel)],
            out_specs=pl.BlockSpec((1, NSA_HEADS, 2 * NSA_HEAD_DIM), lambda b, hf, nw, ph: (b, 0, 0)),
        ),
        out_shape=jax.ShapeDtypeStruct((n, NSA_HEADS, 2 * NSA_HEAD_DIM), jnp.float32),
        compiler_params=pltpu.CompilerParams(dimension_semantics=("parallel",)),
        name="decode_attention",
    )(half, is_new, phys, q_rot, rows, window, *([pool] * (NSA_KV_GROUPS * k_sel)))
    return out[:, :, :NSA_HEAD_DIM], out[:, :, NSA_HEAD_DIM:]


NSA_SLABS = NSA_GROUP_HEADS
NSA_TILES = NSA_SLABS * NSA_KV_GROUPS
SELECTED_KEY_BLOCK = 512
DENSE_BATCH = 16
MASKED = -1e30
SLC_PER_CMP = SLC_BLOCK // CMP_STRIDE
CMP_LEAD = CMP_BLOCK // CMP_STRIDE - 1
SLAB = NSA_KV_GROUPS * NSA_HEAD_DIM
SEL_LANES = 64


def _group_tiles(q_ref, qs_ref):
    tq = q_ref.shape[1]
    lane = lax.broadcasted_iota(jnp.int32, (tq, SLAB), 1)
    for i in range(NSA_SLABS):
        qs = q_ref[0, :, i * 128:(i + 1) * 128]
        for g in range(NSA_KV_GROUPS):
            in_group = (lane >= g * NSA_HEAD_DIM) & (lane < (g + 1) * NSA_HEAD_DIM)
            qs_ref[2 * i + g] = jnp.where(in_group, qs, jnp.zeros_like(qs)).astype(jnp.bfloat16)


def _merge_groups(o_ref, tiles, gate_ref, branch):
    tq = tiles[0].shape[0]
    lane = lax.broadcasted_iota(jnp.int32, (tq, SLAB), 1)
    gates = gate_ref[0]
    col = lambda g, i: (g * NSA_GROUP_HEADS + i) * N_BRANCH + branch
    for i in range(NSA_SLABS):
        lo = tiles[2 * i] * gates[:, col(0, i):col(0, i) + 1]
        hi = tiles[2 * i + 1] * gates[:, col(1, i):col(1, i) + 1]
        o_ref[0, :, i * 128:(i + 1) * 128] = jnp.where(lane < NSA_HEAD_DIM, lo, hi)


def _nsa_cmp_kernel(q_ref, kc_ref, vc_ref, gate_ref, o_ref, sel_ref, qs_ref):
    qi = pl.program_id(1)
    tq = q_ref.shape[1]
    nc = kc_ref.shape[1]
    ns = nc // SLC_PER_CMP
    t0 = qi * tq
    _group_tiles(q_ref, qs_ref)
    kc = kc_ref[0]
    vc = vc_ref[0]
    t_row = t0 + lax.broadcasted_iota(jnp.int32, (tq, nc), 0)
    c_pos = lax.broadcasted_iota(jnp.int32, (tq, nc), 1) * CMP_STRIDE + (CMP_BLOCK - 1)
    vis = c_pos <= t_row
    tiles = range(NSA_TILES)
    s = [_mm(qs_ref[r], kc, _NT) for r in tiles]
    p = []
    for r in tiles:
        sr = jnp.where(vis, s[r], MASKED)
        m = jnp.max(sr, axis=-1, keepdims=True)
        e = jnp.where(vis, jnp.exp(sr - m), 0.0)
        p.append(e / jnp.maximum(jnp.sum(e, axis=-1, keepdims=True), 1e-30))
    _merge_groups(o_ref, [_mm(p[r], vc, _NN) for r in tiles], gate_ref, 0)

    jrow = lax.broadcasted_iota(jnp.int32, (ns, nc), 0)
    ccol = lax.broadcasted_iota(jnp.int32, (ns, nc), 1)
    pool = ((ccol >= SLC_PER_CMP * jrow - CMP_LEAD) & (ccol < SLC_PER_CMP * (jrow + 1))).astype(jnp.bfloat16)
    jb = lax.broadcasted_iota(jnp.int32, (ns, tq), 0)
    jt = (t0 + lax.broadcasted_iota(jnp.int32, (ns, tq), 1)) // SLC_BLOCK
    forced = (jb == 0) | (jb == jt) | (jb == jt - 1)
    sel_t = []
    for g in range(NSA_KV_GROUPS):
        pg = p[g]
        for i in range(1, NSA_SLABS):
            pg = pg + p[2 * i + g]
        pg_hi = pg.astype(jnp.bfloat16)
        pg_lo = (pg - pg_hi.astype(jnp.float32)).astype(jnp.bfloat16)
        p_slc = (lax.dot_general(pool, pg_hi, _NT, preferred_element_type=jnp.float32)
                 + lax.dot_general(pool, pg_lo, _NT, preferred_element_type=jnp.float32))
        score = jnp.where(jb > jt, -jnp.inf, jnp.where(forced, FORCED_SCORE, p_slc))
        rank = jnp.zeros((ns, tq), jnp.float32)
        for i in range(ns):
            row = score[i:i + 1, :]
            tie = jnp.where(jb > i, 1.0, 0.0)
            rank = rank + jnp.where(row > score, 1.0, 0.0) + jnp.where(row == score, tie, 0.0)
        sel_t.append(jnp.where(rank < min(SLC_TOPN, ns), 1.0, 0.0))
        if ns < SEL_LANES:
            sel_t.append(jnp.zeros((SEL_LANES - ns, tq), jnp.float32))
    sel_ref[0] = jnp.concatenate(sel_t, axis=0).T.astype(sel_ref.dtype)


def _nsa_dense_kernel(q_ref, k_ref, vt_ref, sel_ref, gate_ref, o_ref, qs_ref, m_ref, acc_ref, *, windowed, kblock):
    qi = pl.program_id(1)
    tq = q_ref.shape[1]
    t0 = qi * tq
    _group_tiles(q_ref, qs_ref)
    m_ref[...] = jnp.full(m_ref.shape, MASKED, jnp.float32)
    acc_ref[...] = jnp.zeros(acc_ref.shape, jnp.float32)
    k_row = lax.broadcasted_iota(jnp.int32, (kblock, tq), 0)
    t_lane = t0 + lax.broadcasted_iota(jnp.int32, (kblock, tq), 1)
    tiles = range(NSA_TILES)
    first = 0
    last = 0 if windowed else (t0 + tq - 1) // kblock

    def key_block(kb, carry):
        start = pl.multiple_of(jnp.maximum(t0 - WINDOW, 0) if windowed else kb * kblock, tq)
        kblk = k_ref[0, pl.ds(start, kblock), :]
        vt = vt_ref[0, :, pl.ds(start, kblock)]
        k_pos = start + k_row
        if windowed:
            allowed = [(k_pos <= t_lane) & (k_pos >= t_lane - WINDOW)] * NSA_KV_GROUPS
        else:
            sel = sel_ref[0]
            n_sel = NSA_KV_GROUPS * SEL_LANES
            e_key = lax.broadcasted_iota(jnp.int32, (kblock, n_sel), 0)
            e_lane = lax.broadcasted_iota(jnp.int32, (kblock, n_sel), 1)
            blk = kb * (kblock // SLC_BLOCK) + e_key // SLC_BLOCK
            allowed = []
            for g in range(NSA_KV_GROUPS):
                expand = (e_lane == blk + g * SEL_LANES).astype(jnp.bfloat16)
                picked = lax.dot_general(expand, sel, _NT, preferred_element_type=jnp.float32)
                allowed.append((picked > 0.5) & (k_pos <= t_lane))
        bias = [jnp.where(ok, 0.0, MASKED) for ok in allowed]
        d_row = lax.broadcasted_iota(jnp.int32, vt.shape, 0)
        vt_g = [jnp.where((d_row >= g * NSA_HEAD_DIM) & (d_row < (g + 1) * NSA_HEAD_DIM), vt, jnp.ones_like(vt))
                for g in range(NSA_KV_GROUPS)]
        s = [_mm(kblk, qs_ref[r], _NT) for r in tiles]
        for lo in range(0, NSA_TILES, DENSE_BATCH):
            batch = range(lo, lo + DENSE_BATCH)
            p, alpha = {}, {}
            for r in batch:
                sr = s[r] + bias[r % NSA_KV_GROUPS]
                m_prev = m_ref[r]
                m_new = jnp.maximum(m_prev, jnp.max(sr, axis=0, keepdims=True))
                alpha[r] = jnp.exp(m_prev - m_new)
                p[r] = jnp.exp(sr - m_new).astype(jnp.bfloat16)
                m_ref[r] = m_new
            pv = {r: jnp.dot(vt_g[r % NSA_KV_GROUPS], p[r], preferred_element_type=jnp.float32) for r in batch}
            for r in batch:
                acc_ref[r] = alpha[r] * acc_ref[r] + pv[r]
        return carry

    lax.fori_loop(first, last + 1, key_block, 0)
    out = []
    for r in tiles:
        acc = acc_ref[r]
        sum_row = (1 - r % NSA_KV_GROUPS) * NSA_HEAD_DIM
        out.append((acc / jnp.maximum(acc[sum_row:sum_row + 1, :], 1e-30)).T)
    _merge_groups(o_ref, out, gate_ref, 2 if windowed else 1)


def nsa_prompt_attention(q, q_rot, kc, vc, kv_att, gates):
    n, T, D = q.shape
    tq = Q_BLOCK
    qspec = pl.BlockSpec((1, tq, D), lambda b, i: (b, i, 0))
    whole = lambda a: pl.BlockSpec((1,) + a.shape[1:], lambda b, i: (b, 0, 0))
    lanes = lambda j: pl.BlockSpec((1, T, SLAB), lambda b, i: (b, 0, j))
    n_sel = NSA_KV_GROUPS * SEL_LANES
    sel_spec = pl.BlockSpec((1, tq, n_sel), lambda b, i: (b, i, 0))
    gate_spec = pl.BlockSpec((1, tq, gates.shape[2]), lambda b, i: (b, i, 0))
    params = pltpu.CompilerParams(dimension_semantics=("parallel", "arbitrary"),
                                  vmem_limit_bytes=48 * 1024 * 1024)
    qs_scratch = pltpu.VMEM((NSA_TILES, tq, SLAB), jnp.bfloat16)
    o_cmp, sel = pl.pallas_call(
        _nsa_cmp_kernel,
        grid=(n, T // tq),
        in_specs=[qspec, whole(kc), whole(vc), gate_spec],
        out_specs=[qspec, sel_spec],
        out_shape=[jax.ShapeDtypeStruct((n, T, D), jnp.float32),
                   jax.ShapeDtypeStruct((n, T, n_sel), jnp.bfloat16)],
        scratch_shapes=[qs_scratch],
        compiler_params=params,
        name="nsa_cmp_select",
    )(q, kc, vc, gates)
    stat = pltpu.VMEM((NSA_TILES, 1, tq), jnp.float32)
    acc = pltpu.VMEM((NSA_TILES, SLAB, tq), jnp.float32)
    v_t = jnp.swapaxes(jnp.concatenate([kv_att[:, :, SLAB:2 * SLAB], kv_att[:, :, 3 * SLAB:]], axis=-1), 1, 2)
    rows = lambda j: pl.BlockSpec((1, SLAB, T), lambda b, i: (b, j, 0))

    def dense(windowed, branch, name):
        kblock = min(WINDOW + tq, T) if windowed else min(SELECTED_KEY_BLOCK, T)
        return pl.pallas_call(
            functools.partial(_nsa_dense_kernel, windowed=windowed, kblock=kblock),
            grid=(n, T // tq),
            in_specs=[qspec, lanes(2 * branch), rows(branch), sel_spec, gate_spec],
            out_specs=qspec,
            out_shape=jax.ShapeDtypeStruct((n, T, D), jnp.float32),
            scratch_shapes=[qs_scratch, stat, acc],
            compiler_params=params,
            name=name,
        )(q_rot, kv_att, v_t, sel, gates)

    return o_cmp, dense(False, 0, "nsa_selected"), dense(True, 1, "nsa_window")


def _rope_tables(pos):
    half = ROT_DIM // 2
    inv = ROPE_THETA ** (-2.0 * jnp.arange(half, dtype=jnp.float32) / ROT_DIM)
    ang = pos.astype(jnp.float32)[:, None] * inv[None, :]
    rest = NSA_HEAD_DIM - ROT_DIM
    cos = jnp.concatenate([jnp.cos(ang), jnp.cos(ang), jnp.ones((pos.shape[0], rest), jnp.float32)], axis=1)
    sin = jnp.concatenate([-jnp.sin(ang), jnp.sin(ang), jnp.zeros((pos.shape[0], rest), jnp.float32)], axis=1)
    return jnp.tile(cos, (1, NSA_KV_GROUPS)), jnp.tile(sin, (1, NSA_KV_GROUPS))


def _nsa_pre_kernel(x_ref, gkv_ref, gmix_ref, cos_ref, sin_ref, wkv_ref, wq_ref, wg_ref,
                    cmp_ref, slc_ref, win_ref, kvb_ref, q_ref, qr_ref, gate_ref):
    xh = _rms(x_ref[0])
    cos, sin = cos_ref[...], sin_ref[...]
    low = lax.broadcasted_iota(jnp.int32, cos.shape, 1) % NSA_HEAD_DIM < ROT_DIM // 2

    def rope(t):
        swapped = jnp.where(low, pltpu.roll(t, SLAB - ROT_DIM // 2, 1), pltpu.roll(t, ROT_DIM // 2, 1))
        return t * cos + swapped * sin

    kv = _bdot(xh * gkv_ref[...], wkv_ref[...])
    part = lambda j: kv[:, j * SLAB:(j + 1) * SLAB]
    k_slc, k_win = rope(part(2)), rope(part(4))
    cmp_ref[0] = kv[:, :2 * SLAB]
    slc_ref[0] = jnp.concatenate([k_slc, part(3)], axis=-1)
    win_ref[0] = jnp.concatenate([k_win, part(5)], axis=-1)
    kvb_ref[0] = jnp.concatenate([k_slc, part(3), k_win, part(5)], axis=-1).astype(jnp.bfloat16)
    h = xh * gmix_ref[...]
    q = _bdot(h, wq_ref[...])
    q_ref[0] = q.astype(jnp.bfloat16)
    qr_ref[0] = jnp.concatenate([rope(q[:, i * SLAB:(i + 1) * SLAB]) for i in range(NSA_SLABS)],
                                axis=-1).astype(jnp.bfloat16)
    gate_ref[0] = jax.nn.sigmoid(_bdot(h, wg_ref[...]))


def nsa_pre(x, pos, norm_kv, norm_mix, w_kv, w_in):
    n, T, D = x.shape
    tm = min(ROW_TILE, T)
    nq = NSA_HEADS * NSA_HEAD_DIM
    cos, sin = _rope_tables(pos)
    w_q = (_to_slabs(w_in[:, :nq]) * NSA_HEAD_DIM ** -0.5).astype(jnp.bfloat16)
    w_g = jnp.pad(w_in[:, nq:], ((0, 0), (0, SLAB - (w_in.shape[1] - nq)))).astype(jnp.bfloat16)
    seq = lambda width: pl.BlockSpec((1, tm, width), lambda b, t: (b, t, 0))
    full = lambda a: pl.BlockSpec(a.shape, lambda b, t: (0,) * a.ndim)
    table = pl.BlockSpec((tm, SLAB), lambda b, t: (t, 0))
    consts = [norm_kv.reshape(1, D), norm_mix.reshape(1, D)]
    weights = [w_kv.astype(jnp.bfloat16), w_q, w_g]
    widths = [2 * SLAB, 2 * SLAB, 2 * SLAB, 4 * SLAB, nq, nq, SLAB]
    dtypes = [jnp.float32] * 3 + [jnp.bfloat16] * 3 + [jnp.float32]
    return pl.pallas_call(
        _nsa_pre_kernel,
        grid=(n, T // tm),
        in_specs=[seq(D)] + [full(a) for a in consts] + [table, table] + [full(a) for a in weights],
        out_specs=[seq(w) for w in widths],
        out_shape=[jax.ShapeDtypeStruct((n, T, w), dt) for w, dt in zip(widths, dtypes)],
        compiler_params=pltpu.CompilerParams(dimension_semantics=("parallel", "parallel"),
                                             vmem_limit_bytes=48 * 1024 * 1024),
        name="nsa_pre",
    )(x, *consts, cos, sin, *weights)


def _to_slabs(x):
    lead = x.shape[:-1]
    x = x.reshape(lead + (NSA_KV_GROUPS, NSA_GROUP_HEADS, NSA_HEAD_DIM))
    return jnp.swapaxes(x, -3, -2).reshape(lead + (NSA_HEADS * NSA_HEAD_DIM,))


def nsa_layer_prompt(x, norm_kv, norm_mix, w_kv, cmp_pos, cmp_w1, cmp_b1, cmp_w2, w_in, w_o, norm_next, w_router):
    n, T, D = x.shape
    cmp_kv, slc_kv, win_kv, kv_att, q, q_rot, gates = nsa_pre(x, jnp.arange(T), norm_kv, norm_mix, w_kv, w_in)
    hidden = rows_chunk_hidden(cmp_kv, _chunk_weights(cmp_w1))
    kc, vc = compress_from_hidden(hidden, cmp_pos, cmp_w1, cmp_b1, cmp_w2)
    lanes = lambda a: a.reshape(n, a.shape[1], SLAB).astype(jnp.bfloat16)
    branches = nsa_prompt_attention(q, q_rot, lanes(kc), lanes(vc), kv_att, gates)
    flat = lambda a: a.reshape(n * T, a.shape[-1])
    x_out, h_out, logits = out_proj([flat(o) for o in branches], None, flat(x), _to_slabs(w_o.T).T, norm_next,
                                    w_router)
    return x_out, h_out, logits, cmp_kv, slc_kv, win_kv


def sample_kv_context(cmp_new, slc_new, win_new, cache_cmp_kv, cache_slc_kv, cache_win_kv, page_table,
                      cmp_pos, cmp_w1, cmp_b1, cmp_w2):
    n, S = cmp_new.shape[:2]
    n_new_blk = -(-S // SLC_BLOCK)
    pad = ((0, 0), (0, n_new_blk * SLC_BLOCK - S), (0, 0), (0, 0), (0, 0))
    w_chunk = _chunk_weights(cmp_w1)
    hidden = jnp.concatenate([paged_chunk_hidden(cache_cmp_kv, page_table, _token_weights(cmp_w1)),
                              rows_chunk_hidden(jnp.pad(cmp_new.astype(cache_cmp_kv.dtype), pad), w_chunk)], axis=1)
    kc, vc = compress_from_hidden(hidden, cmp_pos, cmp_w1, cmp_b1, cmp_w2)
    win_all = jnp.concatenate([cache_win_kv, win_new.astype(cache_win_kv.dtype)], axis=1)
    return kc, vc, win_all


def sample_nsa(h, kc, vc, slc_new, win_new, cache_slc_kv, cache_win_kv, page_table, w_in, w_o):
    n, S, _ = h.shape
    assert S == 1
    q, gate = nsa_query(h, w_in, precise=True)
    t_pos = PAST_LEN + jnp.arange(S)
    with jax.default_matmul_precision("highest"):
        o_cmp, idx = compressed_branch_and_selection(q, t_pos, kc, vc)
    q_rot = (rope_partial(q, t_pos) * NSA_HEAD_DIM ** -0.5).reshape(n, NSA_HEADS, NSA_HEAD_DIM)
    o_slc, o_win = decode_attention(q_rot, idx[:, :, 0], page_table, cache_slc_kv, cache_win_kv, slc_new, win_new)
    g = jax.nn.sigmoid(gate.astype(jnp.float32)).astype(q.dtype)
    o = g[..., 0:1] * o_cmp + g[..., 1:2] * o_slc.reshape(q.shape) + g[..., 2:3] * o_win.reshape(q.shape)
    return dense(o.reshape(n, S, NSA_HEADS * NSA_HEAD_DIM), w_o, precise=True)


def kernel(x_prompt, x_sample, state_wkv, state_shift, cache_cmp_kv, cache_slc_kv, cache_win_kv, page_table, norm_mix, norm_ffn, norm_kv, norm_final, rw_mu, rw_w_rkv, rw_w0, rw_w1, rw_w2, rw_a0, rw_a1, rw_a2, rw_g1, rw_g2, rw_k_k, rw_k_a, rw_r_k, rw_gn_w, rw_gn_b, rw_w_o, nsa_w_kv, nsa_cmp_pos, nsa_cmp_w1, nsa_cmp_b1, nsa_cmp_w2, nsa_w_in, nsa_w_o, ffn_w_gu, ffn_w_down, moe_router, moe_w_gu, moe_w_down):
    cmp_params = (nsa_cmp_pos, nsa_cmp_w1, nsa_cmp_b1, nsa_cmp_w2)
    assert DEPTH == 2 and N_A_LAYERS == 1
    D = D_MODEL
    moe_gu, moe_down = moe_w_gu[0].astype(jnp.bfloat16), moe_w_down[0].astype(jnp.bfloat16)
    x_prompt, moe_gu, moe_down = lax.optimization_barrier((x_prompt, moe_gu, moe_down))

    n_p, T = x_prompt.shape[:2]
    x1, h1, wkv_fin, h_last = rwkv7_layer(
        x_prompt, jnp.zeros((n_p, D), x_prompt.dtype), jnp.zeros((n_p, RWKV_HEADS, RWKV_HEAD, RWKV_HEAD), jnp.float32),
        norm_mix[0], rw_mu[0], rw_w_rkv[0], rw_w0[0], rw_w1[0], rw_w2[0], rw_a0[0], rw_a1[0], rw_a2[0],
        rw_g1[0], rw_g2[0], rw_k_k[0], rw_k_a[0], rw_r_k[0], rw_gn_w[0], rw_gn_b[0], rw_w_o[0], norm_ffn[0])
    wkv_p, shift_p = wkv_fin[None], h_last[None]
    x2 = swiglu_residual(x1, h1, ffn_w_gu[0], ffn_w_down[0])
    x_p, h_p, logits_p, cmp_rows, slc_rows, win_rows = nsa_layer_prompt(
        x2.reshape(n_p, T, D), norm_kv, norm_mix[1], nsa_w_kv, *cmp_params, nsa_w_in[0], nsa_w_o[0], norm_ffn[1],
        moe_router[0])
    kv_shape = (n_p, T, 2, NSA_KV_GROUPS, NSA_HEAD_DIM)
    cmp_kv_p, slc_kv_p = cmp_rows.reshape(kv_shape), slc_rows.reshape(kv_shape)
    win_kv_p = win_rows.reshape(kv_shape)[:, -min(WINDOW, T):]

    pos_s = PAST_LEN + jnp.arange(x_sample.shape[1], dtype=jnp.int32)
    h = rmsnorm(x_sample, norm_mix[0])
    y, s_fin, h_last_s = rwkv7_time_mix(
        h, state_shift[0], state_wkv[0], rw_mu[0], rw_w_rkv[0], rw_w0[0], rw_w1[0], rw_w2[0],
        rw_a0[0], rw_a1[0], rw_a2[0], rw_g1[0], rw_g2[0], rw_k_k[0], rw_k_a[0],
        rw_r_k[0], rw_gn_w[0], rw_gn_b[0], rw_w_o[0], precise=True)
    wkv_s, shift_s = s_fin[None], h_last_s[None]
    x_s = x_sample + y
    x_s = x_s + swiglu(rmsnorm(x_s, norm_ffn[0]), ffn_w_gu[0], ffn_w_down[0], precise=True)
    cmp_kv_s, slc_kv_s, win_new = shared_kv_rows(x_s, norm_kv, nsa_w_kv, pos_s, precise=True)
    kc_s, vc_s, win_all = sample_kv_context(cmp_kv_s, slc_kv_s, win_new, cache_cmp_kv, cache_slc_kv, cache_win_kv,
                                            page_table, *cmp_params)
    x_s = x_s + sample_nsa(rmsnorm(x_s, norm_mix[1]), kc_s, vc_s, slc_kv_s, win_new, cache_slc_kv, cache_win_kv,
                           page_table, nsa_w_in[0], nsa_w_o[0])
    win_kv_s = win_all[:, -cache_win_kv.shape[1]:]
    x_s = x_s.reshape(-1, D)
    h_s = rmsnorm(x_s, norm_ffn[1])
    logits_s = matmul_precise(h_s, moe_router[0])

    n_tok_p = n_p * T
    logits = jnp.concatenate([logits_p[:, :N_EXPERTS], logits_s], axis=0)
    y0, y1, gate = moe_swiglu(jnp.concatenate([h_p, h_s.astype(jnp.bfloat16)], axis=0), logits,
                              moe_gu, moe_down)
    gate = jnp.pad(gate, ((0, 0), (0, 128 - TOP_K)))
    y_prompt = combine_norm(x_p, y0[:n_tok_p], y1[:n_tok_p], gate[:n_tok_p], norm_final).reshape(x_prompt.shape)
    y_sample = combine_norm(x_s, y0[n_tok_p:], y1[n_tok_p:], gate[n_tok_p:], norm_final).reshape(x_sample.shape)


    return (y_prompt, y_sample, wkv_p, shift_p, cmp_kv_p, slc_kv_p, win_kv_p,
            wkv_s, shift_s, cmp_kv_s, slc_kv_s, win_kv_s)
```

```python
import functools
import math

import jax
import jax.numpy as jnp
from jax import lax
from jax.experimental import pallas as pl
from jax.experimental.pallas import tpu as pltpu

D_MODEL = 1024
DEPTH = 2
PAST_LEN = 16384
PAGE_SIZE = 128
N_A_LAYERS = DEPTH // 2
RWKV_HEAD = 64
RWKV_HEADS = D_MODEL // RWKV_HEAD
DECAY_SCALE = math.exp(-0.5)
GN_EPS = RWKV_HEAD * 1e-5
NSA_HEADS = 16
NSA_HEAD_DIM = 64
NSA_KV_GROUPS = 2
NSA_GROUP_HEADS = NSA_HEADS // NSA_KV_GROUPS
N_BRANCH = 3
CMP_BLOCK = 32
CMP_STRIDE = 16
CMP_HIDDEN = 128
SLC_BLOCK = 64
SLC_TOPN = 16
WINDOW = 512
Q_BLOCK = 128
FORCED_SCORE = 1e4
ROPE_THETA = 500000.0
ROT_DIM = NSA_HEAD_DIM // 4
N_EXPERTS = 8
TOP_K = 2
NORM_EPS = 1e-6


def _rmsnorm_kernel(x_ref, g_ref, o_ref):
    x = x_ref[...]
    y = x * lax.rsqrt(jnp.mean(x * x, axis=-1, keepdims=True) + NORM_EPS)
    o_ref[...] = y * g_ref[...]


def rmsnorm(x, g):
    shp = x.shape
    x2 = x.reshape(-1, shp[-1])
    rows = x2.shape[0]
    tm = min(rows, 512)
    out = pl.pallas_call(
        _rmsnorm_kernel,
        grid=(rows // tm,),
        in_specs=[pl.BlockSpec((tm, shp[-1]), lambda i: (i, 0)),
                  pl.BlockSpec((1, shp[-1]), lambda i: (0, 0))],
        out_specs=pl.BlockSpec((tm, shp[-1]), lambda i: (i, 0)),
        out_shape=jax.ShapeDtypeStruct(x2.shape, x.dtype),
        name="rmsnorm",
    )(x2, g.reshape(1, -1))
    return out.reshape(shp)


def rope_partial(x, pos):
    half = ROT_DIM // 2
    inv = ROPE_THETA ** (-2.0 * jnp.arange(half, dtype=jnp.float32) / ROT_DIM)
    ang = pos.astype(jnp.float32)[:, None] * inv[None, :]
    shape = (1, pos.shape[0]) + (1,) * (x.ndim - 3) + (half,)
    cos = jnp.cos(ang).reshape(shape)
    sin = jnp.sin(ang).reshape(shape)
    xf = x.astype(jnp.float32)
    x1, x2 = xf[..., :half], xf[..., half:ROT_DIM]
    out = jnp.concatenate([x1 * cos - x2 * sin, x2 * cos + x1 * sin, xf[..., ROT_DIM:]], axis=-1)
    return out.astype(x.dtype)


def masked_softmax(s, mask):
    s = jnp.where(mask, s.astype(jnp.float32), -jnp.inf)
    m = jnp.max(s, axis=-1, keepdims=True)
    m = jnp.where(jnp.isfinite(m), m, 0.0)
    e = jnp.where(mask, jnp.exp(s - m), 0.0)
    return e / jnp.maximum(jnp.sum(e, axis=-1, keepdims=True), 1e-30)


FF_CHUNK = 1408
SWIGLU_ROWS = 512
MOE_PARTS = 8


def _swiglu_kernel(blk_e_ref, n_used_ref, x_ref, wg_ref, wu_ref, wd_ref, *rest, has_res):
    o_ref = rest[-1]
    i = pl.program_id(0)
    f = pl.program_id(1)

    @pl.when(i < n_used_ref[0])
    def _():
        x = x_ref[...].astype(jnp.bfloat16)
        g = jnp.dot(x, wg_ref[0], preferred_element_type=jnp.float32)
        u = jnp.dot(x, wu_ref[0], preferred_element_type=jnp.float32)
        act = (g * jax.nn.sigmoid(g) * u).astype(jnp.bfloat16)
        y = jnp.dot(act, wd_ref[0], preferred_element_type=jnp.float32)

        @pl.when(f == 0)
        def _():
            o_ref[...] = y + rest[0][...] if has_res else y

        @pl.when(f > 0)
        def _():
            o_ref[...] += y

    @pl.when(i >= n_used_ref[0])
    def _():
        o_ref[...] = jnp.zeros(o_ref.shape, o_ref.dtype)


def grouped_swiglu(xb, blk_e, n_used, w_gu, w_down, res=None, into=None):
    rows, d = xb.shape
    b = min(SWIGLU_ROWS, rows)
    n_blk = rows // b
    ff = w_down.shape[1]
    tf = FF_CHUNK
    n_f = ff // tf
    chunk = lambda i, f, be, nu: jnp.where(i < nu[0], f, n_f - 1)
    in_specs = [
        pl.BlockSpec((b, d), lambda i, f, be, nu: (i, 0)),
        pl.BlockSpec((1, d, tf), lambda i, f, be, nu: (be[i], 0, chunk(i, f, be, nu))),
        pl.BlockSpec((1, d, tf), lambda i, f, be, nu: (be[i], 0, n_f + chunk(i, f, be, nu))),
        pl.BlockSpec((1, tf, d), lambda i, f, be, nu: (be[i], chunk(i, f, be, nu), 0)),
    ]
    operands = [xb, w_gu, w_gu, w_down]
    if res is not None:
        in_specs.append(pl.BlockSpec((b, d), lambda i, f, be, nu: (i, 0)))
        operands.append(res)
    out_rows, first_blk, aliases = rows, 0, {}
    if into is not None:
        prev, first_blk, out_rows = into
        if prev is not None:
            in_specs.append(pl.BlockSpec(memory_space=pl.ANY))
            aliases = {2 + len(operands): 0}
            operands.append(prev)
    return pl.pallas_call(
        functools.partial(_swiglu_kernel, has_res=res is not None),
        grid_spec=pltpu.PrefetchScalarGridSpec(
            num_scalar_prefetch=2,
            grid=(n_blk, n_f),
            in_specs=in_specs,
            out_specs=pl.BlockSpec((b, d), lambda i, f, be, nu: (i + first_blk, 0)),
        ),
        out_shape=jax.ShapeDtypeStruct((out_rows, d), jnp.float32),
        input_output_aliases=aliases,
        compiler_params=pltpu.CompilerParams(dimension_semantics=("arbitrary", "arbitrary"),
                                             vmem_limit_bytes=56 * 1024 * 1024),
        name="grouped_swiglu",
    )(blk_e.astype(jnp.int32), jnp.reshape(n_used, (1,)).astype(jnp.int32), *operands)


def swiglu(h, w_gu, w_down, precise=False):
    g, u = jnp.split(dense(h, w_gu, precise), 2, axis=-1)
    return dense(jax.nn.silu(g) * u, w_down, precise)


def swiglu_residual(x, h, w_gu, w_down):
    n_blk = h.shape[0] // min(SWIGLU_ROWS, h.shape[0])
    return grouped_swiglu(h, jnp.zeros((n_blk,), jnp.int32), jnp.int32(n_blk),
                          w_gu.astype(jnp.bfloat16)[None], w_down.astype(jnp.bfloat16)[None], res=x)


def _combine_norm_kernel(x_ref, y0_ref, y1_ref, g_ref, gain_ref, o_ref):
    g = g_ref[...]
    x = x_ref[...] + (y0_ref[...] * g[:, 0:1] + y1_ref[...] * g[:, 1:2])
    o_ref[...] = _rms(x) * gain_ref[...]


def combine_norm(x, y0, y1, gate, gain):
    m, d = x.shape
    tm = min(ROW_TILE, m)
    rows = lambda width: pl.BlockSpec((tm, width), lambda i: (i, 0))
    return pl.pallas_call(
        _combine_norm_kernel,
        grid=(m // tm,),
        in_specs=[rows(d), rows(d), rows(d), rows(gate.shape[1]), pl.BlockSpec((1, d), lambda i: (0, 0))],
        out_specs=rows(d),
        out_shape=jax.ShapeDtypeStruct((m, d), jnp.float32),
        compiler_params=pltpu.CompilerParams(dimension_semantics=("parallel",)),
        name="combine_norm",
    )(x, y0, y1, gate, gain.reshape(1, d))


def moe_swiglu(xt, logits, w_gu_e, w_down_e):
    n_tok, d = xt.shape
    b = SWIGLU_ROWS
    experts = jnp.arange(N_EXPERTS)[None, :]
    e0 = jnp.argmax(logits, axis=-1)
    v0 = jnp.max(logits, axis=-1)
    rest = jnp.where(experts == e0[:, None], -jnp.inf, logits)
    e1 = jnp.argmax(rest, axis=-1)
    v1 = jnp.max(rest, axis=-1)
    top_idx = jnp.stack([e0, e1], axis=-1).astype(jnp.int32)
    gate = jax.nn.softmax(jnp.stack([v0, v1], axis=-1), axis=-1)
    nk = n_tok * TOP_K
    flat_e = top_idx.reshape(nk)
    onehot = (flat_e[:, None] == jnp.arange(N_EXPERTS)[None, :]).astype(jnp.int32)
    before = jnp.cumsum(onehot, axis=0) - onehot
    counts = jnp.sum(onehot, axis=0)
    padded = (counts + b - 1) // b * b
    ends_pad = jnp.cumsum(padded)
    starts_pad = ends_pad - padded
    dest = jnp.sum(onehot * (starts_pad[None, :] + before), axis=1)
    part_rows = MOE_PARTS * b
    n_rows = -(-((nk + b - 1) // b * b + N_EXPERTS * b) // part_rows) * part_rows
    n_blk = n_rows // b
    flat_tok = jnp.repeat(jnp.arange(n_tok, dtype=jnp.int32), TOP_K)
    row_tok = jnp.full((n_rows,), n_tok, jnp.int32).at[dest].set(flat_tok)
    blk_e = jnp.minimum(jnp.searchsorted(ends_pad, jnp.arange(n_blk) * b, side='right'), N_EXPERTS - 1)
    x_pad = jnp.concatenate([xt.astype(jnp.float32), jnp.zeros((1, d), jnp.float32)], axis=0)
    w_gu_b, w_down_b = w_gu_e.astype(jnp.bfloat16), w_down_e.astype(jnp.bfloat16)
    n_used = ends_pad[-1] // b
    per_part = n_blk // MOE_PARTS
    yb = None
    for i in range(MOE_PARTS):
        rows = x_pad[row_tok[i * per_part * b:(i + 1) * per_part * b]]
        yb = grouped_swiglu(rows, blk_e[i * per_part:(i + 1) * per_part],
                            jnp.clip(n_used - i * per_part, 0, per_part), w_gu_b, w_down_b,
                            into=(yb, i * per_part, n_rows))
    dest = dest.reshape(n_tok, TOP_K)
    return yb[dest[:, 0]], yb[dest[:, 1]], gate


WKV_CHUNK = 64
WKV_HEADS_PER_STEP = 16
WKV_SEQS_PER_STEP = 2

_NN = (((1,), (0,)), ((), ()))
_NT = (((1,), (1,)), ((), ()))
_TN = (((0,), (0,)), ((), ()))


def _mm(x, y, dims):
    return lax.dot_general(x.astype(jnp.bfloat16), y.astype(jnp.bfloat16), dims,
                           preferred_element_type=jnp.float32)


def _wkv7_chunk_kernel(r_ref, lw_ref, k_ref, v_ref, kk_ref, a_ref, rk_ref, gnw_ref, gnb_ref, s0_ref,
                       y_ref, sout_ref, state_ref):
    c = pl.program_id(2)
    n_seq, L = r_ref.shape[:2]
    n_pairs = state_ref.shape[0]
    per_seq = n_pairs // n_seq
    N = RWKV_HEAD
    W = 2 * N
    lanes = lambda ref: jnp.concatenate([ref[i] for i in range(n_seq)], axis=-1)
    tiled = lambda ref: jnp.concatenate([ref[...]] * n_seq, axis=-1)

    def block_diag(top, bottom):
        z = jnp.zeros((N, N), jnp.float32)
        return jnp.concatenate([jnp.concatenate([top, z], axis=1), jnp.concatenate([z, bottom], axis=1)], axis=0)

    @pl.when(c == 0)
    def _():
        for p in range(n_pairs):
            b, q = p // per_seq, p % per_seq
            state_ref[p] = block_diag(s0_ref[b, 2 * q], s0_ref[b, 2 * q + 1])

    row2 = lax.broadcasted_iota(jnp.int32, (2 * L, W), 0)
    lane2 = lax.broadcasted_iota(jnp.int32, (2 * L, W), 1)
    own_lanes = (row2 // L) == (lane2 // N)
    rr = lax.broadcasted_iota(jnp.int32, (2 * L, 2 * L), 0)
    cc = lax.broadcasted_iota(jnp.int32, (2 * L, 2 * L), 1)
    same = (rr // L) == (cc // L)
    strict = same & (rr % L > cc % L)
    incl = same & (rr % L >= cc % L)
    wr = lax.broadcasted_iota(jnp.int32, (W, W), 0)
    wc = lax.broadcasted_iota(jnp.int32, (W, W), 1)
    eye_w = wr == wc
    ones_bd = ((wr // N) == (wc // N)).astype(jnp.bfloat16)
    tl = lax.broadcasted_iota(jnp.int32, (L, L), 0) >= lax.broadcasted_iota(jnp.int32, (L, L), 1)
    tri = tl.astype(jnp.bfloat16)

    def head_sum(x):
        hi, lo = _split_bf16(x)
        return (jnp.dot(hi, ones_bd, preferred_element_type=jnp.float32)
                + jnp.dot(lo, ones_bd, preferred_element_type=jnp.float32))

    def stack(x):
        return jnp.where(own_lanes, jnp.concatenate([x, x], axis=0), 0.0)

    unstack = lambda x: x[:L] + x[L:]

    lw = lanes(lw_ref)
    lw_hi, lw_lo = _split_bf16(lw)
    cum = (lax.dot_general(tri, lw_hi, _NN, preferred_element_type=jnp.float32)
           + lax.dot_general(tri, lw_lo, _NN, preferred_element_type=jnp.float32))
    cum_last = cum[L - 1:L, :]
    e_neg_all = jnp.exp(-cum)
    e_tail_all = jnp.exp(cum_last - cum)
    e_prev_all = jnp.exp(cum - lw)
    e_cum_all = jnp.exp(cum)
    wl_all = jnp.exp(cum_last)

    pairs = range(n_pairs)
    slab = lambda t, p: t[:, p * W:(p + 1) * W]
    r_all, k_all, v_all, kk_all, a_all = (lanes(ref) for ref in (r_ref, k_ref, v_ref, kk_ref, a_ref))
    r_in = [slab(r_all, p) for p in pairs]
    k_in = [slab(k_all, p) for p in pairs]
    v_in = [slab(v_all, p) for p in pairs]
    ssq = [head_sum(slab(kk_all, p) * slab(kk_all, p)) for p in pairs]
    at_st, bt_st, bh_st, rt_st, kt_st, kh_st, v_st = [], [], [], [], [], [], []
    for p in pairs:
        kk = slab(kk_all, p) / jnp.maximum(jnp.sqrt(ssq[p]), 1e-12)
        b = kk * slab(a_all, p)
        at_st.append(stack(-kk * slab(e_prev_all, p)))
        bt_st.append(stack(b * slab(e_neg_all, p)))
        bh_st.append(stack(b * slab(e_tail_all, p)))
        rt_st.append(stack(r_in[p] * slab(e_cum_all, p)))
        kt_st.append(stack(k_in[p] * slab(e_neg_all, p)))
        kh_st.append(stack(k_in[p] * slab(e_tail_all, p)))
        v_st.append(stack(v_in[p]))
    a_ab = [jnp.where(strict, _mm(at_st[p], bt_st[p], _NT), 0.0) for p in pairs]
    a_ak = [jnp.where(strict, _mm(at_st[p], kt_st[p], _NT), 0.0) for p in pairs]
    r_b = [jnp.where(incl, _mm(rt_st[p], bt_st[p], _NT), 0.0) for p in pairs]
    r_k = [jnp.where(incl, _mm(rt_st[p], kt_st[p], _NT), 0.0) for p in pairs]
    av = [_mm(a_ak[p], v_st[p], _NN) for p in pairs]
    ht = [_mm(v_st[p], kh_st[p], _TN) for p in pairs]
    yp = [_mm(r_k[p], v_st[p], _NN) for p in pairs]
    pw = a_ab
    inv_a = a_ab
    n = 1
    while 2 * n < L:
        pw = [_mm(pw[p], pw[p], _NN) for p in pairs]
        inv_a = [inv_a[p] + pw[p] + _mm(inv_a[p], pw[p], _NN) for p in pairs]
        n *= 2
    ap = [at_st[p] + _mm(inv_a[p], at_st[p], _NN) for p in pairs]
    vp = [av[p] + _mm(inv_a[p], av[p], _NN) for p in pairs]
    g = [jnp.where(eye_w, slab(wl_all, p), 0.0) + _mm(bh_st[p], ap[p], _TN) for p in pairs]
    ht = [ht[p] + _mm(vp[p], bh_st[p], _TN) for p in pairs]
    rp = [unstack(rt_st[p] + _mm(r_b[p], ap[p], _NN)) for p in pairs]
    yp = [unstack(yp[p] + _mm(r_b[p], vp[p], _NN)) for p in pairs]
    s_prev = [state_ref[p] for p in pairs]
    y = [_mm(rp[p], s_prev[p], _NT) + yp[p] for p in pairs]
    for p in pairs:
        state_ref[p] = _mm(s_prev[p], g[p], _NT) + ht[p]
    mean = [head_sum(y[p]) * (1.0 / N) for p in pairs]
    cen = [y[p] - mean[p] for p in pairs]
    var = [head_sum(cen[p] * cen[p]) * (1.0 / N) for p in pairs]
    rk, gnw, gnb = tiled(rk_ref), tiled(gnw_ref), tiled(gnb_ref)
    bonus = [head_sum(r_in[p] * k_in[p] * slab(rk, p)) * v_in[p] for p in pairs]
    out = [cen[p] * lax.rsqrt(var[p] + GN_EPS) * slab(gnw, p) + slab(gnb, p) + bonus[p] for p in pairs]
    for b in range(n_seq):
        y_ref[b] = jnp.concatenate(out[b * per_seq:(b + 1) * per_seq], axis=-1)

    @pl.when(c == pl.num_programs(2) - 1)
    def _():
        for p in pairs:
            b, q = p // per_seq, p % per_seq
            s = state_ref[p]
            sout_ref[b, 2 * q] = s[:N, :N]
            sout_ref[b, 2 * q + 1] = s[N:, N:]


def wkv7_chunked(r, lw, k, v, kk, a_gate, r_k, gn_w, gn_b, s0):
    n, T, D = r.shape
    L = WKV_CHUNK
    hb = WKV_HEADS_PER_STEP
    w = hb * RWKV_HEAD
    nb = WKV_SEQS_PER_STEP if n % WKV_SEQS_PER_STEP == 0 else 1
    seq = pl.BlockSpec((nb, L, w), lambda b, h, c: (b, c, h))
    vec = pl.BlockSpec((1, w), lambda b, h, c: (0, h))
    st = pl.BlockSpec((nb, hb, RWKV_HEAD, RWKV_HEAD), lambda b, h, c: (b, h, 0, 0))
    row = lambda t: t.reshape(1, D).astype(jnp.float32)
    return pl.pallas_call(
        _wkv7_chunk_kernel,
        grid=(n // nb, D // w, T // L),
        in_specs=[seq] * 6 + [vec] * 3 + [st],
        out_specs=[seq, st],
        out_shape=[jax.ShapeDtypeStruct((n, T, D), jnp.float32),
                   jax.ShapeDtypeStruct(s0.shape, jnp.float32)],
        scratch_shapes=[pltpu.VMEM((nb * hb // 2, 2 * RWKV_HEAD, 2 * RWKV_HEAD), jnp.float32)],
        compiler_params=pltpu.CompilerParams(dimension_semantics=("parallel", "parallel", "arbitrary")),
        name="wkv7_chunked",
    )(r, lw, k, v, kk, a_gate, row(r_k), row(gn_w), row(gn_b), s0)


def wkv7_scan(r, lw, k, v, a_vec, b_vec, s0):
    def step(S, inp):
        r_t, lw_t, k_t, v_t, a_t, b_t = inp
        sa = jnp.sum(S * a_t[:, :, None, :], axis=-1)
        S = S * jnp.exp(lw_t)[:, :, None, :] + sa[..., None] * b_t[:, :, None, :] + v_t[..., None] * k_t[:, :, None, :]
        return S, jnp.sum(S * r_t[:, :, None, :], axis=-1)

    xs = tuple(jnp.moveaxis(t, 1, 0) for t in (r, lw, k, v, a_vec, b_vec))
    s_fin, ys = lax.scan(step, s0, xs)
    return jnp.moveaxis(ys, 0, 1), s_fin


ROW_TILE = 512


def _bdot(x, w):
    return jnp.dot(x.astype(jnp.bfloat16), w, preferred_element_type=jnp.float32)


def _rms(x):
    return x * lax.rsqrt(jnp.mean(x * x, axis=-1, keepdims=True) + NORM_EPS)


def _rwkv_pre_kernel(x_ref, shift_ref, gain_ref, mu_ref, vec_ref, wrkv_ref, w1_ref, a1_ref, g1_ref,
                     w2_ref, a2_ref, g2_ref,
                     r_ref, lw_ref, k_ref, v_ref, kk_ref, a_ref, g_ref, hlast_ref, prev_ref):
    @pl.when(pl.program_id(1) == 0)
    def _():
        prev_ref[...] = shift_ref[0]

    tm = x_ref.shape[1]
    h = _rms(x_ref[0]) * gain_ref[...]
    first_row = lax.broadcasted_iota(jnp.int32, h.shape, 0) == 0
    h_prev = jnp.where(first_row, prev_ref[...], pltpu.roll(h, 1, 0))
    prev_ref[...] = h[tm - 1:tm, :]
    hlast_ref[0] = h[tm - 1:tm, :]
    dx = h_prev - h
    mix = lambda i: (h + dx * mu_ref[i:i + 1, :]).astype(jnp.bfloat16)
    w0, a0, k_k, k_a = (vec_ref[i:i + 1, :] for i in range(4))
    r_ref[0] = _bdot(mix(0), wrkv_ref[0])
    k = _bdot(mix(2), wrkv_ref[1])
    v_ref[0] = _bdot(mix(3), wrkv_ref[2])
    lw_ref[0] = -DECAY_SCALE * jax.nn.sigmoid(w0 + _bdot(jnp.tanh(_bdot(mix(1), w1_ref[...])), w2_ref[...]))
    a = jax.nn.sigmoid(a0 + _bdot(_bdot(mix(4), a1_ref[...]), a2_ref[...]))
    g_ref[0] = _bdot(jax.nn.sigmoid(_bdot(mix(5), g1_ref[...])), g2_ref[...])
    a_ref[0] = a
    kk_ref[0] = k * k_k
    k_ref[0] = k * (1.0 + (a - 1.0) * k_a)


def _out_proj_kernel(*refs, n_terms, gated, routed):
    terms = refs[:n_terms]
    rest = refs[n_terms:]
    if gated:
        gate_ref, rest = rest[0], rest[1:]
    x_ref, w_ref, gain_ref = rest[:3]
    rest = rest[3:]
    if routed:
        wr_ref, rest = rest[0], rest[1:]
    xo_ref, ho_ref = rest[:2]
    y = terms[0][...]
    for t in terms[1:]:
        y = y + t[...]
    if gated:
        y = y * gate_ref[...]
    xo = x_ref[...] + _bdot(y, w_ref[...])
    xo_ref[...] = xo
    h = _rms(xo) * gain_ref[...]
    ho_ref[...] = h.astype(ho_ref.dtype)
    if routed:
        h_hi, h_lo = _split_bf16(h)
        w_hi, w_lo = _split_bf16(wr_ref[...])
        dot = functools.partial(jnp.dot, preferred_element_type=jnp.float32)
        rest[2][...] = dot(h_hi, w_hi) + (dot(h_hi, w_lo) + dot(h_lo, w_hi))


def out_proj(terms, gate, x, w, gain, w_router=None):
    m, d = x.shape
    kdim = w.shape[0]
    tm = min(ROW_TILE, m)
    rows = lambda width: pl.BlockSpec((tm, width), lambda i: (i, 0))
    full = lambda a: pl.BlockSpec(a.shape, lambda i: (0, 0))
    ins = list(terms) + ([gate] if gate is not None else [])
    consts = [w.astype(jnp.bfloat16), gain.reshape(1, d)]
    out_specs = [rows(d), rows(d)]
    out_shape = [jax.ShapeDtypeStruct((m, d), jnp.float32), jax.ShapeDtypeStruct((m, d), jnp.bfloat16)]
    if w_router is not None:
        consts.append(jnp.pad(w_router, ((0, 0), (0, 128 - w_router.shape[1]))))
        out_specs.append(rows(128))
        out_shape.append(jax.ShapeDtypeStruct((m, 128), jnp.float32))
    return pl.pallas_call(
        functools.partial(_out_proj_kernel, n_terms=len(terms), gated=gate is not None,
                          routed=w_router is not None),
        grid=(m // tm,),
        in_specs=[rows(kdim)] * len(ins) + [rows(d)] + [full(a) for a in consts],
        out_specs=out_specs,
        out_shape=out_shape,
        compiler_params=pltpu.CompilerParams(dimension_semantics=("parallel",),
                                             vmem_limit_bytes=48 * 1024 * 1024),
        name="out_proj",
    )(*ins, x, *consts)


def rwkv7_layer(x, shift0, s0, gain, mu, w_rkv, w0, w1, w2, a0, a1, a2, g1, g2, k_k, k_a, r_k, gn_w, gn_b, w_o,
                gain_next):
    n, T, D = x.shape
    tm = min(ROW_TILE, T)
    bf = lambda t: t.astype(jnp.bfloat16)
    seq = pl.BlockSpec((1, tm, D), lambda b, t: (b, t, 0))
    full = lambda a: pl.BlockSpec(a.shape, lambda b, t: (0,) * a.ndim)
    per_seq = pl.BlockSpec((1, 1, D), lambda b, t: (b, 0, 0))
    vecs = jnp.stack([w0, a0, k_k, k_a]).astype(jnp.float32)
    weights = [bf(w_rkv), bf(w1), bf(a1), bf(g1), bf(w2), bf(a2), bf(g2)]
    small = [gain.reshape(1, D), mu, vecs]
    outs = pl.pallas_call(
        _rwkv_pre_kernel,
        grid=(n, T // tm),
        in_specs=[seq, per_seq] + [full(a) for a in small + weights],
        out_specs=[seq] * 7 + [per_seq],
        out_shape=[jax.ShapeDtypeStruct((n, T, D), jnp.float32)] * 7
                  + [jax.ShapeDtypeStruct((n, 1, D), jnp.float32)],
        scratch_shapes=[pltpu.VMEM((1, D), jnp.float32)],
        compiler_params=pltpu.CompilerParams(dimension_semantics=("parallel", "arbitrary"),
                                             vmem_limit_bytes=56 * 1024 * 1024),
        name="rwkv_pre",
    )(x, shift0.reshape(n, 1, D), *small, *weights)
    r, lw, k, v, kk, a_gate, g, h_last = outs
    y, s_fin = wkv7_chunked(r, lw, k, v, kk, a_gate, r_k.reshape(-1), gn_w, gn_b, s0.astype(jnp.float32))
    flat = lambda t: t.reshape(n * T, D)
    x1, h1 = out_proj([flat(y)], flat(g), flat(x), w_o, gain_next)
    return x1, h1, s_fin, h_last.reshape(n, D)


def rwkv7_time_mix(h, h_prev, s0, mu, w_rkv, w0, w1, w2, a0, a1, a2, g1, g2, k_k, k_a, r_k, gn_w, gn_b, w_o,
                   precise=False):
    n, T, D = h.shape
    f32 = jnp.float32
    mm = functools.partial(dense, precise=precise)
    dx = jnp.concatenate([h_prev[:, None, :].astype(h.dtype), h[:, :-1]], axis=1) - h
    xr, xw, xk, xv, xa, xg = (h + dx * mu[i] for i in range(6))
    r = mm(xr, w_rkv[0])
    k = mm(xk, w_rkv[1])
    v = mm(xv, w_rkv[2])
    log_decay = -DECAY_SCALE * jax.nn.sigmoid((w0 + mm(jnp.tanh(mm(xw, w1)), w2)).astype(f32))
    a = jax.nn.sigmoid((a0 + mm(mm(xa, a1), a2)).astype(f32))
    g = mm(jax.nn.sigmoid(mm(xg, g1)), g2)
    heads = lambda t: t.astype(f32).reshape(n, T, RWKV_HEADS, RWKV_HEAD)
    kk = heads(k * k_k)
    kk = kk / jnp.maximum(jnp.sqrt(jnp.sum(kk * kk, axis=-1, keepdims=True)), 1e-12)
    a_h = heads(a)
    k_h = heads(k.astype(f32) * (1.0 + (a - 1.0) * k_a.astype(f32)))
    r_h, v_h = heads(r), heads(v)
    y, s_fin = wkv7_scan(r_h, heads(log_decay), k_h, v_h, -kk, kk * a_h, s0.astype(f32))
    mean = jnp.mean(y, axis=-1, keepdims=True)
    var = jnp.mean(jnp.square(y - mean), axis=-1, keepdims=True)
    y = ((y - mean) * lax.rsqrt(var + GN_EPS)).reshape(n, T, D) * gn_w.astype(f32) + gn_b.astype(f32)
    bonus = jnp.sum(r_h * k_h * r_k.astype(f32), axis=-1, keepdims=True) * v_h
    y = (y + bonus.reshape(n, T, D)).astype(h.dtype)
    return mm(y * g, w_o), s_fin, h[:, -1]


def shared_kv_rows(x, norm_kv, w_kv, pos, precise=False):
    n, T, _ = x.shape
    kv = dense(rmsnorm(x, norm_kv), w_kv, precise).reshape(n, T, N_BRANCH, 2, NSA_KV_GROUPS, NSA_HEAD_DIM)
    cmp_kv = kv[:, :, 0]
    slc_kv = jnp.stack([rope_partial(kv[:, :, 1, 0], pos), kv[:, :, 1, 1]], axis=2)
    win_kv = jnp.stack([rope_partial(kv[:, :, 2, 0], pos), kv[:, :, 2, 1]], axis=2)
    return cmp_kv, slc_kv, win_kv


CHUNK_LANES = CMP_STRIDE * 2 * NSA_KV_GROUPS * NSA_HEAD_DIM
HIDDEN_LANES = 2 * NSA_KV_GROUPS * 2 * CMP_HIDDEN
PAGES_PER_STEP = 32
CHUNKS_PER_PAGE = PAGE_SIZE // CMP_STRIDE


def _chunk_weights(cmp_w1):
    w = cmp_w1.reshape(2, 2, CMP_STRIDE, NSA_HEAD_DIM, CMP_HIDDEN)
    w = w.transpose(2, 0, 3, 1, 4)
    eye = jnp.eye(2, dtype=w.dtype)
    big = jnp.einsum('ab,cf,jaehk->jacebfhk', eye, jnp.eye(NSA_KV_GROUPS, dtype=w.dtype), w)
    return big.reshape(CHUNK_LANES, HIDDEN_LANES).astype(jnp.bfloat16)


def _token_weights(cmp_w1):
    w = cmp_w1.reshape(2, 2, CMP_STRIDE, NSA_HEAD_DIM, CMP_HIDDEN)
    return w.transpose(2, 0, 3, 1, 4).reshape(CMP_STRIDE, 2, NSA_HEAD_DIM, 2 * CMP_HIDDEN).astype(jnp.bfloat16)


def _paged_chunk_kernel(pt_ref, *refs):
    pages, w_ref, o_ref, xt_ref = (refs[:PAGES_PER_STEP], refs[PAGES_PER_STEP], refs[PAGES_PER_STEP + 1],
                                   refs[PAGES_PER_STEP + 2])
    planes = [(kv, g) for kv in range(2) for g in range(NSA_KV_GROUPS)]
    tok = lax.broadcasted_iota(jnp.int32, (PAGE_SIZE, PAGE_SIZE), 0)
    dst = lax.broadcasted_iota(jnp.int32, (PAGE_SIZE, PAGE_SIZE), 1)
    regroup = (tok == (dst % CHUNKS_PER_PAGE) * CMP_STRIDE + dst // CHUNKS_PER_PAGE).astype(jnp.bfloat16)
    for p in range(PAGES_PER_STEP):
        for i, (kv, g) in enumerate(planes):
            xt_ref[p, i] = _bdot(pages[p][0, kv, g], regroup).T
    width = 2 * CMP_HIDDEN
    for i, (kv, g) in enumerate(planes):
        acc = None
        for j in range(CMP_STRIDE):
            rows = [xt_ref[p, i, j * CHUNKS_PER_PAGE:(j + 1) * CHUNKS_PER_PAGE, :] for p in range(PAGES_PER_STEP)]
            y = _bdot(jnp.concatenate(rows, axis=0), w_ref[j, kv])
            acc = y if acc is None else acc + y
        o_ref[0, :, i * width:(i + 1) * width] = acc


def paged_chunk_hidden(cache, page_table, w_token):
    n, n_pages = page_table.shape
    planes = cache.transpose(0, 2, 3, 4, 1)
    rows = PAGES_PER_STEP * CHUNKS_PER_PAGE

    def page_spec(k):
        return pl.BlockSpec((1,) + planes.shape[1:], lambda b, s, pt: (pt[b, s * PAGES_PER_STEP + k], 0, 0, 0, 0))

    return pl.pallas_call(
        _paged_chunk_kernel,
        grid_spec=pltpu.PrefetchScalarGridSpec(
            num_scalar_prefetch=1,
            grid=(n, n_pages // PAGES_PER_STEP),
            in_specs=[page_spec(k) for k in range(PAGES_PER_STEP)]
                     + [pl.BlockSpec(w_token.shape, lambda b, s, pt: (0, 0, 0, 0))],
            out_specs=pl.BlockSpec((1, rows, HIDDEN_LANES), lambda b, s, pt: (b, s, 0)),
            scratch_shapes=[pltpu.VMEM((PAGES_PER_STEP, 2 * NSA_KV_GROUPS, PAGE_SIZE, NSA_HEAD_DIM), jnp.float32)],
        ),
        out_shape=jax.ShapeDtypeStruct((n, n_pages * CHUNKS_PER_PAGE, HIDDEN_LANES), jnp.float32),
        compiler_params=pltpu.CompilerParams(dimension_semantics=("parallel", "arbitrary"),
                                             vmem_limit_bytes=48 * 1024 * 1024),
        name="paged_chunk_hidden",
    )(page_table.astype(jnp.int32), *([planes] * PAGES_PER_STEP), w_token)


def _matmul_kernel(x_ref, w_ref, o_ref):
    o_ref[...] = jnp.dot(x_ref[...].astype(jnp.bfloat16), w_ref[...],
                         preferred_element_type=jnp.float32).astype(o_ref.dtype)


def matmul(x, w, rows_per_step=512, out_dtype=jnp.float32):
    m, k = x.shape
    nn = w.shape[1]
    tm = min(rows_per_step, m)
    return pl.pallas_call(
        _matmul_kernel,
        grid=(m // tm,),
        in_specs=[pl.BlockSpec((tm, k), lambda i: (i, 0)), pl.BlockSpec((k, nn), lambda i: (0, 0))],
        out_specs=pl.BlockSpec((tm, nn), lambda i: (i, 0)),
        out_shape=jax.ShapeDtypeStruct((m, nn), out_dtype),
        compiler_params=pltpu.CompilerParams(dimension_semantics=("parallel",),
                                             vmem_limit_bytes=48 * 1024 * 1024),
        name="matmul",
    )(x, w.astype(jnp.bfloat16))


def _split_bf16(x):
    hi = x.astype(jnp.bfloat16)
    return hi, (x - hi.astype(jnp.float32)).astype(jnp.bfloat16)


def _matmul3_kernel(x_ref, w_ref, o_ref):
    x_hi, x_lo = _split_bf16(x_ref[...])
    w_hi, w_lo = _split_bf16(w_ref[...])
    dot = functools.partial(jnp.dot, preferred_element_type=jnp.float32)
    o_ref[...] = dot(x_hi, w_hi) + (dot(x_hi, w_lo) + dot(x_lo, w_hi))


def matmul_precise(x, w, cols_per_step=512):
    m, k = x.shape
    nn = w.shape[1]
    pad = (-nn) % 128
    if pad:
        w = jnp.pad(w, ((0, 0), (0, pad)))
    tn = math.gcd(cols_per_step, nn + pad)
    tm = min(m, 512)
    out = pl.pallas_call(
        _matmul3_kernel,
        grid=(m // tm, (nn + pad) // tn),
        in_specs=[pl.BlockSpec((tm, k), lambda i, j: (i, 0)), pl.BlockSpec((k, tn), lambda i, j: (0, j))],
        out_specs=pl.BlockSpec((tm, tn), lambda i, j: (i, j)),
        out_shape=jax.ShapeDtypeStruct((m, nn + pad), jnp.float32),
        compiler_params=pltpu.CompilerParams(dimension_semantics=("parallel", "parallel")),
        name="matmul_precise",
    )(x, w)
    return out[:, :nn] if pad else out


def dense(x, w, precise=False):
    if precise:
        return matmul_precise(x.reshape(-1, x.shape[-1]), w).reshape(x.shape[:-1] + (w.shape[1],))
    return x @ w


def compress_from_hidden(hidden, pos_emb, w1, b1, w2):
    n, C = hidden.shape[:2]
    rows = -(-C // 8) * 8
    if rows != C:
        hidden = jnp.pad(hidden, ((0, 0), (0, rows - C), (0, 0)))
    w1r = w1.reshape(2, CMP_BLOCK, NSA_HEAD_DIM, CMP_HIDDEN)
    bias = jnp.einsum('ajd,ajdh->ah', pos_emb, w1r) + b1
    out = pl.pallas_call(
        functools.partial(_compress_finish_kernel, n_blocks=C),
        grid=(n,),
        in_specs=[pl.BlockSpec((1, rows, HIDDEN_LANES), lambda b: (b, 0, 0)),
                  pl.BlockSpec(bias.shape, lambda b: (0, 0)), pl.BlockSpec(w2.shape, lambda b: (0, 0, 0))],
        out_specs=pl.BlockSpec((1, rows, 2 * SLAB), lambda b: (b, 0, 0)),
        out_shape=jax.ShapeDtypeStruct((n, rows, 2 * SLAB), jnp.float32),
        compiler_params=pltpu.CompilerParams(dimension_semantics=("parallel",)),
        name="compress_finish",
    )(hidden, bias, w2.astype(jnp.bfloat16))[:, :C]
    shape = (n, C, NSA_KV_GROUPS, NSA_HEAD_DIM)
    return out[:, :, :SLAB].reshape(shape), out[:, :, SLAB:].reshape(shape)


def _compress_finish_kernel(h_ref, bias_ref, w2_ref, o_ref, *, n_blocks):
    h = h_ref[0]
    rows = h.shape[0]
    keep = lax.broadcasted_iota(jnp.int32, (rows, CMP_HIDDEN), 0) < n_blocks - 1
    outs = []
    for i in range(2 * NSA_KV_GROUPS):
        kv = i // NSA_KV_GROUPS
        first = h[:, 2 * i * CMP_HIDDEN:(2 * i + 1) * CMP_HIDDEN]
        second = h[:, (2 * i + 1) * CMP_HIDDEN:(2 * i + 2) * CMP_HIDDEN]
        nxt = jnp.where(keep, pltpu.roll(second, rows - 1, 0), 0.0)
        act = jax.nn.gelu(first + nxt + bias_ref[kv:kv + 1, :])
        outs.append(_bdot(act, w2_ref[kv]))
    o_ref[0] = jnp.concatenate(outs, axis=-1)


def rows_chunk_hidden(kv_rows, w_chunk):
    n, T = kv_rows.shape[:2]
    chunks = kv_rows.reshape(n * T // CMP_STRIDE, CHUNK_LANES)
    return matmul(chunks, w_chunk, rows_per_step=256).reshape(n, T // CMP_STRIDE, HIDDEN_LANES)


def nsa_query(h, w_in, precise=False):
    n, T, _ = h.shape
    proj = dense(h, w_in, precise)
    q = proj[..., :NSA_HEADS * NSA_HEAD_DIM].reshape(n, T, NSA_KV_GROUPS, NSA_GROUP_HEADS, NSA_HEAD_DIM)
    gate = proj[..., NSA_HEADS * NSA_HEAD_DIM:].reshape(n, T, NSA_KV_GROUPS, NSA_GROUP_HEADS, N_BRANCH)
    return q, gate


def compressed_branch_and_selection(q, t_pos, kc, vc):
    scale = NSA_HEAD_DIM ** -0.5
    n_c = kc.shape[1]
    vis = (jnp.arange(n_c) * CMP_STRIDE + CMP_BLOCK - 1)[None, :] <= t_pos[:, None]
    p_cmp = masked_softmax(jnp.einsum('nqghd,ncgd->nghqc', q, kc) * scale, vis)
    o_cmp = jnp.einsum('nghqc,ncgd->nqghd', p_cmp.astype(vc.dtype), vc)
    ratio = SLC_BLOCK // CMP_STRIDE
    lead = CMP_BLOCK // CMP_STRIDE - 1
    n_s = n_c // ratio
    pg = jnp.pad(jnp.sum(p_cmp, axis=2), ((0, 0), (0, 0), (0, 0), (lead, 0)))
    p_slc = pg[..., 0:ratio * n_s:ratio]
    for o in range(1, ratio + lead):
        p_slc = p_slc + pg[..., o:o + ratio * n_s:ratio]
    jb = jnp.arange(n_s)[None, :]
    jt = (t_pos // SLC_BLOCK)[:, None]
    forced = (jb == 0) | (jb == jt) | (jb == jt - 1)
    score = jnp.where(jb > jt, -jnp.inf, jnp.where(forced, FORCED_SCORE, p_slc))
    before = (score[..., :, None] > score[..., None, :]) | (
        (score[..., :, None] == score[..., None, :]) & (jb[0][:, None] < jb[0][None, :]))
    rank = jnp.sum(before, axis=-2)
    slots = jnp.arange(min(SLC_TOPN, n_s))
    idx = jnp.sum(jnp.where(rank[..., None, :] == slots[:, None], jb[0], 0), axis=-1)
    return o_cmp, idx


def _dot3(x, y, dims):
    x_hi, x_lo = _split_bf16(x)
    y_hi, y_lo = _split_bf16(y)
    dot = lambda a, b: lax.dot_general(a, b, dims, preferred_element_type=jnp.float32)
    return dot(x_hi, y_hi) + (dot(x_hi, y_lo) + dot(x_lo, y_hi))


def _decode_kernel(half_ref, new_ref, phys_ref, q_ref, rows_ref, win_ref, *refs):
    b = pl.program_id(0)
    n_sel = len(refs) - 1
    pages, o_ref = refs[:n_sel], refs[n_sel]
    k_per_group = n_sel // NSA_KV_GROUPS
    hg, dh = NSA_GROUP_HEADS, NSA_HEAD_DIM
    tok_half = lax.broadcasted_iota(jnp.int32, (hg, PAGE_SIZE), 1) // SLC_BLOCK
    rows = rows_ref[0]

    def attend(qg, scores, values, k_new, v_new, new_ok):
        s_new = jnp.where(new_ok, jnp.sum(qg * k_new, axis=-1, keepdims=True), MASKED)
        m = s_new
        for s in scores:
            m = jnp.maximum(m, jnp.max(s, axis=-1, keepdims=True))
        e_new = jnp.exp(s_new - m)
        total, out = e_new, e_new * v_new
        for s, v in zip(scores, values):
            e = jnp.exp(s - m)
            total = total + jnp.sum(e, axis=-1, keepdims=True)
            out = out + _dot3(e, v, _NT)
        return out / jnp.maximum(total, 1e-30)

    outs = []
    for g in range(NSA_KV_GROUPS):
        qg = q_ref[0, g * hg:(g + 1) * hg, :]
        lanes = slice(g * dh, (g + 1) * dh)
        scores, values = [], []
        any_new = jnp.int32(0)
        for k in range(k_per_group):
            j = g * k_per_group + k
            plane = pages[j][0]
            is_new = new_ref[b, j]
            ok = (tok_half == half_ref[b, j]) & (is_new == 0)
            scores.append(jnp.where(ok, _dot3(qg, plane[0, 0], _NN), MASKED))
            values.append(plane[1, 0])
            any_new = jnp.maximum(any_new, is_new)
        o_slc = attend(qg, scores, values, rows[0:1, lanes], rows[1:2, lanes], any_new > 0)
        o_win = attend(qg, [_dot3(qg, win_ref[0, 0, g], _NN)], [win_ref[0, 1, g]],
                       rows[2:3, lanes], rows[3:4, lanes], True)
        outs.append(jnp.concatenate([o_slc, o_win], axis=-1))
    o_ref[0] = jnp.concatenate(outs, axis=0)


def decode_attention(q_rot, idx, page_table, cache_slc_kv, cache_win_kv, slc_new, win_new):
    n = q_rot.shape[0]
    sub = PAGE_SIZE // SLC_BLOCK
    n_past_blk = PAST_LEN // SLC_BLOCK
    assert cache_win_kv.shape[1] <= WINDOW and cache_win_kv.shape[1] <= PAST_LEN
    k_sel = idx.shape[-1]
    flat = idx.reshape(n, NSA_KV_GROUPS * k_sel)
    past = jnp.minimum(flat, n_past_blk - 1)
    hit = (past // sub)[:, :, None] == jnp.arange(page_table.shape[1])[None, None, :]
    phys = jnp.sum(jnp.where(hit, page_table[:, None, :], 0), axis=-1).astype(jnp.int32)
    half = (past % sub).astype(jnp.int32)
    is_new = (flat >= n_past_blk).astype(jnp.int32)
    pool = cache_slc_kv.transpose(0, 2, 3, 4, 1)
    window = cache_win_kv.transpose(0, 2, 3, 4, 1)
    rows = jnp.concatenate([slc_new.reshape(n, 2, SLAB), win_new.reshape(n, 2, SLAB)], axis=1)

    def page_spec(j):
        g = j // k_sel
        return pl.BlockSpec((1, 2, 1, NSA_HEAD_DIM, PAGE_SIZE), lambda b, hf, nw, ph: (ph[b, j], 0, g, 0, 0))

    out = pl.pallas_call(
        _decode_kernel,
        grid_spec=pltpu.PrefetchScalarGridSpec(
            num_scalar_prefetch=3,
            grid=(n,),
            in_specs=[pl.BlockSpec((1,) + q_rot.shape[1:], lambda b, hf, nw, ph: (b, 0, 0)),
                      pl.BlockSpec((1,) + rows.shape[1:], lambda b, hf, nw, ph: (b, 0, 0)),
                      pl.BlockSpec((1,) + window.shape[1:], lambda b, hf, nw, ph: (b, 0, 0, 0, 0))]
                     + [page_spec(j) for j in range(NSA_KV_GROUPS * k_sel)],
            out_specs=pl.BlockSpec((1, NSA_HEADS, 2 * NSA_HEAD_DIM), lambda b, hf, nw, ph: (b, 0, 0)),
        ),
        out_shape=jax.ShapeDtypeStruct((n, NSA_HEADS, 2 * NSA_HEAD_DIM), jnp.float32),
        compiler_params=pltpu.CompilerParams(dimension_semantics=("parallel",)),
        name="decode_attention",
    )(half, is_new, phys, q_rot, rows, window, *([pool] * (NSA_KV_GROUPS * k_sel)))
    return out[:, :, :NSA_HEAD_DIM], out[:, :, NSA_HEAD_DIM:]


NSA_SLABS = NSA_GROUP_HEADS
NSA_TILES = NSA_SLABS * NSA_KV_GROUPS
SELECTED_KEY_BLOCK = 512
DENSE_BATCH = 16
MASKED = -1e30
SLC_PER_CMP = SLC_BLOCK // CMP_STRIDE
CMP_LEAD = CMP_BLOCK // CMP_STRIDE - 1
SLAB = NSA_KV_GROUPS * NSA_HEAD_DIM
SEL_LANES = 64


def _group_tiles(q_ref, qs_ref):
    tq = q_ref.shape[1]
    lane = lax.broadcasted_iota(jnp.int32, (tq, SLAB), 1)
    for i in range(NSA_SLABS):
        qs = q_ref[0, :, i * 128:(i + 1) * 128]
        for g in range(NSA_KV_GROUPS):
            in_group = (lane >= g * NSA_HEAD_DIM) & (lane < (g + 1) * NSA_HEAD_DIM)
            qs_ref[2 * i + g] = jnp.where(in_group, qs, jnp.zeros_like(qs)).astype(jnp.bfloat16)


def _merge_groups(o_ref, tiles, gate_ref, branch):
    tq = tiles[0].shape[0]
    lane = lax.broadcasted_iota(jnp.int32, (tq, SLAB), 1)
    gates = gate_ref[0]
    col = lambda g, i: (g * NSA_GROUP_HEADS + i) * N_BRANCH + branch
    for i in range(NSA_SLABS):
        lo = tiles[2 * i] * gates[:, col(0, i):col(0, i) + 1]
        hi = tiles[2 * i + 1] * gates[:, col(1, i):col(1, i) + 1]
        o_ref[0, :, i * 128:(i + 1) * 128] = jnp.where(lane < NSA_HEAD_DIM, lo, hi)


def _nsa_cmp_kernel(q_ref, kc_ref, vc_ref, gate_ref, o_ref, sel_ref, qs_ref):
    qi = pl.program_id(1)
    tq = q_ref.shape[1]
    nc = kc_ref.shape[1]
    ns = nc // SLC_PER_CMP
    t0 = qi * tq
    _group_tiles(q_ref, qs_ref)
    kc = kc_ref[0]
    vc = vc_ref[0]
    t_row = t0 + lax.broadcasted_iota(jnp.int32, (tq, nc), 0)
    c_pos = lax.broadcasted_iota(jnp.int32, (tq, nc), 1) * CMP_STRIDE + (CMP_BLOCK - 1)
    vis = c_pos <= t_row
    tiles = range(NSA_TILES)
    s = [_mm(qs_ref[r], kc, _NT) for r in tiles]
    p = []
    for r in tiles:
        sr = jnp.where(vis, s[r], MASKED)
        m = jnp.max(sr, axis=-1, keepdims=True)
        e = jnp.where(vis, jnp.exp(sr - m), 0.0)
        p.append(e / jnp.maximum(jnp.sum(e, axis=-1, keepdims=True), 1e-30))
    _merge_groups(o_ref, [_mm(p[r], vc, _NN) for r in tiles], gate_ref, 0)

    jrow = lax.broadcasted_iota(jnp.int32, (ns, nc), 0)
    ccol = lax.broadcasted_iota(jnp.int32, (ns, nc), 1)
    pool = ((ccol >= SLC_PER_CMP * jrow - CMP_LEAD) & (ccol < SLC_PER_CMP * (jrow + 1))).astype(jnp.bfloat16)
    jb = lax.broadcasted_iota(jnp.int32, (ns, tq), 0)
    jt = (t0 + lax.broadcasted_iota(jnp.int32, (ns, tq), 1)) // SLC_BLOCK
    forced = (jb == 0) | (jb == jt) | (jb == jt - 1)
    sel_t = []
    for g in range(NSA_KV_GROUPS):
        pg = p[g]
        for i in range(1, NSA_SLABS):
            pg = pg + p[2 * i + g]
        pg_hi = pg.astype(jnp.bfloat16)
        pg_lo = (pg - pg_hi.astype(jnp.float32)).astype(jnp.bfloat16)
        p_slc = (lax.dot_general(pool, pg_hi, _NT, preferred_element_type=jnp.float32)
                 + lax.dot_general(pool, pg_lo, _NT, preferred_element_type=jnp.float32))
        score = jnp.where(jb > jt, -jnp.inf, jnp.where(forced, FORCED_SCORE, p_slc))
        rank = jnp.zeros((ns, tq), jnp.float32)
        for i in range(ns):
            row = score[i:i + 1, :]
            tie = jnp.where(jb > i, 1.0, 0.0)
            rank = rank + jnp.where(row > score, 1.0, 0.0) + jnp.where(row == score, tie, 0.0)
        sel_t.append(jnp.where(rank < min(SLC_TOPN, ns), 1.0, 0.0))
        if ns < SEL_LANES:
            sel_t.append(jnp.zeros((SEL_LANES - ns, tq), jnp.float32))
    sel_ref[0] = jnp.concatenate(sel_t, axis=0).T.astype(sel_ref.dtype)


def _nsa_dense_kernel(q_ref, k_ref, vt_ref, sel_ref, gate_ref, o_ref, qs_ref, m_ref, acc_ref, *, windowed, kblock):
    qi = pl.program_id(1)
    tq = q_ref.shape[1]
    t0 = qi * tq
    _group_tiles(q_ref, qs_ref)
    m_ref[...] = jnp.full(m_ref.shape, MASKED, jnp.float32)
    acc_ref[...] = jnp.zeros(acc_ref.shape, jnp.float32)
    k_row = lax.broadcasted_iota(jnp.int32, (kblock, tq), 0)
    t_lane = t0 + lax.broadcasted_iota(jnp.int32, (kblock, tq), 1)
    tiles = range(NSA_TILES)
    first = 0
    last = 0 if windowed else (t0 + tq - 1) // kblock

    def key_block(kb, carry):
        start = pl.multiple_of(jnp.maximum(t0 - WINDOW, 0) if windowed else kb * kblock, tq)
        kblk = k_ref[0, pl.ds(start, kblock), :]
        vt = vt_ref[0, :, pl.ds(start, kblock)]
        k_pos = start + k_row
        if windowed:
            allowed = [(k_pos <= t_lane) & (k_pos >= t_lane - WINDOW)] * NSA_KV_GROUPS
        else:
            sel = sel_ref[0]
            n_sel = NSA_KV_GROUPS * SEL_LANES
            e_key = lax.broadcasted_iota(jnp.int32, (kblock, n_sel), 0)
            e_lane = lax.broadcasted_iota(jnp.int32, (kblock, n_sel), 1)
            blk = kb * (kblock // SLC_BLOCK) + e_key // SLC_BLOCK
            allowed = []
            for g in range(NSA_KV_GROUPS):
                expand = (e_lane == blk + g * SEL_LANES).astype(jnp.bfloat16)
                picked = lax.dot_general(expand, sel, _NT, preferred_element_type=jnp.float32)
                allowed.append((picked > 0.5) & (k_pos <= t_lane))
        bias = [jnp.where(ok, 0.0, MASKED) for ok in allowed]
        d_row = lax.broadcasted_iota(jnp.int32, vt.shape, 0)
        vt_g = [jnp.where((d_row >= g * NSA_HEAD_DIM) & (d_row < (g + 1) * NSA_HEAD_DIM), vt, jnp.ones_like(vt))
                for g in range(NSA_KV_GROUPS)]
        s = [_mm(kblk, qs_ref[r], _NT) for r in tiles]
        for lo in range(0, NSA_TILES, DENSE_BATCH):
            batch = range(lo, lo + DENSE_BATCH)
            p, alpha = {}, {}
            for r in batch:
                sr = s[r] + bias[r % NSA_KV_GROUPS]
                m_prev = m_ref[r]
                m_new = jnp.maximum(m_prev, jnp.max(sr, axis=0, keepdims=True))
                alpha[r] = jnp.exp(m_prev - m_new)
                p[r] = jnp.exp(sr - m_new).astype(jnp.bfloat16)
                m_ref[r] = m_new
            pv = {r: jnp.dot(vt_g[r % NSA_KV_GROUPS], p[r], preferred_element_type=jnp.float32) for r in batch}
            for r in batch:
                acc_ref[r] = alpha[r] * acc_ref[r] + pv[r]
        return carry

    lax.fori_loop(first, last + 1, key_block, 0)
    out = []
    for r in tiles:
        acc = acc_ref[r]
        sum_row = (1 - r % NSA_KV_GROUPS) * NSA_HEAD_DIM
        out.append((acc / jnp.maximum(acc[sum_row:sum_row + 1, :], 1e-30)).T)
    _merge_groups(o_ref, out, gate_ref, 2 if windowed else 1)


def nsa_prompt_attention(q, q_rot, kc, vc, kv_att, gates):
    n, T, D = q.shape
    tq = Q_BLOCK
    qspec = pl.BlockSpec((1, tq, D), lambda b, i: (b, i, 0))
    whole = lambda a: pl.BlockSpec((1,) + a.shape[1:], lambda b, i: (b, 0, 0))
    lanes = lambda j: pl.BlockSpec((1, T, SLAB), lambda b, i: (b, 0, j))
    n_sel = NSA_KV_GROUPS * SEL_LANES
    sel_spec = pl.BlockSpec((1, tq, n_sel), lambda b, i: (b, i, 0))
    gate_spec = pl.BlockSpec((1, tq, gates.shape[2]), lambda b, i: (b, i, 0))
    params = pltpu.CompilerParams(dimension_semantics=("parallel", "arbitrary"),
                                  vmem_limit_bytes=48 * 1024 * 1024)
    qs_scratch = pltpu.VMEM((NSA_TILES, tq, SLAB), jnp.bfloat16)
    o_cmp, sel = pl.pallas_call(
        _nsa_cmp_kernel,
        grid=(n, T // tq),
        in_specs=[qspec, whole(kc), whole(vc), gate_spec],
        out_specs=[qspec, sel_spec],
        out_shape=[jax.ShapeDtypeStruct((n, T, D), jnp.float32),
                   jax.ShapeDtypeStruct((n, T, n_sel), jnp.bfloat16)],
        scratch_shapes=[qs_scratch],
        compiler_params=params,
        name="nsa_cmp_select",
    )(q, kc, vc, gates)
    stat = pltpu.VMEM((NSA_TILES, 1, tq), jnp.float32)
    acc = pltpu.VMEM((NSA_TILES, SLAB, tq), jnp.float32)
    v_t = jnp.swapaxes(jnp.concatenate([kv_att[:, :, SLAB:2 * SLAB], kv_att[:, :, 3 * SLAB:]], axis=-1), 1, 2)
    rows = lambda j: pl.BlockSpec((1, SLAB, T), lambda b, i: (b, j, 0))

    def dense(windowed, branch, name):
        kblock = min(WINDOW + tq, T) if windowed else min(SELECTED_KEY_BLOCK, T)
        return pl.pallas_call(
            functools.partial(_nsa_dense_kernel, windowed=windowed, kblock=kblock),
            grid=(n, T // tq),
            in_specs=[qspec, lanes(2 * branch), rows(branch), sel_spec, gate_spec],
            out_specs=qspec,
            out_shape=jax.ShapeDtypeStruct((n, T, D), jnp.float32),
            scratch_shapes=[qs_scratch, stat, acc],
            compiler_params=params,
            name=name,
        )(q_rot, kv_att, v_t, sel, gates)

    return o_cmp, dense(False, 0, "nsa_selected"), dense(True, 1, "nsa_window")


def _rope_tables(pos):
    half = ROT_DIM // 2
    inv = ROPE_THETA ** (-2.0 * jnp.arange(half, dtype=jnp.float32) / ROT_DIM)
    ang = pos.astype(jnp.float32)[:, None] * inv[None, :]
    rest = NSA_HEAD_DIM - ROT_DIM
    cos = jnp.concatenate([jnp.cos(ang), jnp.cos(ang), jnp.ones((pos.shape[0], rest), jnp.float32)], axis=1)
    sin = jnp.concatenate([-jnp.sin(ang), jnp.sin(ang), jnp.zeros((pos.shape[0], rest), jnp.float32)], axis=1)
    return jnp.tile(cos, (1, NSA_KV_GROUPS)), jnp.tile(sin, (1, NSA_KV_GROUPS))


def _nsa_pre_kernel(x_ref, gkv_ref, gmix_ref, cos_ref, sin_ref, wkv_ref, wq_ref, wg_ref,
                    cmp_ref, slc_ref, win_ref, kvb_ref, q_ref, qr_ref, gate_ref):
    xh = _rms(x_ref[0])
    cos, sin = cos_ref[...], sin_ref[...]
    low = lax.broadcasted_iota(jnp.int32, cos.shape, 1) % NSA_HEAD_DIM < ROT_DIM // 2

    def rope(t):
        swapped = jnp.where(low, pltpu.roll(t, SLAB - ROT_DIM // 2, 1), pltpu.roll(t, ROT_DIM // 2, 1))
        return t * cos + swapped * sin

    kv = _bdot(xh * gkv_ref[...], wkv_ref[...])
    part = lambda j: kv[:, j * SLAB:(j + 1) * SLAB]
    k_slc, k_win = rope(part(2)), rope(part(4))
    cmp_ref[0] = kv[:, :2 * SLAB]
    slc_ref[0] = jnp.concatenate([k_slc, part(3)], axis=-1)
    win_ref[0] = jnp.concatenate([k_win, part(5)], axis=-1)
    kvb_ref[0] = jnp.concatenate([k_slc, part(3), k_win, part(5)], axis=-1).astype(jnp.bfloat16)
    h = xh * gmix_ref[...]
    q = _bdot(h, wq_ref[...])
    q_ref[0] = q.astype(jnp.bfloat16)
    qr_ref[0] = jnp.concatenate([rope(q[:, i * SLAB:(i + 1) * SLAB]) for i in range(NSA_SLABS)],
                                axis=-1).astype(jnp.bfloat16)
    gate_ref[0] = jax.nn.sigmoid(_bdot(h, wg_ref[...]))


def nsa_pre(x, pos, norm_kv, norm_mix, w_kv, w_in):
    n, T, D = x.shape
    tm = min(ROW_TILE, T)
    nq = NSA_HEADS * NSA_HEAD_DIM
    cos, sin = _rope_tables(pos)
    w_q = (_to_slabs(w_in[:, :nq]) * NSA_HEAD_DIM ** -0.5).astype(jnp.bfloat16)
    w_g = jnp.pad(w_in[:, nq:], ((0, 0), (0, SLAB - (w_in.shape[1] - nq)))).astype(jnp.bfloat16)
    seq = lambda width: pl.BlockSpec((1, tm, width), lambda b, t: (b, t, 0))
    full = lambda a: pl.BlockSpec(a.shape, lambda b, t: (0,) * a.ndim)
    table = pl.BlockSpec((tm, SLAB), lambda b, t: (t, 0))
    consts = [norm_kv.reshape(1, D), norm_mix.reshape(1, D)]
    weights = [w_kv.astype(jnp.bfloat16), w_q, w_g]
    widths = [2 * SLAB, 2 * SLAB, 2 * SLAB, 4 * SLAB, nq, nq, SLAB]
    dtypes = [jnp.float32] * 3 + [jnp.bfloat16] * 3 + [jnp.float32]
    return pl.pallas_call(
        _nsa_pre_kernel,
        grid=(n, T // tm),
        in_specs=[seq(D)] + [full(a) for a in consts] + [table, table] + [full(a) for a in weights],
        out_specs=[seq(w) for w in widths],
        out_shape=[jax.ShapeDtypeStruct((n, T, w), dt) for w, dt in zip(widths, dtypes)],
        compiler_params=pltpu.CompilerParams(dimension_semantics=("parallel", "parallel"),
                                             vmem_limit_bytes=48 * 1024 * 1024),
        name="nsa_pre",
    )(x, *consts, cos, sin, *weights)


def _to_slabs(x):
    lead = x.shape[:-1]
    x = x.reshape(lead + (NSA_KV_GROUPS, NSA_GROUP_HEADS, NSA_HEAD_DIM))
    return jnp.swapaxes(x, -3, -2).reshape(lead + (NSA_HEADS * NSA_HEAD_DIM,))


def nsa_layer_prompt(x, norm_kv, norm_mix, w_kv, cmp_pos, cmp_w1, cmp_b1, cmp_w2, w_in, w_o, norm_next, w_router):
    n, T, D = x.shape
    cmp_kv, slc_kv, win_kv, kv_att, q, q_rot, gates = nsa_pre(x, jnp.arange(T), norm_kv, norm_mix, w_kv, w_in)
    hidden = rows_chunk_hidden(cmp_kv, _chunk_weights(cmp_w1))
    kc, vc = compress_from_hidden(hidden, cmp_pos, cmp_w1, cmp_b1, cmp_w2)
    lanes = lambda a: a.reshape(n, a.shape[1], SLAB).astype(jnp.bfloat16)
    branches = nsa_prompt_attention(q, q_rot, lanes(kc), lanes(vc), kv_att, gates)
    flat = lambda a: a.reshape(n * T, a.shape[-1])
    x_out, h_out, logits = out_proj([flat(o) for o in branches], None, flat(x), _to_slabs(w_o.T).T, norm_next,
                                    w_router)
    return x_out, h_out, logits, cmp_kv, slc_kv, win_kv


def sample_kv_context(cmp_new, slc_new, win_new, cache_cmp_kv, cache_slc_kv, cache_win_kv, page_table,
                      cmp_pos, cmp_w1, cmp_b1, cmp_w2):
    n, S = cmp_new.shape[:2]
    n_new_blk = -(-S // SLC_BLOCK)
    pad = ((0, 0), (0, n_new_blk * SLC_BLOCK - S), (0, 0), (0, 0), (0, 0))
    w_chunk = _chunk_weights(cmp_w1)
    hidden = jnp.concatenate([paged_chunk_hidden(cache_cmp_kv, page_table, _token_weights(cmp_w1)),
                              rows_chunk_hidden(jnp.pad(cmp_new.astype(cache_cmp_kv.dtype), pad), w_chunk)], axis=1)
    kc, vc = compress_from_hidden(hidden, cmp_pos, cmp_w1, cmp_b1, cmp_w2)
    win_all = jnp.concatenate([cache_win_kv, win_new.astype(cache_win_kv.dtype)], axis=1)
    return kc, vc, win_all


def sample_nsa(h, kc, vc, slc_new, win_new, cache_slc_kv, cache_win_kv, page_table, w_in, w_o):
    n, S, _ = h.shape
    assert S == 1
    q, gate = nsa_query(h, w_in, precise=True)
    t_pos = PAST_LEN + jnp.arange(S)
    with jax.default_matmul_precision("highest"):
        o_cmp, idx = compressed_branch_and_selection(q, t_pos, kc, vc)
    q_rot = (rope_partial(q, t_pos) * NSA_HEAD_DIM ** -0.5).reshape(n, NSA_HEADS, NSA_HEAD_DIM)
    o_slc, o_win = decode_attention(q_rot, idx[:, :, 0], page_table, cache_slc_kv, cache_win_kv, slc_new, win_new)
    g = jax.nn.sigmoid(gate.astype(jnp.float32)).astype(q.dtype)
    o = g[..., 0:1] * o_cmp + g[..., 1:2] * o_slc.reshape(q.shape) + g[..., 2:3] * o_win.reshape(q.shape)
    return dense(o.reshape(n, S, NSA_HEADS * NSA_HEAD_DIM), w_o, precise=True)


def kernel(x_prompt, x_sample, state_wkv, state_shift, cache_cmp_kv, cache_slc_kv, cache_win_kv, page_table, norm_mix, norm_ffn, norm_kv, norm_final, rw_mu, rw_w_rkv, rw_w0, rw_w1, rw_w2, rw_a0, rw_a1, rw_a2, rw_g1, rw_g2, rw_k_k, rw_k_a, rw_r_k, rw_gn_w, rw_gn_b, rw_w_o, nsa_w_kv, nsa_cmp_pos, nsa_cmp_w1, nsa_cmp_b1, nsa_cmp_w2, nsa_w_in, nsa_w_o, ffn_w_gu, ffn_w_down, moe_router, moe_w_gu, moe_w_down):
    cmp_params = (nsa_cmp_pos, nsa_cmp_w1, nsa_cmp_b1, nsa_cmp_w2)
    assert DEPTH == 2 and N_A_LAYERS == 1
    D = D_MODEL
    moe_gu, moe_down = moe_w_gu[0].astype(jnp.bfloat16), moe_w_down[0].astype(jnp.bfloat16)
    x_prompt, moe_gu, moe_down = lax.optimization_barrier((x_prompt, moe_gu, moe_down))

    n_p, T = x_prompt.shape[:2]
    x1, h1, wkv_fin, h_last = rwkv7_layer(
        x_prompt, jnp.zeros((n_p, D), x_prompt.dtype), jnp.zeros((n_p, RWKV_HEADS, RWKV_HEAD, RWKV_HEAD), jnp.float32),
        norm_mix[0], rw_mu[0], rw_w_rkv[0], rw_w0[0], rw_w1[0], rw_w2[0], rw_a0[0], rw_a1[0], rw_a2[0],
        rw_g1[0], rw_g2[0], rw_k_k[0], rw_k_a[0], rw_r_k[0], rw_gn_w[0], rw_gn_b[0], rw_w_o[0], norm_ffn[0])
    wkv_p, shift_p = wkv_fin[None], h_last[None]
    x2 = swiglu_residual(x1, h1, ffn_w_gu[0], ffn_w_down[0])
    x_p, h_p, logits_p, cmp_rows, slc_rows, win_rows = nsa_layer_prompt(
        x2.reshape(n_p, T, D), norm_kv, norm_mix[1], nsa_w_kv, *cmp_params, nsa_w_in[0], nsa_w_o[0], norm_ffn[1],
        moe_router[0])
    kv_shape = (n_p, T, 2, NSA_KV_GROUPS, NSA_HEAD_DIM)
    cmp_kv_p, slc_kv_p = cmp_rows.reshape(kv_shape), slc_rows.reshape(kv_shape)
    win_kv_p = win_rows.reshape(kv_shape)[:, -min(WINDOW, T):]

    pos_s = PAST_LEN + jnp.arange(x_sample.shape[1], dtype=jnp.int32)
    h = rmsnorm(x_sample, norm_mix[0])
    y, s_fin, h_last_s = rwkv7_time_mix(
        h, state_shift[0], state_wkv[0], rw_mu[0], rw_w_rkv[0], rw_w0[0], rw_w1[0], rw_w2[0],
        rw_a0[0], rw_a1[0], rw_a2[0], rw_g1[0], rw_g2[0], rw_k_k[0], rw_k_a[0],
        rw_r_k[0], rw_gn_w[0], rw_gn_b[0], rw_w_o[0], precise=True)
    wkv_s, shift_s = s_fin[None], h_last_s[None]
    x_s = x_sample + y
    x_s = x_s + swiglu(rmsnorm(x_s, norm_ffn[0]), ffn_w_gu[0], ffn_w_down[0], precise=True)
    cmp_kv_s, slc_kv_s, win_new = shared_kv_rows(x_s, norm_kv, nsa_w_kv, pos_s, precise=True)
    kc_s, vc_s, win_all = sample_kv_context(cmp_kv_s, slc_kv_s, win_new, cache_cmp_kv, cache_slc_kv, cache_win_kv,
                                            page_table, *cmp_params)
    x_s = x_s + sample_nsa(rmsnorm(x_s, norm_mix[1]), kc_s, vc_s, slc_kv_s, win_new, cache_slc_kv, cache_win_kv,
                           page_table, nsa_w_in[0], nsa_w_o[0])
    win_kv_s = win_all[:, -cache_win_kv.shape[1]:]
    x_s = x_s.reshape(-1, D)
    h_s = rmsnorm(x_s, norm_ffn[1])
    logits_s = matmul_precise(h_s, moe_router[0])

    n_tok_p = n_p * T
    logits = jnp.concatenate([logits_p[:, :N_EXPERTS], logits_s], axis=0)
    y0, y1, gate = moe_swiglu(jnp.concatenate([h_p, h_s.astype(jnp.bfloat16)], axis=0), logits,
                              moe_gu, moe_down)
    gate = jnp.pad(gate, ((0, 0), (0, 128 - TOP_K)))
    y_prompt = combine_norm(x_p, y0[:n_tok_p], y1[:n_tok_p], gate[:n_tok_p], norm_final).reshape(x_prompt.shape)
    y_sample = combine_norm(x_s, y0[n_tok_p:], y1[n_tok_p:], gate[n_tok_p:], norm_final).reshape(x_sample.shape)


    return (y_prompt, y_sample, wkv_p, shift_p, cmp_kv_p, slc_kv_p, win_kv_p,
            wkv_s, shift_s, cmp_kv_s, slc_kv_s, win_kv_s)
```

```python
import functools
import math

import jax
import jax.numpy as jnp
from jax import lax
from jax.experimental import pallas as pl
from jax.experimental.pallas import tpu as pltpu

D_MODEL = 1024
DEPTH = 2
PAST_LEN = 16384
PAGE_SIZE = 128
N_A_LAYERS = DEPTH // 2
RWKV_HEAD = 64
RWKV_HEADS = D_MODEL // RWKV_HEAD
DECAY_SCALE = math.exp(-0.5)
GN_EPS = RWKV_HEAD * 1e-5
NSA_HEADS = 16
NSA_HEAD_DIM = 64
NSA_KV_GROUPS = 2
NSA_GROUP_HEADS = NSA_HEADS // NSA_KV_GROUPS
N_BRANCH = 3
CMP_BLOCK = 32
CMP_STRIDE = 16
CMP_HIDDEN = 128
SLC_BLOCK = 64
SLC_TOPN = 16
WINDOW = 512
Q_BLOCK = 128
FORCED_SCORE = 1e4
ROPE_THETA = 500000.0
ROT_DIM = NSA_HEAD_DIM // 4
N_EXPERTS = 8
TOP_K = 2
NORM_EPS = 1e-6


def _rmsnorm_kernel(x_ref, g_ref, o_ref):
    x = x_ref[...]
    y = x * lax.rsqrt(jnp.mean(x * x, axis=-1, keepdims=True) + NORM_EPS)
    o_ref[...] = y * g_ref[...]


def rmsnorm(x, g):
    shp = x.shape
    x2 = x.reshape(-1, shp[-1])
    rows = x2.shape[0]
    tm = min(rows, 512)
    out = pl.pallas_call(
        _rmsnorm_kernel,
        grid=(rows // tm,),
        in_specs=[pl.BlockSpec((tm, shp[-1]), lambda i: (i, 0)),
                  pl.BlockSpec((1, shp[-1]), lambda i: (0, 0))],
        out_specs=pl.BlockSpec((tm, shp[-1]), lambda i: (i, 0)),
        out_shape=jax.ShapeDtypeStruct(x2.shape, x.dtype),
        name="rmsnorm",
    )(x2, g.reshape(1, -1))
    return out.reshape(shp)


def rope_partial(x, pos):
    half = ROT_DIM // 2
    inv = ROPE_THETA ** (-2.0 * jnp.arange(half, dtype=jnp.float32) / ROT_DIM)
    ang = pos.astype(jnp.float32)[:, None] * inv[None, :]
    shape = (1, pos.shape[0]) + (1,) * (x.ndim - 3) + (half,)
    cos = jnp.cos(ang).reshape(shape)
    sin = jnp.sin(ang).reshape(shape)
    xf = x.astype(jnp.float32)
    x1, x2 = xf[..., :half], xf[..., half:ROT_DIM]
    out = jnp.concatenate([x1 * cos - x2 * sin, x2 * cos + x1 * sin, xf[..., ROT_DIM:]], axis=-1)
    return out.astype(x.dtype)


def masked_softmax(s, mask):
    s = jnp.where(mask, s.astype(jnp.float32), -jnp.inf)
    m = jnp.max(s, axis=-1, keepdims=True)
    m = jnp.where(jnp.isfinite(m), m, 0.0)
    e = jnp.where(mask, jnp.exp(s - m), 0.0)
    return e / jnp.maximum(jnp.sum(e, axis=-1, keepdims=True), 1e-30)


FF_CHUNK = 1408
SWIGLU_ROWS = 512
MOE_PARTS = 8


def _swiglu_kernel(blk_e_ref, n_used_ref, x_ref, wg_ref, wu_ref, wd_ref, *rest, has_res):
    o_ref = rest[-1]
    i = pl.program_id(0)
    f = pl.program_id(1)

    @pl.when(i < n_used_ref[0])
    def _():
        x = x_ref[...].astype(jnp.bfloat16)
        g = jnp.dot(x, wg_ref[0], preferred_element_type=jnp.float32)
        u = jnp.dot(x, wu_ref[0], preferred_element_type=jnp.float32)
        act = (g * jax.nn.sigmoid(g) * u).astype(jnp.bfloat16)
        y = jnp.dot(act, wd_ref[0], preferred_element_type=jnp.float32)

        @pl.when(f == 0)
        def _():
            o_ref[...] = y + rest[0][...] if has_res else y

        @pl.when(f > 0)
        def _():
            o_ref[...] += y

    @pl.when(i >= n_used_ref[0])
    def _():
        o_ref[...] = jnp.zeros(o_ref.shape, o_ref.dtype)


def grouped_swiglu(xb, blk_e, n_used, w_gu, w_down, res=None, into=None):
    rows, d = xb.shape
    b = min(SWIGLU_ROWS, rows)
    n_blk = rows // b
    ff = w_down.shape[1]
    tf = FF_CHUNK
    n_f = ff // tf
    chunk = lambda i, f, be, nu: jnp.where(i < nu[0], f, n_f - 1)
    in_specs = [
        pl.BlockSpec((b, d), lambda i, f, be, nu: (i, 0)),
        pl.BlockSpec((1, d, tf), lambda i, f, be, nu: (be[i], 0, chunk(i, f, be, nu))),
        pl.BlockSpec((1, d, tf), lambda i, f, be, nu: (be[i], 0, n_f + chunk(i, f, be, nu))),
        pl.BlockSpec((1, tf, d), lambda i, f, be, nu: (be[i], chunk(i, f, be, nu), 0)),
    ]
    operands = [xb, w_gu, w_gu, w_down]
    if res is not None:
        in_specs.append(pl.BlockSpec((b, d), lambda i, f, be, nu: (i, 0)))
        operands.append(res)
    out_rows, first_blk, aliases = rows, 0, {}
    if into is not None:
        prev, first_blk, out_rows = into
        if prev is not None:
            in_specs.append(pl.BlockSpec(memory_space=pl.ANY))
            aliases = {2 + len(operands): 0}
            operands.append(prev)
    return pl.pallas_call(
        functools.partial(_swiglu_kernel, has_res=res is not None),
        grid_spec=pltpu.PrefetchScalarGridSpec(
            num_scalar_prefetch=2,
            grid=(n_blk, n_f),
            in_specs=in_specs,
            out_specs=pl.BlockSpec((b, d), lambda i, f, be, nu: (i + first_blk, 0)),
        ),
        out_shape=jax.ShapeDtypeStruct((out_rows, d), jnp.float32),
        input_output_aliases=aliases,
        compiler_params=pltpu.CompilerParams(dimension_semantics=("arbitrary", "arbitrary"),
                                             vmem_limit_bytes=56 * 1024 * 1024),
        name="grouped_swiglu",
    )(blk_e.astype(jnp.int32), jnp.reshape(n_used, (1,)).astype(jnp.int32), *operands)


def swiglu(h, w_gu, w_down, precise=False):
    g, u = jnp.split(dense(h, w_gu, precise), 2, axis=-1)
    return dense(jax.nn.silu(g) * u, w_down, precise)


def swiglu_residual(x, h, w_gu, w_down):
    n_blk = h.shape[0] // min(SWIGLU_ROWS, h.shape[0])
    return grouped_swiglu(h, jnp.zeros((n_blk,), jnp.int32), jnp.int32(n_blk),
                          w_gu.astype(jnp.bfloat16)[None], w_down.astype(jnp.bfloat16)[None], res=x)


def _combine_norm_kernel(x_ref, y0_ref, y1_ref, g_ref, gain_ref, o_ref):
    g = g_ref[...]
    x = x_ref[...] + (y0_ref[...] * g[:, 0:1] + y1_ref[...] * g[:, 1:2])
    o_ref[...] = _rms(x) * gain_ref[...]


def combine_norm(x, y0, y1, gate, gain):
    m, d = x.shape
    tm = min(ROW_TILE, m)
    rows = lambda width: pl.BlockSpec((tm, width), lambda i: (i, 0))
    return pl.pallas_call(
        _combine_norm_kernel,
        grid=(m // tm,),
        in_specs=[rows(d), rows(d), rows(d), rows(gate.shape[1]), pl.BlockSpec((1, d), lambda i: (0, 0))],
        out_specs=rows(d),
        out_shape=jax.ShapeDtypeStruct((m, d), jnp.float32),
        compiler_params=pltpu.CompilerParams(dimension_semantics=("parallel",)),
        name="combine_norm",
    )(x, y0, y1, gate, gain.reshape(1, d))


def moe_swiglu(xt, logits, w_gu_e, w_down_e):
    n_tok, d = xt.shape
    b = SWIGLU_ROWS
    experts = jnp.arange(N_EXPERTS)[None, :]
    e0 = jnp.argmax(logits, axis=-1)
    v0 = jnp.max(logits, axis=-1)
    rest = jnp.where(experts == e0[:, None], -jnp.inf, logits)
    e1 = jnp.argmax(rest, axis=-1)
    v1 = jnp.max(rest, axis=-1)
    top_idx = jnp.stack([e0, e1], axis=-1).astype(jnp.int32)
    gate = jax.nn.softmax(jnp.stack([v0, v1], axis=-1), axis=-1)
    nk = n_tok * TOP_K
    flat_e = top_idx.reshape(nk)
    onehot = (flat_e[:, None] == jnp.arange(N_EXPERTS)[None, :]).astype(jnp.int32)
    before = jnp.cumsum(onehot, axis=0) - onehot
    counts = jnp.sum(onehot, axis=0)
    padded = (counts + b - 1) // b * b
    ends_pad = jnp.cumsum(padded)
    starts_pad = ends_pad - padded
    dest = jnp.sum(onehot * (starts_pad[None, :] + before), axis=1)
    part_rows = MOE_PARTS * b
    n_rows = -(-((nk + b - 1) // b * b + N_EXPERTS * b) // part_rows) * part_rows
    n_blk = n_rows // b
    flat_tok = jnp.repeat(jnp.arange(n_tok, dtype=jnp.int32), TOP_K)
    row_tok = jnp.full((n_rows,), n_tok, jnp.int32).at[dest].set(flat_tok)
    blk_e = jnp.minimum(jnp.searchsorted(ends_pad, jnp.arange(n_blk) * b, side='right'), N_EXPERTS - 1)
    x_pad = jnp.concatenate([xt.astype(jnp.float32), jnp.zeros((1, d), jnp.float32)], axis=0)
    w_gu_b, w_down_b = w_gu_e.astype(jnp.bfloat16), w_down_e.astype(jnp.bfloat16)
    n_used = ends_pad[-1] // b
    per_part = n_blk // MOE_PARTS
    yb = None
    for i in range(MOE_PARTS):
        rows = x_pad[row_tok[i * per_part * b:(i + 1) * per_part * b]]
        yb = grouped_swiglu(rows, blk_e[i * per_part:(i + 1) * per_part],
                            jnp.clip(n_used - i * per_part, 0, per_part), w_gu_b, w_down_b,
                            into=(yb, i * per_part, n_rows))
    dest = dest.reshape(n_tok, TOP_K)
    return yb[dest[:, 0]], yb[dest[:, 1]], gate


WKV_CHUNK = 64
WKV_HEADS_PER_STEP = 16
WKV_SEQS_PER_STEP = 2

_NN = (((1,), (0,)), ((), ()))
_NT = (((1,), (1,)), ((), ()))
_TN = (((0,), (0,)), ((), ()))


def _mm(x, y, dims):
    return lax.dot_general(x.astype(jnp.bfloat16), y.astype(jnp.bfloat16), dims,
                           preferred_element_type=jnp.float32)


def _wkv7_chunk_kernel(r_ref, lw_ref, k_ref, v_ref, kk_ref, a_ref, rk_ref, gnw_ref, gnb_ref, s0_ref,
                       y_ref, sout_ref, state_ref):
    c = pl.program_id(2)
    n_seq, L = r_ref.shape[:2]
    n_pairs = state_ref.shape[0]
    per_seq = n_pairs // n_seq
    N = RWKV_HEAD
    W = 2 * N
    lanes = lambda ref: jnp.concatenate([ref[i] for i in range(n_seq)], axis=-1)
    tiled = lambda ref: jnp.concatenate([ref[...]] * n_seq, axis=-1)

    def block_diag(top, bottom):
        z = jnp.zeros((N, N), jnp.float32)
        return jnp.concatenate([jnp.concatenate([top, z], axis=1), jnp.concatenate([z, bottom], axis=1)], axis=0)

    @pl.when(c == 0)
    def _():
        for p in range(n_pairs):
            b, q = p // per_seq, p % per_seq
            state_ref[p] = block_diag(s0_ref[b, 2 * q], s0_ref[b, 2 * q + 1])

    row2 = lax.broadcasted_iota(jnp.int32, (2 * L, W), 0)
    lane2 = lax.broadcasted_iota(jnp.int32, (2 * L, W), 1)
    own_lanes = (row2 // L) == (lane2 // N)
    rr = lax.broadcasted_iota(jnp.int32, (2 * L, 2 * L), 0)
    cc = lax.broadcasted_iota(jnp.int32, (2 * L, 2 * L), 1)
    same = (rr // L) == (cc // L)
    strict = same & (rr % L > cc % L)
    incl = same & (rr % L >= cc % L)
    wr = lax.broadcasted_iota(jnp.int32, (W, W), 0)
    wc = lax.broadcasted_iota(jnp.int32, (W, W), 1)
    eye_w = wr == wc
    ones_bd = ((wr // N) == (wc // N)).astype(jnp.bfloat16)
    tl = lax.broadcasted_iota(jnp.int32, (L, L), 0) >= lax.broadcasted_iota(jnp.int32, (L, L), 1)
    tri = tl.astype(jnp.bfloat16)

    def head_sum(x):
        hi, lo = _split_bf16(x)
        return (jnp.dot(hi, ones_bd, preferred_element_type=jnp.float32)
                + jnp.dot(lo, ones_bd, preferred_element_type=jnp.float32))

    def stack(x):
        return jnp.where(own_lanes, jnp.concatenate([x, x], axis=0), 0.0)

    unstack = lambda x: x[:L] + x[L:]

    lw = lanes(lw_ref)
    lw_hi, lw_lo = _split_bf16(lw)
    cum = (lax.dot_general(tri, lw_hi, _NN, preferred_element_type=jnp.float32)
           + lax.dot_general(tri, lw_lo, _NN, preferred_element_type=jnp.float32))
    cum_last = cum[L - 1:L, :]
    e_neg_all = jnp.exp(-cum)
    e_tail_all = jnp.exp(cum_last - cum)
    e_prev_all = jnp.exp(cum - lw)
    e_cum_all = jnp.exp(cum)
    wl_all = jnp.exp(cum_last)

    pairs = range(n_pairs)
    slab = lambda t, p: t[:, p * W:(p + 1) * W]
    r_all, k_all, v_all, kk_all, a_all = (lanes(ref) for ref in (r_ref, k_ref, v_ref, kk_ref, a_ref))
    r_in = [slab(r_all, p) for p in pairs]
    k_in = [slab(k_all, p) for p in pairs]
    v_in = [slab(v_all, p) for p in pairs]
    ssq = [head_sum(slab(kk_all, p) * slab(kk_all, p)) for p in pairs]
    at_st, bt_st, bh_st, rt_st, kt_st, kh_st, v_st = [], [], [], [], [], [], []
    for p in pairs:
        kk = slab(kk_all, p) / jnp.maximum(jnp.sqrt(ssq[p]), 1e-12)
        b = kk * slab(a_all, p)
        at_st.append(stack(-kk * slab(e_prev_all, p)))
        bt_st.append(stack(b * slab(e_neg_all, p)))
        bh_st.append(stack(b * slab(e_tail_all, p)))
        rt_st.append(stack(r_in[p] * slab(e_cum_all, p)))
        kt_st.append(stack(k_in[p] * slab(e_neg_all, p)))
        kh_st.append(stack(k_in[p] * slab(e_tail_all, p)))
        v_st.append(stack(v_in[p]))
    a_ab = [jnp.where(strict, _mm(at_st[p], bt_st[p], _NT), 0.0) for p in pairs]
    a_ak = [jnp.where(strict, _mm(at_st[p], kt_st[p], _NT), 0.0) for p in pairs]
    r_b = [jnp.where(incl, _mm(rt_st[p], bt_st[p], _NT), 0.0) for p in pairs]
    r_k = [jnp.where(incl, _mm(rt_st[p], kt_st[p], _NT), 0.0) for p in pairs]
    av = [_mm(a_ak[p], v_st[p], _NN) for p in pairs]
    ht = [_mm(v_st[p], kh_st[p], _TN) for p in pairs]
    yp = [_mm(r_k[p], v_st[p], _NN) for p in pairs]
    pw = a_ab
    inv_a = a_ab
    n = 1
    while 2 * n < L:
        pw = [_mm(pw[p], pw[p], _NN) for p in pairs]
        inv_a = [inv_a[p] + pw[p] + _mm(inv_a[p], pw[p], _NN) for p in pairs]
        n *= 2
    ap = [at_st[p] + _mm(inv_a[p], at_st[p], _NN) for p in pairs]
    vp = [av[p] + _mm(inv_a[p], av[p], _NN) for p in pairs]
    g = [jnp.where(eye_w, slab(wl_all, p), 0.0) + _mm(bh_st[p], ap[p], _TN) for p in pairs]
    ht = [ht[p] + _mm(vp[p], bh_st[p], _TN) for p in pairs]
    rp = [unstack(rt_st[p] + _mm(r_b[p], ap[p], _NN)) for p in pairs]
    yp = [unstack(yp[p] + _mm(r_b[p], vp[p], _NN)) for p in pairs]
    s_prev = [state_ref[p] for p in pairs]
    y = [_mm(rp[p], s_prev[p], _NT) + yp[p] for p in pairs]
    for p in pairs:
        state_ref[p] = _mm(s_prev[p], g[p], _NT) + ht[p]
    mean = [head_sum(y[p]) * (1.0 / N) for p in pairs]
    cen = [y[p] - mean[p] for p in pairs]
    var = [head_sum(cen[p] * cen[p]) * (1.0 / N) for p in pairs]
    rk, gnw, gnb = tiled(rk_ref), tiled(gnw_ref), tiled(gnb_ref)
    bonus = [head_sum(r_in[p] * k_in[p] * slab(rk, p)) * v_in[p] for p in pairs]
    out = [cen[p] * lax.rsqrt(var[p] + GN_EPS) * slab(gnw, p) + slab(gnb, p) + bonus[p] for p in pairs]
    for b in range(n_seq):
        y_ref[b] = jnp.concatenate(out[b * per_seq:(b + 1) * per_seq], axis=-1)

    @pl.when(c == pl.num_programs(2) - 1)
    def _():
        for p in pairs:
            b, q = p // per_seq, p % per_seq
            s = state_ref[p]
            sout_ref[b, 2 * q] = s[:N, :N]
            sout_ref[b, 2 * q + 1] = s[N:, N:]


def wkv7_chunked(r, lw, k, v, kk, a_gate, r_k, gn_w, gn_b, s0):
    n, T, D = r.shape
    L = WKV_CHUNK
    hb = WKV_HEADS_PER_STEP
    w = hb * RWKV_HEAD
    nb = WKV_SEQS_PER_STEP if n % WKV_SEQS_PER_STEP == 0 else 1
    seq = pl.BlockSpec((nb, L, w), lambda b, h, c: (b, c, h))
    vec = pl.BlockSpec((1, w), lambda b, h, c: (0, h))
    st = pl.BlockSpec((nb, hb, RWKV_HEAD, RWKV_HEAD), lambda b, h, c: (b, h, 0, 0))
    row = lambda t: t.reshape(1, D).astype(jnp.float32)
    return pl.pallas_call(
        _wkv7_chunk_kernel,
        grid=(n // nb, D // w, T // L),
        in_specs=[seq] * 6 + [vec] * 3 + [st],
        out_specs=[seq, st],
        out_shape=[jax.ShapeDtypeStruct((n, T, D), jnp.float32),
                   jax.ShapeDtypeStruct(s0.shape, jnp.float32)],
        scratch_shapes=[pltpu.VMEM((nb * hb // 2, 2 * RWKV_HEAD, 2 * RWKV_HEAD), jnp.float32)],
        compiler_params=pltpu.CompilerParams(dimension_semantics=("parallel", "parallel", "arbitrary")),
        name="wkv7_chunked",
    )(r, lw, k, v, kk, a_gate, row(r_k), row(gn_w), row(gn_b), s0)


def wkv7_scan(r, lw, k, v, a_vec, b_vec, s0):
    def step(S, inp):
        r_t, lw_t, k_t, v_t, a_t, b_t = inp
        sa = jnp.sum(S * a_t[:, :, None, :], axis=-1)
        S = S * jnp.exp(lw_t)[:, :, None, :] + sa[..., None] * b_t[:, :, None, :] + v_t[..., None] * k_t[:, :, None, :]
        return S, jnp.sum(S * r_t[:, :, None, :], axis=-1)

    xs = tuple(jnp.moveaxis(t, 1, 0) for t in (r, lw, k, v, a_vec, b_vec))
    s_fin, ys = lax.scan(step, s0, xs)
    return jnp.moveaxis(ys, 0, 1), s_fin


ROW_TILE = 512


def _bdot(x, w):
    return jnp.dot(x.astype(jnp.bfloat16), w, preferred_element_type=jnp.float32)


def _rms(x):
    return x * lax.rsqrt(jnp.mean(x * x, axis=-1, keepdims=True) + NORM_EPS)


def _rwkv_pre_kernel(x_ref, shift_ref, gain_ref, mu_ref, vec_ref, wrkv_ref, w1_ref, a1_ref, g1_ref,
                     w2_ref, a2_ref, g2_ref,
                     r_ref, lw_ref, k_ref, v_ref, kk_ref, a_ref, g_ref, hlast_ref, prev_ref):
    @pl.when(pl.program_id(1) == 0)
    def _():
        prev_ref[...] = shift_ref[0]

    tm = x_ref.shape[1]
    h = _rms(x_ref[0]) * gain_ref[...]
    first_row = lax.broadcasted_iota(jnp.int32, h.shape, 0) == 0
    h_prev = jnp.where(first_row, prev_ref[...], pltpu.roll(h, 1, 0))
    prev_ref[...] = h[tm - 1:tm, :]
    hlast_ref[0] = h[tm - 1:tm, :]
    dx = h_prev - h
    mix = lambda i: (h + dx * mu_ref[i:i + 1, :]).astype(jnp.bfloat16)
    w0, a0, k_k, k_a = (vec_ref[i:i + 1, :] for i in range(4))
    r_ref[0] = _bdot(mix(0), wrkv_ref[0])
    k = _bdot(mix(2), wrkv_ref[1])
    v_ref[0] = _bdot(mix(3), wrkv_ref[2])
    lw_ref[0] = -DECAY_SCALE * jax.nn.sigmoid(w0 + _bdot(jnp.tanh(_bdot(mix(1), w1_ref[...])), w2_ref[...]))
    a = jax.nn.sigmoid(a0 + _bdot(_bdot(mix(4), a1_ref[...]), a2_ref[...]))
    g_ref[0] = _bdot(jax.nn.sigmoid(_bdot(mix(5), g1_ref[...])), g2_ref[...])
    a_ref[0] = a
    kk_ref[0] = k * k_k
    k_ref[0] = k * (1.0 + (a - 1.0) * k_a)


def _out_proj_kernel(*refs, n_terms, gated, routed):
    terms = refs[:n_terms]
    rest = refs[n_terms:]
    if gated:
        gate_ref, rest = rest[0], rest[1:]
    x_ref, w_ref, gain_ref = rest[:3]
    rest = rest[3:]
    if routed:
        wr_ref, rest = rest[0], rest[1:]
    xo_ref, ho_ref = rest[:2]
    y = terms[0][...].astype(jnp.float32)
    for t in terms[1:]:
        y = y + t[...].astype(jnp.float32)
    if gated:
        y = y * gate_ref[...]
    xo = x_ref[...] + _bdot(y, w_ref[...])
    xo_ref[...] = xo
    h = _rms(xo) * gain_ref[...]
    ho_ref[...] = h.astype(ho_ref.dtype)
    if routed:
        h_hi, h_lo = _split_bf16(h)
        w_hi, w_lo = _split_bf16(wr_ref[...])
        dot = functools.partial(jnp.dot, preferred_element_type=jnp.float32)
        rest[2][...] = dot(h_hi, w_hi) + (dot(h_hi, w_lo) + dot(h_lo, w_hi))


def out_proj(terms, gate, x, w, gain, w_router=None):
    m, d = x.shape
    kdim = w.shape[0]
    tm = min(ROW_TILE, m)
    rows = lambda width: pl.BlockSpec((tm, width), lambda i: (i, 0))
    full = lambda a: pl.BlockSpec(a.shape, lambda i: (0, 0))
    ins = list(terms) + ([gate] if gate is not None else [])
    consts = [w.astype(jnp.bfloat16), gain.reshape(1, d)]
    out_specs = [rows(d), rows(d)]
    out_shape = [jax.ShapeDtypeStruct((m, d), jnp.float32), jax.ShapeDtypeStruct((m, d), jnp.bfloat16)]
    if w_router is not None:
        consts.append(jnp.pad(w_router, ((0, 0), (0, 128 - w_router.shape[1]))))
        out_specs.append(rows(128))
        out_shape.append(jax.ShapeDtypeStruct((m, 128), jnp.float32))
    return pl.pallas_call(
        functools.partial(_out_proj_kernel, n_terms=len(terms), gated=gate is not None,
                          routed=w_router is not None),
        grid=(m // tm,),
        in_specs=[rows(kdim)] * len(ins) + [rows(d)] + [full(a) for a in consts],
        out_specs=out_specs,
        out_shape=out_shape,
        compiler_params=pltpu.CompilerParams(dimension_semantics=("parallel",),
                                             vmem_limit_bytes=48 * 1024 * 1024),
        name="out_proj",
    )(*ins, x, *consts)


def rwkv7_layer(x, shift0, s0, gain, mu, w_rkv, w0, w1, w2, a0, a1, a2, g1, g2, k_k, k_a, r_k, gn_w, gn_b, w_o,
                gain_next):
    n, T, D = x.shape
    tm = min(ROW_TILE, T)
    bf = lambda t: t.astype(jnp.bfloat16)
    seq = pl.BlockSpec((1, tm, D), lambda b, t: (b, t, 0))
    full = lambda a: pl.BlockSpec(a.shape, lambda b, t: (0,) * a.ndim)
    per_seq = pl.BlockSpec((1, 1, D), lambda b, t: (b, 0, 0))
    vecs = jnp.stack([w0, a0, k_k, k_a]).astype(jnp.float32)
    weights = [bf(w_rkv), bf(w1), bf(a1), bf(g1), bf(w2), bf(a2), bf(g2)]
    small = [gain.reshape(1, D), mu, vecs]
    outs = pl.pallas_call(
        _rwkv_pre_kernel,
        grid=(n, T // tm),
        in_specs=[seq, per_seq] + [full(a) for a in small + weights],
        out_specs=[seq] * 7 + [per_seq],
        out_shape=[jax.ShapeDtypeStruct((n, T, D), jnp.float32)] * 7
                  + [jax.ShapeDtypeStruct((n, 1, D), jnp.float32)],
        scratch_shapes=[pltpu.VMEM((1, D), jnp.float32)],
        compiler_params=pltpu.CompilerParams(dimension_semantics=("parallel", "arbitrary"),
                                             vmem_limit_bytes=56 * 1024 * 1024),
        name="rwkv_pre",
    )(x, shift0.reshape(n, 1, D), *small, *weights)
    r, lw, k, v, kk, a_gate, g, h_last = outs
    y, s_fin = wkv7_chunked(r, lw, k, v, kk, a_gate, r_k.reshape(-1), gn_w, gn_b, s0.astype(jnp.float32))
    flat = lambda t: t.reshape(n * T, D)
    x1, h1 = out_proj([flat(y)], flat(g), flat(x), w_o, gain_next)
    return x1, h1, s_fin, h_last.reshape(n, D)


def rwkv7_time_mix(h, h_prev, s0, mu, w_rkv, w0, w1, w2, a0, a1, a2, g1, g2, k_k, k_a, r_k, gn_w, gn_b, w_o,
                   precise=False):
    n, T, D = h.shape
    f32 = jnp.float32
    mm = functools.partial(dense, precise=precise)
    dx = jnp.concatenate([h_prev[:, None, :].astype(h.dtype), h[:, :-1]], axis=1) - h
    xr, xw, xk, xv, xa, xg = (h + dx * mu[i] for i in range(6))
    r = mm(xr, w_rkv[0])
    k = mm(xk, w_rkv[1])
    v = mm(xv, w_rkv[2])
    log_decay = -DECAY_SCALE * jax.nn.sigmoid((w0 + mm(jnp.tanh(mm(xw, w1)), w2)).astype(f32))
    a = jax.nn.sigmoid((a0 + mm(mm(xa, a1), a2)).astype(f32))
    g = mm(jax.nn.sigmoid(mm(xg, g1)), g2)
    heads = lambda t: t.astype(f32).reshape(n, T, RWKV_HEADS, RWKV_HEAD)
    kk = heads(k * k_k)
    kk = kk / jnp.maximum(jnp.sqrt(jnp.sum(kk * kk, axis=-1, keepdims=True)), 1e-12)
    a_h = heads(a)
    k_h = heads(k.astype(f32) * (1.0 + (a - 1.0) * k_a.astype(f32)))
    r_h, v_h = heads(r), heads(v)
    y, s_fin = wkv7_scan(r_h, heads(log_decay), k_h, v_h, -kk, kk * a_h, s0.astype(f32))
    mean = jnp.mean(y, axis=-1, keepdims=True)
    var = jnp.mean(jnp.square(y - mean), axis=-1, keepdims=True)
    y = ((y - mean) * lax.rsqrt(var + GN_EPS)).reshape(n, T, D) * gn_w.astype(f32) + gn_b.astype(f32)
    bonus = jnp.sum(r_h * k_h * r_k.astype(f32), axis=-1, keepdims=True) * v_h
    y = (y + bonus.reshape(n, T, D)).astype(h.dtype)
    return mm(y * g, w_o), s_fin, h[:, -1]


def shared_kv_rows(x, norm_kv, w_kv, pos, precise=False):
    n, T, _ = x.shape
    kv = dense(rmsnorm(x, norm_kv), w_kv, precise).reshape(n, T, N_BRANCH, 2, NSA_KV_GROUPS, NSA_HEAD_DIM)
    cmp_kv = kv[:, :, 0]
    slc_kv = jnp.stack([rope_partial(kv[:, :, 1, 0], pos), kv[:, :, 1, 1]], axis=2)
    win_kv = jnp.stack([rope_partial(kv[:, :, 2, 0], pos), kv[:, :, 2, 1]], axis=2)
    return cmp_kv, slc_kv, win_kv


CHUNK_LANES = CMP_STRIDE * 2 * NSA_KV_GROUPS * NSA_HEAD_DIM
HIDDEN_LANES = 2 * NSA_KV_GROUPS * 2 * CMP_HIDDEN
PAGES_PER_STEP = 32
CHUNKS_PER_PAGE = PAGE_SIZE // CMP_STRIDE


def _chunk_weights(cmp_w1):
    w = cmp_w1.reshape(2, 2, CMP_STRIDE, NSA_HEAD_DIM, CMP_HIDDEN)
    w = w.transpose(2, 0, 3, 1, 4)
    eye = jnp.eye(2, dtype=w.dtype)
    big = jnp.einsum('ab,cf,jaehk->jacebfhk', eye, jnp.eye(NSA_KV_GROUPS, dtype=w.dtype), w)
    return big.reshape(CHUNK_LANES, HIDDEN_LANES).astype(jnp.bfloat16)


def _token_weights(cmp_w1):
    w = cmp_w1.reshape(2, 2, CMP_STRIDE, NSA_HEAD_DIM, CMP_HIDDEN)
    return w.transpose(2, 0, 3, 1, 4).reshape(CMP_STRIDE, 2, NSA_HEAD_DIM, 2 * CMP_HIDDEN).astype(jnp.bfloat16)


def _paged_chunk_kernel(pt_ref, *refs):
    pages, w_ref, o_ref, xt_ref = (refs[:PAGES_PER_STEP], refs[PAGES_PER_STEP], refs[PAGES_PER_STEP + 1],
                                   refs[PAGES_PER_STEP + 2])
    planes = [(kv, g) for kv in range(2) for g in range(NSA_KV_GROUPS)]
    tok = lax.broadcasted_iota(jnp.int32, (PAGE_SIZE, PAGE_SIZE), 0)
    dst = lax.broadcasted_iota(jnp.int32, (PAGE_SIZE, PAGE_SIZE), 1)
    regroup = (tok == (dst % CHUNKS_PER_PAGE) * CMP_STRIDE + dst // CHUNKS_PER_PAGE).astype(jnp.bfloat16)
    for p in range(PAGES_PER_STEP):
        for i, (kv, g) in enumerate(planes):
            xt_ref[p, i] = _bdot(pages[p][0, kv, g], regroup).T
    width = 2 * CMP_HIDDEN
    for i, (kv, g) in enumerate(planes):
        acc = None
        for j in range(CMP_STRIDE):
            rows = [xt_ref[p, i, j * CHUNKS_PER_PAGE:(j + 1) * CHUNKS_PER_PAGE, :] for p in range(PAGES_PER_STEP)]
            y = _bdot(jnp.concatenate(rows, axis=0), w_ref[j, kv])
            acc = y if acc is None else acc + y
        o_ref[0, :, i * width:(i + 1) * width] = acc


def paged_chunk_hidden(cache, page_table, w_token, extra_rows=0):
    n, n_pages = page_table.shape
    planes = cache.transpose(0, 2, 3, 4, 1)
    rows = PAGES_PER_STEP * CHUNKS_PER_PAGE

    def page_spec(k):
        return pl.BlockSpec((1,) + planes.shape[1:], lambda b, s, pt: (pt[b, s * PAGES_PER_STEP + k], 0, 0, 0, 0))

    return pl.pallas_call(
        _paged_chunk_kernel,
        grid_spec=pltpu.PrefetchScalarGridSpec(
            num_scalar_prefetch=1,
            grid=(n, n_pages // PAGES_PER_STEP),
            in_specs=[page_spec(k) for k in range(PAGES_PER_STEP)]
                     + [pl.BlockSpec(w_token.shape, lambda b, s, pt: (0, 0, 0, 0))],
            out_specs=pl.BlockSpec((1, rows, HIDDEN_LANES), lambda b, s, pt: (b, s, 0)),
            scratch_shapes=[pltpu.VMEM((PAGES_PER_STEP, 2 * NSA_KV_GROUPS, PAGE_SIZE, NSA_HEAD_DIM), jnp.float32)],
        ),
        out_shape=jax.ShapeDtypeStruct((n, n_pages * CHUNKS_PER_PAGE + extra_rows, HIDDEN_LANES), jnp.float32),
        compiler_params=pltpu.CompilerParams(dimension_semantics=("parallel", "arbitrary"),
                                             vmem_limit_bytes=48 * 1024 * 1024),
        name="paged_chunk_hidden",
    )(page_table.astype(jnp.int32), *([planes] * PAGES_PER_STEP), w_token)


def _matmul_kernel(x_ref, w_ref, o_ref):
    o_ref[...] = jnp.dot(x_ref[...].astype(jnp.bfloat16), w_ref[...],
                         preferred_element_type=jnp.float32).astype(o_ref.dtype)


def matmul(x, w, rows_per_step=512, out_dtype=jnp.float32):
    m, k = x.shape
    nn = w.shape[1]
    tm = min(rows_per_step, m)
    return pl.pallas_call(
        _matmul_kernel,
        grid=(m // tm,),
        in_specs=[pl.BlockSpec((tm, k), lambda i: (i, 0)), pl.BlockSpec((k, nn), lambda i: (0, 0))],
        out_specs=pl.BlockSpec((tm, nn), lambda i: (i, 0)),
        out_shape=jax.ShapeDtypeStruct((m, nn), out_dtype),
        compiler_params=pltpu.CompilerParams(dimension_semantics=("parallel",),
                                             vmem_limit_bytes=48 * 1024 * 1024),
        name="matmul",
    )(x, w.astype(jnp.bfloat16))


def _split_bf16(x):
    hi = x.astype(jnp.bfloat16)
    return hi, (x - hi.astype(jnp.float32)).astype(jnp.bfloat16)


def _matmul3_kernel(x_ref, w_ref, o_ref):
    x_hi, x_lo = _split_bf16(x_ref[...])
    w_hi, w_lo = _split_bf16(w_ref[...])
    dot = functools.partial(jnp.dot, preferred_element_type=jnp.float32)
    o_ref[...] = dot(x_hi, w_hi) + (dot(x_hi, w_lo) + dot(x_lo, w_hi))


def matmul_precise(x, w, cols_per_step=512):
    m, k = x.shape
    nn = w.shape[1]
    pad = (-nn) % 128
    if pad:
        w = jnp.pad(w, ((0, 0), (0, pad)))
    tn = math.gcd(cols_per_step, nn + pad)
    tm = min(m, 512)
    out = pl.pallas_call(
        _matmul3_kernel,
        grid=(m // tm, (nn + pad) // tn),
        in_specs=[pl.BlockSpec((tm, k), lambda i, j: (i, 0)), pl.BlockSpec((k, tn), lambda i, j: (0, j))],
        out_specs=pl.BlockSpec((tm, tn), lambda i, j: (i, j)),
        out_shape=jax.ShapeDtypeStruct((m, nn + pad), jnp.float32),
        compiler_params=pltpu.CompilerParams(dimension_semantics=("parallel", "parallel")),
        name="matmul_precise",
    )(x, w)
    return out[:, :nn] if pad else out


def dense(x, w, precise=False):
    if precise:
        return matmul_precise(x.reshape(-1, x.shape[-1]), w).reshape(x.shape[:-1] + (w.shape[1],))
    return x @ w


def compress_from_hidden(hidden, pos_emb, w1, b1, w2, n_blocks=None):
    n, rows = hidden.shape[:2]
    C = rows if n_blocks is None else n_blocks
    assert rows % 8 == 0
    w1r = w1.reshape(2, CMP_BLOCK, NSA_HEAD_DIM, CMP_HIDDEN)
    bias = jnp.einsum('ajd,ajdh->ah', pos_emb, w1r) + b1
    out = pl.pallas_call(
        functools.partial(_compress_finish_kernel, n_blocks=C),
        grid=(n,),
        in_specs=[pl.BlockSpec((1, rows, HIDDEN_LANES), lambda b: (b, 0, 0)),
                  pl.BlockSpec(bias.shape, lambda b: (0, 0)), pl.BlockSpec(w2.shape, lambda b: (0, 0, 0))],
        out_specs=pl.BlockSpec((1, rows, 2 * SLAB), lambda b: (b, 0, 0)),
        out_shape=jax.ShapeDtypeStruct((n, rows, 2 * SLAB), jnp.float32),
        compiler_params=pltpu.CompilerParams(dimension_semantics=("parallel",)),
        name="compress_finish",
    )(hidden, bias, w2.astype(jnp.bfloat16))[:, :C]
    shape = (n, C, NSA_KV_GROUPS, NSA_HEAD_DIM)
    return out[:, :, :SLAB].reshape(shape), out[:, :, SLAB:].reshape(shape)


def _compress_finish_kernel(h_ref, bias_ref, w2_ref, o_ref, *, n_blocks):
    h = h_ref[0]
    rows = h.shape[0]
    keep = lax.broadcasted_iota(jnp.int32, (rows, CMP_HIDDEN), 0) < n_blocks - 1
    outs = []
    for i in range(2 * NSA_KV_GROUPS):
        kv = i // NSA_KV_GROUPS
        first = h[:, 2 * i * CMP_HIDDEN:(2 * i + 1) * CMP_HIDDEN]
        second = h[:, (2 * i + 1) * CMP_HIDDEN:(2 * i + 2) * CMP_HIDDEN]
        nxt = jnp.where(keep, pltpu.roll(second, rows - 1, 0), 0.0)
        act = jax.nn.gelu(first + nxt + bias_ref[kv:kv + 1, :])
        outs.append(_bdot(act, w2_ref[kv]))
    o_ref[0] = jnp.concatenate(outs, axis=-1)


def _tail_chunk_kernel(x_ref, w_ref, prev_ref, o_ref):
    o_ref[0] = _bdot(x_ref[0], w_ref[...])


def rows_chunk_hidden(kv_rows, w_chunk):
    n, T = kv_rows.shape[:2]
    chunks = kv_rows.reshape(n * T // CMP_STRIDE, CHUNK_LANES)
    return matmul(chunks, w_chunk, rows_per_step=256).reshape(n, T // CMP_STRIDE, HIDDEN_LANES)


def nsa_query(h, w_in, precise=False):
    n, T, _ = h.shape
    proj = dense(h, w_in, precise)
    q = proj[..., :NSA_HEADS * NSA_HEAD_DIM].reshape(n, T, NSA_KV_GROUPS, NSA_GROUP_HEADS, NSA_HEAD_DIM)
    gate = proj[..., NSA_HEADS * NSA_HEAD_DIM:].reshape(n, T, NSA_KV_GROUPS, NSA_GROUP_HEADS, N_BRANCH)
    return q, gate


def compressed_branch_and_selection(q, t_pos, kc, vc):
    scale = NSA_HEAD_DIM ** -0.5
    n_c = kc.shape[1]
    vis = (jnp.arange(n_c) * CMP_STRIDE + CMP_BLOCK - 1)[None, :] <= t_pos[:, None]
    p_cmp = masked_softmax(jnp.einsum('nqghd,ncgd->nghqc', q, kc) * scale, vis)
    o_cmp = jnp.einsum('nghqc,ncgd->nqghd', p_cmp.astype(vc.dtype), vc)
    ratio = SLC_BLOCK // CMP_STRIDE
    lead = CMP_BLOCK // CMP_STRIDE - 1
    n_s = n_c // ratio
    pg = jnp.pad(jnp.sum(p_cmp, axis=2), ((0, 0), (0, 0), (0, 0), (lead, 0)))
    p_slc = pg[..., 0:ratio * n_s:ratio]
    for o in range(1, ratio + lead):
        p_slc = p_slc + pg[..., o:o + ratio * n_s:ratio]
    jb = jnp.arange(n_s)[None, :]
    jt = (t_pos // SLC_BLOCK)[:, None]
    forced = (jb == 0) | (jb == jt) | (jb == jt - 1)
    score = jnp.where(jb > jt, -jnp.inf, jnp.where(forced, FORCED_SCORE, p_slc))
    before = (score[..., :, None] > score[..., None, :]) | (
        (score[..., :, None] == score[..., None, :]) & (jb[0][:, None] < jb[0][None, :]))
    rank = jnp.sum(before, axis=-2)
    slots = jnp.arange(min(SLC_TOPN, n_s))
    idx = jnp.sum(jnp.where(rank[..., None, :] == slots[:, None], jb[0], 0), axis=-1)
    return o_cmp, idx


def _dot3(x, y, dims):
    x_hi, x_lo = _split_bf16(x)
    y_hi, y_lo = _split_bf16(y)
    dot = lambda a, b: lax.dot_general(a, b, dims, preferred_element_type=jnp.float32)
    return dot(x_hi, y_hi) + (dot(x_hi, y_lo) + dot(x_lo, y_hi))


def _decode_kernel(half_ref, new_ref, phys_ref, q_ref, rows_ref, win_ref, *refs):
    b = pl.program_id(0)
    n_sel = len(refs) - 1
    pages, o_ref = refs[:n_sel], refs[n_sel]
    k_per_group = n_sel // NSA_KV_GROUPS
    hg, dh = NSA_GROUP_HEADS, NSA_HEAD_DIM
    tok_half = lax.broadcasted_iota(jnp.int32, (hg, PAGE_SIZE), 1) // SLC_BLOCK
    rows = rows_ref[0]

    def attend(qg, scores, values, k_new, v_new, new_ok):
        s_new = jnp.where(new_ok, jnp.sum(qg * k_new, axis=-1, keepdims=True), MASKED)
        m = s_new
        for s in scores:
            m = jnp.maximum(m, jnp.max(s, axis=-1, keepdims=True))
        e_new = jnp.exp(s_new - m)
        total, out = e_new, e_new * v_new
        for s, v in zip(scores, values):
            e = jnp.exp(s - m)
            total = total + jnp.sum(e, axis=-1, keepdims=True)
            out = out + _dot3(e, v, _NT)
        return out / jnp.maximum(total, 1e-30)

    outs = []
    for g in range(NSA_KV_GROUPS):
        qg = q_ref[0, g * hg:(g + 1) * hg, :]
        lanes = slice(g * dh, (g + 1) * dh)
        scores, values = [], []
        any_new = jnp.int32(0)
        for k in range(k_per_group):
            j = g * k_per_group + k
            plane = pages[j][0]
            is_new = new_ref[b, j]
            ok = (tok_half == half_ref[b, j]) & (is_new == 0)
            scores.append(jnp.where(ok, _dot3(qg, plane[0, 0], _NN), MASKED))
            values.append(plane[1, 0])
            any_new = jnp.maximum(any_new, is_new)
        o_slc = attend(qg, scores, values, rows[0:1, lanes], rows[1:2, lanes], any_new > 0)
        o_win = attend(qg, [_dot3(qg, win_ref[0, 0, g], _NN)], [win_ref[0, 1, g]],
                       rows[2:3, lanes], rows[3:4, lanes], True)
        outs.append(jnp.concatenate([o_slc, o_win], axis=-1))
    o_ref[0] = jnp.concatenate(outs, axis=0)


def decode_attention(q_rot, idx, page_table, cache_slc_kv, cache_win_kv, slc_new, win_new):
    n = q_rot.shape[0]
    sub = PAGE_SIZE // SLC_BLOCK
    n_past_blk = PAST_LEN // SLC_BLOCK
    assert cache_win_kv.shape[1] <= WINDOW and cache_win_kv.shape[1] <= PAST_LEN
    k_sel = idx.shape[-1]
    flat = idx.reshape(n, NSA_KV_GROUPS * k_sel)
    past = jnp.minimum(flat, n_past_blk - 1)
    hit = (past // sub)[:, :, None] == jnp.arange(page_table.shape[1])[None, None, :]
    phys = jnp.sum(jnp.where(hit, page_table[:, None, :], 0), axis=-1).astype(jnp.int32)
    half = (past % sub).astype(jnp.int32)
    is_new = (flat >= n_past_blk).astype(jnp.int32)
    pool = cache_slc_kv.transpose(0, 2, 3, 4, 1)
    window = cache_win_kv.transpose(0, 2, 3, 4, 1)
    rows = jnp.concatenate([slc_new.reshape(n, 2, SLAB), win_new.reshape(n, 2, SLAB)], axis=1)

    def page_spec(j):
        g = j // k_sel
        return pl.BlockSpec((1, 2, 1, NSA_HEAD_DIM, PAGE_SIZE), lambda b, hf, nw, ph: (ph[b, j], 0, g, 0, 0))

    out = pl.pallas_call(
        _decode_kernel,
        grid_spec=pltpu.PrefetchScalarGridSpec(
            num_scalar_prefetch=3,
            grid=(n,),
            in_specs=[pl.BlockSpec((1,) + q_rot.shape[1:], lambda b, hf, nw, ph: (b, 0, 0)),
                      pl.BlockSpec((1,) + rows.shape[1:], lambda b, hf, nw, ph: (b, 0, 0)),
                      pl.BlockSpec((1,) + window.shape[1:], lambda b, hf, nw, ph: (b, 0, 0, 0, 0))]
                     + [page_spec(j) for j in range(NSA_KV_GROUPS * k_sel)],
            out_specs=pl.BlockSpec((1, NSA_HEADS, 2 * NSA_HEAD_DIM), lambda b, hf, nw, ph: (b, 0, 0)),
        ),
        out_shape=jax.ShapeDtypeStruct((n, NSA_HEADS, 2 * NSA_HEAD_DIM), jnp.float32),
        compiler_params=pltpu.CompilerParams(dimension_semantics=("parallel",)),
        name="decode_attention",
    )(half, is_new, phys, q_rot, rows, window, *([pool] * (NSA_KV_GROUPS * k_sel)))
    return out[:, :, :NSA_HEAD_DIM], out[:, :, NSA_HEAD_DIM:]


NSA_SLABS = NSA_GROUP_HEADS
NSA_TILES = NSA_SLABS * NSA_KV_GROUPS
SELECTED_KEY_BLOCK = 512
DENSE_BATCH = 16
MASKED = -1e30
SLC_PER_CMP = SLC_BLOCK // CMP_STRIDE
CMP_LEAD = CMP_BLOCK // CMP_STRIDE - 1
SLAB = NSA_KV_GROUPS * NSA_HEAD_DIM
SEL_LANES = 64


def _group_tiles(q_ref, qs_ref):
    tq = q_ref.shape[1]
    lane = lax.broadcasted_iota(jnp.int32, (tq, SLAB), 1)
    for i in range(NSA_SLABS):
        qs = q_ref[0, :, i * 128:(i + 1) * 128]
        for g in range(NSA_KV_GROUPS):
            in_group = (lane >= g * NSA_HEAD_DIM) & (lane < (g + 1) * NSA_HEAD_DIM)
            qs_ref[2 * i + g] = jnp.where(in_group, qs, jnp.zeros_like(qs)).astype(jnp.bfloat16)


def _merge_groups(o_ref, tiles, gate_ref, branch):
    tq = tiles[0].shape[0]
    lane = lax.broadcasted_iota(jnp.int32, (tq, SLAB), 1)
    gates = gate_ref[0]
    col = lambda g, i: (g * NSA_GROUP_HEADS + i) * N_BRANCH + branch
    for i in range(NSA_SLABS):
        lo = tiles[2 * i] * gates[:, col(0, i):col(0, i) + 1]
        hi = tiles[2 * i + 1] * gates[:, col(1, i):col(1, i) + 1]
        o_ref[0, :, i * 128:(i + 1) * 128] = jnp.where(lane < NSA_HEAD_DIM, lo, hi).astype(o_ref.dtype)


def _nsa_cmp_kernel(q_ref, kc_ref, vc_ref, gate_ref, o_ref, sel_ref, qs_ref):
    qi = pl.program_id(1)
    tq = q_ref.shape[1]
    nc = kc_ref.shape[1]
    ns = nc // SLC_PER_CMP
    t0 = qi * tq
    _group_tiles(q_ref, qs_ref)
    kc = kc_ref[0]
    vc = vc_ref[0]
    t_row = t0 + lax.broadcasted_iota(jnp.int32, (tq, nc), 0)
    c_pos = lax.broadcasted_iota(jnp.int32, (tq, nc), 1) * CMP_STRIDE + (CMP_BLOCK - 1)
    vis = c_pos <= t_row
    tiles = range(NSA_TILES)
    s = [_mm(qs_ref[r], kc, _NT) for r in tiles]
    p = []
    for r in tiles:
        sr = jnp.where(vis, s[r], MASKED)
        m = jnp.max(sr, axis=-1, keepdims=True)
        e = jnp.where(vis, jnp.exp(sr - m), 0.0)
        p.append(e / jnp.maximum(jnp.sum(e, axis=-1, keepdims=True), 1e-30))
    _merge_groups(o_ref, [_mm(p[r], vc, _NN) for r in tiles], gate_ref, 0)

    jrow = lax.broadcasted_iota(jnp.int32, (ns, nc), 0)
    ccol = lax.broadcasted_iota(jnp.int32, (ns, nc), 1)
    pool = ((ccol >= SLC_PER_CMP * jrow - CMP_LEAD) & (ccol < SLC_PER_CMP * (jrow + 1))).astype(jnp.bfloat16)
    jb = lax.broadcasted_iota(jnp.int32, (ns, tq), 0)
    jt = (t0 + lax.broadcasted_iota(jnp.int32, (ns, tq), 1)) // SLC_BLOCK
    forced = (jb == 0) | (jb == jt) | (jb == jt - 1)
    sel_t = []
    for g in range(NSA_KV_GROUPS):
        pg = p[g]
        for i in range(1, NSA_SLABS):
            pg = pg + p[2 * i + g]
        pg_hi = pg.astype(jnp.bfloat16)
        pg_lo = (pg - pg_hi.astype(jnp.float32)).astype(jnp.bfloat16)
        p_slc = (lax.dot_general(pool, pg_hi, _NT, preferred_element_type=jnp.float32)
                 + lax.dot_general(pool, pg_lo, _NT, preferred_element_type=jnp.float32))
        score = jnp.where(jb > jt, -jnp.inf, jnp.where(forced, FORCED_SCORE, p_slc))
        rank = jnp.zeros((ns, tq), jnp.float32)
        for i in range(ns):
            row = score[i:i + 1, :]
            tie = jnp.where(jb > i, 1.0, 0.0)
            rank = rank + jnp.where(row > score, 1.0, 0.0) + jnp.where(row == score, tie, 0.0)
        sel_t.append(jnp.where(rank < min(SLC_TOPN, ns), 1.0, 0.0))
        if ns < SEL_LANES:
            sel_t.append(jnp.zeros((SEL_LANES - ns, tq), jnp.float32))
    sel_ref[0] = jnp.concatenate(sel_t, axis=0).T.astype(sel_ref.dtype)


def _nsa_dense_kernel(q_ref, k_ref, vt_ref, sel_ref, gate_ref, o_ref, qs_ref, m_ref, acc_ref, *, windowed, kblock):
    qi = pl.program_id(1)
    tq = q_ref.shape[1]
    t0 = qi * tq
    _group_tiles(q_ref, qs_ref)
    m_ref[...] = jnp.full(m_ref.shape, MASKED, jnp.float32)
    acc_ref[...] = jnp.zeros(acc_ref.shape, jnp.float32)
    k_row = lax.broadcasted_iota(jnp.int32, (kblock, tq), 0)
    t_lane = t0 + lax.broadcasted_iota(jnp.int32, (kblock, tq), 1)
    tiles = range(NSA_TILES)
    first = 0
    last = 0 if windowed else (t0 + tq - 1) // kblock

    def key_block(kb, carry):
        start = pl.multiple_of(jnp.maximum(t0 - WINDOW, 0) if windowed else kb * kblock, tq)
        kblk = k_ref[0, pl.ds(start, kblock), :]
        vt = vt_ref[0, :, pl.ds(start, kblock)]
        k_pos = start + k_row
        if windowed:
            allowed = [(k_pos <= t_lane) & (k_pos >= t_lane - WINDOW)] * NSA_KV_GROUPS
        else:
            sel = sel_ref[0]
            n_sel = NSA_KV_GROUPS * SEL_LANES
            e_key = lax.broadcasted_iota(jnp.int32, (kblock, n_sel), 0)
            e_lane = lax.broadcasted_iota(jnp.int32, (kblock, n_sel), 1)
            blk = kb * (kblock // SLC_BLOCK) + e_key // SLC_BLOCK
            allowed = []
            for g in range(NSA_KV_GROUPS):
                expand = (e_lane == blk + g * SEL_LANES).astype(jnp.bfloat16)
                picked = lax.dot_general(expand, sel, _NT, preferred_element_type=jnp.float32)
                allowed.append((picked > 0.5) & (k_pos <= t_lane))
        bias = [jnp.where(ok, 0.0, MASKED) for ok in allowed]
        d_row = lax.broadcasted_iota(jnp.int32, vt.shape, 0)
        vt_g = [jnp.where((d_row >= g * NSA_HEAD_DIM) & (d_row < (g + 1) * NSA_HEAD_DIM), vt, jnp.ones_like(vt))
                for g in range(NSA_KV_GROUPS)]
        s = [_mm(kblk, qs_ref[r], _NT) for r in tiles]
        for lo in range(0, NSA_TILES, DENSE_BATCH):
            batch = range(lo, lo + DENSE_BATCH)
            p, alpha = {}, {}
            for r in batch:
                sr = s[r] + bias[r % NSA_KV_GROUPS]
                m_prev = m_ref[r]
                m_new = jnp.maximum(m_prev, jnp.max(sr, axis=0, keepdims=True))
                alpha[r] = jnp.exp(m_prev - m_new)
                p[r] = jnp.exp(sr - m_new).astype(jnp.bfloat16)
                m_ref[r] = m_new
            pv = {r: jnp.dot(vt_g[r % NSA_KV_GROUPS], p[r], preferred_element_type=jnp.float32) for r in batch}
            for r in batch:
                acc_ref[r] = alpha[r] * acc_ref[r] + pv[r]
        return carry

    lax.fori_loop(first, last + 1, key_block, 0)
    out = []
    for r in tiles:
        acc = acc_ref[r]
        sum_row = (1 - r % NSA_KV_GROUPS) * NSA_HEAD_DIM
        out.append((acc / jnp.maximum(acc[sum_row:sum_row + 1, :], 1e-30)).T)
    _merge_groups(o_ref, out, gate_ref, 2 if windowed else 1)


def nsa_prompt_attention(q, q_rot, kc, vc, kv_att, gates):
    n, T, D = q.shape
    tq = Q_BLOCK
    qspec = pl.BlockSpec((1, tq, D), lambda b, i: (b, i, 0))
    whole = lambda a: pl.BlockSpec((1,) + a.shape[1:], lambda b, i: (b, 0, 0))
    lanes = lambda j: pl.BlockSpec((1, T, SLAB), lambda b, i: (b, 0, j))
    n_sel = NSA_KV_GROUPS * SEL_LANES
    sel_spec = pl.BlockSpec((1, tq, n_sel), lambda b, i: (b, i, 0))
    gate_spec = pl.BlockSpec((1, tq, gates.shape[2]), lambda b, i: (b, i, 0))
    params = pltpu.CompilerParams(dimension_semantics=("parallel", "arbitrary"),
                                  vmem_limit_bytes=48 * 1024 * 1024)
    qs_scratch = pltpu.VMEM((NSA_TILES, tq, SLAB), jnp.bfloat16)
    o_cmp, sel = pl.pallas_call(
        _nsa_cmp_kernel,
        grid=(n, T // tq),
        in_specs=[qspec, whole(kc), whole(vc), gate_spec],
        out_specs=[qspec, sel_spec],
        out_shape=[jax.ShapeDtypeStruct((n, T, D), jnp.bfloat16),
                   jax.ShapeDtypeStruct((n, T, n_sel), jnp.bfloat16)],
        scratch_shapes=[qs_scratch],
        compiler_params=params,
        name="nsa_cmp_select",
    )(q, kc, vc, gates)
    stat = pltpu.VMEM((NSA_TILES, 1, tq), jnp.float32)
    acc = pltpu.VMEM((NSA_TILES, SLAB, tq), jnp.float32)
    v_t = jnp.swapaxes(jnp.concatenate([kv_att[:, :, SLAB:2 * SLAB], kv_att[:, :, 3 * SLAB:]], axis=-1), 1, 2)
    rows = lambda j: pl.BlockSpec((1, SLAB, T), lambda b, i: (b, j, 0))

    def dense(windowed, branch, name):
        kblock = min(WINDOW + tq, T) if windowed else min(SELECTED_KEY_BLOCK, T)
        return pl.pallas_call(
            functools.partial(_nsa_dense_kernel, windowed=windowed, kblock=kblock),
            grid=(n, T // tq),
            in_specs=[qspec, lanes(2 * branch), rows(branch), sel_spec, gate_spec],
            out_specs=qspec,
            out_shape=jax.ShapeDtypeStruct((n, T, D), jnp.bfloat16),
            scratch_shapes=[qs_scratch, stat, acc],
            compiler_params=params,
            name=name,
        )(q_rot, kv_att, v_t, sel, gates)

    return o_cmp, dense(False, 0, "nsa_selected"), dense(True, 1, "nsa_window")


def _rope_tables(pos):
    half = ROT_DIM // 2
    inv = ROPE_THETA ** (-2.0 * jnp.arange(half, dtype=jnp.float32) / ROT_DIM)
    ang = pos.astype(jnp.float32)[:, None] * inv[None, :]
    rest = NSA_HEAD_DIM - ROT_DIM
    cos = jnp.concatenate([jnp.cos(ang), jnp.cos(ang), jnp.ones((pos.shape[0], rest), jnp.float32)], axis=1)
    sin = jnp.concatenate([-jnp.sin(ang), jnp.sin(ang), jnp.zeros((pos.shape[0], rest), jnp.float32)], axis=1)
    return jnp.tile(cos, (1, NSA_KV_GROUPS)), jnp.tile(sin, (1, NSA_KV_GROUPS))


def _nsa_pre_kernel(x_ref, gkv_ref, gmix_ref, cos_ref, sin_ref, wkv_ref, wq_ref, wg_ref,
                    cmp_ref, slc_ref, win_ref, kvb_ref, q_ref, qr_ref, gate_ref):
    xh = _rms(x_ref[0])
    cos, sin = cos_ref[...], sin_ref[...]
    low = lax.broadcasted_iota(jnp.int32, cos.shape, 1) % NSA_HEAD_DIM < ROT_DIM // 2

    def rope(t):
        swapped = jnp.where(low, pltpu.roll(t, SLAB - ROT_DIM // 2, 1), pltpu.roll(t, ROT_DIM // 2, 1))
        return t * cos + swapped * sin

    kv = _bdot(xh * gkv_ref[...], wkv_ref[...])
    part = lambda j: kv[:, j * SLAB:(j + 1) * SLAB]
    k_slc, k_win = rope(part(2)), rope(part(4))
    cmp_ref[0] = kv[:, :2 * SLAB]
    slc_ref[0] = jnp.concatenate([k_slc, part(3)], axis=-1)
    win_ref[0] = jnp.concatenate([k_win, part(5)], axis=-1)
    kvb_ref[0] = jnp.concatenate([k_slc, part(3), k_win, part(5)], axis=-1).astype(jnp.bfloat16)
    h = xh * gmix_ref[...]
    q = _bdot(h, wq_ref[...])
    q_ref[0] = q.astype(jnp.bfloat16)
    qr_ref[0] = jnp.concatenate([rope(q[:, i * SLAB:(i + 1) * SLAB]) for i in range(NSA_SLABS)],
                                axis=-1).astype(jnp.bfloat16)
    gate_ref[0] = jax.nn.sigmoid(_bdot(h, wg_ref[...]))


def nsa_pre(x, pos, norm_kv, norm_mix, w_kv, w_in):
    n, T, D = x.shape
    tm = min(ROW_TILE, T)
    nq = NSA_HEADS * NSA_HEAD_DIM
    cos, sin = _rope_tables(pos)
    w_q = (_to_slabs(w_in[:, :nq]) * NSA_HEAD_DIM ** -0.5).astype(jnp.bfloat16)
    w_g = jnp.pad(w_in[:, nq:], ((0, 0), (0, SLAB - (w_in.shape[1] - nq)))).astype(jnp.bfloat16)
    seq = lambda width: pl.BlockSpec((1, tm, width), lambda b, t: (b, t, 0))
    full = lambda a: pl.BlockSpec(a.shape, lambda b, t: (0,) * a.ndim)
    table = pl.BlockSpec((tm, SLAB), lambda b, t: (t, 0))
    consts = [norm_kv.reshape(1, D), norm_mix.reshape(1, D)]
    weights = [w_kv.astype(jnp.bfloat16), w_q, w_g]
    widths = [2 * SLAB, 2 * SLAB, 2 * SLAB, 4 * SLAB, nq, nq, SLAB]
    dtypes = [jnp.float32] * 3 + [jnp.bfloat16] * 3 + [jnp.float32]
    return pl.pallas_call(
        _nsa_pre_kernel,
        grid=(n, T // tm),
        in_specs=[seq(D)] + [full(a) for a in consts] + [table, table] + [full(a) for a in weights],
        out_specs=[seq(w) for w in widths],
        out_shape=[jax.ShapeDtypeStruct((n, T, w), dt) for w, dt in zip(widths, dtypes)],
        compiler_params=pltpu.CompilerParams(dimension_semantics=("parallel", "parallel"),
                                             vmem_limit_bytes=48 * 1024 * 1024),
        name="nsa_pre",
    )(x, *consts, cos, sin, *weights)


def _to_slabs(x):
    lead = x.shape[:-1]
    x = x.reshape(lead + (NSA_KV_GROUPS, NSA_GROUP_HEADS, NSA_HEAD_DIM))
    return jnp.swapaxes(x, -3, -2).reshape(lead + (NSA_HEADS * NSA_HEAD_DIM,))


def nsa_layer_prompt(x, norm_kv, norm_mix, w_kv, cmp_pos, cmp_w1, cmp_b1, cmp_w2, w_in, w_o, norm_next, w_router):
    n, T, D = x.shape
    cmp_kv, slc_kv, win_kv, kv_att, q, q_rot, gates = nsa_pre(x, jnp.arange(T), norm_kv, norm_mix, w_kv, w_in)
    hidden = rows_chunk_hidden(cmp_kv, _chunk_weights(cmp_w1))
    kc, vc = compress_from_hidden(hidden, cmp_pos, cmp_w1, cmp_b1, cmp_w2)
    lanes = lambda a: a.reshape(n, a.shape[1], SLAB).astype(jnp.bfloat16)
    branches = nsa_prompt_attention(q, q_rot, lanes(kc), lanes(vc), kv_att, gates)
    flat = lambda a: a.reshape(n * T, a.shape[-1])
    x_out, h_out, logits = out_proj([flat(o) for o in branches], None, flat(x), _to_slabs(w_o.T).T, norm_next,
                                    w_router)
    return x_out, h_out, logits, cmp_kv, slc_kv, win_kv


def sample_kv_context(cmp_new, slc_new, win_new, cache_cmp_kv, cache_slc_kv, cache_win_kv, page_table,
                      cmp_pos, cmp_w1, cmp_b1, cmp_w2):
    n, S = cmp_new.shape[:2]
    n_new_blk = -(-S // SLC_BLOCK)
    w_chunk = _chunk_weights(cmp_w1)
    n_new = n_new_blk * SLC_BLOCK // CMP_STRIDE
    tail = -(-n_new // 8) * 8
    new_rows = jnp.pad(cmp_new.astype(cache_cmp_kv.dtype), ((0, 0), (0, tail * CMP_STRIDE - S)) + ((0, 0),) * 3)
    hidden = paged_chunk_hidden(cache_cmp_kv, page_table, _token_weights(cmp_w1), extra_rows=tail)
    n_past = hidden.shape[1] - tail
    hidden = pl.pallas_call(
        _tail_chunk_kernel,
        grid=(n,),
        in_specs=[pl.BlockSpec((1, tail, CHUNK_LANES), lambda b: (b, 0, 0)),
                  pl.BlockSpec(w_chunk.shape, lambda b: (0, 0)), pl.BlockSpec(memory_space=pl.ANY)],
        out_specs=pl.BlockSpec((1, tail, HIDDEN_LANES), lambda b: (b, n_past // tail, 0)),
        out_shape=jax.ShapeDtypeStruct(hidden.shape, hidden.dtype),
        input_output_aliases={2: 0},
        name="tail_chunk_hidden",
    )(new_rows.reshape(n, tail, CHUNK_LANES), w_chunk, hidden)
    kc, vc = compress_from_hidden(hidden, cmp_pos, cmp_w1, cmp_b1, cmp_w2, n_blocks=n_past + n_new)
    win_all = jnp.concatenate([cache_win_kv, win_new.astype(cache_win_kv.dtype)], axis=1)
    return kc, vc, win_all


def sample_nsa(h, kc, vc, slc_new, win_new, cache_slc_kv, cache_win_kv, page_table, w_in, w_o):
    n, S, _ = h.shape
    assert S == 1
    q, gate = nsa_query(h, w_in, precise=True)
    t_pos = PAST_LEN + jnp.arange(S)
    with jax.default_matmul_precision("highest"):
        o_cmp, idx = compressed_branch_and_selection(q, t_pos, kc, vc)
    q_rot = (rope_partial(q, t_pos) * NSA_HEAD_DIM ** -0.5).reshape(n, NSA_HEADS, NSA_HEAD_DIM)
    o_slc, o_win = decode_attention(q_rot, idx[:, :, 0], page_table, cache_slc_kv, cache_win_kv, slc_new, win_new)
    g = jax.nn.sigmoid(gate.astype(jnp.float32)).astype(q.dtype)
    o = g[..., 0:1] * o_cmp + g[..., 1:2] * o_slc.reshape(q.shape) + g[..., 2:3] * o_win.reshape(q.shape)
    return dense(o.reshape(n, S, NSA_HEADS * NSA_HEAD_DIM), w_o, precise=True)


def kernel(x_prompt, x_sample, state_wkv, state_shift, cache_cmp_kv, cache_slc_kv, cache_win_kv, page_table, norm_mix, norm_ffn, norm_kv, norm_final, rw_mu, rw_w_rkv, rw_w0, rw_w1, rw_w2, rw_a0, rw_a1, rw_a2, rw_g1, rw_g2, rw_k_k, rw_k_a, rw_r_k, rw_gn_w, rw_gn_b, rw_w_o, nsa_w_kv, nsa_cmp_pos, nsa_cmp_w1, nsa_cmp_b1, nsa_cmp_w2, nsa_w_in, nsa_w_o, ffn_w_gu, ffn_w_down, moe_router, moe_w_gu, moe_w_down):
    cmp_params = (nsa_cmp_pos, nsa_cmp_w1, nsa_cmp_b1, nsa_cmp_w2)
    assert DEPTH == 2 and N_A_LAYERS == 1
    D = D_MODEL
    moe_gu, moe_down = moe_w_gu[0].astype(jnp.bfloat16), moe_w_down[0].astype(jnp.bfloat16)
    x_prompt, moe_gu, moe_down = lax.optimization_barrier((x_prompt, moe_gu, moe_down))

    n_p, T = x_prompt.shape[:2]
    x1, h1, wkv_fin, h_last = rwkv7_layer(
        x_prompt, jnp.zeros((n_p, D), x_prompt.dtype), jnp.zeros((n_p, RWKV_HEADS, RWKV_HEAD, RWKV_HEAD), jnp.float32),
        norm_mix[0], rw_mu[0], rw_w_rkv[0], rw_w0[0], rw_w1[0], rw_w2[0], rw_a0[0], rw_a1[0], rw_a2[0],
        rw_g1[0], rw_g2[0], rw_k_k[0], rw_k_a[0], rw_r_k[0], rw_gn_w[0], rw_gn_b[0], rw_w_o[0], norm_ffn[0])
    wkv_p, shift_p = wkv_fin[None], h_last[None]
    x2 = swiglu_residual(x1, h1, ffn_w_gu[0], ffn_w_down[0])
    x_p, h_p, logits_p, cmp_rows, slc_rows, win_rows = nsa_layer_prompt(
        x2.reshape(n_p, T, D), norm_kv, norm_mix[1], nsa_w_kv, *cmp_params, nsa_w_in[0], nsa_w_o[0], norm_ffn[1],
        moe_router[0])
    kv_shape = (n_p, T, 2, NSA_KV_GROUPS, NSA_HEAD_DIM)
    cmp_kv_p, slc_kv_p = cmp_rows.reshape(kv_shape), slc_rows.reshape(kv_shape)
    win_kv_p = win_rows.reshape(kv_shape)[:, -min(WINDOW, T):]

    pos_s = PAST_LEN + jnp.arange(x_sample.shape[1], dtype=jnp.int32)
    h = rmsnorm(x_sample, norm_mix[0])
    y, s_fin, h_last_s = rwkv7_time_mix(
        h, state_shift[0], state_wkv[0], rw_mu[0], rw_w_rkv[0], rw_w0[0], rw_w1[0], rw_w2[0],
        rw_a0[0], rw_a1[0], rw_a2[0], rw_g1[0], rw_g2[0], rw_k_k[0], rw_k_a[0],
        rw_r_k[0], rw_gn_w[0], rw_gn_b[0], rw_w_o[0], precise=True)
    wkv_s, shift_s = s_fin[None], h_last_s[None]
    x_s = x_sample + y
    x_s = x_s + swiglu(rmsnorm(x_s, norm_ffn[0]), ffn_w_gu[0], ffn_w_down[0], precise=True)
    cmp_kv_s, slc_kv_s, win_new = shared_kv_rows(x_s, norm_kv, nsa_w_kv, pos_s, precise=True)
    kc_s, vc_s, win_all = sample_kv_context(cmp_kv_s, slc_kv_s, win_new, cache_cmp_kv, cache_slc_kv, cache_win_kv,
                                            page_table, *cmp_params)
    x_s = x_s + sample_nsa(rmsnorm(x_s, norm_mix[1]), kc_s, vc_s, slc_kv_s, win_new, cache_slc_kv, cache_win_kv,
                           page_table, nsa_w_in[0], nsa_w_o[0])
    win_kv_s = win_all[:, -cache_win_kv.shape[1]:]
    x_s = x_s.reshape(-1, D)
    h_s = rmsnorm(x_s, norm_ffn[1])
    logits_s = matmul_precise(h_s, moe_router[0])

    n_tok_p = n_p * T
    logits = jnp.concatenate([logits_p[:, :N_EXPERTS], logits_s], axis=0)
    y0, y1, gate = moe_swiglu(jnp.concatenate([h_p, h_s.astype(jnp.bfloat16)], axis=0), logits,
                              moe_gu, moe_down)
    gate = jnp.pad(gate, ((0, 0), (0, 128 - TOP_K)))
    y_prompt = combine_norm(x_p, y0[:n_tok_p], y1[:n_tok_p], gate[:n_tok_p], norm_final).reshape(x_prompt.shape)
    y_sample = combine_norm(x_s, y0[n_tok_p:], y1[n_tok_p:], gate[n_tok_p:], norm_final).reshape(x_sample.shape)


    return (y_prompt, y_sample, wkv_p, shift_p, cmp_kv_p, slc_kv_p, win_kv_p,
            wkv_s, shift_s, cmp_kv_s, slc_kv_s, win_kv_s)
```

```python
import functools
import math

import jax
import jax.numpy as jnp
from jax import lax
from jax.experimental import pallas as pl
from jax.experimental.pallas import tpu as pltpu

D_MODEL = 1024
DEPTH = 2
PAST_LEN = 16384
PAGE_SIZE = 128
N_A_LAYERS = DEPTH // 2
RWKV_HEAD = 64
RWKV_HEADS = D_MODEL // RWKV_HEAD
DECAY_SCALE = math.exp(-0.5)
GN_EPS = RWKV_HEAD * 1e-5
NSA_HEADS = 16
NSA_HEAD_DIM = 64
NSA_KV_GROUPS = 2
NSA_GROUP_HEADS = NSA_HEADS // NSA_KV_GROUPS
N_BRANCH = 3
CMP_BLOCK = 32
CMP_STRIDE = 16
CMP_HIDDEN = 128
SLC_BLOCK = 64
SLC_TOPN = 16
WINDOW = 512
Q_BLOCK = 128
FORCED_SCORE = 1e4
ROPE_THETA = 500000.0
ROT_DIM = NSA_HEAD_DIM // 4
N_EXPERTS = 8
TOP_K = 2
NORM_EPS = 1e-6


def _rmsnorm_kernel(x_ref, g_ref, o_ref):
    x = x_ref[...]
    y = x * lax.rsqrt(jnp.mean(x * x, axis=-1, keepdims=True) + NORM_EPS)
    o_ref[...] = y * g_ref[...]


def rmsnorm(x, g):
    shp = x.shape
    x2 = x.reshape(-1, shp[-1])
    rows = x2.shape[0]
    tm = min(rows, 512)
    out = pl.pallas_call(
        _rmsnorm_kernel,
        grid=(rows // tm,),
        in_specs=[pl.BlockSpec((tm, shp[-1]), lambda i: (i, 0)),
                  pl.BlockSpec((1, shp[-1]), lambda i: (0, 0))],
        out_specs=pl.BlockSpec((tm, shp[-1]), lambda i: (i, 0)),
        out_shape=jax.ShapeDtypeStruct(x2.shape, x.dtype),
        name="rmsnorm",
    )(x2, g.reshape(1, -1))
    return out.reshape(shp)


def rope_partial(x, pos):
    half = ROT_DIM // 2
    inv = ROPE_THETA ** (-2.0 * jnp.arange(half, dtype=jnp.float32) / ROT_DIM)
    ang = pos.astype(jnp.float32)[:, None] * inv[None, :]
    shape = (1, pos.shape[0]) + (1,) * (x.ndim - 3) + (half,)
    cos = jnp.cos(ang).reshape(shape)
    sin = jnp.sin(ang).reshape(shape)
    xf = x.astype(jnp.float32)
    x1, x2 = xf[..., :half], xf[..., half:ROT_DIM]
    out = jnp.concatenate([x1 * cos - x2 * sin, x2 * cos + x1 * sin, xf[..., ROT_DIM:]], axis=-1)
    return out.astype(x.dtype)


def masked_softmax(s, mask):
    s = jnp.where(mask, s.astype(jnp.float32), -jnp.inf)
    m = jnp.max(s, axis=-1, keepdims=True)
    m = jnp.where(jnp.isfinite(m), m, 0.0)
    e = jnp.where(mask, jnp.exp(s - m), 0.0)
    return e / jnp.maximum(jnp.sum(e, axis=-1, keepdims=True), 1e-30)


FF_CHUNK = 1408
SWIGLU_ROWS = 512
MOE_PARTS = 8


def _swiglu_kernel(blk_e_ref, n_used_ref, x_ref, wg_ref, wu_ref, wd_ref, *rest, has_res):
    o_ref = rest[-1]
    i = pl.program_id(0)
    f = pl.program_id(1)

    @pl.when(i < n_used_ref[0])
    def _():
        x = x_ref[...].astype(jnp.bfloat16)
        g = jnp.dot(x, wg_ref[0], preferred_element_type=jnp.float32)
        u = jnp.dot(x, wu_ref[0], preferred_element_type=jnp.float32)
        act = (g * jax.nn.sigmoid(g) * u).astype(jnp.bfloat16)
        y = jnp.dot(act, wd_ref[0], preferred_element_type=jnp.float32)

        @pl.when(f == 0)
        def _():
            o_ref[...] = y + rest[0][...] if has_res else y

        @pl.when(f > 0)
        def _():
            o_ref[...] += y

    @pl.when(i >= n_used_ref[0])
    def _():
        o_ref[...] = jnp.zeros(o_ref.shape, o_ref.dtype)


def grouped_swiglu(xb, blk_e, n_used, w_gu, w_down, res=None, into=None):
    rows, d = xb.shape
    b = min(SWIGLU_ROWS, rows)
    n_blk = rows // b
    ff = w_down.shape[1]
    tf = FF_CHUNK
    n_f = ff // tf
    chunk = lambda i, f, be, nu: jnp.where(i < nu[0], f, n_f - 1)
    in_specs = [
        pl.BlockSpec((b, d), lambda i, f, be, nu: (i, 0)),
        pl.BlockSpec((1, d, tf), lambda i, f, be, nu: (be[i], 0, chunk(i, f, be, nu))),
        pl.BlockSpec((1, d, tf), lambda i, f, be, nu: (be[i], 0, n_f + chunk(i, f, be, nu))),
        pl.BlockSpec((1, tf, d), lambda i, f, be, nu: (be[i], chunk(i, f, be, nu), 0)),
    ]
    operands = [xb, w_gu, w_gu, w_down]
    if res is not None:
        in_specs.append(pl.BlockSpec((b, d), lambda i, f, be, nu: (i, 0)))
        operands.append(res)
    out_rows, first_blk, aliases = rows, 0, {}
    if into is not None:
        prev, first_blk, out_rows = into
        if prev is not None:
            in_specs.append(pl.BlockSpec(memory_space=pl.ANY))
            aliases = {2 + len(operands): 0}
            operands.append(prev)
    return pl.pallas_call(
        functools.partial(_swiglu_kernel, has_res=res is not None),
        grid_spec=pltpu.PrefetchScalarGridSpec(
            num_scalar_prefetch=2,
            grid=(n_blk, n_f),
            in_specs=in_specs,
            out_specs=pl.BlockSpec((b, d), lambda i, f, be, nu: (i + first_blk, 0)),
        ),
        out_shape=jax.ShapeDtypeStruct((out_rows, d), jnp.float32),
        input_output_aliases=aliases,
        compiler_params=pltpu.CompilerParams(dimension_semantics=("arbitrary", "arbitrary"),
                                             vmem_limit_bytes=56 * 1024 * 1024),
        name="grouped_swiglu",
    )(blk_e.astype(jnp.int32), jnp.reshape(n_used, (1,)).astype(jnp.int32), *operands)


def swiglu(h, w_gu, w_down, precise=False):
    g, u = jnp.split(dense(h, w_gu, precise), 2, axis=-1)
    return dense(jax.nn.silu(g) * u, w_down, precise)


def swiglu_residual(x, h, w_gu, w_down):
    n_blk = h.shape[0] // min(SWIGLU_ROWS, h.shape[0])
    return grouped_swiglu(h, jnp.zeros((n_blk,), jnp.int32), jnp.int32(n_blk),
                          w_gu.astype(jnp.bfloat16)[None], w_down.astype(jnp.bfloat16)[None], res=x)


def _combine_norm_kernel(x_ref, y0_ref, y1_ref, g_ref, gain_ref, o_ref):
    g = g_ref[...]
    x = x_ref[...] + (y0_ref[...] * g[:, 0:1] + y1_ref[...] * g[:, 1:2])
    o_ref[...] = _rms(x) * gain_ref[...]


def combine_norm(x, y0, y1, gate, gain):
    m, d = x.shape
    tm = min(ROW_TILE, m)
    rows = lambda width: pl.BlockSpec((tm, width), lambda i: (i, 0))
    return pl.pallas_call(
        _combine_norm_kernel,
        grid=(m // tm,),
        in_specs=[rows(d), rows(d), rows(d), rows(gate.shape[1]), pl.BlockSpec((1, d), lambda i: (0, 0))],
        out_specs=rows(d),
        out_shape=jax.ShapeDtypeStruct((m, d), jnp.float32),
        compiler_params=pltpu.CompilerParams(dimension_semantics=("parallel",)),
        name="combine_norm",
    )(x, y0, y1, gate, gain.reshape(1, d))


def moe_swiglu(xt, logits, w_gu_e, w_down_e):
    n_tok, d = xt.shape
    b = SWIGLU_ROWS
    experts = jnp.arange(N_EXPERTS)[None, :]
    e0 = jnp.argmax(logits, axis=-1)
    v0 = jnp.max(logits, axis=-1)
    rest = jnp.where(experts == e0[:, None], -jnp.inf, logits)
    e1 = jnp.argmax(rest, axis=-1)
    v1 = jnp.max(rest, axis=-1)
    top_idx = jnp.stack([e0, e1], axis=-1).astype(jnp.int32)
    gate = jax.nn.softmax(jnp.stack([v0, v1], axis=-1), axis=-1)
    nk = n_tok * TOP_K
    flat_e = top_idx.reshape(nk)
    onehot = (flat_e[:, None] == jnp.arange(N_EXPERTS)[None, :]).astype(jnp.int32)
    before = jnp.cumsum(onehot, axis=0) - onehot
    counts = jnp.sum(onehot, axis=0)
    padded = (counts + b - 1) // b * b
    ends_pad = jnp.cumsum(padded)
    starts_pad = ends_pad - padded
    dest = jnp.sum(onehot * (starts_pad[None, :] + before), axis=1)
    part_rows = MOE_PARTS * b
    n_rows = -(-((nk + b - 1) // b * b + N_EXPERTS * b) // part_rows) * part_rows
    n_blk = n_rows // b
    flat_tok = jnp.repeat(jnp.arange(n_tok, dtype=jnp.int32), TOP_K)
    row_tok = jnp.full((n_rows,), n_tok, jnp.int32).at[dest].set(flat_tok, unique_indices=True)
    blk_e = jnp.minimum(jnp.searchsorted(ends_pad, jnp.arange(n_blk) * b, side='right'), N_EXPERTS - 1)
    x_pad = jnp.concatenate([xt.astype(jnp.float32), jnp.zeros((1, d), jnp.float32)], axis=0)
    w_gu_b, w_down_b = w_gu_e.astype(jnp.bfloat16), w_down_e.astype(jnp.bfloat16)
    n_used = ends_pad[-1] // b
    per_part = n_blk // MOE_PARTS
    yb = None
    for i in range(MOE_PARTS):
        rows = x_pad[row_tok[i * per_part * b:(i + 1) * per_part * b]]
        yb = grouped_swiglu(rows, blk_e[i * per_part:(i + 1) * per_part],
                            jnp.clip(n_used - i * per_part, 0, per_part), w_gu_b, w_down_b,
                            into=(yb, i * per_part, n_rows))
    dest = dest.reshape(n_tok, TOP_K)
    return yb[dest[:, 0]], yb[dest[:, 1]], gate


WKV_CHUNK = 64
WKV_HEADS_PER_STEP = 16
WKV_SEQS_PER_STEP = 2

_NN = (((1,), (0,)), ((), ()))
_NT = (((1,), (1,)), ((), ()))
_TN = (((0,), (0,)), ((), ()))


def _mm(x, y, dims):
    return lax.dot_general(x.astype(jnp.bfloat16), y.astype(jnp.bfloat16), dims,
                           preferred_element_type=jnp.float32)


def _wkv7_chunk_kernel(r_ref, lw_ref, k_ref, v_ref, kk_ref, a_ref, rk_ref, gnw_ref, gnb_ref, s0_ref,
                       y_ref, sout_ref, state_ref):
    c = pl.program_id(2)
    n_seq, L = r_ref.shape[:2]
    n_pairs = state_ref.shape[0]
    per_seq = n_pairs // n_seq
    N = RWKV_HEAD
    W = 2 * N
    lanes = lambda ref: jnp.concatenate([ref[i] for i in range(n_seq)], axis=-1)
    tiled = lambda ref: jnp.concatenate([ref[...]] * n_seq, axis=-1)

    def block_diag(top, bottom):
        z = jnp.zeros((N, N), jnp.float32)
        return jnp.concatenate([jnp.concatenate([top, z], axis=1), jnp.concatenate([z, bottom], axis=1)], axis=0)

    @pl.when(c == 0)
    def _():
        for p in range(n_pairs):
            b, q = p // per_seq, p % per_seq
            state_ref[p] = block_diag(s0_ref[b, 2 * q], s0_ref[b, 2 * q + 1])

    row2 = lax.broadcasted_iota(jnp.int32, (2 * L, W), 0)
    lane2 = lax.broadcasted_iota(jnp.int32, (2 * L, W), 1)
    own_lanes = (row2 // L) == (lane2 // N)
    rr = lax.broadcasted_iota(jnp.int32, (2 * L, 2 * L), 0)
    cc = lax.broadcasted_iota(jnp.int32, (2 * L, 2 * L), 1)
    same = (rr // L) == (cc // L)
    strict = same & (rr % L > cc % L)
    incl = same & (rr % L >= cc % L)
    wr = lax.broadcasted_iota(jnp.int32, (W, W), 0)
    wc = lax.broadcasted_iota(jnp.int32, (W, W), 1)
    eye_w = wr == wc
    ones_bd = ((wr // N) == (wc // N)).astype(jnp.bfloat16)
    tl = lax.broadcasted_iota(jnp.int32, (L, L), 0) >= lax.broadcasted_iota(jnp.int32, (L, L), 1)
    tri = tl.astype(jnp.bfloat16)

    def head_sum(x):
        hi, lo = _split_bf16(x)
        return (jnp.dot(hi, ones_bd, preferred_element_type=jnp.float32)
                + jnp.dot(lo, ones_bd, preferred_element_type=jnp.float32))

    def stack(x):
        return jnp.where(own_lanes, jnp.concatenate([x, x], axis=0), 0.0)

    unstack = lambda x: x[:L] + x[L:]

    lw = lanes(lw_ref)
    lw_hi, lw_lo = _split_bf16(lw)
    cum = (lax.dot_general(tri, lw_hi, _NN, preferred_element_type=jnp.float32)
           + lax.dot_general(tri, lw_lo, _NN, preferred_element_type=jnp.float32))
    cum_last = cum[L - 1:L, :]
    e_neg_all = jnp.exp(-cum)
    e_tail_all = jnp.exp(cum_last - cum)
    e_prev_all = jnp.exp(cum - lw)
    e_cum_all = jnp.exp(cum)
    wl_all = jnp.exp(cum_last)

    pairs = range(n_pairs)
    slab = lambda t, p: t[:, p * W:(p + 1) * W]
    r_all, k_all, v_all, kk_all, a_all = (lanes(ref) for ref in (r_ref, k_ref, v_ref, kk_ref, a_ref))
    r_in = [slab(r_all, p) for p in pairs]
    k_in = [slab(k_all, p) for p in pairs]
    v_in = [slab(v_all, p) for p in pairs]
    ssq = [head_sum(slab(kk_all, p) * slab(kk_all, p)) for p in pairs]
    at_st, bt_st, bh_st, rt_st, kt_st, kh_st, v_st = [], [], [], [], [], [], []
    for p in pairs:
        kk = slab(kk_all, p) / jnp.maximum(jnp.sqrt(ssq[p]), 1e-12)
        b = kk * slab(a_all, p)
        at_st.append(stack(-kk * slab(e_prev_all, p)))
        bt_st.append(stack(b * slab(e_neg_all, p)))
        bh_st.append(stack(b * slab(e_tail_all, p)))
        rt_st.append(stack(r_in[p] * slab(e_cum_all, p)))
        kt_st.append(stack(k_in[p] * slab(e_neg_all, p)))
        kh_st.append(stack(k_in[p] * slab(e_tail_all, p)))
        v_st.append(stack(v_in[p]))
    a_ab = [jnp.where(strict, _mm(at_st[p], bt_st[p], _NT), 0.0) for p in pairs]
    a_ak = [jnp.where(strict, _mm(at_st[p], kt_st[p], _NT), 0.0) for p in pairs]
    r_b = [jnp.where(incl, _mm(rt_st[p], bt_st[p], _NT), 0.0) for p in pairs]
    r_k = [jnp.where(incl, _mm(rt_st[p], kt_st[p], _NT), 0.0) for p in pairs]
    av = [_mm(a_ak[p], v_st[p], _NN) for p in pairs]
    ht = [_mm(v_st[p], kh_st[p], _TN) for p in pairs]
    yp = [_mm(r_k[p], v_st[p], _NN) for p in pairs]
    pw = a_ab
    inv_a = a_ab
    n = 1
    while 2 * n < L:
        pw = [_mm(pw[p], pw[p], _NN) for p in pairs]
        inv_a = [inv_a[p] + pw[p] + _mm(inv_a[p], pw[p], _NN) for p in pairs]
        n *= 2
    ap = [at_st[p] + _mm(inv_a[p], at_st[p], _NN) for p in pairs]
    vp = [av[p] + _mm(inv_a[p], av[p], _NN) for p in pairs]
    g = [jnp.where(eye_w, slab(wl_all, p), 0.0) + _mm(bh_st[p], ap[p], _TN) for p in pairs]
    ht = [ht[p] + _mm(vp[p], bh_st[p], _TN) for p in pairs]
    rp = [unstack(rt_st[p] + _mm(r_b[p], ap[p], _NN)) for p in pairs]
    yp = [unstack(yp[p] + _mm(r_b[p], vp[p], _NN)) for p in pairs]
    s_prev = [state_ref[p] for p in pairs]
    y = [_mm(rp[p], s_prev[p], _NT) + yp[p] for p in pairs]
    for p in pairs:
        state_ref[p] = _mm(s_prev[p], g[p], _NT) + ht[p]
    mean = [head_sum(y[p]) * (1.0 / N) for p in pairs]
    cen = [y[p] - mean[p] for p in pairs]
    var = [head_sum(cen[p] * cen[p]) * (1.0 / N) for p in pairs]
    rk, gnw, gnb = tiled(rk_ref), tiled(gnw_ref), tiled(gnb_ref)
    bonus = [head_sum(r_in[p] * k_in[p] * slab(rk, p)) * v_in[p] for p in pairs]
    out = [cen[p] * lax.rsqrt(var[p] + GN_EPS) * slab(gnw, p) + slab(gnb, p) + bonus[p] for p in pairs]
    for b in range(n_seq):
        y_ref[b] = jnp.concatenate(out[b * per_seq:(b + 1) * per_seq], axis=-1)

    @pl.when(c == pl.num_programs(2) - 1)
    def _():
        for p in pairs:
            b, q = p // per_seq, p % per_seq
            s = state_ref[p]
            sout_ref[b, 2 * q] = s[:N, :N]
            sout_ref[b, 2 * q + 1] = s[N:, N:]


def wkv7_chunked(r, lw, k, v, kk, a_gate, r_k, gn_w, gn_b, s0):
    n, T, D = r.shape
    L = WKV_CHUNK
    hb = WKV_HEADS_PER_STEP
    w = hb * RWKV_HEAD
    nb = WKV_SEQS_PER_STEP if n % WKV_SEQS_PER_STEP == 0 else 1
    seq = pl.BlockSpec((nb, L, w), lambda b, h, c: (b, c, h))
    vec = pl.BlockSpec((1, w), lambda b, h, c: (0, h))
    st = pl.BlockSpec((nb, hb, RWKV_HEAD, RWKV_HEAD), lambda b, h, c: (b, h, 0, 0))
    row = lambda t: t.reshape(1, D).astype(jnp.float32)
    return pl.pallas_call(
        _wkv7_chunk_kernel,
        grid=(n // nb, D // w, T // L),
        in_specs=[seq] * 6 + [vec] * 3 + [st],
        out_specs=[seq, st],
        out_shape=[jax.ShapeDtypeStruct((n, T, D), jnp.float32),
                   jax.ShapeDtypeStruct(s0.shape, jnp.float32)],
        scratch_shapes=[pltpu.VMEM((nb * hb // 2, 2 * RWKV_HEAD, 2 * RWKV_HEAD), jnp.float32)],
        compiler_params=pltpu.CompilerParams(dimension_semantics=("parallel", "parallel", "arbitrary")),
        name="wkv7_chunked",
    )(r, lw, k, v, kk, a_gate, row(r_k), row(gn_w), row(gn_b), s0)


def wkv7_scan(r, lw, k, v, a_vec, b_vec, s0):
    def step(S, inp):
        r_t, lw_t, k_t, v_t, a_t, b_t = inp
        sa = jnp.sum(S * a_t[:, :, None, :], axis=-1)
        S = S * jnp.exp(lw_t)[:, :, None, :] + sa[..., None] * b_t[:, :, None, :] + v_t[..., None] * k_t[:, :, None, :]
        return S, jnp.sum(S * r_t[:, :, None, :], axis=-1)

    xs = tuple(jnp.moveaxis(t, 1, 0) for t in (r, lw, k, v, a_vec, b_vec))
    s_fin, ys = lax.scan(step, s0, xs)
    return jnp.moveaxis(ys, 0, 1), s_fin


ROW_TILE = 512


def _bdot(x, w):
    return jnp.dot(x.astype(jnp.bfloat16), w, preferred_element_type=jnp.float32)


def _rms(x):
    return x * lax.rsqrt(jnp.mean(x * x, axis=-1, keepdims=True) + NORM_EPS)


def _rwkv_pre_kernel(x_ref, shift_ref, gain_ref, mu_ref, vec_ref, wrkv_ref, w1_ref, a1_ref, g1_ref,
                     w2_ref, a2_ref, g2_ref,
                     r_ref, lw_ref, k_ref, v_ref, kk_ref, a_ref, g_ref, hlast_ref, prev_ref):
    @pl.when(pl.program_id(1) == 0)
    def _():
        prev_ref[...] = shift_ref[0]

    tm = x_ref.shape[1]
    h = _rms(x_ref[0]) * gain_ref[...]
    first_row = lax.broadcasted_iota(jnp.int32, h.shape, 0) == 0
    h_prev = jnp.where(first_row, prev_ref[...], pltpu.roll(h, 1, 0))
    prev_ref[...] = h[tm - 1:tm, :]
    hlast_ref[0] = h[tm - 1:tm, :]
    dx = h_prev - h
    mix = lambda i: (h + dx * mu_ref[i:i + 1, :]).astype(jnp.bfloat16)
    w0, a0, k_k, k_a = (vec_ref[i:i + 1, :] for i in range(4))
    r_ref[0] = _bdot(mix(0), wrkv_ref[0])
    k = _bdot(mix(2), wrkv_ref[1])
    v_ref[0] = _bdot(mix(3), wrkv_ref[2])
    lw_ref[0] = -DECAY_SCALE * jax.nn.sigmoid(w0 + _bdot(jnp.tanh(_bdot(mix(1), w1_ref[...])), w2_ref[...]))
    a = jax.nn.sigmoid(a0 + _bdot(_bdot(mix(4), a1_ref[...]), a2_ref[...]))
    g_ref[0] = _bdot(jax.nn.sigmoid(_bdot(mix(5), g1_ref[...])), g2_ref[...])
    a_ref[0] = a
    kk_ref[0] = k * k_k
    k_ref[0] = k * (1.0 + (a - 1.0) * k_a)


def _out_proj_kernel(*refs, n_terms, gated, routed):
    terms = refs[:n_terms]
    rest = refs[n_terms:]
    if gated:
        gate_ref, rest = rest[0], rest[1:]
    x_ref, w_ref, gain_ref = rest[:3]
    rest = rest[3:]
    if routed:
        wr_ref, rest = rest[0], rest[1:]
    xo_ref, ho_ref = rest[:2]
    y = terms[0][...].astype(jnp.float32)
    for t in terms[1:]:
        y = y + t[...].astype(jnp.float32)
    if gated:
        y = y * gate_ref[...]
    xo = x_ref[...] + _bdot(y, w_ref[...])
    xo_ref[...] = xo
    h = _rms(xo) * gain_ref[...]
    ho_ref[...] = h.astype(ho_ref.dtype)
    if routed:
        h_hi, h_lo = _split_bf16(h)
        w_hi, w_lo = _split_bf16(wr_ref[...])
        dot = functools.partial(jnp.dot, preferred_element_type=jnp.float32)
        rest[2][...] = dot(h_hi, w_hi) + (dot(h_hi, w_lo) + dot(h_lo, w_hi))


def out_proj(terms, gate, x, w, gain, w_router=None):
    m, d = x.shape
    kdim = w.shape[0]
    tm = min(ROW_TILE, m)
    rows = lambda width: pl.BlockSpec((tm, width), lambda i: (i, 0))
    full = lambda a: pl.BlockSpec(a.shape, lambda i: (0, 0))
    ins = list(terms) + ([gate] if gate is not None else [])
    consts = [w.astype(jnp.bfloat16), gain.reshape(1, d)]
    out_specs = [rows(d), rows(d)]
    out_shape = [jax.ShapeDtypeStruct((m, d), jnp.float32), jax.ShapeDtypeStruct((m, d), jnp.bfloat16)]
    if w_router is not None:
        consts.append(jnp.pad(w_router, ((0, 0), (0, 128 - w_router.shape[1]))))
        out_specs.append(rows(128))
        out_shape.append(jax.ShapeDtypeStruct((m, 128), jnp.float32))
    return pl.pallas_call(
        functools.partial(_out_proj_kernel, n_terms=len(terms), gated=gate is not None,
                          routed=w_router is not None),
        grid=(m // tm,),
        in_specs=[rows(kdim)] * len(ins) + [rows(d)] + [full(a) for a in consts],
        out_specs=out_specs,
        out_shape=out_shape,
        compiler_params=pltpu.CompilerParams(dimension_semantics=("parallel",),
                                             vmem_limit_bytes=48 * 1024 * 1024),
        name="out_proj",
    )(*ins, x, *consts)


def rwkv7_layer(x, shift0, s0, gain, mu, w_rkv, w0, w1, w2, a0, a1, a2, g1, g2, k_k, k_a, r_k, gn_w, gn_b, w_o,
                gain_next):
    n, T, D = x.shape
    tm = min(ROW_TILE, T)
    bf = lambda t: t.astype(jnp.bfloat16)
    seq = pl.BlockSpec((1, tm, D), lambda b, t: (b, t, 0))
    full = lambda a: pl.BlockSpec(a.shape, lambda b, t: (0,) * a.ndim)
    per_seq = pl.BlockSpec((1, 1, D), lambda b, t: (b, 0, 0))
    vecs = jnp.stack([w0, a0, k_k, k_a]).astype(jnp.float32)
    weights = [bf(w_rkv), bf(w1), bf(a1), bf(g1), bf(w2), bf(a2), bf(g2)]
    small = [gain.reshape(1, D), mu, vecs]
    outs = pl.pallas_call(
        _rwkv_pre_kernel,
        grid=(n, T // tm),
        in_specs=[seq, per_seq] + [full(a) for a in small + weights],
        out_specs=[seq] * 7 + [per_seq],
        out_shape=[jax.ShapeDtypeStruct((n, T, D), jnp.float32)] * 7
                  + [jax.ShapeDtypeStruct((n, 1, D), jnp.float32)],
        scratch_shapes=[pltpu.VMEM((1, D), jnp.float32)],
        compiler_params=pltpu.CompilerParams(dimension_semantics=("parallel", "arbitrary"),
                                             vmem_limit_bytes=56 * 1024 * 1024),
        name="rwkv_pre",
    )(x, shift0.reshape(n, 1, D), *small, *weights)
    r, lw, k, v, kk, a_gate, g, h_last = outs
    y, s_fin = wkv7_chunked(r, lw, k, v, kk, a_gate, r_k.reshape(-1), gn_w, gn_b, s0.astype(jnp.float32))
    flat = lambda t: t.reshape(n * T, D)
    x1, h1 = out_proj([flat(y)], flat(g), flat(x), w_o, gain_next)
    return x1, h1, s_fin, h_last.reshape(n, D)


def rwkv7_time_mix(h, h_prev, s0, mu, w_rkv, w0, w1, w2, a0, a1, a2, g1, g2, k_k, k_a, r_k, gn_w, gn_b, w_o,
                   precise=False):
    n, T, D = h.shape
    f32 = jnp.float32
    mm = functools.partial(dense, precise=precise)
    dx = jnp.concatenate([h_prev[:, None, :].astype(h.dtype), h[:, :-1]], axis=1) - h
    xr, xw, xk, xv, xa, xg = (h + dx * mu[i] for i in range(6))
    r = mm(xr, w_rkv[0])
    k = mm(xk, w_rkv[1])
    v = mm(xv, w_rkv[2])
    log_decay = -DECAY_SCALE * jax.nn.sigmoid((w0 + mm(jnp.tanh(mm(xw, w1)), w2)).astype(f32))
    a = jax.nn.sigmoid((a0 + mm(mm(xa, a1), a2)).astype(f32))
    g = mm(jax.nn.sigmoid(mm(xg, g1)), g2)
    heads = lambda t: t.astype(f32).reshape(n, T, RWKV_HEADS, RWKV_HEAD)
    kk = heads(k * k_k)
    kk = kk / jnp.maximum(jnp.sqrt(jnp.sum(kk * kk, axis=-1, keepdims=True)), 1e-12)
    a_h = heads(a)
    k_h = heads(k.astype(f32) * (1.0 + (a - 1.0) * k_a.astype(f32)))
    r_h, v_h = heads(r), heads(v)
    y, s_fin = wkv7_scan(r_h, heads(log_decay), k_h, v_h, -kk, kk * a_h, s0.astype(f32))
    mean = jnp.mean(y, axis=-1, keepdims=True)
    var = jnp.mean(jnp.square(y - mean), axis=-1, keepdims=True)
    y = ((y - mean) * lax.rsqrt(var + GN_EPS)).reshape(n, T, D) * gn_w.astype(f32) + gn_b.astype(f32)
    bonus = jnp.sum(r_h * k_h * r_k.astype(f32), axis=-1, keepdims=True) * v_h
    y = (y + bonus.reshape(n, T, D)).astype(h.dtype)
    return mm(y * g, w_o), s_fin, h[:, -1]


def shared_kv_rows(x, norm_kv, w_kv, pos, precise=False):
    n, T, _ = x.shape
    kv = dense(rmsnorm(x, norm_kv), w_kv, precise).reshape(n, T, N_BRANCH, 2, NSA_KV_GROUPS, NSA_HEAD_DIM)
    cmp_kv = kv[:, :, 0]
    slc_kv = jnp.stack([rope_partial(kv[:, :, 1, 0], pos), kv[:, :, 1, 1]], axis=2)
    win_kv = jnp.stack([rope_partial(kv[:, :, 2, 0], pos), kv[:, :, 2, 1]], axis=2)
    return cmp_kv, slc_kv, win_kv


CHUNK_LANES = CMP_STRIDE * 2 * NSA_KV_GROUPS * NSA_HEAD_DIM
HIDDEN_LANES = 2 * NSA_KV_GROUPS * 2 * CMP_HIDDEN
PAGES_PER_STEP = 32
CHUNKS_PER_PAGE = PAGE_SIZE // CMP_STRIDE


def _chunk_weights(cmp_w1):
    w = cmp_w1.reshape(2, 2, CMP_STRIDE, NSA_HEAD_DIM, CMP_HIDDEN)
    w = w.transpose(2, 0, 3, 1, 4)
    eye = jnp.eye(2, dtype=w.dtype)
    big = jnp.einsum('ab,cf,jaehk->jacebfhk', eye, jnp.eye(NSA_KV_GROUPS, dtype=w.dtype), w)
    return big.reshape(CHUNK_LANES, HIDDEN_LANES).astype(jnp.bfloat16)


def _token_weights(cmp_w1):
    w = cmp_w1.reshape(2, 2, CMP_STRIDE, NSA_HEAD_DIM, CMP_HIDDEN)
    return w.transpose(2, 0, 3, 1, 4).reshape(CMP_STRIDE, 2, NSA_HEAD_DIM, 2 * CMP_HIDDEN).astype(jnp.bfloat16)


def _paged_chunk_kernel(pt_ref, *refs):
    pages, w_ref, o_ref, xt_ref = (refs[:PAGES_PER_STEP], refs[PAGES_PER_STEP], refs[PAGES_PER_STEP + 1],
                                   refs[PAGES_PER_STEP + 2])
    planes = [(kv, g) for kv in range(2) for g in range(NSA_KV_GROUPS)]
    tok = lax.broadcasted_iota(jnp.int32, (PAGE_SIZE, PAGE_SIZE), 0)
    dst = lax.broadcasted_iota(jnp.int32, (PAGE_SIZE, PAGE_SIZE), 1)
    regroup = (tok == (dst % CHUNKS_PER_PAGE) * CMP_STRIDE + dst // CHUNKS_PER_PAGE).astype(jnp.bfloat16)
    for p in range(PAGES_PER_STEP):
        for i, (kv, g) in enumerate(planes):
            xt_ref[p, i] = _bdot(pages[p][0, kv, g], regroup).T
    width = 2 * CMP_HIDDEN
    for i, (kv, g) in enumerate(planes):
        acc = None
        for j in range(CMP_STRIDE):
            rows = [xt_ref[p, i, j * CHUNKS_PER_PAGE:(j + 1) * CHUNKS_PER_PAGE, :] for p in range(PAGES_PER_STEP)]
            y = _bdot(jnp.concatenate(rows, axis=0), w_ref[j, kv])
            acc = y if acc is None else acc + y
        o_ref[0, :, i * width:(i + 1) * width] = acc


def paged_chunk_hidden(cache, page_table, w_token, extra_rows=0):
    n, n_pages = page_table.shape
    planes = cache.transpose(0, 2, 3, 4, 1)
    rows = PAGES_PER_STEP * CHUNKS_PER_PAGE

    def page_spec(k):
        return pl.BlockSpec((1,) + planes.shape[1:], lambda b, s, pt: (pt[b, s * PAGES_PER_STEP + k], 0, 0, 0, 0))

    return pl.pallas_call(
        _paged_chunk_kernel,
        grid_spec=pltpu.PrefetchScalarGridSpec(
            num_scalar_prefetch=1,
            grid=(n, n_pages // PAGES_PER_STEP),
            in_specs=[page_spec(k) for k in range(PAGES_PER_STEP)]
                     + [pl.BlockSpec(w_token.shape, lambda b, s, pt: (0, 0, 0, 0))],
            out_specs=pl.BlockSpec((1, rows, HIDDEN_LANES), lambda b, s, pt: (b, s, 0)),
            scratch_shapes=[pltpu.VMEM((PAGES_PER_STEP, 2 * NSA_KV_GROUPS, PAGE_SIZE, NSA_HEAD_DIM), jnp.float32)],
        ),
        out_shape=jax.ShapeDtypeStruct((n, n_pages * CHUNKS_PER_PAGE + extra_rows, HIDDEN_LANES), jnp.float32),
        compiler_params=pltpu.CompilerParams(dimension_semantics=("parallel", "arbitrary"),
                                             vmem_limit_bytes=48 * 1024 * 1024),
        name="paged_chunk_hidden",
    )(page_table.astype(jnp.int32), *([planes] * PAGES_PER_STEP), w_token)


def _matmul_kernel(x_ref, w_ref, o_ref):
    o_ref[...] = jnp.dot(x_ref[...].astype(jnp.bfloat16), w_ref[...],
                         preferred_element_type=jnp.float32).astype(o_ref.dtype)


def matmul(x, w, rows_per_step=512, out_dtype=jnp.float32):
    m, k = x.shape
    nn = w.shape[1]
    tm = min(rows_per_step, m)
    return pl.pallas_call(
        _matmul_kernel,
        grid=(m // tm,),
        in_specs=[pl.BlockSpec((tm, k), lambda i: (i, 0)), pl.BlockSpec((k, nn), lambda i: (0, 0))],
        out_specs=pl.BlockSpec((tm, nn), lambda i: (i, 0)),
        out_shape=jax.ShapeDtypeStruct((m, nn), out_dtype),
        compiler_params=pltpu.CompilerParams(dimension_semantics=("parallel",),
                                             vmem_limit_bytes=48 * 1024 * 1024),
        name="matmul",
    )(x, w.astype(jnp.bfloat16))


def _split_bf16(x):
    hi = x.astype(jnp.bfloat16)
    return hi, (x - hi.astype(jnp.float32)).astype(jnp.bfloat16)


def _matmul3_kernel(x_ref, w_ref, o_ref):
    x_hi, x_lo = _split_bf16(x_ref[...])
    w_hi, w_lo = _split_bf16(w_ref[...])
    dot = functools.partial(jnp.dot, preferred_element_type=jnp.float32)
    o_ref[...] = dot(x_hi, w_hi) + (dot(x_hi, w_lo) + dot(x_lo, w_hi))


def matmul_precise(x, w, cols_per_step=512):
    m, k = x.shape
    nn = w.shape[1]
    pad = (-nn) % 128
    if pad:
        w = jnp.pad(w, ((0, 0), (0, pad)))
    tn = math.gcd(cols_per_step, nn + pad)
    tm = min(m, 512)
    out = pl.pallas_call(
        _matmul3_kernel,
        grid=(m // tm, (nn + pad) // tn),
        in_specs=[pl.BlockSpec((tm, k), lambda i, j: (i, 0)), pl.BlockSpec((k, tn), lambda i, j: (0, j))],
        out_specs=pl.BlockSpec((tm, tn), lambda i, j: (i, j)),
        out_shape=jax.ShapeDtypeStruct((m, nn + pad), jnp.float32),
        compiler_params=pltpu.CompilerParams(dimension_semantics=("parallel", "parallel")),
        name="matmul_precise",
    )(x, w)
    return out[:, :nn] if pad else out


def dense(x, w, precise=False):
    if precise:
        return matmul_precise(x.reshape(-1, x.shape[-1]), w).reshape(x.shape[:-1] + (w.shape[1],))
    return x @ w


def compress_from_hidden(hidden, pos_emb, w1, b1, w2, n_blocks=None):
    n, rows = hidden.shape[:2]
    C = rows if n_blocks is None else n_blocks
    assert rows % 8 == 0
    w1r = w1.reshape(2, CMP_BLOCK, NSA_HEAD_DIM, CMP_HIDDEN)
    bias = jnp.einsum('ajd,ajdh->ah', pos_emb, w1r) + b1
    out = pl.pallas_call(
        functools.partial(_compress_finish_kernel, n_blocks=C),
        grid=(n,),
        in_specs=[pl.BlockSpec((1, rows, HIDDEN_LANES), lambda b: (b, 0, 0)),
                  pl.BlockSpec(bias.shape, lambda b: (0, 0)), pl.BlockSpec(w2.shape, lambda b: (0, 0, 0))],
        out_specs=pl.BlockSpec((1, rows, 2 * SLAB), lambda b: (b, 0, 0)),
        out_shape=jax.ShapeDtypeStruct((n, rows, 2 * SLAB), jnp.float32),
        compiler_params=pltpu.CompilerParams(dimension_semantics=("parallel",)),
        name="compress_finish",
    )(hidden, bias, w2.astype(jnp.bfloat16))[:, :C]
    shape = (n, C, NSA_KV_GROUPS, NSA_HEAD_DIM)
    return out[:, :, :SLAB].reshape(shape), out[:, :, SLAB:].reshape(shape)


def _compress_finish_kernel(h_ref, bias_ref, w2_ref, o_ref, *, n_blocks):
    h = h_ref[0]
    rows = h.shape[0]
    keep = lax.broadcasted_iota(jnp.int32, (rows, CMP_HIDDEN), 0) < n_blocks - 1
    outs = []
    for i in range(2 * NSA_KV_GROUPS):
        kv = i // NSA_KV_GROUPS
        first = h[:, 2 * i * CMP_HIDDEN:(2 * i + 1) * CMP_HIDDEN]
        second = h[:, (2 * i + 1) * CMP_HIDDEN:(2 * i + 2) * CMP_HIDDEN]
        nxt = jnp.where(keep, pltpu.roll(second, rows - 1, 0), 0.0)
        act = jax.nn.gelu(first + nxt + bias_ref[kv:kv + 1, :])
        outs.append(_bdot(act, w2_ref[kv]))
    o_ref[0] = jnp.concatenate(outs, axis=-1)


def _tail_chunk_kernel(x_ref, w_ref, prev_ref, o_ref):
    o_ref[0] = _bdot(x_ref[0], w_ref[...])


def rows_chunk_hidden(kv_rows, w_chunk):
    n, T = kv_rows.shape[:2]
    chunks = kv_rows.reshape(n * T // CMP_STRIDE, CHUNK_LANES)
    return matmul(chunks, w_chunk, rows_per_step=256).reshape(n, T // CMP_STRIDE, HIDDEN_LANES)


def nsa_query(h, w_in, precise=False):
    n, T, _ = h.shape
    proj = dense(h, w_in, precise)
    q = proj[..., :NSA_HEADS * NSA_HEAD_DIM].reshape(n, T, NSA_KV_GROUPS, NSA_GROUP_HEADS, NSA_HEAD_DIM)
    gate = proj[..., NSA_HEADS * NSA_HEAD_DIM:].reshape(n, T, NSA_KV_GROUPS, NSA_GROUP_HEADS, N_BRANCH)
    return q, gate


def compressed_branch_and_selection(q, t_pos, kc, vc):
    scale = NSA_HEAD_DIM ** -0.5
    n_c = kc.shape[1]
    vis = (jnp.arange(n_c) * CMP_STRIDE + CMP_BLOCK - 1)[None, :] <= t_pos[:, None]
    p_cmp = masked_softmax(jnp.einsum('nqghd,ncgd->nghqc', q, kc) * scale, vis)
    o_cmp = jnp.einsum('nghqc,ncgd->nqghd', p_cmp.astype(vc.dtype), vc)
    ratio = SLC_BLOCK // CMP_STRIDE
    lead = CMP_BLOCK // CMP_STRIDE - 1
    n_s = n_c // ratio
    pg = jnp.pad(jnp.sum(p_cmp, axis=2), ((0, 0), (0, 0), (0, 0), (lead, 0)))
    p_slc = pg[..., 0:ratio * n_s:ratio]
    for o in range(1, ratio + lead):
        p_slc = p_slc + pg[..., o:o + ratio * n_s:ratio]
    jb = jnp.arange(n_s)[None, :]
    jt = (t_pos // SLC_BLOCK)[:, None]
    forced = (jb == 0) | (jb == jt) | (jb == jt - 1)
    score = jnp.where(jb > jt, -jnp.inf, jnp.where(forced, FORCED_SCORE, p_slc))
    before = (score[..., :, None] > score[..., None, :]) | (
        (score[..., :, None] == score[..., None, :]) & (jb[0][:, None] < jb[0][None, :]))
    rank = jnp.sum(before, axis=-2)
    slots = jnp.arange(min(SLC_TOPN, n_s))
    idx = jnp.sum(jnp.where(rank[..., None, :] == slots[:, None], jb[0], 0), axis=-1)
    return o_cmp, idx


def _dot3(x, y, dims):
    x_hi, x_lo = _split_bf16(x)
    y_hi, y_lo = _split_bf16(y)
    dot = lambda a, b: lax.dot_general(a, b, dims, preferred_element_type=jnp.float32)
    return dot(x_hi, y_hi) + (dot(x_hi, y_lo) + dot(x_lo, y_hi))


def _decode_kernel(half_ref, new_ref, phys_ref, q_ref, rows_ref, win_ref, *refs):
    b = pl.program_id(0)
    n_sel = len(refs) - 1
    pages, o_ref = refs[:n_sel], refs[n_sel]
    k_per_group = n_sel // NSA_KV_GROUPS
    hg, dh = NSA_GROUP_HEADS, NSA_HEAD_DIM
    tok_half = lax.broadcasted_iota(jnp.int32, (hg, PAGE_SIZE), 1) // SLC_BLOCK
    rows = rows_ref[0]

    def attend(qg, scores, values, k_new, v_new, new_ok):
        s_new = jnp.where(new_ok, jnp.sum(qg * k_new, axis=-1, keepdims=True), MASKED)
        m = s_new
        for s in scores:
            m = jnp.maximum(m, jnp.max(s, axis=-1, keepdims=True))
        e_new = jnp.exp(s_new - m)
        total, out = e_new, e_new * v_new
        for s, v in zip(scores, values):
            e = jnp.exp(s - m)
            total = total + jnp.sum(e, axis=-1, keepdims=True)
            out = out + _dot3(e, v, _NT)
        return out / jnp.maximum(total, 1e-30)

    outs = []
    for g in range(NSA_KV_GROUPS):
        qg = q_ref[0, g * hg:(g + 1) * hg, :]
        lanes = slice(g * dh, (g + 1) * dh)
        scores, values = [], []
        any_new = jnp.int32(0)
        for k in range(k_per_group):
            j = g * k_per_group + k
            plane = pages[j][0]
            is_new = new_ref[b, j]
            ok = (tok_half == half_ref[b, j]) & (is_new == 0)
            scores.append(jnp.where(ok, _dot3(qg, plane[0, 0], _NN), MASKED))
            values.append(plane[1, 0])
            any_new = jnp.maximum(any_new, is_new)
        o_slc = attend(qg, scores, values, rows[0:1, lanes], rows[1:2, lanes], any_new > 0)
        o_win = attend(qg, [_dot3(qg, win_ref[0, 0, g], _NN)], [win_ref[0, 1, g]],
                       rows[2:3, lanes], rows[3:4, lanes], True)
        outs.append(jnp.concatenate([o_slc, o_win], axis=-1))
    o_ref[0] = jnp.concatenate(outs, axis=0)


def decode_attention(q_rot, idx, page_table, cache_slc_kv, cache_win_kv, slc_new, win_new):
    n = q_rot.shape[0]
    sub = PAGE_SIZE // SLC_BLOCK
    n_past_blk = PAST_LEN // SLC_BLOCK
    assert cache_win_kv.shape[1] <= WINDOW and cache_win_kv.shape[1] <= PAST_LEN
    k_sel = idx.shape[-1]
    flat = idx.reshape(n, NSA_KV_GROUPS * k_sel)
    past = jnp.minimum(flat, n_past_blk - 1)
    hit = (past // sub)[:, :, None] == jnp.arange(page_table.shape[1])[None, None, :]
    phys = jnp.sum(jnp.where(hit, page_table[:, None, :], 0), axis=-1).astype(jnp.int32)
    half = (past % sub).astype(jnp.int32)
    is_new = (flat >= n_past_blk).astype(jnp.int32)
    pool = cache_slc_kv.transpose(0, 2, 3, 4, 1)
    window = cache_win_kv.transpose(0, 2, 3, 4, 1)
    rows = jnp.concatenate([slc_new.reshape(n, 2, SLAB), win_new.reshape(n, 2, SLAB)], axis=1)

    def page_spec(j):
        g = j // k_sel
        return pl.BlockSpec((1, 2, 1, NSA_HEAD_DIM, PAGE_SIZE), lambda b, hf, nw, ph: (ph[b, j], 0, g, 0, 0))

    out = pl.pallas_call(
        _decode_kernel,
        grid_spec=pltpu.PrefetchScalarGridSpec(
            num_scalar_prefetch=3,
            grid=(n,),
            in_specs=[pl.BlockSpec((1,) + q_rot.shape[1:], lambda b, hf, nw, ph: (b, 0, 0)),
                      pl.BlockSpec((1,) + rows.shape[1:], lambda b, hf, nw, ph: (b, 0, 0)),
                      pl.BlockSpec((1,) + window.shape[1:], lambda b, hf, nw, ph: (b, 0, 0, 0, 0))]
                     + [page_spec(j) for j in range(NSA_KV_GROUPS * k_sel)],
            out_specs=pl.BlockSpec((1, NSA_HEADS, 2 * NSA_HEAD_DIM), lambda b, hf, nw, ph: (b, 0, 0)),
        ),
        out_shape=jax.ShapeDtypeStruct((n, NSA_HEADS, 2 * NSA_HEAD_DIM), jnp.float32),
        compiler_params=pltpu.CompilerParams(dimension_semantics=("parallel",)),
        name="decode_attention",
    )(half, is_new, phys, q_rot, rows, window, *([pool] * (NSA_KV_GROUPS * k_sel)))
    return out[:, :, :NSA_HEAD_DIM], out[:, :, NSA_HEAD_DIM:]


NSA_SLABS = NSA_GROUP_HEADS
NSA_TILES = NSA_SLABS * NSA_KV_GROUPS
SELECTED_KEY_BLOCK = 512
DENSE_BATCH = 16
MASKED = -1e30
SLC_PER_CMP = SLC_BLOCK // CMP_STRIDE
CMP_LEAD = CMP_BLOCK // CMP_STRIDE - 1
SLAB = NSA_KV_GROUPS * NSA_HEAD_DIM
SEL_LANES = 64


def _group_tiles(q_ref, qs_ref):
    tq = q_ref.shape[1]
    lane = lax.broadcasted_iota(jnp.int32, (tq, SLAB), 1)
    for i in range(NSA_SLABS):
        qs = q_ref[0, :, i * 128:(i + 1) * 128]
        for g in range(NSA_KV_GROUPS):
            in_group = (lane >= g * NSA_HEAD_DIM) & (lane < (g + 1) * NSA_HEAD_DIM)
            qs_ref[2 * i + g] = jnp.where(in_group, qs, jnp.zeros_like(qs)).astype(jnp.bfloat16)


def _merge_groups(o_ref, tiles, gate_ref, branch):
    tq = tiles[0].shape[0]
    lane = lax.broadcasted_iota(jnp.int32, (tq, SLAB), 1)
    gates = gate_ref[0]
    col = lambda g, i: (g * NSA_GROUP_HEADS + i) * N_BRANCH + branch
    for i in range(NSA_SLABS):
        lo = tiles[2 * i] * gates[:, col(0, i):col(0, i) + 1]
        hi = tiles[2 * i + 1] * gates[:, col(1, i):col(1, i) + 1]
        o_ref[0, :, i * 128:(i + 1) * 128] = jnp.where(lane < NSA_HEAD_DIM, lo, hi).astype(o_ref.dtype)


def _nsa_cmp_kernel(q_ref, kc_ref, vc_ref, gate_ref, o_ref, sel_ref, qs_ref):
    qi = pl.program_id(1)
    tq = q_ref.shape[1]
    nc = kc_ref.shape[1]
    ns = nc // SLC_PER_CMP
    t0 = qi * tq
    _group_tiles(q_ref, qs_ref)
    kc = kc_ref[0]
    vc = vc_ref[0]
    t_row = t0 + lax.broadcasted_iota(jnp.int32, (tq, nc), 0)
    c_pos = lax.broadcasted_iota(jnp.int32, (tq, nc), 1) * CMP_STRIDE + (CMP_BLOCK - 1)
    vis = c_pos <= t_row
    tiles = range(NSA_TILES)
    s = [_mm(qs_ref[r], kc, _NT) for r in tiles]
    p = []
    for r in tiles:
        sr = jnp.where(vis, s[r], MASKED)
        m = jnp.max(sr, axis=-1, keepdims=True)
        e = jnp.where(vis, jnp.exp(sr - m), 0.0)
        p.append(e / jnp.maximum(jnp.sum(e, axis=-1, keepdims=True), 1e-30))
    _merge_groups(o_ref, [_mm(p[r], vc, _NN) for r in tiles], gate_ref, 0)

    jrow = lax.broadcasted_iota(jnp.int32, (ns, nc), 0)
    ccol = lax.broadcasted_iota(jnp.int32, (ns, nc), 1)
    pool = ((ccol >= SLC_PER_CMP * jrow - CMP_LEAD) & (ccol < SLC_PER_CMP * (jrow + 1))).astype(jnp.bfloat16)
    jb = lax.broadcasted_iota(jnp.int32, (ns, tq), 0)
    jt = (t0 + lax.broadcasted_iota(jnp.int32, (ns, tq), 1)) // SLC_BLOCK
    forced = (jb == 0) | (jb == jt) | (jb == jt - 1)
    sel_t = []
    for g in range(NSA_KV_GROUPS):
        pg = p[g]
        for i in range(1, NSA_SLABS):
            pg = pg + p[2 * i + g]
        pg_hi = pg.astype(jnp.bfloat16)
        pg_lo = (pg - pg_hi.astype(jnp.float32)).astype(jnp.bfloat16)
        p_slc = (lax.dot_general(pool, pg_hi, _NT, preferred_element_type=jnp.float32)
                 + lax.dot_general(pool, pg_lo, _NT, preferred_element_type=jnp.float32))
        score = jnp.where(jb > jt, -jnp.inf, jnp.where(forced, FORCED_SCORE, p_slc))
        rank = jnp.zeros((ns, tq), jnp.float32)
        for i in range(ns):
            row = score[i:i + 1, :]
            tie = jnp.where(jb > i, 1.0, 0.0)
            rank = rank + jnp.where(row > score, 1.0, 0.0) + jnp.where(row == score, tie, 0.0)
        sel_t.append(jnp.where(rank < min(SLC_TOPN, ns), 1.0, 0.0))
        if ns < SEL_LANES:
            sel_t.append(jnp.zeros((SEL_LANES - ns, tq), jnp.float32))
    sel_ref[0] = jnp.concatenate(sel_t, axis=0).T.astype(sel_ref.dtype)


def _nsa_dense_kernel(q_ref, k_ref, vt_ref, sel_ref, gate_ref, o_ref, qs_ref, m_ref, acc_ref, *, windowed, kblock):
    qi = pl.program_id(1)
    tq = q_ref.shape[1]
    t0 = qi * tq
    _group_tiles(q_ref, qs_ref)
    m_ref[...] = jnp.full(m_ref.shape, MASKED, jnp.float32)
    acc_ref[...] = jnp.zeros(acc_ref.shape, jnp.float32)
    k_row = lax.broadcasted_iota(jnp.int32, (kblock, tq), 0)
    t_lane = t0 + lax.broadcasted_iota(jnp.int32, (kblock, tq), 1)
    tiles = range(NSA_TILES)
    first = 0
    last = 0 if windowed else (t0 + tq - 1) // kblock

    def key_block(kb, carry):
        start = pl.multiple_of(jnp.maximum(t0 - WINDOW, 0) if windowed else kb * kblock, tq)
        kblk = k_ref[0, pl.ds(start, kblock), :]
        vt = vt_ref[0, :, pl.ds(start, kblock)]
        k_pos = start + k_row
        if windowed:
            allowed = [(k_pos <= t_lane) & (k_pos >= t_lane - WINDOW)] * NSA_KV_GROUPS
        else:
            sel = sel_ref[0]
            n_sel = NSA_KV_GROUPS * SEL_LANES
            e_key = lax.broadcasted_iota(jnp.int32, (kblock, n_sel), 0)
            e_lane = lax.broadcasted_iota(jnp.int32, (kblock, n_sel), 1)
            blk = kb * (kblock // SLC_BLOCK) + e_key // SLC_BLOCK
            allowed = []
            for g in range(NSA_KV_GROUPS):
                expand = (e_lane == blk + g * SEL_LANES).astype(jnp.bfloat16)
                picked = lax.dot_general(expand, sel, _NT, preferred_element_type=jnp.float32)
                allowed.append((picked > 0.5) & (k_pos <= t_lane))
        bias = [jnp.where(ok, 0.0, MASKED) for ok in allowed]
        d_row = lax.broadcasted_iota(jnp.int32, vt.shape, 0)
        vt_g = [jnp.where((d_row >= g * NSA_HEAD_DIM) & (d_row < (g + 1) * NSA_HEAD_DIM), vt, jnp.ones_like(vt))
                for g in range(NSA_KV_GROUPS)]
        s = [_mm(kblk, qs_ref[r], _NT) for r in tiles]
        for lo in range(0, NSA_TILES, DENSE_BATCH):
            batch = range(lo, lo + DENSE_BATCH)
            p, alpha = {}, {}
            for r in batch:
                sr = s[r] + bias[r % NSA_KV_GROUPS]
                m_prev = m_ref[r]
                m_new = jnp.maximum(m_prev, jnp.max(sr, axis=0, keepdims=True))
                alpha[r] = jnp.exp(m_prev - m_new)
                p[r] = jnp.exp(sr - m_new).astype(jnp.bfloat16)
                m_ref[r] = m_new
            pv = {r: jnp.dot(vt_g[r % NSA_KV_GROUPS], p[r], preferred_element_type=jnp.float32) for r in batch}
            for r in batch:
                acc_ref[r] = alpha[r] * acc_ref[r] + pv[r]
        return carry

    lax.fori_loop(first, last + 1, key_block, 0)
    out = []
    for r in tiles:
        acc = acc_ref[r]
        sum_row = (1 - r % NSA_KV_GROUPS) * NSA_HEAD_DIM
        out.append((acc / jnp.maximum(acc[sum_row:sum_row + 1, :], 1e-30)).T)
    _merge_groups(o_ref, out, gate_ref, 2 if windowed else 1)


def nsa_prompt_attention(q, q_rot, kc, vc, kv_att, gates):
    n, T, D = q.shape
    tq = Q_BLOCK
    qspec = pl.BlockSpec((1, tq, D), lambda b, i: (b, i, 0))
    whole = lambda a: pl.BlockSpec((1,) + a.shape[1:], lambda b, i: (b, 0, 0))
    lanes = lambda j: pl.BlockSpec((1, T, SLAB), lambda b, i: (b, 0, j))
    n_sel = NSA_KV_GROUPS * SEL_LANES
    sel_spec = pl.BlockSpec((1, tq, n_sel), lambda b, i: (b, i, 0))
    gate_spec = pl.BlockSpec((1, tq, gates.shape[2]), lambda b, i: (b, i, 0))
    params = pltpu.CompilerParams(dimension_semantics=("parallel", "arbitrary"),
                                  vmem_limit_bytes=48 * 1024 * 1024)
    qs_scratch = pltpu.VMEM((NSA_TILES, tq, SLAB), jnp.bfloat16)
    o_cmp, sel = pl.pallas_call(
        _nsa_cmp_kernel,
        grid=(n, T // tq),
        in_specs=[qspec, whole(kc), whole(vc), gate_spec],
        out_specs=[qspec, sel_spec],
        out_shape=[jax.ShapeDtypeStruct((n, T, D), jnp.bfloat16),
                   jax.ShapeDtypeStruct((n, T, n_sel), jnp.bfloat16)],
        scratch_shapes=[qs_scratch],
        compiler_params=params,
        name="nsa_cmp_select",
    )(q, kc, vc, gates)
    stat = pltpu.VMEM((NSA_TILES, 1, tq), jnp.float32)
    acc = pltpu.VMEM((NSA_TILES, SLAB, tq), jnp.float32)
    v_t = jnp.swapaxes(jnp.concatenate([kv_att[:, :, SLAB:2 * SLAB], kv_att[:, :, 3 * SLAB:]], axis=-1), 1, 2)
    rows = lambda j: pl.BlockSpec((1, SLAB, T), lambda b, i: (b, j, 0))

    def dense(windowed, branch, name):
        kblock = min(WINDOW + tq, T) if windowed else min(SELECTED_KEY_BLOCK, T)
        return pl.pallas_call(
            functools.partial(_nsa_dense_kernel, windowed=windowed, kblock=kblock),
            grid=(n, T // tq),
            in_specs=[qspec, lanes(2 * branch), rows(branch), sel_spec, gate_spec],
            out_specs=qspec,
            out_shape=jax.ShapeDtypeStruct((n, T, D), jnp.bfloat16),
            scratch_shapes=[qs_scratch, stat, acc],
            compiler_params=params,
            name=name,
        )(q_rot, kv_att, v_t, sel, gates)

    return o_cmp, dense(False, 0, "nsa_selected"), dense(True, 1, "nsa_window")


def _rope_tables(pos):
    half = ROT_DIM // 2
    inv = ROPE_THETA ** (-2.0 * jnp.arange(half, dtype=jnp.float32) / ROT_DIM)
    ang = pos.astype(jnp.float32)[:, None] * inv[None, :]
    rest = NSA_HEAD_DIM - ROT_DIM
    cos = jnp.concatenate([jnp.cos(ang), jnp.cos(ang), jnp.ones((pos.shape[0], rest), jnp.float32)], axis=1)
    sin = jnp.concatenate([-jnp.sin(ang), jnp.sin(ang), jnp.zeros((pos.shape[0], rest), jnp.float32)], axis=1)
    return jnp.tile(cos, (1, NSA_KV_GROUPS)), jnp.tile(sin, (1, NSA_KV_GROUPS))


def _nsa_pre_kernel(x_ref, gkv_ref, gmix_ref, cos_ref, sin_ref, wkv_ref, wq_ref, wg_ref,
                    cmp_ref, slc_ref, win_ref, kvb_ref, q_ref, qr_ref, gate_ref):
    xh = _rms(x_ref[0])
    cos, sin = cos_ref[...], sin_ref[...]
    low = lax.broadcasted_iota(jnp.int32, cos.shape, 1) % NSA_HEAD_DIM < ROT_DIM // 2

    def rope(t):
        swapped = jnp.where(low, pltpu.roll(t, SLAB - ROT_DIM // 2, 1), pltpu.roll(t, ROT_DIM // 2, 1))
        return t * cos + swapped * sin

    kv = _bdot(xh * gkv_ref[...], wkv_ref[...])
    part = lambda j: kv[:, j * SLAB:(j + 1) * SLAB]
    k_slc, k_win = rope(part(2)), rope(part(4))
    cmp_ref[0] = kv[:, :2 * SLAB]
    slc_ref[0] = jnp.concatenate([k_slc, part(3)], axis=-1)
    win_ref[0] = jnp.concatenate([k_win, part(5)], axis=-1)
    kvb_ref[0] = jnp.concatenate([k_slc, part(3), k_win, part(5)], axis=-1).astype(jnp.bfloat16)
    h = xh * gmix_ref[...]
    q = _bdot(h, wq_ref[...])
    q_ref[0] = q.astype(jnp.bfloat16)
    qr_ref[0] = jnp.concatenate([rope(q[:, i * SLAB:(i + 1) * SLAB]) for i in range(NSA_SLABS)],
                                axis=-1).astype(jnp.bfloat16)
    gate_ref[0] = jax.nn.sigmoid(_bdot(h, wg_ref[...]))


def nsa_pre(x, pos, norm_kv, norm_mix, w_kv, w_in):
    n, T, D = x.shape
    tm = min(ROW_TILE, T)
    nq = NSA_HEADS * NSA_HEAD_DIM
    cos, sin = _rope_tables(pos)
    w_q = (_to_slabs(w_in[:, :nq]) * NSA_HEAD_DIM ** -0.5).astype(jnp.bfloat16)
    w_g = jnp.pad(w_in[:, nq:], ((0, 0), (0, SLAB - (w_in.shape[1] - nq)))).astype(jnp.bfloat16)
    seq = lambda width: pl.BlockSpec((1, tm, width), lambda b, t: (b, t, 0))
    full = lambda a: pl.BlockSpec(a.shape, lambda b, t: (0,) * a.ndim)
    table = pl.BlockSpec((tm, SLAB), lambda b, t: (t, 0))
    consts = [norm_kv.reshape(1, D), norm_mix.reshape(1, D)]
    weights = [w_kv.astype(jnp.bfloat16), w_q, w_g]
    widths = [2 * SLAB, 2 * SLAB, 2 * SLAB, 4 * SLAB, nq, nq, SLAB]
    dtypes = [jnp.float32] * 3 + [jnp.bfloat16] * 3 + [jnp.float32]
    return pl.pallas_call(
        _nsa_pre_kernel,
        grid=(n, T // tm),
        in_specs=[seq(D)] + [full(a) for a in consts] + [table, table] + [full(a) for a in weights],
        out_specs=[seq(w) for w in widths],
        out_shape=[jax.ShapeDtypeStruct((n, T, w), dt) for w, dt in zip(widths, dtypes)],
        compiler_params=pltpu.CompilerParams(dimension_semantics=("parallel", "parallel"),
                                             vmem_limit_bytes=48 * 1024 * 1024),
        name="nsa_pre",
    )(x, *consts, cos, sin, *weights)


def _to_slabs(x):
    lead = x.shape[:-1]
    x = x.reshape(lead + (NSA_KV_GROUPS, NSA_GROUP_HEADS, NSA_HEAD_DIM))
    return jnp.swapaxes(x, -3, -2).reshape(lead + (NSA_HEADS * NSA_HEAD_DIM,))


def nsa_layer_prompt(x, norm_kv, norm_mix, w_kv, cmp_pos, cmp_w1, cmp_b1, cmp_w2, w_in, w_o, norm_next, w_router):
    n, T, D = x.shape
    cmp_kv, slc_kv, win_kv, kv_att, q, q_rot, gates = nsa_pre(x, jnp.arange(T), norm_kv, norm_mix, w_kv, w_in)
    hidden = rows_chunk_hidden(cmp_kv, _chunk_weights(cmp_w1))
    kc, vc = compress_from_hidden(hidden, cmp_pos, cmp_w1, cmp_b1, cmp_w2)
    lanes = lambda a: a.reshape(n, a.shape[1], SLAB).astype(jnp.bfloat16)
    branches = nsa_prompt_attention(q, q_rot, lanes(kc), lanes(vc), kv_att, gates)
    flat = lambda a: a.reshape(n * T, a.shape[-1])
    x_out, h_out, logits = out_proj([flat(o) for o in branches], None, flat(x), _to_slabs(w_o.T).T, norm_next,
                                    w_router)
    return x_out, h_out, logits, cmp_kv, slc_kv, win_kv


def sample_kv_context(cmp_new, slc_new, win_new, cache_cmp_kv, cache_slc_kv, cache_win_kv, page_table,
                      cmp_pos, cmp_w1, cmp_b1, cmp_w2):
    n, S = cmp_new.shape[:2]
    n_new_blk = -(-S // SLC_BLOCK)
    w_chunk = _chunk_weights(cmp_w1)
    n_new = n_new_blk * SLC_BLOCK // CMP_STRIDE
    tail = -(-n_new // 8) * 8
    new_rows = jnp.pad(cmp_new.astype(cache_cmp_kv.dtype), ((0, 0), (0, tail * CMP_STRIDE - S)) + ((0, 0),) * 3)
    hidden = paged_chunk_hidden(cache_cmp_kv, page_table, _token_weights(cmp_w1), extra_rows=tail)
    n_past = hidden.shape[1] - tail
    hidden = pl.pallas_call(
        _tail_chunk_kernel,
        grid=(n,),
        in_specs=[pl.BlockSpec((1, tail, CHUNK_LANES), lambda b: (b, 0, 0)),
                  pl.BlockSpec(w_chunk.shape, lambda b: (0, 0)), pl.BlockSpec(memory_space=pl.ANY)],
        out_specs=pl.BlockSpec((1, tail, HIDDEN_LANES), lambda b: (b, n_past // tail, 0)),
        out_shape=jax.ShapeDtypeStruct(hidden.shape, hidden.dtype),
        input_output_aliases={2: 0},
        name="tail_chunk_hidden",
    )(new_rows.reshape(n, tail, CHUNK_LANES), w_chunk, hidden)
    kc, vc = compress_from_hidden(hidden, cmp_pos, cmp_w1, cmp_b1, cmp_w2, n_blocks=n_past + n_new)
    win_all = jnp.concatenate([cache_win_kv, win_new.astype(cache_win_kv.dtype)], axis=1)
    return kc, vc, win_all


def sample_nsa(h, kc, vc, slc_new, win_new, cache_slc_kv, cache_win_kv, page_table, w_in, w_o):
    n, S, _ = h.shape
    assert S == 1
    q, gate = nsa_query(h, w_in, precise=True)
    t_pos = PAST_LEN + jnp.arange(S)
    with jax.default_matmul_precision("highest"):
        o_cmp, idx = compressed_branch_and_selection(q, t_pos, kc, vc)
    q_rot = (rope_partial(q, t_pos) * NSA_HEAD_DIM ** -0.5).reshape(n, NSA_HEADS, NSA_HEAD_DIM)
    o_slc, o_win = decode_attention(q_rot, idx[:, :, 0], page_table, cache_slc_kv, cache_win_kv, slc_new, win_new)
    g = jax.nn.sigmoid(gate.astype(jnp.float32)).astype(q.dtype)
    o = g[..., 0:1] * o_cmp + g[..., 1:2] * o_slc.reshape(q.shape) + g[..., 2:3] * o_win.reshape(q.shape)
    return dense(o.reshape(n, S, NSA_HEADS * NSA_HEAD_DIM), w_o, precise=True)


def kernel(x_prompt, x_sample, state_wkv, state_shift, cache_cmp_kv, cache_slc_kv, cache_win_kv, page_table, norm_mix, norm_ffn, norm_kv, norm_final, rw_mu, rw_w_rkv, rw_w0, rw_w1, rw_w2, rw_a0, rw_a1, rw_a2, rw_g1, rw_g2, rw_k_k, rw_k_a, rw_r_k, rw_gn_w, rw_gn_b, rw_w_o, nsa_w_kv, nsa_cmp_pos, nsa_cmp_w1, nsa_cmp_b1, nsa_cmp_w2, nsa_w_in, nsa_w_o, ffn_w_gu, ffn_w_down, moe_router, moe_w_gu, moe_w_down):
    cmp_params = (nsa_cmp_pos, nsa_cmp_w1, nsa_cmp_b1, nsa_cmp_w2)
    assert DEPTH == 2 and N_A_LAYERS == 1
    D = D_MODEL
    moe_gu, moe_down = moe_w_gu[0].astype(jnp.bfloat16), moe_w_down[0].astype(jnp.bfloat16)
    x_prompt, moe_gu, moe_down = lax.optimization_barrier((x_prompt, moe_gu, moe_down))

    n_p, T = x_prompt.shape[:2]
    x1, h1, wkv_fin, h_last = rwkv7_layer(
        x_prompt, jnp.zeros((n_p, D), x_prompt.dtype), jnp.zeros((n_p, RWKV_HEADS, RWKV_HEAD, RWKV_HEAD), jnp.float32),
        norm_mix[0], rw_mu[0], rw_w_rkv[0], rw_w0[0], rw_w1[0], rw_w2[0], rw_a0[0], rw_a1[0], rw_a2[0],
        rw_g1[0], rw_g2[0], rw_k_k[0], rw_k_a[0], rw_r_k[0], rw_gn_w[0], rw_gn_b[0], rw_w_o[0], norm_ffn[0])
    wkv_p, shift_p = wkv_fin[None], h_last[None]
    x2 = swiglu_residual(x1, h1, ffn_w_gu[0], ffn_w_down[0])
    x_p, h_p, logits_p, cmp_rows, slc_rows, win_rows = nsa_layer_prompt(
        x2.reshape(n_p, T, D), norm_kv, norm_mix[1], nsa_w_kv, *cmp_params, nsa_w_in[0], nsa_w_o[0], norm_ffn[1],
        moe_router[0])
    kv_shape = (n_p, T, 2, NSA_KV_GROUPS, NSA_HEAD_DIM)
    cmp_kv_p, slc_kv_p = cmp_rows.reshape(kv_shape), slc_rows.reshape(kv_shape)
    win_kv_p = win_rows.reshape(kv_shape)[:, -min(WINDOW, T):]

    pos_s = PAST_LEN + jnp.arange(x_sample.shape[1], dtype=jnp.int32)
    h = rmsnorm(x_sample, norm_mix[0])
    y, s_fin, h_last_s = rwkv7_time_mix(
        h, state_shift[0], state_wkv[0], rw_mu[0], rw_w_rkv[0], rw_w0[0], rw_w1[0], rw_w2[0],
        rw_a0[0], rw_a1[0], rw_a2[0], rw_g1[0], rw_g2[0], rw_k_k[0], rw_k_a[0],
        rw_r_k[0], rw_gn_w[0], rw_gn_b[0], rw_w_o[0], precise=True)
    wkv_s, shift_s = s_fin[None], h_last_s[None]
    x_s = x_sample + y
    x_s = x_s + swiglu(rmsnorm(x_s, norm_ffn[0]), ffn_w_gu[0], ffn_w_down[0], precise=True)
    cmp_kv_s, slc_kv_s, win_new = shared_kv_rows(x_s, norm_kv, nsa_w_kv, pos_s, precise=True)
    kc_s, vc_s, win_all = sample_kv_context(cmp_kv_s, slc_kv_s, win_new, cache_cmp_kv, cache_slc_kv, cache_win_kv,
                                            page_table, *cmp_params)
    x_s = x_s + sample_nsa(rmsnorm(x_s, norm_mix[1]), kc_s, vc_s, slc_kv_s, win_new, cache_slc_kv, cache_win_kv,
                           page_table, nsa_w_in[0], nsa_w_o[0])
    win_kv_s = win_all[:, -cache_win_kv.shape[1]:]
    x_s = x_s.reshape(-1, D)
    h_s = rmsnorm(x_s, norm_ffn[1])
    logits_s = matmul_precise(h_s, moe_router[0])

    n_tok_p = n_p * T
    logits = jnp.concatenate([logits_p[:, :N_EXPERTS], logits_s], axis=0)
    y0, y1, gate = moe_swiglu(jnp.concatenate([h_p, h_s.astype(jnp.bfloat16)], axis=0), logits,
                              moe_gu, moe_down)
    gate = jnp.pad(gate, ((0, 0), (0, 128 - TOP_K)))
    y_prompt = combine_norm(x_p, y0[:n_tok_p], y1[:n_tok_p], gate[:n_tok_p], norm_final).reshape(x_prompt.shape)
    y_sample = combine_norm(x_s, y0[n_tok_p:], y1[n_tok_p:], gate[n_tok_p:], norm_final).reshape(x_sample.shape)


    return (y_prompt, y_sample, wkv_p, shift_p, cmp_kv_p, slc_kv_p, win_kv_p,
            wkv_s, shift_s, cmp_kv_s, slc_kv_s, win_kv_s)
```
